```python
import math
import jax
import jax.numpy as jnp
from jax import lax
import numpy as np

D_MODEL = 1024
BATCH = 2
SEQ = 8192
DEPTH = 2

D_MIX = D_MODEL
RMS_EPS = 1e-6

GLA_HEADS = 4
GLA_WIDTH = D_MIX // 2
GLA_DV = GLA_WIDTH // GLA_HEADS
GLA_DK = GLA_DV // 2
GLA_KDIM = GLA_HEADS * GLA_DK
GLA_GATE_RANK = 16
GLA_GATE_NORM = 16.0
GLA_CHUNK = 64

GMLP_HEADS = 4
GMLP_WIDTH = D_MIX // 4
GMLP_DH = GMLP_WIDTH // GMLP_HEADS
GMLP_CHUNK = 128

CONV_GROUPS = 4
CONV_WIDTH = D_MIX // 4
CONV_K = 3

PROJ_SIZES = (GLA_KDIM, GLA_KDIM, GLA_WIDTH, GLA_WIDTH, GLA_GATE_RANK,
              GMLP_WIDTH, GMLP_WIDTH,
              CONV_WIDTH, CONV_WIDTH, CONV_WIDTH)
D_IN = sum(PROJ_SIZES)

N_GROUPS = 4
EXPERTS_PER_GROUP = 8
N_EXPERTS = N_GROUPS * EXPERTS_PER_GROUP
TOP_K = 2
D_EXPERT = D_MODEL // 4
MOE_BLOCK = 128

kernel_name = "hymba_gla_gmlp_shortconv_hiermoe"


def rms_norm(x, gain):
    x32 = x.astype(jnp.float32)
    y = x32 * lax.rsqrt(jnp.mean(x32 * x32, axis=-1, keepdims=True) + RMS_EPS)
    return (y * gain.astype(jnp.float32)).astype(x.dtype)


def gla_mixer(q, k, v, g_out, gk_low, w_gk_up, b_gk, gla_norm):
    B, S, _ = q.shape
    N = S // GLA_CHUNK
    f32 = jnp.float32
    gk = jax.nn.log_sigmoid((gk_low @ w_gk_up + b_gk).astype(f32)) / GLA_GATE_NORM

    def to_chunks(t, d):
        return t.reshape(B, N, GLA_CHUNK, GLA_HEADS, d).transpose(0, 3, 1, 2, 4).astype(f32)

    qc = to_chunks(q, GLA_DK) * (GLA_DK ** -0.5)
    kc = to_chunks(k, GLA_DK)
    vc = to_chunks(v, GLA_DV)
    gc = to_chunks(gk, GLA_DK)
    b = jnp.cumsum(gc, axis=3)
    b_last = b[:, :, :, -1:, :]
    q_dec = qc * jnp.exp(b)
    k_inv = kc * jnp.exp(-b)
    k_dec = kc * jnp.exp(b_last - b)
    mask = jnp.tril(jnp.ones((GLA_CHUNK, GLA_CHUNK), dtype=bool))
    scores = jnp.where(mask, jnp.einsum('bhncd,bhnsd->bhncs', q_dec, k_inv), 0.0)
    o_intra = jnp.einsum('bhncs,bhnsv->bhncv', scores, vc)
    kv = jnp.einsum('bhncd,bhncv->nbhdv', k_dec, vc)
    decay = jnp.exp(b_last[:, :, :, 0, :]).transpose(2, 0, 1, 3)

    def step(state, inp):
        kv_n, d_n = inp
        return d_n[..., None] * state + kv_n, state

    s0 = jnp.zeros((B, GLA_HEADS, GLA_DK, GLA_DV), f32)
    _, s_prev = lax.scan(step, s0, (kv, decay))
    o_inter = jnp.einsum('bhncd,nbhdv->bhncv', q_dec, s_prev)
    o = (o_intra + o_inter).transpose(0, 2, 3, 1, 4).reshape(B, S, GLA_HEADS, GLA_DV)
    o = o * lax.rsqrt(jnp.mean(o * o, axis=-1, keepdims=True) + RMS_EPS) * gla_norm.astype(f32)
    gate = jax.nn.silu(g_out.astype(f32)).reshape(B, S, GLA_HEADS, GLA_DV)
    return (o * gate).reshape(B, S, GLA_WIDTH).astype(q.dtype)


def gmlp_mixer(u_raw, v_raw, gmlp_norm, w_spatial, b_spatial):
    B, S, _ = u_raw.shape
    N = S // GMLP_CHUNK
    shp = (B, N, GMLP_CHUNK, GMLP_HEADS, GMLP_DH)
    u = jax.nn.gelu(u_raw).reshape(shp)
    v32 = jax.nn.gelu(v_raw).reshape(shp).astype(jnp.float32)
    v32 = v32 * lax.rsqrt(jnp.mean(v32 * v32, axis=-1, keepdims=True) + RMS_EPS)
    v32 = v32 * gmlp_norm.reshape(GMLP_HEADS, GMLP_DH).astype(jnp.float32)
    mask = jnp.tril(jnp.ones((GMLP_CHUNK, GMLP_CHUNK), dtype=bool))
    w = jnp.where(mask[None], w_spatial, 0.0).astype(jnp.float32)
    mixed = jnp.einsum('hts,bnshd->bnthd', w, v32) + b_spatial.T.astype(jnp.float32)[:, :, None]
    return (u * mixed.astype(u.dtype)).reshape(B, S, GMLP_WIDTH)


def conv_mixer(x_in, b_gate, c_gate, w_conv):
    h = c_gate * x_in
    y = lax.conv_general_dilated(
        h, w_conv[:, None, :].astype(h.dtype), window_strides=(1,),
        padding=((CONV_K - 1, 0),), dimension_numbers=('NWC', 'WIO', 'NWC'),
        feature_group_count=CONV_WIDTH)
    return b_gate * y


def hier_moe(x2d, w_rg, b_rg, w_re, b_re, w_gate, w_up, w_down):
    T, D = x2d.shape
    f32 = jnp.float32
    g_logits = (x2d @ w_rg).astype(f32) + b_rg.astype(f32)
    g_prob = jax.nn.softmax(g_logits, axis=-1)
    g_top = jnp.argmax(g_logits, axis=-1)
    g_w = jnp.take_along_axis(g_prob, g_top[:, None], axis=-1)[:, 0]
    e_logits = ((x2d @ w_re).astype(f32) + b_re.astype(f32)).reshape(T, N_GROUPS, EXPERTS_PER_GROUP)
    e_in = jnp.take_along_axis(e_logits, g_top[:, None, None], axis=1)[:, 0]
    e_prob = jax.nn.softmax(e_in, axis=-1)
    top_p, top_i = lax.top_k(e_prob, TOP_K)
    weights = top_p / jnp.sum(top_p, axis=-1, keepdims=True) * g_w[:, None]
    expert_ids = g_top[:, None] * EXPERTS_PER_GROUP + top_i

    A = T * TOP_K
    e_flat = expert_ids.reshape(-1).astype(jnp.int32)
    t_flat = jnp.repeat(jnp.arange(T, dtype=jnp.int32), TOP_K)
    w_flat = weights.reshape(-1)
    order = jnp.argsort(e_flat)
    e_s, t_s, w_s = e_flat[order], t_flat[order], w_flat[order]
    counts = jnp.bincount(e_flat, length=N_EXPERTS)
    starts = jnp.cumsum(counts) - counts
    padded = (counts + MOE_BLOCK - 1) // MOE_BLOCK * MOE_BLOCK
    pend = jnp.cumsum(padded)
    pstarts = pend - padded
    dest = pstarts[e_s] + (jnp.arange(A, dtype=jnp.int32) - starts[e_s])
    P = ((A + MOE_BLOCK - 1) // MOE_BLOCK) * MOE_BLOCK + N_EXPERTS * MOE_BLOCK
    NB = P // MOE_BLOCK
    slot_tok = jnp.full((P,), T, dtype=jnp.int32).at[dest].set(t_s)
    slot_w = jnp.zeros((P,), f32).at[dest].set(w_s)
    block_start = jnp.arange(NB, dtype=jnp.int32) * MOE_BLOCK
    block_exp = jnp.minimum(jnp.sum(block_start[:, None] >= pend[None, :], axis=1), N_EXPERTS - 1)
    x_pad = jnp.concatenate([x2d, jnp.zeros((1, D), x2d.dtype)], axis=0)
    xb = x_pad[slot_tok].reshape(NB, MOE_BLOCK, D)

    def expert_block(args):
        xblk, e = args
        h = jax.nn.silu(xblk @ w_gate[e]) * (xblk @ w_up[e])
        return h @ w_down[e]

    yb = lax.map(expert_block, (xb, block_exp)).reshape(P, D)
    y = jax.ops.segment_sum(yb * slot_w[:, None].astype(yb.dtype), slot_tok, num_segments=T + 1)
    return y[:T]


def setup_inputs(seed: int = 0) -> dict:
    key = jax.random.key(seed)
    ks = jax.random.split(key, 24)
    L, D = DEPTH, D_MODEL

    def nrm(k, shape, scale):
        return jax.random.normal(k, shape, dtype=jnp.float32) * scale

    return {
        'x': nrm(ks[0], (BATCH, SEQ, D), 1.0),
        'attn_norm': 1.0 + nrm(ks[1], (L, D), 0.02),
        'w_in': nrm(ks[2], (L, D, D_IN), D ** -0.5),
        'w_gk_up': nrm(ks[3], (L, GLA_GATE_RANK, GLA_KDIM), GLA_GATE_RANK ** -0.5),
        'b_gk': nrm(ks[4], (L, GLA_KDIM), 0.1),
        'gla_norm': 1.0 + nrm(ks[5], (L, GLA_DV), 0.02),
        'gmlp_norm': 1.0 + nrm(ks[6], (L, GMLP_WIDTH), 0.02),
        'w_spatial': nrm(ks[7], (L, GMLP_HEADS, GMLP_CHUNK, GMLP_CHUNK), GMLP_CHUNK ** -0.5),
        'b_spatial': 1.0 + nrm(ks[8], (L, GMLP_HEADS, GMLP_CHUNK), 0.02),
        'w_conv': nrm(ks[9], (L, CONV_K, CONV_WIDTH), CONV_K ** -0.5),
        'w_out': nrm(ks[10], (L, D_MIX, D), D_MIX ** -0.5),
        'ffn_norm': 1.0 + nrm(ks[11], (L, D), 0.02),
        'w_router_group': nrm(ks[12], (L, D, N_GROUPS), D ** -0.5),
        'b_router_group': nrm(ks[13], (L, N_GROUPS), 0.01),
        'w_router_expert': nrm(ks[14], (L, D, N_EXPERTS), D ** -0.5),
        'b_router_expert': nrm(ks[15], (L, N_EXPERTS), 0.01),
        'w_gate': nrm(ks[16], (L, N_EXPERTS, D, D_EXPERT), D ** -0.5),
        'w_up': nrm(ks[17], (L, N_EXPERTS, D, D_EXPERT), D ** -0.5),
        'w_down': nrm(ks[18], (L, N_EXPERTS, D_EXPERT, D), D_EXPERT ** -0.5),
        'final_norm': 1.0 + nrm(ks[19], (D,), 0.02),
    }


def reference(x, attn_norm, w_in, w_gk_up, b_gk, gla_norm, gmlp_norm, w_spatial, b_spatial,
              w_conv, w_out, ffn_norm, w_router_group, b_router_group, w_router_expert,
              b_router_expert, w_gate, w_up, w_down, final_norm):
    B, S, D = x.shape
    split_points = []
    acc = 0
    for s in PROJ_SIZES[:-1]:
        acc += s
        split_points.append(acc)
    for l in range(DEPTH):
        h = rms_norm(x, attn_norm[l])
        proj = h @ w_in[l]
        q, k, v, g_out, gk_low, u, vg, x_in, b_gate, c_gate = jnp.split(proj, split_points, axis=-1)
        o_gla = gla_mixer(q, k, v, g_out, gk_low, w_gk_up[l], b_gk[l], gla_norm[l])
        o_gmlp = gmlp_mixer(u, vg, gmlp_norm[l], w_spatial[l], b_spatial[l])
        o_conv = conv_mixer(x_in, b_gate, c_gate, w_conv[l])
        mixed = jnp.concatenate([o_gla, o_gmlp, o_conv], axis=-1)
        x = x + mixed @ w_out[l]
        h = rms_norm(x, ffn_norm[l]).reshape(B * S, D)
        y = hier_moe(h, w_router_group[l], b_router_group[l], w_router_expert[l], b_router_expert[l],
                     w_gate[l], w_up[l], w_down[l])
        x = x + y.reshape(B, S, D)
    return rms_norm(x, final_norm)
```

```python
import functools

import jax
import jax.numpy as jnp
from jax import lax
from jax.experimental import pallas as pl
from jax.experimental.pallas import tpu as pltpu

F32 = jnp.float32
BF16 = jnp.bfloat16

D_MODEL = 1024
RMS_EPS = 1e-6
GLA_HEADS = 4
GLA_WIDTH = 512
GLA_DV = 128
GLA_DK = 64
GLA_KDIM = 256
GLA_GATE_RANK = 16
GLA_GATE_NORM = 16.0
GLA_CHUNK = 64
GMLP_HEADS = 4
GMLP_WIDTH = 256
GMLP_DH = 64
GMLP_CHUNK = 128
CONV_WIDTH = 256
CONV_K = 3
N_GROUPS = 4
EXPERTS_PER_GROUP = 8
N_EXPERTS = 32
TOP_K = 2
D_EXPERT = 256

LANES = 128
C_Q, C_K, C_V, C_G = 0, 256, 512, 1024
C_U, C_VG, C_X, C_BG, C_CG, C_GKL = 1536, 1792, 2048, 2304, 2560, 2816
D_PROJ = C_GKL + LANES

TM_PROJ = 256
TS_MIX = 256
TM_OUT = 256
MOE_BLK = 256
ROUTER_COLS = LANES
VMEM_LIMIT = 56 * 1024 * 1024


def _dot(a, b):
    return jnp.dot(a, b, preferred_element_type=F32)


def _split_bf16(x):
    hi = x.astype(BF16)
    lo = (x - hi.astype(F32)).astype(BF16)
    return hi, lo


def _norm_proj_kernel(*refs, n_add):
    x_refs = refs[:n_add]
    gain_ref, w_ref, xo_ref, proj_ref = refs[n_add:]
    x = x_refs[0][...]
    for r in x_refs[1:]:
        x = x + r[...]
    if n_add > 1:
        xo_ref[...] = x
    h = x * lax.rsqrt(jnp.mean(x * x, axis=-1, keepdims=True) + RMS_EPS) * gain_ref[...]
    proj_ref[...] = _dot(h.astype(BF16), w_ref[...])


def _norm_proj(xs, gain, w_p):
    n_add = len(xs)
    T = xs[0].shape[0]
    row = pl.BlockSpec((TM_PROJ, D_MODEL), lambda i: (i, 0))
    outs = pl.pallas_call(
        functools.partial(_norm_proj_kernel, n_add=n_add),
        grid=(T // TM_PROJ,),
        in_specs=[row] * n_add + [
            pl.BlockSpec((1, D_MODEL), lambda i: (0, 0)),
            pl.BlockSpec((D_MODEL, D_PROJ), lambda i: (0, 0)),
        ],
        out_specs=[row, pl.BlockSpec((TM_PROJ, D_PROJ), lambda i: (i, 0))],
        out_shape=[jax.ShapeDtypeStruct((T, D_MODEL), F32),
                   jax.ShapeDtypeStruct((T, D_PROJ), F32)],
        compiler_params=pltpu.CompilerParams(
            dimension_semantics=("arbitrary",), vmem_limit_bytes=VMEM_LIMIT),
        name="norm_proj",
    )(*xs, gain, w_p)
    x = xs[0] if n_add == 1 else outs[0]
    return x, outs[1]


def _gelu_tanh(x):
    c = 0.7978845608028654
    return x * (0.5 * (1.0 + jnp.tanh(c * (x + 0.044715 * (x * x * x)))))


def _mixer_kernel(proj_ref, wgk_ref, bgk_ref, glan_ref, gmn_ref, wsp_ref, bsp_ref, wconv_ref,
                  out_ref, st_ref, hc_ref, lcat_ref, wm_ref):
    TS = TS_MIX
    n_gla = TS // GLA_CHUNK
    n_gm = TS // GMLP_CHUNK

    @pl.when(pl.program_id(1) == 0)
    def _():
        st_ref[...] = jnp.zeros_like(st_ref)
        hc_ref[...] = jnp.zeros_like(hc_ref)
        r = lax.broadcasted_iota(jnp.int32, (2 * TS, TS), 0)
        c = lax.broadcasted_iota(jnp.int32, (2 * TS, TS), 1)
        rr = jnp.where(r >= TS, r - TS, r)
        same = (rr // GLA_CHUNK) == (c // GLA_CHUNK)
        keep = same & ((r >= TS) | (c <= rr))
        lcat_ref[...] = jnp.where(keep, 1.0, 0.0).astype(BF16)
        t = lax.broadcasted_iota(jnp.int32, (GMLP_CHUNK, GMLP_HEADS * GMLP_CHUNK), 0)
        s = lax.broadcasted_iota(jnp.int32, (GMLP_CHUNK, GMLP_HEADS * GMLP_CHUNK), 1) % GMLP_CHUNK
        wm_ref[...] = jnp.where(s <= t, wsp_ref[...], 0.0).astype(BF16)

    lane256 = lax.broadcasted_iota(jnp.int32, (1, GLA_KDIM), 1)

    q = proj_ref[:, C_Q:C_Q + GLA_KDIM]
    k = proj_ref[:, C_K:C_K + GLA_KDIM]
    v = proj_ref[:, C_V:C_V + GLA_WIDTH]
    z = _dot(proj_ref[:, C_GKL:C_GKL + LANES].astype(BF16), wgk_ref[...]) + bgk_ref[...]
    gk = (jnp.minimum(z, 0.0) - jnp.log1p(jnp.exp(-jnp.abs(z)))) * (1.0 / GLA_GATE_NORM)
    gk_hi, gk_lo = _split_bf16(gk)
    cs = _dot(lcat_ref[...], jnp.concatenate([gk_hi, gk_lo], axis=1))
    b = cs[:TS, :GLA_KDIM] + cs[:TS, GLA_KDIM:]
    bl = cs[TS:, :GLA_KDIM] + cs[TS:, GLA_KDIM:]
    q_dec = (q * (GLA_DK ** -0.5)) * jnp.exp(b)
    k_inv = (k * jnp.exp(-b)).astype(BF16)
    k_dec = (k * jnp.exp(bl - b)).astype(BF16)
    q_dec_b = q_dec.astype(BF16)
    v_b = v.astype(BF16)

    zero_b = jnp.zeros_like(q_dec_b)
    q_stack = jnp.concatenate(
        [jnp.where((lane256 // GLA_DK) == h, q_dec_b, zero_b) for h in range(GLA_HEADS)], axis=0)
    scores = lax.dot_general(q_stack, k_inv, (((1,), (1,)), ((), ())),
                             preferred_element_type=F32)
    rt = lax.broadcasted_iota(jnp.int32, (TS, TS), 0)
    ct = lax.broadcasted_iota(jnp.int32, (TS, TS), 1)
    causal = ((rt // GLA_CHUNK) == (ct // GLA_CHUNK)) & (ct <= rt)
    o_heads = []
    for h in range(GLA_HEADS):
        p_h = jnp.where(causal, scores[h * TS:(h + 1) * TS, :], 0.0).astype(BF16)
        o_heads.append(_dot(p_h, v_b[:, h * GLA_DV:(h + 1) * GLA_DV]))

    sr = lax.broadcasted_iota(jnp.int32, (GLA_WIDTH, GLA_KDIM), 0) // GLA_DV
    sc = lax.broadcasted_iota(jnp.int32, (GLA_WIDTH, GLA_KDIM), 1) // GLA_DK
    bd_mask = sr == sc
    o_inter = []
    for c in range(n_gla):
        rows = slice(c * GLA_CHUNK, (c + 1) * GLA_CHUNK)
        st = st_ref[...]
        o_inter.append(lax.dot_general(q_dec_b[rows], st.astype(BF16), (((1,), (1,)), ((), ())),
                                       preferred_element_type=F32))
        upd = lax.dot_general(v_b[rows], k_dec[rows], (((0,), (0,)), ((), ())),
                              preferred_element_type=F32)
        decay = jnp.exp(bl[c * GLA_CHUNK:c * GLA_CHUNK + 1, :])
        st_ref[...] = st * decay + jnp.where(bd_mask, upd, 0.0)
    o_inter = jnp.concatenate(o_inter, axis=0)

    for h in range(GLA_HEADS):
        cols = slice(h * GLA_DV, (h + 1) * GLA_DV)
        o = o_heads[h] + o_inter[:, cols]
        o = o * lax.rsqrt(jnp.mean(o * o, axis=-1, keepdims=True) + RMS_EPS) * glan_ref[...]
        g = proj_ref[:, C_G + h * GLA_DV:C_G + (h + 1) * GLA_DV]
        out_ref[:, cols] = (o * (g * (1.0 / (1.0 + jnp.exp(-g))))).astype(out_ref.dtype)

    u = _gelu_tanh(proj_ref[:, C_U:C_U + GMLP_WIDTH])
    vg = _gelu_tanh(proj_ref[:, C_VG:C_VG + GMLP_WIDTH])
    hr = lax.broadcasted_iota(jnp.int32, (GMLP_WIDTH, GMLP_WIDTH), 0) // GMLP_DH
    hcn = lax.broadcasted_iota(jnp.int32, (GMLP_WIDTH, GMLP_WIDTH), 1) // GMLP_DH
    head_mean = jnp.where(hr == hcn, 1.0 / GMLP_DH, 0.0).astype(BF16)
    sq_hi, sq_lo = _split_bf16(vg * vg)
    ms = _dot(sq_hi, head_mean) + _dot(sq_lo, head_mean)
    v32 = vg * lax.rsqrt(ms + RMS_EPS) * gmn_ref[...]
    for c in range(n_gm):
        rows = slice(c * GMLP_CHUNK, (c + 1) * GMLP_CHUNK)
        vc = v32[rows].astype(BF16)
        zc = jnp.zeros_like(vc)
        rhs = jnp.concatenate(
            [jnp.where((lane256 // GMLP_DH) == h, vc, zc) for h in range(GMLP_HEADS)], axis=0)
        mixed = _dot(wm_ref[...], rhs) + bsp_ref[...]
        out_ref[rows, GLA_WIDTH:GLA_WIDTH + GMLP_WIDTH] = (u[rows] * mixed).astype(out_ref.dtype)

    hcv = proj_ref[:, C_CG:C_CG + CONV_WIDTH] * proj_ref[:, C_X:C_X + CONV_WIDTH]
    hc_ref[8:8 + TS, :] = hcv
    y = (wconv_ref[2:3, :] * hcv + wconv_ref[1:2, :] * hc_ref[7:7 + TS, :]
         + wconv_ref[0:1, :] * hc_ref[6:6 + TS, :])
    out_ref[:, GLA_WIDTH + GMLP_WIDTH:] = (
        proj_ref[:, C_BG:C_BG + CONV_WIDTH] * y).astype(out_ref.dtype)
    hc_ref[0:8, :] = hc_ref[TS:TS + 8, :]


def _mixers(proj, wgk, bgk, glan, gmn, wsp, bsp, wconv, batch, seq):
    n_seq = seq // TS_MIX
    full = lambda shape: pl.BlockSpec(shape, lambda b, i: (0,) * len(shape))
    return pl.pallas_call(
        _mixer_kernel,
        grid=(batch, n_seq),
        in_specs=[
            pl.BlockSpec((TS_MIX, D_PROJ), lambda b, i: (b * n_seq + i, 0)),
            full((LANES, GLA_KDIM)), full((1, GLA_KDIM)), full((1, GLA_DV)), full((1, GMLP_WIDTH)),
            full((GMLP_CHUNK, GMLP_HEADS * GMLP_CHUNK)), full((GMLP_CHUNK, GMLP_WIDTH)),
            full((8, CONV_WIDTH)),
        ],
        out_specs=pl.BlockSpec((TS_MIX, D_MODEL), lambda b, i: (b * n_seq + i, 0)),
        out_shape=jax.ShapeDtypeStruct((batch * seq, D_MODEL), BF16),
        scratch_shapes=[
            pltpu.VMEM((GLA_WIDTH, GLA_KDIM), F32),
            pltpu.VMEM((TS_MIX + 8, CONV_WIDTH), F32),
            pltpu.VMEM((2 * TS_MIX, TS_MIX), BF16),
            pltpu.VMEM((GMLP_CHUNK, GMLP_HEADS * GMLP_CHUNK), BF16),
        ],
        compiler_params=pltpu.CompilerParams(
            dimension_semantics=("arbitrary", "arbitrary"), vmem_limit_bytes=VMEM_LIMIT),
        name="mixers",
    )(proj, wgk, bgk, glan, gmn, wsp, bsp, wconv)


def _out_router_kernel(mix_ref, x_ref, wo_ref, gain_ref, wrh_ref, wrl_ref, br_ref,
                       x2_ref, h2_ref, lg_ref):
    x2 = x_ref[...] + _dot(mix_ref[...], wo_ref[...])
    x2_ref[...] = x2
    h = x2 * lax.rsqrt(jnp.mean(x2 * x2, axis=-1, keepdims=True) + RMS_EPS) * gain_ref[...]
    h_hi, h_lo = _split_bf16(h)
    h2_ref[...] = h_hi
    lg_ref[...] = (_dot(h_hi, wrh_ref[...]) + _dot(h_lo, wrh_ref[...]) + _dot(h_hi, wrl_ref[...])
                   + br_ref[...])


def _out_router(mixed, x, wo, gain, wr_hi, wr_lo, br):
    T = x.shape[0]
    row = lambda w: pl.BlockSpec((TM_OUT, w), lambda i: (i, 0))
    full = lambda shape: pl.BlockSpec(shape, lambda i: (0, 0))
    return pl.pallas_call(
        _out_router_kernel,
        grid=(T // TM_OUT,),
        in_specs=[row(D_MODEL), row(D_MODEL), full((D_MODEL, D_MODEL)), full((1, D_MODEL)),
                  full((D_MODEL, ROUTER_COLS)), full((D_MODEL, ROUTER_COLS)), full((1, ROUTER_COLS))],
        out_specs=[row(D_MODEL), row(D_MODEL), row(ROUTER_COLS)],
        out_shape=[jax.ShapeDtypeStruct((T, D_MODEL), F32),
                   jax.ShapeDtypeStruct((T, D_MODEL), BF16),
                   jax.ShapeDtypeStruct((T, ROUTER_COLS), F32)],
        compiler_params=pltpu.CompilerParams(
            dimension_semantics=("arbitrary",), vmem_limit_bytes=VMEM_LIMIT),
        name="out_router",
    )(mixed, x, wo, gain, wr_hi, wr_lo, br)


def _expert_kernel(be_ref, x_ref, sw_ref, wg_ref, wu_ref, wd_ref, o_ref):
    del be_ref
    x = x_ref[...]
    g = _dot(x, wg_ref[0])
    u = _dot(x, wu_ref[0])
    h = (g * (1.0 / (1.0 + jnp.exp(-g)))) * u
    o_ref[...] = _dot(h.astype(BF16), wd_ref[0]) * sw_ref[...]


def _experts(block_exp, xb, slot_w, wg, wu, wd):
    P = xb.shape[0]
    grid_spec = pltpu.PrefetchScalarGridSpec(
        num_scalar_prefetch=1,
        grid=(P // MOE_BLK,),
        in_specs=[
            pl.BlockSpec((MOE_BLK, D_MODEL), lambda i, be: (i, 0)),
            pl.BlockSpec((MOE_BLK, 1), lambda i, be: (i, 0)),
            pl.BlockSpec((1, D_MODEL, D_EXPERT), lambda i, be: (be[i], 0, 0)),
            pl.BlockSpec((1, D_MODEL, D_EXPERT), lambda i, be: (be[i], 0, 0)),
            pl.BlockSpec((1, D_EXPERT, D_MODEL), lambda i, be: (be[i], 0, 0)),
        ],
        out_specs=pl.BlockSpec((MOE_BLK, D_MODEL), lambda i, be: (i, 0)),
    )
    return pl.pallas_call(
        _expert_kernel,
        grid_spec=grid_spec,
        out_shape=jax.ShapeDtypeStruct((P, D_MODEL), F32),
        compiler_params=pltpu.CompilerParams(
            dimension_semantics=("arbitrary",), vmem_limit_bytes=VMEM_LIMIT),
        name="experts",
    )(block_exp, xb, slot_w, wg, wu, wd)


def _final_norm_kernel(x_ref, y_ref, gain_ref, o_ref):
    x = x_ref[...] + y_ref[...]
    o_ref[...] = x * lax.rsqrt(jnp.mean(x * x, axis=-1, keepdims=True) + RMS_EPS) * gain_ref[...]


def _final_norm(x, y, gain):
    T = x.shape[0]
    row = pl.BlockSpec((TM_OUT, D_MODEL), lambda i: (i, 0))
    return pl.pallas_call(
        _final_norm_kernel,
        grid=(T // TM_OUT,),
        in_specs=[row, row, pl.BlockSpec((1, D_MODEL), lambda i: (0, 0))],
        out_specs=row,
        out_shape=jax.ShapeDtypeStruct((T, D_MODEL), F32),
        compiler_params=pltpu.CompilerParams(dimension_semantics=("arbitrary",)),
        name="final_norm",
    )(x, y, gain)


def _route(logits, T):
    g_logits = logits[:, :N_GROUPS]
    g_prob = jax.nn.softmax(g_logits, axis=-1)
    g_top = jnp.argmax(g_logits, axis=-1)
    g_w = jnp.take_along_axis(g_prob, g_top[:, None], axis=-1)[:, 0]
    e_logits = logits[:, N_GROUPS:N_GROUPS + N_EXPERTS].reshape(T, N_GROUPS, EXPERTS_PER_GROUP)
    e_in = jnp.take_along_axis(e_logits, g_top[:, None, None], axis=1)[:, 0]
    e_prob = jax.nn.softmax(e_in, axis=-1)
    top_p, top_i = lax.top_k(e_prob, TOP_K)
    weights = top_p / jnp.sum(top_p, axis=-1, keepdims=True) * g_w[:, None]
    expert_ids = g_top[:, None] * EXPERTS_PER_GROUP + top_i

    A = T * TOP_K
    e_flat = expert_ids.reshape(-1).astype(jnp.int32)
    t_flat = jnp.repeat(jnp.arange(T, dtype=jnp.int32), TOP_K)
    w_flat = weights.reshape(-1)
    order = jnp.argsort(e_flat)
    e_s, t_s, w_s = e_flat[order], t_flat[order], w_flat[order]
    counts = jnp.bincount(e_flat, length=N_EXPERTS)
    starts = jnp.cumsum(counts) - counts
    padded = (counts + MOE_BLK - 1) // MOE_BLK * MOE_BLK
    pend = jnp.cumsum(padded)
    pstarts = pend - padded
    dest = pstarts[e_s] + (jnp.arange(A, dtype=jnp.int32) - starts[e_s])
    P = ((A + MOE_BLK - 1) // MOE_BLK) * MOE_BLK + N_EXPERTS * MOE_BLK
    NB = P // MOE_BLK
    slot_tok = jnp.full((P,), T, dtype=jnp.int32).at[dest].set(t_s)
    slot_w = jnp.zeros((P,), F32).at[dest].set(w_s)
    block_start = jnp.arange(NB, dtype=jnp.int32) * MOE_BLK
    block_exp = jnp.minimum(jnp.sum(block_start[:, None] >= pend[None, :], axis=1), N_EXPERTS - 1)
    return slot_tok, slot_w, block_exp.astype(jnp.int32)


def _prep_layer(l, w_in, w_gk_up, b_gk, gla_norm, gmlp_norm, w_spatial, b_spatial, w_conv, w_out,
                w_router_group, b_router_group, w_router_expert, b_router_expert,
                w_gate, w_up, w_down):
    wi = w_in[l]
    s_gkl = 2 * GLA_KDIM + 2 * GLA_WIDTH
    w_p = jnp.concatenate(
        [wi[:, :s_gkl], wi[:, s_gkl + GLA_GATE_RANK:], wi[:, s_gkl:s_gkl + GLA_GATE_RANK],
         jnp.zeros((D_MODEL, LANES - GLA_GATE_RANK), F32)], axis=1).astype(BF16)
    wgk = jnp.concatenate(
        [w_gk_up[l], jnp.zeros((LANES - GLA_GATE_RANK, GLA_KDIM), F32)], axis=0).astype(BF16)
    wsp = w_spatial[l].transpose(1, 0, 2).reshape(GMLP_CHUNK, GMLP_HEADS * GMLP_CHUNK)
    bsp = jnp.repeat(b_spatial[l].T, GMLP_DH, axis=1)
    wconv = jnp.concatenate([w_conv[l], jnp.zeros((8 - CONV_K, CONV_WIDTH), F32)], axis=0)
    wr = jnp.concatenate(
        [w_router_group[l], w_router_expert[l],
         jnp.zeros((D_MODEL, ROUTER_COLS - N_GROUPS - N_EXPERTS), F32)], axis=1)
    wr_hi = wr.astype(BF16)
    wr_lo = (wr - wr_hi.astype(F32)).astype(BF16)
    br = jnp.concatenate(
        [b_router_group[l], b_router_expert[l],
         jnp.zeros((ROUTER_COLS - N_GROUPS - N_EXPERTS,), F32)])[None, :]
    return dict(
        w_p=w_p, wgk=wgk, bgk=b_gk[l][None, :], glan=gla_norm[l][None, :],
        gmn=gmlp_norm[l][None, :], wsp=wsp, bsp=bsp, wconv=wconv, wo=w_out[l].astype(BF16),
        wr_hi=wr_hi, wr_lo=wr_lo, br=br,
        wg=w_gate[l].astype(BF16), wu=w_up[l].astype(BF16), wd=w_down[l].astype(BF16))


def kernel(x, attn_norm, w_in, w_gk_up, b_gk, gla_norm, gmlp_norm, w_spatial, b_spatial, w_conv, w_out, ffn_norm, w_router_group, b_router_group, w_router_expert, b_router_expert, w_gate, w_up, w_down, final_norm):
    B, S, D = x.shape
    T = B * S
    depth = w_in.shape[0]
    xs = [x.reshape(T, D)]
    for l in range(depth):
        p = _prep_layer(l, w_in, w_gk_up, b_gk, gla_norm, gmlp_norm, w_spatial, b_spatial, w_conv,
                        w_out, w_router_group, b_router_group, w_router_expert, b_router_expert,
                        w_gate, w_up, w_down)
        xr, proj = _norm_proj(xs, attn_norm[l][None, :], p["w_p"])
        mixed = _mixers(proj, p["wgk"], p["bgk"], p["glan"], p["gmn"], p["wsp"], p["bsp"],
                        p["wconv"], B, S)
        x2, h2, logits = _out_router(mixed, xr, p["wo"], ffn_norm[l][None, :],
                                     p["wr_hi"], p["wr_lo"], p["br"])
        slot_tok, slot_w, block_exp = _route(logits, T)
        h2_pad = jnp.concatenate([h2, jnp.zeros((1, D), h2.dtype)], axis=0)
        xb = h2_pad[slot_tok]
        yb = _experts(block_exp, xb, slot_w[:, None], p["wg"], p["wu"], p["wd"])
        y = jax.ops.segment_sum(yb, slot_tok, num_segments=T + 1)[:T]
        xs = [x2, y]
    out = _final_norm(xs[0], xs[1], final_norm[None, :])
    return out.reshape(B, S, D)
```

```python
import jax
import jax.numpy as jnp
from jax import lax
from jax.experimental import pallas as pl
from jax.experimental.pallas import tpu as pltpu

F32 = jnp.float32
BF16 = jnp.bfloat16

D_MODEL = 1024
RMS_EPS = 1e-6
GLA_HEADS = 4
GLA_WIDTH = 512
GLA_DV = 128
GLA_DK = 64
GLA_KDIM = 256
GLA_GATE_RANK = 16
GLA_GATE_NORM = 16.0
GLA_CHUNK = 64
GMLP_HEADS = 4
GMLP_WIDTH = 256
GMLP_DH = 64
GMLP_CHUNK = 128
CONV_WIDTH = 256
CONV_K = 3
N_GROUPS = 4
EXPERTS_PER_GROUP = 8
N_EXPERTS = 32
TOP_K = 2
D_EXPERT = 256

LANES = 128
C_Q, C_K, C_V, C_G = 0, 256, 512, 1024
C_U, C_VG, C_X, C_BG, C_CG, C_GKL = 1536, 1792, 2048, 2304, 2560, 2816
D_PROJ = C_GKL + LANES

TM = 256
TS_MIX = 256
MOE_BLK = 256
ROUTER_COLS = LANES
D_PACK = D_MODEL // 2
VMEM_LIMIT = 56 * 1024 * 1024
R_E, R_RANK, R_W = 0, 2, 4


def _dot(a, b):
    return jnp.dot(a, b, preferred_element_type=F32)


def _split_bf16(x):
    hi = x.astype(BF16)
    lo = (x - hi.astype(F32)).astype(BF16)
    return hi, lo


def _rms(x, gain):
    return x * lax.rsqrt(jnp.mean(x * x, axis=-1, keepdims=True) + RMS_EPS) * gain


def _norm_proj_kernel(x_ref, gain_ref, w_ref, proj_ref):
    proj_ref[...] = _dot(_rms(x_ref[...], gain_ref[...]).astype(BF16), w_ref[...])


def _norm_proj(x, gain, w_p):
    T = x.shape[0]
    return pl.pallas_call(
        _norm_proj_kernel,
        grid=(T // TM,),
        in_specs=[
            pl.BlockSpec((TM, D_MODEL), lambda i: (i, 0)),
            pl.BlockSpec((1, D_MODEL), lambda i: (0, 0)),
            pl.BlockSpec((D_MODEL, D_PROJ), lambda i: (0, 0)),
        ],
        out_specs=pl.BlockSpec((TM, D_PROJ), lambda i: (i, 0)),
        out_shape=jax.ShapeDtypeStruct((T, D_PROJ), F32),
        compiler_params=pltpu.CompilerParams(
            dimension_semantics=("arbitrary",), vmem_limit_bytes=VMEM_LIMIT),
        name="norm_proj",
    )(x, gain, w_p)


def _row_gather_copy(yb_ref, buf_ref, sem_ref, slot, k, r, d):
    return pltpu.make_async_copy(yb_ref.at[pl.ds(d, 1)], buf_ref.at[slot, k, pl.ds(r, 1)],
                                 sem_ref.at[slot])


def _gather_start(dest_ref, yb_ref, buf_ref, sem_ref, slot):
    def body(r, carry):
        for k in range(TOP_K):
            _row_gather_copy(yb_ref, buf_ref, sem_ref, slot, k, r, dest_ref[0, 0, k * TM + r]).start()
        return carry
    lax.fori_loop(0, TM, body, 0, unroll=8)


def _gather_wait(yb_ref, buf_ref, sem_ref, slot):
    for k in range(TOP_K):
        pltpu.make_async_copy(yb_ref.at[pl.ds(0, TM)], buf_ref.at[slot, k], sem_ref.at[slot]).wait()


def _combined_residual(dcur_ref, dnxt_ref, x_ref, route_ref, yb_ref, buf_ref, sem_ref):
    i = pl.program_id(0)
    n = pl.num_programs(0)
    slot = lax.rem(i, 2)

    @pl.when(i == 0)
    def _():
        _gather_start(dcur_ref, yb_ref, buf_ref, sem_ref, 0)

    @pl.when(i + 1 < n)
    def _():
        _gather_start(dnxt_ref, yb_ref, buf_ref, sem_ref, 1 - slot)

    _gather_wait(yb_ref, buf_ref, sem_ref, slot)
    w0 = route_ref[:, R_W:R_W + 1]
    w1 = route_ref[:, R_W + 1:R_W + 2]
    return x_ref[...] + (w0 * buf_ref[slot, 0] + w1 * buf_ref[slot, 1])


def _combine_specs(n_tiles):
    smem_tile = lambda f: pl.BlockSpec((1, 1, TOP_K * TM), f, memory_space=pltpu.SMEM)
    return [
        smem_tile(lambda i: (i, 0, 0)),
        smem_tile(lambda i: (jnp.minimum(i + 1, n_tiles - 1), 0, 0)),
        pl.BlockSpec((TM, D_MODEL), lambda i: (i, 0)),
        pl.BlockSpec((TM, LANES), lambda i: (i, 0)),
        pl.BlockSpec(memory_space=pl.ANY),
    ]


_COMBINE_SCRATCH = [pltpu.VMEM((2, TOP_K, TM, D_MODEL), F32), pltpu.SemaphoreType.DMA((2,))]


def _combine_norm_proj_kernel(dcur_ref, dnxt_ref, x_ref, route_ref, yb_ref, gain_ref, w_ref,
                              xo_ref, proj_ref, buf_ref, sem_ref):
    x = _combined_residual(dcur_ref, dnxt_ref, x_ref, route_ref, yb_ref, buf_ref, sem_ref)
    xo_ref[...] = x
    proj_ref[...] = _dot(_rms(x, gain_ref[...]).astype(BF16), w_ref[...])


def _combine_norm_proj(dest_tiles, x2, route, yb, gain, w_p):
    T = x2.shape[0]
    n_tiles = T // TM
    return pl.pallas_call(
        _combine_norm_proj_kernel,
        grid=(n_tiles,),
        in_specs=_combine_specs(n_tiles) + [
            pl.BlockSpec((1, D_MODEL), lambda i: (0, 0)),
            pl.BlockSpec((D_MODEL, D_PROJ), lambda i: (0, 0)),
        ],
        out_specs=[pl.BlockSpec((TM, D_MODEL), lambda i: (i, 0)),
                   pl.BlockSpec((TM, D_PROJ), lambda i: (i, 0))],
        out_shape=[jax.ShapeDtypeStruct((T, D_MODEL), F32),
                   jax.ShapeDtypeStruct((T, D_PROJ), F32)],
        scratch_shapes=_COMBINE_SCRATCH,
        compiler_params=pltpu.CompilerParams(
            dimension_semantics=("arbitrary",), vmem_limit_bytes=VMEM_LIMIT),
        name="combine_norm_proj",
    )(dest_tiles, dest_tiles, x2, route, yb, gain, w_p)


def _combine_final_norm_kernel(dcur_ref, dnxt_ref, x_ref, route_ref, yb_ref, gain_ref,
                               o_ref, buf_ref, sem_ref):
    x = _combined_residual(dcur_ref, dnxt_ref, x_ref, route_ref, yb_ref, buf_ref, sem_ref)
    o_ref[...] = _rms(x, gain_ref[...])


def _combine_final_norm(dest_tiles, x2, route, yb, gain):
    T = x2.shape[0]
    n_tiles = T // TM
    return pl.pallas_call(
        _combine_final_norm_kernel,
        grid=(n_tiles,),
        in_specs=_combine_specs(n_tiles) + [pl.BlockSpec((1, D_MODEL), lambda i: (0, 0))],
        out_specs=pl.BlockSpec((TM, D_MODEL), lambda i: (i, 0)),
        out_shape=jax.ShapeDtypeStruct((T, D_MODEL), F32),
        scratch_shapes=_COMBINE_SCRATCH,
        compiler_params=pltpu.CompilerParams(
            dimension_semantics=("arbitrary",), vmem_limit_bytes=VMEM_LIMIT),
        name="combine_final_norm",
    )(dest_tiles, dest_tiles, x2, route, yb, gain)


def _gelu_tanh(x):
    c = 0.7978845608028654
    return x * (0.5 * (1.0 + jnp.tanh(c * (x + 0.044715 * (x * x * x)))))


def _mixer_kernel(proj_ref, wgk_ref, bgk_ref, glan_ref, gmn_ref, wsp_ref, bsp_ref, wconv_ref,
                  out_ref, st_ref, hc_ref, lcat_ref, wm_ref):
    TS = TS_MIX
    n_gla = TS // GLA_CHUNK
    n_gm = TS // GMLP_CHUNK

    @pl.when(pl.program_id(1) == 0)
    def _():
        st_ref[...] = jnp.zeros_like(st_ref)
        hc_ref[...] = jnp.zeros_like(hc_ref)
        r = lax.broadcasted_iota(jnp.int32, (2 * TS, TS), 0)
        c = lax.broadcasted_iota(jnp.int32, (2 * TS, TS), 1)
        rr = jnp.where(r >= TS, r - TS, r)
        same = (rr // GLA_CHUNK) == (c // GLA_CHUNK)
        keep = same & ((r >= TS) | (c <= rr))
        lcat_ref[...] = jnp.where(keep, 1.0, 0.0).astype(BF16)
        t = lax.broadcasted_iota(jnp.int32, (GMLP_CHUNK, GMLP_HEADS * GMLP_CHUNK), 0)
        s = lax.broadcasted_iota(jnp.int32, (GMLP_CHUNK, GMLP_HEADS * GMLP_CHUNK), 1) % GMLP_CHUNK
        wm_ref[...] = jnp.where(s <= t, wsp_ref[...], 0.0).astype(BF16)

    lane256 = lax.broadcasted_iota(jnp.int32, (1, GLA_KDIM), 1)

    q = proj_ref[:, C_Q:C_Q + GLA_KDIM]
    k = proj_ref[:, C_K:C_K + GLA_KDIM]
    v = proj_ref[:, C_V:C_V + GLA_WIDTH]
    z = _dot(proj_ref[:, C_GKL:C_GKL + LANES].astype(BF16), wgk_ref[...]) + bgk_ref[...]
    gk = (jnp.minimum(z, 0.0) - jnp.log1p(jnp.exp(-jnp.abs(z)))) * (1.0 / GLA_GATE_NORM)
    gk_hi, gk_lo = _split_bf16(gk)
    cs = _dot(lcat_ref[...], jnp.concatenate([gk_hi, gk_lo], axis=1))
    b = cs[:TS, :GLA_KDIM] + cs[:TS, GLA_KDIM:]
    bl = cs[TS:, :GLA_KDIM] + cs[TS:, GLA_KDIM:]
    q_dec = (q * (GLA_DK ** -0.5)) * jnp.exp(b)
    k_inv = (k * jnp.exp(-b)).astype(BF16)
    k_dec = (k * jnp.exp(bl - b)).astype(BF16)
    q_dec_b = q_dec.astype(BF16)
    v_b = v.astype(BF16)

    zero_b = jnp.zeros_like(q_dec_b)
    q_stack = jnp.concatenate(
        [jnp.where((lane256 // GLA_DK) == h, q_dec_b, zero_b) for h in range(GLA_HEADS)], axis=0)
    scores = lax.dot_general(q_stack, k_inv, (((1,), (1,)), ((), ())),
                             preferred_element_type=F32)
    rt = lax.broadcasted_iota(jnp.int32, (TS, TS), 0)
    ct = lax.broadcasted_iota(jnp.int32, (TS, TS), 1)
    causal = ((rt // GLA_CHUNK) == (ct // GLA_CHUNK)) & (ct <= rt)
    o_heads = []
    for h in range(GLA_HEADS):
        p_h = jnp.where(causal, scores[h * TS:(h + 1) * TS, :], 0.0).astype(BF16)
        o_heads.append(_dot(p_h, v_b[:, h * GLA_DV:(h + 1) * GLA_DV]))

    sr = lax.broadcasted_iota(jnp.int32, (GLA_WIDTH, GLA_KDIM), 0) // GLA_DV
    sc = lax.broadcasted_iota(jnp.int32, (GLA_WIDTH, GLA_KDIM), 1) // GLA_DK
    bd_mask = sr == sc
    o_inter = []
    for c in range(n_gla):
        rows = slice(c * GLA_CHUNK, (c + 1) * GLA_CHUNK)
        st = st_ref[...]
        o_inter.append(lax.dot_general(q_dec_b[rows], st.astype(BF16), (((1,), (1,)), ((), ())),
                                       preferred_element_type=F32))
        upd = lax.dot_general(v_b[rows], k_dec[rows], (((0,), (0,)), ((), ())),
                              preferred_element_type=F32)
        decay = jnp.exp(bl[c * GLA_CHUNK:c * GLA_CHUNK + 1, :])
        st_ref[...] = st * decay + jnp.where(bd_mask, upd, 0.0)
    o_inter = jnp.concatenate(o_inter, axis=0)

    for h in range(GLA_HEADS):
        cols = slice(h * GLA_DV, (h + 1) * GLA_DV)
        o = o_heads[h] + o_inter[:, cols]
        o = o * lax.rsqrt(jnp.mean(o * o, axis=-1, keepdims=True) + RMS_EPS) * glan_ref[...]
        g = proj_ref[:, C_G + h * GLA_DV:C_G + (h + 1) * GLA_DV]
        out_ref[:, cols] = (o * (g * (1.0 / (1.0 + jnp.exp(-g))))).astype(out_ref.dtype)

    u = _gelu_tanh(proj_ref[:, C_U:C_U + GMLP_WIDTH])
    vg = _gelu_tanh(proj_ref[:, C_VG:C_VG + GMLP_WIDTH])
    hr = lax.broadcasted_iota(jnp.int32, (GMLP_WIDTH, GMLP_WIDTH), 0) // GMLP_DH
    hcn = lax.broadcasted_iota(jnp.int32, (GMLP_WIDTH, GMLP_WIDTH), 1) // GMLP_DH
    head_mean = jnp.where(hr == hcn, 1.0 / GMLP_DH, 0.0).astype(BF16)
    sq_hi, sq_lo = _split_bf16(vg * vg)
    ms = _dot(sq_hi, head_mean) + _dot(sq_lo, head_mean)
    v32 = vg * lax.rsqrt(ms + RMS_EPS) * gmn_ref[...]
    for c in range(n_gm):
        rows = slice(c * GMLP_CHUNK, (c + 1) * GMLP_CHUNK)
        vc = v32[rows].astype(BF16)
        zc = jnp.zeros_like(vc)
        rhs = jnp.concatenate(
            [jnp.where((lane256 // GMLP_DH) == h, vc, zc) for h in range(GMLP_HEADS)], axis=0)
        mixed = _dot(wm_ref[...], rhs) + bsp_ref[...]
        out_ref[rows, GLA_WIDTH:GLA_WIDTH + GMLP_WIDTH] = (u[rows] * mixed).astype(out_ref.dtype)

    hcv = proj_ref[:, C_CG:C_CG + CONV_WIDTH] * proj_ref[:, C_X:C_X + CONV_WIDTH]
    hc_ref[8:8 + TS, :] = hcv
    y = (wconv_ref[2:3, :] * hcv + wconv_ref[1:2, :] * hc_ref[7:7 + TS, :]
         + wconv_ref[0:1, :] * hc_ref[6:6 + TS, :])
    out_ref[:, GLA_WIDTH + GMLP_WIDTH:] = (
        proj_ref[:, C_BG:C_BG + CONV_WIDTH] * y).astype(out_ref.dtype)
    hc_ref[0:8, :] = hc_ref[TS:TS + 8, :]


def _mixers(proj, wgk, bgk, glan, gmn, wsp, bsp, wconv, batch, seq):
    n_seq = seq // TS_MIX
    full = lambda shape: pl.BlockSpec(shape, lambda b, i: (0,) * len(shape))
    return pl.pallas_call(
        _mixer_kernel,
        grid=(batch, n_seq),
        in_specs=[
            pl.BlockSpec((TS_MIX, D_PROJ), lambda b, i: (b * n_seq + i, 0)),
            full((LANES, GLA_KDIM)), full((1, GLA_KDIM)), full((1, GLA_DV)), full((1, GMLP_WIDTH)),
            full((GMLP_CHUNK, GMLP_HEADS * GMLP_CHUNK)), full((GMLP_CHUNK, GMLP_WIDTH)),
            full((8, CONV_WIDTH)),
        ],
        out_specs=pl.BlockSpec((TS_MIX, D_MODEL), lambda b, i: (b * n_seq + i, 0)),
        out_shape=jax.ShapeDtypeStruct((batch * seq, D_MODEL), BF16),
        scratch_shapes=[
            pltpu.VMEM((GLA_WIDTH, GLA_KDIM), F32),
            pltpu.VMEM((TS_MIX + 8, CONV_WIDTH), F32),
            pltpu.VMEM((2 * TS_MIX, TS_MIX), BF16),
            pltpu.VMEM((GMLP_CHUNK, GMLP_HEADS * GMLP_CHUNK), BF16),
        ],
        compiler_params=pltpu.CompilerParams(
            dimension_semantics=("arbitrary", "arbitrary"), vmem_limit_bytes=VMEM_LIMIT),
        name="mixers",
    )(proj, wgk, bgk, glan, gmn, wsp, bsp, wconv)


def _pack_bf16_pairs(h_b):
    lo = lax.bitcast_convert_type(h_b[:, :D_PACK].astype(F32), jnp.uint32) >> 16
    hi = lax.bitcast_convert_type(h_b[:, D_PACK:].astype(F32), jnp.uint32) & jnp.uint32(0xFFFF0000)
    return hi | lo


def _unpack_bf16_pairs(w):
    lo = lax.bitcast_convert_type(w << 16, F32).astype(BF16)
    hi = lax.bitcast_convert_type(w & jnp.uint32(0xFFFF0000), F32).astype(BF16)
    return lo, hi


def _out_router_kernel(mix_ref, x_ref, wo_ref, gain_ref, wrh_ref, wrl_ref, br_ref,
                       x2_ref, h2p_ref, route_ref, cnt_ref, tri_ref):
    @pl.when(pl.program_id(0) == 0)
    def _():
        cnt_ref[...] = jnp.zeros_like(cnt_ref)
        r = lax.broadcasted_iota(jnp.int32, (TM, TM), 0)
        c = lax.broadcasted_iota(jnp.int32, (TM, TM), 1)
        tri_ref[...] = jnp.where(c < r, 1.0, 0.0).astype(BF16)

    x2 = x_ref[...] + _dot(mix_ref[...], wo_ref[...])
    x2_ref[...] = x2
    h = _rms(x2, gain_ref[...])
    h_hi, h_lo = _split_bf16(h)
    h2p_ref[...] = _pack_bf16_pairs(h_hi)
    lg = (_dot(h_hi, wrh_ref[...]) + _dot(h_lo, wrh_ref[...]) + _dot(h_hi, wrl_ref[...])
          + br_ref[...])

    lane = lax.broadcasted_iota(jnp.int32, (TM, LANES), 1).astype(F32)
    neg = -jnp.inf
    is_g = lane < N_GROUPS
    gl = jnp.where(is_g, lg, neg)
    gmax = jnp.max(gl, axis=1, keepdims=True)
    g_top = jnp.min(jnp.where(gl == gmax, lane, float(LANES)), axis=1, keepdims=True)
    g_w = 1.0 / jnp.sum(jnp.where(is_g, jnp.exp(lg - gmax), 0.0), axis=1, keepdims=True)
    first = N_GROUPS + EXPERTS_PER_GROUP * g_top
    el = jnp.where((lane >= first) & (lane < first + EXPERTS_PER_GROUP), lg, neg)
    m1 = jnp.max(el, axis=1, keepdims=True)
    i1 = jnp.min(jnp.where(el == m1, lane, float(LANES)), axis=1, keepdims=True)
    el2 = jnp.where(lane == i1, neg, el)
    m2 = jnp.max(el2, axis=1, keepdims=True)
    i2 = jnp.min(jnp.where(el2 == m2, lane, float(LANES)), axis=1, keepdims=True)
    ratio = jnp.exp(m2 - m1)
    w1 = g_w / (1.0 + ratio)
    w2 = w1 * ratio

    oh1 = jnp.where(lane == i1, 1.0, 0.0)
    oh2 = jnp.where(lane == i2, 1.0, 0.0)
    oh = oh1 + oh2
    before = _dot(tri_ref[...], oh.astype(BF16)) + cnt_ref[0:1, :]
    rank1 = jnp.sum(oh1 * before, axis=1, keepdims=True)
    rank2 = jnp.sum(oh2 * before, axis=1, keepdims=True)
    cnt_ref[...] = cnt_ref[...] + jnp.sum(oh, axis=0, keepdims=True)

    rec = jnp.zeros((TM, LANES), F32)
    for col, val in ((R_E, i1 - N_GROUPS), (R_E + 1, i2 - N_GROUPS), (R_RANK, rank1),
                     (R_RANK + 1, rank2), (R_W, w1), (R_W + 1, w2)):
        rec = jnp.where(lane == col, val, rec)
    route_ref[...] = rec


def _out_router(mixed, x, wo, gain, wr_hi, wr_lo, br):
    T = x.shape[0]
    row = lambda w: pl.BlockSpec((TM, w), lambda i: (i, 0))
    full = lambda shape: pl.BlockSpec(shape, lambda i: (0, 0))
    return pl.pallas_call(
        _out_router_kernel,
        grid=(T // TM,),
        in_specs=[row(D_MODEL), row(D_MODEL), full((D_MODEL, D_MODEL)), full((1, D_MODEL)),
                  full((D_MODEL, ROUTER_COLS)), full((D_MODEL, ROUTER_COLS)), full((1, ROUTER_COLS))],
        out_specs=[row(D_MODEL), row(D_PACK), row(LANES), full((8, LANES))],
        out_shape=[jax.ShapeDtypeStruct((T, D_MODEL), F32),
                   jax.ShapeDtypeStruct((T, D_PACK), jnp.uint32),
                   jax.ShapeDtypeStruct((T, LANES), F32),
                   jax.ShapeDtypeStruct((8, LANES), F32)],
        scratch_shapes=[pltpu.VMEM((TM, TM), BF16)],
        compiler_params=pltpu.CompilerParams(
            dimension_semantics=("arbitrary",), vmem_limit_bytes=VMEM_LIMIT),
        name="out_router",
    )(mixed, x, wo, gain, wr_hi, wr_lo, br)


def _dispatch_kernel(fill_ref, nu_ref, dest_ref, h_ref, xb_ref, zero_ref, sem_ref, zsem_ref):
    @pl.when(pl.program_id(0) == 0)
    def _():
        zero_ref[...] = jnp.zeros_like(zero_ref)
        fills = [(fill_ref[e] >= 0, pltpu.make_async_copy(
            zero_ref, xb_ref.at[pl.ds(pl.multiple_of(jnp.maximum(fill_ref[e], 0), MOE_BLK), MOE_BLK)],
            zsem_ref)) for e in range(N_EXPERTS)]
        n_blocks = xb_ref.shape[0] // MOE_BLK
        fills += [(j >= nu_ref[0], pltpu.make_async_copy(
            zero_ref, xb_ref.at[pl.ds(j * MOE_BLK, MOE_BLK)], zsem_ref))
            for j in range(n_blocks - N_EXPERTS, n_blocks)]
        for cond, f in fills:
            pl.when(cond)(f.start)
        for cond, f in fills:
            pl.when(cond)(f.wait)

    def body(r, carry):
        for k in range(TOP_K):
            pltpu.make_async_copy(h_ref.at[pl.ds(r, 1)],
                                  xb_ref.at[pl.ds(dest_ref[0, 0, k * TM + r], 1)], sem_ref).start()
        return carry
    lax.fori_loop(0, TM, body, 0, unroll=8)
    for _ in range(TOP_K):
        pltpu.make_async_copy(h_ref, xb_ref.at[pl.ds(0, TM)], sem_ref).wait()


def _dispatch(fill_start, n_used, dest_tiles, h2p, n_rows):
    T = h2p.shape[0]
    grid_spec = pltpu.PrefetchScalarGridSpec(
        num_scalar_prefetch=2,
        grid=(T // TM,),
        in_specs=[
            pl.BlockSpec((1, 1, TOP_K * TM), lambda i, fs, nu: (i, 0, 0), memory_space=pltpu.SMEM),
            pl.BlockSpec((TM, D_PACK), lambda i, fs, nu: (i, 0)),
        ],
        out_specs=pl.BlockSpec(memory_space=pl.ANY),
        scratch_shapes=[pltpu.VMEM((MOE_BLK, D_PACK), jnp.uint32),
                        pltpu.SemaphoreType.DMA(()), pltpu.SemaphoreType.DMA(())],
    )
    return pl.pallas_call(
        _dispatch_kernel,
        grid_spec=grid_spec,
        out_shape=jax.ShapeDtypeStruct((n_rows, D_PACK), jnp.uint32),
        compiler_params=pltpu.CompilerParams(dimension_semantics=("arbitrary",)),
        name="dispatch",
    )(fill_start, n_used, dest_tiles, h2p)


def _expert_kernel(blk_ref, be_ref, nu_ref, x_ref, wg_ref, wu_ref, wd_ref, o_ref):
    del blk_ref, be_ref
    used = pl.program_id(0) < nu_ref[0]

    @pl.when(used)
    def _():
        x_lo, x_hi = _unpack_bf16_pairs(x_ref[...])
        g = _dot(x_lo, wg_ref[0, :D_PACK, :]) + _dot(x_hi, wg_ref[0, D_PACK:, :])
        u = _dot(x_lo, wu_ref[0, :D_PACK, :]) + _dot(x_hi, wu_ref[0, D_PACK:, :])
        h = (g * (1.0 / (1.0 + jnp.exp(-g)))) * u
        o_ref[...] = _dot(h.astype(BF16), wd_ref[0])

    @pl.when(jnp.logical_not(used))
    def _():
        o_ref[...] = jnp.zeros_like(o_ref)


def _experts(blk_idx, blk_exp, n_used, xb, wg, wu, wd):
    n_steps = blk_idx.shape[0]
    grid_spec = pltpu.PrefetchScalarGridSpec(
        num_scalar_prefetch=3,
        grid=(n_steps,),
        in_specs=[
            pl.BlockSpec((MOE_BLK, D_PACK), lambda j, bi, be, nu: (bi[j], 0)),
            pl.BlockSpec((1, D_MODEL, D_EXPERT), lambda j, bi, be, nu: (be[j], 0, 0)),
            pl.BlockSpec((1, D_MODEL, D_EXPERT), lambda j, bi, be, nu: (be[j], 0, 0)),
            pl.BlockSpec((1, D_EXPERT, D_MODEL), lambda j, bi, be, nu: (be[j], 0, 0)),
        ],
        out_specs=pl.BlockSpec((MOE_BLK, D_MODEL), lambda j, bi, be, nu: (j, 0)),
    )
    return pl.pallas_call(
        _expert_kernel,
        grid_spec=grid_spec,
        out_shape=jax.ShapeDtypeStruct((n_steps * MOE_BLK, D_MODEL), F32),
        compiler_params=pltpu.CompilerParams(
            dimension_semantics=("arbitrary",), vmem_limit_bytes=VMEM_LIMIT),
        name="experts",
    )(blk_idx, blk_exp, n_used, xb, wg, wu, wd)


def _dispatch_tables(route, counts_rec, T):
    counts = counts_rec[0, N_GROUPS:N_GROUPS + N_EXPERTS].astype(jnp.int32)
    n_steps = (T * TOP_K) // MOE_BLK + N_EXPERTS
    nblk = (counts + MOE_BLK - 1) // MOE_BLK
    bend = jnp.cumsum(nblk)
    pstart = (bend - nblk) * MOE_BLK
    n_used = bend[-1]
    j = jnp.minimum(jnp.arange(n_steps, dtype=jnp.int32), n_used - 1)
    blk_exp = jnp.minimum(jnp.sum(j[:, None] >= bend[None, :], axis=1), N_EXPERTS - 1)
    n_rows = n_steps * MOE_BLK
    last_blk = jnp.where(counts > 0, (bend - 1) * MOE_BLK, -1)
    e = route[:, R_E:R_E + TOP_K].astype(jnp.int32)
    rank = route[:, R_RANK:R_RANK + TOP_K].astype(jnp.int32)
    seg = jnp.sum(jnp.where(e[..., None] == jnp.arange(N_EXPERTS), pstart, 0), axis=-1)
    dest = jnp.clip(seg + rank, 0, n_steps * MOE_BLK - 1)
    dest_tiles = dest.reshape(T // TM, TM, TOP_K).transpose(0, 2, 1).reshape(T // TM, 1, TOP_K * TM)
    return dict(dest_tiles=dest_tiles, fill_start=last_blk.astype(jnp.int32),
                blk_idx=j, blk_exp=blk_exp.astype(jnp.int32),
                n_used=n_used.reshape(1).astype(jnp.int32), n_rows=n_rows)


def _prep_layer(l, w_in, w_gk_up, b_gk, gla_norm, gmlp_norm, w_spatial, b_spatial, w_conv, w_out,
                w_router_group, b_router_group, w_router_expert, b_router_expert,
                w_gate, w_up, w_down):
    wi = w_in[l]
    s_gkl = 2 * GLA_KDIM + 2 * GLA_WIDTH
    w_p = jnp.concatenate(
        [wi[:, :s_gkl], wi[:, s_gkl + GLA_GATE_RANK:], wi[:, s_gkl:s_gkl + GLA_GATE_RANK],
         jnp.zeros((D_MODEL, LANES - GLA_GATE_RANK), F32)], axis=1).astype(BF16)
    wgk = jnp.concatenate(
        [w_gk_up[l], jnp.zeros((LANES - GLA_GATE_RANK, GLA_KDIM), F32)], axis=0).astype(BF16)
    wsp = w_spatial[l].transpose(1, 0, 2).reshape(GMLP_CHUNK, GMLP_HEADS * GMLP_CHUNK)
    bsp = jnp.repeat(b_spatial[l].T, GMLP_DH, axis=1)
    wconv = jnp.concatenate([w_conv[l], jnp.zeros((8 - CONV_K, CONV_WIDTH), F32)], axis=0)
    wr = jnp.concatenate(
        [w_router_group[l], w_router_expert[l],
         jnp.zeros((D_MODEL, ROUTER_COLS - N_GROUPS - N_EXPERTS), F32)], axis=1)
    wr_hi = wr.astype(BF16)
    wr_lo = (wr - wr_hi.astype(F32)).astype(BF16)
    br = jnp.concatenate(
        [b_router_group[l], b_router_expert[l],
         jnp.zeros((ROUTER_COLS - N_GROUPS - N_EXPERTS,), F32)])[None, :]
    return dict(
        w_p=w_p, wgk=wgk, bgk=b_gk[l][None, :], glan=gla_norm[l][None, :],
        gmn=gmlp_norm[l][None, :], wsp=wsp, bsp=bsp, wconv=wconv, wo=w_out[l].astype(BF16),
        wr_hi=wr_hi, wr_lo=wr_lo, br=br,
        wg=w_gate[l].astype(BF16), wu=w_up[l].astype(BF16), wd=w_down[l].astype(BF16))


def kernel(x, attn_norm, w_in, w_gk_up, b_gk, gla_norm, gmlp_norm, w_spatial, b_spatial, w_conv, w_out, ffn_norm, w_router_group, b_router_group, w_router_expert, b_router_expert, w_gate, w_up, w_down, final_norm):
    B, S, D = x.shape
    T = B * S
    depth = w_in.shape[0]
    xr = x.reshape(T, D)
    moe = None
    for l in range(depth):
        p = _prep_layer(l, w_in, w_gk_up, b_gk, gla_norm, gmlp_norm, w_spatial, b_spatial, w_conv,
                        w_out, w_router_group, b_router_group, w_router_expert, b_router_expert,
                        w_gate, w_up, w_down)
        if moe is None:
            proj = _norm_proj(xr, attn_norm[l][None, :], p["w_p"])
        else:
            xr, proj = _combine_norm_proj(moe["dest_tiles"], moe["x2"], moe["route"], moe["yb"],
                                          attn_norm[l][None, :], p["w_p"])
        mixed = _mixers(proj, p["wgk"], p["bgk"], p["glan"], p["gmn"], p["wsp"], p["bsp"],
                        p["wconv"], B, S)
        x2, h2p, route, counts_rec = _out_router(mixed, xr, p["wo"], ffn_norm[l][None, :],
                                                 p["wr_hi"], p["wr_lo"], p["br"])
        moe = _dispatch_tables(route, counts_rec, T)
        xb = _dispatch(moe["fill_start"], moe["n_used"], moe["dest_tiles"], h2p, moe["n_rows"])
        yb = _experts(moe["blk_idx"], moe["blk_exp"], moe["n_used"], xb, p["wg"], p["wu"], p["wd"])
        moe.update(x2=x2, route=route, yb=yb)
    out = _combine_final_norm(moe["dest_tiles"], moe["x2"], moe["route"], moe["yb"],
                              final_norm[None, :])
    return out.reshape(B, S, D)
```

```python
import jax
import jax.numpy as jnp
from jax import lax
from jax.experimental import pallas as pl
from jax.experimental.pallas import tpu as pltpu

F32 = jnp.float32
BF16 = jnp.bfloat16

D_MODEL = 1024
RMS_EPS = 1e-6
GLA_HEADS = 4
GLA_WIDTH = 512
GLA_DV = 128
GLA_DK = 64
GLA_KDIM = 256
GLA_GATE_RANK = 16
GLA_GATE_NORM = 16.0
GLA_CHUNK = 64
GMLP_HEADS = 4
GMLP_WIDTH = 256
GMLP_DH = 64
GMLP_CHUNK = 128
CONV_WIDTH = 256
CONV_K = 3
N_GROUPS = 4
EXPERTS_PER_GROUP = 8
N_EXPERTS = 32
TOP_K = 2
D_EXPERT = 256

LANES = 128
C_Q, C_K, C_V, C_G = 0, 256, 512, 1024
C_U, C_VG, C_X, C_BG, C_CG, C_GKL = 1536, 1792, 2048, 2304, 2560, 2816
D_PROJ = C_GKL + LANES
D_IN = C_GKL + GLA_GATE_RANK

TM = 256
TS_MIX = 256
MOE_BLK = 256
ROUTER_COLS = LANES
D_PACK = D_MODEL // 2
VMEM_LIMIT = 56 * 1024 * 1024
R_E, R_RANK, R_W = 0, 2, 4


def _dot(a, b):
    return jnp.dot(a, b, preferred_element_type=F32)


def _split_bf16(x):
    hi = x.astype(BF16)
    lo = (x - hi.astype(F32)).astype(BF16)
    return hi, lo


def _rms(x, gain):
    return x * lax.rsqrt(jnp.mean(x * x, axis=-1, keepdims=True) + RMS_EPS) * gain


W_PREP_ROWS = 128


def _stage_w_in(w_ref, wb_ref):
    for r0 in range(0, D_MODEL, W_PREP_ROWS):
        rows = slice(r0, r0 + W_PREP_ROWS)
        wb_ref[rows, 0:C_U] = w_ref[0, rows, 0:C_U].astype(BF16)
        wb_ref[rows, C_U:C_GKL] = w_ref[0, rows, C_U + GLA_GATE_RANK:D_IN].astype(BF16)
        low = w_ref[0, rows, C_U:C_U + GLA_GATE_RANK]
        wb_ref[rows, C_GKL:D_PROJ] = jnp.concatenate(
            [low, jnp.zeros((W_PREP_ROWS, LANES - GLA_GATE_RANK), F32)], axis=1).astype(BF16)


def _w_in_specs(layer):
    return pl.BlockSpec((1, D_MODEL, D_IN), lambda i: (layer, 0, 0), pipeline_mode=pl.Buffered(1))


def _norm_proj_kernel(x_ref, gain_ref, w_ref, proj_ref, wb_ref):
    @pl.when(pl.program_id(0) == 0)
    def _():
        _stage_w_in(w_ref, wb_ref)
    proj_ref[...] = _dot(_rms(x_ref[...], gain_ref[...]).astype(BF16), wb_ref[...])


def _norm_proj(x, gain, w_in, layer):
    T = x.shape[0]
    return pl.pallas_call(
        _norm_proj_kernel,
        grid=(T // TM,),
        in_specs=[
            pl.BlockSpec((TM, D_MODEL), lambda i: (i, 0)),
            pl.BlockSpec((1, D_MODEL), lambda i: (0, 0)),
            _w_in_specs(layer),
        ],
        out_specs=pl.BlockSpec((TM, D_PROJ), lambda i: (i, 0)),
        out_shape=jax.ShapeDtypeStruct((T, D_PROJ), F32),
        scratch_shapes=[pltpu.VMEM((D_MODEL, D_PROJ), BF16)],
        compiler_params=pltpu.CompilerParams(
            dimension_semantics=("arbitrary",), vmem_limit_bytes=VMEM_LIMIT),
        name="norm_proj",
    )(x, gain, w_in)


def _row_gather_copy(yb_ref, buf_ref, sem_ref, slot, k, r, d):
    return pltpu.make_async_copy(yb_ref.at[pl.ds(d, 1)], buf_ref.at[slot, k, pl.ds(r, 1)],
                                 sem_ref.at[slot])


def _gather_start(dest_ref, yb_ref, buf_ref, sem_ref, slot):
    for r in range(TM):
        for k in range(TOP_K):
            _row_gather_copy(yb_ref, buf_ref, sem_ref, slot, k, r,
                             dest_ref[0, 0, k * TM + r]).start(priority=k)


def _gather_wait(yb_ref, buf_ref, sem_ref, slot):
    for k in range(TOP_K):
        pltpu.make_async_copy(yb_ref.at[pl.ds(0, TM)], buf_ref.at[slot, k], sem_ref.at[slot]).wait()


def _combined_residual(dcur_ref, dnxt_ref, x_ref, route_ref, yb_ref, buf_ref, sem_ref):
    i = pl.program_id(0)
    n = pl.num_programs(0)
    slot = lax.rem(i, 2)

    @pl.when(i == 0)
    def _():
        _gather_start(dcur_ref, yb_ref, buf_ref, sem_ref, 0)

    @pl.when(i + 1 < n)
    def _():
        _gather_start(dnxt_ref, yb_ref, buf_ref, sem_ref, 1 - slot)

    _gather_wait(yb_ref, buf_ref, sem_ref, slot)
    w0 = route_ref[:, R_W:R_W + 1]
    w1 = route_ref[:, R_W + 1:R_W + 2]
    return x_ref[...] + (w0 * buf_ref[slot, 0] + w1 * buf_ref[slot, 1])


def _combine_specs(n_tiles):
    smem_tile = lambda f: pl.BlockSpec((1, 1, TOP_K * TM), f, memory_space=pltpu.SMEM)
    return [
        smem_tile(lambda i: (i, 0, 0)),
        smem_tile(lambda i: (jnp.minimum(i + 1, n_tiles - 1), 0, 0)),
        pl.BlockSpec((TM, D_MODEL), lambda i: (i, 0)),
        pl.BlockSpec((TM, LANES), lambda i: (i, 0)),
        pl.BlockSpec(memory_space=pl.ANY),
    ]


_COMBINE_SCRATCH = [pltpu.VMEM((2, TOP_K, TM, D_MODEL), F32), pltpu.SemaphoreType.DMA((2,))]


def _combine_norm_proj_kernel(dcur_ref, dnxt_ref, x_ref, route_ref, yb_ref, gain_ref, w_ref,
                              xo_ref, proj_ref, buf_ref, sem_ref, wb_ref):
    @pl.when(pl.program_id(0) == 0)
    def _():
        _stage_w_in(w_ref, wb_ref)
    x = _combined_residual(dcur_ref, dnxt_ref, x_ref, route_ref, yb_ref, buf_ref, sem_ref)
    xo_ref[...] = x
    proj_ref[...] = _dot(_rms(x, gain_ref[...]).astype(BF16), wb_ref[...])


def _combine_norm_proj(dest_tiles, x2, route, yb, gain, w_in, layer):
    T = x2.shape[0]
    n_tiles = T // TM
    return pl.pallas_call(
        _combine_norm_proj_kernel,
        grid=(n_tiles,),
        in_specs=_combine_specs(n_tiles) + [
            pl.BlockSpec((1, D_MODEL), lambda i: (0, 0)),
            _w_in_specs(layer),
        ],
        out_specs=[pl.BlockSpec((TM, D_MODEL), lambda i: (i, 0)),
                   pl.BlockSpec((TM, D_PROJ), lambda i: (i, 0))],
        out_shape=[jax.ShapeDtypeStruct((T, D_MODEL), F32),
                   jax.ShapeDtypeStruct((T, D_PROJ), F32)],
        scratch_shapes=_COMBINE_SCRATCH + [pltpu.VMEM((D_MODEL, D_PROJ), BF16)],
        compiler_params=pltpu.CompilerParams(
            dimension_semantics=("arbitrary",), vmem_limit_bytes=VMEM_LIMIT),
        name="combine_norm_proj",
    )(dest_tiles, dest_tiles, x2, route, yb, gain, w_in)


def _combine_final_norm_kernel(dcur_ref, dnxt_ref, x_ref, route_ref, yb_ref, gain_ref,
                               o_ref, buf_ref, sem_ref):
    x = _combined_residual(dcur_ref, dnxt_ref, x_ref, route_ref, yb_ref, buf_ref, sem_ref)
    o_ref[...] = _rms(x, gain_ref[...])


def _combine_final_norm(dest_tiles, x2, route, yb, gain):
    T = x2.shape[0]
    n_tiles = T // TM
    return pl.pallas_call(
        _combine_final_norm_kernel,
        grid=(n_tiles,),
        in_specs=_combine_specs(n_tiles) + [pl.BlockSpec((1, D_MODEL), lambda i: (0, 0))],
        out_specs=pl.BlockSpec((TM, D_MODEL), lambda i: (i, 0)),
        out_shape=jax.ShapeDtypeStruct((T, D_MODEL), F32),
        scratch_shapes=_COMBINE_SCRATCH,
        compiler_params=pltpu.CompilerParams(
            dimension_semantics=("arbitrary",), vmem_limit_bytes=VMEM_LIMIT),
        name="combine_final_norm",
    )(dest_tiles, dest_tiles, x2, route, yb, gain)


def _gelu_tanh(x):
    c = 0.7978845608028654
    return x * (0.5 * (1.0 + jnp.tanh(c * (x + 0.044715 * (x * x * x)))))


def _mixer_kernel(proj_ref, wgk_ref, bgk_ref, glan_ref, gmn_ref, wsp_ref, bsp_ref, wconv_ref,
                  out_ref, st_ref, hc_ref, lcat_ref, wm_ref):
    TS = TS_MIX
    n_gla = TS // GLA_CHUNK
    n_gm = TS // GMLP_CHUNK

    @pl.when(pl.program_id(1) == 0)
    def _():
        st_ref[...] = jnp.zeros_like(st_ref)
        hc_ref[...] = jnp.zeros_like(hc_ref)
        r = lax.broadcasted_iota(jnp.int32, (2 * TS, TS), 0)
        c = lax.broadcasted_iota(jnp.int32, (2 * TS, TS), 1)
        rr = jnp.where(r >= TS, r - TS, r)
        same = (rr // GLA_CHUNK) == (c // GLA_CHUNK)
        keep = same & ((r >= TS) | (c <= rr))
        lcat_ref[...] = jnp.where(keep, 1.0, 0.0).astype(BF16)
        t = lax.broadcasted_iota(jnp.int32, (GMLP_CHUNK, GMLP_HEADS * GMLP_CHUNK), 0)
        s = lax.broadcasted_iota(jnp.int32, (GMLP_CHUNK, GMLP_HEADS * GMLP_CHUNK), 1) % GMLP_CHUNK
        wm_ref[...] = jnp.where(s <= t, wsp_ref[...], 0.0).astype(BF16)

    lane256 = lax.broadcasted_iota(jnp.int32, (1, GLA_KDIM), 1)

    q = proj_ref[:, C_Q:C_Q + GLA_KDIM]
    k = proj_ref[:, C_K:C_K + GLA_KDIM]
    v = proj_ref[:, C_V:C_V + GLA_WIDTH]
    z = _dot(proj_ref[:, C_GKL:C_GKL + LANES].astype(BF16), wgk_ref[...]) + bgk_ref[...]
    gk = (jnp.minimum(z, 0.0) - jnp.log1p(jnp.exp(-jnp.abs(z)))) * (1.0 / GLA_GATE_NORM)
    gk_hi, gk_lo = _split_bf16(gk)
    cs = _dot(lcat_ref[...], jnp.concatenate([gk_hi, gk_lo], axis=1))
    b = cs[:TS, :GLA_KDIM] + cs[:TS, GLA_KDIM:]
    bl = cs[TS:, :GLA_KDIM] + cs[TS:, GLA_KDIM:]
    q_dec = (q * (GLA_DK ** -0.5)) * jnp.exp(b)
    k_inv = (k * jnp.exp(-b)).astype(BF16)
    k_dec = (k * jnp.exp(bl - b)).astype(BF16)
    q_dec_b = q_dec.astype(BF16)
    v_b = v.astype(BF16)

    zero_b = jnp.zeros_like(q_dec_b)
    q_stack = jnp.concatenate(
        [jnp.where((lane256 // GLA_DK) == h, q_dec_b, zero_b) for h in range(GLA_HEADS)], axis=0)
    scores = lax.dot_general(q_stack, k_inv, (((1,), (1,)), ((), ())),
                             preferred_element_type=F32)
    rt = lax.broadcasted_iota(jnp.int32, (TS, TS), 0)
    ct = lax.broadcasted_iota(jnp.int32, (TS, TS), 1)
    causal = ((rt // GLA_CHUNK) == (ct // GLA_CHUNK)) & (ct <= rt)
    o_heads = []
    for h in range(GLA_HEADS):
        p_h = jnp.where(causal, scores[h * TS:(h + 1) * TS, :], 0.0).astype(BF16)
        o_heads.append(_dot(p_h, v_b[:, h * GLA_DV:(h + 1) * GLA_DV]))

    sr = lax.broadcasted_iota(jnp.int32, (GLA_WIDTH, GLA_KDIM), 0) // GLA_DV
    sc = lax.broadcasted_iota(jnp.int32, (GLA_WIDTH, GLA_KDIM), 1) // GLA_DK
    bd_mask = sr == sc
    o_inter = []
    for c in range(n_gla):
        rows = slice(c * GLA_CHUNK, (c + 1) * GLA_CHUNK)
        st = st_ref[...]
        o_inter.append(lax.dot_general(q_dec_b[rows], st.astype(BF16), (((1,), (1,)), ((), ())),
                                       preferred_element_type=F32))
        upd = lax.dot_general(v_b[rows], k_dec[rows], (((0,), (0,)), ((), ())),
                              preferred_element_type=F32)
        decay = jnp.exp(bl[c * GLA_CHUNK:c * GLA_CHUNK + 1, :])
        st_ref[...] = st * decay + jnp.where(bd_mask, upd, 0.0)
    o_inter = jnp.concatenate(o_inter, axis=0)

    for h in range(GLA_HEADS):
        cols = slice(h * GLA_DV, (h + 1) * GLA_DV)
        o = o_heads[h] + o_inter[:, cols]
        o = o * lax.rsqrt(jnp.mean(o * o, axis=-1, keepdims=True) + RMS_EPS) * glan_ref[...]
        g = proj_ref[:, C_G + h * GLA_DV:C_G + (h + 1) * GLA_DV]
        out_ref[:, cols] = (o * (g * (1.0 / (1.0 + jnp.exp(-g))))).astype(out_ref.dtype)

    u = _gelu_tanh(proj_ref[:, C_U:C_U + GMLP_WIDTH])
    vg = _gelu_tanh(proj_ref[:, C_VG:C_VG + GMLP_WIDTH])
    hr = lax.broadcasted_iota(jnp.int32, (GMLP_WIDTH, GMLP_WIDTH), 0) // GMLP_DH
    hcn = lax.broadcasted_iota(jnp.int32, (GMLP_WIDTH, GMLP_WIDTH), 1) // GMLP_DH
    head_mean = jnp.where(hr == hcn, 1.0 / GMLP_DH, 0.0).astype(BF16)
    sq_hi, sq_lo = _split_bf16(vg * vg)
    ms = _dot(sq_hi, head_mean) + _dot(sq_lo, head_mean)
    v32 = vg * lax.rsqrt(ms + RMS_EPS) * gmn_ref[...]
    for c in range(n_gm):
        rows = slice(c * GMLP_CHUNK, (c + 1) * GMLP_CHUNK)
        vc = v32[rows].astype(BF16)
        zc = jnp.zeros_like(vc)
        rhs = jnp.concatenate(
            [jnp.where((lane256 // GMLP_DH) == h, vc, zc) for h in range(GMLP_HEADS)], axis=0)
        mixed = _dot(wm_ref[...], rhs) + bsp_ref[...]
        out_ref[rows, GLA_WIDTH:GLA_WIDTH + GMLP_WIDTH] = (u[rows] * mixed).astype(out_ref.dtype)

    hcv = proj_ref[:, C_CG:C_CG + CONV_WIDTH] * proj_ref[:, C_X:C_X + CONV_WIDTH]
    hc_ref[8:8 + TS, :] = hcv
    y = (wconv_ref[2:3, :] * hcv + wconv_ref[1:2, :] * hc_ref[7:7 + TS, :]
         + wconv_ref[0:1, :] * hc_ref[6:6 + TS, :])
    out_ref[:, GLA_WIDTH + GMLP_WIDTH:] = (
        proj_ref[:, C_BG:C_BG + CONV_WIDTH] * y).astype(out_ref.dtype)
    hc_ref[0:8, :] = hc_ref[TS:TS + 8, :]


def _mixers(proj, wgk, bgk, glan, gmn, wsp, bsp, wconv, batch, seq):
    n_seq = seq // TS_MIX
    full = lambda shape: pl.BlockSpec(shape, lambda b, i: (0,) * len(shape))
    return pl.pallas_call(
        _mixer_kernel,
        grid=(batch, n_seq),
        in_specs=[
            pl.BlockSpec((TS_MIX, D_PROJ), lambda b, i: (b * n_seq + i, 0)),
            full((LANES, GLA_KDIM)), full((1, GLA_KDIM)), full((1, GLA_DV)), full((1, GMLP_WIDTH)),
            full((GMLP_CHUNK, GMLP_HEADS * GMLP_CHUNK)), full((GMLP_CHUNK, GMLP_WIDTH)),
            full((8, CONV_WIDTH)),
        ],
        out_specs=pl.BlockSpec((TS_MIX, D_MODEL), lambda b, i: (b * n_seq + i, 0)),
        out_shape=jax.ShapeDtypeStruct((batch * seq, D_MODEL), BF16),
        scratch_shapes=[
            pltpu.VMEM((GLA_WIDTH, GLA_KDIM), F32),
            pltpu.VMEM((TS_MIX + 8, CONV_WIDTH), F32),
            pltpu.VMEM((2 * TS_MIX, TS_MIX), BF16),
            pltpu.VMEM((GMLP_CHUNK, GMLP_HEADS * GMLP_CHUNK), BF16),
        ],
        compiler_params=pltpu.CompilerParams(
            dimension_semantics=("arbitrary", "arbitrary"), vmem_limit_bytes=VMEM_LIMIT),
        name="mixers",
    )(proj, wgk, bgk, glan, gmn, wsp, bsp, wconv)


def _pack_bf16_pairs(h_b):
    lo = lax.bitcast_convert_type(h_b[:, :D_PACK].astype(F32), jnp.uint32) >> 16
    hi = lax.bitcast_convert_type(h_b[:, D_PACK:].astype(F32), jnp.uint32) & jnp.uint32(0xFFFF0000)
    return hi | lo


def _unpack_bf16_pairs(w):
    lo = lax.bitcast_convert_type(w << 16, F32).astype(BF16)
    hi = lax.bitcast_convert_type(w & jnp.uint32(0xFFFF0000), F32).astype(BF16)
    return lo, hi


def _out_router_kernel(mix_ref, x_ref, wo_ref, gain_ref, wrh_ref, wrl_ref, br_ref,
                       x2_ref, h2p_ref, route_ref, cnt_ref, tri_ref, wob_ref):
    @pl.when(pl.program_id(0) == 0)
    def _():
        cnt_ref[...] = jnp.zeros_like(cnt_ref)
        r = lax.broadcasted_iota(jnp.int32, (TM, TM), 0)
        c = lax.broadcasted_iota(jnp.int32, (TM, TM), 1)
        tri_ref[...] = jnp.where(c < r, 1.0, 0.0).astype(BF16)
        for r0 in range(0, D_MODEL, W_PREP_ROWS):
            wob_ref[r0:r0 + W_PREP_ROWS, :] = wo_ref[0, r0:r0 + W_PREP_ROWS, :].astype(BF16)

    x2 = x_ref[...] + _dot(mix_ref[...], wob_ref[...])
    x2_ref[...] = x2
    h = _rms(x2, gain_ref[...])
    h_hi, h_lo = _split_bf16(h)
    h2p_ref[...] = _pack_bf16_pairs(h_hi)
    lg = (_dot(h_hi, wrh_ref[...]) + _dot(h_lo, wrh_ref[...]) + _dot(h_hi, wrl_ref[...])
          + br_ref[...])

    lane = lax.broadcasted_iota(jnp.int32, (TM, LANES), 1).astype(F32)
    neg = -jnp.inf
    is_g = lane < N_GROUPS
    gl = jnp.where(is_g, lg, neg)
    gmax = jnp.max(gl, axis=1, keepdims=True)
    g_top = jnp.min(jnp.where(gl == gmax, lane, float(LANES)), axis=1, keepdims=True)
    g_w = 1.0 / jnp.sum(jnp.where(is_g, jnp.exp(lg - gmax), 0.0), axis=1, keepdims=True)
    first = N_GROUPS + EXPERTS_PER_GROUP * g_top
    el = jnp.where((lane >= first) & (lane < first + EXPERTS_PER_GROUP), lg, neg)
    m1 = jnp.max(el, axis=1, keepdims=True)
    i1 = jnp.min(jnp.where(el == m1, lane, float(LANES)), axis=1, keepdims=True)
    el2 = jnp.where(lane == i1, neg, el)
    m2 = jnp.max(el2, axis=1, keepdims=True)
    i2 = jnp.min(jnp.where(el2 == m2, lane, float(LANES)), axis=1, keepdims=True)
    ratio = jnp.exp(m2 - m1)
    w1 = g_w / (1.0 + ratio)
    w2 = w1 * ratio

    oh1 = jnp.where(lane == i1, 1.0, 0.0)
    oh2 = jnp.where(lane == i2, 1.0, 0.0)
    oh = oh1 + oh2
    before = _dot(tri_ref[...], oh.astype(BF16)) + cnt_ref[0:1, :]
    rank1 = jnp.sum(oh1 * before, axis=1, keepdims=True)
    rank2 = jnp.sum(oh2 * before, axis=1, keepdims=True)
    cnt_ref[...] = cnt_ref[...] + jnp.sum(oh, axis=0, keepdims=True)

    rec = jnp.zeros((TM, LANES), F32)
    for col, val in ((R_E, i1 - N_GROUPS), (R_E + 1, i2 - N_GROUPS), (R_RANK, rank1),
                     (R_RANK + 1, rank2), (R_W, w1), (R_W + 1, w2)):
        rec = jnp.where(lane == col, val, rec)
    route_ref[...] = rec


def _out_router(mixed, x, w_out, layer, gain, wr_hi, wr_lo, br):
    T = x.shape[0]
    row = lambda w: pl.BlockSpec((TM, w), lambda i: (i, 0))
    full = lambda shape: pl.BlockSpec(shape, lambda i: (0, 0))
    wo_spec = pl.BlockSpec((1, D_MODEL, D_MODEL), lambda i: (layer, 0, 0),
                           pipeline_mode=pl.Buffered(1))
    return pl.pallas_call(
        _out_router_kernel,
        grid=(T // TM,),
        in_specs=[row(D_MODEL), row(D_MODEL), wo_spec, full((1, D_MODEL)),
                  full((D_MODEL, ROUTER_COLS)), full((D_MODEL, ROUTER_COLS)), full((1, ROUTER_COLS))],
        out_specs=[row(D_MODEL), row(D_PACK), row(LANES), full((8, LANES))],
        out_shape=[jax.ShapeDtypeStruct((T, D_MODEL), F32),
                   jax.ShapeDtypeStruct((T, D_PACK), jnp.uint32),
                   jax.ShapeDtypeStruct((T, LANES), F32),
                   jax.ShapeDtypeStruct((8, LANES), F32)],
        scratch_shapes=[pltpu.VMEM((TM, TM), BF16), pltpu.VMEM((D_MODEL, D_MODEL), BF16)],
        compiler_params=pltpu.CompilerParams(
            dimension_semantics=("arbitrary",), vmem_limit_bytes=VMEM_LIMIT),
        name="out_router",
    )(mixed, x, w_out, gain, wr_hi, wr_lo, br)


def _dispatch_kernel(fill_ref, nu_ref, dest_ref, h_ref, xb_ref, zero_ref, sem_ref, zsem_ref):
    @pl.when(pl.program_id(0) == 0)
    def _():
        zero_ref[...] = jnp.zeros_like(zero_ref)
        fills = [(fill_ref[e] >= 0, pltpu.make_async_copy(
            zero_ref, xb_ref.at[pl.ds(pl.multiple_of(jnp.maximum(fill_ref[e], 0), MOE_BLK), MOE_BLK)],
            zsem_ref)) for e in range(N_EXPERTS)]
        n_blocks = xb_ref.shape[0] // MOE_BLK
        fills += [(j >= nu_ref[0], pltpu.make_async_copy(
            zero_ref, xb_ref.at[pl.ds(j * MOE_BLK, MOE_BLK)], zsem_ref))
            for j in range(n_blocks - N_EXPERTS, n_blocks)]
        for cond, f in fills:
            pl.when(cond)(f.start)
        for cond, f in fills:
            pl.when(cond)(f.wait)

    for r in range(TM):
        for k in range(TOP_K):
            pltpu.make_async_copy(h_ref.at[pl.ds(r, 1)],
                                  xb_ref.at[pl.ds(dest_ref[0, 0, k * TM + r], 1)],
                                  sem_ref).start(priority=k)
    for _ in range(TOP_K):
        pltpu.make_async_copy(h_ref, xb_ref.at[pl.ds(0, TM)], sem_ref).wait()


def _dispatch(fill_start, n_used, dest_tiles, h2p, n_rows):
    T = h2p.shape[0]
    grid_spec = pltpu.PrefetchScalarGridSpec(
        num_scalar_prefetch=2,
        grid=(T // TM,),
        in_specs=[
            pl.BlockSpec((1, 1, TOP_K * TM), lambda i, fs, nu: (i, 0, 0), memory_space=pltpu.SMEM),
            pl.BlockSpec((TM, D_PACK), lambda i, fs, nu: (i, 0)),
        ],
        out_specs=pl.BlockSpec(memory_space=pl.ANY),
        scratch_shapes=[pltpu.VMEM((MOE_BLK, D_PACK), jnp.uint32),
                        pltpu.SemaphoreType.DMA(()), pltpu.SemaphoreType.DMA(())],
    )
    return pl.pallas_call(
        _dispatch_kernel,
        grid_spec=grid_spec,
        out_shape=jax.ShapeDtypeStruct((n_rows, D_PACK), jnp.uint32),
        compiler_params=pltpu.CompilerParams(dimension_semantics=("arbitrary",)),
        name="dispatch",
    )(fill_start, n_used, dest_tiles, h2p)


def _expert_kernel(blk_ref, be_ref, nu_ref, x_ref, wg_ref, wu_ref, wd_ref, o_ref,
                   wgb_ref, wub_ref, wdb_ref):
    del blk_ref
    j = pl.program_id(0)
    used = j < nu_ref[0]

    @pl.when((j == 0) | (be_ref[j] != be_ref[jnp.maximum(j - 1, 0)]))
    def _():
        for r0 in range(0, D_MODEL, W_PREP_ROWS):
            rows = slice(r0, r0 + W_PREP_ROWS)
            wgb_ref[rows, :] = wg_ref[0, 0, rows, :].astype(BF16)
            wub_ref[rows, :] = wu_ref[0, 0, rows, :].astype(BF16)
        for r0 in range(0, D_EXPERT, W_PREP_ROWS):
            rows = slice(r0, r0 + W_PREP_ROWS)
            wdb_ref[rows, :] = wd_ref[0, 0, rows, :].astype(BF16)

    @pl.when(used)
    def _():
        x_lo, x_hi = _unpack_bf16_pairs(x_ref[...])
        g = _dot(x_lo, wgb_ref[:D_PACK, :]) + _dot(x_hi, wgb_ref[D_PACK:, :])
        u = _dot(x_lo, wub_ref[:D_PACK, :]) + _dot(x_hi, wub_ref[D_PACK:, :])
        h = (g * (1.0 / (1.0 + jnp.exp(-g)))) * u
        o_ref[...] = _dot(h.astype(BF16), wdb_ref[...])

    @pl.when(jnp.logical_not(used))
    def _():
        o_ref[...] = jnp.zeros_like(o_ref)


def _experts(blk_idx, blk_exp, n_used, xb, w_gate, w_up, w_down, layer):
    n_steps = blk_idx.shape[0]
    w_spec = lambda k, n: pl.BlockSpec((1, 1, k, n), lambda j, bi, be, nu: (layer, be[j], 0, 0))
    grid_spec = pltpu.PrefetchScalarGridSpec(
        num_scalar_prefetch=3,
        grid=(n_steps,),
        in_specs=[
            pl.BlockSpec((MOE_BLK, D_PACK), lambda j, bi, be, nu: (bi[j], 0)),
            w_spec(D_MODEL, D_EXPERT), w_spec(D_MODEL, D_EXPERT), w_spec(D_EXPERT, D_MODEL),
        ],
        out_specs=pl.BlockSpec((MOE_BLK, D_MODEL), lambda j, bi, be, nu: (j, 0)),
        scratch_shapes=[pltpu.VMEM((D_MODEL, D_EXPERT), BF16), pltpu.VMEM((D_MODEL, D_EXPERT), BF16),
                        pltpu.VMEM((D_EXPERT, D_MODEL), BF16)],
    )
    return pl.pallas_call(
        _expert_kernel,
        grid_spec=grid_spec,
        out_shape=jax.ShapeDtypeStruct((n_steps * MOE_BLK, D_MODEL), F32),
        compiler_params=pltpu.CompilerParams(
            dimension_semantics=("arbitrary",), vmem_limit_bytes=VMEM_LIMIT),
        name="experts",
    )(blk_idx, blk_exp, n_used, xb, w_gate, w_up, w_down)


def _dispatch_tables(route, counts_rec, T):
    counts = counts_rec[0, N_GROUPS:N_GROUPS + N_EXPERTS].astype(jnp.int32)
    n_steps = (T * TOP_K) // MOE_BLK + N_EXPERTS
    nblk = (counts + MOE_BLK - 1) // MOE_BLK
    bend = jnp.cumsum(nblk)
    pstart = (bend - nblk) * MOE_BLK
    n_used = bend[-1]
    j = jnp.minimum(jnp.arange(n_steps, dtype=jnp.int32), n_used - 1)
    blk_exp = jnp.minimum(jnp.sum(j[:, None] >= bend[None, :], axis=1), N_EXPERTS - 1)
    n_rows = n_steps * MOE_BLK
    last_blk = jnp.where(counts > 0, (bend - 1) * MOE_BLK, -1)
    e = route[:, R_E:R_E + TOP_K].astype(jnp.int32)
    rank = route[:, R_RANK:R_RANK + TOP_K].astype(jnp.int32)
    seg = jnp.sum(jnp.where(e[..., None] == jnp.arange(N_EXPERTS), pstart, 0), axis=-1)
    dest = jnp.clip(seg + rank, 0, n_steps * MOE_BLK - 1)
    dest_tiles = dest.reshape(T // TM, TM, TOP_K).transpose(0, 2, 1).reshape(T // TM, 1, TOP_K * TM)
    return dict(dest_tiles=dest_tiles, fill_start=last_blk.astype(jnp.int32),
                blk_idx=j, blk_exp=blk_exp.astype(jnp.int32),
                n_used=n_used.reshape(1).astype(jnp.int32), n_rows=n_rows)


def _prep_layer(l, w_gk_up, b_gk, gla_norm, gmlp_norm, w_spatial, b_spatial, w_conv,
                w_router_group, b_router_group, w_router_expert, b_router_expert):
    wgk = jnp.concatenate(
        [w_gk_up[l], jnp.zeros((LANES - GLA_GATE_RANK, GLA_KDIM), F32)], axis=0).astype(BF16)
    wsp = w_spatial[l].transpose(1, 0, 2).reshape(GMLP_CHUNK, GMLP_HEADS * GMLP_CHUNK)
    bsp = jnp.repeat(b_spatial[l].T, GMLP_DH, axis=1)
    wconv = jnp.concatenate([w_conv[l], jnp.zeros((8 - CONV_K, CONV_WIDTH), F32)], axis=0)
    wr = jnp.concatenate(
        [w_router_group[l], w_router_expert[l],
         jnp.zeros((D_MODEL, ROUTER_COLS - N_GROUPS - N_EXPERTS), F32)], axis=1)
    wr_hi = wr.astype(BF16)
    wr_lo = (wr - wr_hi.astype(F32)).astype(BF16)
    br = jnp.concatenate(
        [b_router_group[l], b_router_expert[l],
         jnp.zeros((ROUTER_COLS - N_GROUPS - N_EXPERTS,), F32)])[None, :]
    return dict(
        wgk=wgk, bgk=b_gk[l][None, :], glan=gla_norm[l][None, :], gmn=gmlp_norm[l][None, :],
        wsp=wsp, bsp=bsp, wconv=wconv, wr_hi=wr_hi, wr_lo=wr_lo, br=br)


def kernel(x, attn_norm, w_in, w_gk_up, b_gk, gla_norm, gmlp_norm, w_spatial, b_spatial, w_conv, w_out, ffn_norm, w_router_group, b_router_group, w_router_expert, b_router_expert, w_gate, w_up, w_down, final_norm):
    B, S, D = x.shape
    T = B * S
    depth = w_in.shape[0]
    xr = x.reshape(T, D)
    moe = None
    for l in range(depth):
        p = _prep_layer(l, w_gk_up, b_gk, gla_norm, gmlp_norm, w_spatial, b_spatial, w_conv,
                        w_router_group, b_router_group, w_router_expert, b_router_expert)
        if moe is None:
            proj = _norm_proj(xr, attn_norm[l][None, :], w_in, l)
        else:
            xr, proj = _combine_norm_proj(moe["dest_tiles"], moe["x2"], moe["route"], moe["yb"],
                                          attn_norm[l][None, :], w_in, l)
        mixed = _mixers(proj, p["wgk"], p["bgk"], p["glan"], p["gmn"], p["wsp"], p["bsp"],
                        p["wconv"], B, S)
        x2, h2p, route, counts_rec = _out_router(mixed, xr, w_out, l, ffn_norm[l][None, :],
                                                 p["wr_hi"], p["wr_lo"], p["br"])
        moe = _dispatch_tables(route, counts_rec, T)
        xb = _dispatch(moe["fill_start"], moe["n_used"], moe["dest_tiles"], h2p, moe["n_rows"])
        yb = _experts(moe["blk_idx"], moe["blk_exp"], moe["n_used"], xb, w_gate, w_up, w_down, l)
        moe.update(x2=x2, route=route, yb=yb)
    out = _combine_final_norm(moe["dest_tiles"], moe["x2"], moe["route"], moe["yb"],
                              final_norm[None, :])
    return out.reshape(B, S, D)
```

```python
import jax
import jax.numpy as jnp
from jax import lax
from jax.experimental import pallas as pl
from jax.experimental.pallas import tpu as pltpu

F32 = jnp.float32
BF16 = jnp.bfloat16

D_MODEL = 1024
RMS_EPS = 1e-6
GLA_HEADS = 4
GLA_WIDTH = 512
GLA_DV = 128
GLA_DK = 64
GLA_KDIM = 256
GLA_GATE_RANK = 16
GLA_GATE_NORM = 16.0
GLA_CHUNK = 64
GMLP_HEADS = 4
GMLP_WIDTH = 256
GMLP_DH = 64
GMLP_CHUNK = 128
CONV_WIDTH = 256
CONV_K = 3
N_GROUPS = 4
EXPERTS_PER_GROUP = 8
N_EXPERTS = 32
TOP_K = 2
D_EXPERT = 256

LANES = 128
C_Q, C_K, C_V, C_G = 0, 256, 512, 1024
C_U, C_VG, C_X, C_BG, C_CG, C_GKL = 1536, 1792, 2048, 2304, 2560, 2816
D_PROJ = C_GKL + LANES
D_IN = C_GKL + GLA_GATE_RANK

TM = 256
TS_MIX = 256
MOE_BLK = 256
ROUTER_COLS = LANES
D_PACK = D_MODEL // 2
VMEM_LIMIT = 56 * 1024 * 1024
R_E, R_RANK, R_W = 0, 2, 4


def _dot(a, b):
    return jnp.dot(a, b, preferred_element_type=F32)


def _split_bf16(x):
    hi = x.astype(BF16)
    lo = (x - hi.astype(F32)).astype(BF16)
    return hi, lo


def _rms(x, gain):
    return x * lax.rsqrt(jnp.mean(x * x, axis=-1, keepdims=True) + RMS_EPS) * gain


W_PREP_ROWS = 128
PROJ_CHUNK = 256


def _stage_w_in(w_ref, wb_ref):
    for r0 in range(0, D_MODEL, W_PREP_ROWS):
        rows = slice(r0, r0 + W_PREP_ROWS)
        wb_ref[rows, 0:C_U] = w_ref[0, rows, 0:C_U].astype(BF16)
        wb_ref[rows, C_U:C_GKL] = w_ref[0, rows, C_U + GLA_GATE_RANK:D_IN].astype(BF16)
        low = w_ref[0, rows, C_U:C_U + GLA_GATE_RANK]
        wb_ref[rows, C_GKL:D_PROJ] = jnp.concatenate(
            [low, jnp.zeros((W_PREP_ROWS, LANES - GLA_GATE_RANK), F32)], axis=1).astype(BF16)


def _w_in_specs(layer):
    return pl.BlockSpec((1, D_MODEL, D_IN), lambda i: (layer, 0, 0), pipeline_mode=pl.Buffered(1))


def _norm_proj_kernel(x_ref, gain_ref, w_ref, proj_ref, wb_ref):
    @pl.when(pl.program_id(0) == 0)
    def _():
        _stage_w_in(w_ref, wb_ref)
    proj_ref[...] = _dot(_rms(x_ref[...], gain_ref[...]).astype(BF16), wb_ref[...]).astype(BF16)


def _norm_proj(x, gain, w_in, layer):
    T = x.shape[0]
    return pl.pallas_call(
        _norm_proj_kernel,
        grid=(T // TM,),
        in_specs=[
            pl.BlockSpec((TM, D_MODEL), lambda i: (i, 0)),
            pl.BlockSpec((1, D_MODEL), lambda i: (0, 0)),
            _w_in_specs(layer),
        ],
        out_specs=pl.BlockSpec((TM, D_PROJ), lambda i: (i, 0)),
        out_shape=jax.ShapeDtypeStruct((T, D_PROJ), BF16),
        scratch_shapes=[pltpu.VMEM((D_MODEL, D_PROJ), BF16)],
        compiler_params=pltpu.CompilerParams(
            dimension_semantics=("arbitrary",), vmem_limit_bytes=VMEM_LIMIT),
        name="norm_proj",
    )(x, gain, w_in)


def _row_gather_copy(yb_ref, buf_ref, sem_ref, slot, k, r, d):
    return pltpu.make_async_copy(yb_ref.at[pl.ds(d, 1)], buf_ref.at[slot, k, pl.ds(r, 1)],
                                 sem_ref.at[slot])


def _gather_start(dest_ref, yb_ref, buf_ref, sem_ref, slot, rows=range(TM)):
    for r in rows:
        for k in range(TOP_K):
            _row_gather_copy(yb_ref, buf_ref, sem_ref, slot, k, r,
                             dest_ref[0, 0, k * TM + r]).start(priority=k)


def _gather_wait(yb_ref, buf_ref, sem_ref, slot):
    for k in range(TOP_K):
        pltpu.make_async_copy(yb_ref.at[pl.ds(0, TM)], buf_ref.at[slot, k], sem_ref.at[slot]).wait()


def _combined_residual(dcur_ref, x_ref, route_ref, yb_ref, buf_ref, sem_ref):
    i = pl.program_id(0)
    slot = lax.rem(i, 2)

    @pl.when(i == 0)
    def _():
        _gather_start(dcur_ref, yb_ref, buf_ref, sem_ref, 0)

    _gather_wait(yb_ref, buf_ref, sem_ref, slot)
    w0 = route_ref[:, R_W:R_W + 1]
    w1 = route_ref[:, R_W + 1:R_W + 2]
    return x_ref[...] + (w0 * buf_ref[slot, 0] + w1 * buf_ref[slot, 1])


def _prefetch_groups(n_groups):
    per = -(-TM // n_groups)
    return [range(g * per, min(TM, (g + 1) * per)) for g in range(n_groups)]


def _drain_last_prefetch(yb_ref, buf_ref, sem_ref):
    i = pl.program_id(0)

    @pl.when(i == pl.num_programs(0) - 1)
    def _():
        _gather_wait(yb_ref, buf_ref, sem_ref, 1 - lax.rem(i, 2))


def _combine_specs(n_tiles):
    smem_tile = lambda f: pl.BlockSpec((1, 1, TOP_K * TM), f, memory_space=pltpu.SMEM)
    return [
        smem_tile(lambda i: (i, 0, 0)),
        smem_tile(lambda i: (jnp.minimum(i + 1, n_tiles - 1), 0, 0)),
        pl.BlockSpec((TM, D_MODEL), lambda i: (i, 0)),
        pl.BlockSpec((TM, LANES), lambda i: (i, 0)),
        pl.BlockSpec(memory_space=pl.ANY),
    ]


_COMBINE_SCRATCH = [pltpu.VMEM((2, TOP_K, TM, D_MODEL), F32), pltpu.SemaphoreType.DMA((2,))]


def _combine_norm_proj_kernel(dcur_ref, dnxt_ref, x_ref, route_ref, yb_ref, gain_ref, w_ref,
                              xo_ref, proj_ref, buf_ref, sem_ref, wb_ref):
    @pl.when(pl.program_id(0) == 0)
    def _():
        _stage_w_in(w_ref, wb_ref)
    x = _combined_residual(dcur_ref, x_ref, route_ref, yb_ref, buf_ref, sem_ref)
    xo_ref[...] = x
    h = _rms(x, gain_ref[...]).astype(BF16)
    nxt = 1 - lax.rem(pl.program_id(0), 2)
    col_chunks = [(c0, min(c0 + PROJ_CHUNK, D_PROJ)) for c0 in range(0, D_PROJ, PROJ_CHUNK)]
    for rows, (c0, c1) in zip(_prefetch_groups(len(col_chunks)), col_chunks):
        _gather_start(dnxt_ref, yb_ref, buf_ref, sem_ref, nxt, rows)
        proj_ref[:, c0:c1] = _dot(h, wb_ref[:, c0:c1]).astype(BF16)
    _drain_last_prefetch(yb_ref, buf_ref, sem_ref)


def _combine_norm_proj(dest_tiles, x2, route, yb, gain, w_in, layer):
    T = x2.shape[0]
    n_tiles = T // TM
    return pl.pallas_call(
        _combine_norm_proj_kernel,
        grid=(n_tiles,),
        in_specs=_combine_specs(n_tiles) + [
            pl.BlockSpec((1, D_MODEL), lambda i: (0, 0)),
            _w_in_specs(layer),
        ],
        out_specs=[pl.BlockSpec((TM, D_MODEL), lambda i: (i, 0)),
                   pl.BlockSpec((TM, D_PROJ), lambda i: (i, 0))],
        out_shape=[jax.ShapeDtypeStruct((T, D_MODEL), F32),
                   jax.ShapeDtypeStruct((T, D_PROJ), BF16)],
        scratch_shapes=_COMBINE_SCRATCH + [pltpu.VMEM((D_MODEL, D_PROJ), BF16)],
        compiler_params=pltpu.CompilerParams(
            dimension_semantics=("arbitrary",), vmem_limit_bytes=VMEM_LIMIT),
        name="combine_norm_proj",
    )(dest_tiles, dest_tiles, x2, route, yb, gain, w_in)


def _combine_final_norm_kernel(dcur_ref, dnxt_ref, x_ref, route_ref, yb_ref, gain_ref,
                               o_ref, buf_ref, sem_ref):
    x = _combined_residual(dcur_ref, x_ref, route_ref, yb_ref, buf_ref, sem_ref)
    o_ref[...] = _rms(x, gain_ref[...])
    _gather_start(dnxt_ref, yb_ref, buf_ref, sem_ref, 1 - lax.rem(pl.program_id(0), 2))
    _drain_last_prefetch(yb_ref, buf_ref, sem_ref)


def _combine_final_norm(dest_tiles, x2, route, yb, gain):
    T = x2.shape[0]
    n_tiles = T // TM
    return pl.pallas_call(
        _combine_final_norm_kernel,
        grid=(n_tiles,),
        in_specs=_combine_specs(n_tiles) + [pl.BlockSpec((1, D_MODEL), lambda i: (0, 0))],
        out_specs=pl.BlockSpec((TM, D_MODEL), lambda i: (i, 0)),
        out_shape=jax.ShapeDtypeStruct((T, D_MODEL), F32),
        scratch_shapes=_COMBINE_SCRATCH,
        compiler_params=pltpu.CompilerParams(
            dimension_semantics=("arbitrary",), vmem_limit_bytes=VMEM_LIMIT),
        name="combine_final_norm",
    )(dest_tiles, dest_tiles, x2, route, yb, gain)


def _gelu_tanh(x):
    c = 0.7978845608028654
    return x * (0.5 * (1.0 + jnp.tanh(c * (x + 0.044715 * (x * x * x)))))


def _mixer_kernel(proj_ref, wgk_ref, bgk_ref, glan_ref, gmn_ref, wsp_ref, bsp_ref, wconv_ref,
                  out_ref, st_ref, hc_ref, lcat_ref, wm_ref):
    TS = TS_MIX
    n_gla = TS // GLA_CHUNK
    n_gm = TS // GMLP_CHUNK

    @pl.when(pl.program_id(1) == 0)
    def _():
        st_ref[...] = jnp.zeros_like(st_ref)
        hc_ref[...] = jnp.zeros_like(hc_ref)
        r = lax.broadcasted_iota(jnp.int32, (2 * TS, TS), 0)
        c = lax.broadcasted_iota(jnp.int32, (2 * TS, TS), 1)
        rr = jnp.where(r >= TS, r - TS, r)
        same = (rr // GLA_CHUNK) == (c // GLA_CHUNK)
        keep = same & ((r >= TS) | (c <= rr))
        lcat_ref[...] = jnp.where(keep, 1.0, 0.0).astype(BF16)
        t = lax.broadcasted_iota(jnp.int32, (GMLP_CHUNK, GMLP_HEADS * GMLP_CHUNK), 0)
        s = lax.broadcasted_iota(jnp.int32, (GMLP_CHUNK, GMLP_HEADS * GMLP_CHUNK), 1) % GMLP_CHUNK
        wm_ref[...] = jnp.where(s <= t, wsp_ref[...], 0.0).astype(BF16)

    lane256 = lax.broadcasted_iota(jnp.int32, (1, GLA_KDIM), 1)

    q = proj_ref[:, C_Q:C_Q + GLA_KDIM].astype(F32)
    k = proj_ref[:, C_K:C_K + GLA_KDIM].astype(F32)
    v_b = proj_ref[:, C_V:C_V + GLA_WIDTH]
    z = _dot(proj_ref[:, C_GKL:C_GKL + LANES], wgk_ref[...]) + bgk_ref[...]
    gk = (jnp.minimum(z, 0.0) - jnp.log1p(jnp.exp(-jnp.abs(z)))) * (1.0 / GLA_GATE_NORM)
    gk_hi, gk_lo = _split_bf16(gk)
    cs = _dot(lcat_ref[...], jnp.concatenate([gk_hi, gk_lo], axis=1))
    b = cs[:TS, :GLA_KDIM] + cs[:TS, GLA_KDIM:]
    bl = cs[TS:, :GLA_KDIM] + cs[TS:, GLA_KDIM:]
    q_dec = (q * (GLA_DK ** -0.5)) * jnp.exp(b)
    k_inv = (k * jnp.exp(-b)).astype(BF16)
    k_dec = (k * jnp.exp(bl - b)).astype(BF16)
    q_dec_b = q_dec.astype(BF16)

    zero_b = jnp.zeros_like(q_dec_b)
    q_stack = jnp.concatenate(
        [jnp.where((lane256 // GLA_DK) == h, q_dec_b, zero_b) for h in range(GLA_HEADS)], axis=0)
    scores = lax.dot_general(q_stack, k_inv, (((1,), (1,)), ((), ())),
                             preferred_element_type=F32)
    rt = lax.broadcasted_iota(jnp.int32, (TS, TS), 0)
    ct = lax.broadcasted_iota(jnp.int32, (TS, TS), 1)
    causal = ((rt // GLA_CHUNK) == (ct // GLA_CHUNK)) & (ct <= rt)
    o_heads = []
    for h in range(GLA_HEADS):
        p_h = jnp.where(causal, scores[h * TS:(h + 1) * TS, :], 0.0).astype(BF16)
        o_heads.append(_dot(p_h, v_b[:, h * GLA_DV:(h + 1) * GLA_DV]))

    sr = lax.broadcasted_iota(jnp.int32, (GLA_WIDTH, GLA_KDIM), 0) // GLA_DV
    sc = lax.broadcasted_iota(jnp.int32, (GLA_WIDTH, GLA_KDIM), 1) // GLA_DK
    bd_mask = sr == sc
    o_inter = []
    for c in range(n_gla):
        rows = slice(c * GLA_CHUNK, (c + 1) * GLA_CHUNK)
        st = st_ref[...]
        o_inter.append(lax.dot_general(q_dec_b[rows], st.astype(BF16), (((1,), (1,)), ((), ())),
                                       preferred_element_type=F32))
        upd = lax.dot_general(v_b[rows], k_dec[rows], (((0,), (0,)), ((), ())),
                              preferred_element_type=F32)
        decay = jnp.exp(bl[c * GLA_CHUNK:c * GLA_CHUNK + 1, :])
        st_ref[...] = st * decay + jnp.where(bd_mask, upd, 0.0)
    o_inter = jnp.concatenate(o_inter, axis=0)

    for h in range(GLA_HEADS):
        cols = slice(h * GLA_DV, (h + 1) * GLA_DV)
        o = o_heads[h] + o_inter[:, cols]
        o = o * lax.rsqrt(jnp.mean(o * o, axis=-1, keepdims=True) + RMS_EPS) * glan_ref[...]
        g = proj_ref[:, C_G + h * GLA_DV:C_G + (h + 1) * GLA_DV].astype(F32)
        out_ref[:, cols] = (o * (g * (1.0 / (1.0 + jnp.exp(-g))))).astype(out_ref.dtype)

    u = _gelu_tanh(proj_ref[:, C_U:C_U + GMLP_WIDTH].astype(F32))
    vg = _gelu_tanh(proj_ref[:, C_VG:C_VG + GMLP_WIDTH].astype(F32))
    hr = lax.broadcasted_iota(jnp.int32, (GMLP_WIDTH, GMLP_WIDTH), 0) // GMLP_DH
    hcn = lax.broadcasted_iota(jnp.int32, (GMLP_WIDTH, GMLP_WIDTH), 1) // GMLP_DH
    head_mean = jnp.where(hr == hcn, 1.0 / GMLP_DH, 0.0).astype(BF16)
    sq_hi, sq_lo = _split_bf16(vg * vg)
    ms = _dot(sq_hi, head_mean) + _dot(sq_lo, head_mean)
    v32 = vg * lax.rsqrt(ms + RMS_EPS) * gmn_ref[...]
    for c in range(n_gm):
        rows = slice(c * GMLP_CHUNK, (c + 1) * GMLP_CHUNK)
        vc = v32[rows].astype(BF16)
        zc = jnp.zeros_like(vc)
        rhs = jnp.concatenate(
            [jnp.where((lane256 // GMLP_DH) == h, vc, zc) for h in range(GMLP_HEADS)], axis=0)
        mixed = _dot(wm_ref[...], rhs) + bsp_ref[...]
        out_ref[rows, GLA_WIDTH:GLA_WIDTH + GMLP_WIDTH] = (u[rows] * mixed).astype(out_ref.dtype)

    hcv = (proj_ref[:, C_CG:C_CG + CONV_WIDTH].astype(F32)
           * proj_ref[:, C_X:C_X + CONV_WIDTH].astype(F32))
    hc_ref[8:8 + TS, :] = hcv
    y = (wconv_ref[2:3, :] * hcv + wconv_ref[1:2, :] * hc_ref[7:7 + TS, :]
         + wconv_ref[0:1, :] * hc_ref[6:6 + TS, :])
    out_ref[:, GLA_WIDTH + GMLP_WIDTH:] = (
        proj_ref[:, C_BG:C_BG + CONV_WIDTH].astype(F32) * y).astype(out_ref.dtype)
    hc_ref[0:8, :] = hc_ref[TS:TS + 8, :]


def _mixers(proj, wgk, bgk, glan, gmn, wsp, bsp, wconv, batch, seq):
    n_seq = seq // TS_MIX
    full = lambda shape: pl.BlockSpec(shape, lambda b, i: (0,) * len(shape))
    return pl.pallas_call(
        _mixer_kernel,
        grid=(batch, n_seq),
        in_specs=[
            pl.BlockSpec((TS_MIX, D_PROJ), lambda b, i: (b * n_seq + i, 0)),
            full((LANES, GLA_KDIM)), full((1, GLA_KDIM)), full((1, GLA_DV)), full((1, GMLP_WIDTH)),
            full((GMLP_CHUNK, GMLP_HEADS * GMLP_CHUNK)), full((GMLP_CHUNK, GMLP_WIDTH)),
            full((8, CONV_WIDTH)),
        ],
        out_specs=pl.BlockSpec((TS_MIX, D_MODEL), lambda b, i: (b * n_seq + i, 0)),
        out_shape=jax.ShapeDtypeStruct((batch * seq, D_MODEL), BF16),
        scratch_shapes=[
            pltpu.VMEM((GLA_WIDTH, GLA_KDIM), F32),
            pltpu.VMEM((TS_MIX + 8, CONV_WIDTH), F32),
            pltpu.VMEM((2 * TS_MIX, TS_MIX), BF16),
            pltpu.VMEM((GMLP_CHUNK, GMLP_HEADS * GMLP_CHUNK), BF16),
        ],
        compiler_params=pltpu.CompilerParams(
            dimension_semantics=("arbitrary", "arbitrary"), vmem_limit_bytes=VMEM_LIMIT),
        name="mixers",
    )(proj, wgk, bgk, glan, gmn, wsp, bsp, wconv)


def _pack_bf16_pairs(h_b):
    lo = lax.bitcast_convert_type(h_b[:, :D_PACK].astype(F32), jnp.uint32) >> 16
    hi = lax.bitcast_convert_type(h_b[:, D_PACK:].astype(F32), jnp.uint32) & jnp.uint32(0xFFFF0000)
    return hi | lo


def _unpack_bf16_pairs(w):
    lo = lax.bitcast_convert_type(w << 16, F32).astype(BF16)
    hi = lax.bitcast_convert_type(w & jnp.uint32(0xFFFF0000), F32).astype(BF16)
    return lo, hi


def _out_router_kernel(mix_ref, x_ref, wo_ref, gain_ref, wrh_ref, wrl_ref, br_ref,
                       x2_ref, h2p_ref, route_ref, cnt_ref, tri_ref, wob_ref):
    @pl.when(pl.program_id(0) == 0)
    def _():
        cnt_ref[...] = jnp.zeros_like(cnt_ref)
        r = lax.broadcasted_iota(jnp.int32, (TM, TM), 0)
        c = lax.broadcasted_iota(jnp.int32, (TM, TM), 1)
        tri_ref[...] = jnp.where(c < r, 1.0, 0.0).astype(BF16)
        for r0 in range(0, D_MODEL, W_PREP_ROWS):
            wob_ref[r0:r0 + W_PREP_ROWS, :] = wo_ref[0, r0:r0 + W_PREP_ROWS, :].astype(BF16)

    x2 = x_ref[...] + _dot(mix_ref[...], wob_ref[...])
    x2_ref[...] = x2
    h = _rms(x2, gain_ref[...])
    h_hi, h_lo = _split_bf16(h)
    h2p_ref[...] = _pack_bf16_pairs(h_hi)
    lg = (_dot(h_hi, wrh_ref[...]) + _dot(h_lo, wrh_ref[...]) + _dot(h_hi, wrl_ref[...])
          + br_ref[...])

    lane = lax.broadcasted_iota(jnp.int32, (TM, LANES), 1).astype(F32)
    neg = -jnp.inf
    is_g = lane < N_GROUPS
    gl = jnp.where(is_g, lg, neg)
    gmax = jnp.max(gl, axis=1, keepdims=True)
    g_top = jnp.min(jnp.where(gl == gmax, lane, float(LANES)), axis=1, keepdims=True)
    g_w = 1.0 / jnp.sum(jnp.where(is_g, jnp.exp(lg - gmax), 0.0), axis=1, keepdims=True)
    first = N_GROUPS + EXPERTS_PER_GROUP * g_top
    el = jnp.where((lane >= first) & (lane < first + EXPERTS_PER_GROUP), lg, neg)
    m1 = jnp.max(el, axis=1, keepdims=True)
    i1 = jnp.min(jnp.where(el == m1, lane, float(LANES)), axis=1, keepdims=True)
    el2 = jnp.where(lane == i1, neg, el)
    m2 = jnp.max(el2, axis=1, keepdims=True)
    i2 = jnp.min(jnp.where(el2 == m2, lane, float(LANES)), axis=1, keepdims=True)
    ratio = jnp.exp(m2 - m1)
    w1 = g_w / (1.0 + ratio)
    w2 = w1 * ratio

    oh1 = jnp.where(lane == i1, 1.0, 0.0)
    oh2 = jnp.where(lane == i2, 1.0, 0.0)
    oh = oh1 + oh2
    before = _dot(tri_ref[...], oh.astype(BF16)) + cnt_ref[0:1, :]
    rank1 = jnp.sum(oh1 * before, axis=1, keepdims=True)
    rank2 = jnp.sum(oh2 * before, axis=1, keepdims=True)
    cnt_ref[...] = cnt_ref[...] + jnp.sum(oh, axis=0, keepdims=True)

    rec = jnp.zeros((TM, LANES), F32)
    for col, val in ((R_E, i1 - N_GROUPS), (R_E + 1, i2 - N_GROUPS), (R_RANK, rank1),
                     (R_RANK + 1, rank2), (R_W, w1), (R_W + 1, w2)):
        rec = jnp.where(lane == col, val, rec)
    route_ref[...] = rec


def _out_router(mixed, x, w_out, layer, gain, wr_hi, wr_lo, br):
    T = x.shape[0]
    row = lambda w: pl.BlockSpec((TM, w), lambda i: (i, 0))
    full = lambda shape: pl.BlockSpec(shape, lambda i: (0, 0))
    wo_spec = pl.BlockSpec((1, D_MODEL, D_MODEL), lambda i: (layer, 0, 0),
                           pipeline_mode=pl.Buffered(1))
    return pl.pallas_call(
        _out_router_kernel,
        grid=(T // TM,),
        in_specs=[row(D_MODEL), row(D_MODEL), wo_spec, full((1, D_MODEL)),
                  full((D_MODEL, ROUTER_COLS)), full((D_MODEL, ROUTER_COLS)), full((1, ROUTER_COLS))],
        out_specs=[row(D_MODEL), row(D_PACK), row(LANES), full((8, LANES))],
        out_shape=[jax.ShapeDtypeStruct((T, D_MODEL), F32),
                   jax.ShapeDtypeStruct((T, D_PACK), jnp.uint32),
                   jax.ShapeDtypeStruct((T, LANES), F32),
                   jax.ShapeDtypeStruct((8, LANES), F32)],
        scratch_shapes=[pltpu.VMEM((TM, TM), BF16), pltpu.VMEM((D_MODEL, D_MODEL), BF16)],
        compiler_params=pltpu.CompilerParams(
            dimension_semantics=("arbitrary",), vmem_limit_bytes=VMEM_LIMIT),
        name="out_router",
    )(mixed, x, w_out, gain, wr_hi, wr_lo, br)


def _dispatch_kernel(fill_ref, nu_ref, dest_ref, h_ref, xb_ref, zero_ref, sem_ref, zsem_ref,
                     stage_ref):
    i = pl.program_id(0)
    par = lax.rem(i, 2)

    @pl.when(i == 0)
    def _():
        zero_ref[...] = jnp.zeros_like(zero_ref)
        fills = [(fill_ref[e] >= 0, pltpu.make_async_copy(
            zero_ref, xb_ref.at[pl.ds(pl.multiple_of(jnp.maximum(fill_ref[e], 0), MOE_BLK), MOE_BLK)],
            zsem_ref)) for e in range(N_EXPERTS)]
        n_blocks = xb_ref.shape[0] // MOE_BLK
        fills += [(j >= nu_ref[0], pltpu.make_async_copy(
            zero_ref, xb_ref.at[pl.ds(j * MOE_BLK, MOE_BLK)], zsem_ref))
            for j in range(n_blocks - N_EXPERTS, n_blocks)]
        for cond, f in fills:
            pl.when(cond)(f.start)
        for cond, f in fills:
            pl.when(cond)(f.wait)

    stage_ref[par] = h_ref[...]
    for r in range(TM):
        for k in range(TOP_K):
            pltpu.make_async_copy(stage_ref.at[par, pl.ds(r, 1)],
                                  xb_ref.at[pl.ds(dest_ref[0, 0, k * TM + r], 1)],
                                  sem_ref.at[par]).start(priority=k)

    def wait_tile(p):
        for _ in range(TOP_K):
            pltpu.make_async_copy(stage_ref.at[p], xb_ref.at[pl.ds(0, TM)], sem_ref.at[p]).wait()

    pl.when(i > 0)(lambda: wait_tile(1 - par))
    pl.when(i == pl.num_programs(0) - 1)(lambda: wait_tile(par))


def _dispatch(fill_start, n_used, dest_tiles, h2p, n_rows):
    T = h2p.shape[0]
    grid_spec = pltpu.PrefetchScalarGridSpec(
        num_scalar_prefetch=2,
        grid=(T // TM,),
        in_specs=[
            pl.BlockSpec((1, 1, TOP_K * TM), lambda i, fs, nu: (i, 0, 0), memory_space=pltpu.SMEM),
            pl.BlockSpec((TM, D_PACK), lambda i, fs, nu: (i, 0)),
        ],
        out_specs=pl.BlockSpec(memory_space=pl.ANY),
        scratch_shapes=[pltpu.VMEM((MOE_BLK, D_PACK), jnp.uint32),
                        pltpu.SemaphoreType.DMA((2,)), pltpu.SemaphoreType.DMA(()),
                        pltpu.VMEM((2, TM, D_PACK), jnp.uint32)],
    )
    return pl.pallas_call(
        _dispatch_kernel,
        grid_spec=grid_spec,
        out_shape=jax.ShapeDtypeStruct((n_rows, D_PACK), jnp.uint32),
        compiler_params=pltpu.CompilerParams(dimension_semantics=("arbitrary",)),
        name="dispatch",
    )(fill_start, n_used, dest_tiles, h2p)


def _expert_kernel(blk_ref, be_ref, nu_ref, x_ref, wg_ref, wu_ref, wd_ref, o_ref,
                   wgb_ref, wub_ref, wdb_ref):
    del blk_ref
    j = pl.program_id(0)
    used = j < nu_ref[0]

    @pl.when((j == 0) | (be_ref[j] != be_ref[jnp.maximum(j - 1, 0)]))
    def _():
        for r0 in range(0, D_MODEL, W_PREP_ROWS):
            rows = slice(r0, r0 + W_PREP_ROWS)
            wgb_ref[rows, :] = wg_ref[0, 0, rows, :].astype(BF16)
            wub_ref[rows, :] = wu_ref[0, 0, rows, :].astype(BF16)
        for r0 in range(0, D_EXPERT, W_PREP_ROWS):
            rows = slice(r0, r0 + W_PREP_ROWS)
            wdb_ref[rows, :] = wd_ref[0, 0, rows, :].astype(BF16)

    @pl.when(used)
    def _():
        x_lo, x_hi = _unpack_bf16_pairs(x_ref[...])
        g = _dot(x_lo, wgb_ref[:D_PACK, :]) + _dot(x_hi, wgb_ref[D_PACK:, :])
        u = _dot(x_lo, wub_ref[:D_PACK, :]) + _dot(x_hi, wub_ref[D_PACK:, :])
        h = (g * (1.0 / (1.0 + jnp.exp(-g)))) * u
        o_ref[...] = _dot(h.astype(BF16), wdb_ref[...])

    @pl.when(jnp.logical_not(used))
    def _():
        o_ref[...] = jnp.zeros_like(o_ref)


def _experts(blk_idx, blk_exp, n_used, xb, w_gate, w_up, w_down, layer):
    n_steps = blk_idx.shape[0]
    w_spec = lambda k, n: pl.BlockSpec((1, 1, k, n), lambda j, bi, be, nu: (layer, be[j], 0, 0))
    grid_spec = pltpu.PrefetchScalarGridSpec(
        num_scalar_prefetch=3,
        grid=(n_steps,),
        in_specs=[
            pl.BlockSpec((MOE_BLK, D_PACK), lambda j, bi, be, nu: (bi[j], 0)),
            w_spec(D_MODEL, D_EXPERT), w_spec(D_MODEL, D_EXPERT), w_spec(D_EXPERT, D_MODEL),
        ],
        out_specs=pl.BlockSpec((MOE_BLK, D_MODEL), lambda j, bi, be, nu: (j, 0)),
        scratch_shapes=[pltpu.VMEM((D_MODEL, D_EXPERT), BF16), pltpu.VMEM((D_MODEL, D_EXPERT), BF16),
                        pltpu.VMEM((D_EXPERT, D_MODEL), BF16)],
    )
    return pl.pallas_call(
        _expert_kernel,
        grid_spec=grid_spec,
        out_shape=jax.ShapeDtypeStruct((n_steps * MOE_BLK, D_MODEL), F32),
        compiler_params=pltpu.CompilerParams(
            dimension_semantics=("arbitrary",), vmem_limit_bytes=VMEM_LIMIT),
        name="experts",
    )(blk_idx, blk_exp, n_used, xb, w_gate, w_up, w_down)


def _dispatch_tables(route, counts_rec, T):
    counts = counts_rec[0, N_GROUPS:N_GROUPS + N_EXPERTS].astype(jnp.int32)
    n_steps = (T * TOP_K) // MOE_BLK + N_EXPERTS
    nblk = (counts + MOE_BLK - 1) // MOE_BLK
    bend = jnp.cumsum(nblk)
    pstart = (bend - nblk) * MOE_BLK
    n_used = bend[-1]
    j = jnp.minimum(jnp.arange(n_steps, dtype=jnp.int32), n_used - 1)
    blk_exp = jnp.minimum(jnp.sum(j[:, None] >= bend[None, :], axis=1), N_EXPERTS - 1)
    n_rows = n_steps * MOE_BLK
    last_blk = jnp.where(counts > 0, (bend - 1) * MOE_BLK, -1)
    e = route[:, R_E:R_E + TOP_K].astype(jnp.int32)
    rank = route[:, R_RANK:R_RANK + TOP_K].astype(jnp.int32)
    seg = jnp.sum(jnp.where(e[..., None] == jnp.arange(N_EXPERTS), pstart, 0), axis=-1)
    dest = jnp.clip(seg + rank, 0, n_steps * MOE_BLK - 1)
    dest_tiles = dest.reshape(T // TM, TM, TOP_K).transpose(0, 2, 1).reshape(T // TM, 1, TOP_K * TM)
    return dict(dest_tiles=dest_tiles, fill_start=last_blk.astype(jnp.int32),
                blk_idx=j, blk_exp=blk_exp.astype(jnp.int32),
                n_used=n_used.reshape(1).astype(jnp.int32), n_rows=n_rows)


def _prep_layer(l, w_gk_up, b_gk, gla_norm, gmlp_norm, w_spatial, b_spatial, w_conv,
                w_router_group, b_router_group, w_router_expert, b_router_expert):
    wgk = jnp.concatenate(
        [w_gk_up[l], jnp.zeros((LANES - GLA_GATE_RANK, GLA_KDIM), F32)], axis=0).astype(BF16)
    wsp = w_spatial[l].transpose(1, 0, 2).reshape(GMLP_CHUNK, GMLP_HEADS * GMLP_CHUNK)
    bsp = jnp.repeat(b_spatial[l].T, GMLP_DH, axis=1)
    wconv = jnp.concatenate([w_conv[l], jnp.zeros((8 - CONV_K, CONV_WIDTH), F32)], axis=0)
    wr = jnp.concatenate(
        [w_router_group[l], w_router_expert[l],
         jnp.zeros((D_MODEL, ROUTER_COLS - N_GROUPS - N_EXPERTS), F32)], axis=1)
    wr_hi = wr.astype(BF16)
    wr_lo = (wr - wr_hi.astype(F32)).astype(BF16)
    br = jnp.concatenate(
        [b_router_group[l], b_router_expert[l],
         jnp.zeros((ROUTER_COLS - N_GROUPS - N_EXPERTS,), F32)])[None, :]
    return dict(
        wgk=wgk, bgk=b_gk[l][None, :], glan=gla_norm[l][None, :], gmn=gmlp_norm[l][None, :],
        wsp=wsp, bsp=bsp, wconv=wconv, wr_hi=wr_hi, wr_lo=wr_lo, br=br)


def kernel(x, attn_norm, w_in, w_gk_up, b_gk, gla_norm, gmlp_norm, w_spatial, b_spatial, w_conv, w_out, ffn_norm, w_router_group, b_router_group, w_router_expert, b_router_expert, w_gate, w_up, w_down, final_norm):
    B, S, D = x.shape
    T = B * S
    depth = w_in.shape[0]
    xr = x.reshape(T, D)
    moe = None
    for l in range(depth):
        p = _prep_layer(l, w_gk_up, b_gk, gla_norm, gmlp_norm, w_spatial, b_spatial, w_conv,
                        w_router_group, b_router_group, w_router_expert, b_router_expert)
        if moe is None:
            proj = _norm_proj(xr, attn_norm[l][None, :], w_in, l)
        else:
            xr, proj = _combine_norm_proj(moe["dest_tiles"], moe["x2"], moe["route"], moe["yb"],
                                          attn_norm[l][None, :], w_in, l)
        mixed = _mixers(proj, p["wgk"], p["bgk"], p["glan"], p["gmn"], p["wsp"], p["bsp"],
                        p["wconv"], B, S)
        x2, h2p, route, counts_rec = _out_router(mixed, xr, w_out, l, ffn_norm[l][None, :],
                                                 p["wr_hi"], p["wr_lo"], p["br"])
        moe = _dispatch_tables(route, counts_rec, T)
        xb = _dispatch(moe["fill_start"], moe["n_used"], moe["dest_tiles"], h2p, moe["n_rows"])
        yb = _experts(moe["blk_idx"], moe["blk_exp"], moe["n_used"], xb, w_gate, w_up, w_down, l)
        moe.update(x2=x2, route=route, yb=yb)
    out = _combine_final_norm(moe["dest_tiles"], moe["x2"], moe["route"], moe["yb"],
                              final_norm[None, :])
    return out.reshape(B, S, D)
```

```python
import jax
import jax.numpy as jnp
from jax import lax
from jax.experimental import pallas as pl
from jax.experimental.pallas import tpu as pltpu

F32 = jnp.float32
BF16 = jnp.bfloat16

D_MODEL = 1024
RMS_EPS = 1e-6
GLA_HEADS = 4
GLA_WIDTH = 512
GLA_DV = 128
GLA_DK = 64
GLA_KDIM = 256
GLA_GATE_RANK = 16
GLA_GATE_NORM = 16.0
GLA_CHUNK = 64
GMLP_HEADS = 4
GMLP_WIDTH = 256
GMLP_DH = 64
GMLP_CHUNK = 128
CONV_WIDTH = 256
CONV_K = 3
N_GROUPS = 4
EXPERTS_PER_GROUP = 8
N_EXPERTS = 32
TOP_K = 2
D_EXPERT = 256

LANES = 128
C_Q, C_K, C_V, C_G = 0, 256, 512, 1024
C_U, C_VG, C_X, C_BG, C_CG, C_GKL = 1536, 1792, 2048, 2304, 2560, 2816
D_PROJ = C_GKL + LANES
D_IN = C_GKL + GLA_GATE_RANK

TM = 256
TS_MIX = 256
MOE_BLK = 256
ROUTER_COLS = LANES
SUBLANES = 8
ROW_TILES = D_MODEL // LANES
assert ROW_TILES == SUBLANES
VMEM_LIMIT = 56 * 1024 * 1024
R_E, R_RANK, R_W = 0, 2, 4


def _dot(a, b):
    return jnp.dot(a, b, preferred_element_type=F32)


def _split_bf16(x):
    hi = x.astype(BF16)
    lo = (x - hi.astype(F32)).astype(BF16)
    return hi, lo


def _rms(x, gain):
    return x * lax.rsqrt(jnp.mean(x * x, axis=-1, keepdims=True) + RMS_EPS) * gain


W_PREP_ROWS = 128
PROJ_CHUNK = 256


def _stage_w_in(w_ref, wb_ref):
    for r0 in range(0, D_MODEL, W_PREP_ROWS):
        rows = slice(r0, r0 + W_PREP_ROWS)
        wb_ref[rows, 0:C_U] = w_ref[0, rows, 0:C_U].astype(BF16)
        wb_ref[rows, C_U:C_GKL] = w_ref[0, rows, C_U + GLA_GATE_RANK:D_IN].astype(BF16)
        low = w_ref[0, rows, C_U:C_U + GLA_GATE_RANK]
        wb_ref[rows, C_GKL:D_PROJ] = jnp.concatenate(
            [low, jnp.zeros((W_PREP_ROWS, LANES - GLA_GATE_RANK), F32)], axis=1).astype(BF16)


def _w_in_specs(layer):
    return pl.BlockSpec((1, D_MODEL, D_IN), lambda i: (layer, 0, 0), pipeline_mode=pl.Buffered(1))


def _norm_proj_kernel(x_ref, gain_ref, w_ref, proj_ref, wb_ref):
    @pl.when(pl.program_id(0) == 0)
    def _():
        _stage_w_in(w_ref, wb_ref)
    proj_ref[...] = _dot(_rms(x_ref[...], gain_ref[...]).astype(BF16), wb_ref[...]).astype(BF16)


def _norm_proj(x, gain, w_in, layer):
    T = x.shape[0]
    return pl.pallas_call(
        _norm_proj_kernel,
        grid=(T // TM,),
        in_specs=[
            pl.BlockSpec((TM, D_MODEL), lambda i: (i, 0)),
            pl.BlockSpec((1, D_MODEL), lambda i: (0, 0)),
            _w_in_specs(layer),
        ],
        out_specs=pl.BlockSpec((TM, D_PROJ), lambda i: (i, 0)),
        out_shape=jax.ShapeDtypeStruct((T, D_PROJ), BF16),
        scratch_shapes=[pltpu.VMEM((D_MODEL, D_PROJ), BF16)],
        compiler_params=pltpu.CompilerParams(
            dimension_semantics=("arbitrary",), vmem_limit_bytes=VMEM_LIMIT),
        name="norm_proj",
    )(x, gain, w_in)


def _row_gather_copy(yb_ref, buf_ref, sem_ref, slot, k, r, d):
    return pltpu.make_async_copy(yb_ref.at[d], buf_ref.at[slot, k, r], sem_ref.at[slot])


def _gather_start(dest_ref, yb_ref, buf_ref, sem_ref, slot, rows=range(TM)):
    for r in rows:
        for k in range(TOP_K):
            _row_gather_copy(yb_ref, buf_ref, sem_ref, slot, k, r,
                             dest_ref[0, 0, k * TM + r]).start(priority=k)


def _gather_wait(yb_ref, buf_ref, sem_ref, slot):
    for k in range(TOP_K):
        pltpu.make_async_copy(yb_ref.at[pl.ds(0, TM)], buf_ref.at[slot, k], sem_ref.at[slot]).wait()


def _combined_residual(dcur_ref, x_ref, route_ref, yb_ref, buf_ref, sem_ref):
    i = pl.program_id(0)
    slot = lax.rem(i, 2)

    @pl.when(i == 0)
    def _():
        _gather_start(dcur_ref, yb_ref, buf_ref, sem_ref, 0)

    _gather_wait(yb_ref, buf_ref, sem_ref, slot)
    w0 = route_ref[:, R_W:R_W + 1]
    w1 = route_ref[:, R_W + 1:R_W + 2]
    y0 = _row_tiles_chunks(buf_ref.at[slot, 0], TM)
    y1 = _row_tiles_chunks(buf_ref.at[slot, 1], TM)
    return jnp.concatenate(
        [x_ref[:, c * LANES:(c + 1) * LANES] + (w0 * y0[c] + w1 * y1[c]) for c in range(ROW_TILES)],
        axis=1)


def _prefetch_groups(n_groups):
    per = -(-TM // n_groups)
    return [range(g * per, min(TM, (g + 1) * per)) for g in range(n_groups)]


def _drain_last_prefetch(yb_ref, buf_ref, sem_ref):
    i = pl.program_id(0)

    @pl.when(i == pl.num_programs(0) - 1)
    def _():
        _gather_wait(yb_ref, buf_ref, sem_ref, 1 - lax.rem(i, 2))


def _combine_specs(n_tiles):
    smem_tile = lambda f: pl.BlockSpec((1, 1, TOP_K * TM), f, memory_space=pltpu.SMEM)
    return [
        smem_tile(lambda i: (i, 0, 0)),
        smem_tile(lambda i: (jnp.minimum(i + 1, n_tiles - 1), 0, 0)),
        pl.BlockSpec((TM, D_MODEL), lambda i: (i, 0)),
        pl.BlockSpec((TM, LANES), lambda i: (i, 0)),
        pl.BlockSpec(memory_space=pl.ANY),
    ]


_COMBINE_SCRATCH = [pltpu.VMEM((2, TOP_K, TM, ROW_TILES, LANES), F32),
                    pltpu.SemaphoreType.DMA((2,))]


def _combine_norm_proj_kernel(dcur_ref, dnxt_ref, x_ref, route_ref, yb_ref, gain_ref, w_ref,
                              xo_ref, proj_ref, buf_ref, sem_ref, wb_ref):
    @pl.when(pl.program_id(0) == 0)
    def _():
        _stage_w_in(w_ref, wb_ref)
    x = _combined_residual(dcur_ref, x_ref, route_ref, yb_ref, buf_ref, sem_ref)
    xo_ref[...] = x
    h = _rms(x, gain_ref[...]).astype(BF16)
    nxt = 1 - lax.rem(pl.program_id(0), 2)
    col_chunks = [(c0, min(c0 + PROJ_CHUNK, D_PROJ)) for c0 in range(0, D_PROJ, PROJ_CHUNK)]
    for rows, (c0, c1) in zip(_prefetch_groups(len(col_chunks)), col_chunks):
        _gather_start(dnxt_ref, yb_ref, buf_ref, sem_ref, nxt, rows)
        proj_ref[:, c0:c1] = _dot(h, wb_ref[:, c0:c1]).astype(BF16)
    _drain_last_prefetch(yb_ref, buf_ref, sem_ref)


def _combine_norm_proj(dest_tiles, x2, route, yb, gain, w_in, layer):
    T = x2.shape[0]
    n_tiles = T // TM
    return pl.pallas_call(
        _combine_norm_proj_kernel,
        grid=(n_tiles,),
        in_specs=_combine_specs(n_tiles) + [
            pl.BlockSpec((1, D_MODEL), lambda i: (0, 0)),
            _w_in_specs(layer),
        ],
        out_specs=[pl.BlockSpec((TM, D_MODEL), lambda i: (i, 0)),
                   pl.BlockSpec((TM, D_PROJ), lambda i: (i, 0))],
        out_shape=[jax.ShapeDtypeStruct((T, D_MODEL), F32),
                   jax.ShapeDtypeStruct((T, D_PROJ), BF16)],
        scratch_shapes=_COMBINE_SCRATCH + [pltpu.VMEM((D_MODEL, D_PROJ), BF16)],
        compiler_params=pltpu.CompilerParams(
            dimension_semantics=("arbitrary",), vmem_limit_bytes=VMEM_LIMIT),
        name="combine_norm_proj",
    )(dest_tiles, dest_tiles, x2, route, yb, gain, w_in)


def _combine_final_norm_kernel(dcur_ref, dnxt_ref, x_ref, route_ref, yb_ref, gain_ref,
                               o_ref, buf_ref, sem_ref):
    _gather_start(dnxt_ref, yb_ref, buf_ref, sem_ref, 1 - lax.rem(pl.program_id(0), 2))
    x = _combined_residual(dcur_ref, x_ref, route_ref, yb_ref, buf_ref, sem_ref)
    o_ref[...] = _rms(x, gain_ref[...])
    _drain_last_prefetch(yb_ref, buf_ref, sem_ref)


def _combine_final_norm(dest_tiles, x2, route, yb, gain):
    T = x2.shape[0]
    n_tiles = T // TM
    return pl.pallas_call(
        _combine_final_norm_kernel,
        grid=(n_tiles,),
        in_specs=_combine_specs(n_tiles) + [pl.BlockSpec((1, D_MODEL), lambda i: (0, 0))],
        out_specs=pl.BlockSpec((TM, D_MODEL), lambda i: (i, 0)),
        out_shape=jax.ShapeDtypeStruct((T, D_MODEL), F32),
        scratch_shapes=_COMBINE_SCRATCH,
        compiler_params=pltpu.CompilerParams(
            dimension_semantics=("arbitrary",), vmem_limit_bytes=VMEM_LIMIT),
        name="combine_final_norm",
    )(dest_tiles, dest_tiles, x2, route, yb, gain)


def _gelu_tanh(x):
    c = 0.7978845608028654
    return x * (0.5 * (1.0 + jnp.tanh(c * (x + 0.044715 * (x * x * x)))))


def _mixer_kernel(proj_ref, wgk_ref, bgk_ref, glan_ref, gmn_ref, wsp_ref, bsp_ref, wconv_ref,
                  out_ref, st_ref, hc_ref, lcat_ref, wm_ref):
    TS = TS_MIX
    n_gla = TS // GLA_CHUNK
    n_gm = TS // GMLP_CHUNK

    @pl.when(pl.program_id(1) == 0)
    def _():
        st_ref[...] = jnp.zeros_like(st_ref)
        hc_ref[...] = jnp.zeros_like(hc_ref)
        r = lax.broadcasted_iota(jnp.int32, (2 * TS, TS), 0)
        c = lax.broadcasted_iota(jnp.int32, (2 * TS, TS), 1)
        rr = jnp.where(r >= TS, r - TS, r)
        same = (rr // GLA_CHUNK) == (c // GLA_CHUNK)
        keep = same & ((r >= TS) | (c <= rr))
        lcat_ref[...] = jnp.where(keep, 1.0, 0.0).astype(BF16)
        t = lax.broadcasted_iota(jnp.int32, (GMLP_CHUNK, GMLP_HEADS * GMLP_CHUNK), 0)
        s = lax.broadcasted_iota(jnp.int32, (GMLP_CHUNK, GMLP_HEADS * GMLP_CHUNK), 1) % GMLP_CHUNK
        wm_ref[...] = jnp.where(s <= t, wsp_ref[...], 0.0).astype(BF16)

    lane256 = lax.broadcasted_iota(jnp.int32, (1, GLA_KDIM), 1)

    q = proj_ref[:, C_Q:C_Q + GLA_KDIM].astype(F32)
    k = proj_ref[:, C_K:C_K + GLA_KDIM].astype(F32)
    v_b = proj_ref[:, C_V:C_V + GLA_WIDTH]
    z = _dot(proj_ref[:, C_GKL:C_GKL + LANES], wgk_ref[...]) + bgk_ref[...]
    gk = (jnp.minimum(z, 0.0) - jnp.log1p(jnp.exp(-jnp.abs(z)))) * (1.0 / GLA_GATE_NORM)
    gk_hi, gk_lo = _split_bf16(gk)
    cs = _dot(lcat_ref[...], jnp.concatenate([gk_hi, gk_lo], axis=1))
    b = cs[:TS, :GLA_KDIM] + cs[:TS, GLA_KDIM:]
    bl = cs[TS:, :GLA_KDIM] + cs[TS:, GLA_KDIM:]
    q_dec = (q * (GLA_DK ** -0.5)) * jnp.exp(b)
    k_inv = (k * jnp.exp(-b)).astype(BF16)
    k_dec = (k * jnp.exp(bl - b)).astype(BF16)
    q_dec_b = q_dec.astype(BF16)

    zero_b = jnp.zeros_like(q_dec_b)
    q_stack = jnp.concatenate(
        [jnp.where((lane256 // GLA_DK) == h, q_dec_b, zero_b) for h in range(GLA_HEADS)], axis=0)
    scores = lax.dot_general(q_stack, k_inv, (((1,), (1,)), ((), ())),
                             preferred_element_type=F32)
    rt = lax.broadcasted_iota(jnp.int32, (TS, TS), 0)
    ct = lax.broadcasted_iota(jnp.int32, (TS, TS), 1)
    causal = ((rt // GLA_CHUNK) == (ct // GLA_CHUNK)) & (ct <= rt)
    o_heads = []
    for h in range(GLA_HEADS):
        p_h = jnp.where(causal, scores[h * TS:(h + 1) * TS, :], 0.0).astype(BF16)
        o_heads.append(_dot(p_h, v_b[:, h * GLA_DV:(h + 1) * GLA_DV]))

    sr = lax.broadcasted_iota(jnp.int32, (GLA_WIDTH, GLA_KDIM), 0) // GLA_DV
    sc = lax.broadcasted_iota(jnp.int32, (GLA_WIDTH, GLA_KDIM), 1) // GLA_DK
    bd_mask = sr == sc
    o_inter = []
    for c in range(n_gla):
        rows = slice(c * GLA_CHUNK, (c + 1) * GLA_CHUNK)
        st = st_ref[...]
        o_inter.append(lax.dot_general(q_dec_b[rows], st.astype(BF16), (((1,), (1,)), ((), ())),
                                       preferred_element_type=F32))
        upd = lax.dot_general(v_b[rows], k_dec[rows], (((0,), (0,)), ((), ())),
                              preferred_element_type=F32)
        decay = jnp.exp(bl[c * GLA_CHUNK:c * GLA_CHUNK + 1, :])
        st_ref[...] = st * decay + jnp.where(bd_mask, upd, 0.0)
    o_inter = jnp.concatenate(o_inter, axis=0)

    for h in range(GLA_HEADS):
        cols = slice(h * GLA_DV, (h + 1) * GLA_DV)
        o = o_heads[h] + o_inter[:, cols]
        o = o * lax.rsqrt(jnp.mean(o * o, axis=-1, keepdims=True) + RMS_EPS) * glan_ref[...]
        g = proj_ref[:, C_G + h * GLA_DV:C_G + (h + 1) * GLA_DV].astype(F32)
        out_ref[:, cols] = (o * (g * (1.0 / (1.0 + jnp.exp(-g))))).astype(out_ref.dtype)

    u = _gelu_tanh(proj_ref[:, C_U:C_U + GMLP_WIDTH].astype(F32))
    vg = _gelu_tanh(proj_ref[:, C_VG:C_VG + GMLP_WIDTH].astype(F32))
    hr = lax.broadcasted_iota(jnp.int32, (GMLP_WIDTH, GMLP_WIDTH), 0) // GMLP_DH
    hcn = lax.broadcasted_iota(jnp.int32, (GMLP_WIDTH, GMLP_WIDTH), 1) // GMLP_DH
    head_mean = jnp.where(hr == hcn, 1.0 / GMLP_DH, 0.0).astype(BF16)
    sq_hi, sq_lo = _split_bf16(vg * vg)
    ms = _dot(sq_hi, head_mean) + _dot(sq_lo, head_mean)
    v32 = vg * lax.rsqrt(ms + RMS_EPS) * gmn_ref[...]
    for c in range(n_gm):
        rows = slice(c * GMLP_CHUNK, (c + 1) * GMLP_CHUNK)
        vc = v32[rows].astype(BF16)
        zc = jnp.zeros_like(vc)
        rhs = jnp.concatenate(
            [jnp.where((lane256 // GMLP_DH) == h, vc, zc) for h in range(GMLP_HEADS)], axis=0)
        mixed = _dot(wm_ref[...], rhs) + bsp_ref[...]
        out_ref[rows, GLA_WIDTH:GLA_WIDTH + GMLP_WIDTH] = (u[rows] * mixed).astype(out_ref.dtype)

    hcv = (proj_ref[:, C_CG:C_CG + CONV_WIDTH].astype(F32)
           * proj_ref[:, C_X:C_X + CONV_WIDTH].astype(F32))
    hc_ref[8:8 + TS, :] = hcv
    y = (wconv_ref[2:3, :] * hcv + wconv_ref[1:2, :] * hc_ref[7:7 + TS, :]
         + wconv_ref[0:1, :] * hc_ref[6:6 + TS, :])
    out_ref[:, GLA_WIDTH + GMLP_WIDTH:] = (
        proj_ref[:, C_BG:C_BG + CONV_WIDTH].astype(F32) * y).astype(out_ref.dtype)
    hc_ref[0:8, :] = hc_ref[TS:TS + 8, :]


def _mixers(proj, wgk, bgk, glan, gmn, wsp, bsp, wconv, batch, seq):
    n_seq = seq // TS_MIX
    full = lambda shape: pl.BlockSpec(shape, lambda b, i: (0,) * len(shape))
    return pl.pallas_call(
        _mixer_kernel,
        grid=(batch, n_seq),
        in_specs=[
            pl.BlockSpec((TS_MIX, D_PROJ), lambda b, i: (b * n_seq + i, 0)),
            full((LANES, GLA_KDIM)), full((1, GLA_KDIM)), full((1, GLA_DV)), full((1, GMLP_WIDTH)),
            full((GMLP_CHUNK, GMLP_HEADS * GMLP_CHUNK)), full((GMLP_CHUNK, GMLP_WIDTH)),
            full((8, CONV_WIDTH)),
        ],
        out_specs=pl.BlockSpec((TS_MIX, D_MODEL), lambda b, i: (b * n_seq + i, 0)),
        out_shape=jax.ShapeDtypeStruct((batch * seq, D_MODEL), BF16),
        scratch_shapes=[
            pltpu.VMEM((GLA_WIDTH, GLA_KDIM), F32),
            pltpu.VMEM((TS_MIX + 8, CONV_WIDTH), F32),
            pltpu.VMEM((2 * TS_MIX, TS_MIX), BF16),
            pltpu.VMEM((GMLP_CHUNK, GMLP_HEADS * GMLP_CHUNK), BF16),
        ],
        compiler_params=pltpu.CompilerParams(
            dimension_semantics=("arbitrary", "arbitrary"), vmem_limit_bytes=VMEM_LIMIT),
        name="mixers",
    )(proj, wgk, bgk, glan, gmn, wsp, bsp, wconv)


def _row_tiles_store(tiles_ref, x):
    rows = x.shape[0]
    flat = tiles_ref.reshape(rows * ROW_TILES, LANES)
    for c in range(ROW_TILES):
        flat[pl.ds(c, rows, stride=ROW_TILES), :] = x[:, c * LANES:(c + 1) * LANES]


def _row_tiles_chunks(tiles_ref, rows):
    flat = tiles_ref.reshape(rows * ROW_TILES, LANES)
    return [flat[pl.ds(c, rows, stride=ROW_TILES), :] for c in range(ROW_TILES)]


def _out_router_kernel(mix_ref, x_ref, wo_ref, gain_ref, wrh_ref, wrl_ref, br_ref,
                       x2_ref, h2_ref, route_ref, cnt_ref, tri_ref, wob_ref):
    @pl.when(pl.program_id(0) == 0)
    def _():
        cnt_ref[...] = jnp.zeros_like(cnt_ref)
        r = lax.broadcasted_iota(jnp.int32, (TM, TM), 0)
        c = lax.broadcasted_iota(jnp.int32, (TM, TM), 1)
        tri_ref[...] = jnp.where(c < r, 1.0, 0.0).astype(BF16)
        for r0 in range(0, D_MODEL, W_PREP_ROWS):
            wob_ref[r0:r0 + W_PREP_ROWS, :] = wo_ref[0, r0:r0 + W_PREP_ROWS, :].astype(BF16)

    x2 = x_ref[...] + _dot(mix_ref[...], wob_ref[...])
    x2_ref[...] = x2
    h = _rms(x2, gain_ref[...])
    h_hi, h_lo = _split_bf16(h)
    h2_ref[...] = h_hi
    lg = (_dot(h_hi, wrh_ref[...]) + _dot(h_lo, wrh_ref[...]) + _dot(h_hi, wrl_ref[...])
          + br_ref[...])

    lane = lax.broadcasted_iota(jnp.int32, (TM, LANES), 1).astype(F32)
    neg = -jnp.inf
    is_g = lane < N_GROUPS
    gl = jnp.where(is_g, lg, neg)
    gmax = jnp.max(gl, axis=1, keepdims=True)
    g_top = jnp.min(jnp.where(gl == gmax, lane, float(LANES)), axis=1, keepdims=True)
    g_w = 1.0 / jnp.sum(jnp.where(is_g, jnp.exp(lg - gmax), 0.0), axis=1, keepdims=True)
    first = N_GROUPS + EXPERTS_PER_GROUP * g_top
    el = jnp.where((lane >= first) & (lane < first + EXPERTS_PER_GROUP), lg, neg)
    m1 = jnp.max(el, axis=1, keepdims=True)
    i1 = jnp.min(jnp.where(el == m1, lane, float(LANES)), axis=1, keepdims=True)
    el2 = jnp.where(lane == i1, neg, el)
    m2 = jnp.max(el2, axis=1, keepdims=True)
    i2 = jnp.min(jnp.where(el2 == m2, lane, float(LANES)), axis=1, keepdims=True)
    ratio = jnp.exp(m2 - m1)
    w1 = g_w / (1.0 + ratio)
    w2 = w1 * ratio

    oh1 = jnp.where(lane == i1, 1.0, 0.0)
    oh2 = jnp.where(lane == i2, 1.0, 0.0)
    oh = oh1 + oh2
    before = _dot(tri_ref[...], oh.astype(BF16)) + cnt_ref[0:1, :]
    rank1 = jnp.sum(oh1 * before, axis=1, keepdims=True)
    rank2 = jnp.sum(oh2 * before, axis=1, keepdims=True)
    cnt_ref[...] = cnt_ref[...] + jnp.sum(oh, axis=0, keepdims=True)

    rec = jnp.zeros((TM, LANES), F32)
    for col, val in ((R_E, i1 - N_GROUPS), (R_E + 1, i2 - N_GROUPS), (R_RANK, rank1),
                     (R_RANK + 1, rank2), (R_W, w1), (R_W + 1, w2)):
        rec = jnp.where(lane == col, val, rec)
    route_ref[...] = rec


def _out_router(mixed, x, w_out, layer, gain, wr_hi, wr_lo, br):
    T = x.shape[0]
    row = lambda w: pl.BlockSpec((TM, w), lambda i: (i, 0))
    full = lambda shape: pl.BlockSpec(shape, lambda i: (0, 0))
    wo_spec = pl.BlockSpec((1, D_MODEL, D_MODEL), lambda i: (layer, 0, 0),
                           pipeline_mode=pl.Buffered(1))
    return pl.pallas_call(
        _out_router_kernel,
        grid=(T // TM,),
        in_specs=[row(D_MODEL), row(D_MODEL), wo_spec, full((1, D_MODEL)),
                  full((D_MODEL, ROUTER_COLS)), full((D_MODEL, ROUTER_COLS)), full((1, ROUTER_COLS))],
        out_specs=[row(D_MODEL), row(D_MODEL), row(LANES), full((8, LANES))],
        out_shape=[jax.ShapeDtypeStruct((T, D_MODEL), F32),
                   jax.ShapeDtypeStruct((T, D_MODEL), BF16),
                   jax.ShapeDtypeStruct((T, LANES), F32),
                   jax.ShapeDtypeStruct((8, LANES), F32)],
        scratch_shapes=[pltpu.VMEM((TM, TM), BF16), pltpu.VMEM((D_MODEL, D_MODEL), BF16)],
        compiler_params=pltpu.CompilerParams(
            dimension_semantics=("arbitrary",), vmem_limit_bytes=VMEM_LIMIT),
        name="out_router",
    )(mixed, x, w_out, gain, wr_hi, wr_lo, br)


def _dispatch_kernel(fill_ref, nu_ref, dest_ref, h_ref, xb_ref, zero_ref, sem_ref, zsem_ref,
                     stage_ref):
    i = pl.program_id(0)
    par = lax.rem(i, 2)

    @pl.when(i == 0)
    def _():
        zero_ref[...] = jnp.zeros_like(zero_ref)
        fills = [(fill_ref[e] >= 0, pltpu.make_async_copy(
            zero_ref, xb_ref.at[pl.ds(pl.multiple_of(jnp.maximum(fill_ref[e], 0), MOE_BLK), MOE_BLK)],
            zsem_ref)) for e in range(N_EXPERTS)]
        n_blocks = xb_ref.shape[0] // MOE_BLK
        fills += [(j >= nu_ref[0], pltpu.make_async_copy(
            zero_ref, xb_ref.at[pl.ds(j * MOE_BLK, MOE_BLK)], zsem_ref))
            for j in range(n_blocks - N_EXPERTS, n_blocks)]
        for cond, f in fills:
            pl.when(cond)(f.start)
        for cond, f in fills:
            pl.when(cond)(f.wait)

    _row_tiles_store(stage_ref.at[par], h_ref[...].astype(F32))
    for r in range(TM):
        for k in range(TOP_K):
            pltpu.make_async_copy(stage_ref.at[par, r], xb_ref.at[dest_ref[0, 0, k * TM + r]],
                                  sem_ref.at[par]).start(priority=k)

    def wait_tile(p):
        for _ in range(TOP_K):
            pltpu.make_async_copy(stage_ref.at[p], xb_ref.at[pl.ds(0, TM)], sem_ref.at[p]).wait()

    pl.when(i > 0)(lambda: wait_tile(1 - par))
    pl.when(i == pl.num_programs(0) - 1)(lambda: wait_tile(par))


def _dispatch(fill_start, n_used, dest_tiles, h2, n_rows):
    T = h2.shape[0]
    grid_spec = pltpu.PrefetchScalarGridSpec(
        num_scalar_prefetch=2,
        grid=(T // TM,),
        in_specs=[
            pl.BlockSpec((1, 1, TOP_K * TM), lambda i, fs, nu: (i, 0, 0), memory_space=pltpu.SMEM),
            pl.BlockSpec((TM, D_MODEL), lambda i, fs, nu: (i, 0)),
        ],
        out_specs=pl.BlockSpec(memory_space=pl.ANY),
        scratch_shapes=[pltpu.VMEM((MOE_BLK, ROW_TILES, LANES), F32),
                        pltpu.SemaphoreType.DMA((2,)), pltpu.SemaphoreType.DMA(()),
                        pltpu.VMEM((2, TM, ROW_TILES, LANES), F32)],
    )
    return pl.pallas_call(
        _dispatch_kernel,
        grid_spec=grid_spec,
        out_shape=jax.ShapeDtypeStruct((n_rows, ROW_TILES, LANES), F32),
        compiler_params=pltpu.CompilerParams(dimension_semantics=("arbitrary",)),
        name="dispatch",
    )(fill_start, n_used, dest_tiles, h2)


def _expert_kernel(blk_ref, be_ref, nu_ref, x_ref, wg_ref, wu_ref, wd_ref, o_ref,
                   wgb_ref, wub_ref, wdb_ref):
    del blk_ref
    j = pl.program_id(0)
    used = j < nu_ref[0]

    @pl.when((j == 0) | (be_ref[j] != be_ref[jnp.maximum(j - 1, 0)]))
    def _():
        for r0 in range(0, D_MODEL, W_PREP_ROWS):
            rows = slice(r0, r0 + W_PREP_ROWS)
            wgb_ref[rows, :] = wg_ref[0, 0, rows, :].astype(BF16)
            wub_ref[rows, :] = wu_ref[0, 0, rows, :].astype(BF16)
        for r0 in range(0, D_EXPERT, W_PREP_ROWS):
            rows = slice(r0, r0 + W_PREP_ROWS)
            wdb_ref[rows, :] = wd_ref[0, 0, rows, :].astype(BF16)

    @pl.when(used)
    def _():
        x = jnp.concatenate([c.astype(BF16) for c in _row_tiles_chunks(x_ref, MOE_BLK)], axis=1)
        g = _dot(x, wgb_ref[...])
        u = _dot(x, wub_ref[...])
        h = (g * (1.0 / (1.0 + jnp.exp(-g)))) * u
        _row_tiles_store(o_ref, _dot(h.astype(BF16), wdb_ref[...]))

    @pl.when(jnp.logical_not(used))
    def _():
        o_ref[...] = jnp.zeros_like(o_ref)


def _experts(blk_idx, blk_exp, n_used, xb, w_gate, w_up, w_down, layer):
    n_steps = blk_idx.shape[0]
    w_spec = lambda k, n: pl.BlockSpec((1, 1, k, n), lambda j, bi, be, nu: (layer, be[j], 0, 0))
    grid_spec = pltpu.PrefetchScalarGridSpec(
        num_scalar_prefetch=3,
        grid=(n_steps,),
        in_specs=[
            pl.BlockSpec((MOE_BLK, ROW_TILES, LANES), lambda j, bi, be, nu: (bi[j], 0, 0)),
            w_spec(D_MODEL, D_EXPERT), w_spec(D_MODEL, D_EXPERT), w_spec(D_EXPERT, D_MODEL),
        ],
        out_specs=pl.BlockSpec((MOE_BLK, ROW_TILES, LANES), lambda j, bi, be, nu: (j, 0, 0)),
        scratch_shapes=[pltpu.VMEM((D_MODEL, D_EXPERT), BF16), pltpu.VMEM((D_MODEL, D_EXPERT), BF16),
                        pltpu.VMEM((D_EXPERT, D_MODEL), BF16)],
    )
    return pl.pallas_call(
        _expert_kernel,
        grid_spec=grid_spec,
        out_shape=jax.ShapeDtypeStruct((n_steps * MOE_BLK, ROW_TILES, LANES), F32),
        compiler_params=pltpu.CompilerParams(
            dimension_semantics=("arbitrary",), vmem_limit_bytes=VMEM_LIMIT),
        name="experts",
    )(blk_idx, blk_exp, n_used, xb, w_gate, w_up, w_down)


def _dispatch_tables(route, counts_rec, T):
    counts = counts_rec[0, N_GROUPS:N_GROUPS + N_EXPERTS].astype(jnp.int32)
    n_steps = (T * TOP_K) // MOE_BLK + N_EXPERTS
    nblk = (counts + MOE_BLK - 1) // MOE_BLK
    bend = jnp.cumsum(nblk)
    pstart = (bend - nblk) * MOE_BLK
    n_used = bend[-1]
    j = jnp.minimum(jnp.arange(n_steps, dtype=jnp.int32), n_used - 1)
    blk_exp = jnp.minimum(jnp.sum(j[:, None] >= bend[None, :], axis=1), N_EXPERTS - 1)
    n_rows = n_steps * MOE_BLK
    last_blk = jnp.where(counts > 0, (bend - 1) * MOE_BLK, -1)
    e = route[:, R_E:R_E + TOP_K].astype(jnp.int32)
    rank = route[:, R_RANK:R_RANK + TOP_K].astype(jnp.int32)
    seg = jnp.sum(jnp.where(e[..., None] == jnp.arange(N_EXPERTS), pstart, 0), axis=-1)
    dest = jnp.clip(seg + rank, 0, n_steps * MOE_BLK - 1)
    dest_tiles = dest.reshape(T // TM, TM, TOP_K).transpose(0, 2, 1).reshape(T // TM, 1, TOP_K * TM)
    return dict(dest_tiles=dest_tiles, fill_start=last_blk.astype(jnp.int32),
                blk_idx=j, blk_exp=blk_exp.astype(jnp.int32),
                n_used=n_used.reshape(1).astype(jnp.int32), n_rows=n_rows)


def _prep_layer(l, w_gk_up, b_gk, gla_norm, gmlp_norm, w_spatial, b_spatial, w_conv,
                w_router_group, b_router_group, w_router_expert, b_router_expert):
    wgk = jnp.concatenate(
        [w_gk_up[l], jnp.zeros((LANES - GLA_GATE_RANK, GLA_KDIM), F32)], axis=0).astype(BF16)
    wsp = w_spatial[l].transpose(1, 0, 2).reshape(GMLP_CHUNK, GMLP_HEADS * GMLP_CHUNK)
    bsp = jnp.repeat(b_spatial[l].T, GMLP_DH, axis=1)
    wconv = jnp.concatenate([w_conv[l], jnp.zeros((8 - CONV_K, CONV_WIDTH), F32)], axis=0)
    wr = jnp.concatenate(
        [w_router_group[l], w_router_expert[l],
         jnp.zeros((D_MODEL, ROUTER_COLS - N_GROUPS - N_EXPERTS), F32)], axis=1)
    wr_hi = wr.astype(BF16)
    wr_lo = (wr - wr_hi.astype(F32)).astype(BF16)
    br = jnp.concatenate(
        [b_router_group[l], b_router_expert[l],
         jnp.zeros((ROUTER_COLS - N_GROUPS - N_EXPERTS,), F32)])[None, :]
    return dict(
        wgk=wgk, bgk=b_gk[l][None, :], glan=gla_norm[l][None, :], gmn=gmlp_norm[l][None, :],
        wsp=wsp, bsp=bsp, wconv=wconv, wr_hi=wr_hi, wr_lo=wr_lo, br=br)


def kernel(x, attn_norm, w_in, w_gk_up, b_gk, gla_norm, gmlp_norm, w_spatial, b_spatial, w_conv, w_out, ffn_norm, w_router_group, b_router_group, w_router_expert, b_router_expert, w_gate, w_up, w_down, final_norm):
    B, S, D = x.shape
    T = B * S
    depth = w_in.shape[0]
    xr = x.reshape(T, D)
    moe = None
    for l in range(depth):
        p = _prep_layer(l, w_gk_up, b_gk, gla_norm, gmlp_norm, w_spatial, b_spatial, w_conv,
                        w_router_group, b_router_group, w_router_expert, b_router_expert)
        if moe is None:
            proj = _norm_proj(xr, attn_norm[l][None, :], w_in, l)
        else:
            xr, proj = _combine_norm_proj(moe["dest_tiles"], moe["x2"], moe["route"], moe["yb"],
                                          attn_norm[l][None, :], w_in, l)
        mixed = _mixers(proj, p["wgk"], p["bgk"], p["glan"], p["gmn"], p["wsp"], p["bsp"],
                        p["wconv"], B, S)
        x2, h2, route, counts_rec = _out_router(mixed, xr, w_out, l, ffn_norm[l][None, :],
                                                 p["wr_hi"], p["wr_lo"], p["br"])
        moe = _dispatch_tables(route, counts_rec, T)
        xb = _dispatch(moe["fill_start"], moe["n_used"], moe["dest_tiles"], h2, moe["n_rows"])
        yb = _experts(moe["blk_idx"], moe["blk_exp"], moe["n_used"], xb, w_gate, w_up, w_down, l)
        moe.update(x2=x2, route=route, yb=yb)
    out = _combine_final_norm(moe["dest_tiles"], moe["x2"], moe["route"], moe["yb"],
                              final_norm[None, :])
    return out.reshape(B, S, D)
```

```python
import functools

import jax
import jax.numpy as jnp
from jax import lax
from jax.experimental import pallas as pl
from jax.experimental.pallas import tpu as pltpu

F32 = jnp.float32
BF16 = jnp.bfloat16

D_MODEL = 1024
RMS_EPS = 1e-6
GLA_HEADS = 4
GLA_WIDTH = 512
GLA_DV = 128
GLA_DK = 64
GLA_KDIM = 256
GLA_GATE_RANK = 16
GLA_GATE_NORM = 16.0
GLA_CHUNK = 64
GMLP_HEADS = 4
GMLP_WIDTH = 256
GMLP_DH = 64
GMLP_CHUNK = 128
CONV_WIDTH = 256
CONV_K = 3
N_GROUPS = 4
EXPERTS_PER_GROUP = 8
N_EXPERTS = 32
TOP_K = 2
D_EXPERT = 256

LANES = 128
C_Q, C_K, C_V, C_G = 0, 256, 512, 1024
C_U, C_VG, C_X, C_BG, C_CG, C_GKL = 1536, 1792, 2048, 2304, 2560, 2816
D_PROJ = C_GKL + LANES
D_IN = C_GKL + GLA_GATE_RANK

TM = 256
TS_MIX = 256
MOE_BLK = 256
ROUTER_COLS = LANES
SUBLANES = 8
ROW_TILES = D_MODEL // LANES
assert ROW_TILES == SUBLANES
VMEM_LIMIT = 56 * 1024 * 1024
R_E, R_RANK, R_W = 0, 2, 4


def _dot(a, b):
    return jnp.dot(a, b, preferred_element_type=F32)


def _split_bf16(x):
    hi = x.astype(BF16)
    lo = (x - hi.astype(F32)).astype(BF16)
    return hi, lo


def _rms(x, gain):
    return x * lax.rsqrt(jnp.mean(x * x, axis=-1, keepdims=True) + RMS_EPS) * gain


W_PREP_ROWS = 128
PROJ_CHUNK = 256


def _stage_w_in(w_ref, wb_ref):
    for r0 in range(0, D_MODEL, W_PREP_ROWS):
        rows = slice(r0, r0 + W_PREP_ROWS)
        wb_ref[rows, 0:C_U] = w_ref[0, rows, 0:C_U].astype(BF16)
        wb_ref[rows, C_U:C_GKL] = w_ref[0, rows, C_U + GLA_GATE_RANK:D_IN].astype(BF16)
        low = w_ref[0, rows, C_U:C_U + GLA_GATE_RANK]
        wb_ref[rows, C_GKL:D_PROJ] = jnp.concatenate(
            [low, jnp.zeros((W_PREP_ROWS, LANES - GLA_GATE_RANK), F32)], axis=1).astype(BF16)


def _w_in_specs(layer):
    return pl.BlockSpec((1, D_MODEL, D_IN), lambda i: (layer, 0, 0), pipeline_mode=pl.Buffered(1))


def _norm_proj_kernel(x_ref, gain_ref, w_ref, proj_ref, wb_ref):
    @pl.when(pl.program_id(0) == 0)
    def _():
        _stage_w_in(w_ref, wb_ref)
    proj_ref[...] = _dot(_rms(x_ref[...], gain_ref[...]).astype(BF16), wb_ref[...]).astype(BF16)


def _norm_proj(x, gain, w_in, layer):
    T = x.shape[0]
    return pl.pallas_call(
        _norm_proj_kernel,
        grid=(T // TM,),
        in_specs=[
            pl.BlockSpec((TM, D_MODEL), lambda i: (i, 0)),
            pl.BlockSpec((1, D_MODEL), lambda i: (0, 0)),
            _w_in_specs(layer),
        ],
        out_specs=pl.BlockSpec((TM, D_PROJ), lambda i: (i, 0)),
        out_shape=jax.ShapeDtypeStruct((T, D_PROJ), BF16),
        scratch_shapes=[pltpu.VMEM((D_MODEL, D_PROJ), BF16)],
        compiler_params=pltpu.CompilerParams(
            dimension_semantics=("arbitrary",), vmem_limit_bytes=VMEM_LIMIT),
        name="norm_proj",
    )(x, gain, w_in)


def _row_gather_copy(yb_ref, buf_ref, sem_ref, slot, k, r, d):
    return pltpu.make_async_copy(yb_ref.at[d], buf_ref.at[slot, k, r], sem_ref.at[slot])


def _gather_start(dest_ref, yb_ref, buf_ref, sem_ref, slot, rows=range(TM)):
    for r in rows:
        for k in range(TOP_K):
            _row_gather_copy(yb_ref, buf_ref, sem_ref, slot, k, r,
                             dest_ref[0, 0, k * TM + r]).start(priority=k)


def _gather_wait(yb_ref, buf_ref, sem_ref, slot):
    for k in range(TOP_K):
        pltpu.make_async_copy(yb_ref.at[pl.ds(0, TM)], buf_ref.at[slot, k], sem_ref.at[slot]).wait()


def _combined_residual(dcur_ref, x_ref, route_ref, yb_ref, buf_ref, sem_ref):
    i = pl.program_id(0)
    slot = lax.rem(i, 2)

    @pl.when(i == 0)
    def _():
        _gather_start(dcur_ref, yb_ref, buf_ref, sem_ref, 0)

    _gather_wait(yb_ref, buf_ref, sem_ref, slot)
    w0 = route_ref[:, R_W:R_W + 1]
    w1 = route_ref[:, R_W + 1:R_W + 2]
    y0 = _row_tiles_chunks(buf_ref.at[slot, 0], TM)
    y1 = _row_tiles_chunks(buf_ref.at[slot, 1], TM)
    return jnp.concatenate(
        [x_ref[:, c * LANES:(c + 1) * LANES] + (w0 * y0[c] + w1 * y1[c]) for c in range(ROW_TILES)],
        axis=1)


def _prefetch_groups(n_groups):
    per = -(-TM // n_groups)
    return [range(g * per, min(TM, (g + 1) * per)) for g in range(n_groups)]


def _drain_last_prefetch(yb_ref, buf_ref, sem_ref):
    i = pl.program_id(0)

    @pl.when(i == pl.num_programs(0) - 1)
    def _():
        _gather_wait(yb_ref, buf_ref, sem_ref, 1 - lax.rem(i, 2))


def _combine_specs(n_tiles):
    smem_tile = lambda f: pl.BlockSpec((1, 1, TOP_K * TM), f, memory_space=pltpu.SMEM)
    return [
        smem_tile(lambda i: (i, 0, 0)),
        smem_tile(lambda i: (jnp.minimum(i + 1, n_tiles - 1), 0, 0)),
        pl.BlockSpec((TM, D_MODEL), lambda i: (i, 0)),
        pl.BlockSpec((TM, LANES), lambda i: (i, 0)),
        pl.BlockSpec(memory_space=pl.ANY),
    ]


_COMBINE_SCRATCH = [pltpu.VMEM((2, TOP_K, TM, ROW_TILES, LANES), F32),
                    pltpu.SemaphoreType.DMA((2,))]


def _combine_norm_proj_kernel(dcur_ref, dnxt_ref, x_ref, route_ref, yb_ref, gain_ref, w_ref,
                              xo_ref, proj_ref, buf_ref, sem_ref, wb_ref):
    @pl.when(pl.program_id(0) == 0)
    def _():
        _stage_w_in(w_ref, wb_ref)
    x = _combined_residual(dcur_ref, x_ref, route_ref, yb_ref, buf_ref, sem_ref)
    xo_ref[...] = x
    h = _rms(x, gain_ref[...]).astype(BF16)
    nxt = 1 - lax.rem(pl.program_id(0), 2)
    col_chunks = [(c0, min(c0 + PROJ_CHUNK, D_PROJ)) for c0 in range(0, D_PROJ, PROJ_CHUNK)]
    for rows, (c0, c1) in zip(_prefetch_groups(len(col_chunks)), col_chunks):
        _gather_start(dnxt_ref, yb_ref, buf_ref, sem_ref, nxt, rows)
        proj_ref[:, c0:c1] = _dot(h, wb_ref[:, c0:c1]).astype(BF16)
    _drain_last_prefetch(yb_ref, buf_ref, sem_ref)


def _combine_norm_proj(dest_tiles, x2, route, yb, gain, w_in, layer):
    T = x2.shape[0]
    n_tiles = T // TM
    return pl.pallas_call(
        _combine_norm_proj_kernel,
        grid=(n_tiles,),
        in_specs=_combine_specs(n_tiles) + [
            pl.BlockSpec((1, D_MODEL), lambda i: (0, 0)),
            _w_in_specs(layer),
        ],
        out_specs=[pl.BlockSpec((TM, D_MODEL), lambda i: (i, 0)),
                   pl.BlockSpec((TM, D_PROJ), lambda i: (i, 0))],
        out_shape=[jax.ShapeDtypeStruct((T, D_MODEL), F32),
                   jax.ShapeDtypeStruct((T, D_PROJ), BF16)],
        scratch_shapes=_COMBINE_SCRATCH + [pltpu.VMEM((D_MODEL, D_PROJ), BF16)],
        compiler_params=pltpu.CompilerParams(
            dimension_semantics=("arbitrary",), vmem_limit_bytes=VMEM_LIMIT),
        name="combine_norm_proj",
    )(dest_tiles, dest_tiles, x2, route, yb, gain, w_in)


def _combine_final_norm_kernel(dcur_ref, dnxt_ref, x_ref, route_ref, yb_ref, gain_ref,
                               o_ref, buf_ref, sem_ref):
    _gather_start(dnxt_ref, yb_ref, buf_ref, sem_ref, 1 - lax.rem(pl.program_id(0), 2))
    x = _combined_residual(dcur_ref, x_ref, route_ref, yb_ref, buf_ref, sem_ref)
    o_ref[...] = _rms(x, gain_ref[...])
    _drain_last_prefetch(yb_ref, buf_ref, sem_ref)


def _combine_final_norm(dest_tiles, x2, route, yb, gain):
    T = x2.shape[0]
    n_tiles = T // TM
    return pl.pallas_call(
        _combine_final_norm_kernel,
        grid=(n_tiles,),
        in_specs=_combine_specs(n_tiles) + [pl.BlockSpec((1, D_MODEL), lambda i: (0, 0))],
        out_specs=pl.BlockSpec((TM, D_MODEL), lambda i: (i, 0)),
        out_shape=jax.ShapeDtypeStruct((T, D_MODEL), F32),
        scratch_shapes=_COMBINE_SCRATCH,
        compiler_params=pltpu.CompilerParams(
            dimension_semantics=("arbitrary",), vmem_limit_bytes=VMEM_LIMIT),
        name="combine_final_norm",
    )(dest_tiles, dest_tiles, x2, route, yb, gain)


def _gelu_tanh(x):
    c = 0.7978845608028654
    return x * (0.5 * (1.0 + jnp.tanh(c * (x + 0.044715 * (x * x * x)))))


def _mixer_kernel(proj_ref, wgk_ref, bgk_ref, glan_ref, gmn_ref, wsp_ref, bsp_ref, wconv_ref,
                  out_ref, st_ref, hc_ref, lcat_ref, wm_ref):
    TS = TS_MIX
    n_gla = TS // GLA_CHUNK
    n_gm = TS // GMLP_CHUNK

    @pl.when(pl.program_id(1) == 0)
    def _():
        st_ref[...] = jnp.zeros_like(st_ref)
        hc_ref[...] = jnp.zeros_like(hc_ref)
        r = lax.broadcasted_iota(jnp.int32, (2 * TS, TS), 0)
        c = lax.broadcasted_iota(jnp.int32, (2 * TS, TS), 1)
        rr = jnp.where(r >= TS, r - TS, r)
        same = (rr // GLA_CHUNK) == (c // GLA_CHUNK)
        keep = same & ((r >= TS) | (c <= rr))
        lcat_ref[...] = jnp.where(keep, 1.0, 0.0).astype(BF16)
        t = lax.broadcasted_iota(jnp.int32, (GMLP_CHUNK, GMLP_HEADS * GMLP_CHUNK), 0)
        s = lax.broadcasted_iota(jnp.int32, (GMLP_CHUNK, GMLP_HEADS * GMLP_CHUNK), 1) % GMLP_CHUNK
        wm_ref[...] = jnp.where(s <= t, wsp_ref[...], 0.0).astype(BF16)

    lane256 = lax.broadcasted_iota(jnp.int32, (1, GLA_KDIM), 1)

    q = proj_ref[:, C_Q:C_Q + GLA_KDIM].astype(F32)
    k = proj_ref[:, C_K:C_K + GLA_KDIM].astype(F32)
    v_b = proj_ref[:, C_V:C_V + GLA_WIDTH]
    z = _dot(proj_ref[:, C_GKL:C_GKL + LANES], wgk_ref[...]) + bgk_ref[...]
    gk = (jnp.minimum(z, 0.0) - jnp.log1p(jnp.exp(-jnp.abs(z)))) * (1.0 / GLA_GATE_NORM)
    gk_hi, gk_lo = _split_bf16(gk)
    cs = _dot(lcat_ref[...], jnp.concatenate([gk_hi, gk_lo], axis=1))
    b = cs[:TS, :GLA_KDIM] + cs[:TS, GLA_KDIM:]
    bl = cs[TS:, :GLA_KDIM] + cs[TS:, GLA_KDIM:]
    q_dec = (q * (GLA_DK ** -0.5)) * jnp.exp(b)
    k_inv = (k * jnp.exp(-b)).astype(BF16)
    k_dec = (k * jnp.exp(bl - b)).astype(BF16)
    q_dec_b = q_dec.astype(BF16)

    zero_b = jnp.zeros_like(q_dec_b)
    q_stack = jnp.concatenate(
        [jnp.where((lane256 // GLA_DK) == h, q_dec_b, zero_b) for h in range(GLA_HEADS)], axis=0)
    scores = lax.dot_general(q_stack, k_inv, (((1,), (1,)), ((), ())),
                             preferred_element_type=F32)
    rt = lax.broadcasted_iota(jnp.int32, (TS, TS), 0)
    ct = lax.broadcasted_iota(jnp.int32, (TS, TS), 1)
    causal = ((rt // GLA_CHUNK) == (ct // GLA_CHUNK)) & (ct <= rt)
    o_heads = []
    for h in range(GLA_HEADS):
        p_h = jnp.where(causal, scores[h * TS:(h + 1) * TS, :], 0.0).astype(BF16)
        o_heads.append(_dot(p_h, v_b[:, h * GLA_DV:(h + 1) * GLA_DV]))

    sr = lax.broadcasted_iota(jnp.int32, (GLA_WIDTH, GLA_KDIM), 0) // GLA_DV
    sc = lax.broadcasted_iota(jnp.int32, (GLA_WIDTH, GLA_KDIM), 1) // GLA_DK
    bd_mask = sr == sc
    o_inter = []
    for c in range(n_gla):
        rows = slice(c * GLA_CHUNK, (c + 1) * GLA_CHUNK)
        st = st_ref[...]
        o_inter.append(lax.dot_general(q_dec_b[rows], st.astype(BF16), (((1,), (1,)), ((), ())),
                                       preferred_element_type=F32))
        upd = lax.dot_general(v_b[rows], k_dec[rows], (((0,), (0,)), ((), ())),
                              preferred_element_type=F32)
        decay = jnp.exp(bl[c * GLA_CHUNK:c * GLA_CHUNK + 1, :])
        st_ref[...] = st * decay + jnp.where(bd_mask, upd, 0.0)
    o_inter = jnp.concatenate(o_inter, axis=0)

    for h in range(GLA_HEADS):
        cols = slice(h * GLA_DV, (h + 1) * GLA_DV)
        o = o_heads[h] + o_inter[:, cols]
        o = o * lax.rsqrt(jnp.mean(o * o, axis=-1, keepdims=True) + RMS_EPS) * glan_ref[...]
        g = proj_ref[:, C_G + h * GLA_DV:C_G + (h + 1) * GLA_DV].astype(F32)
        out_ref[:, cols] = (o * (g * (1.0 / (1.0 + jnp.exp(-g))))).astype(out_ref.dtype)

    u = _gelu_tanh(proj_ref[:, C_U:C_U + GMLP_WIDTH].astype(F32))
    vg = _gelu_tanh(proj_ref[:, C_VG:C_VG + GMLP_WIDTH].astype(F32))
    hr = lax.broadcasted_iota(jnp.int32, (GMLP_WIDTH, GMLP_WIDTH), 0) // GMLP_DH
    hcn = lax.broadcasted_iota(jnp.int32, (GMLP_WIDTH, GMLP_WIDTH), 1) // GMLP_DH
    head_mean = jnp.where(hr == hcn, 1.0 / GMLP_DH, 0.0).astype(BF16)
    sq_hi, sq_lo = _split_bf16(vg * vg)
    ms = _dot(sq_hi, head_mean) + _dot(sq_lo, head_mean)
    v32 = vg * lax.rsqrt(ms + RMS_EPS) * gmn_ref[...]
    for c in range(n_gm):
        rows = slice(c * GMLP_CHUNK, (c + 1) * GMLP_CHUNK)
        vc = v32[rows].astype(BF16)
        zc = jnp.zeros_like(vc)
        rhs = jnp.concatenate(
            [jnp.where((lane256 // GMLP_DH) == h, vc, zc) for h in range(GMLP_HEADS)], axis=0)
        mixed = _dot(wm_ref[...], rhs) + bsp_ref[...]
        out_ref[rows, GLA_WIDTH:GLA_WIDTH + GMLP_WIDTH] = (u[rows] * mixed).astype(out_ref.dtype)

    hcv = (proj_ref[:, C_CG:C_CG + CONV_WIDTH].astype(F32)
           * proj_ref[:, C_X:C_X + CONV_WIDTH].astype(F32))
    hc_ref[8:8 + TS, :] = hcv
    y = (wconv_ref[2:3, :] * hcv + wconv_ref[1:2, :] * hc_ref[7:7 + TS, :]
         + wconv_ref[0:1, :] * hc_ref[6:6 + TS, :])
    out_ref[:, GLA_WIDTH + GMLP_WIDTH:] = (
        proj_ref[:, C_BG:C_BG + CONV_WIDTH].astype(F32) * y).astype(out_ref.dtype)
    hc_ref[0:8, :] = hc_ref[TS:TS + 8, :]


def _mixers(proj, wgk, bgk, glan, gmn, wsp, bsp, wconv, batch, seq):
    n_seq = seq // TS_MIX
    full = lambda shape: pl.BlockSpec(shape, lambda b, i: (0,) * len(shape))
    return pl.pallas_call(
        _mixer_kernel,
        grid=(batch, n_seq),
        in_specs=[
            pl.BlockSpec((TS_MIX, D_PROJ), lambda b, i: (b * n_seq + i, 0)),
            full((LANES, GLA_KDIM)), full((1, GLA_KDIM)), full((1, GLA_DV)), full((1, GMLP_WIDTH)),
            full((GMLP_CHUNK, GMLP_HEADS * GMLP_CHUNK)), full((GMLP_CHUNK, GMLP_WIDTH)),
            full((8, CONV_WIDTH)),
        ],
        out_specs=pl.BlockSpec((TS_MIX, D_MODEL), lambda b, i: (b * n_seq + i, 0)),
        out_shape=jax.ShapeDtypeStruct((batch * seq, D_MODEL), BF16),
        scratch_shapes=[
            pltpu.VMEM((GLA_WIDTH, GLA_KDIM), F32),
            pltpu.VMEM((TS_MIX + 8, CONV_WIDTH), F32),
            pltpu.VMEM((2 * TS_MIX, TS_MIX), BF16),
            pltpu.VMEM((GMLP_CHUNK, GMLP_HEADS * GMLP_CHUNK), BF16),
        ],
        compiler_params=pltpu.CompilerParams(
            dimension_semantics=("arbitrary", "arbitrary"), vmem_limit_bytes=VMEM_LIMIT),
        name="mixers",
    )(proj, wgk, bgk, glan, gmn, wsp, bsp, wconv)


def _row_tiles_store(tiles_ref, x):
    rows = x.shape[0]
    flat = tiles_ref.reshape(rows * ROW_TILES, LANES)
    for c in range(ROW_TILES):
        flat[pl.ds(c, rows, stride=ROW_TILES), :] = x[:, c * LANES:(c + 1) * LANES]


def _row_tiles_chunks(tiles_ref, rows):
    flat = tiles_ref.reshape(rows * ROW_TILES, LANES)
    return [flat[pl.ds(c, rows, stride=ROW_TILES), :] for c in range(ROW_TILES)]


def _out_router_kernel(mix_ref, x_ref, wo_ref, gain_ref, wrh_ref, wrl_ref, br_ref,
                       x2_ref, h2_ref, route_ref, route_t_ref, cnt_ref, tri_ref, wob_ref):
    @pl.when(pl.program_id(0) == 0)
    def _():
        cnt_ref[...] = jnp.zeros_like(cnt_ref)
        r = lax.broadcasted_iota(jnp.int32, (TM, TM), 0)
        c = lax.broadcasted_iota(jnp.int32, (TM, TM), 1)
        tri_ref[...] = jnp.where(c < r, 1.0, 0.0).astype(BF16)
        for r0 in range(0, D_MODEL, W_PREP_ROWS):
            wob_ref[r0:r0 + W_PREP_ROWS, :] = wo_ref[0, r0:r0 + W_PREP_ROWS, :].astype(BF16)

    x2 = x_ref[...] + _dot(mix_ref[...], wob_ref[...])
    x2_ref[...] = x2
    h = _rms(x2, gain_ref[...])
    h_hi, h_lo = _split_bf16(h)
    h2_ref[...] = h_hi
    lg = (_dot(h_hi, wrh_ref[...]) + _dot(h_lo, wrh_ref[...]) + _dot(h_hi, wrl_ref[...])
          + br_ref[...])

    lane = lax.broadcasted_iota(jnp.int32, (TM, LANES), 1).astype(F32)
    neg = -jnp.inf
    is_g = lane < N_GROUPS
    gl = jnp.where(is_g, lg, neg)
    gmax = jnp.max(gl, axis=1, keepdims=True)
    g_top = jnp.min(jnp.where(gl == gmax, lane, float(LANES)), axis=1, keepdims=True)
    g_w = 1.0 / jnp.sum(jnp.where(is_g, jnp.exp(lg - gmax), 0.0), axis=1, keepdims=True)
    first = N_GROUPS + EXPERTS_PER_GROUP * g_top
    el = jnp.where((lane >= first) & (lane < first + EXPERTS_PER_GROUP), lg, neg)
    m1 = jnp.max(el, axis=1, keepdims=True)
    i1 = jnp.min(jnp.where(el == m1, lane, float(LANES)), axis=1, keepdims=True)
    el2 = jnp.where(lane == i1, neg, el)
    m2 = jnp.max(el2, axis=1, keepdims=True)
    i2 = jnp.min(jnp.where(el2 == m2, lane, float(LANES)), axis=1, keepdims=True)
    ratio = jnp.exp(m2 - m1)
    w1 = g_w / (1.0 + ratio)
    w2 = w1 * ratio

    oh1 = jnp.where(lane == i1, 1.0, 0.0)
    oh2 = jnp.where(lane == i2, 1.0, 0.0)
    oh = oh1 + oh2
    before = _dot(tri_ref[...], oh.astype(BF16)) + cnt_ref[0:1, :]
    rank1 = jnp.sum(oh1 * before, axis=1, keepdims=True)
    rank2 = jnp.sum(oh2 * before, axis=1, keepdims=True)
    cnt_ref[...] = cnt_ref[...] + jnp.sum(oh, axis=0, keepdims=True)

    rec = jnp.zeros((TM, LANES), F32)
    for col, val in ((R_E, i1 - N_GROUPS), (R_E + 1, i2 - N_GROUPS), (R_RANK, rank1),
                     (R_RANK + 1, rank2), (R_W, w1), (R_W + 1, w2)):
        rec = jnp.where(lane == col, val, rec)
    route_ref[...] = rec
    route_t_ref[0] = rec.T[0:SUBLANES, :]


def _out_router(mixed, x, w_out, layer, gain, wr_hi, wr_lo, br):
    T = x.shape[0]
    row = lambda w: pl.BlockSpec((TM, w), lambda i: (i, 0))
    full = lambda shape: pl.BlockSpec(shape, lambda i: (0, 0))
    wo_spec = pl.BlockSpec((1, D_MODEL, D_MODEL), lambda i: (layer, 0, 0),
                           pipeline_mode=pl.Buffered(1))
    return pl.pallas_call(
        _out_router_kernel,
        grid=(T // TM,),
        in_specs=[row(D_MODEL), row(D_MODEL), wo_spec, full((1, D_MODEL)),
                  full((D_MODEL, ROUTER_COLS)), full((D_MODEL, ROUTER_COLS)), full((1, ROUTER_COLS))],
        out_specs=[row(D_MODEL), row(D_MODEL), row(LANES),
                   pl.BlockSpec((1, SUBLANES, TM), lambda i: (i, 0, 0)), full((8, LANES))],
        out_shape=[jax.ShapeDtypeStruct((T, D_MODEL), F32),
                   jax.ShapeDtypeStruct((T, D_MODEL), BF16),
                   jax.ShapeDtypeStruct((T, LANES), F32),
                   jax.ShapeDtypeStruct((T // TM, SUBLANES, TM), F32),
                   jax.ShapeDtypeStruct((8, LANES), F32)],
        scratch_shapes=[pltpu.VMEM((TM, TM), BF16), pltpu.VMEM((D_MODEL, D_MODEL), BF16)],
        compiler_params=pltpu.CompilerParams(
            dimension_semantics=("arbitrary",), vmem_limit_bytes=VMEM_LIMIT),
        name="out_router",
    )(mixed, x, w_out, gain, wr_hi, wr_lo, br)


def _dispatch_kernel(fill_ref, nu_ref, dest_ref, h_ref, xb_ref, zero_ref, sem_ref, zsem_ref,
                     stage_ref):
    i = pl.program_id(0)
    par = lax.rem(i, 2)

    @pl.when(i == 0)
    def _():
        zero_ref[...] = jnp.zeros_like(zero_ref)
        fills = [(fill_ref[e] >= 0, pltpu.make_async_copy(
            zero_ref, xb_ref.at[pl.ds(pl.multiple_of(jnp.maximum(fill_ref[e], 0), MOE_BLK), MOE_BLK)],
            zsem_ref)) for e in range(N_EXPERTS)]
        n_blocks = xb_ref.shape[0] // MOE_BLK
        fills += [(j >= nu_ref[0], pltpu.make_async_copy(
            zero_ref, xb_ref.at[pl.ds(j * MOE_BLK, MOE_BLK)], zsem_ref))
            for j in range(n_blocks - N_EXPERTS, n_blocks)]
        for cond, f in fills:
            pl.when(cond)(f.start)
        for cond, f in fills:
            pl.when(cond)(f.wait)

    _row_tiles_store(stage_ref.at[par], h_ref[...].astype(F32))
    for r in range(TM):
        for k in range(TOP_K):
            pltpu.make_async_copy(stage_ref.at[par, r], xb_ref.at[dest_ref[0, 0, k * TM + r]],
                                  sem_ref.at[par]).start(priority=k)

    def wait_tile(p):
        for _ in range(TOP_K):
            pltpu.make_async_copy(stage_ref.at[p], xb_ref.at[pl.ds(0, TM)], sem_ref.at[p]).wait()

    pl.when(i > 0)(lambda: wait_tile(1 - par))
    pl.when(i == pl.num_programs(0) - 1)(lambda: wait_tile(par))


def _dispatch(fill_start, n_used, dest_tiles, h2, n_rows):
    T = h2.shape[0]
    grid_spec = pltpu.PrefetchScalarGridSpec(
        num_scalar_prefetch=2,
        grid=(T // TM,),
        in_specs=[
            pl.BlockSpec((1, 1, TOP_K * TM), lambda i, fs, nu: (i, 0, 0), memory_space=pltpu.SMEM),
            pl.BlockSpec((TM, D_MODEL), lambda i, fs, nu: (i, 0)),
        ],
        out_specs=pl.BlockSpec(memory_space=pl.ANY),
        scratch_shapes=[pltpu.VMEM((MOE_BLK, ROW_TILES, LANES), F32),
                        pltpu.SemaphoreType.DMA((2,)), pltpu.SemaphoreType.DMA(()),
                        pltpu.VMEM((2, TM, ROW_TILES, LANES), F32)],
    )
    return pl.pallas_call(
        _dispatch_kernel,
        grid_spec=grid_spec,
        out_shape=jax.ShapeDtypeStruct((n_rows, ROW_TILES, LANES), F32),
        compiler_params=pltpu.CompilerParams(dimension_semantics=("arbitrary",)),
        name="dispatch",
    )(fill_start, n_used, dest_tiles, h2)


def _expert_kernel(be_ref, nxt_ref, nu_ref, xb_ref, wg_ref, wu_ref, wd_ref, yb_ref,
                   xbuf_ref, ybuf_ref, wgs_ref, wus_ref, wds_ref, wgb_ref, wub_ref, wdb_ref,
                   xsem_ref, ysem_ref, wsem_ref, *, layer, n_blocks):
    n_used = nu_ref[0]

    def x_copy(j, slot):
        return pltpu.make_async_copy(xb_ref.at[pl.ds(j * MOE_BLK, MOE_BLK)], xbuf_ref.at[slot],
                                     xsem_ref.at[slot])

    def y_copy(j, slot):
        return pltpu.make_async_copy(ybuf_ref.at[slot], yb_ref.at[pl.ds(j * MOE_BLK, MOE_BLK)],
                                     ysem_ref.at[slot])

    def w_copies(e, ws):
        return [pltpu.make_async_copy(src.at[layer, e], dst.at[ws], wsem_ref.at[ws])
                for src, dst in ((wg_ref, wgs_ref), (wu_ref, wus_ref), (wd_ref, wds_ref))]

    x_copy(0, 0).start()
    for c in w_copies(be_ref[0], 0):
        c.start()

    def block(j, ws):
        slot = lax.rem(j, 2)
        first = (j == 0) | (be_ref[j] != be_ref[jnp.maximum(j - 1, 0)])
        ws = jnp.where(first & (j > 0), 1 - ws, ws)

        @pl.when(first)
        def _():
            for c in w_copies(be_ref[j], ws):
                c.wait()
            for r0 in range(0, D_MODEL, W_PREP_ROWS):
                rows = slice(r0, r0 + W_PREP_ROWS)
                wgb_ref[rows, :] = wgs_ref[ws, rows, :].astype(BF16)
                wub_ref[rows, :] = wus_ref[ws, rows, :].astype(BF16)
            for r0 in range(0, D_EXPERT, W_PREP_ROWS):
                rows = slice(r0, r0 + W_PREP_ROWS)
                wdb_ref[rows, :] = wds_ref[ws, rows, :].astype(BF16)

            @pl.when(nxt_ref[j] >= 0)
            def _():
                for c in w_copies(nxt_ref[j], 1 - ws):
                    c.start()

        @pl.when(j + 1 < n_used)
        def _():
            x_copy(j + 1, 1 - slot).start()

        x_copy(j, slot).wait()
        x = jnp.concatenate(
            [c.astype(BF16) for c in _row_tiles_chunks(xbuf_ref.at[slot], MOE_BLK)], axis=1)
        g = _dot(x, wgb_ref[...])
        u = _dot(x, wub_ref[...])
        h = (g * (1.0 / (1.0 + jnp.exp(-g)))) * u
        y = _dot(h.astype(BF16), wdb_ref[...])

        @pl.when(j >= 2)
        def _():
            y_copy(j - 2, slot).wait()

        _row_tiles_store(ybuf_ref.at[slot], y)
        y_copy(j, slot).start()
        return ws

    lax.fori_loop(0, n_used, block, jnp.int32(0))

    y_copy(n_used - 2, lax.rem(n_used, 2)).wait()
    y_copy(n_used - 1, lax.rem(n_used - 1, 2)).wait()

    ybuf_ref[0] = jnp.zeros((MOE_BLK, ROW_TILES, LANES), F32)

    def fill(j, carry):
        y_copy(j, 0).start()
        return carry

    def fill_wait(j, carry):
        y_copy(j, 0).wait()
        return carry

    lax.fori_loop(n_used, n_blocks, fill, 0)
    lax.fori_loop(n_used, n_blocks, fill_wait, 0)


def _experts(blk_exp, nxt_exp, n_used, xb, w_gate, w_up, w_down, layer):
    n_blocks = blk_exp.shape[0]
    any_spec = pl.BlockSpec(memory_space=pl.ANY)
    blk = (MOE_BLK, ROW_TILES, LANES)
    grid_spec = pltpu.PrefetchScalarGridSpec(
        num_scalar_prefetch=3,
        grid=(1,),
        in_specs=[any_spec, any_spec, any_spec, any_spec],
        out_specs=any_spec,
        scratch_shapes=[
            pltpu.VMEM((2,) + blk, F32), pltpu.VMEM((2,) + blk, F32),
            pltpu.VMEM((2, D_MODEL, D_EXPERT), F32), pltpu.VMEM((2, D_MODEL, D_EXPERT), F32),
            pltpu.VMEM((2, D_EXPERT, D_MODEL), F32),
            pltpu.VMEM((D_MODEL, D_EXPERT), BF16), pltpu.VMEM((D_MODEL, D_EXPERT), BF16),
            pltpu.VMEM((D_EXPERT, D_MODEL), BF16),
            pltpu.SemaphoreType.DMA((2,)), pltpu.SemaphoreType.DMA((2,)),
            pltpu.SemaphoreType.DMA((2,)),
        ],
    )
    return pl.pallas_call(
        functools.partial(_expert_kernel, layer=layer, n_blocks=n_blocks),
        grid_spec=grid_spec,
        out_shape=jax.ShapeDtypeStruct((n_blocks * MOE_BLK, ROW_TILES, LANES), F32),
        compiler_params=pltpu.CompilerParams(
            dimension_semantics=("arbitrary",), vmem_limit_bytes=VMEM_LIMIT),
        name="experts",
    )(blk_exp, nxt_exp, n_used, xb, w_gate, w_up, w_down)


def _dispatch_tables(route_t, counts_rec, T):
    counts = counts_rec[0, N_GROUPS:N_GROUPS + N_EXPERTS].astype(jnp.int32)
    n_steps = (T * TOP_K) // MOE_BLK + N_EXPERTS
    nblk = (counts + MOE_BLK - 1) // MOE_BLK
    bend = jnp.cumsum(nblk)
    pstart = (bend - nblk) * MOE_BLK
    n_used = bend[-1]
    j = jnp.minimum(jnp.arange(n_steps, dtype=jnp.int32), n_used - 1)
    blk_exp = jnp.minimum(jnp.sum(j[:, None] >= bend[None, :], axis=1), N_EXPERTS - 1)
    n_rows = n_steps * MOE_BLK
    last_blk = jnp.where(counts > 0, (bend - 1) * MOE_BLK, -1)
    ids = jnp.arange(N_EXPERTS, dtype=jnp.int32)
    later = (ids[None, :] > ids[:, None]) & (nblk[None, :] > 0)
    nxt_of = jnp.min(jnp.where(later, ids[None, :], N_EXPERTS), axis=1)
    nxt_tab = jnp.where(nxt_of < N_EXPERTS, nxt_of, -1)
    nxt_exp = jnp.sum(jnp.where(blk_exp[:, None] == ids[None, :], nxt_tab[None, :], 0), axis=1)
    e = route_t[:, R_E:R_E + TOP_K, :].astype(jnp.int32)
    rank = route_t[:, R_RANK:R_RANK + TOP_K, :].astype(jnp.int32)
    seg = jnp.sum(jnp.where(e[..., None] == jnp.arange(N_EXPERTS), pstart, 0), axis=-1)
    dest = jnp.clip(seg + rank, 0, n_steps * MOE_BLK - 1)
    dest_tiles = dest.reshape(T // TM, 1, TOP_K * TM)
    return dict(dest_tiles=dest_tiles, fill_start=last_blk.astype(jnp.int32),
                blk_exp=blk_exp.astype(jnp.int32), nxt_exp=nxt_exp.astype(jnp.int32),
                n_used=n_used.reshape(1).astype(jnp.int32), n_rows=n_rows)


def _prep_layer(l, w_gk_up, b_gk, gla_norm, gmlp_norm, w_spatial, b_spatial, w_conv,
                w_router_group, b_router_group, w_router_expert, b_router_expert):
    wgk = jnp.concatenate(
        [w_gk_up[l], jnp.zeros((LANES - GLA_GATE_RANK, GLA_KDIM), F32)], axis=0).astype(BF16)
    wsp = w_spatial[l].transpose(1, 0, 2).reshape(GMLP_CHUNK, GMLP_HEADS * GMLP_CHUNK)
    bsp = jnp.repeat(b_spatial[l].T, GMLP_DH, axis=1)
    wconv = jnp.concatenate([w_conv[l], jnp.zeros((8 - CONV_K, CONV_WIDTH), F32)], axis=0)
    wr = jnp.concatenate(
        [w_router_group[l], w_router_expert[l],
         jnp.zeros((D_MODEL, ROUTER_COLS - N_GROUPS - N_EXPERTS), F32)], axis=1)
    wr_hi = wr.astype(BF16)
    wr_lo = (wr - wr_hi.astype(F32)).astype(BF16)
    br = jnp.concatenate(
        [b_router_group[l], b_router_expert[l],
         jnp.zeros((ROUTER_COLS - N_GROUPS - N_EXPERTS,), F32)])[None, :]
    return dict(
        wgk=wgk, bgk=b_gk[l][None, :], glan=gla_norm[l][None, :], gmn=gmlp_norm[l][None, :],
        wsp=wsp, bsp=bsp, wconv=wconv, wr_hi=wr_hi, wr_lo=wr_lo, br=br)


def kernel(x, attn_norm, w_in, w_gk_up, b_gk, gla_norm, gmlp_norm, w_spatial, b_spatial, w_conv, w_out, ffn_norm, w_router_group, b_router_group, w_router_expert, b_router_expert, w_gate, w_up, w_down, final_norm):
    B, S, D = x.shape
    T = B * S
    depth = w_in.shape[0]
    xr = x.reshape(T, D)
    moe = None
    for l in range(depth):
        p = _prep_layer(l, w_gk_up, b_gk, gla_norm, gmlp_norm, w_spatial, b_spatial, w_conv,
                        w_router_group, b_router_group, w_router_expert, b_router_expert)
        if moe is None:
            proj = _norm_proj(xr, attn_norm[l][None, :], w_in, l)
        else:
            xr, proj = _combine_norm_proj(moe["dest_tiles"], moe["x2"], moe["route"], moe["yb"],
                                          attn_norm[l][None, :], w_in, l)
        mixed = _mixers(proj, p["wgk"], p["bgk"], p["glan"], p["gmn"], p["wsp"], p["bsp"],
                        p["wconv"], B, S)
        x2, h2, route, route_t, counts_rec = _out_router(
            mixed, xr, w_out, l, ffn_norm[l][None, :], p["wr_hi"], p["wr_lo"], p["br"])
        moe = _dispatch_tables(route_t, counts_rec, T)
        xb = _dispatch(moe["fill_start"], moe["n_used"], moe["dest_tiles"], h2, moe["n_rows"])
        yb = _experts(moe["blk_exp"], moe["nxt_exp"], moe["n_used"], xb, w_gate, w_up, w_down, l)
        moe.update(x2=x2, route=route, yb=yb)
    out = _combine_final_norm(moe["dest_tiles"], moe["x2"], moe["route"], moe["yb"],
                              final_norm[None, :])
    return out.reshape(B, S, D)
```

```python
import functools

import jax
import jax.numpy as jnp
from jax import lax
from jax.experimental import pallas as pl
from jax.experimental.pallas import tpu as pltpu

F32 = jnp.float32
BF16 = jnp.bfloat16

D_MODEL = 1024
RMS_EPS = 1e-6
GLA_HEADS = 4
GLA_WIDTH = 512
GLA_DV = 128
GLA_DK = 64
GLA_KDIM = 256
GLA_GATE_RANK = 16
GLA_GATE_NORM = 16.0
GLA_CHUNK = 64
GMLP_HEADS = 4
GMLP_WIDTH = 256
GMLP_DH = 64
GMLP_CHUNK = 128
CONV_WIDTH = 256
CONV_K = 3
N_GROUPS = 4
EXPERTS_PER_GROUP = 8
N_EXPERTS = 32
TOP_K = 2
D_EXPERT = 256

LANES = 128
C_Q, C_K, C_V, C_G = 0, 256, 512, 1024
C_U, C_VG, C_X, C_BG, C_CG, C_GKL = 1536, 1792, 2048, 2304, 2560, 2816
D_PROJ = C_GKL + LANES
D_IN = C_GKL + GLA_GATE_RANK

TM = 256
TS_MIX = 256
MOE_BLK = 256
ROUTER_COLS = LANES
SUBLANES = 8
ROW_TILES = D_MODEL // LANES
assert ROW_TILES == SUBLANES
VMEM_LIMIT = 56 * 1024 * 1024
R_E, R_RANK, R_W = 0, 2, 4


def _dot(a, b):
    return jnp.dot(a, b, preferred_element_type=F32)


def _split_bf16(x):
    hi = x.astype(BF16)
    lo = (x - hi.astype(F32)).astype(BF16)
    return hi, lo


def _rms(x, gain):
    return x * lax.rsqrt(jnp.mean(x * x, axis=-1, keepdims=True) + RMS_EPS) * gain


W_PREP_ROWS = 128
PROJ_CHUNK = 256


def _stage_w_in(w_ref, wb_ref):
    for r0 in range(0, D_MODEL, W_PREP_ROWS):
        rows = slice(r0, r0 + W_PREP_ROWS)
        wb_ref[rows, 0:C_U] = w_ref[0, rows, 0:C_U].astype(BF16)
        wb_ref[rows, C_U:C_GKL] = w_ref[0, rows, C_U + GLA_GATE_RANK:D_IN].astype(BF16)
        low = w_ref[0, rows, C_U:C_U + GLA_GATE_RANK]
        wb_ref[rows, C_GKL:D_PROJ] = jnp.concatenate(
            [low, jnp.zeros((W_PREP_ROWS, LANES - GLA_GATE_RANK), F32)], axis=1).astype(BF16)


def _w_in_specs(layer):
    return pl.BlockSpec((1, D_MODEL, D_IN), lambda i: (layer, 0, 0), pipeline_mode=pl.Buffered(1))


def _norm_proj_kernel(x_ref, gain_ref, w_ref, proj_ref, wb_ref):
    @pl.when(pl.program_id(0) == 0)
    def _():
        _stage_w_in(w_ref, wb_ref)
    proj_ref[...] = _dot(_rms(x_ref[...], gain_ref[...]).astype(BF16), wb_ref[...]).astype(BF16)


def _norm_proj(x, gain, w_in, layer):
    T = x.shape[0]
    return pl.pallas_call(
        _norm_proj_kernel,
        grid=(T // TM,),
        in_specs=[
            pl.BlockSpec((TM, D_MODEL), lambda i: (i, 0)),
            pl.BlockSpec((1, D_MODEL), lambda i: (0, 0)),
            _w_in_specs(layer),
        ],
        out_specs=pl.BlockSpec((TM, D_PROJ), lambda i: (i, 0)),
        out_shape=jax.ShapeDtypeStruct((T, D_PROJ), BF16),
        scratch_shapes=[pltpu.VMEM((D_MODEL, D_PROJ), BF16)],
        compiler_params=pltpu.CompilerParams(
            dimension_semantics=("arbitrary",), vmem_limit_bytes=VMEM_LIMIT),
        name="norm_proj",
    )(x, gain, w_in)


def _row_gather_copy(yb_ref, buf_ref, sem_ref, slot, k, r, d):
    return pltpu.make_async_copy(yb_ref.at[d], buf_ref.at[slot, k, r], sem_ref.at[slot])


def _gather_start(dest_ref, yb_ref, buf_ref, sem_ref, slot, rows=range(TM)):
    for r in rows:
        for k in range(TOP_K):
            _row_gather_copy(yb_ref, buf_ref, sem_ref, slot, k, r,
                             dest_ref[0, 0, k * TM + r]).start(priority=k)


def _gather_wait(yb_ref, buf_ref, sem_ref, slot):
    for k in range(TOP_K):
        pltpu.make_async_copy(yb_ref.at[pl.ds(0, TM)], buf_ref.at[slot, k], sem_ref.at[slot]).wait()


def _combined_residual(dcur_ref, x_ref, route_ref, yb_ref, buf_ref, sem_ref):
    i = pl.program_id(0)
    slot = lax.rem(i, 2)

    @pl.when(i == 0)
    def _():
        _gather_start(dcur_ref, yb_ref, buf_ref, sem_ref, 0)

    _gather_wait(yb_ref, buf_ref, sem_ref, slot)
    w0 = route_ref[:, R_W:R_W + 1]
    w1 = route_ref[:, R_W + 1:R_W + 2]
    y0 = _row_tiles_chunks(buf_ref.at[slot, 0], TM)
    y1 = _row_tiles_chunks(buf_ref.at[slot, 1], TM)
    return jnp.concatenate(
        [x_ref[:, c * LANES:(c + 1) * LANES] + (w0 * y0[c] + w1 * y1[c]) for c in range(ROW_TILES)],
        axis=1)


def _prefetch_groups(n_groups):
    per = -(-TM // n_groups)
    return [range(g * per, min(TM, (g + 1) * per)) for g in range(n_groups)]


def _drain_last_prefetch(yb_ref, buf_ref, sem_ref):
    i = pl.program_id(0)

    @pl.when(i == pl.num_programs(0) - 1)
    def _():
        _gather_wait(yb_ref, buf_ref, sem_ref, 1 - lax.rem(i, 2))


def _combine_specs(n_tiles):
    smem_tile = lambda f: pl.BlockSpec((1, 1, TOP_K * TM), f, memory_space=pltpu.SMEM)
    return [
        smem_tile(lambda i: (i, 0, 0)),
        smem_tile(lambda i: (jnp.minimum(i + 1, n_tiles - 1), 0, 0)),
        pl.BlockSpec((TM, D_MODEL), lambda i: (i, 0)),
        pl.BlockSpec((TM, LANES), lambda i: (i, 0)),
        pl.BlockSpec(memory_space=pl.ANY),
    ]


_COMBINE_SCRATCH = [pltpu.VMEM((2, TOP_K, TM, ROW_TILES, LANES), F32),
                    pltpu.SemaphoreType.DMA((2,))]


def _combine_norm_proj_kernel(dcur_ref, dnxt_ref, x_ref, route_ref, yb_ref, gain_ref, w_ref,
                              xo_ref, proj_ref, buf_ref, sem_ref, wb_ref):
    @pl.when(pl.program_id(0) == 0)
    def _():
        _stage_w_in(w_ref, wb_ref)
    x = _combined_residual(dcur_ref, x_ref, route_ref, yb_ref, buf_ref, sem_ref)
    xo_ref[...] = x
    h = _rms(x, gain_ref[...]).astype(BF16)
    nxt = 1 - lax.rem(pl.program_id(0), 2)
    col_chunks = [(c0, min(c0 + PROJ_CHUNK, D_PROJ)) for c0 in range(0, D_PROJ, PROJ_CHUNK)]
    for rows, (c0, c1) in zip(_prefetch_groups(len(col_chunks)), col_chunks):
        _gather_start(dnxt_ref, yb_ref, buf_ref, sem_ref, nxt, rows)
        proj_ref[:, c0:c1] = _dot(h, wb_ref[:, c0:c1]).astype(BF16)
    _drain_last_prefetch(yb_ref, buf_ref, sem_ref)


def _combine_norm_proj(dest_tiles, x2, route, yb, gain, w_in, layer):
    T = x2.shape[0]
    n_tiles = T // TM
    return pl.pallas_call(
        _combine_norm_proj_kernel,
        grid=(n_tiles,),
        in_specs=_combine_specs(n_tiles) + [
            pl.BlockSpec((1, D_MODEL), lambda i: (0, 0)),
            _w_in_specs(layer),
        ],
        out_specs=[pl.BlockSpec((TM, D_MODEL), lambda i: (i, 0)),
                   pl.BlockSpec((TM, D_PROJ), lambda i: (i, 0))],
        out_shape=[jax.ShapeDtypeStruct((T, D_MODEL), F32),
                   jax.ShapeDtypeStruct((T, D_PROJ), BF16)],
        scratch_shapes=_COMBINE_SCRATCH + [pltpu.VMEM((D_MODEL, D_PROJ), BF16)],
        compiler_params=pltpu.CompilerParams(
            dimension_semantics=("arbitrary",), vmem_limit_bytes=VMEM_LIMIT),
        name="combine_norm_proj",
    )(dest_tiles, dest_tiles, x2, route, yb, gain, w_in)


def _combine_final_norm_kernel(dcur_ref, dnxt_ref, x_ref, route_ref, yb_ref, gain_ref,
                               o_ref, buf_ref, sem_ref):
    _gather_start(dnxt_ref, yb_ref, buf_ref, sem_ref, 1 - lax.rem(pl.program_id(0), 2))
    x = _combined_residual(dcur_ref, x_ref, route_ref, yb_ref, buf_ref, sem_ref)
    o_ref[...] = _rms(x, gain_ref[...])
    _drain_last_prefetch(yb_ref, buf_ref, sem_ref)


def _combine_final_norm(dest_tiles, x2, route, yb, gain):
    T = x2.shape[0]
    n_tiles = T // TM
    return pl.pallas_call(
        _combine_final_norm_kernel,
        grid=(n_tiles,),
        in_specs=_combine_specs(n_tiles) + [pl.BlockSpec((1, D_MODEL), lambda i: (0, 0))],
        out_specs=pl.BlockSpec((TM, D_MODEL), lambda i: (i, 0)),
        out_shape=jax.ShapeDtypeStruct((T, D_MODEL), F32),
        scratch_shapes=_COMBINE_SCRATCH,
        compiler_params=pltpu.CompilerParams(
            dimension_semantics=("arbitrary",), vmem_limit_bytes=VMEM_LIMIT),
        name="combine_final_norm",
    )(dest_tiles, dest_tiles, x2, route, yb, gain)


def _gelu_tanh(x):
    c = 0.7978845608028654
    return x * (0.5 * (1.0 + jnp.tanh(c * (x + 0.044715 * (x * x * x)))))


def _mixer_kernel(proj_ref, wgk_ref, bgk_ref, glan_ref, gmn_ref, wsp_ref, bsp_ref, wconv_ref,
                  out_ref, st_ref, hc_ref, lcat_ref, wm_ref):
    TS = TS_MIX
    n_gla = TS // GLA_CHUNK
    n_gm = TS // GMLP_CHUNK

    @pl.when(pl.program_id(1) == 0)
    def _():
        st_ref[...] = jnp.zeros_like(st_ref)
        hc_ref[...] = jnp.zeros_like(hc_ref)
        r = lax.broadcasted_iota(jnp.int32, (2 * TS, TS), 0)
        c = lax.broadcasted_iota(jnp.int32, (2 * TS, TS), 1)
        rr = jnp.where(r >= TS, r - TS, r)
        same = (rr // GLA_CHUNK) == (c // GLA_CHUNK)
        keep = same & ((r >= TS) | (c <= rr))
        lcat_ref[...] = jnp.where(keep, 1.0, 0.0).astype(BF16)
        t = lax.broadcasted_iota(jnp.int32, (GMLP_CHUNK, GMLP_HEADS * GMLP_CHUNK), 0)
        s = lax.broadcasted_iota(jnp.int32, (GMLP_CHUNK, GMLP_HEADS * GMLP_CHUNK), 1) % GMLP_CHUNK
        wm_ref[...] = jnp.where(s <= t, wsp_ref[...], 0.0).astype(BF16)

    lane256 = lax.broadcasted_iota(jnp.int32, (1, GLA_KDIM), 1)

    q = proj_ref[:, C_Q:C_Q + GLA_KDIM].astype(F32)
    k = proj_ref[:, C_K:C_K + GLA_KDIM].astype(F32)
    v_b = proj_ref[:, C_V:C_V + GLA_WIDTH]
    z = _dot(proj_ref[:, C_GKL:C_GKL + LANES], wgk_ref[...]) + bgk_ref[...]
    gk = (jnp.minimum(z, 0.0) - jnp.log1p(jnp.exp(-jnp.abs(z)))) * (1.0 / GLA_GATE_NORM)
    gk_hi, gk_lo = _split_bf16(gk)
    cs = _dot(lcat_ref[...], jnp.concatenate([gk_hi, gk_lo], axis=1))
    b = cs[:TS, :GLA_KDIM] + cs[:TS, GLA_KDIM:]
    bl = cs[TS:, :GLA_KDIM] + cs[TS:, GLA_KDIM:]
    q_dec = (q * (GLA_DK ** -0.5)) * jnp.exp(b)
    k_inv = (k * jnp.exp(-b)).astype(BF16)
    k_dec = (k * jnp.exp(bl - b)).astype(BF16)
    q_dec_b = q_dec.astype(BF16)

    zero_b = jnp.zeros_like(q_dec_b)
    q_stack = jnp.concatenate(
        [jnp.where((lane256 // GLA_DK) == h, q_dec_b, zero_b) for h in range(GLA_HEADS)], axis=0)
    scores = lax.dot_general(q_stack, k_inv, (((1,), (1,)), ((), ())),
                             preferred_element_type=F32)
    rt = lax.broadcasted_iota(jnp.int32, (TS, TS), 0)
    ct = lax.broadcasted_iota(jnp.int32, (TS, TS), 1)
    causal = ((rt // GLA_CHUNK) == (ct // GLA_CHUNK)) & (ct <= rt)
    o_heads = []
    for h in range(GLA_HEADS):
        p_h = jnp.where(causal, scores[h * TS:(h + 1) * TS, :], 0.0).astype(BF16)
        o_heads.append(_dot(p_h, v_b[:, h * GLA_DV:(h + 1) * GLA_DV]))

    sr = lax.broadcasted_iota(jnp.int32, (GLA_WIDTH, GLA_KDIM), 0) // GLA_DV
    sc = lax.broadcasted_iota(jnp.int32, (GLA_WIDTH, GLA_KDIM), 1) // GLA_DK
    bd_mask = sr == sc
    o_inter = []
    for c in range(n_gla):
        rows = slice(c * GLA_CHUNK, (c + 1) * GLA_CHUNK)
        st = st_ref[...]
        o_inter.append(lax.dot_general(q_dec_b[rows], st.astype(BF16), (((1,), (1,)), ((), ())),
                                       preferred_element_type=F32))
        upd = lax.dot_general(v_b[rows], k_dec[rows], (((0,), (0,)), ((), ())),
                              preferred_element_type=F32)
        decay = jnp.exp(bl[c * GLA_CHUNK:c * GLA_CHUNK + 1, :])
        st_ref[...] = st * decay + jnp.where(bd_mask, upd, 0.0)
    o_inter = jnp.concatenate(o_inter, axis=0)

    for h in range(GLA_HEADS):
        cols = slice(h * GLA_DV, (h + 1) * GLA_DV)
        o = o_heads[h] + o_inter[:, cols]
        o = o * lax.rsqrt(jnp.mean(o * o, axis=-1, keepdims=True) + RMS_EPS) * glan_ref[...]
        g = proj_ref[:, C_G + h * GLA_DV:C_G + (h + 1) * GLA_DV].astype(F32)
        out_ref[:, cols] = (o * (g * (1.0 / (1.0 + jnp.exp(-g))))).astype(out_ref.dtype)

    u = _gelu_tanh(proj_ref[:, C_U:C_U + GMLP_WIDTH].astype(F32))
    vg = _gelu_tanh(proj_ref[:, C_VG:C_VG + GMLP_WIDTH].astype(F32))
    hr = lax.broadcasted_iota(jnp.int32, (GMLP_WIDTH, GMLP_WIDTH), 0) // GMLP_DH
    hcn = lax.broadcasted_iota(jnp.int32, (GMLP_WIDTH, GMLP_WIDTH), 1) // GMLP_DH
    head_mean = jnp.where(hr == hcn, 1.0 / GMLP_DH, 0.0).astype(BF16)
    sq_hi, sq_lo = _split_bf16(vg * vg)
    ms = _dot(sq_hi, head_mean) + _dot(sq_lo, head_mean)
    v32 = vg * lax.rsqrt(ms + RMS_EPS) * gmn_ref[...]
    for c in range(n_gm):
        rows = slice(c * GMLP_CHUNK, (c + 1) * GMLP_CHUNK)
        vc = v32[rows].astype(BF16)
        zc = jnp.zeros_like(vc)
        rhs = jnp.concatenate(
            [jnp.where((lane256 // GMLP_DH) == h, vc, zc) for h in range(GMLP_HEADS)], axis=0)
        mixed = _dot(wm_ref[...], rhs) + bsp_ref[...]
        out_ref[rows, GLA_WIDTH:GLA_WIDTH + GMLP_WIDTH] = (u[rows] * mixed).astype(out_ref.dtype)

    hcv = (proj_ref[:, C_CG:C_CG + CONV_WIDTH].astype(F32)
           * proj_ref[:, C_X:C_X + CONV_WIDTH].astype(F32))
    hc_ref[8:8 + TS, :] = hcv
    y = (wconv_ref[2:3, :] * hcv + wconv_ref[1:2, :] * hc_ref[7:7 + TS, :]
         + wconv_ref[0:1, :] * hc_ref[6:6 + TS, :])
    out_ref[:, GLA_WIDTH + GMLP_WIDTH:] = (
        proj_ref[:, C_BG:C_BG + CONV_WIDTH].astype(F32) * y).astype(out_ref.dtype)
    hc_ref[0:8, :] = hc_ref[TS:TS + 8, :]


def _mixers(proj, wgk, bgk, glan, gmn, wsp, bsp, wconv, batch, seq):
    n_seq = seq // TS_MIX
    full = lambda shape: pl.BlockSpec(shape, lambda b, i: (0,) * len(shape))
    return pl.pallas_call(
        _mixer_kernel,
        grid=(batch, n_seq),
        in_specs=[
            pl.BlockSpec((TS_MIX, D_PROJ), lambda b, i: (b * n_seq + i, 0)),
            full((LANES, GLA_KDIM)), full((1, GLA_KDIM)), full((1, GLA_DV)), full((1, GMLP_WIDTH)),
            full((GMLP_CHUNK, GMLP_HEADS * GMLP_CHUNK)), full((GMLP_CHUNK, GMLP_WIDTH)),
            full((8, CONV_WIDTH)),
        ],
        out_specs=pl.BlockSpec((TS_MIX, D_MODEL), lambda b, i: (b * n_seq + i, 0)),
        out_shape=jax.ShapeDtypeStruct((batch * seq, D_MODEL), BF16),
        scratch_shapes=[
            pltpu.VMEM((GLA_WIDTH, GLA_KDIM), F32),
            pltpu.VMEM((TS_MIX + 8, CONV_WIDTH), F32),
            pltpu.VMEM((2 * TS_MIX, TS_MIX), BF16),
            pltpu.VMEM((GMLP_CHUNK, GMLP_HEADS * GMLP_CHUNK), BF16),
        ],
        compiler_params=pltpu.CompilerParams(
            dimension_semantics=("arbitrary", "arbitrary"), vmem_limit_bytes=VMEM_LIMIT),
        name="mixers",
    )(proj, wgk, bgk, glan, gmn, wsp, bsp, wconv)


def _row_tiles_store(tiles_ref, x):
    rows = x.shape[0]
    flat = tiles_ref.reshape(rows * ROW_TILES, LANES)
    for c in range(ROW_TILES):
        flat[pl.ds(c, rows, stride=ROW_TILES), :] = x[:, c * LANES:(c + 1) * LANES]


def _row_tiles_chunks(tiles_ref, rows):
    flat = tiles_ref.reshape(rows * ROW_TILES, LANES)
    return [flat[pl.ds(c, rows, stride=ROW_TILES), :] for c in range(ROW_TILES)]


def _out_router_kernel(mix_ref, x_ref, wo_ref, gain_ref, wrh_ref, wrl_ref, br_ref,
                       x2_ref, h2_ref, route_ref, route_t_ref, cnt_ref, tri_ref, wob_ref):
    @pl.when(pl.program_id(0) == 0)
    def _():
        cnt_ref[...] = jnp.zeros_like(cnt_ref)
        r = lax.broadcasted_iota(jnp.int32, (TM, TM), 0)
        c = lax.broadcasted_iota(jnp.int32, (TM, TM), 1)
        tri_ref[...] = jnp.where(c < r, 1.0, 0.0).astype(BF16)
        for r0 in range(0, D_MODEL, W_PREP_ROWS):
            wob_ref[r0:r0 + W_PREP_ROWS, :] = wo_ref[0, r0:r0 + W_PREP_ROWS, :].astype(BF16)

    x2 = x_ref[...] + _dot(mix_ref[...], wob_ref[...])
    x2_ref[...] = x2
    h = _rms(x2, gain_ref[...])
    h_hi, h_lo = _split_bf16(h)
    h2_ref[...] = h_hi
    lg = (_dot(h_hi, wrh_ref[...]) + _dot(h_lo, wrh_ref[...]) + _dot(h_hi, wrl_ref[...])
          + br_ref[...])

    lane = lax.broadcasted_iota(jnp.int32, (TM, LANES), 1).astype(F32)
    neg = -jnp.inf
    is_g = lane < N_GROUPS
    gl = jnp.where(is_g, lg, neg)
    gmax = jnp.max(gl, axis=1, keepdims=True)
    g_top = jnp.min(jnp.where(gl == gmax, lane, float(LANES)), axis=1, keepdims=True)
    g_w = 1.0 / jnp.sum(jnp.where(is_g, jnp.exp(lg - gmax), 0.0), axis=1, keepdims=True)
    first = N_GROUPS + EXPERTS_PER_GROUP * g_top
    el = jnp.where((lane >= first) & (lane < first + EXPERTS_PER_GROUP), lg, neg)
    m1 = jnp.max(el, axis=1, keepdims=True)
    i1 = jnp.min(jnp.where(el == m1, lane, float(LANES)), axis=1, keepdims=True)
    el2 = jnp.where(lane == i1, neg, el)
    m2 = jnp.max(el2, axis=1, keepdims=True)
    i2 = jnp.min(jnp.where(el2 == m2, lane, float(LANES)), axis=1, keepdims=True)
    ratio = jnp.exp(m2 - m1)
    w1 = g_w / (1.0 + ratio)
    w2 = w1 * ratio

    oh1 = jnp.where(lane == i1, 1.0, 0.0)
    oh2 = jnp.where(lane == i2, 1.0, 0.0)
    oh = oh1 + oh2
    before = _dot(tri_ref[...], oh.astype(BF16)) + cnt_ref[0:1, :]
    rank1 = jnp.sum(oh1 * before, axis=1, keepdims=True)
    rank2 = jnp.sum(oh2 * before, axis=1, keepdims=True)
    cnt_ref[...] = cnt_ref[...] + jnp.sum(oh, axis=0, keepdims=True)

    rec = jnp.zeros((TM, LANES), F32)
    for col, val in ((R_E, i1 - N_GROUPS), (R_E + 1, i2 - N_GROUPS), (R_RANK, rank1),
                     (R_RANK + 1, rank2), (R_W, w1), (R_W + 1, w2)):
        rec = jnp.where(lane == col, val, rec)
    route_ref[...] = rec
    route_t_ref[0] = rec.T[0:SUBLANES, :]


def _out_router(mixed, x, w_out, layer, gain, wr_hi, wr_lo, br):
    T = x.shape[0]
    row = lambda w: pl.BlockSpec((TM, w), lambda i: (i, 0))
    full = lambda shape: pl.BlockSpec(shape, lambda i: (0, 0))
    wo_spec = pl.BlockSpec((1, D_MODEL, D_MODEL), lambda i: (layer, 0, 0),
                           pipeline_mode=pl.Buffered(1))
    return pl.pallas_call(
        _out_router_kernel,
        grid=(T // TM,),
        in_specs=[row(D_MODEL), row(D_MODEL), wo_spec, full((1, D_MODEL)),
                  full((D_MODEL, ROUTER_COLS)), full((D_MODEL, ROUTER_COLS)), full((1, ROUTER_COLS))],
        out_specs=[row(D_MODEL), row(D_MODEL), row(LANES),
                   pl.BlockSpec((1, SUBLANES, TM), lambda i: (i, 0, 0)), full((8, LANES))],
        out_shape=[jax.ShapeDtypeStruct((T, D_MODEL), F32),
                   jax.ShapeDtypeStruct((T, D_MODEL), BF16),
                   jax.ShapeDtypeStruct((T, LANES), F32),
                   jax.ShapeDtypeStruct((T // TM, SUBLANES, TM), F32),
                   jax.ShapeDtypeStruct((8, LANES), F32)],
        scratch_shapes=[pltpu.VMEM((TM, TM), BF16), pltpu.VMEM((D_MODEL, D_MODEL), BF16)],
        compiler_params=pltpu.CompilerParams(
            dimension_semantics=("arbitrary",), vmem_limit_bytes=VMEM_LIMIT),
        name="out_router",
    )(mixed, x, w_out, gain, wr_hi, wr_lo, br)


def _dispatch_kernel(fill_ref, nu_ref, dest_ref, h_ref, xb_ref, zero_ref, sem_ref, zsem_ref,
                     stage_ref):
    i = pl.program_id(0)
    par = lax.rem(i, 2)

    @pl.when(i == 0)
    def _():
        zero_ref[...] = jnp.zeros_like(zero_ref)
        fills = [(fill_ref[e] >= 0, pltpu.make_async_copy(
            zero_ref, xb_ref.at[pl.ds(pl.multiple_of(jnp.maximum(fill_ref[e], 0), MOE_BLK), MOE_BLK)],
            zsem_ref)) for e in range(N_EXPERTS)]
        n_blocks = xb_ref.shape[0] // MOE_BLK
        fills += [(j >= nu_ref[0], pltpu.make_async_copy(
            zero_ref, xb_ref.at[pl.ds(j * MOE_BLK, MOE_BLK)], zsem_ref))
            for j in range(n_blocks - N_EXPERTS, n_blocks)]
        for cond, f in fills:
            pl.when(cond)(f.start)
        for cond, f in fills:
            pl.when(cond)(f.wait)

    _row_tiles_store(stage_ref.at[par], h_ref[...].astype(F32))
    for r in range(TM):
        for k in range(TOP_K):
            pltpu.make_async_copy(stage_ref.at[par, r], xb_ref.at[dest_ref[0, 0, k * TM + r]],
                                  sem_ref.at[par]).start(priority=k)

    def wait_tile(p):
        for _ in range(TOP_K):
            pltpu.make_async_copy(stage_ref.at[p], xb_ref.at[pl.ds(0, TM)], sem_ref.at[p]).wait()

    pl.when(i > 0)(lambda: wait_tile(1 - par))
    pl.when(i == pl.num_programs(0) - 1)(lambda: wait_tile(par))


def _dispatch(fill_start, n_used, dest_tiles, h2, n_rows):
    T = h2.shape[0]
    grid_spec = pltpu.PrefetchScalarGridSpec(
        num_scalar_prefetch=2,
        grid=(T // TM,),
        in_specs=[
            pl.BlockSpec((1, 1, TOP_K * TM), lambda i, fs, nu: (i, 0, 0), memory_space=pltpu.SMEM),
            pl.BlockSpec((TM, D_MODEL), lambda i, fs, nu: (i, 0)),
        ],
        out_specs=pl.BlockSpec(memory_space=pl.ANY),
        scratch_shapes=[pltpu.VMEM((MOE_BLK, ROW_TILES, LANES), F32),
                        pltpu.SemaphoreType.DMA((2,)), pltpu.SemaphoreType.DMA(()),
                        pltpu.VMEM((2, TM, ROW_TILES, LANES), F32)],
    )
    return pl.pallas_call(
        _dispatch_kernel,
        grid_spec=grid_spec,
        out_shape=jax.ShapeDtypeStruct((n_rows, ROW_TILES, LANES), F32),
        compiler_params=pltpu.CompilerParams(dimension_semantics=("arbitrary",)),
        name="dispatch",
    )(fill_start, n_used, dest_tiles, h2)


BLOCK_COPY_PARTS = 4
X_SLOTS = 3


class _CopyGroup:
    def __init__(self, copies):
        self.copies = copies

    def start(self):
        for n, c in enumerate(self.copies):
            c.start(priority=n % 2)

    def wait(self):
        for c in self.copies:
            c.wait()


def _expert_kernel(be_ref, nxt_ref, nu_ref, xb_ref, wg_ref, wu_ref, wd_ref, yb_ref,
                   xbuf_ref, ybuf_ref, wgs_ref, wus_ref, wds_ref, wgb_ref, wub_ref, wdb_ref,
                   xsem_ref, ysem_ref, wsem_ref, *, layer, n_blocks):
    n_used = nu_ref[0]

    part = MOE_BLK // BLOCK_COPY_PARTS

    def x_copy(j, slot):
        return _CopyGroup([pltpu.make_async_copy(
            xb_ref.at[pl.ds(j * MOE_BLK + p * part, part)],
            xbuf_ref.at[slot, pl.ds(p * part, part)], xsem_ref.at[slot])
            for p in range(BLOCK_COPY_PARTS)])

    def y_copy(j, slot):
        return _CopyGroup([pltpu.make_async_copy(
            ybuf_ref.at[slot, pl.ds(p * part, part)],
            yb_ref.at[pl.ds(j * MOE_BLK + p * part, part)], ysem_ref.at[slot])
            for p in range(BLOCK_COPY_PARTS)])

    def w_copies(e, ws):
        return [pltpu.make_async_copy(src.at[layer, e], dst.at[ws], wsem_ref.at[ws])
                for src, dst in ((wg_ref, wgs_ref), (wu_ref, wus_ref), (wd_ref, wds_ref))]

    x_copy(0, 0).start()
    x_copy(1, 1).start()
    for c in w_copies(be_ref[0], 0):
        c.start()

    def block(j, ws):
        slot = lax.rem(j, 2)
        xslot = lax.rem(j, X_SLOTS)
        first = (j == 0) | (be_ref[j] != be_ref[jnp.maximum(j - 1, 0)])
        ws = jnp.where(first & (j > 0), 1 - ws, ws)

        @pl.when(first)
        def _():
            for c in w_copies(be_ref[j], ws):
                c.wait()
            for r0 in range(0, D_MODEL, W_PREP_ROWS):
                rows = slice(r0, r0 + W_PREP_ROWS)
                wgb_ref[rows, :] = wgs_ref[ws, rows, :].astype(BF16)
                wub_ref[rows, :] = wus_ref[ws, rows, :].astype(BF16)
            for r0 in range(0, D_EXPERT, W_PREP_ROWS):
                rows = slice(r0, r0 + W_PREP_ROWS)
                wdb_ref[rows, :] = wds_ref[ws, rows, :].astype(BF16)

            @pl.when(nxt_ref[j] >= 0)
            def _():
                for c in w_copies(nxt_ref[j], 1 - ws):
                    c.start()

        @pl.when(j + 2 < n_used)
        def _():
            x_copy(j + 2, lax.rem(j + 2, X_SLOTS)).start()

        x_copy(j, xslot).wait()
        x = jnp.concatenate(
            [c.astype(BF16) for c in _row_tiles_chunks(xbuf_ref.at[xslot], MOE_BLK)], axis=1)
        g = _dot(x, wgb_ref[...])
        u = _dot(x, wub_ref[...])
        h = (g * (1.0 / (1.0 + jnp.exp(-g)))) * u
        y = _dot(h.astype(BF16), wdb_ref[...])

        @pl.when(j >= 2)
        def _():
            y_copy(j - 2, slot).wait()

        _row_tiles_store(ybuf_ref.at[slot], y)
        y_copy(j, slot).start()
        return ws

    lax.fori_loop(0, n_used, block, jnp.int32(0))

    y_copy(n_used - 2, lax.rem(n_used, 2)).wait()
    y_copy(n_used - 1, lax.rem(n_used - 1, 2)).wait()

    ybuf_ref[0] = jnp.zeros((MOE_BLK, ROW_TILES, LANES), F32)

    def fill(j, carry):
        y_copy(j, 0).start()
        return carry

    def fill_wait(j, carry):
        y_copy(j, 0).wait()
        return carry

    lax.fori_loop(n_used, n_blocks, fill, 0)
    lax.fori_loop(n_used, n_blocks, fill_wait, 0)


def _experts(blk_exp, nxt_exp, n_used, xb, w_gate, w_up, w_down, layer):
    n_blocks = blk_exp.shape[0]
    any_spec = pl.BlockSpec(memory_space=pl.ANY)
    blk = (MOE_BLK, ROW_TILES, LANES)
    grid_spec = pltpu.PrefetchScalarGridSpec(
        num_scalar_prefetch=3,
        grid=(1,),
        in_specs=[any_spec, any_spec, any_spec, any_spec],
        out_specs=any_spec,
        scratch_shapes=[
            pltpu.VMEM((X_SLOTS,) + blk, F32), pltpu.VMEM((2,) + blk, F32),
            pltpu.VMEM((2, D_MODEL, D_EXPERT), F32), pltpu.VMEM((2, D_MODEL, D_EXPERT), F32),
            pltpu.VMEM((2, D_EXPERT, D_MODEL), F32),
            pltpu.VMEM((D_MODEL, D_EXPERT), BF16), pltpu.VMEM((D_MODEL, D_EXPERT), BF16),
            pltpu.VMEM((D_EXPERT, D_MODEL), BF16),
            pltpu.SemaphoreType.DMA((X_SLOTS,)), pltpu.SemaphoreType.DMA((2,)),
            pltpu.SemaphoreType.DMA((2,)),
        ],
    )
    return pl.pallas_call(
        functools.partial(_expert_kernel, layer=layer, n_blocks=n_blocks),
        grid_spec=grid_spec,
        out_shape=jax.ShapeDtypeStruct((n_blocks * MOE_BLK, ROW_TILES, LANES), F32),
        compiler_params=pltpu.CompilerParams(
            dimension_semantics=("arbitrary",), vmem_limit_bytes=VMEM_LIMIT),
        name="experts",
    )(blk_exp, nxt_exp, n_used, xb, w_gate, w_up, w_down)


def _dispatch_tables(route_t, counts_rec, T):
    counts = counts_rec[0, N_GROUPS:N_GROUPS + N_EXPERTS].astype(jnp.int32)
    n_steps = (T * TOP_K) // MOE_BLK + N_EXPERTS
    nblk = (counts + MOE_BLK - 1) // MOE_BLK
    bend = jnp.cumsum(nblk)
    pstart = (bend - nblk) * MOE_BLK
    n_used = bend[-1]
    j = jnp.minimum(jnp.arange(n_steps, dtype=jnp.int32), n_used - 1)
    blk_exp = jnp.minimum(jnp.sum(j[:, None] >= bend[None, :], axis=1), N_EXPERTS - 1)
    n_rows = n_steps * MOE_BLK
    last_blk = jnp.where(counts > 0, (bend - 1) * MOE_BLK, -1)
    ids = jnp.arange(N_EXPERTS, dtype=jnp.int32)
    later = (ids[None, :] > ids[:, None]) & (nblk[None, :] > 0)
    nxt_of = jnp.min(jnp.where(later, ids[None, :], N_EXPERTS), axis=1)
    nxt_tab = jnp.where(nxt_of < N_EXPERTS, nxt_of, -1)
    nxt_exp = jnp.sum(jnp.where(blk_exp[:, None] == ids[None, :], nxt_tab[None, :], 0), axis=1)
    e = route_t[:, R_E:R_E + TOP_K, :].astype(jnp.int32)
    rank = route_t[:, R_RANK:R_RANK + TOP_K, :].astype(jnp.int32)
    seg = jnp.sum(jnp.where(e[..., None] == jnp.arange(N_EXPERTS), pstart, 0), axis=-1)
    dest = jnp.clip(seg + rank, 0, n_steps * MOE_BLK - 1)
    dest_tiles = dest.reshape(T // TM, 1, TOP_K * TM)
    return dict(dest_tiles=dest_tiles, fill_start=last_blk.astype(jnp.int32),
                blk_exp=blk_exp.astype(jnp.int32), nxt_exp=nxt_exp.astype(jnp.int32),
                n_used=n_used.reshape(1).astype(jnp.int32), n_rows=n_rows)


def _prep_layer(l, w_gk_up, b_gk, gla_norm, gmlp_norm, w_spatial, b_spatial, w_conv,
                w_router_group, b_router_group, w_router_expert, b_router_expert):
    wgk = jnp.concatenate(
        [w_gk_up[l], jnp.zeros((LANES - GLA_GATE_RANK, GLA_KDIM), F32)], axis=0).astype(BF16)
    wsp = w_spatial[l].transpose(1, 0, 2).reshape(GMLP_CHUNK, GMLP_HEADS * GMLP_CHUNK)
    bsp = jnp.repeat(b_spatial[l].T, GMLP_DH, axis=1)
    wconv = jnp.concatenate([w_conv[l], jnp.zeros((8 - CONV_K, CONV_WIDTH), F32)], axis=0)
    wr = jnp.concatenate(
        [w_router_group[l], w_router_expert[l],
         jnp.zeros((D_MODEL, ROUTER_COLS - N_GROUPS - N_EXPERTS), F32)], axis=1)
    wr_hi = wr.astype(BF16)
    wr_lo = (wr - wr_hi.astype(F32)).astype(BF16)
    br = jnp.concatenate(
        [b_router_group[l], b_router_expert[l],
         jnp.zeros((ROUTER_COLS - N_GROUPS - N_EXPERTS,), F32)])[None, :]
    return dict(
        wgk=wgk, bgk=b_gk[l][None, :], glan=gla_norm[l][None, :], gmn=gmlp_norm[l][None, :],
        wsp=wsp, bsp=bsp, wconv=wconv, wr_hi=wr_hi, wr_lo=wr_lo, br=br)


def kernel(x, attn_norm, w_in, w_gk_up, b_gk, gla_norm, gmlp_norm, w_spatial, b_spatial, w_conv, w_out, ffn_norm, w_router_group, b_router_group, w_router_expert, b_router_expert, w_gate, w_up, w_down, final_norm):
    B, S, D = x.shape
    T = B * S
    depth = w_in.shape[0]
    xr = x.reshape(T, D)
    moe = None
    for l in range(depth):
        p = _prep_layer(l, w_gk_up, b_gk, gla_norm, gmlp_norm, w_spatial, b_spatial, w_conv,
                        w_router_group, b_router_group, w_router_expert, b_router_expert)
        if moe is None:
            proj = _norm_proj(xr, attn_norm[l][None, :], w_in, l)
        else:
            xr, proj = _combine_norm_proj(moe["dest_tiles"], moe["x2"], moe["route"], moe["yb"],
                                          attn_norm[l][None, :], w_in, l)
        mixed = _mixers(proj, p["wgk"], p["bgk"], p["glan"], p["gmn"], p["wsp"], p["bsp"],
                        p["wconv"], B, S)
        x2, h2, route, route_t, counts_rec = _out_router(
            mixed, xr, w_out, l, ffn_norm[l][None, :], p["wr_hi"], p["wr_lo"], p["br"])
        moe = _dispatch_tables(route_t, counts_rec, T)
        xb = _dispatch(moe["fill_start"], moe["n_used"], moe["dest_tiles"], h2, moe["n_rows"])
        yb = _experts(moe["blk_exp"], moe["nxt_exp"], moe["n_used"], xb, w_gate, w_up, w_down, l)
        moe.update(x2=x2, route=route, yb=yb)
    out = _combine_final_norm(moe["dest_tiles"], moe["x2"], moe["route"], moe["yb"],
                              final_norm[None, :])
    return out.reshape(B, S, D)
```

```python
import functools

import jax
import jax.numpy as jnp
from jax import lax
from jax.experimental import pallas as pl
from jax.experimental.pallas import tpu as pltpu

F32 = jnp.float32
BF16 = jnp.bfloat16

D_MODEL = 1024
RMS_EPS = 1e-6
GLA_HEADS = 4
GLA_WIDTH = 512
GLA_DV = 128
GLA_DK = 64
GLA_KDIM = 256
GLA_GATE_RANK = 16
GLA_GATE_NORM = 16.0
GLA_CHUNK = 64
GMLP_HEADS = 4
GMLP_WIDTH = 256
GMLP_DH = 64
GMLP_CHUNK = 128
CONV_WIDTH = 256
CONV_K = 3
N_GROUPS = 4
EXPERTS_PER_GROUP = 8
N_EXPERTS = 32
TOP_K = 2
D_EXPERT = 256

LANES = 128
C_Q, C_K, C_V, C_G = 0, 256, 512, 1024
C_U, C_VG, C_X, C_BG, C_CG, C_GKL = 1536, 1792, 2048, 2304, 2560, 2816
D_PROJ = C_GKL + LANES
D_IN = C_GKL + GLA_GATE_RANK

TM = 256
TS_MIX = 256
MOE_BLK = 256
ROUTER_COLS = LANES
SUBLANES = 8
ROW_TILES = D_MODEL // LANES
assert ROW_TILES == SUBLANES
VMEM_LIMIT = 56 * 1024 * 1024
R_E, R_RANK, R_W = 0, 2, 4


def _dot(a, b):
    return jnp.dot(a, b, preferred_element_type=F32)


def _split_bf16(x):
    hi = x.astype(BF16)
    lo = (x - hi.astype(F32)).astype(BF16)
    return hi, lo


def _rms(x, gain):
    return x * lax.rsqrt(jnp.mean(x * x, axis=-1, keepdims=True) + RMS_EPS) * gain


W_PREP_ROWS = 128
PROJ_CHUNK = 256


def _stage_w_in(w_ref, wb_ref):
    for r0 in range(0, D_MODEL, W_PREP_ROWS):
        rows = slice(r0, r0 + W_PREP_ROWS)
        wb_ref[rows, 0:C_U] = w_ref[0, rows, 0:C_U].astype(BF16)
        wb_ref[rows, C_U:C_GKL] = w_ref[0, rows, C_U + GLA_GATE_RANK:D_IN].astype(BF16)
        low = w_ref[0, rows, C_U:C_U + GLA_GATE_RANK]
        wb_ref[rows, C_GKL:D_PROJ] = jnp.concatenate(
            [low, jnp.zeros((W_PREP_ROWS, LANES - GLA_GATE_RANK), F32)], axis=1).astype(BF16)


def _w_in_specs(layer):
    return pl.BlockSpec((1, D_MODEL, D_IN), lambda i: (layer, 0, 0), pipeline_mode=pl.Buffered(1))


def _norm_proj_kernel(x_ref, gain_ref, w_ref, proj_ref, wb_ref):
    @pl.when(pl.program_id(0) == 0)
    def _():
        _stage_w_in(w_ref, wb_ref)
    proj_ref[...] = _dot(_rms(x_ref[...], gain_ref[...]).astype(BF16), wb_ref[...]).astype(BF16)


def _norm_proj(x, gain, w_in, layer):
    T = x.shape[0]
    return pl.pallas_call(
        _norm_proj_kernel,
        grid=(T // TM,),
        in_specs=[
            pl.BlockSpec((TM, D_MODEL), lambda i: (i, 0)),
            pl.BlockSpec((1, D_MODEL), lambda i: (0, 0)),
            _w_in_specs(layer),
        ],
        out_specs=pl.BlockSpec((TM, D_PROJ), lambda i: (i, 0)),
        out_shape=jax.ShapeDtypeStruct((T, D_PROJ), BF16),
        scratch_shapes=[pltpu.VMEM((D_MODEL, D_PROJ), BF16)],
        compiler_params=pltpu.CompilerParams(
            dimension_semantics=("arbitrary",), vmem_limit_bytes=VMEM_LIMIT),
        name="norm_proj",
    )(x, gain, w_in)


def _row_gather_copy(yb_ref, buf_ref, sem_ref, slot, k, r, d):
    return pltpu.make_async_copy(yb_ref.at[d], buf_ref.at[slot, k, r], sem_ref.at[slot])


def _gather_start(dest_ref, yb_ref, buf_ref, sem_ref, slot, rows=range(TM)):
    for r in rows:
        for k in range(TOP_K):
            _row_gather_copy(yb_ref, buf_ref, sem_ref, slot, k, r,
                             dest_ref[0, 0, k * TM + r]).start(priority=k)


def _gather_wait(yb_ref, buf_ref, sem_ref, slot):
    for k in range(TOP_K):
        pltpu.make_async_copy(yb_ref.at[pl.ds(0, TM)], buf_ref.at[slot, k], sem_ref.at[slot]).wait()


def _combined_residual(dcur_ref, x_ref, route_ref, yb_ref, buf_ref, sem_ref):
    i = pl.program_id(0)
    slot = lax.rem(i, 2)

    @pl.when(i == 0)
    def _():
        _gather_start(dcur_ref, yb_ref, buf_ref, sem_ref, 0)

    _gather_wait(yb_ref, buf_ref, sem_ref, slot)
    w0 = route_ref[:, R_W:R_W + 1]
    w1 = route_ref[:, R_W + 1:R_W + 2]
    y0 = _row_tiles_chunks(buf_ref.at[slot, 0], TM)
    y1 = _row_tiles_chunks(buf_ref.at[slot, 1], TM)
    return jnp.concatenate(
        [x_ref[:, c * LANES:(c + 1) * LANES] + (w0 * y0[c] + w1 * y1[c]) for c in range(ROW_TILES)],
        axis=1)


def _prefetch_groups(n_groups):
    per = -(-TM // n_groups)
    return [range(g * per, min(TM, (g + 1) * per)) for g in range(n_groups)]


def _drain_last_prefetch(yb_ref, buf_ref, sem_ref):
    i = pl.program_id(0)

    @pl.when(i == pl.num_programs(0) - 1)
    def _():
        _gather_wait(yb_ref, buf_ref, sem_ref, 1 - lax.rem(i, 2))


def _combine_specs(n_tiles):
    smem_tile = lambda f: pl.BlockSpec((1, 1, TOP_K * TM), f, memory_space=pltpu.SMEM)
    return [
        smem_tile(lambda i: (i, 0, 0)),
        smem_tile(lambda i: (jnp.minimum(i + 1, n_tiles - 1), 0, 0)),
        pl.BlockSpec((TM, D_MODEL), lambda i: (i, 0)),
        pl.BlockSpec((TM, LANES), lambda i: (i, 0)),
        pl.BlockSpec(memory_space=pl.ANY),
    ]


_COMBINE_SCRATCH = [pltpu.VMEM((2, TOP_K, TM, ROW_TILES, LANES), F32),
                    pltpu.SemaphoreType.DMA((2,))]


def _combine_norm_proj_kernel(dcur_ref, dnxt_ref, x_ref, route_ref, yb_ref, gain_ref, w_ref,
                              xo_ref, proj_ref, buf_ref, sem_ref, wb_ref):
    @pl.when(pl.program_id(0) == 0)
    def _():
        _stage_w_in(w_ref, wb_ref)
    x = _combined_residual(dcur_ref, x_ref, route_ref, yb_ref, buf_ref, sem_ref)
    xo_ref[...] = x
    h = _rms(x, gain_ref[...]).astype(BF16)
    nxt = 1 - lax.rem(pl.program_id(0), 2)
    col_chunks = [(c0, min(c0 + PROJ_CHUNK, D_PROJ)) for c0 in range(0, D_PROJ, PROJ_CHUNK)]
    for rows, (c0, c1) in zip(_prefetch_groups(len(col_chunks)), col_chunks):
        _gather_start(dnxt_ref, yb_ref, buf_ref, sem_ref, nxt, rows)
        proj_ref[:, c0:c1] = _dot(h, wb_ref[:, c0:c1]).astype(BF16)
    _drain_last_prefetch(yb_ref, buf_ref, sem_ref)


def _combine_norm_proj(dest_tiles, x2, route, yb, gain, w_in, layer):
    T = x2.shape[0]
    n_tiles = T // TM
    return pl.pallas_call(
        _combine_norm_proj_kernel,
        grid=(n_tiles,),
        in_specs=_combine_specs(n_tiles) + [
            pl.BlockSpec((1, D_MODEL), lambda i: (0, 0)),
            _w_in_specs(layer),
        ],
        out_specs=[pl.BlockSpec((TM, D_MODEL), lambda i: (i, 0)),
                   pl.BlockSpec((TM, D_PROJ), lambda i: (i, 0))],
        out_shape=[jax.ShapeDtypeStruct((T, D_MODEL), F32),
                   jax.ShapeDtypeStruct((T, D_PROJ), BF16)],
        scratch_shapes=_COMBINE_SCRATCH + [pltpu.VMEM((D_MODEL, D_PROJ), BF16)],
        compiler_params=pltpu.CompilerParams(
            dimension_semantics=("arbitrary",), vmem_limit_bytes=VMEM_LIMIT),
        name="combine_norm_proj",
    )(dest_tiles, dest_tiles, x2, route, yb, gain, w_in)


def _combine_final_norm_kernel(dcur_ref, dnxt_ref, x_ref, route_ref, yb_ref, gain_ref,
                               o_ref, buf_ref, sem_ref):
    _gather_start(dnxt_ref, yb_ref, buf_ref, sem_ref, 1 - lax.rem(pl.program_id(0), 2))
    x = _combined_residual(dcur_ref, x_ref, route_ref, yb_ref, buf_ref, sem_ref)
    o_ref[...] = _rms(x, gain_ref[...])
    _drain_last_prefetch(yb_ref, buf_ref, sem_ref)


def _combine_final_norm(dest_tiles, x2, route, yb, gain):
    T = x2.shape[0]
    n_tiles = T // TM
    return pl.pallas_call(
        _combine_final_norm_kernel,
        grid=(n_tiles,),
        in_specs=_combine_specs(n_tiles) + [pl.BlockSpec((1, D_MODEL), lambda i: (0, 0))],
        out_specs=pl.BlockSpec((TM, D_MODEL), lambda i: (i, 0)),
        out_shape=jax.ShapeDtypeStruct((T, D_MODEL), F32),
        scratch_shapes=_COMBINE_SCRATCH,
        compiler_params=pltpu.CompilerParams(
            dimension_semantics=("arbitrary",), vmem_limit_bytes=VMEM_LIMIT),
        name="combine_final_norm",
    )(dest_tiles, dest_tiles, x2, route, yb, gain)


def _gelu_tanh(x):
    c = 0.7978845608028654
    return x * (0.5 * (1.0 + jnp.tanh(c * (x + 0.044715 * (x * x * x)))))


def _mixer_kernel(proj_ref, wgk_ref, bgk_ref, glan_ref, gmn_ref, wsp_ref, bsp_ref, wconv_ref,
                  out_ref, st_ref, hc_ref, lcat_ref, wm_ref):
    TS = TS_MIX
    n_gla = TS // GLA_CHUNK
    n_gm = TS // GMLP_CHUNK

    @pl.when(pl.program_id(1) == 0)
    def _():
        st_ref[...] = jnp.zeros_like(st_ref)
        hc_ref[...] = jnp.zeros_like(hc_ref)
        r = lax.broadcasted_iota(jnp.int32, (2 * TS, TS), 0)
        c = lax.broadcasted_iota(jnp.int32, (2 * TS, TS), 1)
        rr = jnp.where(r >= TS, r - TS, r)
        same = (rr // GLA_CHUNK) == (c // GLA_CHUNK)
        keep = same & ((r >= TS) | (c <= rr))
        lcat_ref[...] = jnp.where(keep, 1.0, 0.0).astype(BF16)
        t = lax.broadcasted_iota(jnp.int32, (GMLP_CHUNK, GMLP_HEADS * GMLP_CHUNK), 0)
        s = lax.broadcasted_iota(jnp.int32, (GMLP_CHUNK, GMLP_HEADS * GMLP_CHUNK), 1) % GMLP_CHUNK
        wm_ref[...] = jnp.where(s <= t, wsp_ref[...], 0.0).astype(BF16)

    lane256 = lax.broadcasted_iota(jnp.int32, (1, GLA_KDIM), 1)

    q = proj_ref[:, C_Q:C_Q + GLA_KDIM].astype(F32)
    k = proj_ref[:, C_K:C_K + GLA_KDIM].astype(F32)
    v_b = proj_ref[:, C_V:C_V + GLA_WIDTH]
    z = _dot(proj_ref[:, C_GKL:C_GKL + LANES], wgk_ref[...]) + bgk_ref[...]
    gk = (jnp.minimum(z, 0.0) - jnp.log1p(jnp.exp(-jnp.abs(z)))) * (1.0 / GLA_GATE_NORM)
    gk_hi, gk_lo = _split_bf16(gk)
    cs = _dot(lcat_ref[...], jnp.concatenate([gk_hi, gk_lo], axis=1))
    b = cs[:TS, :GLA_KDIM] + cs[:TS, GLA_KDIM:]
    bl = cs[TS:, :GLA_KDIM] + cs[TS:, GLA_KDIM:]
    q_dec = (q * (GLA_DK ** -0.5)) * jnp.exp(b)
    k_inv = (k * jnp.exp(-b)).astype(BF16)
    k_dec = (k * jnp.exp(bl - b)).astype(BF16)
    q_dec_b = q_dec.astype(BF16)

    zero_b = jnp.zeros_like(q_dec_b)
    q_stack = jnp.concatenate(
        [jnp.where((lane256 // GLA_DK) == h, q_dec_b, zero_b) for h in range(GLA_HEADS)], axis=0)
    scores = lax.dot_general(q_stack, k_inv, (((1,), (1,)), ((), ())),
                             preferred_element_type=F32)
    rt = lax.broadcasted_iota(jnp.int32, (TS, TS), 0)
    ct = lax.broadcasted_iota(jnp.int32, (TS, TS), 1)
    causal = ((rt // GLA_CHUNK) == (ct // GLA_CHUNK)) & (ct <= rt)
    o_heads = []
    for h in range(GLA_HEADS):
        p_h = jnp.where(causal, scores[h * TS:(h + 1) * TS, :], 0.0).astype(BF16)
        o_heads.append(_dot(p_h, v_b[:, h * GLA_DV:(h + 1) * GLA_DV]))

    sr = lax.broadcasted_iota(jnp.int32, (GLA_WIDTH, GLA_KDIM), 0) // GLA_DV
    sc = lax.broadcasted_iota(jnp.int32, (GLA_WIDTH, GLA_KDIM), 1) // GLA_DK
    bd_mask = sr == sc
    o_inter = []
    for c in range(n_gla):
        rows = slice(c * GLA_CHUNK, (c + 1) * GLA_CHUNK)
        st = st_ref[...]
        o_inter.append(lax.dot_general(q_dec_b[rows], st.astype(BF16), (((1,), (1,)), ((), ())),
                                       preferred_element_type=F32))
        upd = lax.dot_general(v_b[rows], k_dec[rows], (((0,), (0,)), ((), ())),
                              preferred_element_type=F32)
        decay = jnp.exp(bl[c * GLA_CHUNK:c * GLA_CHUNK + 1, :])
        st_ref[...] = st * decay + jnp.where(bd_mask, upd, 0.0)
    o_inter = jnp.concatenate(o_inter, axis=0)

    for h in range(GLA_HEADS):
        cols = slice(h * GLA_DV, (h + 1) * GLA_DV)
        o = o_heads[h] + o_inter[:, cols]
        o = o * lax.rsqrt(jnp.mean(o * o, axis=-1, keepdims=True) + RMS_EPS) * glan_ref[...]
        g = proj_ref[:, C_G + h * GLA_DV:C_G + (h + 1) * GLA_DV].astype(F32)
        out_ref[:, cols] = (o * (g * (1.0 / (1.0 + jnp.exp(-g))))).astype(out_ref.dtype)

    u = _gelu_tanh(proj_ref[:, C_U:C_U + GMLP_WIDTH].astype(F32))
    vg = _gelu_tanh(proj_ref[:, C_VG:C_VG + GMLP_WIDTH].astype(F32))
    hr = lax.broadcasted_iota(jnp.int32, (GMLP_WIDTH, GMLP_WIDTH), 0) // GMLP_DH
    hcn = lax.broadcasted_iota(jnp.int32, (GMLP_WIDTH, GMLP_WIDTH), 1) // GMLP_DH
    head_mean = jnp.where(hr == hcn, 1.0 / GMLP_DH, 0.0).astype(BF16)
    sq_hi, sq_lo = _split_bf16(vg * vg)
    ms = _dot(sq_hi, head_mean) + _dot(sq_lo, head_mean)
    v32 = vg * lax.rsqrt(ms + RMS_EPS) * gmn_ref[...]
    for c in range(n_gm):
        rows = slice(c * GMLP_CHUNK, (c + 1) * GMLP_CHUNK)
        vc = v32[rows].astype(BF16)
        zc = jnp.zeros_like(vc)
        rhs = jnp.concatenate(
            [jnp.where((lane256 // GMLP_DH) == h, vc, zc) for h in range(GMLP_HEADS)], axis=0)
        mixed = _dot(wm_ref[...], rhs) + bsp_ref[...]
        out_ref[rows, GLA_WIDTH:GLA_WIDTH + GMLP_WIDTH] = (u[rows] * mixed).astype(out_ref.dtype)

    hcv = (proj_ref[:, C_CG:C_CG + CONV_WIDTH].astype(F32)
           * proj_ref[:, C_X:C_X + CONV_WIDTH].astype(F32))
    hc_ref[8:8 + TS, :] = hcv
    y = (wconv_ref[2:3, :] * hcv + wconv_ref[1:2, :] * hc_ref[7:7 + TS, :]
         + wconv_ref[0:1, :] * hc_ref[6:6 + TS, :])
    out_ref[:, GLA_WIDTH + GMLP_WIDTH:] = (
        proj_ref[:, C_BG:C_BG + CONV_WIDTH].astype(F32) * y).astype(out_ref.dtype)
    hc_ref[0:8, :] = hc_ref[TS:TS + 8, :]


def _mixers(proj, wgk, bgk, glan, gmn, wsp, bsp, wconv, batch, seq):
    n_seq = seq // TS_MIX
    full = lambda shape: pl.BlockSpec(shape, lambda b, i: (0,) * len(shape))
    return pl.pallas_call(
        _mixer_kernel,
        grid=(batch, n_seq),
        in_specs=[
            pl.BlockSpec((TS_MIX, D_PROJ), lambda b, i: (b * n_seq + i, 0)),
            full((LANES, GLA_KDIM)), full((1, GLA_KDIM)), full((1, GLA_DV)), full((1, GMLP_WIDTH)),
            full((GMLP_CHUNK, GMLP_HEADS * GMLP_CHUNK)), full((GMLP_CHUNK, GMLP_WIDTH)),
            full((8, CONV_WIDTH)),
        ],
        out_specs=pl.BlockSpec((TS_MIX, D_MODEL), lambda b, i: (b * n_seq + i, 0)),
        out_shape=jax.ShapeDtypeStruct((batch * seq, D_MODEL), BF16),
        scratch_shapes=[
            pltpu.VMEM((GLA_WIDTH, GLA_KDIM), F32),
            pltpu.VMEM((TS_MIX + 8, CONV_WIDTH), F32),
            pltpu.VMEM((2 * TS_MIX, TS_MIX), BF16),
            pltpu.VMEM((GMLP_CHUNK, GMLP_HEADS * GMLP_CHUNK), BF16),
        ],
        compiler_params=pltpu.CompilerParams(
            dimension_semantics=("arbitrary", "arbitrary"), vmem_limit_bytes=VMEM_LIMIT),
        name="mixers",
    )(proj, wgk, bgk, glan, gmn, wsp, bsp, wconv)


def _row_tiles_store(tiles_ref, x):
    rows = x.shape[0]
    flat = tiles_ref.reshape(rows * ROW_TILES, LANES)
    for c in range(ROW_TILES):
        flat[pl.ds(c, rows, stride=ROW_TILES), :] = x[:, c * LANES:(c + 1) * LANES]


def _row_tiles_chunks(tiles_ref, rows):
    flat = tiles_ref.reshape(rows * ROW_TILES, LANES)
    return [flat[pl.ds(c, rows, stride=ROW_TILES), :] for c in range(ROW_TILES)]


def _out_router_kernel(mix_ref, x_ref, wo_ref, gain_ref, wrc_ref, br_ref,
                       x2_ref, h2_ref, route_ref, route_t_ref, cnt_ref, tri_ref, wob_ref, lg_ref):
    i = pl.program_id(0)

    @pl.when(i == 0)
    def _():
        cnt_ref[...] = jnp.zeros_like(cnt_ref)
        lg_ref[...] = jnp.zeros_like(lg_ref)
        r = lax.broadcasted_iota(jnp.int32, (TM, TM), 0)
        c = lax.broadcasted_iota(jnp.int32, (TM, TM), 1)
        tri_ref[...] = jnp.where(c < r, 1.0, 0.0).astype(BF16)
        for r0 in range(0, D_MODEL, W_PREP_ROWS):
            wob_ref[r0:r0 + W_PREP_ROWS, :] = wo_ref[0, r0:r0 + W_PREP_ROWS, :].astype(BF16)

    lg = lg_ref[...]
    x2 = x_ref[...] + _dot(mix_ref[...], wob_ref[...])
    x2_ref[...] = x2
    h = _rms(x2, gain_ref[...])
    h_hi, h_lo = _split_bf16(h)
    h2_ref[...] = h_hi
    hh_hl = _dot(h_hi, wrc_ref[...])
    lg_ref[...] = (hh_hl[:, :ROUTER_COLS] + hh_hl[:, ROUTER_COLS:]
                   + _dot(h_lo, wrc_ref[:, :ROUTER_COLS]) + br_ref[...])

    lane = lax.broadcasted_iota(jnp.int32, (TM, LANES), 1).astype(F32)
    neg = -jnp.inf
    is_g = lane < N_GROUPS
    gl = jnp.where(is_g, lg, neg)
    gmax = jnp.max(gl, axis=1, keepdims=True)
    g_top = jnp.min(jnp.where(gl == gmax, lane, float(LANES)), axis=1, keepdims=True)
    g_w = 1.0 / jnp.sum(jnp.where(is_g, jnp.exp(lg - gmax), 0.0), axis=1, keepdims=True)
    first = N_GROUPS + EXPERTS_PER_GROUP * g_top
    el = jnp.where((lane >= first) & (lane < first + EXPERTS_PER_GROUP), lg, neg)
    m1 = jnp.max(el, axis=1, keepdims=True)
    i1 = jnp.min(jnp.where(el == m1, lane, float(LANES)), axis=1, keepdims=True)
    el2 = jnp.where(lane == i1, neg, el)
    m2 = jnp.max(el2, axis=1, keepdims=True)
    i2 = jnp.min(jnp.where(el2 == m2, lane, float(LANES)), axis=1, keepdims=True)
    ratio = jnp.exp(m2 - m1)
    w1 = g_w / (1.0 + ratio)
    w2 = w1 * ratio

    oh1 = jnp.where(lane == i1, 1.0, 0.0)
    oh2 = jnp.where(lane == i2, 1.0, 0.0)
    oh = jnp.where(i > 0, oh1 + oh2, 0.0)
    before = _dot(tri_ref[...], oh.astype(BF16)) + cnt_ref[0:1, :]
    rank1 = jnp.sum(oh1 * before, axis=1, keepdims=True)
    rank2 = jnp.sum(oh2 * before, axis=1, keepdims=True)
    cnt_ref[...] = cnt_ref[...] + jnp.sum(oh, axis=0, keepdims=True)

    rec = jnp.zeros((TM, LANES), F32)
    for col, val in ((R_E, i1 - N_GROUPS), (R_E + 1, i2 - N_GROUPS), (R_RANK, rank1),
                     (R_RANK + 1, rank2), (R_W, w1), (R_W + 1, w2)):
        rec = jnp.where(lane == col, val, rec)
    route_ref[...] = rec
    route_t_ref[0] = rec.T[0:SUBLANES, :]


def _out_router(mixed, x, w_out, layer, gain, wr_cat, br):
    T = x.shape[0]
    n = T // TM
    row = lambda w: pl.BlockSpec((TM, w), lambda i: (jnp.minimum(i, n - 1), 0))
    lag = lambda i: jnp.maximum(i - 1, 0)
    full = lambda shape: pl.BlockSpec(shape, lambda i: (0, 0))
    wo_spec = pl.BlockSpec((1, D_MODEL, D_MODEL), lambda i: (layer, 0, 0),
                           pipeline_mode=pl.Buffered(1))
    return pl.pallas_call(
        _out_router_kernel,
        grid=(n + 1,),
        in_specs=[row(D_MODEL), row(D_MODEL), wo_spec, full((1, D_MODEL)),
                  full((D_MODEL, 2 * ROUTER_COLS)), full((1, ROUTER_COLS))],
        out_specs=[row(D_MODEL), row(D_MODEL),
                   pl.BlockSpec((TM, LANES), lambda i: (lag(i), 0)),
                   pl.BlockSpec((1, SUBLANES, TM), lambda i: (lag(i), 0, 0)), full((8, LANES))],
        out_shape=[jax.ShapeDtypeStruct((T, D_MODEL), F32),
                   jax.ShapeDtypeStruct((T, D_MODEL), BF16),
                   jax.ShapeDtypeStruct((T, LANES), F32),
                   jax.ShapeDtypeStruct((T // TM, SUBLANES, TM), F32),
                   jax.ShapeDtypeStruct((8, LANES), F32)],
        scratch_shapes=[pltpu.VMEM((TM, TM), BF16), pltpu.VMEM((D_MODEL, D_MODEL), BF16),
                        pltpu.VMEM((TM, ROUTER_COLS), F32)],
        compiler_params=pltpu.CompilerParams(
            dimension_semantics=("arbitrary",), vmem_limit_bytes=VMEM_LIMIT),
        name="out_router",
    )(mixed, x, w_out, gain, wr_cat, br)


def _dispatch_kernel(fill_ref, nu_ref, dest_ref, h_ref, xb_ref, zero_ref, sem_ref, zsem_ref,
                     stage_ref):
    i = pl.program_id(0)
    par = lax.rem(i, 2)

    @pl.when(i == 0)
    def _():
        zero_ref[...] = jnp.zeros_like(zero_ref)
        fills = [(fill_ref[e] >= 0, pltpu.make_async_copy(
            zero_ref, xb_ref.at[pl.ds(pl.multiple_of(jnp.maximum(fill_ref[e], 0), MOE_BLK), MOE_BLK)],
            zsem_ref)) for e in range(N_EXPERTS)]
        n_blocks = xb_ref.shape[0] // MOE_BLK
        fills += [(j >= nu_ref[0], pltpu.make_async_copy(
            zero_ref, xb_ref.at[pl.ds(j * MOE_BLK, MOE_BLK)], zsem_ref))
            for j in range(n_blocks - N_EXPERTS, n_blocks)]
        for cond, f in fills:
            pl.when(cond)(f.start)
        for cond, f in fills:
            pl.when(cond)(f.wait)

    _row_tiles_store(stage_ref.at[par], h_ref[...].astype(F32))
    for r in range(TM):
        for k in range(TOP_K):
            pltpu.make_async_copy(stage_ref.at[par, r], xb_ref.at[dest_ref[0, 0, k * TM + r]],
                                  sem_ref.at[par]).start(priority=k)

    def wait_tile(p):
        for _ in range(TOP_K):
            pltpu.make_async_copy(stage_ref.at[p], xb_ref.at[pl.ds(0, TM)], sem_ref.at[p]).wait()

    pl.when(i > 0)(lambda: wait_tile(1 - par))
    pl.when(i == pl.num_programs(0) - 1)(lambda: wait_tile(par))


def _dispatch(fill_start, n_used, dest_tiles, h2, n_rows):
    T = h2.shape[0]
    grid_spec = pltpu.PrefetchScalarGridSpec(
        num_scalar_prefetch=2,
        grid=(T // TM,),
        in_specs=[
            pl.BlockSpec((1, 1, TOP_K * TM), lambda i, fs, nu: (i, 0, 0), memory_space=pltpu.SMEM),
            pl.BlockSpec((TM, D_MODEL), lambda i, fs, nu: (i, 0)),
        ],
        out_specs=pl.BlockSpec(memory_space=pl.ANY),
        scratch_shapes=[pltpu.VMEM((MOE_BLK, ROW_TILES, LANES), F32),
                        pltpu.SemaphoreType.DMA((2,)), pltpu.SemaphoreType.DMA(()),
                        pltpu.VMEM((2, TM, ROW_TILES, LANES), F32)],
    )
    return pl.pallas_call(
        _dispatch_kernel,
        grid_spec=grid_spec,
        out_shape=jax.ShapeDtypeStruct((n_rows, ROW_TILES, LANES), F32),
        compiler_params=pltpu.CompilerParams(dimension_semantics=("arbitrary",)),
        name="dispatch",
    )(fill_start, n_used, dest_tiles, h2)


BLOCK_COPY_PARTS = 4
X_SLOTS = 4
Y_SLOTS = 3


class _CopyGroup:
    def __init__(self, copies):
        self.copies = copies

    def start(self):
        for n, c in enumerate(self.copies):
            c.start(priority=n % 2)

    def wait(self):
        for c in self.copies:
            c.wait()


def _expert_kernel(be_ref, nxt_ref, nu_ref, xb_ref, wg_ref, wu_ref, wd_ref, yb_ref,
                   xbuf_ref, ybuf_ref, wgs_ref, wus_ref, wds_ref, wgb_ref, wub_ref, wdb_ref,
                   xsem_ref, ysem_ref, wsem_ref, *, layer, n_blocks):
    n_used = nu_ref[0]

    part = MOE_BLK // BLOCK_COPY_PARTS

    def x_copy(j, slot):
        return _CopyGroup([pltpu.make_async_copy(
            xb_ref.at[pl.ds(j * MOE_BLK + p * part, part)],
            xbuf_ref.at[slot, pl.ds(p * part, part)], xsem_ref.at[slot])
            for p in range(BLOCK_COPY_PARTS)])

    def y_copy(j, slot):
        return _CopyGroup([pltpu.make_async_copy(
            ybuf_ref.at[slot, pl.ds(p * part, part)],
            yb_ref.at[pl.ds(j * MOE_BLK + p * part, part)], ysem_ref.at[slot])
            for p in range(BLOCK_COPY_PARTS)])

    def w_copies(e, ws):
        return [pltpu.make_async_copy(src.at[layer, e], dst.at[ws], wsem_ref.at[ws])
                for src, dst in ((wg_ref, wgs_ref), (wu_ref, wus_ref), (wd_ref, wds_ref))]

    for j0 in range(X_SLOTS - 1):
        x_copy(j0, j0).start()
    for c in w_copies(be_ref[0], 0):
        c.start()

    def block(j, ws):
        slot = lax.rem(j, Y_SLOTS)
        xslot = lax.rem(j, X_SLOTS)
        first = (j == 0) | (be_ref[j] != be_ref[jnp.maximum(j - 1, 0)])
        ws = jnp.where(first & (j > 0), 1 - ws, ws)

        @pl.when(first)
        def _():
            for c in w_copies(be_ref[j], ws):
                c.wait()
            for r0 in range(0, D_MODEL, W_PREP_ROWS):
                rows = slice(r0, r0 + W_PREP_ROWS)
                wgb_ref[rows, :] = wgs_ref[ws, rows, :].astype(BF16)
                wub_ref[rows, :] = wus_ref[ws, rows, :].astype(BF16)
            for r0 in range(0, D_EXPERT, W_PREP_ROWS):
                rows = slice(r0, r0 + W_PREP_ROWS)
                wdb_ref[rows, :] = wds_ref[ws, rows, :].astype(BF16)

            @pl.when(nxt_ref[j] >= 0)
            def _():
                for c in w_copies(nxt_ref[j], 1 - ws):
                    c.start()

        ahead = j + X_SLOTS - 1

        @pl.when(ahead < n_used)
        def _():
            x_copy(ahead, lax.rem(ahead, X_SLOTS)).start()

        x_copy(j, xslot).wait()
        x = jnp.concatenate(
            [c.astype(BF16) for c in _row_tiles_chunks(xbuf_ref.at[xslot], MOE_BLK)], axis=1)
        g = _dot(x, wgb_ref[...])
        u = _dot(x, wub_ref[...])
        h = (g * (1.0 / (1.0 + jnp.exp(-g)))) * u
        y = _dot(h.astype(BF16), wdb_ref[...])

        @pl.when(j >= Y_SLOTS)
        def _():
            y_copy(j - Y_SLOTS, slot).wait()

        _row_tiles_store(ybuf_ref.at[slot], y)
        y_copy(j, slot).start()
        return ws

    lax.fori_loop(0, n_used, block, jnp.int32(0))

    for back in range(Y_SLOTS, 0, -1):
        y_copy(n_used - back, lax.rem(n_used - back, Y_SLOTS)).wait()

    ybuf_ref[0] = jnp.zeros((MOE_BLK, ROW_TILES, LANES), F32)

    def fill(j, carry):
        y_copy(j, 0).start()
        return carry

    def fill_wait(j, carry):
        y_copy(j, 0).wait()
        return carry

    lax.fori_loop(n_used, n_blocks, fill, 0)
    lax.fori_loop(n_used, n_blocks, fill_wait, 0)


def _experts(blk_exp, nxt_exp, n_used, xb, w_gate, w_up, w_down, layer):
    n_blocks = blk_exp.shape[0]
    any_spec = pl.BlockSpec(memory_space=pl.ANY)
    blk = (MOE_BLK, ROW_TILES, LANES)
    grid_spec = pltpu.PrefetchScalarGridSpec(
        num_scalar_prefetch=3,
        grid=(1,),
        in_specs=[any_spec, any_spec, any_spec, any_spec],
        out_specs=any_spec,
        scratch_shapes=[
            pltpu.VMEM((X_SLOTS,) + blk, F32), pltpu.VMEM((Y_SLOTS,) + blk, F32),
            pltpu.VMEM((2, D_MODEL, D_EXPERT), F32), pltpu.VMEM((2, D_MODEL, D_EXPERT), F32),
            pltpu.VMEM((2, D_EXPERT, D_MODEL), F32),
            pltpu.VMEM((D_MODEL, D_EXPERT), BF16), pltpu.VMEM((D_MODEL, D_EXPERT), BF16),
            pltpu.VMEM((D_EXPERT, D_MODEL), BF16),
            pltpu.SemaphoreType.DMA((X_SLOTS,)), pltpu.SemaphoreType.DMA((Y_SLOTS,)),
            pltpu.SemaphoreType.DMA((2,)),
        ],
    )
    return pl.pallas_call(
        functools.partial(_expert_kernel, layer=layer, n_blocks=n_blocks),
        grid_spec=grid_spec,
        out_shape=jax.ShapeDtypeStruct((n_blocks * MOE_BLK, ROW_TILES, LANES), F32),
        compiler_params=pltpu.CompilerParams(
            dimension_semantics=("arbitrary",), vmem_limit_bytes=VMEM_LIMIT),
        name="experts",
    )(blk_exp, nxt_exp, n_used, xb, w_gate, w_up, w_down)


def _dispatch_tables(route_t, counts_rec, T):
    counts = counts_rec[0, N_GROUPS:N_GROUPS + N_EXPERTS].astype(jnp.int32)
    n_steps = (T * TOP_K) // MOE_BLK + N_EXPERTS
    nblk = (counts + MOE_BLK - 1) // MOE_BLK
    bend = jnp.cumsum(nblk)
    pstart = (bend - nblk) * MOE_BLK
    n_used = bend[-1]
    j = jnp.minimum(jnp.arange(n_steps, dtype=jnp.int32), n_used - 1)
    blk_exp = jnp.minimum(jnp.sum(j[:, None] >= bend[None, :], axis=1), N_EXPERTS - 1)
    n_rows = n_steps * MOE_BLK
    last_blk = jnp.where(counts > 0, (bend - 1) * MOE_BLK, -1)
    ids = jnp.arange(N_EXPERTS, dtype=jnp.int32)
    later = (ids[None, :] > ids[:, None]) & (nblk[None, :] > 0)
    nxt_of = jnp.min(jnp.where(later, ids[None, :], N_EXPERTS), axis=1)
    nxt_tab = jnp.where(nxt_of < N_EXPERTS, nxt_of, -1)
    nxt_exp = jnp.sum(jnp.where(blk_exp[:, None] == ids[None, :], nxt_tab[None, :], 0), axis=1)
    e = route_t[:, R_E:R_E + TOP_K, :].astype(jnp.int32)
    rank = route_t[:, R_RANK:R_RANK + TOP_K, :].astype(jnp.int32)
    seg = jnp.sum(jnp.where(e[..., None] == jnp.arange(N_EXPERTS), pstart, 0), axis=-1)
    dest = jnp.clip(seg + rank, 0, n_steps * MOE_BLK - 1)
    dest_tiles = dest.reshape(T // TM, 1, TOP_K * TM)
    return dict(dest_tiles=dest_tiles, fill_start=last_blk.astype(jnp.int32),
                blk_exp=blk_exp.astype(jnp.int32), nxt_exp=nxt_exp.astype(jnp.int32),
                n_used=n_used.reshape(1).astype(jnp.int32), n_rows=n_rows)


def _prep_layer(l, w_gk_up, b_gk, gla_norm, gmlp_norm, w_spatial, b_spatial, w_conv,
                w_router_group, b_router_group, w_router_expert, b_router_expert):
    wgk = jnp.concatenate(
        [w_gk_up[l], jnp.zeros((LANES - GLA_GATE_RANK, GLA_KDIM), F32)], axis=0).astype(BF16)
    wsp = w_spatial[l].transpose(1, 0, 2).reshape(GMLP_CHUNK, GMLP_HEADS * GMLP_CHUNK)
    bsp = jnp.repeat(b_spatial[l].T, GMLP_DH, axis=1)
    wconv = jnp.concatenate([w_conv[l], jnp.zeros((8 - CONV_K, CONV_WIDTH), F32)], axis=0)
    wr = jnp.concatenate(
        [w_router_group[l], w_router_expert[l],
         jnp.zeros((D_MODEL, ROUTER_COLS - N_GROUPS - N_EXPERTS), F32)], axis=1)
    wr_hi = wr.astype(BF16)
    wr_lo = (wr - wr_hi.astype(F32)).astype(BF16)
    br = jnp.concatenate(
        [b_router_group[l], b_router_expert[l],
         jnp.zeros((ROUTER_COLS - N_GROUPS - N_EXPERTS,), F32)])[None, :]
    return dict(
        wgk=wgk, bgk=b_gk[l][None, :], glan=gla_norm[l][None, :], gmn=gmlp_norm[l][None, :],
        wsp=wsp, bsp=bsp, wconv=wconv, wr_cat=jnp.concatenate([wr_hi, wr_lo], axis=1), br=br)


def kernel(x, attn_norm, w_in, w_gk_up, b_gk, gla_norm, gmlp_norm, w_spatial, b_spatial, w_conv, w_out, ffn_norm, w_router_group, b_router_group, w_router_expert, b_router_expert, w_gate, w_up, w_down, final_norm):
    B, S, D = x.shape
    T = B * S
    depth = w_in.shape[0]
    xr = x.reshape(T, D)
    moe = None
    for l in range(depth):
        p = _prep_layer(l, w_gk_up, b_gk, gla_norm, gmlp_norm, w_spatial, b_spatial, w_conv,
                        w_router_group, b_router_group, w_router_expert, b_router_expert)
        if moe is None:
            proj = _norm_proj(xr, attn_norm[l][None, :], w_in, l)
        else:
            xr, proj = _combine_norm_proj(moe["dest_tiles"], moe["x2"], moe["route"], moe["yb"],
                                          attn_norm[l][None, :], w_in, l)
        mixed = _mixers(proj, p["wgk"], p["bgk"], p["glan"], p["gmn"], p["wsp"], p["bsp"],
                        p["wconv"], B, S)
        x2, h2, route, route_t, counts_rec = _out_router(
            mixed, xr, w_out, l, ffn_norm[l][None, :], p["wr_cat"], p["br"])
        moe = _dispatch_tables(route_t, counts_rec, T)
        xb = _dispatch(moe["fill_start"], moe["n_used"], moe["dest_tiles"], h2, moe["n_rows"])
        yb = _experts(moe["blk_exp"], moe["nxt_exp"], moe["n_used"], xb, w_gate, w_up, w_down, l)
        moe.update(x2=x2, route=route, yb=yb)
    out = _combine_final_norm(moe["dest_tiles"], moe["x2"], moe["route"], moe["yb"],
                              final_norm[None, :])
    return out.reshape(B, S, D)
```

```python
import functools

import jax
import jax.numpy as jnp
from jax import lax
from jax.experimental import pallas as pl
from jax.experimental.pallas import tpu as pltpu

F32 = jnp.float32
BF16 = jnp.bfloat16

D_MODEL = 1024
RMS_EPS = 1e-6
GLA_HEADS = 4
GLA_WIDTH = 512
GLA_DV = 128
GLA_DK = 64
GLA_KDIM = 256
GLA_GATE_RANK = 16
GLA_GATE_NORM = 16.0
GLA_CHUNK = 64
GMLP_HEADS = 4
GMLP_WIDTH = 256
GMLP_DH = 64
GMLP_CHUNK = 128
CONV_WIDTH = 256
CONV_K = 3
N_GROUPS = 4
EXPERTS_PER_GROUP = 8
N_EXPERTS = 32
TOP_K = 2
D_EXPERT = 256

LANES = 128
C_Q, C_K, C_V, C_G = 0, 256, 512, 1024
C_U, C_VG, C_X, C_BG, C_CG, C_GKL = 1536, 1792, 2048, 2304, 2560, 2816
D_PROJ = C_GKL + LANES
D_IN = C_GKL + GLA_GATE_RANK

TM = 256
TS_MIX = 256
MOE_BLK = 256
ROUTER_COLS = LANES
SUBLANES = 8
ROW_TILES = D_MODEL // LANES
assert ROW_TILES == SUBLANES
VMEM_LIMIT = 56 * 1024 * 1024
R_E, R_RANK, R_W = 0, 2, 4


def _dot(a, b):
    return jnp.dot(a, b, preferred_element_type=F32)


def _split_bf16(x):
    hi = x.astype(BF16)
    lo = (x - hi.astype(F32)).astype(BF16)
    return hi, lo


def _rms(x, gain):
    return x * lax.rsqrt(jnp.mean(x * x, axis=-1, keepdims=True) + RMS_EPS) * gain


W_PREP_ROWS = 128
PROJ_CHUNK = 256


def _stage_w_in(wt_ref, wb_ref):
    for c0 in range(0, C_GKL, LANES):
        src = c0 if c0 < C_U else c0 + GLA_GATE_RANK
        wb_ref[:, c0:c0 + LANES] = wt_ref[0, src:src + LANES, :].T.astype(BF16)
    low = jnp.concatenate([wt_ref[0, C_U:C_U + GLA_GATE_RANK, :],
                           jnp.zeros((LANES - GLA_GATE_RANK, D_MODEL), F32)], axis=0)
    wb_ref[:, C_GKL:D_PROJ] = low.T.astype(BF16)


def _w_in_specs(layer):
    return pl.BlockSpec((1, D_IN, D_MODEL), lambda i: (layer, 0, 0), pipeline_mode=pl.Buffered(1))


def _norm_proj_kernel(x_ref, gain_ref, w_ref, proj_ref, wb_ref):
    @pl.when(pl.program_id(0) == 0)
    def _():
        _stage_w_in(w_ref, wb_ref)
    proj_ref[...] = _dot(_rms(x_ref[...], gain_ref[...]).astype(BF16), wb_ref[...]).astype(BF16)


def _norm_proj(x, gain, w_in, layer):
    T = x.shape[0]
    return pl.pallas_call(
        _norm_proj_kernel,
        grid=(T // TM,),
        in_specs=[
            pl.BlockSpec((TM, D_MODEL), lambda i: (i, 0)),
            pl.BlockSpec((1, D_MODEL), lambda i: (0, 0)),
            _w_in_specs(layer),
        ],
        out_specs=pl.BlockSpec((TM, D_PROJ), lambda i: (i, 0)),
        out_shape=jax.ShapeDtypeStruct((T, D_PROJ), BF16),
        scratch_shapes=[pltpu.VMEM((D_MODEL, D_PROJ), BF16)],
        compiler_params=pltpu.CompilerParams(
            dimension_semantics=("arbitrary",), vmem_limit_bytes=VMEM_LIMIT),
        name="norm_proj",
    )(x, gain, w_in)


def _row_gather_copy(yb_ref, buf_ref, sem_ref, slot, k, r, d):
    return pltpu.make_async_copy(yb_ref.at[d], buf_ref.at[slot, k, r], sem_ref.at[slot])


def _gather_start(dest_ref, yb_ref, buf_ref, sem_ref, slot, rows=range(TM)):
    for r in rows:
        for k in range(TOP_K):
            _row_gather_copy(yb_ref, buf_ref, sem_ref, slot, k, r,
                             dest_ref[0, 0, k * TM + r]).start(priority=k)


def _gather_wait(yb_ref, buf_ref, sem_ref, slot):
    for k in range(TOP_K):
        pltpu.make_async_copy(yb_ref.at[pl.ds(0, TM)], buf_ref.at[slot, k], sem_ref.at[slot]).wait()


def _combined_residual(dcur_ref, x_ref, route_ref, yb_ref, buf_ref, sem_ref):
    i = pl.program_id(0)
    slot = lax.rem(i, 2)

    @pl.when(i == 0)
    def _():
        _gather_start(dcur_ref, yb_ref, buf_ref, sem_ref, 0)

    _gather_wait(yb_ref, buf_ref, sem_ref, slot)
    w0 = route_ref[:, R_W:R_W + 1]
    w1 = route_ref[:, R_W + 1:R_W + 2]
    y0 = _row_tiles_chunks(buf_ref.at[slot, 0], TM)
    y1 = _row_tiles_chunks(buf_ref.at[slot, 1], TM)
    return jnp.concatenate(
        [x_ref[:, c * LANES:(c + 1) * LANES] + (w0 * y0[c] + w1 * y1[c]) for c in range(ROW_TILES)],
        axis=1)


def _prefetch_groups(n_groups):
    per = -(-TM // n_groups)
    return [range(g * per, min(TM, (g + 1) * per)) for g in range(n_groups)]


def _drain_last_prefetch(yb_ref, buf_ref, sem_ref):
    i = pl.program_id(0)

    @pl.when(i == pl.num_programs(0) - 1)
    def _():
        _gather_wait(yb_ref, buf_ref, sem_ref, 1 - lax.rem(i, 2))


def _combine_specs(n_tiles):
    smem_tile = lambda f: pl.BlockSpec((1, 1, TOP_K * TM), f, memory_space=pltpu.SMEM)
    return [
        smem_tile(lambda i: (i, 0, 0)),
        smem_tile(lambda i: (jnp.minimum(i + 1, n_tiles - 1), 0, 0)),
        pl.BlockSpec((TM, D_MODEL), lambda i: (i, 0)),
        pl.BlockSpec((TM, LANES), lambda i: (i, 0)),
        pl.BlockSpec(memory_space=pl.ANY),
    ]


_COMBINE_SCRATCH = [pltpu.VMEM((2, TOP_K, TM, ROW_TILES, LANES), F32),
                    pltpu.SemaphoreType.DMA((2,))]


def _combine_norm_proj_kernel(dcur_ref, dnxt_ref, x_ref, route_ref, yb_ref, gain_ref, w_ref,
                              xo_ref, proj_ref, buf_ref, sem_ref, wb_ref):
    @pl.when(pl.program_id(0) == 0)
    def _():
        _stage_w_in(w_ref, wb_ref)
    x = _combined_residual(dcur_ref, x_ref, route_ref, yb_ref, buf_ref, sem_ref)
    xo_ref[...] = x
    h = _rms(x, gain_ref[...]).astype(BF16)
    nxt = 1 - lax.rem(pl.program_id(0), 2)
    col_chunks = [(c0, min(c0 + PROJ_CHUNK, D_PROJ)) for c0 in range(0, D_PROJ, PROJ_CHUNK)]
    for rows, (c0, c1) in zip(_prefetch_groups(len(col_chunks)), col_chunks):
        _gather_start(dnxt_ref, yb_ref, buf_ref, sem_ref, nxt, rows)
        proj_ref[:, c0:c1] = _dot(h, wb_ref[:, c0:c1]).astype(BF16)
    _drain_last_prefetch(yb_ref, buf_ref, sem_ref)


def _combine_norm_proj(dest_tiles, x2, route, yb, gain, w_in, layer):
    T = x2.shape[0]
    n_tiles = T // TM
    return pl.pallas_call(
        _combine_norm_proj_kernel,
        grid=(n_tiles,),
        in_specs=_combine_specs(n_tiles) + [
            pl.BlockSpec((1, D_MODEL), lambda i: (0, 0)),
            _w_in_specs(layer),
        ],
        out_specs=[pl.BlockSpec((TM, D_MODEL), lambda i: (i, 0)),
                   pl.BlockSpec((TM, D_PROJ), lambda i: (i, 0))],
        out_shape=[jax.ShapeDtypeStruct((T, D_MODEL), F32),
                   jax.ShapeDtypeStruct((T, D_PROJ), BF16)],
        scratch_shapes=_COMBINE_SCRATCH + [pltpu.VMEM((D_MODEL, D_PROJ), BF16)],
        compiler_params=pltpu.CompilerParams(
            dimension_semantics=("arbitrary",), vmem_limit_bytes=VMEM_LIMIT),
        name="combine_norm_proj",
    )(dest_tiles, dest_tiles, x2, route, yb, gain, w_in)


def _combine_final_norm_kernel(dcur_ref, dnxt_ref, x_ref, route_ref, yb_ref, gain_ref,
                               o_ref, buf_ref, sem_ref):
    _gather_start(dnxt_ref, yb_ref, buf_ref, sem_ref, 1 - lax.rem(pl.program_id(0), 2))
    x = _combined_residual(dcur_ref, x_ref, route_ref, yb_ref, buf_ref, sem_ref)
    o_ref[...] = _rms(x, gain_ref[...])
    _drain_last_prefetch(yb_ref, buf_ref, sem_ref)


def _combine_final_norm(dest_tiles, x2, route, yb, gain):
    T = x2.shape[0]
    n_tiles = T // TM
    return pl.pallas_call(
        _combine_final_norm_kernel,
        grid=(n_tiles,),
        in_specs=_combine_specs(n_tiles) + [pl.BlockSpec((1, D_MODEL), lambda i: (0, 0))],
        out_specs=pl.BlockSpec((TM, D_MODEL), lambda i: (i, 0)),
        out_shape=jax.ShapeDtypeStruct((T, D_MODEL), F32),
        scratch_shapes=_COMBINE_SCRATCH,
        compiler_params=pltpu.CompilerParams(
            dimension_semantics=("arbitrary",), vmem_limit_bytes=VMEM_LIMIT),
        name="combine_final_norm",
    )(dest_tiles, dest_tiles, x2, route, yb, gain)


def _gelu_tanh(x):
    c = 0.7978845608028654
    return x * (0.5 * (1.0 + jnp.tanh(c * (x + 0.044715 * (x * x * x)))))


def _mixer_kernel(proj_ref, wgk_ref, bgk_ref, glan_ref, gmn_ref, wsp_ref, bsp_ref, wconv_ref,
                  out_ref, st_ref, hc_ref, lcat_ref, wm_ref):
    TS = TS_MIX
    n_gla = TS // GLA_CHUNK
    n_gm = TS // GMLP_CHUNK

    @pl.when(pl.program_id(1) == 0)
    def _():
        st_ref[...] = jnp.zeros_like(st_ref)
        hc_ref[...] = jnp.zeros_like(hc_ref)
        r = lax.broadcasted_iota(jnp.int32, (TS, TS), 0)
        c = lax.broadcasted_iota(jnp.int32, (TS, TS), 1)
        keep = ((r // GLA_CHUNK) == (c // GLA_CHUNK)) & (c <= r)
        lcat_ref[...] = jnp.where(keep, 1.0, 0.0).astype(BF16)
        t = lax.broadcasted_iota(jnp.int32, (GMLP_CHUNK, GMLP_HEADS * GMLP_CHUNK), 0)
        s = lax.broadcasted_iota(jnp.int32, (GMLP_CHUNK, GMLP_HEADS * GMLP_CHUNK), 1) % GMLP_CHUNK
        wm_ref[...] = jnp.where(s <= t, wsp_ref[...], 0.0).astype(BF16)

    lane256 = lax.broadcasted_iota(jnp.int32, (1, GLA_KDIM), 1)

    q = proj_ref[:, C_Q:C_Q + GLA_KDIM].astype(F32)
    k = proj_ref[:, C_K:C_K + GLA_KDIM].astype(F32)
    v_b = proj_ref[:, C_V:C_V + GLA_WIDTH]
    z = _dot(proj_ref[:, C_GKL:C_GKL + LANES], wgk_ref[...]) + bgk_ref[...]
    gk = (jnp.minimum(z, 0.0) - jnp.log1p(jnp.exp(-jnp.abs(z)))) * (1.0 / GLA_GATE_NORM)
    gk_hi, gk_lo = _split_bf16(gk)
    cs = _dot(lcat_ref[...], jnp.concatenate([gk_hi, gk_lo], axis=1))
    b = cs[:, :GLA_KDIM] + cs[:, GLA_KDIM:]
    b_last = [b[(c + 1) * GLA_CHUNK - 1:(c + 1) * GLA_CHUNK, :] for c in range(n_gla)]
    bl = jnp.concatenate(
        [jnp.broadcast_to(t, (GLA_CHUNK, GLA_KDIM)) for t in b_last], axis=0)
    q_dec = (q * (GLA_DK ** -0.5)) * jnp.exp(b)
    k_inv = (k * jnp.exp(-b)).astype(BF16)
    k_dec = (k * jnp.exp(bl - b)).astype(BF16)
    q_dec_b = q_dec.astype(BF16)

    zero_b = jnp.zeros_like(q_dec_b)
    q_stack = jnp.concatenate(
        [jnp.where((lane256 // GLA_DK) == h, q_dec_b, zero_b) for h in range(GLA_HEADS)], axis=0)
    scores = lax.dot_general(q_stack, k_inv, (((1,), (1,)), ((), ())),
                             preferred_element_type=F32)
    rt = lax.broadcasted_iota(jnp.int32, (TS, TS), 0)
    ct = lax.broadcasted_iota(jnp.int32, (TS, TS), 1)
    causal = ((rt // GLA_CHUNK) == (ct // GLA_CHUNK)) & (ct <= rt)
    o_heads = []
    for h in range(GLA_HEADS):
        p_h = jnp.where(causal, scores[h * TS:(h + 1) * TS, :], 0.0).astype(BF16)
        o_heads.append(_dot(p_h, v_b[:, h * GLA_DV:(h + 1) * GLA_DV]))

    o_inter = [[] for _ in range(GLA_HEADS)]
    for c in range(n_gla):
        rows = slice(c * GLA_CHUNK, (c + 1) * GLA_CHUNK)
        st = st_ref[...]
        st_b = st.astype(BF16)
        for h in range(GLA_HEADS):
            o_inter[h].append(_dot(q_stack[h * TS + c * GLA_CHUNK:h * TS + (c + 1) * GLA_CHUNK, :], st_b))
        upd = lax.dot_general(k_dec[rows], v_b[rows], (((0,), (0,)), ((), ())),
                              preferred_element_type=F32)
        decay = jnp.exp(jnp.broadcast_to(b_last[c], (SUBLANES, GLA_KDIM))).T[:, 0:1]
        st_ref[...] = st * decay + jnp.concatenate(
            [upd[h * GLA_DK:(h + 1) * GLA_DK, h * GLA_DV:(h + 1) * GLA_DV] for h in range(GLA_HEADS)],
            axis=0)

    for h in range(GLA_HEADS):
        cols = slice(h * GLA_DV, (h + 1) * GLA_DV)
        o = o_heads[h] + jnp.concatenate(o_inter[h], axis=0)
        o = o * lax.rsqrt(jnp.mean(o * o, axis=-1, keepdims=True) + RMS_EPS) * glan_ref[...]
        g = proj_ref[:, C_G + h * GLA_DV:C_G + (h + 1) * GLA_DV].astype(F32)
        out_ref[:, cols] = (o * (g * (1.0 / (1.0 + jnp.exp(-g))))).astype(out_ref.dtype)

    u = _gelu_tanh(proj_ref[:, C_U:C_U + GMLP_WIDTH].astype(F32))
    vg = _gelu_tanh(proj_ref[:, C_VG:C_VG + GMLP_WIDTH].astype(F32))
    hr = lax.broadcasted_iota(jnp.int32, (GMLP_WIDTH, GMLP_WIDTH), 0) // GMLP_DH
    hcn = lax.broadcasted_iota(jnp.int32, (GMLP_WIDTH, GMLP_WIDTH), 1) // GMLP_DH
    head_mean = jnp.where(hr == hcn, 1.0 / GMLP_DH, 0.0).astype(BF16)
    sq_hi, sq_lo = _split_bf16(vg * vg)
    ms = _dot(sq_hi, head_mean) + _dot(sq_lo, head_mean)
    v32 = vg * lax.rsqrt(ms + RMS_EPS) * gmn_ref[...]
    for c in range(n_gm):
        rows = slice(c * GMLP_CHUNK, (c + 1) * GMLP_CHUNK)
        vc = v32[rows].astype(BF16)
        zc = jnp.zeros_like(vc)
        rhs = jnp.concatenate(
            [jnp.where((lane256 // GMLP_DH) == h, vc, zc) for h in range(GMLP_HEADS)], axis=0)
        mixed = _dot(wm_ref[...], rhs) + bsp_ref[...]
        out_ref[rows, GLA_WIDTH:GLA_WIDTH + GMLP_WIDTH] = (u[rows] * mixed).astype(out_ref.dtype)

    hcv = (proj_ref[:, C_CG:C_CG + CONV_WIDTH].astype(F32)
           * proj_ref[:, C_X:C_X + CONV_WIDTH].astype(F32))
    hc_ref[8:8 + TS, :] = hcv
    y = (wconv_ref[2:3, :] * hcv + wconv_ref[1:2, :] * hc_ref[7:7 + TS, :]
         + wconv_ref[0:1, :] * hc_ref[6:6 + TS, :])
    out_ref[:, GLA_WIDTH + GMLP_WIDTH:] = (
        proj_ref[:, C_BG:C_BG + CONV_WIDTH].astype(F32) * y).astype(out_ref.dtype)
    hc_ref[0:8, :] = hc_ref[TS:TS + 8, :]


def _mixers(proj, wgk, bgk, glan, gmn, wsp, bsp, wconv, batch, seq):
    n_seq = seq // TS_MIX
    full = lambda shape: pl.BlockSpec(shape, lambda b, i: (0,) * len(shape))
    return pl.pallas_call(
        _mixer_kernel,
        grid=(batch, n_seq),
        in_specs=[
            pl.BlockSpec((TS_MIX, D_PROJ), lambda b, i: (b * n_seq + i, 0)),
            full((LANES, GLA_KDIM)), full((1, GLA_KDIM)), full((1, GLA_DV)), full((1, GMLP_WIDTH)),
            full((GMLP_CHUNK, GMLP_HEADS * GMLP_CHUNK)), full((GMLP_CHUNK, GMLP_WIDTH)),
            full((8, CONV_WIDTH)),
        ],
        out_specs=pl.BlockSpec((TS_MIX, D_MODEL), lambda b, i: (b * n_seq + i, 0)),
        out_shape=jax.ShapeDtypeStruct((batch * seq, D_MODEL), BF16),
        scratch_shapes=[
            pltpu.VMEM((GLA_KDIM, GLA_DV), F32),
            pltpu.VMEM((TS_MIX + 8, CONV_WIDTH), F32),
            pltpu.VMEM((TS_MIX, TS_MIX), BF16),
            pltpu.VMEM((GMLP_CHUNK, GMLP_HEADS * GMLP_CHUNK), BF16),
        ],
        compiler_params=pltpu.CompilerParams(
            dimension_semantics=("arbitrary", "arbitrary"), vmem_limit_bytes=VMEM_LIMIT),
        name="mixers",
    )(proj, wgk, bgk, glan, gmn, wsp, bsp, wconv)


def _row_tiles_store(tiles_ref, x):
    rows = x.shape[0]
    flat = tiles_ref.reshape(rows * ROW_TILES, LANES)
    for c in range(ROW_TILES):
        flat[pl.ds(c, rows, stride=ROW_TILES), :] = x[:, c * LANES:(c + 1) * LANES]


def _row_tiles_chunks(tiles_ref, rows):
    flat = tiles_ref.reshape(rows * ROW_TILES, LANES)
    return [flat[pl.ds(c, rows, stride=ROW_TILES), :] for c in range(ROW_TILES)]


def _out_router_kernel(mix_ref, x_ref, wo_ref, gain_ref, wrc_ref, br_ref,
                       x2_ref, h2_ref, route_ref, route_t_ref, cnt_ref, tri_ref, wob_ref, lg_ref):
    i = pl.program_id(0)

    @pl.when(i == 0)
    def _():
        cnt_ref[...] = jnp.zeros_like(cnt_ref)
        lg_ref[...] = jnp.zeros_like(lg_ref)
        r = lax.broadcasted_iota(jnp.int32, (TM, TM), 0)
        c = lax.broadcasted_iota(jnp.int32, (TM, TM), 1)
        tri_ref[...] = jnp.where(c < r, 1.0, 0.0).astype(BF16)
        for r0 in range(0, D_MODEL, W_PREP_ROWS):
            wob_ref[r0:r0 + W_PREP_ROWS, :] = wo_ref[0, r0:r0 + W_PREP_ROWS, :].astype(BF16)

    lg = lg_ref[...]
    x2 = x_ref[...] + _dot(mix_ref[...], wob_ref[...])
    x2_ref[...] = x2
    h = _rms(x2, gain_ref[...])
    h_hi, h_lo = _split_bf16(h)
    h2_ref[...] = h_hi
    hh_hl = _dot(h_hi, wrc_ref[...])
    lg_ref[...] = (hh_hl[:, :ROUTER_COLS] + hh_hl[:, ROUTER_COLS:]
                   + _dot(h_lo, wrc_ref[:, :ROUTER_COLS]) + br_ref[...])

    lane = lax.broadcasted_iota(jnp.int32, (TM, LANES), 1).astype(F32)
    neg = -jnp.inf
    is_g = lane < N_GROUPS
    gl = jnp.where(is_g, lg, neg)
    gmax = jnp.max(gl, axis=1, keepdims=True)
    g_top = jnp.min(jnp.where(gl == gmax, lane, float(LANES)), axis=1, keepdims=True)
    g_w = 1.0 / jnp.sum(jnp.where(is_g, jnp.exp(lg - gmax), 0.0), axis=1, keepdims=True)
    first = N_GROUPS + EXPERTS_PER_GROUP * g_top
    el = jnp.where((lane >= first) & (lane < first + EXPERTS_PER_GROUP), lg, neg)
    m1 = jnp.max(el, axis=1, keepdims=True)
    i1 = jnp.min(jnp.where(el == m1, lane, float(LANES)), axis=1, keepdims=True)
    el2 = jnp.where(lane == i1, neg, el)
    m2 = jnp.max(el2, axis=1, keepdims=True)
    i2 = jnp.min(jnp.where(el2 == m2, lane, float(LANES)), axis=1, keepdims=True)
    ratio = jnp.exp(m2 - m1)
    w1 = g_w / (1.0 + ratio)
    w2 = w1 * ratio

    oh1 = jnp.where(lane == i1, 1.0, 0.0)
    oh2 = jnp.where(lane == i2, 1.0, 0.0)
    oh = jnp.where(i > 0, oh1 + oh2, 0.0)
    before = _dot(tri_ref[...], oh.astype(BF16)) + cnt_ref[0:1, :]
    rank1 = jnp.sum(oh1 * before, axis=1, keepdims=True)
    rank2 = jnp.sum(oh2 * before, axis=1, keepdims=True)
    cnt_ref[...] = cnt_ref[...] + jnp.sum(oh, axis=0, keepdims=True)

    rec = jnp.zeros((TM, LANES), F32)
    for col, val in ((R_E, i1 - N_GROUPS), (R_E + 1, i2 - N_GROUPS), (R_RANK, rank1),
                     (R_RANK + 1, rank2), (R_W, w1), (R_W + 1, w2)):
        rec = jnp.where(lane == col, val, rec)
    route_ref[...] = rec
    route_t_ref[0] = rec.T[0:SUBLANES, :]


def _out_router(mixed, x, w_out, layer, gain, wr_cat, br):
    T = x.shape[0]
    n = T // TM
    row = lambda w: pl.BlockSpec((TM, w), lambda i: (jnp.minimum(i, n - 1), 0))
    lag = lambda i: jnp.maximum(i - 1, 0)
    full = lambda shape: pl.BlockSpec(shape, lambda i: (0, 0))
    wo_spec = pl.BlockSpec((1, D_MODEL, D_MODEL), lambda i: (layer, 0, 0),
                           pipeline_mode=pl.Buffered(1))
    return pl.pallas_call(
        _out_router_kernel,
        grid=(n + 1,),
        in_specs=[row(D_MODEL), row(D_MODEL), wo_spec, full((1, D_MODEL)),
                  full((D_MODEL, 2 * ROUTER_COLS)), full((1, ROUTER_COLS))],
        out_specs=[row(D_MODEL), row(D_MODEL),
                   pl.BlockSpec((TM, LANES), lambda i: (lag(i), 0)),
                   pl.BlockSpec((1, SUBLANES, TM), lambda i: (lag(i), 0, 0)), full((8, LANES))],
        out_shape=[jax.ShapeDtypeStruct((T, D_MODEL), F32),
                   jax.ShapeDtypeStruct((T, D_MODEL), BF16),
                   jax.ShapeDtypeStruct((T, LANES), F32),
                   jax.ShapeDtypeStruct((T // TM, SUBLANES, TM), F32),
                   jax.ShapeDtypeStruct((8, LANES), F32)],
        scratch_shapes=[pltpu.VMEM((TM, TM), BF16), pltpu.VMEM((D_MODEL, D_MODEL), BF16),
                        pltpu.VMEM((TM, ROUTER_COLS), F32)],
        compiler_params=pltpu.CompilerParams(
            dimension_semantics=("arbitrary",), vmem_limit_bytes=VMEM_LIMIT),
        name="out_router",
    )(mixed, x, w_out, gain, wr_cat, br)


def _dispatch_kernel(fill_ref, nu_ref, dest_ref, h_ref, xb_ref, zero_ref, sem_ref, zsem_ref,
                     stage_ref):
    i = pl.program_id(0)
    par = lax.rem(i, 2)

    @pl.when(i == 0)
    def _():
        zero_ref[...] = jnp.zeros_like(zero_ref)
        fills = [(fill_ref[e] >= 0, pltpu.make_async_copy(
            zero_ref, xb_ref.at[pl.ds(pl.multiple_of(jnp.maximum(fill_ref[e], 0), MOE_BLK), MOE_BLK)],
            zsem_ref)) for e in range(N_EXPERTS)]
        n_blocks = xb_ref.shape[0] // MOE_BLK
        fills += [(j >= nu_ref[0], pltpu.make_async_copy(
            zero_ref, xb_ref.at[pl.ds(j * MOE_BLK, MOE_BLK)], zsem_ref))
            for j in range(n_blocks - N_EXPERTS, n_blocks)]
        for cond, f in fills:
            pl.when(cond)(f.start)
        for cond, f in fills:
            pl.when(cond)(f.wait)

    _row_tiles_store(stage_ref.at[par], h_ref[...].astype(F32))
    for r in range(TM):
        for k in range(TOP_K):
            pltpu.make_async_copy(stage_ref.at[par, r], xb_ref.at[dest_ref[0, 0, k * TM + r]],
                                  sem_ref.at[par]).start(priority=k)

    def wait_tile(p):
        for _ in range(TOP_K):
            pltpu.make_async_copy(stage_ref.at[p], xb_ref.at[pl.ds(0, TM)], sem_ref.at[p]).wait()

    pl.when(i > 0)(lambda: wait_tile(1 - par))
    pl.when(i == pl.num_programs(0) - 1)(lambda: wait_tile(par))


def _dispatch(fill_start, n_used, dest_tiles, h2, n_rows):
    T = h2.shape[0]
    grid_spec = pltpu.PrefetchScalarGridSpec(
        num_scalar_prefetch=2,
        grid=(T // TM,),
        in_specs=[
            pl.BlockSpec((1, 1, TOP_K * TM), lambda i, fs, nu: (i, 0, 0), memory_space=pltpu.SMEM),
            pl.BlockSpec((TM, D_MODEL), lambda i, fs, nu: (i, 0)),
        ],
        out_specs=pl.BlockSpec(memory_space=pl.ANY),
        scratch_shapes=[pltpu.VMEM((MOE_BLK, ROW_TILES, LANES), F32),
                        pltpu.SemaphoreType.DMA((2,)), pltpu.SemaphoreType.DMA(()),
                        pltpu.VMEM((2, TM, ROW_TILES, LANES), F32)],
    )
    return pl.pallas_call(
        _dispatch_kernel,
        grid_spec=grid_spec,
        out_shape=jax.ShapeDtypeStruct((n_rows, ROW_TILES, LANES), F32),
        compiler_params=pltpu.CompilerParams(dimension_semantics=("arbitrary",)),
        name="dispatch",
    )(fill_start, n_used, dest_tiles, h2)


BLOCK_COPY_PARTS = 4
X_SLOTS = 4
Y_SLOTS = 3


class _CopyGroup:
    def __init__(self, copies):
        self.copies = copies

    def start(self):
        for n, c in enumerate(self.copies):
            c.start(priority=n % 2)

    def wait(self):
        for c in self.copies:
            c.wait()


def _expert_kernel(be_ref, nxt_ref, nu_ref, xb_ref, wg_ref, wu_ref, wd_ref, yb_ref,
                   xbuf_ref, ybuf_ref, wgs_ref, wus_ref, wds_ref, wgb_ref, wub_ref, wdb_ref,
                   xsem_ref, ysem_ref, wsem_ref, *, layer, n_blocks):
    n_used = nu_ref[0]

    part = MOE_BLK // BLOCK_COPY_PARTS

    def x_copy(j, slot):
        return _CopyGroup([pltpu.make_async_copy(
            xb_ref.at[pl.ds(j * MOE_BLK + p * part, part)],
            xbuf_ref.at[slot, pl.ds(p * part, part)], xsem_ref.at[slot])
            for p in range(BLOCK_COPY_PARTS)])

    def y_copy(j, slot):
        return _CopyGroup([pltpu.make_async_copy(
            ybuf_ref.at[slot, pl.ds(p * part, part)],
            yb_ref.at[pl.ds(j * MOE_BLK + p * part, part)], ysem_ref.at[slot])
            for p in range(BLOCK_COPY_PARTS)])

    def w_copies(e, ws):
        return [pltpu.make_async_copy(src.at[layer, e], dst.at[ws], wsem_ref.at[ws])
                for src, dst in ((wg_ref, wgs_ref), (wu_ref, wus_ref), (wd_ref, wds_ref))]

    for j0 in range(X_SLOTS - 1):
        x_copy(j0, j0).start()
    for c in w_copies(be_ref[0], 0):
        c.start()

    def block(j, ws):
        slot = lax.rem(j, Y_SLOTS)
        xslot = lax.rem(j, X_SLOTS)
        first = (j == 0) | (be_ref[j] != be_ref[jnp.maximum(j - 1, 0)])
        ws = jnp.where(first & (j > 0), 1 - ws, ws)

        @pl.when(first)
        def _():
            for c in w_copies(be_ref[j], ws):
                c.wait()
            for r0 in range(0, D_MODEL, W_PREP_ROWS):
                rows = slice(r0, r0 + W_PREP_ROWS)
                wgb_ref[rows, :] = wgs_ref[ws, rows, :].astype(BF16)
                wub_ref[rows, :] = wus_ref[ws, rows, :].astype(BF16)
            for r0 in range(0, D_EXPERT, W_PREP_ROWS):
                rows = slice(r0, r0 + W_PREP_ROWS)
                wdb_ref[rows, :] = wds_ref[ws, rows, :].astype(BF16)

            @pl.when(nxt_ref[j] >= 0)
            def _():
                for c in w_copies(nxt_ref[j], 1 - ws):
                    c.start()

        ahead = j + X_SLOTS - 1

        @pl.when(ahead < n_used)
        def _():
            x_copy(ahead, lax.rem(ahead, X_SLOTS)).start()

        x_copy(j, xslot).wait()
        x = jnp.concatenate(
            [c.astype(BF16) for c in _row_tiles_chunks(xbuf_ref.at[xslot], MOE_BLK)], axis=1)
        g = _dot(x, wgb_ref[...])
        u = _dot(x, wub_ref[...])
        h = (g * (1.0 / (1.0 + jnp.exp(-g)))) * u
        y = _dot(h.astype(BF16), wdb_ref[...])

        @pl.when(j >= Y_SLOTS)
        def _():
            y_copy(j - Y_SLOTS, slot).wait()

        _row_tiles_store(ybuf_ref.at[slot], y)
        y_copy(j, slot).start()
        return ws

    lax.fori_loop(0, n_used, block, jnp.int32(0))

    for back in range(Y_SLOTS, 0, -1):
        y_copy(n_used - back, lax.rem(n_used - back, Y_SLOTS)).wait()

    ybuf_ref[0] = jnp.zeros((MOE_BLK, ROW_TILES, LANES), F32)

    def fill(j, carry):
        y_copy(j, 0).start()
        return carry

    def fill_wait(j, carry):
        y_copy(j, 0).wait()
        return carry

    lax.fori_loop(n_used, n_blocks, fill, 0)
    lax.fori_loop(n_used, n_blocks, fill_wait, 0)


def _experts(blk_exp, nxt_exp, n_used, xb, w_gate, w_up, w_down, layer):
    n_blocks = blk_exp.shape[0]
    any_spec = pl.BlockSpec(memory_space=pl.ANY)
    blk = (MOE_BLK, ROW_TILES, LANES)
    grid_spec = pltpu.PrefetchScalarGridSpec(
        num_scalar_prefetch=3,
        grid=(1,),
        in_specs=[any_spec, any_spec, any_spec, any_spec],
        out_specs=any_spec,
        scratch_shapes=[
            pltpu.VMEM((X_SLOTS,) + blk, F32), pltpu.VMEM((Y_SLOTS,) + blk, F32),
            pltpu.VMEM((2, D_MODEL, D_EXPERT), F32), pltpu.VMEM((2, D_MODEL, D_EXPERT), F32),
            pltpu.VMEM((2, D_EXPERT, D_MODEL), F32),
            pltpu.VMEM((D_MODEL, D_EXPERT), BF16), pltpu.VMEM((D_MODEL, D_EXPERT), BF16),
            pltpu.VMEM((D_EXPERT, D_MODEL), BF16),
            pltpu.SemaphoreType.DMA((X_SLOTS,)), pltpu.SemaphoreType.DMA((Y_SLOTS,)),
            pltpu.SemaphoreType.DMA((2,)),
        ],
    )
    return pl.pallas_call(
        functools.partial(_expert_kernel, layer=layer, n_blocks=n_blocks),
        grid_spec=grid_spec,
        out_shape=jax.ShapeDtypeStruct((n_blocks * MOE_BLK, ROW_TILES, LANES), F32),
        compiler_params=pltpu.CompilerParams(
            dimension_semantics=("arbitrary",), vmem_limit_bytes=VMEM_LIMIT),
        name="experts",
    )(blk_exp, nxt_exp, n_used, xb, w_gate, w_up, w_down)


def _dispatch_tables(route_t, counts_rec, T):
    counts = counts_rec[0, N_GROUPS:N_GROUPS + N_EXPERTS].astype(jnp.int32)
    n_steps = (T * TOP_K) // MOE_BLK + N_EXPERTS
    nblk = (counts + MOE_BLK - 1) // MOE_BLK
    bend = jnp.cumsum(nblk)
    pstart = (bend - nblk) * MOE_BLK
    n_used = bend[-1]
    j = jnp.minimum(jnp.arange(n_steps, dtype=jnp.int32), n_used - 1)
    blk_exp = jnp.minimum(jnp.sum(j[:, None] >= bend[None, :], axis=1), N_EXPERTS - 1)
    n_rows = n_steps * MOE_BLK
    last_blk = jnp.where(counts > 0, (bend - 1) * MOE_BLK, -1)
    ids = jnp.arange(N_EXPERTS, dtype=jnp.int32)
    later = (ids[None, :] > ids[:, None]) & (nblk[None, :] > 0)
    nxt_of = jnp.min(jnp.where(later, ids[None, :], N_EXPERTS), axis=1)
    nxt_tab = jnp.where(nxt_of < N_EXPERTS, nxt_of, -1)
    nxt_exp = jnp.sum(jnp.where(blk_exp[:, None] == ids[None, :], nxt_tab[None, :], 0), axis=1)
    e = route_t[:, R_E:R_E + TOP_K, :].astype(jnp.int32)
    rank = route_t[:, R_RANK:R_RANK + TOP_K, :].astype(jnp.int32)
    seg = jnp.sum(jnp.where(e[..., None] == jnp.arange(N_EXPERTS), pstart, 0), axis=-1)
    dest = jnp.clip(seg + rank, 0, n_steps * MOE_BLK - 1)
    dest_tiles = dest.reshape(T // TM, 1, TOP_K * TM)
    return dict(dest_tiles=dest_tiles, fill_start=last_blk.astype(jnp.int32),
                blk_exp=blk_exp.astype(jnp.int32), nxt_exp=nxt_exp.astype(jnp.int32),
                n_used=n_used.reshape(1).astype(jnp.int32), n_rows=n_rows)


def _prep_layer(l, w_gk_up, b_gk, gla_norm, gmlp_norm, w_spatial, b_spatial, w_conv,
                w_router_group, b_router_group, w_router_expert, b_router_expert):
    wgk = jnp.concatenate(
        [w_gk_up[l], jnp.zeros((LANES - GLA_GATE_RANK, GLA_KDIM), F32)], axis=0).astype(BF16)
    wsp = w_spatial[l].transpose(1, 0, 2).reshape(GMLP_CHUNK, GMLP_HEADS * GMLP_CHUNK)
    bsp = jnp.repeat(b_spatial[l].T, GMLP_DH, axis=1)
    wconv = jnp.concatenate([w_conv[l], jnp.zeros((8 - CONV_K, CONV_WIDTH), F32)], axis=0)
    wr = jnp.concatenate(
        [w_router_group[l], w_router_expert[l],
         jnp.zeros((D_MODEL, ROUTER_COLS - N_GROUPS - N_EXPERTS), F32)], axis=1)
    wr_hi = wr.astype(BF16)
    wr_lo = (wr - wr_hi.astype(F32)).astype(BF16)
    br = jnp.concatenate(
        [b_router_group[l], b_router_expert[l],
         jnp.zeros((ROUTER_COLS - N_GROUPS - N_EXPERTS,), F32)])[None, :]
    return dict(
        wgk=wgk, bgk=b_gk[l][None, :], glan=gla_norm[l][None, :], gmn=gmlp_norm[l][None, :],
        wsp=wsp, bsp=bsp, wconv=wconv, wr_cat=jnp.concatenate([wr_hi, wr_lo], axis=1), br=br)


def kernel(x, attn_norm, w_in, w_gk_up, b_gk, gla_norm, gmlp_norm, w_spatial, b_spatial, w_conv, w_out, ffn_norm, w_router_group, b_router_group, w_router_expert, b_router_expert, w_gate, w_up, w_down, final_norm):
    B, S, D = x.shape
    T = B * S
    depth = w_in.shape[0]
    xr = x.reshape(T, D)
    w_in_t = jnp.swapaxes(w_in, 1, 2)
    moe = None
    for l in range(depth):
        p = _prep_layer(l, w_gk_up, b_gk, gla_norm, gmlp_norm, w_spatial, b_spatial, w_conv,
                        w_router_group, b_router_group, w_router_expert, b_router_expert)
        if moe is None:
            proj = _norm_proj(xr, attn_norm[l][None, :], w_in_t, l)
        else:
            xr, proj = _combine_norm_proj(moe["dest_tiles"], moe["x2"], moe["route"], moe["yb"],
                                          attn_norm[l][None, :], w_in_t, l)
        mixed = _mixers(proj, p["wgk"], p["bgk"], p["glan"], p["gmn"], p["wsp"], p["bsp"],
                        p["wconv"], B, S)
        x2, h2, route, route_t, counts_rec = _out_router(
            mixed, xr, w_out, l, ffn_norm[l][None, :], p["wr_cat"], p["br"])
        moe = _dispatch_tables(route_t, counts_rec, T)
        xb = _dispatch(moe["fill_start"], moe["n_used"], moe["dest_tiles"], h2, moe["n_rows"])
        yb = _experts(moe["blk_exp"], moe["nxt_exp"], moe["n_used"], xb, w_gate, w_up, w_down, l)
        moe.update(x2=x2, route=route, yb=yb)
    out = _combine_final_norm(moe["dest_tiles"], moe["x2"], moe["route"], moe["yb"],
                              final_norm[None, :])
    return out.reshape(B, S, D)
```

```python
import functools

import jax
import jax.numpy as jnp
from jax import lax
from jax.experimental import pallas as pl
from jax.experimental.pallas import tpu as pltpu

F32 = jnp.float32
BF16 = jnp.bfloat16

D_MODEL = 1024
RMS_EPS = 1e-6
GLA_HEADS = 4
GLA_WIDTH = 512
GLA_DV = 128
GLA_DK = 64
GLA_KDIM = 256
GLA_GATE_RANK = 16
GLA_GATE_NORM = 16.0
GLA_CHUNK = 64
GMLP_HEADS = 4
GMLP_WIDTH = 256
GMLP_DH = 64
GMLP_CHUNK = 128
CONV_WIDTH = 256
CONV_K = 3
N_GROUPS = 4
EXPERTS_PER_GROUP = 8
N_EXPERTS = 32
TOP_K = 2
D_EXPERT = 256

LANES = 128
C_Q, C_K, C_V, C_G = 0, 256, 512, 1024
C_U, C_VG, C_X, C_BG, C_CG, C_GKL = 1536, 1792, 2048, 2304, 2560, 2816
D_PROJ = C_GKL + LANES
D_IN = C_GKL + GLA_GATE_RANK

TM = 256
TS_MIX = 256
MOE_BLK = 256
ROUTER_COLS = LANES
SUBLANES = 8
ROW_TILES = D_MODEL // LANES
assert ROW_TILES == SUBLANES
VMEM_LIMIT = 56 * 1024 * 1024
R_E, R_RANK, R_W = 0, 2, 4


def _dot(a, b):
    return jnp.dot(a, b, preferred_element_type=F32)


def _split_bf16(x):
    hi = x.astype(BF16)
    lo = (x - hi.astype(F32)).astype(BF16)
    return hi, lo


def _rms(x, gain):
    return x * lax.rsqrt(jnp.mean(x * x, axis=-1, keepdims=True) + RMS_EPS) * gain


W_PREP_ROWS = 128
PROJ_CHUNK = 256


def _stage_w_in(wt_ref, wb_ref):
    for c0 in range(0, C_GKL, LANES):
        src = c0 if c0 < C_U else c0 + GLA_GATE_RANK
        wb_ref[:, c0:c0 + LANES] = wt_ref[0, src:src + LANES, :].T.astype(BF16)
    low = jnp.concatenate([wt_ref[0, C_U:C_U + GLA_GATE_RANK, :],
                           jnp.zeros((LANES - GLA_GATE_RANK, D_MODEL), F32)], axis=0)
    wb_ref[:, C_GKL:D_PROJ] = low.T.astype(BF16)


def _w_in_specs(layer):
    return pl.BlockSpec((1, D_IN, D_MODEL), lambda i: (layer, 0, 0), pipeline_mode=pl.Buffered(1))


def _norm_proj_kernel(x_ref, gain_ref, w_ref, proj_ref, wb_ref):
    @pl.when(pl.program_id(0) == 0)
    def _():
        _stage_w_in(w_ref, wb_ref)
    proj_ref[...] = _dot(_rms(x_ref[...], gain_ref[...]).astype(BF16), wb_ref[...]).astype(BF16)


def _norm_proj(x, gain, w_in, layer):
    T = x.shape[0]
    return pl.pallas_call(
        _norm_proj_kernel,
        grid=(T // TM,),
        in_specs=[
            pl.BlockSpec((TM, D_MODEL), lambda i: (i, 0)),
            pl.BlockSpec((1, D_MODEL), lambda i: (0, 0)),
            _w_in_specs(layer),
        ],
        out_specs=pl.BlockSpec((TM, D_PROJ), lambda i: (i, 0)),
        out_shape=jax.ShapeDtypeStruct((T, D_PROJ), BF16),
        scratch_shapes=[pltpu.VMEM((D_MODEL, D_PROJ), BF16)],
        compiler_params=pltpu.CompilerParams(
            dimension_semantics=("arbitrary",), vmem_limit_bytes=VMEM_LIMIT),
        name="norm_proj",
    )(x, gain, w_in)


def _row_gather_copy(yb_ref, buf_ref, sem_ref, slot, k, r, d):
    return pltpu.make_async_copy(yb_ref.at[d], buf_ref.at[slot, k, r], sem_ref.at[slot])


def _gather_start(dest_ref, yb_ref, buf_ref, sem_ref, slot, rows=range(TM)):
    for r in rows:
        for k in range(TOP_K):
            _row_gather_copy(yb_ref, buf_ref, sem_ref, slot, k, r,
                             dest_ref[0, 0, k * TM + r]).start(priority=k)


def _gather_wait(yb_ref, buf_ref, sem_ref, slot):
    for k in range(TOP_K):
        pltpu.make_async_copy(yb_ref.at[pl.ds(0, TM)], buf_ref.at[slot, k], sem_ref.at[slot]).wait()


def _combined_residual(dcur_ref, x_ref, route_ref, yb_ref, buf_ref, sem_ref):
    i = pl.program_id(0)
    slot = lax.rem(i, 2)

    @pl.when(i == 0)
    def _():
        _gather_start(dcur_ref, yb_ref, buf_ref, sem_ref, 0)

    _gather_wait(yb_ref, buf_ref, sem_ref, slot)
    w0 = route_ref[:, R_W:R_W + 1]
    w1 = route_ref[:, R_W + 1:R_W + 2]
    y0 = _row_tiles_chunks(buf_ref.at[slot, 0], TM)
    y1 = _row_tiles_chunks(buf_ref.at[slot, 1], TM)
    return jnp.concatenate(
        [x_ref[:, c * LANES:(c + 1) * LANES] + (w0 * y0[c] + w1 * y1[c]) for c in range(ROW_TILES)],
        axis=1)


def _prefetch_groups(n_groups):
    per = -(-TM // n_groups)
    return [range(g * per, min(TM, (g + 1) * per)) for g in range(n_groups)]


def _drain_last_prefetch(yb_ref, buf_ref, sem_ref):
    i = pl.program_id(0)

    @pl.when(i == pl.num_programs(0) - 1)
    def _():
        _gather_wait(yb_ref, buf_ref, sem_ref, 1 - lax.rem(i, 2))


def _combine_specs(n_tiles):
    smem_tile = lambda f: pl.BlockSpec((1, 1, TOP_K * TM), f, memory_space=pltpu.SMEM)
    return [
        smem_tile(lambda i: (i, 0, 0)),
        smem_tile(lambda i: (jnp.minimum(i + 1, n_tiles - 1), 0, 0)),
        pl.BlockSpec((TM, D_MODEL), lambda i: (i, 0)),
        pl.BlockSpec((TM, LANES), lambda i: (i, 0)),
        pl.BlockSpec(memory_space=pl.ANY),
    ]


_COMBINE_SCRATCH = [pltpu.VMEM((2, TOP_K, TM, ROW_TILES, LANES), F32),
                    pltpu.SemaphoreType.DMA((2,))]


def _combine_norm_proj_kernel(dcur_ref, dnxt_ref, x_ref, route_ref, yb_ref, gain_ref, w_ref,
                              xo_ref, proj_ref, buf_ref, sem_ref, wb_ref):
    @pl.when(pl.program_id(0) == 0)
    def _():
        _stage_w_in(w_ref, wb_ref)
    x = _combined_residual(dcur_ref, x_ref, route_ref, yb_ref, buf_ref, sem_ref)
    xo_ref[...] = x
    h = _rms(x, gain_ref[...]).astype(BF16)
    nxt = 1 - lax.rem(pl.program_id(0), 2)
    col_chunks = [(c0, min(c0 + PROJ_CHUNK, D_PROJ)) for c0 in range(0, D_PROJ, PROJ_CHUNK)]
    for rows, (c0, c1) in zip(_prefetch_groups(len(col_chunks)), col_chunks):
        _gather_start(dnxt_ref, yb_ref, buf_ref, sem_ref, nxt, rows)
        proj_ref[:, c0:c1] = _dot(h, wb_ref[:, c0:c1]).astype(BF16)
    _drain_last_prefetch(yb_ref, buf_ref, sem_ref)


def _combine_norm_proj(dest_tiles, x2, route, yb, gain, w_in, layer):
    T = x2.shape[0]
    n_tiles = T // TM
    return pl.pallas_call(
        _combine_norm_proj_kernel,
        grid=(n_tiles,),
        in_specs=_combine_specs(n_tiles) + [
            pl.BlockSpec((1, D_MODEL), lambda i: (0, 0)),
            _w_in_specs(layer),
        ],
        out_specs=[pl.BlockSpec((TM, D_MODEL), lambda i: (i, 0)),
                   pl.BlockSpec((TM, D_PROJ), lambda i: (i, 0))],
        out_shape=[jax.ShapeDtypeStruct((T, D_MODEL), F32),
                   jax.ShapeDtypeStruct((T, D_PROJ), BF16)],
        scratch_shapes=_COMBINE_SCRATCH + [pltpu.VMEM((D_MODEL, D_PROJ), BF16)],
        compiler_params=pltpu.CompilerParams(
            dimension_semantics=("arbitrary",), vmem_limit_bytes=VMEM_LIMIT),
        name="combine_norm_proj",
    )(dest_tiles, dest_tiles, x2, route, yb, gain, w_in)


def _combine_final_norm_kernel(dcur_ref, dnxt_ref, x_ref, route_ref, yb_ref, gain_ref,
                               o_ref, buf_ref, sem_ref):
    _gather_start(dnxt_ref, yb_ref, buf_ref, sem_ref, 1 - lax.rem(pl.program_id(0), 2))
    x = _combined_residual(dcur_ref, x_ref, route_ref, yb_ref, buf_ref, sem_ref)
    o_ref[...] = _rms(x, gain_ref[...])
    _drain_last_prefetch(yb_ref, buf_ref, sem_ref)


def _combine_final_norm(dest_tiles, x2, route, yb, gain):
    T = x2.shape[0]
    n_tiles = T // TM
    return pl.pallas_call(
        _combine_final_norm_kernel,
        grid=(n_tiles,),
        in_specs=_combine_specs(n_tiles) + [pl.BlockSpec((1, D_MODEL), lambda i: (0, 0))],
        out_specs=pl.BlockSpec((TM, D_MODEL), lambda i: (i, 0)),
        out_shape=jax.ShapeDtypeStruct((T, D_MODEL), F32),
        scratch_shapes=_COMBINE_SCRATCH,
        compiler_params=pltpu.CompilerParams(
            dimension_semantics=("arbitrary",), vmem_limit_bytes=VMEM_LIMIT),
        name="combine_final_norm",
    )(dest_tiles, dest_tiles, x2, route, yb, gain)


def _gelu_tanh(x):
    c = 0.7978845608028654
    return x * (0.5 * (1.0 + jnp.tanh(c * (x + 0.044715 * (x * x * x)))))


def _mixer_kernel(proj_ref, wgk_ref, bgk_ref, glan_ref, gmn_ref, wsp_ref, bsp_ref, wconv_ref,
                  out_ref, st_ref, hc_ref, lcat_ref, wm_ref):
    TS = TS_MIX
    n_gla = TS // GLA_CHUNK
    n_gm = TS // GMLP_CHUNK

    @pl.when(pl.program_id(1) == 0)
    def _():
        st_ref[...] = jnp.zeros_like(st_ref)
        hc_ref[...] = jnp.zeros_like(hc_ref)
        r = lax.broadcasted_iota(jnp.int32, (TS, TS), 0)
        c = lax.broadcasted_iota(jnp.int32, (TS, TS), 1)
        keep = ((r // GLA_CHUNK) == (c // GLA_CHUNK)) & (c <= r)
        lcat_ref[...] = jnp.where(keep, 1.0, 0.0).astype(BF16)
        t = lax.broadcasted_iota(jnp.int32, (GMLP_CHUNK, GMLP_HEADS * GMLP_CHUNK), 0)
        s = lax.broadcasted_iota(jnp.int32, (GMLP_CHUNK, GMLP_HEADS * GMLP_CHUNK), 1) % GMLP_CHUNK
        wm_ref[...] = jnp.where(s <= t, wsp_ref[...], 0.0).astype(BF16)

    lane256 = lax.broadcasted_iota(jnp.int32, (1, GLA_KDIM), 1)

    q = proj_ref[:, C_Q:C_Q + GLA_KDIM].astype(F32)
    k = proj_ref[:, C_K:C_K + GLA_KDIM].astype(F32)
    v_b = proj_ref[:, C_V:C_V + GLA_WIDTH]
    z = _dot(proj_ref[:, C_GKL:C_GKL + LANES], wgk_ref[...]) + bgk_ref[...]
    gk = (jnp.minimum(z, 0.0) - jnp.log1p(jnp.exp(-jnp.abs(z)))) * (1.0 / GLA_GATE_NORM)
    gk_hi, gk_lo = _split_bf16(gk)
    cs = _dot(lcat_ref[...], jnp.concatenate([gk_hi, gk_lo], axis=1))
    b = cs[:, :GLA_KDIM] + cs[:, GLA_KDIM:]
    b_last = [b[(c + 1) * GLA_CHUNK - 1:(c + 1) * GLA_CHUNK, :] for c in range(n_gla)]
    bl = jnp.concatenate(
        [jnp.broadcast_to(t, (GLA_CHUNK, GLA_KDIM)) for t in b_last], axis=0)
    q_dec = (q * (GLA_DK ** -0.5)) * jnp.exp(b)
    k_inv = (k * jnp.exp(-b)).astype(BF16)
    k_dec = (k * jnp.exp(bl - b)).astype(BF16)
    q_dec_b = q_dec.astype(BF16)

    zero_b = jnp.zeros_like(q_dec_b)
    q_stack = jnp.concatenate(
        [jnp.where((lane256 // GLA_DK) == h, q_dec_b, zero_b) for h in range(GLA_HEADS)], axis=0)
    scores = lax.dot_general(q_stack, k_inv, (((1,), (1,)), ((), ())),
                             preferred_element_type=F32)
    rt = lax.broadcasted_iota(jnp.int32, (TS, TS), 0)
    ct = lax.broadcasted_iota(jnp.int32, (TS, TS), 1)
    causal = ((rt // GLA_CHUNK) == (ct // GLA_CHUNK)) & (ct <= rt)
    o_heads = []
    for h in range(GLA_HEADS):
        p_h = jnp.where(causal, scores[h * TS:(h + 1) * TS, :], 0.0).astype(BF16)
        o_heads.append(_dot(p_h, v_b[:, h * GLA_DV:(h + 1) * GLA_DV]))

    sr = lax.broadcasted_iota(jnp.int32, (GLA_WIDTH, GLA_KDIM), 0) // GLA_DV
    sc = lax.broadcasted_iota(jnp.int32, (GLA_WIDTH, GLA_KDIM), 1) // GLA_DK
    bd_mask = sr == sc
    o_inter = []
    for c in range(n_gla):
        rows = slice(c * GLA_CHUNK, (c + 1) * GLA_CHUNK)
        st = st_ref[...]
        o_inter.append(lax.dot_general(q_dec_b[rows], st.astype(BF16), (((1,), (1,)), ((), ())),
                                       preferred_element_type=F32))
        upd = lax.dot_general(v_b[rows], k_dec[rows], (((0,), (0,)), ((), ())),
                              preferred_element_type=F32)
        decay = jnp.exp(b_last[c])
        st_ref[...] = st * decay + jnp.where(bd_mask, upd, 0.0)
    o_inter = jnp.concatenate(o_inter, axis=0)

    for h in range(GLA_HEADS):
        cols = slice(h * GLA_DV, (h + 1) * GLA_DV)
        o = o_heads[h] + o_inter[:, cols]
        o = o * lax.rsqrt(jnp.mean(o * o, axis=-1, keepdims=True) + RMS_EPS) * glan_ref[...]
        g = proj_ref[:, C_G + h * GLA_DV:C_G + (h + 1) * GLA_DV].astype(F32)
        out_ref[:, cols] = (o * (g * (1.0 / (1.0 + jnp.exp(-g))))).astype(out_ref.dtype)

    u = _gelu_tanh(proj_ref[:, C_U:C_U + GMLP_WIDTH].astype(F32))
    vg = _gelu_tanh(proj_ref[:, C_VG:C_VG + GMLP_WIDTH].astype(F32))
    hr = lax.broadcasted_iota(jnp.int32, (GMLP_WIDTH, GMLP_WIDTH), 0) // GMLP_DH
    hcn = lax.broadcasted_iota(jnp.int32, (GMLP_WIDTH, GMLP_WIDTH), 1) // GMLP_DH
    head_mean = jnp.where(hr == hcn, 1.0 / GMLP_DH, 0.0).astype(BF16)
    sq_hi, sq_lo = _split_bf16(vg * vg)
    ms = _dot(sq_hi, head_mean) + _dot(sq_lo, head_mean)
    v32 = vg * lax.rsqrt(ms + RMS_EPS) * gmn_ref[...]
    for c in range(n_gm):
        rows = slice(c * GMLP_CHUNK, (c + 1) * GMLP_CHUNK)
        vc = v32[rows].astype(BF16)
        zc = jnp.zeros_like(vc)
        rhs = jnp.concatenate(
            [jnp.where((lane256 // GMLP_DH) == h, vc, zc) for h in range(GMLP_HEADS)], axis=0)
        mixed = _dot(wm_ref[...], rhs) + bsp_ref[...]
        out_ref[rows, GLA_WIDTH:GLA_WIDTH + GMLP_WIDTH] = (u[rows] * mixed).astype(out_ref.dtype)

    hcv = (proj_ref[:, C_CG:C_CG + CONV_WIDTH].astype(F32)
           * proj_ref[:, C_X:C_X + CONV_WIDTH].astype(F32))
    hc_ref[8:8 + TS, :] = hcv
    y = (wconv_ref[2:3, :] * hcv + wconv_ref[1:2, :] * hc_ref[7:7 + TS, :]
         + wconv_ref[0:1, :] * hc_ref[6:6 + TS, :])
    out_ref[:, GLA_WIDTH + GMLP_WIDTH:] = (
        proj_ref[:, C_BG:C_BG + CONV_WIDTH].astype(F32) * y).astype(out_ref.dtype)
    hc_ref[0:8, :] = hc_ref[TS:TS + 8, :]


def _mixers(proj, wgk, bgk, glan, gmn, wsp, bsp, wconv, batch, seq):
    n_seq = seq // TS_MIX
    full = lambda shape: pl.BlockSpec(shape, lambda b, i: (0,) * len(shape))
    return pl.pallas_call(
        _mixer_kernel,
        grid=(batch, n_seq),
        in_specs=[
            pl.BlockSpec((TS_MIX, D_PROJ), lambda b, i: (b * n_seq + i, 0)),
            full((LANES, GLA_KDIM)), full((1, GLA_KDIM)), full((1, GLA_DV)), full((1, GMLP_WIDTH)),
            full((GMLP_CHUNK, GMLP_HEADS * GMLP_CHUNK)), full((GMLP_CHUNK, GMLP_WIDTH)),
            full((8, CONV_WIDTH)),
        ],
        out_specs=pl.BlockSpec((TS_MIX, D_MODEL), lambda b, i: (b * n_seq + i, 0)),
        out_shape=jax.ShapeDtypeStruct((batch * seq, D_MODEL), BF16),
        scratch_shapes=[
            pltpu.VMEM((GLA_WIDTH, GLA_KDIM), F32),
            pltpu.VMEM((TS_MIX + 8, CONV_WIDTH), F32),
            pltpu.VMEM((TS_MIX, TS_MIX), BF16),
            pltpu.VMEM((GMLP_CHUNK, GMLP_HEADS * GMLP_CHUNK), BF16),
        ],
        compiler_params=pltpu.CompilerParams(
            dimension_semantics=("arbitrary", "arbitrary"), vmem_limit_bytes=VMEM_LIMIT),
        name="mixers",
    )(proj, wgk, bgk, glan, gmn, wsp, bsp, wconv)


def _row_tiles_store(tiles_ref, x):
    rows = x.shape[0]
    flat = tiles_ref.reshape(rows * ROW_TILES, LANES)
    for c in range(ROW_TILES):
        flat[pl.ds(c, rows, stride=ROW_TILES), :] = x[:, c * LANES:(c + 1) * LANES]


def _row_tiles_chunks(tiles_ref, rows):
    flat = tiles_ref.reshape(rows * ROW_TILES, LANES)
    return [flat[pl.ds(c, rows, stride=ROW_TILES), :] for c in range(ROW_TILES)]


def _out_router_kernel(mix_ref, x_ref, wo_ref, gain_ref, wrc_ref, br_ref,
                       x2_ref, h2_ref, route_ref, route_t_ref, cnt_ref, tri_ref, wob_ref, lg_ref):
    i = pl.program_id(0)

    @pl.when(i == 0)
    def _():
        cnt_ref[...] = jnp.zeros_like(cnt_ref)
        lg_ref[...] = jnp.zeros_like(lg_ref)
        r = lax.broadcasted_iota(jnp.int32, (TM, TM), 0)
        c = lax.broadcasted_iota(jnp.int32, (TM, TM), 1)
        tri_ref[...] = jnp.where(c < r, 1.0, 0.0).astype(BF16)
        for r0 in range(0, D_MODEL, W_PREP_ROWS):
            wob_ref[r0:r0 + W_PREP_ROWS, :] = wo_ref[0, r0:r0 + W_PREP_ROWS, :].astype(BF16)

    lg = lg_ref[...]
    x2 = x_ref[...] + _dot(mix_ref[...], wob_ref[...])
    x2_ref[...] = x2
    h = _rms(x2, gain_ref[...])
    h_hi, h_lo = _split_bf16(h)
    h2_ref[...] = h_hi
    hh_hl = _dot(h_hi, wrc_ref[...])
    lg_ref[...] = (hh_hl[:, :ROUTER_COLS] + hh_hl[:, ROUTER_COLS:]
                   + _dot(h_lo, wrc_ref[:, :ROUTER_COLS]) + br_ref[...])

    lane = lax.broadcasted_iota(jnp.int32, (TM, LANES), 1).astype(F32)
    neg = -jnp.inf
    is_g = lane < N_GROUPS
    gl = jnp.where(is_g, lg, neg)
    gmax = jnp.max(gl, axis=1, keepdims=True)
    g_top = jnp.min(jnp.where(gl == gmax, lane, float(LANES)), axis=1, keepdims=True)
    g_w = 1.0 / jnp.sum(jnp.where(is_g, jnp.exp(lg - gmax), 0.0), axis=1, keepdims=True)
    first = N_GROUPS + EXPERTS_PER_GROUP * g_top
    el = jnp.where((lane >= first) & (lane < first + EXPERTS_PER_GROUP), lg, neg)
    m1 = jnp.max(el, axis=1, keepdims=True)
    i1 = jnp.min(jnp.where(el == m1, lane, float(LANES)), axis=1, keepdims=True)
    el2 = jnp.where(lane == i1, neg, el)
    m2 = jnp.max(el2, axis=1, keepdims=True)
    i2 = jnp.min(jnp.where(el2 == m2, lane, float(LANES)), axis=1, keepdims=True)
    ratio = jnp.exp(m2 - m1)
    w1 = g_w / (1.0 + ratio)
    w2 = w1 * ratio

    oh1 = jnp.where(lane == i1, 1.0, 0.0)
    oh2 = jnp.where(lane == i2, 1.0, 0.0)
    oh = jnp.where(i > 0, oh1 + oh2, 0.0)
    before = _dot(tri_ref[...], oh.astype(BF16)) + cnt_ref[0:1, :]
    rank1 = jnp.sum(oh1 * before, axis=1, keepdims=True)
    rank2 = jnp.sum(oh2 * before, axis=1, keepdims=True)
    cnt_ref[...] = cnt_ref[...] + jnp.sum(oh, axis=0, keepdims=True)

    rec = jnp.zeros((TM, LANES), F32)
    for col, val in ((R_E, i1 - N_GROUPS), (R_E + 1, i2 - N_GROUPS), (R_RANK, rank1),
                     (R_RANK + 1, rank2), (R_W, w1), (R_W + 1, w2)):
        rec = jnp.where(lane == col, val, rec)
    route_ref[...] = rec
    route_t_ref[0] = rec.T[0:SUBLANES, :]


def _out_router(mixed, x, w_out, layer, gain, wr_cat, br):
    T = x.shape[0]
    n = T // TM
    row = lambda w: pl.BlockSpec((TM, w), lambda i: (jnp.minimum(i, n - 1), 0))
    lag = lambda i: jnp.maximum(i - 1, 0)
    full = lambda shape: pl.BlockSpec(shape, lambda i: (0, 0))
    wo_spec = pl.BlockSpec((1, D_MODEL, D_MODEL), lambda i: (layer, 0, 0),
                           pipeline_mode=pl.Buffered(1))
    return pl.pallas_call(
        _out_router_kernel,
        grid=(n + 1,),
        in_specs=[row(D_MODEL), row(D_MODEL), wo_spec, full((1, D_MODEL)),
                  full((D_MODEL, 2 * ROUTER_COLS)), full((1, ROUTER_COLS))],
        out_specs=[row(D_MODEL), row(D_MODEL),
                   pl.BlockSpec((TM, LANES), lambda i: (lag(i), 0)),
                   pl.BlockSpec((1, SUBLANES, TM), lambda i: (lag(i), 0, 0)), full((8, LANES))],
        out_shape=[jax.ShapeDtypeStruct((T, D_MODEL), F32),
                   jax.ShapeDtypeStruct((T, D_MODEL), BF16),
                   jax.ShapeDtypeStruct((T, LANES), F32),
                   jax.ShapeDtypeStruct((T // TM, SUBLANES, TM), F32),
                   jax.ShapeDtypeStruct((8, LANES), F32)],
        scratch_shapes=[pltpu.VMEM((TM, TM), BF16), pltpu.VMEM((D_MODEL, D_MODEL), BF16),
                        pltpu.VMEM((TM, ROUTER_COLS), F32)],
        compiler_params=pltpu.CompilerParams(
            dimension_semantics=("arbitrary",), vmem_limit_bytes=VMEM_LIMIT),
        name="out_router",
    )(mixed, x, w_out, gain, wr_cat, br)


def _dispatch_kernel(fill_ref, nu_ref, dest_ref, h_ref, xb_ref, zero_ref, sem_ref, zsem_ref,
                     stage_ref):
    i = pl.program_id(0)
    par = lax.rem(i, 2)

    @pl.when(i == 0)
    def _():
        zero_ref[...] = jnp.zeros_like(zero_ref)
        fills = [(fill_ref[e] >= 0, pltpu.make_async_copy(
            zero_ref, xb_ref.at[pl.ds(pl.multiple_of(jnp.maximum(fill_ref[e], 0), MOE_BLK), MOE_BLK)],
            zsem_ref)) for e in range(N_EXPERTS)]
        n_blocks = xb_ref.shape[0] // MOE_BLK
        fills += [(j >= nu_ref[0], pltpu.make_async_copy(
            zero_ref, xb_ref.at[pl.ds(j * MOE_BLK, MOE_BLK)], zsem_ref))
            for j in range(n_blocks - N_EXPERTS, n_blocks)]
        for cond, f in fills:
            pl.when(cond)(f.start)
        for cond, f in fills:
            pl.when(cond)(f.wait)

    _row_tiles_store(stage_ref.at[par], h_ref[...].astype(F32))
    for r in range(TM):
        for k in range(TOP_K):
            pltpu.make_async_copy(stage_ref.at[par, r], xb_ref.at[dest_ref[0, 0, k * TM + r]],
                                  sem_ref.at[par]).start(priority=k)

    def wait_tile(p):
        for _ in range(TOP_K):
            pltpu.make_async_copy(stage_ref.at[p], xb_ref.at[pl.ds(0, TM)], sem_ref.at[p]).wait()

    pl.when(i > 0)(lambda: wait_tile(1 - par))
    pl.when(i == pl.num_programs(0) - 1)(lambda: wait_tile(par))


def _dispatch(fill_start, n_used, dest_tiles, h2, n_rows):
    T = h2.shape[0]
    grid_spec = pltpu.PrefetchScalarGridSpec(
        num_scalar_prefetch=2,
        grid=(T // TM,),
        in_specs=[
            pl.BlockSpec((1, 1, TOP_K * TM), lambda i, fs, nu: (i, 0, 0), memory_space=pltpu.SMEM),
            pl.BlockSpec((TM, D_MODEL), lambda i, fs, nu: (i, 0)),
        ],
        out_specs=pl.BlockSpec(memory_space=pl.ANY),
        scratch_shapes=[pltpu.VMEM((MOE_BLK, ROW_TILES, LANES), F32),
                        pltpu.SemaphoreType.DMA((2,)), pltpu.SemaphoreType.DMA(()),
                        pltpu.VMEM((2, TM, ROW_TILES, LANES), F32)],
    )
    return pl.pallas_call(
        _dispatch_kernel,
        grid_spec=grid_spec,
        out_shape=jax.ShapeDtypeStruct((n_rows, ROW_TILES, LANES), F32),
        compiler_params=pltpu.CompilerParams(dimension_semantics=("arbitrary",)),
        name="dispatch",
    )(fill_start, n_used, dest_tiles, h2)


BLOCK_COPY_PARTS = 4
X_SLOTS = 4
Y_SLOTS = 3


class _CopyGroup:
    def __init__(self, copies):
        self.copies = copies

    def start(self):
        for n, c in enumerate(self.copies):
            c.start(priority=n % 2)

    def wait(self):
        for c in self.copies:
            c.wait()


def _expert_kernel(be_ref, nxt_ref, nu_ref, xb_ref, wg_ref, wu_ref, wd_ref, yb_ref,
                   xbuf_ref, ybuf_ref, wgs_ref, wus_ref, wds_ref, wgb_ref, wub_ref, wdb_ref,
                   xsem_ref, ysem_ref, wsem_ref, *, layer, n_blocks):
    n_used = nu_ref[0]

    part = MOE_BLK // BLOCK_COPY_PARTS

    def x_copy(j, slot):
        return _CopyGroup([pltpu.make_async_copy(
            xb_ref.at[pl.ds(j * MOE_BLK + p * part, part)],
            xbuf_ref.at[slot, pl.ds(p * part, part)], xsem_ref.at[slot])
            for p in range(BLOCK_COPY_PARTS)])

    def y_copy(j, slot):
        return _CopyGroup([pltpu.make_async_copy(
            ybuf_ref.at[slot, pl.ds(p * part, part)],
            yb_ref.at[pl.ds(j * MOE_BLK + p * part, part)], ysem_ref.at[slot])
            for p in range(BLOCK_COPY_PARTS)])

    def w_copies(e, ws):
        return [pltpu.make_async_copy(src.at[layer, e], dst.at[ws], wsem_ref.at[ws])
                for src, dst in ((wg_ref, wgs_ref), (wu_ref, wus_ref), (wd_ref, wds_ref))]

    for j0 in range(X_SLOTS - 1):
        x_copy(j0, j0).start()
    for c in w_copies(be_ref[0], 0):
        c.start()

    def block(j, ws):
        slot = lax.rem(j, Y_SLOTS)
        xslot = lax.rem(j, X_SLOTS)
        first = (j == 0) | (be_ref[j] != be_ref[jnp.maximum(j - 1, 0)])
        ws = jnp.where(first & (j > 0), 1 - ws, ws)

        @pl.when(first)
        def _():
            for c in w_copies(be_ref[j], ws):
                c.wait()
            for r0 in range(0, D_MODEL, W_PREP_ROWS):
                rows = slice(r0, r0 + W_PREP_ROWS)
                wgb_ref[rows, :] = wgs_ref[ws, rows, :].astype(BF16)
                wub_ref[rows, :] = wus_ref[ws, rows, :].astype(BF16)
            for r0 in range(0, D_EXPERT, W_PREP_ROWS):
                rows = slice(r0, r0 + W_PREP_ROWS)
                wdb_ref[rows, :] = wds_ref[ws, rows, :].astype(BF16)

            @pl.when(nxt_ref[j] >= 0)
            def _():
                for c in w_copies(nxt_ref[j], 1 - ws):
                    c.start()

        ahead = j + X_SLOTS - 1

        @pl.when(ahead < n_used)
        def _():
            x_copy(ahead, lax.rem(ahead, X_SLOTS)).start()

        x_copy(j, xslot).wait()
        x = jnp.concatenate(
            [c.astype(BF16) for c in _row_tiles_chunks(xbuf_ref.at[xslot], MOE_BLK)], axis=1)
        g = _dot(x, wgb_ref[...])
        u = _dot(x, wub_ref[...])
        h = (g * (1.0 / (1.0 + jnp.exp(-g)))) * u
        y = _dot(h.astype(BF16), wdb_ref[...])

        @pl.when(j >= Y_SLOTS)
        def _():
            y_copy(j - Y_SLOTS, slot).wait()

        _row_tiles_store(ybuf_ref.at[slot], y)
        y_copy(j, slot).start()
        return ws

    lax.fori_loop(0, n_used, block, jnp.int32(0))

    for back in range(Y_SLOTS, 0, -1):
        y_copy(n_used - back, lax.rem(n_used - back, Y_SLOTS)).wait()

    ybuf_ref[0] = jnp.zeros((MOE_BLK, ROW_TILES, LANES), F32)

    def fill(j, carry):
        y_copy(j, 0).start()
        return carry

    def fill_wait(j, carry):
        y_copy(j, 0).wait()
        return carry

    lax.fori_loop(n_used, n_blocks, fill, 0)
    lax.fori_loop(n_used, n_blocks, fill_wait, 0)


def _experts(blk_exp, nxt_exp, n_used, xb, w_gate, w_up, w_down, layer):
    n_blocks = blk_exp.shape[0]
    any_spec = pl.BlockSpec(memory_space=pl.ANY)
    blk = (MOE_BLK, ROW_TILES, LANES)
    grid_spec = pltpu.PrefetchScalarGridSpec(
        num_scalar_prefetch=3,
        grid=(1,),
        in_specs=[any_spec, any_spec, any_spec, any_spec],
        out_specs=any_spec,
        scratch_shapes=[
            pltpu.VMEM((X_SLOTS,) + blk, F32), pltpu.VMEM((Y_SLOTS,) + blk, F32),
            pltpu.VMEM((2, D_MODEL, D_EXPERT), F32), pltpu.VMEM((2, D_MODEL, D_EXPERT), F32),
            pltpu.VMEM((2, D_EXPERT, D_MODEL), F32),
            pltpu.VMEM((D_MODEL, D_EXPERT), BF16), pltpu.VMEM((D_MODEL, D_EXPERT), BF16),
            pltpu.VMEM((D_EXPERT, D_MODEL), BF16),
            pltpu.SemaphoreType.DMA((X_SLOTS,)), pltpu.SemaphoreType.DMA((Y_SLOTS,)),
            pltpu.SemaphoreType.DMA((2,)),
        ],
    )
    return pl.pallas_call(
        functools.partial(_expert_kernel, layer=layer, n_blocks=n_blocks),
        grid_spec=grid_spec,
        out_shape=jax.ShapeDtypeStruct((n_blocks * MOE_BLK, ROW_TILES, LANES), F32),
        compiler_params=pltpu.CompilerParams(
            dimension_semantics=("arbitrary",), vmem_limit_bytes=VMEM_LIMIT),
        name="experts",
    )(blk_exp, nxt_exp, n_used, xb, w_gate, w_up, w_down)


def _dispatch_tables(route_t, counts_rec, T):
    counts = counts_rec[0, N_GROUPS:N_GROUPS + N_EXPERTS].astype(jnp.int32)
    n_steps = (T * TOP_K) // MOE_BLK + N_EXPERTS
    nblk = (counts + MOE_BLK - 1) // MOE_BLK
    bend = jnp.cumsum(nblk)
    pstart = (bend - nblk) * MOE_BLK
    n_used = bend[-1]
    j = jnp.minimum(jnp.arange(n_steps, dtype=jnp.int32), n_used - 1)
    blk_exp = jnp.minimum(jnp.sum(j[:, None] >= bend[None, :], axis=1), N_EXPERTS - 1)
    n_rows = n_steps * MOE_BLK
    last_blk = jnp.where(counts > 0, (bend - 1) * MOE_BLK, -1)
    ids = jnp.arange(N_EXPERTS, dtype=jnp.int32)
    later = (ids[None, :] > ids[:, None]) & (nblk[None, :] > 0)
    nxt_of = jnp.min(jnp.where(later, ids[None, :], N_EXPERTS), axis=1)
    nxt_tab = jnp.where(nxt_of < N_EXPERTS, nxt_of, -1)
    nxt_exp = jnp.sum(jnp.where(blk_exp[:, None] == ids[None, :], nxt_tab[None, :], 0), axis=1)
    e = route_t[:, R_E:R_E + TOP_K, :].astype(jnp.int32)
    rank = route_t[:, R_RANK:R_RANK + TOP_K, :].astype(jnp.int32)
    seg = jnp.sum(jnp.where(e[..., None] == jnp.arange(N_EXPERTS), pstart, 0), axis=-1)
    dest = jnp.clip(seg + rank, 0, n_steps * MOE_BLK - 1)
    dest_tiles = dest.reshape(T // TM, 1, TOP_K * TM)
    return dict(dest_tiles=dest_tiles, fill_start=last_blk.astype(jnp.int32),
                blk_exp=blk_exp.astype(jnp.int32), nxt_exp=nxt_exp.astype(jnp.int32),
                n_used=n_used.reshape(1).astype(jnp.int32), n_rows=n_rows)


def _prep_layer(l, w_gk_up, b_gk, gla_norm, gmlp_norm, w_spatial, b_spatial, w_conv,
                w_router_group, b_router_group, w_router_expert, b_router_expert):
    wgk = jnp.concatenate(
        [w_gk_up[l], jnp.zeros((LANES - GLA_GATE_RANK, GLA_KDIM), F32)], axis=0).astype(BF16)
    wsp = w_spatial[l].transpose(1, 0, 2).reshape(GMLP_CHUNK, GMLP_HEADS * GMLP_CHUNK)
    bsp = jnp.repeat(b_spatial[l].T, GMLP_DH, axis=1)
    wconv = jnp.concatenate([w_conv[l], jnp.zeros((8 - CONV_K, CONV_WIDTH), F32)], axis=0)
    wr = jnp.concatenate(
        [w_router_group[l], w_router_expert[l],
         jnp.zeros((D_MODEL, ROUTER_COLS - N_GROUPS - N_EXPERTS), F32)], axis=1)
    wr_hi = wr.astype(BF16)
    wr_lo = (wr - wr_hi.astype(F32)).astype(BF16)
    br = jnp.concatenate(
        [b_router_group[l], b_router_expert[l],
         jnp.zeros((ROUTER_COLS - N_GROUPS - N_EXPERTS,), F32)])[None, :]
    return dict(
        wgk=wgk, bgk=b_gk[l][None, :], glan=gla_norm[l][None, :], gmn=gmlp_norm[l][None, :],
        wsp=wsp, bsp=bsp, wconv=wconv, wr_cat=jnp.concatenate([wr_hi, wr_lo], axis=1), br=br)


def kernel(x, attn_norm, w_in, w_gk_up, b_gk, gla_norm, gmlp_norm, w_spatial, b_spatial, w_conv, w_out, ffn_norm, w_router_group, b_router_group, w_router_expert, b_router_expert, w_gate, w_up, w_down, final_norm):
    B, S, D = x.shape
    T = B * S
    depth = w_in.shape[0]
    xr = x.reshape(T, D)
    w_in_t = jnp.swapaxes(w_in, 1, 2)
    moe = None
    for l in range(depth):
        p = _prep_layer(l, w_gk_up, b_gk, gla_norm, gmlp_norm, w_spatial, b_spatial, w_conv,
                        w_router_group, b_router_group, w_router_expert, b_router_expert)
        if moe is None:
            proj = _norm_proj(xr, attn_norm[l][None, :], w_in_t, l)
        else:
            xr, proj = _combine_norm_proj(moe["dest_tiles"], moe["x2"], moe["route"], moe["yb"],
                                          attn_norm[l][None, :], w_in_t, l)
        mixed = _mixers(proj, p["wgk"], p["bgk"], p["glan"], p["gmn"], p["wsp"], p["bsp"],
                        p["wconv"], B, S)
        x2, h2, route, route_t, counts_rec = _out_router(
            mixed, xr, w_out, l, ffn_norm[l][None, :], p["wr_cat"], p["br"])
        moe = _dispatch_tables(route_t, counts_rec, T)
        xb = _dispatch(moe["fill_start"], moe["n_used"], moe["dest_tiles"], h2, moe["n_rows"])
        yb = _experts(moe["blk_exp"], moe["nxt_exp"], moe["n_used"], xb, w_gate, w_up, w_down, l)
        moe.update(x2=x2, route=route, yb=yb)
    out = _combine_final_norm(moe["dest_tiles"], moe["x2"], moe["route"], moe["yb"],
                              final_norm[None, :])
    return out.reshape(B, S, D)
```

```python
import functools

import jax
import jax.numpy as jnp
from jax import lax
from jax.experimental import pallas as pl
from jax.experimental.pallas import tpu as pltpu

F32 = jnp.float32
BF16 = jnp.bfloat16

D_MODEL = 1024
RMS_EPS = 1e-6
GLA_HEADS = 4
GLA_WIDTH = 512
GLA_DV = 128
GLA_DK = 64
GLA_KDIM = 256
GLA_GATE_RANK = 16
GLA_GATE_NORM = 16.0
GLA_CHUNK = 64
GMLP_HEADS = 4
GMLP_WIDTH = 256
GMLP_DH = 64
GMLP_CHUNK = 128
CONV_WIDTH = 256
CONV_K = 3
N_GROUPS = 4
EXPERTS_PER_GROUP = 8
N_EXPERTS = 32
TOP_K = 2
D_EXPERT = 256

LANES = 128
C_Q, C_K, C_V, C_G = 0, 256, 512, 1024
C_U, C_VG, C_X, C_BG, C_CG, C_GKL = 1536, 1792, 2048, 2304, 2560, 2816
D_PROJ = C_GKL + LANES
D_IN = C_GKL + GLA_GATE_RANK

TM = 256
TS_MIX = 256
MOE_BLK = 256
ROUTER_COLS = LANES
SUBLANES = 8
ROW_TILES = D_MODEL // LANES
assert ROW_TILES == SUBLANES
VMEM_LIMIT = 56 * 1024 * 1024
R_E, R_RANK, R_W = 0, 2, 4


def _dot(a, b):
    return jnp.dot(a, b, preferred_element_type=F32)


def _split_bf16(x):
    hi = x.astype(BF16)
    lo = (x - hi.astype(F32)).astype(BF16)
    return hi, lo


def _rms(x, gain):
    return x * lax.rsqrt(jnp.mean(x * x, axis=-1, keepdims=True) + RMS_EPS) * gain


W_PREP_ROWS = 128
PROJ_CHUNK = 256


def _stage_w_in(wt_ref, wb_ref):
    for c0 in range(0, C_GKL, LANES):
        src = c0 if c0 < C_U else c0 + GLA_GATE_RANK
        wb_ref[:, c0:c0 + LANES] = wt_ref[0, src:src + LANES, :].T.astype(BF16)
    low = jnp.concatenate([wt_ref[0, C_U:C_U + GLA_GATE_RANK, :],
                           jnp.zeros((LANES - GLA_GATE_RANK, D_MODEL), F32)], axis=0)
    wb_ref[:, C_GKL:D_PROJ] = low.T.astype(BF16)


def _w_in_specs(layer):
    return pl.BlockSpec((1, D_IN, D_MODEL), lambda i: (layer, 0, 0), pipeline_mode=pl.Buffered(1))


def _norm_proj_kernel(x_ref, gain_ref, w_ref, proj_ref, wb_ref):
    @pl.when(pl.program_id(0) == 0)
    def _():
        _stage_w_in(w_ref, wb_ref)
    proj_ref[...] = _dot(_rms(x_ref[...], gain_ref[...]).astype(BF16), wb_ref[...]).astype(BF16)


def _norm_proj(x, gain, w_in, layer):
    T = x.shape[0]
    return pl.pallas_call(
        _norm_proj_kernel,
        grid=(T // TM,),
        in_specs=[
            pl.BlockSpec((TM, D_MODEL), lambda i: (i, 0)),
            pl.BlockSpec((1, D_MODEL), lambda i: (0, 0)),
            _w_in_specs(layer),
        ],
        out_specs=pl.BlockSpec((TM, D_PROJ), lambda i: (i, 0)),
        out_shape=jax.ShapeDtypeStruct((T, D_PROJ), BF16),
        scratch_shapes=[pltpu.VMEM((D_MODEL, D_PROJ), BF16)],
        compiler_params=pltpu.CompilerParams(
            dimension_semantics=("arbitrary",), vmem_limit_bytes=VMEM_LIMIT),
        name="norm_proj",
    )(x, gain, w_in)


def _row_gather_copy(yb_ref, buf_ref, sem_ref, slot, k, r, d):
    return pltpu.make_async_copy(yb_ref.at[d], buf_ref.at[slot, k, r], sem_ref.at[slot])


def _gather_start(dest_ref, yb_ref, buf_ref, sem_ref, slot, rows=range(TM)):
    for r in rows:
        for k in range(TOP_K):
            _row_gather_copy(yb_ref, buf_ref, sem_ref, slot, k, r,
                             dest_ref[0, 0, k * TM + r]).start(priority=k)


def _gather_wait(yb_ref, buf_ref, sem_ref, slot):
    for k in range(TOP_K):
        pltpu.make_async_copy(yb_ref.at[pl.ds(0, TM)], buf_ref.at[slot, k], sem_ref.at[slot]).wait()


def _combined_residual(dcur_ref, x_ref, route_ref, yb_ref, buf_ref, sem_ref):
    i = pl.program_id(0)
    slot = lax.rem(i, 2)

    @pl.when(i == 0)
    def _():
        _gather_start(dcur_ref, yb_ref, buf_ref, sem_ref, 0)

    _gather_wait(yb_ref, buf_ref, sem_ref, slot)
    w0 = route_ref[:, R_W:R_W + 1]
    w1 = route_ref[:, R_W + 1:R_W + 2]
    y0 = _row_tiles_chunks(buf_ref.at[slot, 0], TM)
    y1 = _row_tiles_chunks(buf_ref.at[slot, 1], TM)
    return jnp.concatenate(
        [x_ref[:, c * LANES:(c + 1) * LANES] + (w0 * y0[c] + w1 * y1[c]) for c in range(ROW_TILES)],
        axis=1)


def _prefetch_groups(n_groups):
    per = -(-TM // n_groups)
    return [range(g * per, min(TM, (g + 1) * per)) for g in range(n_groups)]


def _drain_last_prefetch(yb_ref, buf_ref, sem_ref):
    i = pl.program_id(0)

    @pl.when(i == pl.num_programs(0) - 1)
    def _():
        _gather_wait(yb_ref, buf_ref, sem_ref, 1 - lax.rem(i, 2))


def _combine_specs(n_tiles):
    smem_tile = lambda f: pl.BlockSpec((1, 1, TOP_K * TM), f, memory_space=pltpu.SMEM)
    return [
        smem_tile(lambda i: (i, 0, 0)),
        smem_tile(lambda i: (jnp.minimum(i + 1, n_tiles - 1), 0, 0)),
        pl.BlockSpec((TM, D_MODEL), lambda i: (i, 0)),
        pl.BlockSpec((TM, LANES), lambda i: (i, 0)),
        pl.BlockSpec(memory_space=pl.ANY),
    ]


_COMBINE_SCRATCH = [pltpu.VMEM((2, TOP_K, TM, ROW_TILES, LANES), F32),
                    pltpu.SemaphoreType.DMA((2,))]


def _combine_norm_proj_kernel(dcur_ref, dnxt_ref, x_ref, route_ref, yb_ref, gain_ref, w_ref,
                              xo_ref, proj_ref, buf_ref, sem_ref, wb_ref):
    @pl.when(pl.program_id(0) == 0)
    def _():
        _stage_w_in(w_ref, wb_ref)
    x = _combined_residual(dcur_ref, x_ref, route_ref, yb_ref, buf_ref, sem_ref)
    xo_ref[...] = x
    h = _rms(x, gain_ref[...]).astype(BF16)
    nxt = 1 - lax.rem(pl.program_id(0), 2)
    col_chunks = [(c0, min(c0 + PROJ_CHUNK, D_PROJ)) for c0 in range(0, D_PROJ, PROJ_CHUNK)]
    for rows, (c0, c1) in zip(_prefetch_groups(len(col_chunks)), col_chunks):
        _gather_start(dnxt_ref, yb_ref, buf_ref, sem_ref, nxt, rows)
        proj_ref[:, c0:c1] = _dot(h, wb_ref[:, c0:c1]).astype(BF16)
    _drain_last_prefetch(yb_ref, buf_ref, sem_ref)


def _combine_norm_proj(dest_tiles, x2, route, yb, gain, w_in, layer):
    T = x2.shape[0]
    n_tiles = T // TM
    return pl.pallas_call(
        _combine_norm_proj_kernel,
        grid=(n_tiles,),
        in_specs=_combine_specs(n_tiles) + [
            pl.BlockSpec((1, D_MODEL), lambda i: (0, 0)),
            _w_in_specs(layer),
        ],
        out_specs=[pl.BlockSpec((TM, D_MODEL), lambda i: (i, 0)),
                   pl.BlockSpec((TM, D_PROJ), lambda i: (i, 0))],
        out_shape=[jax.ShapeDtypeStruct((T, D_MODEL), F32),
                   jax.ShapeDtypeStruct((T, D_PROJ), BF16)],
        scratch_shapes=_COMBINE_SCRATCH + [pltpu.VMEM((D_MODEL, D_PROJ), BF16)],
        compiler_params=pltpu.CompilerParams(
            dimension_semantics=("arbitrary",), vmem_limit_bytes=VMEM_LIMIT),
        name="combine_norm_proj",
    )(dest_tiles, dest_tiles, x2, route, yb, gain, w_in)


def _combine_final_norm_kernel(dcur_ref, dnxt_ref, x_ref, route_ref, yb_ref, gain_ref,
                               o_ref, buf_ref, sem_ref):
    _gather_start(dnxt_ref, yb_ref, buf_ref, sem_ref, 1 - lax.rem(pl.program_id(0), 2))
    x = _combined_residual(dcur_ref, x_ref, route_ref, yb_ref, buf_ref, sem_ref)
    o_ref[...] = _rms(x, gain_ref[...])
    _drain_last_prefetch(yb_ref, buf_ref, sem_ref)


def _combine_final_norm(dest_tiles, x2, route, yb, gain):
    T = x2.shape[0]
    n_tiles = T // TM
    return pl.pallas_call(
        _combine_final_norm_kernel,
        grid=(n_tiles,),
        in_specs=_combine_specs(n_tiles) + [pl.BlockSpec((1, D_MODEL), lambda i: (0, 0))],
        out_specs=pl.BlockSpec((TM, D_MODEL), lambda i: (i, 0)),
        out_shape=jax.ShapeDtypeStruct((T, D_MODEL), F32),
        scratch_shapes=_COMBINE_SCRATCH,
        compiler_params=pltpu.CompilerParams(
            dimension_semantics=("arbitrary",), vmem_limit_bytes=VMEM_LIMIT),
        name="combine_final_norm",
    )(dest_tiles, dest_tiles, x2, route, yb, gain)


def _gelu_tanh(x):
    c = 0.7978845608028654
    return x * (0.5 * (1.0 + jnp.tanh(c * (x + 0.044715 * (x * x * x)))))


def _mixer_kernel(proj_ref, wgk_ref, bgk_ref, glan_ref, gmn_ref, wsp_ref, bsp_ref, wconv_ref,
                  out_ref, st_ref, hc_ref, lcat_ref, wm_ref, seq_start=None, first_step=None,
                  interleave=()):
    TS = TS_MIX
    n_gla = TS // GLA_CHUNK
    n_gm = TS // GMLP_CHUNK
    if seq_start is None:
        seq_start = first_step = pl.program_id(1) == 0

    @pl.when(seq_start)
    def _():
        st_ref[...] = jnp.zeros_like(st_ref)
        hc_ref[...] = jnp.zeros_like(hc_ref)

    @pl.when(first_step)
    def _():
        r = lax.broadcasted_iota(jnp.int32, (TS, TS), 0)
        c = lax.broadcasted_iota(jnp.int32, (TS, TS), 1)
        keep = ((r // GLA_CHUNK) == (c // GLA_CHUNK)) & (c <= r)
        lcat_ref[...] = jnp.where(keep, 1.0, 0.0).astype(BF16)
        t = lax.broadcasted_iota(jnp.int32, (GMLP_CHUNK, GMLP_HEADS * GMLP_CHUNK), 0)
        s = lax.broadcasted_iota(jnp.int32, (GMLP_CHUNK, GMLP_HEADS * GMLP_CHUNK), 1) % GMLP_CHUNK
        wm_ref[...] = jnp.where(s <= t, wsp_ref[...], 0.0).astype(BF16)

    pending = list(interleave)

    def tick(n=1):
        for _ in range(n):
            if pending:
                pending.pop(0)()

    lane256 = lax.broadcasted_iota(jnp.int32, (1, GLA_KDIM), 1)
    tick()

    q = proj_ref[:, C_Q:C_Q + GLA_KDIM].astype(F32)
    k = proj_ref[:, C_K:C_K + GLA_KDIM].astype(F32)
    v_b = proj_ref[:, C_V:C_V + GLA_WIDTH]
    z = _dot(proj_ref[:, C_GKL:C_GKL + LANES], wgk_ref[...]) + bgk_ref[...]
    gk = (jnp.minimum(z, 0.0) - jnp.log1p(jnp.exp(-jnp.abs(z)))) * (1.0 / GLA_GATE_NORM)
    gk_hi, gk_lo = _split_bf16(gk)
    cs = _dot(lcat_ref[...], jnp.concatenate([gk_hi, gk_lo], axis=1))
    tick()
    b = cs[:, :GLA_KDIM] + cs[:, GLA_KDIM:]
    b_last = [b[(c + 1) * GLA_CHUNK - 1:(c + 1) * GLA_CHUNK, :] for c in range(n_gla)]
    bl = jnp.concatenate(
        [jnp.broadcast_to(t, (GLA_CHUNK, GLA_KDIM)) for t in b_last], axis=0)
    tick()
    q_dec = (q * (GLA_DK ** -0.5)) * jnp.exp(b)
    k_inv = (k * jnp.exp(-b)).astype(BF16)
    k_dec = (k * jnp.exp(bl - b)).astype(BF16)
    q_dec_b = q_dec.astype(BF16)

    zero_b = jnp.zeros_like(q_dec_b)
    q_stack = jnp.concatenate(
        [jnp.where((lane256 // GLA_DK) == h, q_dec_b, zero_b) for h in range(GLA_HEADS)], axis=0)
    tick()
    scores = lax.dot_general(q_stack, k_inv, (((1,), (1,)), ((), ())),
                             preferred_element_type=F32)
    rt = lax.broadcasted_iota(jnp.int32, (TS, TS), 0)
    ct = lax.broadcasted_iota(jnp.int32, (TS, TS), 1)
    causal = ((rt // GLA_CHUNK) == (ct // GLA_CHUNK)) & (ct <= rt)
    o_heads = []
    for h in range(GLA_HEADS):
        p_h = jnp.where(causal, scores[h * TS:(h + 1) * TS, :], 0.0).astype(BF16)
        o_heads.append(_dot(p_h, v_b[:, h * GLA_DV:(h + 1) * GLA_DV]))
        tick()

    sr = lax.broadcasted_iota(jnp.int32, (GLA_WIDTH, GLA_KDIM), 0) // GLA_DV
    sc = lax.broadcasted_iota(jnp.int32, (GLA_WIDTH, GLA_KDIM), 1) // GLA_DK
    bd_mask = sr == sc
    o_inter = []
    for c in range(n_gla):
        rows = slice(c * GLA_CHUNK, (c + 1) * GLA_CHUNK)
        st = st_ref[...]
        o_inter.append(lax.dot_general(q_dec_b[rows], st.astype(BF16), (((1,), (1,)), ((), ())),
                                       preferred_element_type=F32))
        upd = lax.dot_general(v_b[rows], k_dec[rows], (((0,), (0,)), ((), ())),
                              preferred_element_type=F32)
        decay = jnp.exp(b_last[c])
        st_ref[...] = st * decay + jnp.where(bd_mask, upd, 0.0)
        tick()
    o_inter = jnp.concatenate(o_inter, axis=0)

    for h in range(GLA_HEADS):
        cols = slice(h * GLA_DV, (h + 1) * GLA_DV)
        o = o_heads[h] + o_inter[:, cols]
        o = o * lax.rsqrt(jnp.mean(o * o, axis=-1, keepdims=True) + RMS_EPS) * glan_ref[...]
        g = proj_ref[:, C_G + h * GLA_DV:C_G + (h + 1) * GLA_DV].astype(F32)
        out_ref[:, cols] = (o * (g * (1.0 / (1.0 + jnp.exp(-g))))).astype(out_ref.dtype)
        tick()

    u = _gelu_tanh(proj_ref[:, C_U:C_U + GMLP_WIDTH].astype(F32))
    vg = _gelu_tanh(proj_ref[:, C_VG:C_VG + GMLP_WIDTH].astype(F32))
    hr = lax.broadcasted_iota(jnp.int32, (GMLP_WIDTH, GMLP_WIDTH), 0) // GMLP_DH
    hcn = lax.broadcasted_iota(jnp.int32, (GMLP_WIDTH, GMLP_WIDTH), 1) // GMLP_DH
    head_mean = jnp.where(hr == hcn, 1.0 / GMLP_DH, 0.0).astype(BF16)
    sq_hi, sq_lo = _split_bf16(vg * vg)
    ms = _dot(sq_hi, head_mean) + _dot(sq_lo, head_mean)
    v32 = vg * lax.rsqrt(ms + RMS_EPS) * gmn_ref[...]
    for c in range(n_gm):
        rows = slice(c * GMLP_CHUNK, (c + 1) * GMLP_CHUNK)
        vc = v32[rows].astype(BF16)
        zc = jnp.zeros_like(vc)
        rhs = jnp.concatenate(
            [jnp.where((lane256 // GMLP_DH) == h, vc, zc) for h in range(GMLP_HEADS)], axis=0)
        mixed = _dot(wm_ref[...], rhs) + bsp_ref[...]
        out_ref[rows, GLA_WIDTH:GLA_WIDTH + GMLP_WIDTH] = (u[rows] * mixed).astype(out_ref.dtype)
        tick()

    hcv = (proj_ref[:, C_CG:C_CG + CONV_WIDTH].astype(F32)
           * proj_ref[:, C_X:C_X + CONV_WIDTH].astype(F32))
    hc_ref[8:8 + TS, :] = hcv
    y = (wconv_ref[2:3, :] * hcv + wconv_ref[1:2, :] * hc_ref[7:7 + TS, :]
         + wconv_ref[0:1, :] * hc_ref[6:6 + TS, :])
    out_ref[:, GLA_WIDTH + GMLP_WIDTH:] = (
        proj_ref[:, C_BG:C_BG + CONV_WIDTH].astype(F32) * y).astype(out_ref.dtype)
    hc_ref[0:8, :] = hc_ref[TS:TS + 8, :]
    tick(len(pending))


def _mixers(proj, wgk, bgk, glan, gmn, wsp, bsp, wconv, batch, seq):
    n_seq = seq // TS_MIX
    full = lambda shape: pl.BlockSpec(shape, lambda b, i: (0,) * len(shape))
    return pl.pallas_call(
        _mixer_kernel,
        grid=(batch, n_seq),
        in_specs=[
            pl.BlockSpec((TS_MIX, D_PROJ), lambda b, i: (b * n_seq + i, 0)),
            full((LANES, GLA_KDIM)), full((1, GLA_KDIM)), full((1, GLA_DV)), full((1, GMLP_WIDTH)),
            full((GMLP_CHUNK, GMLP_HEADS * GMLP_CHUNK)), full((GMLP_CHUNK, GMLP_WIDTH)),
            full((8, CONV_WIDTH)),
        ],
        out_specs=pl.BlockSpec((TS_MIX, D_MODEL), lambda b, i: (b * n_seq + i, 0)),
        out_shape=jax.ShapeDtypeStruct((batch * seq, D_MODEL), BF16),
        scratch_shapes=[
            pltpu.VMEM((GLA_WIDTH, GLA_KDIM), F32),
            pltpu.VMEM((TS_MIX + 8, CONV_WIDTH), F32),
            pltpu.VMEM((TS_MIX, TS_MIX), BF16),
            pltpu.VMEM((GMLP_CHUNK, GMLP_HEADS * GMLP_CHUNK), BF16),
        ],
        compiler_params=pltpu.CompilerParams(
            dimension_semantics=("arbitrary", "arbitrary"), vmem_limit_bytes=VMEM_LIMIT),
        name="mixers",
    )(proj, wgk, bgk, glan, gmn, wsp, bsp, wconv)


_MIXER_SCRATCH = [
    pltpu.VMEM((GLA_WIDTH, GLA_KDIM), F32),
    pltpu.VMEM((TS_MIX + 8, CONV_WIDTH), F32),
    pltpu.VMEM((TS_MIX, TS_MIX), BF16),
    pltpu.VMEM((GMLP_CHUNK, GMLP_HEADS * GMLP_CHUNK), BF16),
]


def _proj_mixer_kernel(x_ref, gain_ref, wt_ref, wgk_ref, bgk_ref, glan_ref, gmn_ref, wsp_ref, bsp_ref,
                       wconv_ref, out_ref, st_ref, hc_ref, lcat_ref, wm_ref, wb_ref, pcur_ref,
                       pnext_ref, *, tiles_per_seq):
    s = pl.program_id(0)

    @pl.when(s == 0)
    def _():
        _stage_w_in(wt_ref, wb_ref)
        pcur_ref[...] = jnp.zeros_like(pcur_ref)

    h = _rms(x_ref[...], gain_ref[...]).astype(BF16)

    def proj_chunk(c0):
        c1 = min(c0 + PROJ_CHUNK, D_PROJ)
        pnext_ref[:, c0:c1] = _dot(h, wb_ref[:, c0:c1]).astype(BF16)

    _mixer_kernel(pcur_ref, wgk_ref, bgk_ref, glan_ref, gmn_ref, wsp_ref, bsp_ref, wconv_ref,
                  out_ref, st_ref, hc_ref, lcat_ref, wm_ref,
                  seq_start=lax.rem(jnp.maximum(s - 1, 0), tiles_per_seq) == 0, first_step=s == 0,
                  interleave=[])
    for c0 in range(0, D_PROJ, PROJ_CHUNK):
        proj_chunk(c0)
    pcur_ref[...] = pnext_ref[...]


def _proj_mixers(x, gain, w_in_t, layer, wgk, bgk, glan, gmn, wsp, bsp, wconv, batch, seq):
    n_seq = seq // TS_MIX
    n = batch * n_seq
    full = lambda shape: pl.BlockSpec(shape, lambda s: (0,) * len(shape))
    return pl.pallas_call(
        functools.partial(_proj_mixer_kernel, tiles_per_seq=n_seq),
        grid=(n + 1,),
        in_specs=[
            pl.BlockSpec((TS_MIX, D_MODEL), lambda s: (jnp.minimum(s, n - 1), 0)),
            full((1, D_MODEL)),
            pl.BlockSpec((1, D_IN, D_MODEL), lambda s: (layer, 0, 0), pipeline_mode=pl.Buffered(1)),
            full((LANES, GLA_KDIM)), full((1, GLA_KDIM)), full((1, GLA_DV)), full((1, GMLP_WIDTH)),
            full((GMLP_CHUNK, GMLP_HEADS * GMLP_CHUNK)), full((GMLP_CHUNK, GMLP_WIDTH)),
            full((8, CONV_WIDTH)),
        ],
        out_specs=pl.BlockSpec((TS_MIX, D_MODEL), lambda s: (jnp.maximum(s - 1, 0), 0)),
        out_shape=jax.ShapeDtypeStruct((batch * seq, D_MODEL), BF16),
        scratch_shapes=_MIXER_SCRATCH + [pltpu.VMEM((D_MODEL, D_PROJ), BF16),
                                         pltpu.VMEM((TS_MIX, D_PROJ), BF16),
                                         pltpu.VMEM((TS_MIX, D_PROJ), BF16)],
        compiler_params=pltpu.CompilerParams(
            dimension_semantics=("arbitrary",), vmem_limit_bytes=VMEM_LIMIT),
        name="proj_mixers",
    )(x, gain, w_in_t, wgk, bgk, glan, gmn, wsp, bsp, wconv)


def _row_tiles_store(tiles_ref, x):
    rows = x.shape[0]
    flat = tiles_ref.reshape(rows * ROW_TILES, LANES)
    for c in range(ROW_TILES):
        flat[pl.ds(c, rows, stride=ROW_TILES), :] = x[:, c * LANES:(c + 1) * LANES]


def _row_tiles_chunks(tiles_ref, rows):
    flat = tiles_ref.reshape(rows * ROW_TILES, LANES)
    return [flat[pl.ds(c, rows, stride=ROW_TILES), :] for c in range(ROW_TILES)]


def _out_router_kernel(mix_ref, x_ref, wo_ref, gain_ref, wrc_ref, br_ref,
                       x2_ref, h2_ref, route_ref, route_t_ref, cnt_ref, tri_ref, wob_ref, lg_ref):
    i = pl.program_id(0)

    @pl.when(i == 0)
    def _():
        cnt_ref[...] = jnp.zeros_like(cnt_ref)
        lg_ref[...] = jnp.zeros_like(lg_ref)
        r = lax.broadcasted_iota(jnp.int32, (TM, TM), 0)
        c = lax.broadcasted_iota(jnp.int32, (TM, TM), 1)
        tri_ref[...] = jnp.where(c < r, 1.0, 0.0).astype(BF16)
        for r0 in range(0, D_MODEL, W_PREP_ROWS):
            wob_ref[r0:r0 + W_PREP_ROWS, :] = wo_ref[0, r0:r0 + W_PREP_ROWS, :].astype(BF16)

    lg = lg_ref[...]
    x2 = x_ref[...] + _dot(mix_ref[...], wob_ref[...])
    x2_ref[...] = x2
    h = _rms(x2, gain_ref[...])
    h_hi, h_lo = _split_bf16(h)
    h2_ref[...] = h_hi
    hh_hl = _dot(h_hi, wrc_ref[...])
    lg_ref[...] = (hh_hl[:, :ROUTER_COLS] + hh_hl[:, ROUTER_COLS:]
                   + _dot(h_lo, wrc_ref[:, :ROUTER_COLS]) + br_ref[...])

    lane = lax.broadcasted_iota(jnp.int32, (TM, LANES), 1).astype(F32)
    neg = -jnp.inf
    is_g = lane < N_GROUPS
    gl = jnp.where(is_g, lg, neg)
    gmax = jnp.max(gl, axis=1, keepdims=True)
    g_top = jnp.min(jnp.where(gl == gmax, lane, float(LANES)), axis=1, keepdims=True)
    g_w = 1.0 / jnp.sum(jnp.where(is_g, jnp.exp(lg - gmax), 0.0), axis=1, keepdims=True)
    first = N_GROUPS + EXPERTS_PER_GROUP * g_top
    el = jnp.where((lane >= first) & (lane < first + EXPERTS_PER_GROUP), lg, neg)
    m1 = jnp.max(el, axis=1, keepdims=True)
    i1 = jnp.min(jnp.where(el == m1, lane, float(LANES)), axis=1, keepdims=True)
    el2 = jnp.where(lane == i1, neg, el)
    m2 = jnp.max(el2, axis=1, keepdims=True)
    i2 = jnp.min(jnp.where(el2 == m2, lane, float(LANES)), axis=1, keepdims=True)
    ratio = jnp.exp(m2 - m1)
    w1 = g_w / (1.0 + ratio)
    w2 = w1 * ratio

    oh1 = jnp.where(lane == i1, 1.0, 0.0)
    oh2 = jnp.where(lane == i2, 1.0, 0.0)
    oh = jnp.where(i > 0, oh1 + oh2, 0.0)
    before = _dot(tri_ref[...], oh.astype(BF16)) + cnt_ref[0:1, :]
    rank1 = jnp.sum(oh1 * before, axis=1, keepdims=True)
    rank2 = jnp.sum(oh2 * before, axis=1, keepdims=True)
    cnt_ref[...] = cnt_ref[...] + jnp.sum(oh, axis=0, keepdims=True)

    rec = jnp.zeros((TM, LANES), F32)
    for col, val in ((R_E, i1 - N_GROUPS), (R_E + 1, i2 - N_GROUPS), (R_RANK, rank1),
                     (R_RANK + 1, rank2), (R_W, w1), (R_W + 1, w2)):
        rec = jnp.where(lane == col, val, rec)
    route_ref[...] = rec
    route_t_ref[0] = rec.T[0:SUBLANES, :]


def _out_router(mixed, x, w_out, layer, gain, wr_cat, br):
    T = x.shape[0]
    n = T // TM
    row = lambda w: pl.BlockSpec((TM, w), lambda i: (jnp.minimum(i, n - 1), 0))
    lag = lambda i: jnp.maximum(i - 1, 0)
    full = lambda shape: pl.BlockSpec(shape, lambda i: (0, 0))
    wo_spec = pl.BlockSpec((1, D_MODEL, D_MODEL), lambda i: (layer, 0, 0),
                           pipeline_mode=pl.Buffered(1))
    return pl.pallas_call(
        _out_router_kernel,
        grid=(n + 1,),
        in_specs=[row(D_MODEL), row(D_MODEL), wo_spec, full((1, D_MODEL)),
                  full((D_MODEL, 2 * ROUTER_COLS)), full((1, ROUTER_COLS))],
        out_specs=[row(D_MODEL), row(D_MODEL),
                   pl.BlockSpec((TM, LANES), lambda i: (lag(i), 0)),
                   pl.BlockSpec((1, SUBLANES, TM), lambda i: (lag(i), 0, 0)), full((8, LANES))],
        out_shape=[jax.ShapeDtypeStruct((T, D_MODEL), F32),
                   jax.ShapeDtypeStruct((T, D_MODEL), BF16),
                   jax.ShapeDtypeStruct((T, LANES), F32),
                   jax.ShapeDtypeStruct((T // TM, SUBLANES, TM), F32),
                   jax.ShapeDtypeStruct((8, LANES), F32)],
        scratch_shapes=[pltpu.VMEM((TM, TM), BF16), pltpu.VMEM((D_MODEL, D_MODEL), BF16),
                        pltpu.VMEM((TM, ROUTER_COLS), F32)],
        compiler_params=pltpu.CompilerParams(
            dimension_semantics=("arbitrary",), vmem_limit_bytes=VMEM_LIMIT),
        name="out_router",
    )(mixed, x, w_out, gain, wr_cat, br)


def _dispatch_kernel(fill_ref, nu_ref, dest_ref, h_ref, xb_ref, zero_ref, sem_ref, zsem_ref,
                     stage_ref):
    i = pl.program_id(0)
    par = lax.rem(i, 2)

    @pl.when(i == 0)
    def _():
        zero_ref[...] = jnp.zeros_like(zero_ref)
        fills = [(fill_ref[e] >= 0, pltpu.make_async_copy(
            zero_ref, xb_ref.at[pl.ds(pl.multiple_of(jnp.maximum(fill_ref[e], 0), MOE_BLK), MOE_BLK)],
            zsem_ref)) for e in range(N_EXPERTS)]
        n_blocks = xb_ref.shape[0] // MOE_BLK
        fills += [(j >= nu_ref[0], pltpu.make_async_copy(
            zero_ref, xb_ref.at[pl.ds(j * MOE_BLK, MOE_BLK)], zsem_ref))
            for j in range(n_blocks - N_EXPERTS, n_blocks)]
        for cond, f in fills:
            pl.when(cond)(f.start)
        for cond, f in fills:
            pl.when(cond)(f.wait)

    _row_tiles_store(stage_ref.at[par], h_ref[...].astype(F32))
    for r in range(TM):
        for k in range(TOP_K):
            pltpu.make_async_copy(stage_ref.at[par, r], xb_ref.at[dest_ref[0, 0, k * TM + r]],
                                  sem_ref.at[par]).start(priority=k)

    def wait_tile(p):
        for _ in range(TOP_K):
            pltpu.make_async_copy(stage_ref.at[p], xb_ref.at[pl.ds(0, TM)], sem_ref.at[p]).wait()

    pl.when(i > 0)(lambda: wait_tile(1 - par))
    pl.when(i == pl.num_programs(0) - 1)(lambda: wait_tile(par))


def _dispatch(fill_start, n_used, dest_tiles, h2, n_rows):
    T = h2.shape[0]
    grid_spec = pltpu.PrefetchScalarGridSpec(
        num_scalar_prefetch=2,
        grid=(T // TM,),
        in_specs=[
            pl.BlockSpec((1, 1, TOP_K * TM), lambda i, fs, nu: (i, 0, 0), memory_space=pltpu.SMEM),
            pl.BlockSpec((TM, D_MODEL), lambda i, fs, nu: (i, 0)),
        ],
        out_specs=pl.BlockSpec(memory_space=pl.ANY),
        scratch_shapes=[pltpu.VMEM((MOE_BLK, ROW_TILES, LANES), F32),
                        pltpu.SemaphoreType.DMA((2,)), pltpu.SemaphoreType.DMA(()),
                        pltpu.VMEM((2, TM, ROW_TILES, LANES), F32)],
    )
    return pl.pallas_call(
        _dispatch_kernel,
        grid_spec=grid_spec,
        out_shape=jax.ShapeDtypeStruct((n_rows, ROW_TILES, LANES), F32),
        compiler_params=pltpu.CompilerParams(dimension_semantics=("arbitrary",)),
        name="dispatch",
    )(fill_start, n_used, dest_tiles, h2)


BLOCK_COPY_PARTS = 4
X_SLOTS = 4
Y_SLOTS = 3


class _CopyGroup:
    def __init__(self, copies):
        self.copies = copies

    def start(self):
        for n, c in enumerate(self.copies):
            c.start(priority=n % 2)

    def wait(self):
        for c in self.copies:
            c.wait()


def _expert_kernel(be_ref, nxt_ref, nu_ref, xb_ref, wg_ref, wu_ref, wd_ref, yb_ref,
                   xbuf_ref, ybuf_ref, wgs_ref, wus_ref, wds_ref, wgb_ref, wub_ref, wdb_ref,
                   xsem_ref, ysem_ref, wsem_ref, *, layer, n_blocks):
    n_used = nu_ref[0]

    part = MOE_BLK // BLOCK_COPY_PARTS

    def x_copy(j, slot):
        return _CopyGroup([pltpu.make_async_copy(
            xb_ref.at[pl.ds(j * MOE_BLK + p * part, part)],
            xbuf_ref.at[slot, pl.ds(p * part, part)], xsem_ref.at[slot])
            for p in range(BLOCK_COPY_PARTS)])

    def y_copy(j, slot):
        return _CopyGroup([pltpu.make_async_copy(
            ybuf_ref.at[slot, pl.ds(p * part, part)],
            yb_ref.at[pl.ds(j * MOE_BLK + p * part, part)], ysem_ref.at[slot])
            for p in range(BLOCK_COPY_PARTS)])

    def w_copies(e, ws):
        return [pltpu.make_async_copy(src.at[layer, e], dst.at[ws], wsem_ref.at[ws])
                for src, dst in ((wg_ref, wgs_ref), (wu_ref, wus_ref), (wd_ref, wds_ref))]

    for j0 in range(X_SLOTS - 1):
        x_copy(j0, j0).start()
    for c in w_copies(be_ref[0], 0):
        c.start()

    def block(j, ws):
        slot = lax.rem(j, Y_SLOTS)
        xslot = lax.rem(j, X_SLOTS)
        first = (j == 0) | (be_ref[j] != be_ref[jnp.maximum(j - 1, 0)])
        ws = jnp.where(first & (j > 0), 1 - ws, ws)

        @pl.when(first)
        def _():
            for c in w_copies(be_ref[j], ws):
                c.wait()
            for r0 in range(0, D_MODEL, W_PREP_ROWS):
                rows = slice(r0, r0 + W_PREP_ROWS)
                wgb_ref[rows, :] = wgs_ref[ws, rows, :].astype(BF16)
                wub_ref[rows, :] = wus_ref[ws, rows, :].astype(BF16)
            for r0 in range(0, D_EXPERT, W_PREP_ROWS):
                rows = slice(r0, r0 + W_PREP_ROWS)
                wdb_ref[rows, :] = wds_ref[ws, rows, :].astype(BF16)

            @pl.when(nxt_ref[j] >= 0)
            def _():
                for c in w_copies(nxt_ref[j], 1 - ws):
                    c.start()

        ahead = j + X_SLOTS - 1

        @pl.when(ahead < n_used)
        def _():
            x_copy(ahead, lax.rem(ahead, X_SLOTS)).start()

        x_copy(j, xslot).wait()
        x = jnp.concatenate(
            [c.astype(BF16) for c in _row_tiles_chunks(xbuf_ref.at[xslot], MOE_BLK)], axis=1)
        g = _dot(x, wgb_ref[...])
        u = _dot(x, wub_ref[...])
        h = (g * (1.0 / (1.0 + jnp.exp(-g)))) * u
        y = _dot(h.astype(BF16), wdb_ref[...])

        @pl.when(j >= Y_SLOTS)
        def _():
            y_copy(j - Y_SLOTS, slot).wait()

        _row_tiles_store(ybuf_ref.at[slot], y)
        y_copy(j, slot).start()
        return ws

    lax.fori_loop(0, n_used, block, jnp.int32(0))

    for back in range(Y_SLOTS, 0, -1):
        y_copy(n_used - back, lax.rem(n_used - back, Y_SLOTS)).wait()

    ybuf_ref[0] = jnp.zeros((MOE_BLK, ROW_TILES, LANES), F32)

    def fill(j, carry):
        y_copy(j, 0).start()
        return carry

    def fill_wait(j, carry):
        y_copy(j, 0).wait()
        return carry

    lax.fori_loop(n_used, n_blocks, fill, 0)
    lax.fori_loop(n_used, n_blocks, fill_wait, 0)


def _experts(blk_exp, nxt_exp, n_used, xb, w_gate, w_up, w_down, layer):
    n_blocks = blk_exp.shape[0]
    any_spec = pl.BlockSpec(memory_space=pl.ANY)
    blk = (MOE_BLK, ROW_TILES, LANES)
    grid_spec = pltpu.PrefetchScalarGridSpec(
        num_scalar_prefetch=3,
        grid=(1,),
        in_specs=[any_spec, any_spec, any_spec, any_spec],
        out_specs=any_spec,
        scratch_shapes=[
            pltpu.VMEM((X_SLOTS,) + blk, F32), pltpu.VMEM((Y_SLOTS,) + blk, F32),
            pltpu.VMEM((2, D_MODEL, D_EXPERT), F32), pltpu.VMEM((2, D_MODEL, D_EXPERT), F32),
            pltpu.VMEM((2, D_EXPERT, D_MODEL), F32),
            pltpu.VMEM((D_MODEL, D_EXPERT), BF16), pltpu.VMEM((D_MODEL, D_EXPERT), BF16),
            pltpu.VMEM((D_EXPERT, D_MODEL), BF16),
            pltpu.SemaphoreType.DMA((X_SLOTS,)), pltpu.SemaphoreType.DMA((Y_SLOTS,)),
            pltpu.SemaphoreType.DMA((2,)),
        ],
    )
    return pl.pallas_call(
        functools.partial(_expert_kernel, layer=layer, n_blocks=n_blocks),
        grid_spec=grid_spec,
        out_shape=jax.ShapeDtypeStruct((n_blocks * MOE_BLK, ROW_TILES, LANES), F32),
        compiler_params=pltpu.CompilerParams(
            dimension_semantics=("arbitrary",), vmem_limit_bytes=VMEM_LIMIT),
        name="experts",
    )(blk_exp, nxt_exp, n_used, xb, w_gate, w_up, w_down)


def _dispatch_tables(route_t, counts_rec, T):
    counts = counts_rec[0, N_GROUPS:N_GROUPS + N_EXPERTS].astype(jnp.int32)
    n_steps = (T * TOP_K) // MOE_BLK + N_EXPERTS
    nblk = (counts + MOE_BLK - 1) // MOE_BLK
    bend = jnp.cumsum(nblk)
    pstart = (bend - nblk) * MOE_BLK
    n_used = bend[-1]
    j = jnp.minimum(jnp.arange(n_steps, dtype=jnp.int32), n_used - 1)
    blk_exp = jnp.minimum(jnp.sum(j[:, None] >= bend[None, :], axis=1), N_EXPERTS - 1)
    n_rows = n_steps * MOE_BLK
    last_blk = jnp.where(counts > 0, (bend - 1) * MOE_BLK, -1)
    ids = jnp.arange(N_EXPERTS, dtype=jnp.int32)
    later = (ids[None, :] > ids[:, None]) & (nblk[None, :] > 0)
    nxt_of = jnp.min(jnp.where(later, ids[None, :], N_EXPERTS), axis=1)
    nxt_tab = jnp.where(nxt_of < N_EXPERTS, nxt_of, -1)
    nxt_exp = jnp.sum(jnp.where(blk_exp[:, None] == ids[None, :], nxt_tab[None, :], 0), axis=1)
    e = route_t[:, R_E:R_E + TOP_K, :].astype(jnp.int32)
    rank = route_t[:, R_RANK:R_RANK + TOP_K, :].astype(jnp.int32)
    seg = jnp.sum(jnp.where(e[..., None] == jnp.arange(N_EXPERTS), pstart, 0), axis=-1)
    dest = jnp.clip(seg + rank, 0, n_steps * MOE_BLK - 1)
    dest_tiles = dest.reshape(T // TM, 1, TOP_K * TM)
    return dict(dest_tiles=dest_tiles, fill_start=last_blk.astype(jnp.int32),
                blk_exp=blk_exp.astype(jnp.int32), nxt_exp=nxt_exp.astype(jnp.int32),
                n_used=n_used.reshape(1).astype(jnp.int32), n_rows=n_rows)


def _prep_layer(l, w_gk_up, b_gk, gla_norm, gmlp_norm, w_spatial, b_spatial, w_conv,
                w_router_group, b_router_group, w_router_expert, b_router_expert):
    wgk = jnp.concatenate(
        [w_gk_up[l], jnp.zeros((LANES - GLA_GATE_RANK, GLA_KDIM), F32)], axis=0).astype(BF16)
    wsp = w_spatial[l].transpose(1, 0, 2).reshape(GMLP_CHUNK, GMLP_HEADS * GMLP_CHUNK)
    bsp = jnp.repeat(b_spatial[l].T, GMLP_DH, axis=1)
    wconv = jnp.concatenate([w_conv[l], jnp.zeros((8 - CONV_K, CONV_WIDTH), F32)], axis=0)
    wr = jnp.concatenate(
        [w_router_group[l], w_router_expert[l],
         jnp.zeros((D_MODEL, ROUTER_COLS - N_GROUPS - N_EXPERTS), F32)], axis=1)
    wr_hi = wr.astype(BF16)
    wr_lo = (wr - wr_hi.astype(F32)).astype(BF16)
    br = jnp.concatenate(
        [b_router_group[l], b_router_expert[l],
         jnp.zeros((ROUTER_COLS - N_GROUPS - N_EXPERTS,), F32)])[None, :]
    return dict(
        wgk=wgk, bgk=b_gk[l][None, :], glan=gla_norm[l][None, :], gmn=gmlp_norm[l][None, :],
        wsp=wsp, bsp=bsp, wconv=wconv, wr_cat=jnp.concatenate([wr_hi, wr_lo], axis=1), br=br)


def kernel(x, attn_norm, w_in, w_gk_up, b_gk, gla_norm, gmlp_norm, w_spatial, b_spatial, w_conv, w_out, ffn_norm, w_router_group, b_router_group, w_router_expert, b_router_expert, w_gate, w_up, w_down, final_norm):
    B, S, D = x.shape
    T = B * S
    depth = w_in.shape[0]
    xr = x.reshape(T, D)
    w_in_t = jnp.swapaxes(w_in, 1, 2)
    moe = None
    for l in range(depth):
        p = _prep_layer(l, w_gk_up, b_gk, gla_norm, gmlp_norm, w_spatial, b_spatial, w_conv,
                        w_router_group, b_router_group, w_router_expert, b_router_expert)
        mix_params = (p["wgk"], p["bgk"], p["glan"], p["gmn"], p["wsp"], p["bsp"], p["wconv"])
        if moe is None:
            mixed = _proj_mixers(xr, attn_norm[l][None, :], w_in_t, l, *mix_params, B, S)
        else:
            xr, proj = _combine_norm_proj(moe["dest_tiles"], moe["x2"], moe["route"], moe["yb"],
                                          attn_norm[l][None, :], w_in_t, l)
            mixed = _mixers(proj, *mix_params, B, S)
        x2, h2, route, route_t, counts_rec = _out_router(
            mixed, xr, w_out, l, ffn_norm[l][None, :], p["wr_cat"], p["br"])
        moe = _dispatch_tables(route_t, counts_rec, T)
        xb = _dispatch(moe["fill_start"], moe["n_used"], moe["dest_tiles"], h2, moe["n_rows"])
        yb = _experts(moe["blk_exp"], moe["nxt_exp"], moe["n_used"], xb, w_gate, w_up, w_down, l)
        moe.update(x2=x2, route=route, yb=yb)
    out = _combine_final_norm(moe["dest_tiles"], moe["x2"], moe["route"], moe["yb"],
                              final_norm[None, :])
    return out.reshape(B, S, D)
```

```python
import functools

import jax
import jax.numpy as jnp
from jax import lax
from jax.experimental import pallas as pl
from jax.experimental.pallas import tpu as pltpu

F32 = jnp.float32
BF16 = jnp.bfloat16

D_MODEL = 1024
RMS_EPS = 1e-6
GLA_HEADS = 4
GLA_WIDTH = 512
GLA_DV = 128
GLA_DK = 64
GLA_KDIM = 256
GLA_GATE_RANK = 16
GLA_GATE_NORM = 16.0
GLA_CHUNK = 64
GMLP_HEADS = 4
GMLP_WIDTH = 256
GMLP_DH = 64
GMLP_CHUNK = 128
CONV_WIDTH = 256
CONV_K = 3
N_GROUPS = 4
EXPERTS_PER_GROUP = 8
N_EXPERTS = 32
TOP_K = 2
D_EXPERT = 256

LANES = 128
C_Q, C_K, C_V, C_G = 0, 256, 512, 1024
C_U, C_VG, C_X, C_BG, C_CG, C_GKL = 1536, 1792, 2048, 2304, 2560, 2816
D_PROJ = C_GKL + LANES
D_IN = C_GKL + GLA_GATE_RANK

TM = 256
TS_MIX = TM
MOE_BLK = 256
ROUTER_COLS = LANES
SUBLANES = 8
ROW_TILES = D_MODEL // LANES
assert ROW_TILES == SUBLANES
VMEM_LIMIT = 56 * 1024 * 1024
R_E, R_RANK, R_W = 0, 2, 4


def _dot(a, b):
    return jnp.dot(a, b, preferred_element_type=F32)


def _split_bf16(x):
    hi = x.astype(BF16)
    lo = (x - hi.astype(F32)).astype(BF16)
    return hi, lo


def _rms(x, gain):
    return x * lax.rsqrt(jnp.mean(x * x, axis=-1, keepdims=True) + RMS_EPS) * gain


W_PREP_ROWS = 128
PROJ_CHUNK = 256


def _stage_w_in(wt_ref, wb_ref):
    for c0 in range(0, C_GKL, LANES):
        src = c0 if c0 < C_U else c0 + GLA_GATE_RANK
        wb_ref[:, c0:c0 + LANES] = wt_ref[0, src:src + LANES, :].T.astype(BF16)
    low = jnp.concatenate([wt_ref[0, C_U:C_U + GLA_GATE_RANK, :],
                           jnp.zeros((LANES - GLA_GATE_RANK, D_MODEL), F32)], axis=0)
    wb_ref[:, C_GKL:D_PROJ] = low.T.astype(BF16)


def _row_gather_copy(yb_ref, buf_ref, sem_ref, slot, k, r, d):
    return pltpu.make_async_copy(yb_ref.at[d], buf_ref.at[slot, k, r], sem_ref.at[slot])


def _gather_start(dest_ref, yb_ref, buf_ref, sem_ref, slot, rows=range(TM)):
    for r in rows:
        for k in range(TOP_K):
            _row_gather_copy(yb_ref, buf_ref, sem_ref, slot, k, r,
                             dest_ref[0, 0, k * TM + r]).start(priority=k)


def _gather_wait(yb_ref, buf_ref, sem_ref, slot):
    for k in range(TOP_K):
        pltpu.make_async_copy(yb_ref.at[pl.ds(0, TM)], buf_ref.at[slot, k], sem_ref.at[slot]).wait()


def _combined_residual(dcur_ref, x_ref, route_ref, yb_ref, buf_ref, sem_ref):
    i = pl.program_id(0)
    slot = lax.rem(i, 2)

    @pl.when(i == 0)
    def _():
        _gather_start(dcur_ref, yb_ref, buf_ref, sem_ref, 0)

    _gather_wait(yb_ref, buf_ref, sem_ref, slot)
    w0 = route_ref[:, R_W:R_W + 1]
    w1 = route_ref[:, R_W + 1:R_W + 2]
    y0 = _row_tiles_chunks(buf_ref.at[slot, 0], TM)
    y1 = _row_tiles_chunks(buf_ref.at[slot, 1], TM)
    return jnp.concatenate(
        [x_ref[:, c * LANES:(c + 1) * LANES] + (w0 * y0[c] + w1 * y1[c]) for c in range(ROW_TILES)],
        axis=1)


def _prefetch_groups(n_groups):
    per = -(-TM // n_groups)
    return [range(g * per, min(TM, (g + 1) * per)) for g in range(n_groups)]


def _drain_last_prefetch(yb_ref, buf_ref, sem_ref):
    i = pl.program_id(0)

    @pl.when(i == pl.num_programs(0) - 1)
    def _():
        _gather_wait(yb_ref, buf_ref, sem_ref, 1 - lax.rem(i, 2))


def _combine_specs(n_tiles):
    smem_tile = lambda f: pl.BlockSpec((1, 1, TOP_K * TM), f, memory_space=pltpu.SMEM)
    return [
        smem_tile(lambda i: (i, 0, 0)),
        smem_tile(lambda i: (jnp.minimum(i + 1, n_tiles - 1), 0, 0)),
        pl.BlockSpec((TM, D_MODEL), lambda i: (i, 0)),
        pl.BlockSpec((TM, LANES), lambda i: (i, 0)),
        pl.BlockSpec(memory_space=pl.ANY),
    ]


_COMBINE_SCRATCH = [pltpu.VMEM((2, TOP_K, TM, ROW_TILES, LANES), F32),
                    pltpu.SemaphoreType.DMA((2,))]


def _combine_final_norm_kernel(dcur_ref, dnxt_ref, x_ref, route_ref, yb_ref, gain_ref,
                               o_ref, buf_ref, sem_ref):
    _gather_start(dnxt_ref, yb_ref, buf_ref, sem_ref, 1 - lax.rem(pl.program_id(0), 2))
    x = _combined_residual(dcur_ref, x_ref, route_ref, yb_ref, buf_ref, sem_ref)
    o_ref[...] = _rms(x, gain_ref[...])
    _drain_last_prefetch(yb_ref, buf_ref, sem_ref)


def _combine_final_norm(dest_tiles, x2, route, yb, gain):
    T = x2.shape[0]
    n_tiles = T // TM
    return pl.pallas_call(
        _combine_final_norm_kernel,
        grid=(n_tiles,),
        in_specs=_combine_specs(n_tiles) + [pl.BlockSpec((1, D_MODEL), lambda i: (0, 0))],
        out_specs=pl.BlockSpec((TM, D_MODEL), lambda i: (i, 0)),
        out_shape=jax.ShapeDtypeStruct((T, D_MODEL), F32),
        scratch_shapes=_COMBINE_SCRATCH,
        compiler_params=pltpu.CompilerParams(
            dimension_semantics=("arbitrary",), vmem_limit_bytes=VMEM_LIMIT),
        name="combine_final_norm",
    )(dest_tiles, dest_tiles, x2, route, yb, gain)


def _gelu_tanh(x):
    c = 0.7978845608028654
    return x * (0.5 * (1.0 + jnp.tanh(c * (x + 0.044715 * (x * x * x)))))


def _mixer_kernel(proj_ref, wgk_ref, bgk_ref, glan_ref, gmn_ref, wsp_ref, bsp_ref, wconv_ref,
                  out_ref, st_ref, hc_ref, lcat_ref, wm_ref, *, seq_start, first_step):
    TS = TS_MIX
    n_gla = TS // GLA_CHUNK
    n_gm = TS // GMLP_CHUNK

    @pl.when(seq_start)
    def _():
        st_ref[...] = jnp.zeros_like(st_ref)
        hc_ref[...] = jnp.zeros_like(hc_ref)

    @pl.when(first_step)
    def _():
        r = lax.broadcasted_iota(jnp.int32, (TS, TS), 0)
        c = lax.broadcasted_iota(jnp.int32, (TS, TS), 1)
        keep = ((r // GLA_CHUNK) == (c // GLA_CHUNK)) & (c <= r)
        lcat_ref[...] = jnp.where(keep, 1.0, 0.0).astype(BF16)
        t = lax.broadcasted_iota(jnp.int32, (GMLP_CHUNK, GMLP_HEADS * GMLP_CHUNK), 0)
        s = lax.broadcasted_iota(jnp.int32, (GMLP_CHUNK, GMLP_HEADS * GMLP_CHUNK), 1) % GMLP_CHUNK
        wm_ref[...] = jnp.where(s <= t, wsp_ref[...], 0.0).astype(BF16)

    lane256 = lax.broadcasted_iota(jnp.int32, (1, GLA_KDIM), 1)

    q = proj_ref[:, C_Q:C_Q + GLA_KDIM].astype(F32)
    k = proj_ref[:, C_K:C_K + GLA_KDIM].astype(F32)
    v_b = proj_ref[:, C_V:C_V + GLA_WIDTH]
    z = _dot(proj_ref[:, C_GKL:C_GKL + LANES], wgk_ref[...]) + bgk_ref[...]
    gk = (jnp.minimum(z, 0.0) - jnp.log1p(jnp.exp(-jnp.abs(z)))) * (1.0 / GLA_GATE_NORM)
    gk_hi, gk_lo = _split_bf16(gk)
    cs = _dot(lcat_ref[...], jnp.concatenate([gk_hi, gk_lo], axis=1))
    b = cs[:, :GLA_KDIM] + cs[:, GLA_KDIM:]
    b_last = [b[(c + 1) * GLA_CHUNK - 1:(c + 1) * GLA_CHUNK, :] for c in range(n_gla)]
    bl = jnp.concatenate(
        [jnp.broadcast_to(t, (GLA_CHUNK, GLA_KDIM)) for t in b_last], axis=0)
    q_dec = (q * (GLA_DK ** -0.5)) * jnp.exp(b)
    k_inv = (k * jnp.exp(-b)).astype(BF16)
    k_dec = (k * jnp.exp(bl - b)).astype(BF16)
    q_dec_b = q_dec.astype(BF16)

    zero_b = jnp.zeros_like(q_dec_b)
    q_stack = jnp.concatenate(
        [jnp.where((lane256 // GLA_DK) == h, q_dec_b, zero_b) for h in range(GLA_HEADS)], axis=0)
    scores = lax.dot_general(q_stack, k_inv, (((1,), (1,)), ((), ())),
                             preferred_element_type=F32)
    rt = lax.broadcasted_iota(jnp.int32, (TS, TS), 0)
    ct = lax.broadcasted_iota(jnp.int32, (TS, TS), 1)
    causal = ((rt // GLA_CHUNK) == (ct // GLA_CHUNK)) & (ct <= rt)
    o_heads = []
    for h in range(GLA_HEADS):
        p_h = jnp.where(causal, scores[h * TS:(h + 1) * TS, :], 0.0).astype(BF16)
        o_heads.append(_dot(p_h, v_b[:, h * GLA_DV:(h + 1) * GLA_DV]))

    sr = lax.broadcasted_iota(jnp.int32, (GLA_WIDTH, GLA_KDIM), 0) // GLA_DV
    sc = lax.broadcasted_iota(jnp.int32, (GLA_WIDTH, GLA_KDIM), 1) // GLA_DK
    bd_mask = sr == sc
    o_inter = []
    for c in range(n_gla):
        rows = slice(c * GLA_CHUNK, (c + 1) * GLA_CHUNK)
        st = st_ref[...]
        o_inter.append(lax.dot_general(q_dec_b[rows], st.astype(BF16), (((1,), (1,)), ((), ())),
                                       preferred_element_type=F32))
        upd = lax.dot_general(v_b[rows], k_dec[rows], (((0,), (0,)), ((), ())),
                              preferred_element_type=F32)
        decay = jnp.exp(b_last[c])
        st_ref[...] = st * decay + jnp.where(bd_mask, upd, 0.0)
    o_inter = jnp.concatenate(o_inter, axis=0)

    for h in range(GLA_HEADS):
        cols = slice(h * GLA_DV, (h + 1) * GLA_DV)
        o = o_heads[h] + o_inter[:, cols]
        o = o * lax.rsqrt(jnp.mean(o * o, axis=-1, keepdims=True) + RMS_EPS) * glan_ref[...]
        g = proj_ref[:, C_G + h * GLA_DV:C_G + (h + 1) * GLA_DV].astype(F32)
        out_ref[:, cols] = (o * (g * (1.0 / (1.0 + jnp.exp(-g))))).astype(out_ref.dtype)

    u = _gelu_tanh(proj_ref[:, C_U:C_U + GMLP_WIDTH].astype(F32))
    vg = _gelu_tanh(proj_ref[:, C_VG:C_VG + GMLP_WIDTH].astype(F32))
    hr = lax.broadcasted_iota(jnp.int32, (GMLP_WIDTH, GMLP_WIDTH), 0) // GMLP_DH
    hcn = lax.broadcasted_iota(jnp.int32, (GMLP_WIDTH, GMLP_WIDTH), 1) // GMLP_DH
    head_mean = jnp.where(hr == hcn, 1.0 / GMLP_DH, 0.0).astype(BF16)
    sq_hi, sq_lo = _split_bf16(vg * vg)
    ms = _dot(sq_hi, head_mean) + _dot(sq_lo, head_mean)
    v32 = vg * lax.rsqrt(ms + RMS_EPS) * gmn_ref[...]
    for c in range(n_gm):
        rows = slice(c * GMLP_CHUNK, (c + 1) * GMLP_CHUNK)
        vc = v32[rows].astype(BF16)
        zc = jnp.zeros_like(vc)
        rhs = jnp.concatenate(
            [jnp.where((lane256 // GMLP_DH) == h, vc, zc) for h in range(GMLP_HEADS)], axis=0)
        mixed = _dot(wm_ref[...], rhs) + bsp_ref[...]
        out_ref[rows, GLA_WIDTH:GLA_WIDTH + GMLP_WIDTH] = (u[rows] * mixed).astype(out_ref.dtype)

    hcv = (proj_ref[:, C_CG:C_CG + CONV_WIDTH].astype(F32)
           * proj_ref[:, C_X:C_X + CONV_WIDTH].astype(F32))
    hc_ref[8:8 + TS, :] = hcv
    y = (wconv_ref[2:3, :] * hcv + wconv_ref[1:2, :] * hc_ref[7:7 + TS, :]
         + wconv_ref[0:1, :] * hc_ref[6:6 + TS, :])
    out_ref[:, GLA_WIDTH + GMLP_WIDTH:] = (
        proj_ref[:, C_BG:C_BG + CONV_WIDTH].astype(F32) * y).astype(out_ref.dtype)
    hc_ref[0:8, :] = hc_ref[TS:TS + 8, :]


_MIXER_SCRATCH = [
    pltpu.VMEM((GLA_WIDTH, GLA_KDIM), F32),
    pltpu.VMEM((TS_MIX + 8, CONV_WIDTH), F32),
    pltpu.VMEM((TS_MIX, TS_MIX), BF16),
    pltpu.VMEM((GMLP_CHUNK, GMLP_HEADS * GMLP_CHUNK), BF16),
]


N_MIX_PARAMS = 7


def _front_kernel(*refs, tiles_per_seq, combine):
    refs = list(refs)
    if combine:
        dcur_ref, dnxt_ref, x_ref, route_ref, yb_ref = refs[:5]
        del refs[:5]
    else:
        x_ref = refs.pop(0)
    gain_ref, wt_ref = refs[:2]
    mix_refs = refs[2:2 + N_MIX_PARAMS]
    del refs[:2 + N_MIX_PARAMS]
    if combine:
        xo_ref, out_ref, buf_ref, sem_ref = refs[:4]
        del refs[:4]
    else:
        out_ref = refs.pop(0)
    st_ref, hc_ref, lcat_ref, wm_ref, wb_ref, pcur_ref, pnext_ref = refs
    s = pl.program_id(0)

    @pl.when(s == 0)
    def _():
        _stage_w_in(wt_ref, wb_ref)
        pcur_ref[...] = jnp.zeros_like(pcur_ref)

    _mixer_kernel(pcur_ref, *mix_refs, out_ref, st_ref, hc_ref, lcat_ref, wm_ref,
                  seq_start=lax.rem(jnp.maximum(s - 1, 0), tiles_per_seq) == 0, first_step=s == 0)

    if combine:
        x = _combined_residual(dcur_ref, x_ref, route_ref, yb_ref, buf_ref, sem_ref)
        xo_ref[...] = x
    else:
        x = x_ref[...]
    h = _rms(x, gain_ref[...]).astype(BF16)
    col_chunks = [(c0, min(c0 + PROJ_CHUNK, D_PROJ)) for c0 in range(0, D_PROJ, PROJ_CHUNK)]
    for rows, (c0, c1) in zip(_prefetch_groups(len(col_chunks)), col_chunks):
        if combine:
            _gather_start(dnxt_ref, yb_ref, buf_ref, sem_ref, 1 - lax.rem(s, 2), rows)
        pnext_ref[:, c0:c1] = _dot(h, wb_ref[:, c0:c1]).astype(BF16)
    if combine:
        _drain_last_prefetch(yb_ref, buf_ref, sem_ref)
    pcur_ref[...] = pnext_ref[...]


def _front(x, gain, w_in_t, layer, mix_params, batch, seq, moe=None):
    n_seq = seq // TS_MIX
    n = batch * n_seq
    T = batch * seq
    cur = lambda s: jnp.minimum(s, n - 1)
    full = lambda shape: pl.BlockSpec(shape, lambda s: (0,) * len(shape))
    row = lambda w, f: pl.BlockSpec((TM, w), lambda s: (f(s), 0))
    in_specs, args = [row(D_MODEL, cur)], [x]
    out_specs = [row(D_MODEL, lambda s: jnp.maximum(s - 1, 0))]
    out_shape = [jax.ShapeDtypeStruct((T, D_MODEL), BF16)]
    scratch = list(_MIXER_SCRATCH)
    if moe is not None:
        smem_tile = lambda f: pl.BlockSpec((1, 1, TOP_K * TM), lambda s: (f(s), 0, 0),
                                           memory_space=pltpu.SMEM)
        in_specs = [smem_tile(cur), smem_tile(lambda s: jnp.minimum(s + 1, n - 1))] + in_specs + [
            row(LANES, cur), pl.BlockSpec(memory_space=pl.ANY)]
        args = [moe["dest_tiles"], moe["dest_tiles"]] + args + [moe["route"], moe["yb"]]
        out_specs = [row(D_MODEL, cur)] + out_specs
        out_shape = [jax.ShapeDtypeStruct((T, D_MODEL), F32)] + out_shape
        scratch = _COMBINE_SCRATCH + scratch
    in_specs += [
        full((1, D_MODEL)),
        pl.BlockSpec((1, D_IN, D_MODEL), lambda s: (layer, 0, 0), pipeline_mode=pl.Buffered(1)),
        full((LANES, GLA_KDIM)), full((1, GLA_KDIM)), full((1, GLA_DV)), full((1, GMLP_WIDTH)),
        full((GMLP_CHUNK, GMLP_HEADS * GMLP_CHUNK)), full((GMLP_CHUNK, GMLP_WIDTH)),
        full((8, CONV_WIDTH)),
    ]
    scratch += [pltpu.VMEM((D_MODEL, D_PROJ), BF16), pltpu.VMEM((TS_MIX, D_PROJ), BF16),
                pltpu.VMEM((TS_MIX, D_PROJ), BF16)]
    return pl.pallas_call(
        functools.partial(_front_kernel, tiles_per_seq=n_seq, combine=moe is not None),
        grid=(n + 1,),
        in_specs=in_specs,
        out_specs=out_specs if moe is not None else out_specs[0],
        out_shape=out_shape if moe is not None else out_shape[0],
        scratch_shapes=scratch,
        compiler_params=pltpu.CompilerParams(
            dimension_semantics=("arbitrary",), vmem_limit_bytes=VMEM_LIMIT),
        name="front",
    )(*args, gain, w_in_t, *mix_params)


def _row_tiles_store(tiles_ref, x):
    rows = x.shape[0]
    flat = tiles_ref.reshape(rows * ROW_TILES, LANES)
    for c in range(ROW_TILES):
        flat[pl.ds(c, rows, stride=ROW_TILES), :] = x[:, c * LANES:(c + 1) * LANES]


def _row_tiles_chunks(tiles_ref, rows):
    flat = tiles_ref.reshape(rows * ROW_TILES, LANES)
    return [flat[pl.ds(c, rows, stride=ROW_TILES), :] for c in range(ROW_TILES)]


def _out_router_kernel(mix_ref, x_ref, wo_ref, gain_ref, wrc_ref, br_ref,
                       x2_ref, h2_ref, route_ref, route_t_ref, cnt_ref, tri_ref, wob_ref, lg_ref):
    i = pl.program_id(0)

    @pl.when(i == 0)
    def _():
        cnt_ref[...] = jnp.zeros_like(cnt_ref)
        lg_ref[...] = jnp.zeros_like(lg_ref)
        r = lax.broadcasted_iota(jnp.int32, (TM, TM), 0)
        c = lax.broadcasted_iota(jnp.int32, (TM, TM), 1)
        tri_ref[...] = jnp.where(c < r, 1.0, 0.0).astype(BF16)
        for r0 in range(0, D_MODEL, W_PREP_ROWS):
            wob_ref[r0:r0 + W_PREP_ROWS, :] = wo_ref[0, r0:r0 + W_PREP_ROWS, :].astype(BF16)

    lg = lg_ref[...]
    x2 = x_ref[...] + _dot(mix_ref[...], wob_ref[...])
    x2_ref[...] = x2
    h = _rms(x2, gain_ref[...])
    h_hi, h_lo = _split_bf16(h)
    h2_ref[...] = h_hi
    hh_hl = _dot(h_hi, wrc_ref[...])
    lg_ref[...] = (hh_hl[:, :ROUTER_COLS] + hh_hl[:, ROUTER_COLS:]
                   + _dot(h_lo, wrc_ref[:, :ROUTER_COLS]) + br_ref[...])

    lane = lax.broadcasted_iota(jnp.int32, (TM, LANES), 1).astype(F32)
    neg = -jnp.inf
    is_g = lane < N_GROUPS
    gl = jnp.where(is_g, lg, neg)
    gmax = jnp.max(gl, axis=1, keepdims=True)
    g_top = jnp.min(jnp.where(gl == gmax, lane, float(LANES)), axis=1, keepdims=True)
    g_w = 1.0 / jnp.sum(jnp.where(is_g, jnp.exp(lg - gmax), 0.0), axis=1, keepdims=True)
    first = N_GROUPS + EXPERTS_PER_GROUP * g_top
    el = jnp.where((lane >= first) & (lane < first + EXPERTS_PER_GROUP), lg, neg)
    m1 = jnp.max(el, axis=1, keepdims=True)
    i1 = jnp.min(jnp.where(el == m1, lane, float(LANES)), axis=1, keepdims=True)
    el2 = jnp.where(lane == i1, neg, el)
    m2 = jnp.max(el2, axis=1, keepdims=True)
    i2 = jnp.min(jnp.where(el2 == m2, lane, float(LANES)), axis=1, keepdims=True)
    ratio = jnp.exp(m2 - m1)
    w1 = g_w / (1.0 + ratio)
    w2 = w1 * ratio

    oh1 = jnp.where(lane == i1, 1.0, 0.0)
    oh2 = jnp.where(lane == i2, 1.0, 0.0)
    oh = jnp.where(i > 0, oh1 + oh2, 0.0)
    before = _dot(tri_ref[...], oh.astype(BF16)) + cnt_ref[0:1, :]
    rank1 = jnp.sum(oh1 * before, axis=1, keepdims=True)
    rank2 = jnp.sum(oh2 * before, axis=1, keepdims=True)
    cnt_ref[...] = cnt_ref[...] + jnp.sum(oh, axis=0, keepdims=True)

    rec = jnp.zeros((TM, LANES), F32)
    for col, val in ((R_E, i1 - N_GROUPS), (R_E + 1, i2 - N_GROUPS), (R_RANK, rank1),
                     (R_RANK + 1, rank2), (R_W, w1), (R_W + 1, w2)):
        rec = jnp.where(lane == col, val, rec)
    route_ref[...] = rec
    route_t_ref[0] = rec.T[0:SUBLANES, :]


def _out_router(mixed, x, w_out, layer, gain, wr_cat, br):
    T = x.shape[0]
    n = T // TM
    row = lambda w: pl.BlockSpec((TM, w), lambda i: (jnp.minimum(i, n - 1), 0))
    lag = lambda i: jnp.maximum(i - 1, 0)
    full = lambda shape: pl.BlockSpec(shape, lambda i: (0, 0))
    wo_spec = pl.BlockSpec((1, D_MODEL, D_MODEL), lambda i: (layer, 0, 0),
                           pipeline_mode=pl.Buffered(1))
    return pl.pallas_call(
        _out_router_kernel,
        grid=(n + 1,),
        in_specs=[row(D_MODEL), row(D_MODEL), wo_spec, full((1, D_MODEL)),
                  full((D_MODEL, 2 * ROUTER_COLS)), full((1, ROUTER_COLS))],
        out_specs=[row(D_MODEL), row(D_MODEL),
                   pl.BlockSpec((TM, LANES), lambda i: (lag(i), 0)),
                   pl.BlockSpec((1, SUBLANES, TM), lambda i: (lag(i), 0, 0)), full((8, LANES))],
        out_shape=[jax.ShapeDtypeStruct((T, D_MODEL), F32),
                   jax.ShapeDtypeStruct((T, D_MODEL), BF16),
                   jax.ShapeDtypeStruct((T, LANES), F32),
                   jax.ShapeDtypeStruct((T // TM, SUBLANES, TM), F32),
                   jax.ShapeDtypeStruct((8, LANES), F32)],
        scratch_shapes=[pltpu.VMEM((TM, TM), BF16), pltpu.VMEM((D_MODEL, D_MODEL), BF16),
                        pltpu.VMEM((TM, ROUTER_COLS), F32)],
        compiler_params=pltpu.CompilerParams(
            dimension_semantics=("arbitrary",), vmem_limit_bytes=VMEM_LIMIT),
        name="out_router",
    )(mixed, x, w_out, gain, wr_cat, br)


def _dispatch_kernel(fill_ref, nu_ref, dest_ref, h_ref, xb_ref, zero_ref, sem_ref, zsem_ref,
                     stage_ref):
    i = pl.program_id(0)
    par = lax.rem(i, 2)

    @pl.when(i == 0)
    def _():
        zero_ref[...] = jnp.zeros_like(zero_ref)
        fills = [(fill_ref[e] >= 0, pltpu.make_async_copy(
            zero_ref, xb_ref.at[pl.ds(pl.multiple_of(jnp.maximum(fill_ref[e], 0), MOE_BLK), MOE_BLK)],
            zsem_ref)) for e in range(N_EXPERTS)]
        n_blocks = xb_ref.shape[0] // MOE_BLK
        fills += [(j >= nu_ref[0], pltpu.make_async_copy(
            zero_ref, xb_ref.at[pl.ds(j * MOE_BLK, MOE_BLK)], zsem_ref))
            for j in range(n_blocks - N_EXPERTS, n_blocks)]
        for cond, f in fills:
            pl.when(cond)(f.start)
        for cond, f in fills:
            pl.when(cond)(f.wait)

    _row_tiles_store(stage_ref.at[par], h_ref[...].astype(F32))
    for r in range(TM):
        for k in range(TOP_K):
            pltpu.make_async_copy(stage_ref.at[par, r], xb_ref.at[dest_ref[0, 0, k * TM + r]],
                                  sem_ref.at[par]).start(priority=k)

    def wait_tile(p):
        for _ in range(TOP_K):
            pltpu.make_async_copy(stage_ref.at[p], xb_ref.at[pl.ds(0, TM)], sem_ref.at[p]).wait()

    pl.when(i > 0)(lambda: wait_tile(1 - par))
    pl.when(i == pl.num_programs(0) - 1)(lambda: wait_tile(par))


def _dispatch(fill_start, n_used, dest_tiles, h2, n_rows):
    T = h2.shape[0]
    grid_spec = pltpu.PrefetchScalarGridSpec(
        num_scalar_prefetch=2,
        grid=(T // TM,),
        in_specs=[
            pl.BlockSpec((1, 1, TOP_K * TM), lambda i, fs, nu: (i, 0, 0), memory_space=pltpu.SMEM),
            pl.BlockSpec((TM, D_MODEL), lambda i, fs, nu: (i, 0)),
        ],
        out_specs=pl.BlockSpec(memory_space=pl.ANY),
        scratch_shapes=[pltpu.VMEM((MOE_BLK, ROW_TILES, LANES), F32),
                        pltpu.SemaphoreType.DMA((2,)), pltpu.SemaphoreType.DMA(()),
                        pltpu.VMEM((2, TM, ROW_TILES, LANES), F32)],
    )
    return pl.pallas_call(
        _dispatch_kernel,
        grid_spec=grid_spec,
        out_shape=jax.ShapeDtypeStruct((n_rows, ROW_TILES, LANES), F32),
        compiler_params=pltpu.CompilerParams(dimension_semantics=("arbitrary",)),
        name="dispatch",
    )(fill_start, n_used, dest_tiles, h2)


BLOCK_COPY_PARTS = 4
X_SLOTS = 4
Y_SLOTS = 3


class _CopyGroup:
    def __init__(self, copies):
        self.copies = copies

    def start(self):
        for n, c in enumerate(self.copies):
            c.start(priority=n % 2)

    def wait(self):
        for c in self.copies:
            c.wait()


def _expert_kernel(be_ref, nxt_ref, nu_ref, xb_ref, wg_ref, wu_ref, wd_ref, yb_ref,
                   xbuf_ref, ybuf_ref, wgs_ref, wus_ref, wds_ref, wgb_ref, wub_ref, wdb_ref,
                   xsem_ref, ysem_ref, wsem_ref, *, layer, n_blocks):
    n_used = nu_ref[0]

    part = MOE_BLK // BLOCK_COPY_PARTS

    def x_copy(j, slot):
        return _CopyGroup([pltpu.make_async_copy(
            xb_ref.at[pl.ds(j * MOE_BLK + p * part, part)],
            xbuf_ref.at[slot, pl.ds(p * part, part)], xsem_ref.at[slot])
            for p in range(BLOCK_COPY_PARTS)])

    def y_copy(j, slot):
        return _CopyGroup([pltpu.make_async_copy(
            ybuf_ref.at[slot, pl.ds(p * part, part)],
            yb_ref.at[pl.ds(j * MOE_BLK + p * part, part)], ysem_ref.at[slot])
            for p in range(BLOCK_COPY_PARTS)])

    def w_copies(e, ws):
        return [pltpu.make_async_copy(src.at[layer, e], dst.at[ws], wsem_ref.at[ws])
                for src, dst in ((wg_ref, wgs_ref), (wu_ref, wus_ref), (wd_ref, wds_ref))]

    for j0 in range(X_SLOTS - 1):
        x_copy(j0, j0).start()
    for c in w_copies(be_ref[0], 0):
        c.start()

    def block(j, ws):
        slot = lax.rem(j, Y_SLOTS)
        xslot = lax.rem(j, X_SLOTS)
        first = (j == 0) | (be_ref[j] != be_ref[jnp.maximum(j - 1, 0)])
        ws = jnp.where(first & (j > 0), 1 - ws, ws)

        @pl.when(first)
        def _():
            for c in w_copies(be_ref[j], ws):
                c.wait()
            for r0 in range(0, D_MODEL, W_PREP_ROWS):
                rows = slice(r0, r0 + W_PREP_ROWS)
                wgb_ref[rows, :] = wgs_ref[ws, rows, :].astype(BF16)
                wub_ref[rows, :] = wus_ref[ws, rows, :].astype(BF16)
            for r0 in range(0, D_EXPERT, W_PREP_ROWS):
                rows = slice(r0, r0 + W_PREP_ROWS)
                wdb_ref[rows, :] = wds_ref[ws, rows, :].astype(BF16)

            @pl.when(nxt_ref[j] >= 0)
            def _():
                for c in w_copies(nxt_ref[j], 1 - ws):
                    c.start()

        ahead = j + X_SLOTS - 1

        @pl.when(ahead < n_used)
        def _():
            x_copy(ahead, lax.rem(ahead, X_SLOTS)).start()

        x_copy(j, xslot).wait()
        x = jnp.concatenate(
            [c.astype(BF16) for c in _row_tiles_chunks(xbuf_ref.at[xslot], MOE_BLK)], axis=1)
        g = _dot(x, wgb_ref[...])
        u = _dot(x, wub_ref[...])
        h = (g * (1.0 / (1.0 + jnp.exp(-g)))) * u
        y = _dot(h.astype(BF16), wdb_ref[...])

        @pl.when(j >= Y_SLOTS)
        def _():
            y_copy(j - Y_SLOTS, slot).wait()

        _row_tiles_store(ybuf_ref.at[slot], y)
        y_copy(j, slot).start()
        return ws

    lax.fori_loop(0, n_used, block, jnp.int32(0))

    for back in range(Y_SLOTS, 0, -1):
        y_copy(n_used - back, lax.rem(n_used - back, Y_SLOTS)).wait()

    ybuf_ref[0] = jnp.zeros((MOE_BLK, ROW_TILES, LANES), F32)

    def fill(j, carry):
        y_copy(j, 0).start()
        return carry

    def fill_wait(j, carry):
        y_copy(j, 0).wait()
        return carry

    lax.fori_loop(n_used, n_blocks, fill, 0)
    lax.fori_loop(n_used, n_blocks, fill_wait, 0)


def _experts(blk_exp, nxt_exp, n_used, xb, w_gate, w_up, w_down, layer):
    n_blocks = blk_exp.shape[0]
    any_spec = pl.BlockSpec(memory_space=pl.ANY)
    blk = (MOE_BLK, ROW_TILES, LANES)
    grid_spec = pltpu.PrefetchScalarGridSpec(
        num_scalar_prefetch=3,
        grid=(1,),
        in_specs=[any_spec, any_spec, any_spec, any_spec],
        out_specs=any_spec,
        scratch_shapes=[
            pltpu.VMEM((X_SLOTS,) + blk, F32), pltpu.VMEM((Y_SLOTS,) + blk, F32),
            pltpu.VMEM((2, D_MODEL, D_EXPERT), F32), pltpu.VMEM((2, D_MODEL, D_EXPERT), F32),
            pltpu.VMEM((2, D_EXPERT, D_MODEL), F32),
            pltpu.VMEM((D_MODEL, D_EXPERT), BF16), pltpu.VMEM((D_MODEL, D_EXPERT), BF16),
            pltpu.VMEM((D_EXPERT, D_MODEL), BF16),
            pltpu.SemaphoreType.DMA((X_SLOTS,)), pltpu.SemaphoreType.DMA((Y_SLOTS,)),
            pltpu.SemaphoreType.DMA((2,)),
        ],
    )
    return pl.pallas_call(
        functools.partial(_expert_kernel, layer=layer, n_blocks=n_blocks),
        grid_spec=grid_spec,
        out_shape=jax.ShapeDtypeStruct((n_blocks * MOE_BLK, ROW_TILES, LANES), F32),
        compiler_params=pltpu.CompilerParams(
            dimension_semantics=("arbitrary",), vmem_limit_bytes=VMEM_LIMIT),
        name="experts",
    )(blk_exp, nxt_exp, n_used, xb, w_gate, w_up, w_down)


def _dispatch_tables(route_t, counts_rec, T):
    counts = counts_rec[0, N_GROUPS:N_GROUPS + N_EXPERTS].astype(jnp.int32)
    n_steps = (T * TOP_K) // MOE_BLK + N_EXPERTS
    nblk = (counts + MOE_BLK - 1) // MOE_BLK
    bend = jnp.cumsum(nblk)
    pstart = (bend - nblk) * MOE_BLK
    n_used = bend[-1]
    j = jnp.minimum(jnp.arange(n_steps, dtype=jnp.int32), n_used - 1)
    blk_exp = jnp.minimum(jnp.sum(j[:, None] >= bend[None, :], axis=1), N_EXPERTS - 1)
    n_rows = n_steps * MOE_BLK
    last_blk = jnp.where(counts > 0, (bend - 1) * MOE_BLK, -1)
    ids = jnp.arange(N_EXPERTS, dtype=jnp.int32)
    later = (ids[None, :] > ids[:, None]) & (nblk[None, :] > 0)
    nxt_of = jnp.min(jnp.where(later, ids[None, :], N_EXPERTS), axis=1)
    nxt_tab = jnp.where(nxt_of < N_EXPERTS, nxt_of, -1)
    nxt_exp = jnp.sum(jnp.where(blk_exp[:, None] == ids[None, :], nxt_tab[None, :], 0), axis=1)
    e = route_t[:, R_E:R_E + TOP_K, :].astype(jnp.int32)
    rank = route_t[:, R_RANK:R_RANK + TOP_K, :].astype(jnp.int32)
    seg = jnp.sum(jnp.where(e[..., None] == jnp.arange(N_EXPERTS), pstart, 0), axis=-1)
    dest = jnp.clip(seg + rank, 0, n_steps * MOE_BLK - 1)
    dest_tiles = dest.reshape(T // TM, 1, TOP_K * TM)
    return dict(dest_tiles=dest_tiles, fill_start=last_blk.astype(jnp.int32),
                blk_exp=blk_exp.astype(jnp.int32), nxt_exp=nxt_exp.astype(jnp.int32),
                n_used=n_used.reshape(1).astype(jnp.int32), n_rows=n_rows)


def _prep_layer(l, w_gk_up, b_gk, gla_norm, gmlp_norm, w_spatial, b_spatial, w_conv,
                w_router_group, b_router_group, w_router_expert, b_router_expert):
    wgk = jnp.concatenate(
        [w_gk_up[l], jnp.zeros((LANES - GLA_GATE_RANK, GLA_KDIM), F32)], axis=0).astype(BF16)
    wsp = w_spatial[l].transpose(1, 0, 2).reshape(GMLP_CHUNK, GMLP_HEADS * GMLP_CHUNK)
    bsp = jnp.repeat(b_spatial[l].T, GMLP_DH, axis=1)
    wconv = jnp.concatenate([w_conv[l], jnp.zeros((8 - CONV_K, CONV_WIDTH), F32)], axis=0)
    wr = jnp.concatenate(
        [w_router_group[l], w_router_expert[l],
         jnp.zeros((D_MODEL, ROUTER_COLS - N_GROUPS - N_EXPERTS), F32)], axis=1)
    wr_hi = wr.astype(BF16)
    wr_lo = (wr - wr_hi.astype(F32)).astype(BF16)
    br = jnp.concatenate(
        [b_router_group[l], b_router_expert[l],
         jnp.zeros((ROUTER_COLS - N_GROUPS - N_EXPERTS,), F32)])[None, :]
    return dict(
        wgk=wgk, bgk=b_gk[l][None, :], glan=gla_norm[l][None, :], gmn=gmlp_norm[l][None, :],
        wsp=wsp, bsp=bsp, wconv=wconv, wr_cat=jnp.concatenate([wr_hi, wr_lo], axis=1), br=br)


def kernel(x, attn_norm, w_in, w_gk_up, b_gk, gla_norm, gmlp_norm, w_spatial, b_spatial, w_conv, w_out, ffn_norm, w_router_group, b_router_group, w_router_expert, b_router_expert, w_gate, w_up, w_down, final_norm):
    B, S, D = x.shape
    T = B * S
    depth = w_in.shape[0]
    xr = x.reshape(T, D)
    w_in_t = jnp.swapaxes(w_in, 1, 2)
    moe = None
    for l in range(depth):
        p = _prep_layer(l, w_gk_up, b_gk, gla_norm, gmlp_norm, w_spatial, b_spatial, w_conv,
                        w_router_group, b_router_group, w_router_expert, b_router_expert)
        mix_params = (p["wgk"], p["bgk"], p["glan"], p["gmn"], p["wsp"], p["bsp"], p["wconv"])
        if moe is None:
            mixed = _front(xr, attn_norm[l][None, :], w_in_t, l, mix_params, B, S)
        else:
            xr, mixed = _front(moe["x2"], attn_norm[l][None, :], w_in_t, l, mix_params, B, S, moe)
        x2, h2, route, route_t, counts_rec = _out_router(
            mixed, xr, w_out, l, ffn_norm[l][None, :], p["wr_cat"], p["br"])
        moe = _dispatch_tables(route_t, counts_rec, T)
        xb = _dispatch(moe["fill_start"], moe["n_used"], moe["dest_tiles"], h2, moe["n_rows"])
        yb = _experts(moe["blk_exp"], moe["nxt_exp"], moe["n_used"], xb, w_gate, w_up, w_down, l)
        moe.update(x2=x2, route=route, yb=yb)
    out = _combine_final_norm(moe["dest_tiles"], moe["x2"], moe["route"], moe["yb"],
                              final_norm[None, :])
    return out.reshape(B, S, D)
```

```python
import functools

import jax
import jax.numpy as jnp
from jax import lax
from jax.experimental import pallas as pl
from jax.experimental.pallas import tpu as pltpu

F32 = jnp.float32
BF16 = jnp.bfloat16

D_MODEL = 1024
RMS_EPS = 1e-6
GLA_HEADS = 4
GLA_WIDTH = 512
GLA_DV = 128
GLA_DK = 64
GLA_KDIM = 256
GLA_GATE_RANK = 16
GLA_GATE_NORM = 16.0
GLA_CHUNK = 64
GMLP_HEADS = 4
GMLP_WIDTH = 256
GMLP_DH = 64
GMLP_CHUNK = 128
CONV_WIDTH = 256
CONV_K = 3
N_GROUPS = 4
EXPERTS_PER_GROUP = 8
N_EXPERTS = 32
TOP_K = 2
D_EXPERT = 256

LANES = 128
C_Q, C_K, C_V, C_G = 0, 256, 512, 1024
C_U, C_VG, C_X, C_BG, C_CG, C_GKL = 1536, 1792, 2048, 2304, 2560, 2816
D_PROJ = C_GKL + LANES
D_IN = C_GKL + GLA_GATE_RANK

TM = 256
TS_MIX = TM
MOE_BLK = 256
ROUTER_COLS = LANES
SUBLANES = 8
ROW_TILES = D_MODEL // LANES
assert ROW_TILES == SUBLANES
VMEM_LIMIT = 56 * 1024 * 1024
R_E, R_RANK, R_W = 0, 2, 4


def _dot(a, b):
    return jnp.dot(a, b, preferred_element_type=F32)


def _split_bf16(x):
    hi = x.astype(BF16)
    lo = (x - hi.astype(F32)).astype(BF16)
    return hi, lo


def _rms(x, gain):
    return x * lax.rsqrt(jnp.mean(x * x, axis=-1, keepdims=True) + RMS_EPS) * gain


W_PREP_ROWS = 128
PROJ_CHUNK = 256


def _stage_w_in(wt_ref, wb_ref):
    for c0 in range(0, C_GKL, LANES):
        src = c0 if c0 < C_U else c0 + GLA_GATE_RANK
        wb_ref[:, c0:c0 + LANES] = wt_ref[0, src:src + LANES, :].T.astype(BF16)
    low = jnp.concatenate([wt_ref[0, C_U:C_U + GLA_GATE_RANK, :],
                           jnp.zeros((LANES - GLA_GATE_RANK, D_MODEL), F32)], axis=0)
    wb_ref[:, C_GKL:D_PROJ] = low.T.astype(BF16)


def _row_gather_copy(yb_ref, buf_ref, sem_ref, slot, k, r, d):
    return pltpu.make_async_copy(yb_ref.at[d], buf_ref.at[slot, k, r], sem_ref.at[slot])


def _gather_start(dest_ref, yb_ref, buf_ref, sem_ref, slot, rows=range(TM)):
    for r in rows:
        for k in range(TOP_K):
            _row_gather_copy(yb_ref, buf_ref, sem_ref, slot, k, r,
                             dest_ref[0, 0, k * TM + r]).start(priority=k)


def _gather_wait(yb_ref, buf_ref, sem_ref, slot):
    for k in range(TOP_K):
        pltpu.make_async_copy(yb_ref.at[pl.ds(0, TM)], buf_ref.at[slot, k], sem_ref.at[slot]).wait()


def _combined_residual(dcur_ref, x_ref, route_ref, yb_ref, buf_ref, sem_ref):
    i = pl.program_id(0)
    slot = lax.rem(i, 2)

    @pl.when(i == 0)
    def _():
        _gather_start(dcur_ref, yb_ref, buf_ref, sem_ref, 0)

    _gather_wait(yb_ref, buf_ref, sem_ref, slot)
    w0 = route_ref[:, R_W:R_W + 1]
    w1 = route_ref[:, R_W + 1:R_W + 2]
    y0 = _row_tiles_chunks(buf_ref.at[slot, 0], TM)
    y1 = _row_tiles_chunks(buf_ref.at[slot, 1], TM)
    return jnp.concatenate(
        [x_ref[:, c * LANES:(c + 1) * LANES] + (w0 * y0[c] + w1 * y1[c]) for c in range(ROW_TILES)],
        axis=1)


def _prefetch_groups(n_groups):
    per = -(-TM // n_groups)
    return [range(g * per, min(TM, (g + 1) * per)) for g in range(n_groups)]


def _drain_last_prefetch(yb_ref, buf_ref, sem_ref):
    i = pl.program_id(0)

    @pl.when(i == pl.num_programs(0) - 1)
    def _():
        _gather_wait(yb_ref, buf_ref, sem_ref, 1 - lax.rem(i, 2))


def _combine_specs(n_tiles):
    smem_tile = lambda f: pl.BlockSpec((1, 1, TOP_K * TM), f, memory_space=pltpu.SMEM)
    return [
        smem_tile(lambda i: (i, 0, 0)),
        smem_tile(lambda i: (jnp.minimum(i + 1, n_tiles - 1), 0, 0)),
        pl.BlockSpec((TM, D_MODEL), lambda i: (i, 0)),
        pl.BlockSpec((TM, LANES), lambda i: (i, 0)),
        pl.BlockSpec(memory_space=pl.ANY),
    ]


_COMBINE_SCRATCH = [pltpu.VMEM((2, TOP_K, TM, ROW_TILES, LANES), F32),
                    pltpu.SemaphoreType.DMA((2,))]


def _combine_final_norm_kernel(dcur_ref, dnxt_ref, x_ref, route_ref, yb_ref, gain_ref,
                               o_ref, buf_ref, sem_ref):
    _gather_start(dnxt_ref, yb_ref, buf_ref, sem_ref, 1 - lax.rem(pl.program_id(0), 2))
    x = _combined_residual(dcur_ref, x_ref, route_ref, yb_ref, buf_ref, sem_ref)
    o_ref[...] = _rms(x, gain_ref[...])
    _drain_last_prefetch(yb_ref, buf_ref, sem_ref)


def _combine_final_norm(dest_tiles, x2, route, yb, gain):
    T = x2.shape[0]
    n_tiles = T // TM
    return pl.pallas_call(
        _combine_final_norm_kernel,
        grid=(n_tiles,),
        in_specs=_combine_specs(n_tiles) + [pl.BlockSpec((1, D_MODEL), lambda i: (0, 0))],
        out_specs=pl.BlockSpec((TM, D_MODEL), lambda i: (i, 0)),
        out_shape=jax.ShapeDtypeStruct((T, D_MODEL), F32),
        scratch_shapes=_COMBINE_SCRATCH,
        compiler_params=pltpu.CompilerParams(
            dimension_semantics=("arbitrary",), vmem_limit_bytes=VMEM_LIMIT),
        name="combine_final_norm",
    )(dest_tiles, dest_tiles, x2, route, yb, gain)


def _gelu_tanh(x):
    c = 0.7978845608028654
    return x * (0.5 * (1.0 + jnp.tanh(c * (x + 0.044715 * (x * x * x)))))


def _mixer_kernel(proj_ref, wgk_ref, bgk_ref, glan_ref, gmn_ref, wsp_ref, bsp_ref, wconv_ref,
                  out_ref, st_ref, hc_ref, lcat_ref, wm_ref, *, seq_start, first_step):
    TS = TS_MIX
    n_gla = TS // GLA_CHUNK
    n_gm = TS // GMLP_CHUNK

    @pl.when(seq_start)
    def _():
        st_ref[...] = jnp.zeros_like(st_ref)
        hc_ref[...] = jnp.zeros_like(hc_ref)

    @pl.when(first_step)
    def _():
        r = lax.broadcasted_iota(jnp.int32, (TS, TS), 0)
        c = lax.broadcasted_iota(jnp.int32, (TS, TS), 1)
        keep = ((r // GLA_CHUNK) == (c // GLA_CHUNK)) & (c <= r)
        lcat_ref[...] = jnp.where(keep, 1.0, 0.0).astype(BF16)
        t = lax.broadcasted_iota(jnp.int32, (GMLP_CHUNK, GMLP_HEADS * GMLP_CHUNK), 0)
        s = lax.broadcasted_iota(jnp.int32, (GMLP_CHUNK, GMLP_HEADS * GMLP_CHUNK), 1) % GMLP_CHUNK
        wm_ref[...] = jnp.where(s <= t, wsp_ref[...], 0.0).astype(BF16)

    lane256 = lax.broadcasted_iota(jnp.int32, (1, GLA_KDIM), 1)

    q = proj_ref[:, C_Q:C_Q + GLA_KDIM].astype(F32)
    k = proj_ref[:, C_K:C_K + GLA_KDIM].astype(F32)
    v_b = proj_ref[:, C_V:C_V + GLA_WIDTH]
    z = _dot(proj_ref[:, C_GKL:C_GKL + LANES], wgk_ref[...]) + bgk_ref[...]
    gk = (jnp.minimum(z, 0.0) - jnp.log1p(jnp.exp(-jnp.abs(z)))) * (1.0 / GLA_GATE_NORM)
    gk_hi, gk_lo = _split_bf16(gk)
    cs = _dot(lcat_ref[...], jnp.concatenate([gk_hi, gk_lo], axis=1))
    b = cs[:, :GLA_KDIM] + cs[:, GLA_KDIM:]
    b_last = [b[(c + 1) * GLA_CHUNK - 1:(c + 1) * GLA_CHUNK, :] for c in range(n_gla)]
    bl = jnp.concatenate(
        [jnp.broadcast_to(t, (GLA_CHUNK, GLA_KDIM)) for t in b_last], axis=0)
    q_dec = (q * (GLA_DK ** -0.5)) * jnp.exp(b)
    k_inv = (k * jnp.exp(-b)).astype(BF16)
    k_dec = (k * jnp.exp(bl - b)).astype(BF16)
    q_dec_b = q_dec.astype(BF16)

    zero_b = jnp.zeros_like(q_dec_b)
    q_stack = jnp.concatenate(
        [jnp.where((lane256 // GLA_DK) == h, q_dec_b, zero_b) for h in range(GLA_HEADS)], axis=0)
    scores = lax.dot_general(q_stack, k_inv, (((1,), (1,)), ((), ())),
                             preferred_element_type=F32)
    rt = lax.broadcasted_iota(jnp.int32, (TS, TS), 0)
    ct = lax.broadcasted_iota(jnp.int32, (TS, TS), 1)
    causal = ((rt // GLA_CHUNK) == (ct // GLA_CHUNK)) & (ct <= rt)
    o_heads = []
    for h in range(GLA_HEADS):
        p_h = jnp.where(causal, scores[h * TS:(h + 1) * TS, :], 0.0).astype(BF16)
        o_heads.append(_dot(p_h, v_b[:, h * GLA_DV:(h + 1) * GLA_DV]))

    sr = lax.broadcasted_iota(jnp.int32, (GLA_WIDTH, GLA_KDIM), 0) // GLA_DV
    sc = lax.broadcasted_iota(jnp.int32, (GLA_WIDTH, GLA_KDIM), 1) // GLA_DK
    bd_mask = sr == sc
    o_inter = []
    for c in range(n_gla):
        rows = slice(c * GLA_CHUNK, (c + 1) * GLA_CHUNK)
        st = st_ref[...]
        o_inter.append(lax.dot_general(q_dec_b[rows], st.astype(BF16), (((1,), (1,)), ((), ())),
                                       preferred_element_type=F32))
        upd = lax.dot_general(v_b[rows], k_dec[rows], (((0,), (0,)), ((), ())),
                              preferred_element_type=F32)
        decay = jnp.exp(b_last[c])
        st_ref[...] = st * decay + jnp.where(bd_mask, upd, 0.0)
    o_inter = jnp.concatenate(o_inter, axis=0)

    for h in range(GLA_HEADS):
        cols = slice(h * GLA_DV, (h + 1) * GLA_DV)
        o = o_heads[h] + o_inter[:, cols]
        o = o * lax.rsqrt(jnp.mean(o * o, axis=-1, keepdims=True) + RMS_EPS) * glan_ref[...]
        g = proj_ref[:, C_G + h * GLA_DV:C_G + (h + 1) * GLA_DV].astype(F32)
        out_ref[:, cols] = (o * (g * (1.0 / (1.0 + jnp.exp(-g))))).astype(out_ref.dtype)

    u = _gelu_tanh(proj_ref[:, C_U:C_U + GMLP_WIDTH].astype(F32))
    vg = _gelu_tanh(proj_ref[:, C_VG:C_VG + GMLP_WIDTH].astype(F32))
    hr = lax.broadcasted_iota(jnp.int32, (GMLP_WIDTH, GMLP_WIDTH), 0) // GMLP_DH
    hcn = lax.broadcasted_iota(jnp.int32, (GMLP_WIDTH, GMLP_WIDTH), 1) // GMLP_DH
    head_mean = jnp.where(hr == hcn, 1.0 / GMLP_DH, 0.0).astype(BF16)
    sq_hi, sq_lo = _split_bf16(vg * vg)
    ms = _dot(sq_hi, head_mean) + _dot(sq_lo, head_mean)
    v32 = vg * lax.rsqrt(ms + RMS_EPS) * gmn_ref[...]
    for c in range(n_gm):
        rows = slice(c * GMLP_CHUNK, (c + 1) * GMLP_CHUNK)
        vc = v32[rows].astype(BF16)
        zc = jnp.zeros_like(vc)
        rhs = jnp.concatenate(
            [jnp.where((lane256 // GMLP_DH) == h, vc, zc) for h in range(GMLP_HEADS)], axis=0)
        mixed = _dot(wm_ref[...], rhs) + bsp_ref[...]
        out_ref[rows, GLA_WIDTH:GLA_WIDTH + GMLP_WIDTH] = (u[rows] * mixed).astype(out_ref.dtype)

    hcv = (proj_ref[:, C_CG:C_CG + CONV_WIDTH].astype(F32)
           * proj_ref[:, C_X:C_X + CONV_WIDTH].astype(F32))
    hc_ref[8:8 + TS, :] = hcv
    y = (wconv_ref[2:3, :] * hcv + wconv_ref[1:2, :] * hc_ref[7:7 + TS, :]
         + wconv_ref[0:1, :] * hc_ref[6:6 + TS, :])
    out_ref[:, GLA_WIDTH + GMLP_WIDTH:] = (
        proj_ref[:, C_BG:C_BG + CONV_WIDTH].astype(F32) * y).astype(out_ref.dtype)
    hc_ref[0:8, :] = hc_ref[TS:TS + 8, :]


_MIXER_SCRATCH = [
    pltpu.VMEM((GLA_WIDTH, GLA_KDIM), F32),
    pltpu.VMEM((TS_MIX + 8, CONV_WIDTH), F32),
    pltpu.VMEM((TS_MIX, TS_MIX), BF16),
    pltpu.VMEM((GMLP_CHUNK, GMLP_HEADS * GMLP_CHUNK), BF16),
]


N_MIX_PARAMS = 7


def _front_kernel(*refs, tiles_per_seq, combine):
    refs = list(refs)
    if combine:
        dcur_ref, dnxt_ref, x_ref, route_ref, yb_ref = refs[:5]
        del refs[:5]
    else:
        x_ref = refs.pop(0)
    gain_ref, wt_ref = refs[:2]
    mix_refs = refs[2:2 + N_MIX_PARAMS]
    del refs[:2 + N_MIX_PARAMS]
    if combine:
        xo_ref, out_ref, buf_ref, sem_ref = refs[:4]
        del refs[:4]
    else:
        out_ref = refs.pop(0)
    st_ref, hc_ref, lcat_ref, wm_ref, wb_ref, pcur_ref, pnext_ref = refs
    s = pl.program_id(0)

    @pl.when(s == 0)
    def _():
        _stage_w_in(wt_ref, wb_ref)
        pcur_ref[...] = jnp.zeros_like(pcur_ref)

    _mixer_kernel(pcur_ref, *mix_refs, out_ref, st_ref, hc_ref, lcat_ref, wm_ref,
                  seq_start=lax.rem(jnp.maximum(s - 1, 0), tiles_per_seq) == 0, first_step=s == 0)

    if combine:
        x = _combined_residual(dcur_ref, x_ref, route_ref, yb_ref, buf_ref, sem_ref)
        xo_ref[...] = x
    else:
        x = x_ref[...]
    h = _rms(x, gain_ref[...]).astype(BF16)
    col_chunks = [(c0, min(c0 + PROJ_CHUNK, D_PROJ)) for c0 in range(0, D_PROJ, PROJ_CHUNK)]
    for rows, (c0, c1) in zip(_prefetch_groups(len(col_chunks)), col_chunks):
        if combine:
            _gather_start(dnxt_ref, yb_ref, buf_ref, sem_ref, 1 - lax.rem(s, 2), rows)
        pnext_ref[:, c0:c1] = _dot(h, wb_ref[:, c0:c1]).astype(BF16)
    if combine:
        _drain_last_prefetch(yb_ref, buf_ref, sem_ref)
    pcur_ref[...] = pnext_ref[...]


def _front(x, gain, w_in_t, layer, mix_params, batch, seq, moe=None):
    n_seq = seq // TS_MIX
    n = batch * n_seq
    T = batch * seq
    cur = lambda s: jnp.minimum(s, n - 1)
    full = lambda shape: pl.BlockSpec(shape, lambda s: (0,) * len(shape))
    row = lambda w, f: pl.BlockSpec((TM, w), lambda s: (f(s), 0))
    in_specs, args = [row(D_MODEL, cur)], [x]
    out_specs = [row(D_MODEL, lambda s: jnp.maximum(s - 1, 0))]
    out_shape = [jax.ShapeDtypeStruct((T, D_MODEL), BF16)]
    scratch = list(_MIXER_SCRATCH)
    if moe is not None:
        smem_tile = lambda f: pl.BlockSpec((1, 1, TOP_K * TM), lambda s: (f(s), 0, 0),
                                           memory_space=pltpu.SMEM)
        in_specs = [smem_tile(cur), smem_tile(lambda s: jnp.minimum(s + 1, n - 1))] + in_specs + [
            row(LANES, cur), pl.BlockSpec(memory_space=pl.ANY)]
        args = [moe["dest_tiles"], moe["dest_tiles"]] + args + [moe["route"], moe["yb"]]
        out_specs = [row(D_MODEL, cur)] + out_specs
        out_shape = [jax.ShapeDtypeStruct((T, D_MODEL), F32)] + out_shape
        scratch = _COMBINE_SCRATCH + scratch
    in_specs += [
        full((1, D_MODEL)),
        pl.BlockSpec((1, D_IN, D_MODEL), lambda s: (layer, 0, 0), pipeline_mode=pl.Buffered(1)),
        full((LANES, GLA_KDIM)), full((1, GLA_KDIM)), full((1, GLA_DV)), full((1, GMLP_WIDTH)),
        full((GMLP_CHUNK, GMLP_HEADS * GMLP_CHUNK)), full((GMLP_CHUNK, GMLP_WIDTH)),
        full((8, CONV_WIDTH)),
    ]
    scratch += [pltpu.VMEM((D_MODEL, D_PROJ), BF16), pltpu.VMEM((TS_MIX, D_PROJ), BF16),
                pltpu.VMEM((TS_MIX, D_PROJ), BF16)]
    return pl.pallas_call(
        functools.partial(_front_kernel, tiles_per_seq=n_seq, combine=moe is not None),
        grid=(n + 1,),
        in_specs=in_specs,
        out_specs=out_specs if moe is not None else out_specs[0],
        out_shape=out_shape if moe is not None else out_shape[0],
        scratch_shapes=scratch,
        compiler_params=pltpu.CompilerParams(
            dimension_semantics=("arbitrary",), vmem_limit_bytes=VMEM_LIMIT),
        name="front",
    )(*args, gain, w_in_t, *mix_params)


def _row_tiles_store(tiles_ref, x):
    rows = x.shape[0]
    flat = tiles_ref.reshape(rows * ROW_TILES, LANES)
    for c in range(ROW_TILES):
        flat[pl.ds(c, rows, stride=ROW_TILES), :] = x[:, c * LANES:(c + 1) * LANES]


def _row_tiles_chunks(tiles_ref, rows):
    flat = tiles_ref.reshape(rows * ROW_TILES, LANES)
    return [flat[pl.ds(c, rows, stride=ROW_TILES), :] for c in range(ROW_TILES)]


def _out_router_kernel(mix_ref, x_ref, wo_ref, gain_ref, wrc_ref, br_ref,
                       x2_ref, h2_ref, route_ref, route_t_ref, cnt_ref, tri_ref, wob_ref, lg_ref):
    i = pl.program_id(0)

    @pl.when(i == 0)
    def _():
        cnt_ref[...] = jnp.zeros_like(cnt_ref)
        lg_ref[...] = jnp.zeros_like(lg_ref)
        r = lax.broadcasted_iota(jnp.int32, (TM, TM), 0)
        c = lax.broadcasted_iota(jnp.int32, (TM, TM), 1)
        tri_ref[...] = jnp.where(c < r, 1.0, 0.0).astype(BF16)
        for r0 in range(0, D_MODEL, W_PREP_ROWS):
            wob_ref[r0:r0 + W_PREP_ROWS, :] = wo_ref[0, r0:r0 + W_PREP_ROWS, :].astype(BF16)

    lg = lg_ref[...]
    x2 = x_ref[...] + _dot(mix_ref[...], wob_ref[...])
    x2_ref[...] = x2
    h = _rms(x2, gain_ref[...])
    h_hi, h_lo = _split_bf16(h)
    h2_ref[...] = h_hi
    hh_hl = _dot(h_hi, wrc_ref[...])
    lg_ref[...] = (hh_hl[:, :ROUTER_COLS] + hh_hl[:, ROUTER_COLS:]
                   + _dot(h_lo, wrc_ref[:, :ROUTER_COLS]) + br_ref[...])

    lane = lax.broadcasted_iota(jnp.int32, (TM, LANES), 1).astype(F32)
    neg = -jnp.inf
    is_g = lane < N_GROUPS
    gl = jnp.where(is_g, lg, neg)
    gmax = jnp.max(gl, axis=1, keepdims=True)
    g_top = jnp.min(jnp.where(gl == gmax, lane, float(LANES)), axis=1, keepdims=True)
    g_w = 1.0 / jnp.sum(jnp.where(is_g, jnp.exp(lg - gmax), 0.0), axis=1, keepdims=True)
    first = N_GROUPS + EXPERTS_PER_GROUP * g_top
    el = jnp.where((lane >= first) & (lane < first + EXPERTS_PER_GROUP), lg, neg)
    m1 = jnp.max(el, axis=1, keepdims=True)
    i1 = jnp.min(jnp.where(el == m1, lane, float(LANES)), axis=1, keepdims=True)
    el2 = jnp.where(lane == i1, neg, el)
    m2 = jnp.max(el2, axis=1, keepdims=True)
    i2 = jnp.min(jnp.where(el2 == m2, lane, float(LANES)), axis=1, keepdims=True)
    ratio = jnp.exp(m2 - m1)
    w1 = g_w / (1.0 + ratio)
    w2 = w1 * ratio

    oh1 = jnp.where(lane == i1, 1.0, 0.0)
    oh2 = jnp.where(lane == i2, 1.0, 0.0)
    oh = jnp.where(i > 0, oh1 + oh2, 0.0)
    before = _dot(tri_ref[...], oh.astype(BF16)) + cnt_ref[0:1, :]
    rank1 = jnp.sum(oh1 * before, axis=1, keepdims=True)
    rank2 = jnp.sum(oh2 * before, axis=1, keepdims=True)
    cnt_ref[...] = cnt_ref[...] + jnp.sum(oh, axis=0, keepdims=True)

    rec = jnp.zeros((TM, LANES), F32)
    for col, val in ((R_E, i1 - N_GROUPS), (R_E + 1, i2 - N_GROUPS), (R_RANK, rank1),
                     (R_RANK + 1, rank2), (R_W, w1), (R_W + 1, w2)):
        rec = jnp.where(lane == col, val, rec)
    route_ref[...] = rec
    route_t_ref[0] = rec.T[0:SUBLANES, :]


def _out_router(mixed, x, w_out, layer, gain, wr_cat, br):
    T = x.shape[0]
    n = T // TM
    row = lambda w: pl.BlockSpec((TM, w), lambda i: (jnp.minimum(i, n - 1), 0))
    lag = lambda i: jnp.maximum(i - 1, 0)
    full = lambda shape: pl.BlockSpec(shape, lambda i: (0, 0))
    wo_spec = pl.BlockSpec((1, D_MODEL, D_MODEL), lambda i: (layer, 0, 0),
                           pipeline_mode=pl.Buffered(1))
    return pl.pallas_call(
        _out_router_kernel,
        grid=(n + 1,),
        in_specs=[row(D_MODEL), row(D_MODEL), wo_spec, full((1, D_MODEL)),
                  full((D_MODEL, 2 * ROUTER_COLS)), full((1, ROUTER_COLS))],
        out_specs=[row(D_MODEL), row(D_MODEL),
                   pl.BlockSpec((TM, LANES), lambda i: (lag(i), 0)),
                   pl.BlockSpec((1, SUBLANES, TM), lambda i: (lag(i), 0, 0)), full((8, LANES))],
        out_shape=[jax.ShapeDtypeStruct((T, D_MODEL), F32),
                   jax.ShapeDtypeStruct((T, D_MODEL), BF16),
                   jax.ShapeDtypeStruct((T, LANES), F32),
                   jax.ShapeDtypeStruct((T // TM, SUBLANES, TM), F32),
                   jax.ShapeDtypeStruct((8, LANES), F32)],
        scratch_shapes=[pltpu.VMEM((TM, TM), BF16), pltpu.VMEM((D_MODEL, D_MODEL), BF16),
                        pltpu.VMEM((TM, ROUTER_COLS), F32)],
        compiler_params=pltpu.CompilerParams(
            dimension_semantics=("arbitrary",), vmem_limit_bytes=VMEM_LIMIT),
        name="out_router",
    )(mixed, x, w_out, gain, wr_cat, br)


def _dispatch_kernel(fill_ref, nu_ref, dest_ref, h_ref, xb_ref, zero_ref, sem_ref, zsem_ref,
                     stage_ref):
    i = pl.program_id(0)
    par = lax.rem(i, 2)
    last = i == pl.num_programs(0) - 1
    n_blocks = xb_ref.shape[0] // MOE_BLK
    spare_fills = [(j >= nu_ref[0], pltpu.make_async_copy(
        zero_ref, xb_ref.at[pl.ds(j * MOE_BLK, MOE_BLK)], zsem_ref.at[1]))
        for j in range(n_blocks - N_EXPERTS, n_blocks)]

    @pl.when(i == 0)
    def _():
        zero_ref[...] = jnp.zeros_like(zero_ref)
        fills = [(fill_ref[e] >= 0, pltpu.make_async_copy(
            zero_ref, xb_ref.at[pl.ds(pl.multiple_of(jnp.maximum(fill_ref[e], 0), MOE_BLK), MOE_BLK)],
            zsem_ref.at[0])) for e in range(N_EXPERTS)]
        for cond, f in fills + spare_fills:
            pl.when(cond)(f.start)
        for cond, f in fills:
            pl.when(cond)(f.wait)

    _row_tiles_store(stage_ref.at[par], h_ref[...].astype(F32))
    for r in range(TM):
        for k in range(TOP_K):
            pltpu.make_async_copy(stage_ref.at[par, r], xb_ref.at[dest_ref[0, 0, k * TM + r]],
                                  sem_ref.at[par]).start(priority=k)

    def wait_tile(p):
        for _ in range(TOP_K):
            pltpu.make_async_copy(stage_ref.at[p], xb_ref.at[pl.ds(0, TM)], sem_ref.at[p]).wait()

    pl.when(i > 0)(lambda: wait_tile(1 - par))
    @pl.when(last)
    def _():
        wait_tile(par)
        for cond, f in spare_fills:
            pl.when(cond)(f.wait)


def _dispatch(fill_start, n_used, dest_tiles, h2, n_rows):
    T = h2.shape[0]
    grid_spec = pltpu.PrefetchScalarGridSpec(
        num_scalar_prefetch=2,
        grid=(T // TM,),
        in_specs=[
            pl.BlockSpec((1, 1, TOP_K * TM), lambda i, fs, nu: (i, 0, 0), memory_space=pltpu.SMEM),
            pl.BlockSpec((TM, D_MODEL), lambda i, fs, nu: (i, 0)),
        ],
        out_specs=pl.BlockSpec(memory_space=pl.ANY),
        scratch_shapes=[pltpu.VMEM((MOE_BLK, ROW_TILES, LANES), F32),
                        pltpu.SemaphoreType.DMA((2,)), pltpu.SemaphoreType.DMA((2,)),
                        pltpu.VMEM((2, TM, ROW_TILES, LANES), F32)],
    )
    return pl.pallas_call(
        _dispatch_kernel,
        grid_spec=grid_spec,
        out_shape=jax.ShapeDtypeStruct((n_rows, ROW_TILES, LANES), F32),
        compiler_params=pltpu.CompilerParams(dimension_semantics=("arbitrary",)),
        name="dispatch",
    )(fill_start, n_used, dest_tiles, h2)


BLOCK_COPY_PARTS = 4
X_SLOTS = 4
Y_SLOTS = 3


class _CopyGroup:
    def __init__(self, copies):
        self.copies = copies

    def start(self):
        for n, c in enumerate(self.copies):
            c.start(priority=n % 2)

    def wait(self):
        for c in self.copies:
            c.wait()


def _expert_kernel(be_ref, nxt_ref, nu_ref, xb_ref, wg_ref, wu_ref, wd_ref, yb_ref,
                   xbuf_ref, ybuf_ref, wgs_ref, wus_ref, wds_ref, wgb_ref, wub_ref, wdb_ref,
                   xsem_ref, ysem_ref, wsem_ref, zbuf_ref, zsem_ref, *, layer, n_blocks):
    n_used = nu_ref[0]

    part = MOE_BLK // BLOCK_COPY_PARTS

    def x_copy(j, slot):
        return _CopyGroup([pltpu.make_async_copy(
            xb_ref.at[pl.ds(j * MOE_BLK + p * part, part)],
            xbuf_ref.at[slot, pl.ds(p * part, part)], xsem_ref.at[slot])
            for p in range(BLOCK_COPY_PARTS)])

    def y_copy(j, slot):
        return _CopyGroup([pltpu.make_async_copy(
            ybuf_ref.at[slot, pl.ds(p * part, part)],
            yb_ref.at[pl.ds(j * MOE_BLK + p * part, part)], ysem_ref.at[slot])
            for p in range(BLOCK_COPY_PARTS)])

    def w_copies(e, ws):
        return [pltpu.make_async_copy(src.at[layer, e], dst.at[ws], wsem_ref.at[ws])
                for src, dst in ((wg_ref, wgs_ref), (wu_ref, wus_ref), (wd_ref, wds_ref))]

    def fill_copy(j):
        return pltpu.make_async_copy(zbuf_ref, yb_ref.at[pl.ds(j * MOE_BLK, MOE_BLK)], zsem_ref)

    for j0 in range(X_SLOTS - 1):
        x_copy(j0, j0).start()
    for c in w_copies(be_ref[0], 0):
        c.start()

    zbuf_ref[...] = jnp.zeros_like(zbuf_ref)

    def fill(j, carry):
        fill_copy(j).start()
        return carry

    lax.fori_loop(n_used, n_blocks, fill, 0)

    def block(j, ws):
        slot = lax.rem(j, Y_SLOTS)
        xslot = lax.rem(j, X_SLOTS)
        first = (j == 0) | (be_ref[j] != be_ref[jnp.maximum(j - 1, 0)])
        ws = jnp.where(first & (j > 0), 1 - ws, ws)

        @pl.when(first)
        def _():
            for c in w_copies(be_ref[j], ws):
                c.wait()
            for r0 in range(0, D_MODEL, W_PREP_ROWS):
                rows = slice(r0, r0 + W_PREP_ROWS)
                wgb_ref[rows, :] = wgs_ref[ws, rows, :].astype(BF16)
                wub_ref[rows, :] = wus_ref[ws, rows, :].astype(BF16)
            for r0 in range(0, D_EXPERT, W_PREP_ROWS):
                rows = slice(r0, r0 + W_PREP_ROWS)
                wdb_ref[rows, :] = wds_ref[ws, rows, :].astype(BF16)

            @pl.when(nxt_ref[j] >= 0)
            def _():
                for c in w_copies(nxt_ref[j], 1 - ws):
                    c.start()

        ahead = j + X_SLOTS - 1

        @pl.when(ahead < n_used)
        def _():
            x_copy(ahead, lax.rem(ahead, X_SLOTS)).start()

        x_copy(j, xslot).wait()
        x = jnp.concatenate(
            [c.astype(BF16) for c in _row_tiles_chunks(xbuf_ref.at[xslot], MOE_BLK)], axis=1)
        g = _dot(x, wgb_ref[...])
        u = _dot(x, wub_ref[...])
        h = (g * (1.0 / (1.0 + jnp.exp(-g)))) * u
        y = _dot(h.astype(BF16), wdb_ref[...])

        @pl.when(j >= Y_SLOTS)
        def _():
            y_copy(j - Y_SLOTS, slot).wait()

        _row_tiles_store(ybuf_ref.at[slot], y)
        y_copy(j, slot).start()
        return ws

    lax.fori_loop(0, n_used, block, jnp.int32(0))

    for back in range(Y_SLOTS, 0, -1):
        y_copy(n_used - back, lax.rem(n_used - back, Y_SLOTS)).wait()

    def fill_wait(j, carry):
        fill_copy(j).wait()
        return carry

    lax.fori_loop(n_used, n_blocks, fill_wait, 0)


def _experts(blk_exp, nxt_exp, n_used, xb, w_gate, w_up, w_down, layer):
    n_blocks = blk_exp.shape[0]
    any_spec = pl.BlockSpec(memory_space=pl.ANY)
    blk = (MOE_BLK, ROW_TILES, LANES)
    grid_spec = pltpu.PrefetchScalarGridSpec(
        num_scalar_prefetch=3,
        grid=(1,),
        in_specs=[any_spec, any_spec, any_spec, any_spec],
        out_specs=any_spec,
        scratch_shapes=[
            pltpu.VMEM((X_SLOTS,) + blk, F32), pltpu.VMEM((Y_SLOTS,) + blk, F32),
            pltpu.VMEM((2, D_MODEL, D_EXPERT), F32), pltpu.VMEM((2, D_MODEL, D_EXPERT), F32),
            pltpu.VMEM((2, D_EXPERT, D_MODEL), F32),
            pltpu.VMEM((D_MODEL, D_EXPERT), BF16), pltpu.VMEM((D_MODEL, D_EXPERT), BF16),
            pltpu.VMEM((D_EXPERT, D_MODEL), BF16),
            pltpu.SemaphoreType.DMA((X_SLOTS,)), pltpu.SemaphoreType.DMA((Y_SLOTS,)),
            pltpu.SemaphoreType.DMA((2,)),
            pltpu.VMEM(blk, F32), pltpu.SemaphoreType.DMA(()),
        ],
    )
    return pl.pallas_call(
        functools.partial(_expert_kernel, layer=layer, n_blocks=n_blocks),
        grid_spec=grid_spec,
        out_shape=jax.ShapeDtypeStruct((n_blocks * MOE_BLK, ROW_TILES, LANES), F32),
        compiler_params=pltpu.CompilerParams(
            dimension_semantics=("arbitrary",), vmem_limit_bytes=VMEM_LIMIT),
        name="experts",
    )(blk_exp, nxt_exp, n_used, xb, w_gate, w_up, w_down)


def _dispatch_tables(route_t, counts_rec, T):
    counts = counts_rec[0, N_GROUPS:N_GROUPS + N_EXPERTS].astype(jnp.int32)
    n_steps = (T * TOP_K) // MOE_BLK + N_EXPERTS
    nblk = (counts + MOE_BLK - 1) // MOE_BLK
    bend = jnp.cumsum(nblk)
    pstart = (bend - nblk) * MOE_BLK
    n_used = bend[-1]
    j = jnp.minimum(jnp.arange(n_steps, dtype=jnp.int32), n_used - 1)
    blk_exp = jnp.minimum(jnp.sum(j[:, None] >= bend[None, :], axis=1), N_EXPERTS - 1)
    n_rows = n_steps * MOE_BLK
    last_blk = jnp.where(counts > 0, (bend - 1) * MOE_BLK, -1)
    ids = jnp.arange(N_EXPERTS, dtype=jnp.int32)
    later = (ids[None, :] > ids[:, None]) & (nblk[None, :] > 0)
    nxt_of = jnp.min(jnp.where(later, ids[None, :], N_EXPERTS), axis=1)
    nxt_tab = jnp.where(nxt_of < N_EXPERTS, nxt_of, -1)
    nxt_exp = jnp.sum(jnp.where(blk_exp[:, None] == ids[None, :], nxt_tab[None, :], 0), axis=1)
    e = route_t[:, R_E:R_E + TOP_K, :].astype(jnp.int32)
    rank = route_t[:, R_RANK:R_RANK + TOP_K, :].astype(jnp.int32)
    seg = jnp.sum(jnp.where(e[..., None] == jnp.arange(N_EXPERTS), pstart, 0), axis=-1)
    dest = jnp.clip(seg + rank, 0, n_steps * MOE_BLK - 1)
    dest_tiles = dest.reshape(T // TM, 1, TOP_K * TM)
    return dict(dest_tiles=dest_tiles, fill_start=last_blk.astype(jnp.int32),
                blk_exp=blk_exp.astype(jnp.int32), nxt_exp=nxt_exp.astype(jnp.int32),
                n_used=n_used.reshape(1).astype(jnp.int32), n_rows=n_rows)


def _prep_layer(l, w_gk_up, b_gk, gla_norm, gmlp_norm, w_spatial, b_spatial, w_conv,
                w_router_group, b_router_group, w_router_expert, b_router_expert):
    wgk = jnp.concatenate(
        [w_gk_up[l], jnp.zeros((LANES - GLA_GATE_RANK, GLA_KDIM), F32)], axis=0).astype(BF16)
    wsp = w_spatial[l].transpose(1, 0, 2).reshape(GMLP_CHUNK, GMLP_HEADS * GMLP_CHUNK)
    bsp = jnp.repeat(b_spatial[l].T, GMLP_DH, axis=1)
    wconv = jnp.concatenate([w_conv[l], jnp.zeros((8 - CONV_K, CONV_WIDTH), F32)], axis=0)
    wr = jnp.concatenate(
        [w_router_group[l], w_router_expert[l],
         jnp.zeros((D_MODEL, ROUTER_COLS - N_GROUPS - N_EXPERTS), F32)], axis=1)
    wr_hi = wr.astype(BF16)
    wr_lo = (wr - wr_hi.astype(F32)).astype(BF16)
    br = jnp.concatenate(
        [b_router_group[l], b_router_expert[l],
         jnp.zeros((ROUTER_COLS - N_GROUPS - N_EXPERTS,), F32)])[None, :]
    return dict(
        wgk=wgk, bgk=b_gk[l][None, :], glan=gla_norm[l][None, :], gmn=gmlp_norm[l][None, :],
        wsp=wsp, bsp=bsp, wconv=wconv, wr_cat=jnp.concatenate([wr_hi, wr_lo], axis=1), br=br)


def kernel(x, attn_norm, w_in, w_gk_up, b_gk, gla_norm, gmlp_norm, w_spatial, b_spatial, w_conv, w_out, ffn_norm, w_router_group, b_router_group, w_router_expert, b_router_expert, w_gate, w_up, w_down, final_norm):
    B, S, D = x.shape
    T = B * S
    depth = w_in.shape[0]
    xr = x.reshape(T, D)
    w_in_t = jnp.swapaxes(w_in, 1, 2)
    moe = None
    for l in range(depth):
        p = _prep_layer(l, w_gk_up, b_gk, gla_norm, gmlp_norm, w_spatial, b_spatial, w_conv,
                        w_router_group, b_router_group, w_router_expert, b_router_expert)
        mix_params = (p["wgk"], p["bgk"], p["glan"], p["gmn"], p["wsp"], p["bsp"], p["wconv"])
        if moe is None:
            mixed = _front(xr, attn_norm[l][None, :], w_in_t, l, mix_params, B, S)
        else:
            xr, mixed = _front(moe["x2"], attn_norm[l][None, :], w_in_t, l, mix_params, B, S, moe)
        x2, h2, route, route_t, counts_rec = _out_router(
            mixed, xr, w_out, l, ffn_norm[l][None, :], p["wr_cat"], p["br"])
        moe = _dispatch_tables(route_t, counts_rec, T)
        xb = _dispatch(moe["fill_start"], moe["n_used"], moe["dest_tiles"], h2, moe["n_rows"])
        yb = _experts(moe["blk_exp"], moe["nxt_exp"], moe["n_used"], xb, w_gate, w_up, w_down, l)
        moe.update(x2=x2, route=route, yb=yb)
    out = _combine_final_norm(moe["dest_tiles"], moe["x2"], moe["route"], moe["yb"],
                              final_norm[None, :])
    return out.reshape(B, S, D)
```

```python
import functools

import jax
import jax.numpy as jnp
from jax import lax
from jax.experimental import pallas as pl
from jax.experimental.pallas import tpu as pltpu

F32 = jnp.float32
BF16 = jnp.bfloat16

D_MODEL = 1024
RMS_EPS = 1e-6
GLA_HEADS = 4
GLA_WIDTH = 512
GLA_DV = 128
GLA_DK = 64
GLA_KDIM = 256
GLA_GATE_RANK = 16
GLA_GATE_NORM = 16.0
GLA_CHUNK = 64
GMLP_HEADS = 4
GMLP_WIDTH = 256
GMLP_DH = 64
GMLP_CHUNK = 128
CONV_WIDTH = 256
CONV_K = 3
N_GROUPS = 4
EXPERTS_PER_GROUP = 8
N_EXPERTS = 32
TOP_K = 2
D_EXPERT = 256

LANES = 128
C_Q, C_K, C_V, C_G = 0, 256, 512, 1024
C_U, C_VG, C_X, C_BG, C_CG, C_GKL = 1536, 1792, 2048, 2304, 2560, 2816
D_PROJ = C_GKL + LANES
D_IN = C_GKL + GLA_GATE_RANK

TM = 256
TS_MIX = TM
MOE_BLK = 256
ROUTER_COLS = LANES
SUBLANES = 8
ROW_TILES = D_MODEL // LANES
assert ROW_TILES == SUBLANES
VMEM_LIMIT = 56 * 1024 * 1024
R_E, R_RANK, R_W = 0, 2, 4


def _dot(a, b):
    return jnp.dot(a, b, preferred_element_type=F32)


def _split_bf16(x):
    hi = x.astype(BF16)
    lo = (x - hi.astype(F32)).astype(BF16)
    return hi, lo


def _rms(x, gain):
    return x * lax.rsqrt(jnp.mean(x * x, axis=-1, keepdims=True) + RMS_EPS) * gain


W_PREP_ROWS = 128
PROJ_CHUNK = 256


def _stage_w_in(wt_ref, wb_ref):
    for c0 in range(0, C_GKL, LANES):
        src = c0 if c0 < C_U else c0 + GLA_GATE_RANK
        wb_ref[:, c0:c0 + LANES] = wt_ref[0, src:src + LANES, :].T.astype(BF16)
    low = jnp.concatenate([wt_ref[0, C_U:C_U + GLA_GATE_RANK, :],
                           jnp.zeros((LANES - GLA_GATE_RANK, D_MODEL), F32)], axis=0)
    wb_ref[:, C_GKL:D_PROJ] = low.T.astype(BF16)


def _row_gather_copy(yb_ref, buf_ref, sem_ref, slot, k, r, d):
    return pltpu.make_async_copy(yb_ref.at[d], buf_ref.at[slot, k, r], sem_ref.at[slot])


def _gather_start(dest_ref, yb_ref, buf_ref, sem_ref, slot, rows=range(TM)):
    for r in rows:
        for k in range(TOP_K):
            _row_gather_copy(yb_ref, buf_ref, sem_ref, slot, k, r,
                             dest_ref[0, 0, k * TM + r]).start(priority=k)


def _gather_wait(yb_ref, buf_ref, sem_ref, slot):
    for k in range(TOP_K):
        pltpu.make_async_copy(yb_ref.at[pl.ds(0, TM)], buf_ref.at[slot, k], sem_ref.at[slot]).wait()


def _combined_residual(dcur_ref, x_ref, route_ref, yb_ref, buf_ref, sem_ref):
    i = pl.program_id(0)
    slot = lax.rem(i, 2)

    @pl.when(i == 0)
    def _():
        _gather_start(dcur_ref, yb_ref, buf_ref, sem_ref, 0)

    _gather_wait(yb_ref, buf_ref, sem_ref, slot)
    w0 = route_ref[:, R_W:R_W + 1]
    w1 = route_ref[:, R_W + 1:R_W + 2]
    y0 = _row_tiles_chunks(buf_ref.at[slot, 0], TM)
    y1 = _row_tiles_chunks(buf_ref.at[slot, 1], TM)
    return jnp.concatenate(
        [x_ref[:, c * LANES:(c + 1) * LANES] + (w0 * y0[c] + w1 * y1[c]) for c in range(ROW_TILES)],
        axis=1)


def _prefetch_groups(n_groups):
    per = -(-TM // n_groups)
    return [range(g * per, min(TM, (g + 1) * per)) for g in range(n_groups)]


def _drain_last_prefetch(yb_ref, buf_ref, sem_ref):
    i = pl.program_id(0)

    @pl.when(i == pl.num_programs(0) - 1)
    def _():
        _gather_wait(yb_ref, buf_ref, sem_ref, 1 - lax.rem(i, 2))


def _combine_specs(n_tiles):
    smem_tile = lambda f: pl.BlockSpec((1, 1, TOP_K * TM), f, memory_space=pltpu.SMEM)
    return [
        smem_tile(lambda i: (i, 0, 0)),
        smem_tile(lambda i: (jnp.minimum(i + 1, n_tiles - 1), 0, 0)),
        pl.BlockSpec((TM, D_MODEL), lambda i: (i, 0)),
        pl.BlockSpec((TM, LANES), lambda i: (i, 0)),
        pl.BlockSpec(memory_space=pl.ANY),
    ]


_COMBINE_SCRATCH = [pltpu.VMEM((2, TOP_K, TM, ROW_TILES, LANES), F32),
                    pltpu.SemaphoreType.DMA((2,))]


def _combine_final_norm_kernel(dcur_ref, dnxt_ref, x_ref, route_ref, yb_ref, gain_ref,
                               o_ref, buf_ref, sem_ref):
    _gather_start(dnxt_ref, yb_ref, buf_ref, sem_ref, 1 - lax.rem(pl.program_id(0), 2))
    x = _combined_residual(dcur_ref, x_ref, route_ref, yb_ref, buf_ref, sem_ref)
    o_ref[...] = _rms(x, gain_ref[...])
    _drain_last_prefetch(yb_ref, buf_ref, sem_ref)


def _combine_final_norm(dest_tiles, x2, route, yb, gain):
    T = x2.shape[0]
    n_tiles = T // TM
    return pl.pallas_call(
        _combine_final_norm_kernel,
        grid=(n_tiles,),
        in_specs=_combine_specs(n_tiles) + [pl.BlockSpec((1, D_MODEL), lambda i: (0, 0))],
        out_specs=pl.BlockSpec((TM, D_MODEL), lambda i: (i, 0)),
        out_shape=jax.ShapeDtypeStruct((T, D_MODEL), F32),
        scratch_shapes=_COMBINE_SCRATCH,
        compiler_params=pltpu.CompilerParams(
            dimension_semantics=("arbitrary",), vmem_limit_bytes=VMEM_LIMIT),
        name="combine_final_norm",
    )(dest_tiles, dest_tiles, x2, route, yb, gain)


def _gelu_tanh(x):
    c = 0.7978845608028654
    return x * (0.5 * (1.0 + jnp.tanh(c * (x + 0.044715 * (x * x * x)))))


def _mixer_kernel(proj_ref, wgk_ref, bgk_ref, glan_ref, gmn_ref, wsp_ref, bsp_ref, wconv_ref,
                  out_ref, st_ref, hc_ref, lcat_ref, wm_ref, *, seq_start, first_step, between):
    TS = TS_MIX
    n_gla = TS // GLA_CHUNK
    n_gm = TS // GMLP_CHUNK

    @pl.when(seq_start)
    def _():
        st_ref[...] = jnp.zeros_like(st_ref)
        hc_ref[...] = jnp.zeros_like(hc_ref)

    @pl.when(first_step)
    def _():
        r = lax.broadcasted_iota(jnp.int32, (TS, TS), 0)
        c = lax.broadcasted_iota(jnp.int32, (TS, TS), 1)
        keep = ((r // GLA_CHUNK) == (c // GLA_CHUNK)) & (c <= r)
        lcat_ref[...] = jnp.where(keep, 1.0, 0.0).astype(BF16)
        t = lax.broadcasted_iota(jnp.int32, (GMLP_CHUNK, GMLP_HEADS * GMLP_CHUNK), 0)
        s = lax.broadcasted_iota(jnp.int32, (GMLP_CHUNK, GMLP_HEADS * GMLP_CHUNK), 1) % GMLP_CHUNK
        wm_ref[...] = jnp.where(s <= t, wsp_ref[...], 0.0).astype(BF16)

    lane256 = lax.broadcasted_iota(jnp.int32, (1, GLA_KDIM), 1)

    q = proj_ref[:, C_Q:C_Q + GLA_KDIM].astype(F32)
    k = proj_ref[:, C_K:C_K + GLA_KDIM].astype(F32)
    v_b = proj_ref[:, C_V:C_V + GLA_WIDTH]
    z = _dot(proj_ref[:, C_GKL:C_GKL + LANES], wgk_ref[...]) + bgk_ref[...]
    gk = (jnp.minimum(z, 0.0) - jnp.log1p(jnp.exp(-jnp.abs(z)))) * (1.0 / GLA_GATE_NORM)
    gk_hi, gk_lo = _split_bf16(gk)
    cs = _dot(lcat_ref[...], jnp.concatenate([gk_hi, gk_lo], axis=1))
    b = cs[:, :GLA_KDIM] + cs[:, GLA_KDIM:]
    b_last = [b[(c + 1) * GLA_CHUNK - 1:(c + 1) * GLA_CHUNK, :] for c in range(n_gla)]
    bl = jnp.concatenate(
        [jnp.broadcast_to(t, (GLA_CHUNK, GLA_KDIM)) for t in b_last], axis=0)
    q_dec = (q * (GLA_DK ** -0.5)) * jnp.exp(b)
    k_inv = (k * jnp.exp(-b)).astype(BF16)
    k_dec = (k * jnp.exp(bl - b)).astype(BF16)
    q_dec_b = q_dec.astype(BF16)

    zero_b = jnp.zeros_like(q_dec_b)
    q_stack = jnp.concatenate(
        [jnp.where((lane256 // GLA_DK) == h, q_dec_b, zero_b) for h in range(GLA_HEADS)], axis=0)
    scores = lax.dot_general(q_stack, k_inv, (((1,), (1,)), ((), ())),
                             preferred_element_type=F32)
    rt = lax.broadcasted_iota(jnp.int32, (TS, TS), 0)
    ct = lax.broadcasted_iota(jnp.int32, (TS, TS), 1)
    causal = ((rt // GLA_CHUNK) == (ct // GLA_CHUNK)) & (ct <= rt)
    o_heads = []
    for h in range(GLA_HEADS):
        p_h = jnp.where(causal, scores[h * TS:(h + 1) * TS, :], 0.0).astype(BF16)
        o_heads.append(_dot(p_h, v_b[:, h * GLA_DV:(h + 1) * GLA_DV]))

    sr = lax.broadcasted_iota(jnp.int32, (GLA_WIDTH, GLA_KDIM), 0) // GLA_DV
    sc = lax.broadcasted_iota(jnp.int32, (GLA_WIDTH, GLA_KDIM), 1) // GLA_DK
    bd_mask = sr == sc
    o_inter = []
    for c in range(n_gla):
        rows = slice(c * GLA_CHUNK, (c + 1) * GLA_CHUNK)
        st = st_ref[...]
        o_inter.append(lax.dot_general(q_dec_b[rows], st.astype(BF16), (((1,), (1,)), ((), ())),
                                       preferred_element_type=F32))
        upd = lax.dot_general(v_b[rows], k_dec[rows], (((0,), (0,)), ((), ())),
                              preferred_element_type=F32)
        decay = jnp.exp(b_last[c])
        st_ref[...] = st * decay + jnp.where(bd_mask, upd, 0.0)
    o_inter = jnp.concatenate(o_inter, axis=0)

    for h in range(GLA_HEADS):
        cols = slice(h * GLA_DV, (h + 1) * GLA_DV)
        o = o_heads[h] + o_inter[:, cols]
        o = o * lax.rsqrt(jnp.mean(o * o, axis=-1, keepdims=True) + RMS_EPS) * glan_ref[...]
        g = proj_ref[:, C_G + h * GLA_DV:C_G + (h + 1) * GLA_DV].astype(F32)
        out_ref[:, cols] = (o * (g * (1.0 / (1.0 + jnp.exp(-g))))).astype(out_ref.dtype)

    between("gla_done")
    u = _gelu_tanh(proj_ref[:, C_U:C_U + GMLP_WIDTH].astype(F32))
    vg = _gelu_tanh(proj_ref[:, C_VG:C_VG + GMLP_WIDTH].astype(F32))
    hr = lax.broadcasted_iota(jnp.int32, (GMLP_WIDTH, GMLP_WIDTH), 0) // GMLP_DH
    hcn = lax.broadcasted_iota(jnp.int32, (GMLP_WIDTH, GMLP_WIDTH), 1) // GMLP_DH
    head_mean = jnp.where(hr == hcn, 1.0 / GMLP_DH, 0.0).astype(BF16)
    sq_hi, sq_lo = _split_bf16(vg * vg)
    ms = _dot(sq_hi, head_mean) + _dot(sq_lo, head_mean)
    v32 = vg * lax.rsqrt(ms + RMS_EPS) * gmn_ref[...]
    for c in range(n_gm):
        rows = slice(c * GMLP_CHUNK, (c + 1) * GMLP_CHUNK)
        vc = v32[rows].astype(BF16)
        zc = jnp.zeros_like(vc)
        rhs = jnp.concatenate(
            [jnp.where((lane256 // GMLP_DH) == h, vc, zc) for h in range(GMLP_HEADS)], axis=0)
        mixed = _dot(wm_ref[...], rhs) + bsp_ref[...]
        out_ref[rows, GLA_WIDTH:GLA_WIDTH + GMLP_WIDTH] = (u[rows] * mixed).astype(out_ref.dtype)

    between("gmlp_done")
    hcv = (proj_ref[:, C_CG:C_CG + CONV_WIDTH].astype(F32)
           * proj_ref[:, C_X:C_X + CONV_WIDTH].astype(F32))
    hc_ref[8:8 + TS, :] = hcv
    y = (wconv_ref[2:3, :] * hcv + wconv_ref[1:2, :] * hc_ref[7:7 + TS, :]
         + wconv_ref[0:1, :] * hc_ref[6:6 + TS, :])
    out_ref[:, GLA_WIDTH + GMLP_WIDTH:] = (
        proj_ref[:, C_BG:C_BG + CONV_WIDTH].astype(F32) * y).astype(out_ref.dtype)
    hc_ref[0:8, :] = hc_ref[TS:TS + 8, :]


_MIXER_SCRATCH = [
    pltpu.VMEM((GLA_WIDTH, GLA_KDIM), F32),
    pltpu.VMEM((TS_MIX + 8, CONV_WIDTH), F32),
    pltpu.VMEM((TS_MIX, TS_MIX), BF16),
    pltpu.VMEM((GMLP_CHUNK, GMLP_HEADS * GMLP_CHUNK), BF16),
]


N_MIX_PARAMS = 7
PROJ_CHUNKS_AT = {"gla_done": 4, "gmlp_done": 4}


def _front_kernel(*refs, tiles_per_seq, combine):
    refs = list(refs)
    if combine:
        dcur_ref, dnxt_ref, x_ref, route_ref, yb_ref = refs[:5]
        del refs[:5]
    else:
        x_ref = refs.pop(0)
    gain_ref, wt_ref = refs[:2]
    mix_refs = refs[2:2 + N_MIX_PARAMS]
    del refs[:2 + N_MIX_PARAMS]
    if combine:
        xo_ref, out_ref, buf_ref, sem_ref = refs[:4]
        del refs[:4]
    else:
        out_ref = refs.pop(0)
    st_ref, hc_ref, lcat_ref, wm_ref, wb_ref, pcur_ref, pnext_ref = refs
    s = pl.program_id(0)

    @pl.when(s == 0)
    def _():
        _stage_w_in(wt_ref, wb_ref)
        pcur_ref[...] = jnp.zeros_like(pcur_ref)

    col_chunks = [(c0, min(c0 + PROJ_CHUNK, D_PROJ)) for c0 in range(0, D_PROJ, PROJ_CHUNK)]
    work = list(zip(_prefetch_groups(len(col_chunks)), col_chunks))

    def project(h, n):
        for _ in range(min(n, len(work))):
            rows, (c0, c1) = work.pop(0)
            if combine:
                _gather_start(dnxt_ref, yb_ref, buf_ref, sem_ref, 1 - lax.rem(s, 2), rows)
            pnext_ref[:, c0:c1] = _dot(h, wb_ref[:, c0:c1]).astype(BF16)

    mixers = functools.partial(
        _mixer_kernel, pcur_ref, *mix_refs, out_ref, st_ref, hc_ref, lcat_ref, wm_ref,
        seq_start=lax.rem(jnp.maximum(s - 1, 0), tiles_per_seq) == 0, first_step=s == 0)
    if combine:
        x = _combined_residual(dcur_ref, x_ref, route_ref, yb_ref, buf_ref, sem_ref)
        xo_ref[...] = x
        h = _rms(x, gain_ref[...]).astype(BF16)
        mixers(between=lambda site: project(h, PROJ_CHUNKS_AT[site]))
        project(h, len(work))
        _drain_last_prefetch(yb_ref, buf_ref, sem_ref)
    else:
        mixers(between=lambda site: None)
        project(_rms(x_ref[...], gain_ref[...]).astype(BF16), len(work))
    pcur_ref[...] = pnext_ref[...]


def _front(x, gain, w_in_t, layer, mix_params, batch, seq, moe=None):
    n_seq = seq // TS_MIX
    n = batch * n_seq
    T = batch * seq
    cur = lambda s: jnp.minimum(s, n - 1)
    full = lambda shape: pl.BlockSpec(shape, lambda s: (0,) * len(shape))
    row = lambda w, f: pl.BlockSpec((TM, w), lambda s: (f(s), 0))
    in_specs, args = [row(D_MODEL, cur)], [x]
    out_specs = [row(D_MODEL, lambda s: jnp.maximum(s - 1, 0))]
    out_shape = [jax.ShapeDtypeStruct((T, D_MODEL), BF16)]
    scratch = list(_MIXER_SCRATCH)
    if moe is not None:
        smem_tile = lambda f: pl.BlockSpec((1, 1, TOP_K * TM), lambda s: (f(s), 0, 0),
                                           memory_space=pltpu.SMEM)
        in_specs = [smem_tile(cur), smem_tile(lambda s: jnp.minimum(s + 1, n - 1))] + in_specs + [
            row(LANES, cur), pl.BlockSpec(memory_space=pl.ANY)]
        args = [moe["dest_tiles"], moe["dest_tiles"]] + args + [moe["route"], moe["yb"]]
        out_specs = [row(D_MODEL, cur)] + out_specs
        out_shape = [jax.ShapeDtypeStruct((T, D_MODEL), F32)] + out_shape
        scratch = _COMBINE_SCRATCH + scratch
    in_specs += [
        full((1, D_MODEL)),
        pl.BlockSpec((1, D_IN, D_MODEL), lambda s: (layer, 0, 0), pipeline_mode=pl.Buffered(1)),
        full((LANES, GLA_KDIM)), full((1, GLA_KDIM)), full((1, GLA_DV)), full((1, GMLP_WIDTH)),
        full((GMLP_CHUNK, GMLP_HEADS * GMLP_CHUNK)), full((GMLP_CHUNK, GMLP_WIDTH)),
        full((8, CONV_WIDTH)),
    ]
    scratch += [pltpu.VMEM((D_MODEL, D_PROJ), BF16), pltpu.VMEM((TS_MIX, D_PROJ), BF16),
                pltpu.VMEM((TS_MIX, D_PROJ), BF16)]
    return pl.pallas_call(
        functools.partial(_front_kernel, tiles_per_seq=n_seq, combine=moe is not None),
        grid=(n + 1,),
        in_specs=in_specs,
        out_specs=out_specs if moe is not None else out_specs[0],
        out_shape=out_shape if moe is not None else out_shape[0],
        scratch_shapes=scratch,
        compiler_params=pltpu.CompilerParams(
            dimension_semantics=("arbitrary",), vmem_limit_bytes=VMEM_LIMIT),
        name="front",
    )(*args, gain, w_in_t, *mix_params)


def _row_tiles_store(tiles_ref, x):
    rows = x.shape[0]
    flat = tiles_ref.reshape(rows * ROW_TILES, LANES)
    for c in range(ROW_TILES):
        flat[pl.ds(c, rows, stride=ROW_TILES), :] = x[:, c * LANES:(c + 1) * LANES]


def _row_tiles_chunks(tiles_ref, rows):
    flat = tiles_ref.reshape(rows * ROW_TILES, LANES)
    return [flat[pl.ds(c, rows, stride=ROW_TILES), :] for c in range(ROW_TILES)]


def _out_router_kernel(mix_ref, x_ref, wo_ref, gain_ref, wrc_ref, br_ref,
                       x2_ref, h2_ref, route_ref, route_t_ref, cnt_ref, tri_ref, wob_ref, lg_ref):
    i = pl.program_id(0)

    @pl.when(i == 0)
    def _():
        cnt_ref[...] = jnp.zeros_like(cnt_ref)
        lg_ref[...] = jnp.zeros_like(lg_ref)
        r = lax.broadcasted_iota(jnp.int32, (TM, TM), 0)
        c = lax.broadcasted_iota(jnp.int32, (TM, TM), 1)
        tri_ref[...] = jnp.where(c < r, 1.0, 0.0).astype(BF16)
        for r0 in range(0, D_MODEL, W_PREP_ROWS):
            wob_ref[r0:r0 + W_PREP_ROWS, :] = wo_ref[0, r0:r0 + W_PREP_ROWS, :].astype(BF16)

    lg = lg_ref[...]
    x2 = x_ref[...] + _dot(mix_ref[...], wob_ref[...])
    x2_ref[...] = x2
    h = _rms(x2, gain_ref[...])
    h_hi, h_lo = _split_bf16(h)
    h2_ref[...] = h_hi
    hh_hl = _dot(h_hi, wrc_ref[...])
    lg_ref[...] = (hh_hl[:, :ROUTER_COLS] + hh_hl[:, ROUTER_COLS:]
                   + _dot(h_lo, wrc_ref[:, :ROUTER_COLS]) + br_ref[...])

    lane = lax.broadcasted_iota(jnp.int32, (TM, LANES), 1).astype(F32)
    neg = -jnp.inf
    is_g = lane < N_GROUPS
    gl = jnp.where(is_g, lg, neg)
    gmax = jnp.max(gl, axis=1, keepdims=True)
    g_top = jnp.min(jnp.where(gl == gmax, lane, float(LANES)), axis=1, keepdims=True)
    g_w = 1.0 / jnp.sum(jnp.where(is_g, jnp.exp(lg - gmax), 0.0), axis=1, keepdims=True)
    first = N_GROUPS + EXPERTS_PER_GROUP * g_top
    el = jnp.where((lane >= first) & (lane < first + EXPERTS_PER_GROUP), lg, neg)
    m1 = jnp.max(el, axis=1, keepdims=True)
    i1 = jnp.min(jnp.where(el == m1, lane, float(LANES)), axis=1, keepdims=True)
    el2 = jnp.where(lane == i1, neg, el)
    m2 = jnp.max(el2, axis=1, keepdims=True)
    i2 = jnp.min(jnp.where(el2 == m2, lane, float(LANES)), axis=1, keepdims=True)
    ratio = jnp.exp(m2 - m1)
    w1 = g_w / (1.0 + ratio)
    w2 = w1 * ratio

    oh1 = jnp.where(lane == i1, 1.0, 0.0)
    oh2 = jnp.where(lane == i2, 1.0, 0.0)
    oh = jnp.where(i > 0, oh1 + oh2, 0.0)
    before = _dot(tri_ref[...], oh.astype(BF16)) + cnt_ref[0:1, :]
    rank1 = jnp.sum(oh1 * before, axis=1, keepdims=True)
    rank2 = jnp.sum(oh2 * before, axis=1, keepdims=True)
    cnt_ref[...] = cnt_ref[...] + jnp.sum(oh, axis=0, keepdims=True)

    rec = jnp.zeros((TM, LANES), F32)
    for col, val in ((R_E, i1 - N_GROUPS), (R_E + 1, i2 - N_GROUPS), (R_RANK, rank1),
                     (R_RANK + 1, rank2), (R_W, w1), (R_W + 1, w2)):
        rec = jnp.where(lane == col, val, rec)
    route_ref[...] = rec
    route_t_ref[0] = rec.T[0:SUBLANES, :]


def _out_router(mixed, x, w_out, layer, gain, wr_cat, br):
    T = x.shape[0]
    n = T // TM
    row = lambda w: pl.BlockSpec((TM, w), lambda i: (jnp.minimum(i, n - 1), 0))
    lag = lambda i: jnp.maximum(i - 1, 0)
    full = lambda shape: pl.BlockSpec(shape, lambda i: (0, 0))
    wo_spec = pl.BlockSpec((1, D_MODEL, D_MODEL), lambda i: (layer, 0, 0),
                           pipeline_mode=pl.Buffered(1))
    return pl.pallas_call(
        _out_router_kernel,
        grid=(n + 1,),
        in_specs=[row(D_MODEL), row(D_MODEL), wo_spec, full((1, D_MODEL)),
                  full((D_MODEL, 2 * ROUTER_COLS)), full((1, ROUTER_COLS))],
        out_specs=[row(D_MODEL), row(D_MODEL),
                   pl.BlockSpec((TM, LANES), lambda i: (lag(i), 0)),
                   pl.BlockSpec((1, SUBLANES, TM), lambda i: (lag(i), 0, 0)), full((8, LANES))],
        out_shape=[jax.ShapeDtypeStruct((T, D_MODEL), F32),
                   jax.ShapeDtypeStruct((T, D_MODEL), BF16),
                   jax.ShapeDtypeStruct((T, LANES), F32),
                   jax.ShapeDtypeStruct((T // TM, SUBLANES, TM), F32),
                   jax.ShapeDtypeStruct((8, LANES), F32)],
        scratch_shapes=[pltpu.VMEM((TM, TM), BF16), pltpu.VMEM((D_MODEL, D_MODEL), BF16),
                        pltpu.VMEM((TM, ROUTER_COLS), F32)],
        compiler_params=pltpu.CompilerParams(
            dimension_semantics=("arbitrary",), vmem_limit_bytes=VMEM_LIMIT),
        name="out_router",
    )(mixed, x, w_out, gain, wr_cat, br)


def _dispatch_kernel(fill_ref, nu_ref, dest_ref, h_ref, xb_ref, zero_ref, sem_ref, zsem_ref,
                     stage_ref):
    i = pl.program_id(0)
    par = lax.rem(i, 2)
    last = i == pl.num_programs(0) - 1
    n_blocks = xb_ref.shape[0] // MOE_BLK
    spare_fills = [(j >= nu_ref[0], pltpu.make_async_copy(
        zero_ref, xb_ref.at[pl.ds(j * MOE_BLK, MOE_BLK)], zsem_ref.at[1]))
        for j in range(n_blocks - N_EXPERTS, n_blocks)]

    @pl.when(i == 0)
    def _():
        zero_ref[...] = jnp.zeros_like(zero_ref)
        fills = [(fill_ref[e] >= 0, pltpu.make_async_copy(
            zero_ref, xb_ref.at[pl.ds(pl.multiple_of(jnp.maximum(fill_ref[e], 0), MOE_BLK), MOE_BLK)],
            zsem_ref.at[0])) for e in range(N_EXPERTS)]
        for cond, f in fills + spare_fills:
            pl.when(cond)(f.start)
        for cond, f in fills:
            pl.when(cond)(f.wait)

    _row_tiles_store(stage_ref.at[par], h_ref[...].astype(F32))
    for r in range(TM):
        for k in range(TOP_K):
            pltpu.make_async_copy(stage_ref.at[par, r], xb_ref.at[dest_ref[0, 0, k * TM + r]],
                                  sem_ref.at[par]).start(priority=k)

    def wait_tile(p):
        for _ in range(TOP_K):
            pltpu.make_async_copy(stage_ref.at[p], xb_ref.at[pl.ds(0, TM)], sem_ref.at[p]).wait()

    pl.when(i > 0)(lambda: wait_tile(1 - par))
    @pl.when(last)
    def _():
        wait_tile(par)
        for cond, f in spare_fills:
            pl.when(cond)(f.wait)


def _dispatch(fill_start, n_used, dest_tiles, h2, n_rows):
    T = h2.shape[0]
    grid_spec = pltpu.PrefetchScalarGridSpec(
        num_scalar_prefetch=2,
        grid=(T // TM,),
        in_specs=[
            pl.BlockSpec((1, 1, TOP_K * TM), lambda i, fs, nu: (i, 0, 0), memory_space=pltpu.SMEM),
            pl.BlockSpec((TM, D_MODEL), lambda i, fs, nu: (i, 0)),
        ],
        out_specs=pl.BlockSpec(memory_space=pl.ANY),
        scratch_shapes=[pltpu.VMEM((MOE_BLK, ROW_TILES, LANES), F32),
                        pltpu.SemaphoreType.DMA((2,)), pltpu.SemaphoreType.DMA((2,)),
                        pltpu.VMEM((2, TM, ROW_TILES, LANES), F32)],
    )
    return pl.pallas_call(
        _dispatch_kernel,
        grid_spec=grid_spec,
        out_shape=jax.ShapeDtypeStruct((n_rows, ROW_TILES, LANES), F32),
        compiler_params=pltpu.CompilerParams(dimension_semantics=("arbitrary",)),
        name="dispatch",
    )(fill_start, n_used, dest_tiles, h2)


BLOCK_COPY_PARTS = 4
X_SLOTS = 4
Y_SLOTS = 3


class _CopyGroup:
    def __init__(self, copies):
        self.copies = copies

    def start(self):
        for n, c in enumerate(self.copies):
            c.start(priority=n % 2)

    def wait(self):
        for c in self.copies:
            c.wait()


def _expert_kernel(be_ref, nxt_ref, nu_ref, xb_ref, wg_ref, wu_ref, wd_ref, yb_ref,
                   xbuf_ref, ybuf_ref, wgs_ref, wus_ref, wds_ref, wgb_ref, wub_ref, wdb_ref,
                   xsem_ref, ysem_ref, wsem_ref, zbuf_ref, zsem_ref, *, layer, n_blocks):
    n_used = nu_ref[0]

    part = MOE_BLK // BLOCK_COPY_PARTS

    def x_copy(j, slot):
        return _CopyGroup([pltpu.make_async_copy(
            xb_ref.at[pl.ds(j * MOE_BLK + p * part, part)],
            xbuf_ref.at[slot, pl.ds(p * part, part)], xsem_ref.at[slot])
            for p in range(BLOCK_COPY_PARTS)])

    def y_copy(j, slot):
        return _CopyGroup([pltpu.make_async_copy(
            ybuf_ref.at[slot, pl.ds(p * part, part)],
            yb_ref.at[pl.ds(j * MOE_BLK + p * part, part)], ysem_ref.at[slot])
            for p in range(BLOCK_COPY_PARTS)])

    def w_copies(e, ws):
        return [pltpu.make_async_copy(src.at[layer, e], dst.at[ws], wsem_ref.at[ws])
                for src, dst in ((wg_ref, wgs_ref), (wu_ref, wus_ref), (wd_ref, wds_ref))]

    def fill_copy(j):
        return pltpu.make_async_copy(zbuf_ref, yb_ref.at[pl.ds(j * MOE_BLK, MOE_BLK)], zsem_ref)

    for j0 in range(X_SLOTS - 1):
        x_copy(j0, j0).start()
    for c in w_copies(be_ref[0], 0):
        c.start()

    zbuf_ref[...] = jnp.zeros_like(zbuf_ref)

    def fill(j, carry):
        fill_copy(j).start()
        return carry

    lax.fori_loop(n_used, n_blocks, fill, 0)

    def block(j, ws):
        slot = lax.rem(j, Y_SLOTS)
        xslot = lax.rem(j, X_SLOTS)
        first = (j == 0) | (be_ref[j] != be_ref[jnp.maximum(j - 1, 0)])
        ws = jnp.where(first & (j > 0), 1 - ws, ws)

        @pl.when(first)
        def _():
            for c in w_copies(be_ref[j], ws):
                c.wait()
            for r0 in range(0, D_MODEL, W_PREP_ROWS):
                rows = slice(r0, r0 + W_PREP_ROWS)
                wgb_ref[rows, :] = wgs_ref[ws, rows, :].astype(BF16)
                wub_ref[rows, :] = wus_ref[ws, rows, :].astype(BF16)
            for r0 in range(0, D_EXPERT, W_PREP_ROWS):
                rows = slice(r0, r0 + W_PREP_ROWS)
                wdb_ref[rows, :] = wds_ref[ws, rows, :].astype(BF16)

            @pl.when(nxt_ref[j] >= 0)
            def _():
                for c in w_copies(nxt_ref[j], 1 - ws):
                    c.start()

        ahead = j + X_SLOTS - 1

        @pl.when(ahead < n_used)
        def _():
            x_copy(ahead, lax.rem(ahead, X_SLOTS)).start()

        x_copy(j, xslot).wait()
        x = jnp.concatenate(
            [c.astype(BF16) for c in _row_tiles_chunks(xbuf_ref.at[xslot], MOE_BLK)], axis=1)
        g = _dot(x, wgb_ref[...])
        u = _dot(x, wub_ref[...])
        h = (g * (1.0 / (1.0 + jnp.exp(-g)))) * u
        y = _dot(h.astype(BF16), wdb_ref[...])

        @pl.when(j >= Y_SLOTS)
        def _():
            y_copy(j - Y_SLOTS, slot).wait()

        _row_tiles_store(ybuf_ref.at[slot], y)
        y_copy(j, slot).start()
        return ws

    lax.fori_loop(0, n_used, block, jnp.int32(0))

    for back in range(Y_SLOTS, 0, -1):
        y_copy(n_used - back, lax.rem(n_used - back, Y_SLOTS)).wait()

    def fill_wait(j, carry):
        fill_copy(j).wait()
        return carry

    lax.fori_loop(n_used, n_blocks, fill_wait, 0)


def _experts(blk_exp, nxt_exp, n_used, xb, w_gate, w_up, w_down, layer):
    n_blocks = blk_exp.shape[0]
    any_spec = pl.BlockSpec(memory_space=pl.ANY)
    blk = (MOE_BLK, ROW_TILES, LANES)
    grid_spec = pltpu.PrefetchScalarGridSpec(
        num_scalar_prefetch=3,
        grid=(1,),
        in_specs=[any_spec, any_spec, any_spec, any_spec],
        out_specs=any_spec,
        scratch_shapes=[
            pltpu.VMEM((X_SLOTS,) + blk, F32), pltpu.VMEM((Y_SLOTS,) + blk, F32),
            pltpu.VMEM((2, D_MODEL, D_EXPERT), F32), pltpu.VMEM((2, D_MODEL, D_EXPERT), F32),
            pltpu.VMEM((2, D_EXPERT, D_MODEL), F32),
            pltpu.VMEM((D_MODEL, D_EXPERT), BF16), pltpu.VMEM((D_MODEL, D_EXPERT), BF16),
            pltpu.VMEM((D_EXPERT, D_MODEL), BF16),
            pltpu.SemaphoreType.DMA((X_SLOTS,)), pltpu.SemaphoreType.DMA((Y_SLOTS,)),
            pltpu.SemaphoreType.DMA((2,)),
            pltpu.VMEM(blk, F32), pltpu.SemaphoreType.DMA(()),
        ],
    )
    return pl.pallas_call(
        functools.partial(_expert_kernel, layer=layer, n_blocks=n_blocks),
        grid_spec=grid_spec,
        out_shape=jax.ShapeDtypeStruct((n_blocks * MOE_BLK, ROW_TILES, LANES), F32),
        compiler_params=pltpu.CompilerParams(
            dimension_semantics=("arbitrary",), vmem_limit_bytes=VMEM_LIMIT),
        name="experts",
    )(blk_exp, nxt_exp, n_used, xb, w_gate, w_up, w_down)


def _dispatch_tables(route_t, counts_rec, T):
    counts = counts_rec[0, N_GROUPS:N_GROUPS + N_EXPERTS].astype(jnp.int32)
    n_steps = (T * TOP_K) // MOE_BLK + N_EXPERTS
    nblk = (counts + MOE_BLK - 1) // MOE_BLK
    bend = jnp.cumsum(nblk)
    pstart = (bend - nblk) * MOE_BLK
    n_used = bend[-1]
    j = jnp.minimum(jnp.arange(n_steps, dtype=jnp.int32), n_used - 1)
    blk_exp = jnp.minimum(jnp.sum(j[:, None] >= bend[None, :], axis=1), N_EXPERTS - 1)
    n_rows = n_steps * MOE_BLK
    last_blk = jnp.where(counts > 0, (bend - 1) * MOE_BLK, -1)
    ids = jnp.arange(N_EXPERTS, dtype=jnp.int32)
    later = (ids[None, :] > ids[:, None]) & (nblk[None, :] > 0)
    nxt_of = jnp.min(jnp.where(later, ids[None, :], N_EXPERTS), axis=1)
    nxt_tab = jnp.where(nxt_of < N_EXPERTS, nxt_of, -1)
    nxt_exp = jnp.sum(jnp.where(blk_exp[:, None] == ids[None, :], nxt_tab[None, :], 0), axis=1)
    e = route_t[:, R_E:R_E + TOP_K, :].astype(jnp.int32)
    rank = route_t[:, R_RANK:R_RANK + TOP_K, :].astype(jnp.int32)
    seg = jnp.sum(jnp.where(e[..., None] == jnp.arange(N_EXPERTS), pstart, 0), axis=-1)
    dest = jnp.clip(seg + rank, 0, n_steps * MOE_BLK - 1)
    dest_tiles = dest.reshape(T // TM, 1, TOP_K * TM)
    return dict(dest_tiles=dest_tiles, fill_start=last_blk.astype(jnp.int32),
                blk_exp=blk_exp.astype(jnp.int32), nxt_exp=nxt_exp.astype(jnp.int32),
                n_used=n_used.reshape(1).astype(jnp.int32), n_rows=n_rows)


def _prep_layer(l, w_gk_up, b_gk, gla_norm, gmlp_norm, w_spatial, b_spatial, w_conv,
                w_router_group, b_router_group, w_router_expert, b_router_expert):
    wgk = jnp.concatenate(
        [w_gk_up[l], jnp.zeros((LANES - GLA_GATE_RANK, GLA_KDIM), F32)], axis=0).astype(BF16)
    wsp = w_spatial[l].transpose(1, 0, 2).reshape(GMLP_CHUNK, GMLP_HEADS * GMLP_CHUNK)
    bsp = jnp.repeat(b_spatial[l].T, GMLP_DH, axis=1)
    wconv = jnp.concatenate([w_conv[l], jnp.zeros((8 - CONV_K, CONV_WIDTH), F32)], axis=0)
    wr = jnp.concatenate(
        [w_router_group[l], w_router_expert[l],
         jnp.zeros((D_MODEL, ROUTER_COLS - N_GROUPS - N_EXPERTS), F32)], axis=1)
    wr_hi = wr.astype(BF16)
    wr_lo = (wr - wr_hi.astype(F32)).astype(BF16)
    br = jnp.concatenate(
        [b_router_group[l], b_router_expert[l],
         jnp.zeros((ROUTER_COLS - N_GROUPS - N_EXPERTS,), F32)])[None, :]
    return dict(
        wgk=wgk, bgk=b_gk[l][None, :], glan=gla_norm[l][None, :], gmn=gmlp_norm[l][None, :],
        wsp=wsp, bsp=bsp, wconv=wconv, wr_cat=jnp.concatenate([wr_hi, wr_lo], axis=1), br=br)


def kernel(x, attn_norm, w_in, w_gk_up, b_gk, gla_norm, gmlp_norm, w_spatial, b_spatial, w_conv, w_out, ffn_norm, w_router_group, b_router_group, w_router_expert, b_router_expert, w_gate, w_up, w_down, final_norm):
    B, S, D = x.shape
    T = B * S
    depth = w_in.shape[0]
    xr = x.reshape(T, D)
    w_in_t = jnp.swapaxes(w_in, 1, 2)
    moe = None
    for l in range(depth):
        p = _prep_layer(l, w_gk_up, b_gk, gla_norm, gmlp_norm, w_spatial, b_spatial, w_conv,
                        w_router_group, b_router_group, w_router_expert, b_router_expert)
        mix_params = (p["wgk"], p["bgk"], p["glan"], p["gmn"], p["wsp"], p["bsp"], p["wconv"])
        if moe is None:
            mixed = _front(xr, attn_norm[l][None, :], w_in_t, l, mix_params, B, S)
        else:
            xr, mixed = _front(moe["x2"], attn_norm[l][None, :], w_in_t, l, mix_params, B, S, moe)
        x2, h2, route, route_t, counts_rec = _out_router(
            mixed, xr, w_out, l, ffn_norm[l][None, :], p["wr_cat"], p["br"])
        moe = _dispatch_tables(route_t, counts_rec, T)
        xb = _dispatch(moe["fill_start"], moe["n_used"], moe["dest_tiles"], h2, moe["n_rows"])
        yb = _experts(moe["blk_exp"], moe["nxt_exp"], moe["n_used"], xb, w_gate, w_up, w_down, l)
        moe.update(x2=x2, route=route, yb=yb)
    out = _combine_final_norm(moe["dest_tiles"], moe["x2"], moe["route"], moe["yb"],
                              final_norm[None, :])
    return out.reshape(B, S, D)
```

```python
import functools

import jax
import jax.numpy as jnp
from jax import lax
from jax.experimental import pallas as pl
from jax.experimental.pallas import tpu as pltpu

F32 = jnp.float32
BF16 = jnp.bfloat16

D_MODEL = 1024
RMS_EPS = 1e-6
GLA_HEADS = 4
GLA_WIDTH = 512
GLA_DV = 128
GLA_DK = 64
GLA_KDIM = 256
GLA_GATE_RANK = 16
GLA_GATE_NORM = 16.0
GLA_CHUNK = 64
GMLP_HEADS = 4
GMLP_WIDTH = 256
GMLP_DH = 64
GMLP_CHUNK = 128
CONV_WIDTH = 256
CONV_K = 3
N_GROUPS = 4
EXPERTS_PER_GROUP = 8
N_EXPERTS = 32
TOP_K = 2
D_EXPERT = 256

LANES = 128
C_Q, C_K, C_V, C_G = 0, 256, 512, 1024
C_U, C_VG, C_X, C_BG, C_CG, C_GKL = 1536, 1792, 2048, 2304, 2560, 2816
D_PROJ = C_GKL + LANES
D_IN = C_GKL + GLA_GATE_RANK

TM = 256
TS_MIX = TM
MOE_BLK = 256
ROUTER_COLS = LANES
SUBLANES = 8
ROW_TILES = D_MODEL // LANES
assert ROW_TILES == SUBLANES
VMEM_LIMIT = 56 * 1024 * 1024
R_E, R_RANK, R_W = 0, 2, 4


def _dot(a, b):
    return jnp.dot(a, b, preferred_element_type=F32)


def _split_bf16(x):
    hi = x.astype(BF16)
    lo = (x - hi.astype(F32)).astype(BF16)
    return hi, lo


def _rms(x, gain):
    return x * lax.rsqrt(jnp.mean(x * x, axis=-1, keepdims=True) + RMS_EPS) * gain


W_PREP_ROWS = 128
PROJ_CHUNK = 256


def _stage_w_in(wt_ref, wb_ref):
    for c0 in range(0, C_GKL, LANES):
        src = c0 if c0 < C_U else c0 + GLA_GATE_RANK
        wb_ref[:, c0:c0 + LANES] = wt_ref[0, src:src + LANES, :].T.astype(BF16)
    low = jnp.concatenate([wt_ref[0, C_U:C_U + GLA_GATE_RANK, :],
                           jnp.zeros((LANES - GLA_GATE_RANK, D_MODEL), F32)], axis=0)
    wb_ref[:, C_GKL:D_PROJ] = low.T.astype(BF16)


def _row_gather_copy(yb_ref, buf_ref, sem_ref, slot, k, r, d):
    return pltpu.make_async_copy(yb_ref.at[d], buf_ref.at[slot, k, r], sem_ref.at[slot])


def _gather_start(dest_ref, yb_ref, buf_ref, sem_ref, slot, rows=range(TM)):
    for r in rows:
        for k in range(TOP_K):
            _row_gather_copy(yb_ref, buf_ref, sem_ref, slot, k, r,
                             dest_ref[0, 0, k * TM + r]).start(priority=k)


def _gather_wait(yb_ref, buf_ref, sem_ref, slot):
    for k in range(TOP_K):
        pltpu.make_async_copy(yb_ref.at[pl.ds(0, TM)], buf_ref.at[slot, k], sem_ref.at[slot]).wait()


def _combined_residual(dcur_ref, x_ref, route_ref, yb_ref, buf_ref, sem_ref):
    i = pl.program_id(0)
    slot = lax.rem(i, 2)

    @pl.when(i == 0)
    def _():
        _gather_start(dcur_ref, yb_ref, buf_ref, sem_ref, 0)

    _gather_wait(yb_ref, buf_ref, sem_ref, slot)
    w0 = route_ref[:, R_W:R_W + 1]
    w1 = route_ref[:, R_W + 1:R_W + 2]
    y0 = _row_tiles_chunks(buf_ref.at[slot, 0], TM)
    y1 = _row_tiles_chunks(buf_ref.at[slot, 1], TM)
    return jnp.concatenate(
        [x_ref[:, c * LANES:(c + 1) * LANES] + (w0 * y0[c] + w1 * y1[c]) for c in range(ROW_TILES)],
        axis=1)


def _prefetch_groups(n_groups):
    per = -(-TM // n_groups)
    return [range(g * per, min(TM, (g + 1) * per)) for g in range(n_groups)]


def _drain_last_prefetch(yb_ref, buf_ref, sem_ref):
    i = pl.program_id(0)

    @pl.when(i == pl.num_programs(0) - 1)
    def _():
        _gather_wait(yb_ref, buf_ref, sem_ref, 1 - lax.rem(i, 2))


def _combine_specs(n_tiles):
    smem_tile = lambda f: pl.BlockSpec((1, 1, TOP_K * TM), f, memory_space=pltpu.SMEM)
    return [
        smem_tile(lambda i: (i, 0, 0)),
        smem_tile(lambda i: (jnp.minimum(i + 1, n_tiles - 1), 0, 0)),
        pl.BlockSpec((TM, D_MODEL), lambda i: (i, 0)),
        pl.BlockSpec((TM, LANES), lambda i: (i, 0)),
        pl.BlockSpec(memory_space=pl.ANY),
    ]


_COMBINE_SCRATCH = [pltpu.VMEM((2, TOP_K, TM, ROW_TILES, LANES), F32),
                    pltpu.SemaphoreType.DMA((2,))]


def _combine_final_norm_kernel(dcur_ref, dnxt_ref, x_ref, route_ref, yb_ref, gain_ref,
                               o_ref, buf_ref, sem_ref):
    _gather_start(dnxt_ref, yb_ref, buf_ref, sem_ref, 1 - lax.rem(pl.program_id(0), 2))
    x = _combined_residual(dcur_ref, x_ref, route_ref, yb_ref, buf_ref, sem_ref)
    o_ref[...] = _rms(x, gain_ref[...])
    _drain_last_prefetch(yb_ref, buf_ref, sem_ref)


def _combine_final_norm(dest_tiles, x2, route, yb, gain):
    T = x2.shape[0]
    n_tiles = T // TM
    return pl.pallas_call(
        _combine_final_norm_kernel,
        grid=(n_tiles,),
        in_specs=_combine_specs(n_tiles) + [pl.BlockSpec((1, D_MODEL), lambda i: (0, 0))],
        out_specs=pl.BlockSpec((TM, D_MODEL), lambda i: (i, 0)),
        out_shape=jax.ShapeDtypeStruct((T, D_MODEL), F32),
        scratch_shapes=_COMBINE_SCRATCH,
        compiler_params=pltpu.CompilerParams(
            dimension_semantics=("arbitrary",), vmem_limit_bytes=VMEM_LIMIT),
        name="combine_final_norm",
    )(dest_tiles, dest_tiles, x2, route, yb, gain)


def _gelu_tanh(x):
    c = 0.7978845608028654
    return x * (0.5 * (1.0 + jnp.tanh(c * (x + 0.044715 * (x * x * x)))))


def _mixer_kernel(proj_ref, wgk_ref, bgk_ref, glan_ref, gmn_ref, wsp_ref, bsp_ref, wconv_ref,
                  out_ref, st_ref, hc_ref, lcat_ref, wm_ref, *, seq_start, first_step, between):
    TS = TS_MIX
    n_gla = TS // GLA_CHUNK
    n_gm = TS // GMLP_CHUNK

    @pl.when(seq_start)
    def _():
        st_ref[...] = jnp.zeros_like(st_ref)
        hc_ref[...] = jnp.zeros_like(hc_ref)

    @pl.when(first_step)
    def _():
        r = lax.broadcasted_iota(jnp.int32, (TS, TS), 0)
        c = lax.broadcasted_iota(jnp.int32, (TS, TS), 1)
        keep = ((r // GLA_CHUNK) == (c // GLA_CHUNK)) & (c <= r)
        lcat_ref[...] = jnp.where(keep, 1.0, 0.0).astype(BF16)
        t = lax.broadcasted_iota(jnp.int32, (GMLP_CHUNK, GMLP_HEADS * GMLP_CHUNK), 0)
        s = lax.broadcasted_iota(jnp.int32, (GMLP_CHUNK, GMLP_HEADS * GMLP_CHUNK), 1) % GMLP_CHUNK
        wm_ref[...] = jnp.where(s <= t, wsp_ref[...], 0.0).astype(BF16)

    lane256 = lax.broadcasted_iota(jnp.int32, (1, GLA_KDIM), 1)

    q = proj_ref[:, C_Q:C_Q + GLA_KDIM].astype(F32)
    k = proj_ref[:, C_K:C_K + GLA_KDIM].astype(F32)
    v_b = proj_ref[:, C_V:C_V + GLA_WIDTH]
    z = _dot(proj_ref[:, C_GKL:C_GKL + LANES], wgk_ref[...]) + bgk_ref[...]
    gk = (jnp.minimum(z, 0.0) - jnp.log1p(jnp.exp(-jnp.abs(z)))) * (1.0 / GLA_GATE_NORM)
    gk_hi, gk_lo = _split_bf16(gk)
    cs = _dot(lcat_ref[...], jnp.concatenate([gk_hi, gk_lo], axis=1))
    b = cs[:, :GLA_KDIM] + cs[:, GLA_KDIM:]
    b_last = [b[(c + 1) * GLA_CHUNK - 1:(c + 1) * GLA_CHUNK, :] for c in range(n_gla)]
    bl = jnp.concatenate(
        [jnp.broadcast_to(t, (GLA_CHUNK, GLA_KDIM)) for t in b_last], axis=0)
    q_dec = (q * (GLA_DK ** -0.5)) * jnp.exp(b)
    k_inv = (k * jnp.exp(-b)).astype(BF16)
    k_dec = (k * jnp.exp(bl - b)).astype(BF16)
    q_dec_b = q_dec.astype(BF16)

    zero_b = jnp.zeros_like(q_dec_b)
    q_stack = jnp.concatenate(
        [jnp.where((lane256 // GLA_DK) == h, q_dec_b, zero_b) for h in range(GLA_HEADS)], axis=0)
    scores = lax.dot_general(q_stack, k_inv, (((1,), (1,)), ((), ())),
                             preferred_element_type=F32)
    rt = lax.broadcasted_iota(jnp.int32, (TS, TS), 0)
    ct = lax.broadcasted_iota(jnp.int32, (TS, TS), 1)
    causal = ((rt // GLA_CHUNK) == (ct // GLA_CHUNK)) & (ct <= rt)
    o_heads = []
    for h in range(GLA_HEADS):
        p_h = jnp.where(causal, scores[h * TS:(h + 1) * TS, :], 0.0).astype(BF16)
        o_heads.append(_dot(p_h, v_b[:, h * GLA_DV:(h + 1) * GLA_DV]))

    sr = lax.broadcasted_iota(jnp.int32, (GLA_WIDTH, GLA_KDIM), 0) // GLA_DV
    sc = lax.broadcasted_iota(jnp.int32, (GLA_WIDTH, GLA_KDIM), 1) // GLA_DK
    bd_mask = sr == sc
    o_inter = []
    for c in range(n_gla):
        rows = slice(c * GLA_CHUNK, (c + 1) * GLA_CHUNK)
        st = st_ref[...]
        o_inter.append(lax.dot_general(q_dec_b[rows], st.astype(BF16), (((1,), (1,)), ((), ())),
                                       preferred_element_type=F32))
        upd = lax.dot_general(v_b[rows], k_dec[rows], (((0,), (0,)), ((), ())),
                              preferred_element_type=F32)
        decay = jnp.exp(b_last[c])
        st_ref[...] = st * decay + jnp.where(bd_mask, upd, 0.0)
    o_inter = jnp.concatenate(o_inter, axis=0)

    for h in range(GLA_HEADS):
        cols = slice(h * GLA_DV, (h + 1) * GLA_DV)
        o = o_heads[h] + o_inter[:, cols]
        o = o * lax.rsqrt(jnp.mean(o * o, axis=-1, keepdims=True) + RMS_EPS) * glan_ref[...]
        g = proj_ref[:, C_G + h * GLA_DV:C_G + (h + 1) * GLA_DV].astype(F32)
        out_ref[:, cols] = (o * (g * (1.0 / (1.0 + jnp.exp(-g))))).astype(out_ref.dtype)

    between("gla_done")
    u = _gelu_tanh(proj_ref[:, C_U:C_U + GMLP_WIDTH].astype(F32))
    vg = _gelu_tanh(proj_ref[:, C_VG:C_VG + GMLP_WIDTH].astype(F32))
    hr = lax.broadcasted_iota(jnp.int32, (GMLP_WIDTH, GMLP_WIDTH), 0) // GMLP_DH
    hcn = lax.broadcasted_iota(jnp.int32, (GMLP_WIDTH, GMLP_WIDTH), 1) // GMLP_DH
    head_mean = jnp.where(hr == hcn, 1.0 / GMLP_DH, 0.0).astype(BF16)
    sq_hi, sq_lo = _split_bf16(vg * vg)
    ms = _dot(sq_hi, head_mean) + _dot(sq_lo, head_mean)
    v32 = vg * lax.rsqrt(ms + RMS_EPS) * gmn_ref[...]
    for c in range(n_gm):
        rows = slice(c * GMLP_CHUNK, (c + 1) * GMLP_CHUNK)
        vc = v32[rows].astype(BF16)
        zc = jnp.zeros_like(vc)
        rhs = jnp.concatenate(
            [jnp.where((lane256 // GMLP_DH) == h, vc, zc) for h in range(GMLP_HEADS)], axis=0)
        mixed = _dot(wm_ref[...], rhs) + bsp_ref[...]
        out_ref[rows, GLA_WIDTH:GLA_WIDTH + GMLP_WIDTH] = (u[rows] * mixed).astype(out_ref.dtype)

    between("gmlp_done")
    hcv = (proj_ref[:, C_CG:C_CG + CONV_WIDTH].astype(F32)
           * proj_ref[:, C_X:C_X + CONV_WIDTH].astype(F32))
    hc_ref[8:8 + TS, :] = hcv
    y = (wconv_ref[2:3, :] * hcv + wconv_ref[1:2, :] * hc_ref[7:7 + TS, :]
         + wconv_ref[0:1, :] * hc_ref[6:6 + TS, :])
    out_ref[:, GLA_WIDTH + GMLP_WIDTH:] = (
        proj_ref[:, C_BG:C_BG + CONV_WIDTH].astype(F32) * y).astype(out_ref.dtype)
    hc_ref[0:8, :] = hc_ref[TS:TS + 8, :]


_MIXER_SCRATCH = [
    pltpu.VMEM((GLA_WIDTH, GLA_KDIM), F32),
    pltpu.VMEM((TS_MIX + 8, CONV_WIDTH), F32),
    pltpu.VMEM((TS_MIX, TS_MIX), BF16),
    pltpu.VMEM((GMLP_CHUNK, GMLP_HEADS * GMLP_CHUNK), BF16),
]


N_MIX_PARAMS = 7
PROJ_CHUNKS_AT = {"gla_done": 4, "gmlp_done": 4}


def _front_kernel(*refs, tiles_per_seq, combine):
    refs = list(refs)
    if combine:
        dcur_ref, dnxt_ref, x_ref, route_ref, yb_ref = refs[:5]
        del refs[:5]
    else:
        x_ref = refs.pop(0)
    gain_ref, wt_ref = refs[:2]
    mix_refs = refs[2:2 + N_MIX_PARAMS]
    del refs[:2 + N_MIX_PARAMS]
    if combine:
        xo_ref, out_ref, buf_ref, sem_ref = refs[:4]
        del refs[:4]
    else:
        out_ref = refs.pop(0)
    st_ref, hc_ref, lcat_ref, wm_ref, wb_ref, pcur_ref, pnext_ref = refs
    s = pl.program_id(0)

    @pl.when(s == 0)
    def _():
        _stage_w_in(wt_ref, wb_ref)
        pcur_ref[...] = jnp.zeros_like(pcur_ref)

    col_chunks = [(c0, min(c0 + PROJ_CHUNK, D_PROJ)) for c0 in range(0, D_PROJ, PROJ_CHUNK)]
    work = list(zip(_prefetch_groups(len(col_chunks)), col_chunks))

    def project(h, n):
        for _ in range(min(n, len(work))):
            rows, (c0, c1) = work.pop(0)
            if combine:
                _gather_start(dnxt_ref, yb_ref, buf_ref, sem_ref, 1 - lax.rem(s, 2), rows)
            pnext_ref[:, c0:c1] = _dot(h, wb_ref[:, c0:c1]).astype(BF16)

    mixers = functools.partial(
        _mixer_kernel, pcur_ref, *mix_refs, out_ref, st_ref, hc_ref, lcat_ref, wm_ref,
        seq_start=lax.rem(jnp.maximum(s - 1, 0), tiles_per_seq) == 0, first_step=s == 0)
    if combine:
        x = _combined_residual(dcur_ref, x_ref, route_ref, yb_ref, buf_ref, sem_ref)
        xo_ref[...] = x
        h = _rms(x, gain_ref[...]).astype(BF16)
        mixers(between=lambda site: project(h, PROJ_CHUNKS_AT[site]))
        project(h, len(work))
        _drain_last_prefetch(yb_ref, buf_ref, sem_ref)
    else:
        mixers(between=lambda site: None)
        project(_rms(x_ref[...], gain_ref[...]).astype(BF16), len(work))
    pcur_ref[...] = pnext_ref[...]


def _front(x, gain, w_in_t, layer, mix_params, batch, seq, moe=None):
    n_seq = seq // TS_MIX
    n = batch * n_seq
    T = batch * seq
    cur = lambda s: jnp.minimum(s, n - 1)
    full = lambda shape: pl.BlockSpec(shape, lambda s: (0,) * len(shape))
    row = lambda w, f: pl.BlockSpec((TM, w), lambda s: (f(s), 0))
    in_specs, args = [row(D_MODEL, cur)], [x]
    out_specs = [row(D_MODEL, lambda s: jnp.maximum(s - 1, 0))]
    out_shape = [jax.ShapeDtypeStruct((T, D_MODEL), BF16)]
    scratch = list(_MIXER_SCRATCH)
    if moe is not None:
        smem_tile = lambda f: pl.BlockSpec((1, 1, TOP_K * TM), lambda s: (f(s), 0, 0),
                                           memory_space=pltpu.SMEM)
        in_specs = [smem_tile(cur), smem_tile(lambda s: jnp.minimum(s + 1, n - 1))] + in_specs + [
            row(LANES, cur), pl.BlockSpec(memory_space=pl.ANY)]
        args = [moe["dest_tiles"], moe["dest_tiles"]] + args + [moe["route"], moe["yb"]]
        out_specs = [row(D_MODEL, cur)] + out_specs
        out_shape = [jax.ShapeDtypeStruct((T, D_MODEL), F32)] + out_shape
        scratch = _COMBINE_SCRATCH + scratch
    in_specs += [
        full((1, D_MODEL)),
        pl.BlockSpec((1, D_IN, D_MODEL), lambda s: (layer, 0, 0), pipeline_mode=pl.Buffered(1)),
        full((LANES, GLA_KDIM)), full((1, GLA_KDIM)), full((1, GLA_DV)), full((1, GMLP_WIDTH)),
        full((GMLP_CHUNK, GMLP_HEADS * GMLP_CHUNK)), full((GMLP_CHUNK, GMLP_WIDTH)),
        full((8, CONV_WIDTH)),
    ]
    scratch += [pltpu.VMEM((D_MODEL, D_PROJ), BF16), pltpu.VMEM((TS_MIX, D_PROJ), BF16),
                pltpu.VMEM((TS_MIX, D_PROJ), BF16)]
    return pl.pallas_call(
        functools.partial(_front_kernel, tiles_per_seq=n_seq, combine=moe is not None),
        grid=(n + 1,),
        in_specs=in_specs,
        out_specs=out_specs if moe is not None else out_specs[0],
        out_shape=out_shape if moe is not None else out_shape[0],
        scratch_shapes=scratch,
        compiler_params=pltpu.CompilerParams(
            dimension_semantics=("arbitrary",), vmem_limit_bytes=VMEM_LIMIT),
        name="front",
    )(*args, gain, w_in_t, *mix_params)


def _row_tiles_store(tiles_ref, x):
    rows = x.shape[0]
    flat = tiles_ref.reshape(rows * ROW_TILES, LANES)
    for c in range(ROW_TILES):
        flat[pl.ds(c, rows, stride=ROW_TILES), :] = x[:, c * LANES:(c + 1) * LANES]


def _row_tiles_chunks(tiles_ref, rows):
    flat = tiles_ref.reshape(rows * ROW_TILES, LANES)
    return [flat[pl.ds(c, rows, stride=ROW_TILES), :] for c in range(ROW_TILES)]


def _out_router_kernel(mix_ref, x_ref, wo_ref, gain_ref, wrc_ref, br_ref,
                       x2_ref, h2_ref, route_ref, route_t_ref, cnt_ref, tri_ref, wob_ref, lg_ref):
    i = pl.program_id(0)

    @pl.when(i == 0)
    def _():
        cnt_ref[...] = jnp.zeros_like(cnt_ref)
        lg_ref[...] = jnp.zeros_like(lg_ref)
        r = lax.broadcasted_iota(jnp.int32, (TM, TM), 0)
        c = lax.broadcasted_iota(jnp.int32, (TM, TM), 1)
        tri_ref[...] = jnp.where(c < r, 1.0, 0.0).astype(BF16)
        for r0 in range(0, D_MODEL, W_PREP_ROWS):
            wob_ref[r0:r0 + W_PREP_ROWS, :] = wo_ref[0, r0:r0 + W_PREP_ROWS, :].astype(BF16)

    lg = lg_ref[...]
    x2 = x_ref[...] + _dot(mix_ref[...], wob_ref[...])
    x2_ref[...] = x2
    h = _rms(x2, gain_ref[...])
    h_hi, h_lo = _split_bf16(h)
    h2_ref[...] = h_hi
    hh_hl = _dot(h_hi, wrc_ref[...])
    lg_ref[...] = (hh_hl[:, :ROUTER_COLS] + hh_hl[:, ROUTER_COLS:]
                   + _dot(h_lo, wrc_ref[:, :ROUTER_COLS]) + br_ref[...])

    lane = lax.broadcasted_iota(jnp.int32, (TM, LANES), 1).astype(F32)
    neg = -jnp.inf
    is_g = lane < N_GROUPS
    gl = jnp.where(is_g, lg, neg)
    gmax = jnp.max(gl, axis=1, keepdims=True)
    g_top = jnp.min(jnp.where(gl == gmax, lane, float(LANES)), axis=1, keepdims=True)
    g_w = 1.0 / jnp.sum(jnp.where(is_g, jnp.exp(lg - gmax), 0.0), axis=1, keepdims=True)
    first = N_GROUPS + EXPERTS_PER_GROUP * g_top
    el = jnp.where((lane >= first) & (lane < first + EXPERTS_PER_GROUP), lg, neg)
    m1 = jnp.max(el, axis=1, keepdims=True)
    i1 = jnp.min(jnp.where(el == m1, lane, float(LANES)), axis=1, keepdims=True)
    el2 = jnp.where(lane == i1, neg, el)
    m2 = jnp.max(el2, axis=1, keepdims=True)
    i2 = jnp.min(jnp.where(el2 == m2, lane, float(LANES)), axis=1, keepdims=True)
    ratio = jnp.exp(m2 - m1)
    w1 = g_w / (1.0 + ratio)
    w2 = w1 * ratio

    oh1 = jnp.where(lane == i1, 1.0, 0.0)
    oh2 = jnp.where(lane == i2, 1.0, 0.0)
    oh = jnp.where(i > 0, oh1 + oh2, 0.0)
    before = _dot(tri_ref[...], oh.astype(BF16)) + cnt_ref[0:1, :]
    rank1 = jnp.sum(oh1 * before, axis=1, keepdims=True)
    rank2 = jnp.sum(oh2 * before, axis=1, keepdims=True)
    cnt_ref[...] = cnt_ref[...] + jnp.sum(oh, axis=0, keepdims=True)

    rec = jnp.zeros((TM, LANES), F32)
    for col, val in ((R_E, i1 - N_GROUPS), (R_E + 1, i2 - N_GROUPS), (R_RANK, rank1),
                     (R_RANK + 1, rank2), (R_W, w1), (R_W + 1, w2)):
        rec = jnp.where(lane == col, val, rec)
    route_ref[...] = rec
    route_t_ref[0] = rec.T[0:SUBLANES, :]


def _out_router(mixed, x, w_out, layer, gain, wr_cat, br):
    T = x.shape[0]
    n = T // TM
    row = lambda w: pl.BlockSpec((TM, w), lambda i: (jnp.minimum(i, n - 1), 0))
    lag = lambda i: jnp.maximum(i - 1, 0)
    full = lambda shape: pl.BlockSpec(shape, lambda i: (0, 0))
    wo_spec = pl.BlockSpec((1, D_MODEL, D_MODEL), lambda i: (layer, 0, 0),
                           pipeline_mode=pl.Buffered(1))
    return pl.pallas_call(
        _out_router_kernel,
        grid=(n + 1,),
        in_specs=[row(D_MODEL), row(D_MODEL), wo_spec, full((1, D_MODEL)),
                  full((D_MODEL, 2 * ROUTER_COLS)), full((1, ROUTER_COLS))],
        out_specs=[row(D_MODEL), row(D_MODEL),
                   pl.BlockSpec((TM, LANES), lambda i: (lag(i), 0)),
                   pl.BlockSpec((1, SUBLANES, TM), lambda i: (lag(i), 0, 0)), full((8, LANES))],
        out_shape=[jax.ShapeDtypeStruct((T, D_MODEL), F32),
                   jax.ShapeDtypeStruct((T, D_MODEL), BF16),
                   jax.ShapeDtypeStruct((T, LANES), F32),
                   jax.ShapeDtypeStruct((T // TM, SUBLANES, TM), F32),
                   jax.ShapeDtypeStruct((8, LANES), F32)],
        scratch_shapes=[pltpu.VMEM((TM, TM), BF16), pltpu.VMEM((D_MODEL, D_MODEL), BF16),
                        pltpu.VMEM((TM, ROUTER_COLS), F32)],
        compiler_params=pltpu.CompilerParams(
            dimension_semantics=("arbitrary",), vmem_limit_bytes=VMEM_LIMIT),
        name="out_router",
    )(mixed, x, w_out, gain, wr_cat, br)


def _dispatch_kernel(fill_ref, nu_ref, dest_ref, h_ref, xb_ref, zero_ref, sem_ref, zsem_ref,
                     stage_ref):
    i = pl.program_id(0)
    par = lax.rem(i, 2)
    last = i == pl.num_programs(0) - 1
    n_blocks = xb_ref.shape[0] // MOE_BLK
    spare_fills = [(j >= nu_ref[0], pltpu.make_async_copy(
        zero_ref, xb_ref.at[pl.ds(j * MOE_BLK, MOE_BLK)], zsem_ref.at[1]))
        for j in range(n_blocks - N_EXPERTS, n_blocks)]

    @pl.when(i == 0)
    def _():
        zero_ref[...] = jnp.zeros_like(zero_ref)
        fills = [(fill_ref[e] >= 0, pltpu.make_async_copy(
            zero_ref, xb_ref.at[pl.ds(pl.multiple_of(jnp.maximum(fill_ref[e], 0), MOE_BLK), MOE_BLK)],
            zsem_ref.at[0])) for e in range(N_EXPERTS)]
        for cond, f in fills + spare_fills:
            pl.when(cond)(f.start)
        for cond, f in fills:
            pl.when(cond)(f.wait)

    _row_tiles_store(stage_ref.at[par], h_ref[...].astype(F32))
    for r in range(TM):
        for k in range(TOP_K):
            pltpu.make_async_copy(stage_ref.at[par, r], xb_ref.at[dest_ref[0, 0, k * TM + r]],
                                  sem_ref.at[par]).start(priority=k)

    def wait_tile(p):
        for _ in range(TOP_K):
            pltpu.make_async_copy(stage_ref.at[p], xb_ref.at[pl.ds(0, TM)], sem_ref.at[p]).wait()

    pl.when(i > 0)(lambda: wait_tile(1 - par))
    @pl.when(last)
    def _():
        wait_tile(par)
        for cond, f in spare_fills:
            pl.when(cond)(f.wait)


def _dispatch(fill_start, n_used, dest_tiles, h2, n_rows):
    T = h2.shape[0]
    grid_spec = pltpu.PrefetchScalarGridSpec(
        num_scalar_prefetch=2,
        grid=(T // TM,),
        in_specs=[
            pl.BlockSpec((1, 1, TOP_K * TM), lambda i, fs, nu: (i, 0, 0), memory_space=pltpu.SMEM),
            pl.BlockSpec((TM, D_MODEL), lambda i, fs, nu: (i, 0)),
        ],
        out_specs=pl.BlockSpec(memory_space=pl.ANY),
        scratch_shapes=[pltpu.VMEM((MOE_BLK, ROW_TILES, LANES), F32),
                        pltpu.SemaphoreType.DMA((2,)), pltpu.SemaphoreType.DMA((2,)),
                        pltpu.VMEM((2, TM, ROW_TILES, LANES), F32)],
    )
    return pl.pallas_call(
        _dispatch_kernel,
        grid_spec=grid_spec,
        out_shape=jax.ShapeDtypeStruct((n_rows, ROW_TILES, LANES), F32),
        compiler_params=pltpu.CompilerParams(dimension_semantics=("arbitrary",)),
        name="dispatch",
    )(fill_start, n_used, dest_tiles, h2)


BLOCK_COPY_PARTS = 4
X_SLOTS = 6
Y_SLOTS = 4


class _CopyGroup:
    def __init__(self, copies):
        self.copies = copies

    def start(self):
        for n, c in enumerate(self.copies):
            c.start(priority=n % 2)

    def wait(self):
        for c in self.copies:
            c.wait()


def _expert_kernel(be_ref, nxt_ref, nu_ref, xb_ref, wg_ref, wu_ref, wd_ref, yb_ref,
                   xbuf_ref, ybuf_ref, wgs_ref, wus_ref, wds_ref, wgb_ref, wub_ref, wdb_ref,
                   xsem_ref, ysem_ref, wsem_ref, zbuf_ref, zsem_ref, *, layer, n_blocks):
    n_used = nu_ref[0]

    part = MOE_BLK // BLOCK_COPY_PARTS

    def x_copy(j, slot):
        return _CopyGroup([pltpu.make_async_copy(
            xb_ref.at[pl.ds(j * MOE_BLK + p * part, part)],
            xbuf_ref.at[slot, pl.ds(p * part, part)], xsem_ref.at[slot])
            for p in range(BLOCK_COPY_PARTS)])

    def y_copy(j, slot):
        return _CopyGroup([pltpu.make_async_copy(
            ybuf_ref.at[slot, pl.ds(p * part, part)],
            yb_ref.at[pl.ds(j * MOE_BLK + p * part, part)], ysem_ref.at[slot])
            for p in range(BLOCK_COPY_PARTS)])

    def w_copies(e, ws):
        return [pltpu.make_async_copy(src.at[layer, e], dst.at[ws], wsem_ref.at[ws])
                for src, dst in ((wg_ref, wgs_ref), (wu_ref, wus_ref), (wd_ref, wds_ref))]

    def fill_copy(j):
        return pltpu.make_async_copy(zbuf_ref, yb_ref.at[pl.ds(j * MOE_BLK, MOE_BLK)], zsem_ref)

    for j0 in range(X_SLOTS - 1):
        x_copy(j0, j0).start()
    for c in w_copies(be_ref[0], 0):
        c.start()

    zbuf_ref[...] = jnp.zeros_like(zbuf_ref)

    def fill(j, carry):
        fill_copy(j).start()
        return carry

    lax.fori_loop(n_used, n_blocks, fill, 0)

    def block(j, ws):
        slot = lax.rem(j, Y_SLOTS)
        xslot = lax.rem(j, X_SLOTS)
        first = (j == 0) | (be_ref[j] != be_ref[jnp.maximum(j - 1, 0)])
        ws = jnp.where(first & (j > 0), 1 - ws, ws)

        @pl.when(first)
        def _():
            for c in w_copies(be_ref[j], ws):
                c.wait()
            for r0 in range(0, D_MODEL, W_PREP_ROWS):
                rows = slice(r0, r0 + W_PREP_ROWS)
                wgb_ref[rows, :] = wgs_ref[ws, rows, :].astype(BF16)
                wub_ref[rows, :] = wus_ref[ws, rows, :].astype(BF16)
            for r0 in range(0, D_EXPERT, W_PREP_ROWS):
                rows = slice(r0, r0 + W_PREP_ROWS)
                wdb_ref[rows, :] = wds_ref[ws, rows, :].astype(BF16)

            @pl.when(nxt_ref[j] >= 0)
            def _():
                for c in w_copies(nxt_ref[j], 1 - ws):
                    c.start()

        ahead = j + X_SLOTS - 1

        @pl.when(ahead < n_used)
        def _():
            x_copy(ahead, lax.rem(ahead, X_SLOTS)).start()

        x_copy(j, xslot).wait()
        x = jnp.concatenate(
            [c.astype(BF16) for c in _row_tiles_chunks(xbuf_ref.at[xslot], MOE_BLK)], axis=1)
        g = _dot(x, wgb_ref[...])
        u = _dot(x, wub_ref[...])
        h = (g * (1.0 / (1.0 + jnp.exp(-g)))) * u
        y = _dot(h.astype(BF16), wdb_ref[...])

        @pl.when(j >= Y_SLOTS)
        def _():
            y_copy(j - Y_SLOTS, slot).wait()

        _row_tiles_store(ybuf_ref.at[slot], y)
        y_copy(j, slot).start()
        return ws

    lax.fori_loop(0, n_used, block, jnp.int32(0))

    for back in range(Y_SLOTS, 0, -1):
        y_copy(n_used - back, lax.rem(n_used - back, Y_SLOTS)).wait()

    def fill_wait(j, carry):
        fill_copy(j).wait()
        return carry

    lax.fori_loop(n_used, n_blocks, fill_wait, 0)


def _experts(blk_exp, nxt_exp, n_used, xb, w_gate, w_up, w_down, layer):
    n_blocks = blk_exp.shape[0]
    any_spec = pl.BlockSpec(memory_space=pl.ANY)
    blk = (MOE_BLK, ROW_TILES, LANES)
    grid_spec = pltpu.PrefetchScalarGridSpec(
        num_scalar_prefetch=3,
        grid=(1,),
        in_specs=[any_spec, any_spec, any_spec, any_spec],
        out_specs=any_spec,
        scratch_shapes=[
            pltpu.VMEM((X_SLOTS,) + blk, F32), pltpu.VMEM((Y_SLOTS,) + blk, F32),
            pltpu.VMEM((2, D_MODEL, D_EXPERT), F32), pltpu.VMEM((2, D_MODEL, D_EXPERT), F32),
            pltpu.VMEM((2, D_EXPERT, D_MODEL), F32),
            pltpu.VMEM((D_MODEL, D_EXPERT), BF16), pltpu.VMEM((D_MODEL, D_EXPERT), BF16),
            pltpu.VMEM((D_EXPERT, D_MODEL), BF16),
            pltpu.SemaphoreType.DMA((X_SLOTS,)), pltpu.SemaphoreType.DMA((Y_SLOTS,)),
            pltpu.SemaphoreType.DMA((2,)),
            pltpu.VMEM(blk, F32), pltpu.SemaphoreType.DMA(()),
        ],
    )
    return pl.pallas_call(
        functools.partial(_expert_kernel, layer=layer, n_blocks=n_blocks),
        grid_spec=grid_spec,
        out_shape=jax.ShapeDtypeStruct((n_blocks * MOE_BLK, ROW_TILES, LANES), F32),
        compiler_params=pltpu.CompilerParams(
            dimension_semantics=("arbitrary",), vmem_limit_bytes=VMEM_LIMIT),
        name="experts",
    )(blk_exp, nxt_exp, n_used, xb, w_gate, w_up, w_down)


def _dispatch_tables(route_t, counts_rec, T):
    counts = counts_rec[0, N_GROUPS:N_GROUPS + N_EXPERTS].astype(jnp.int32)
    n_steps = (T * TOP_K) // MOE_BLK + N_EXPERTS
    nblk = (counts + MOE_BLK - 1) // MOE_BLK
    bend = jnp.cumsum(nblk)
    pstart = (bend - nblk) * MOE_BLK
    n_used = bend[-1]
    j = jnp.minimum(jnp.arange(n_steps, dtype=jnp.int32), n_used - 1)
    blk_exp = jnp.minimum(jnp.sum(j[:, None] >= bend[None, :], axis=1), N_EXPERTS - 1)
    n_rows = n_steps * MOE_BLK
    last_blk = jnp.where(counts > 0, (bend - 1) * MOE_BLK, -1)
    ids = jnp.arange(N_EXPERTS, dtype=jnp.int32)
    later = (ids[None, :] > ids[:, None]) & (nblk[None, :] > 0)
    nxt_of = jnp.min(jnp.where(later, ids[None, :], N_EXPERTS), axis=1)
    nxt_tab = jnp.where(nxt_of < N_EXPERTS, nxt_of, -1)
    nxt_exp = jnp.sum(jnp.where(blk_exp[:, None] == ids[None, :], nxt_tab[None, :], 0), axis=1)
    e = route_t[:, R_E:R_E + TOP_K, :].astype(jnp.int32)
    rank = route_t[:, R_RANK:R_RANK + TOP_K, :].astype(jnp.int32)
    seg = jnp.sum(jnp.where(e[..., None] == jnp.arange(N_EXPERTS), pstart, 0), axis=-1)
    dest = jnp.clip(seg + rank, 0, n_steps * MOE_BLK - 1)
    dest_tiles = dest.reshape(T // TM, 1, TOP_K * TM)
    return dict(dest_tiles=dest_tiles, fill_start=last_blk.astype(jnp.int32),
                blk_exp=blk_exp.astype(jnp.int32), nxt_exp=nxt_exp.astype(jnp.int32),
                n_used=n_used.reshape(1).astype(jnp.int32), n_rows=n_rows)


def _prep_layer(l, w_gk_up, b_gk, gla_norm, gmlp_norm, w_spatial, b_spatial, w_conv,
                w_router_group, b_router_group, w_router_expert, b_router_expert):
    wgk = jnp.concatenate(
        [w_gk_up[l], jnp.zeros((LANES - GLA_GATE_RANK, GLA_KDIM), F32)], axis=0).astype(BF16)
    wsp = w_spatial[l].transpose(1, 0, 2).reshape(GMLP_CHUNK, GMLP_HEADS * GMLP_CHUNK)
    bsp = jnp.repeat(b_spatial[l].T, GMLP_DH, axis=1)
    wconv = jnp.concatenate([w_conv[l], jnp.zeros((8 - CONV_K, CONV_WIDTH), F32)], axis=0)
    wr = jnp.concatenate(
        [w_router_group[l], w_router_expert[l],
         jnp.zeros((D_MODEL, ROUTER_COLS - N_GROUPS - N_EXPERTS), F32)], axis=1)
    wr_hi = wr.astype(BF16)
    wr_lo = (wr - wr_hi.astype(F32)).astype(BF16)
    br = jnp.concatenate(
        [b_router_group[l], b_router_expert[l],
         jnp.zeros((ROUTER_COLS - N_GROUPS - N_EXPERTS,), F32)])[None, :]
    return dict(
        wgk=wgk, bgk=b_gk[l][None, :], glan=gla_norm[l][None, :], gmn=gmlp_norm[l][None, :],
        wsp=wsp, bsp=bsp, wconv=wconv, wr_cat=jnp.concatenate([wr_hi, wr_lo], axis=1), br=br)


def kernel(x, attn_norm, w_in, w_gk_up, b_gk, gla_norm, gmlp_norm, w_spatial, b_spatial, w_conv, w_out, ffn_norm, w_router_group, b_router_group, w_router_expert, b_router_expert, w_gate, w_up, w_down, final_norm):
    B, S, D = x.shape
    T = B * S
    depth = w_in.shape[0]
    xr = x.reshape(T, D)
    w_in_t = jnp.swapaxes(w_in, 1, 2)
    moe = None
    for l in range(depth):
        p = _prep_layer(l, w_gk_up, b_gk, gla_norm, gmlp_norm, w_spatial, b_spatial, w_conv,
                        w_router_group, b_router_group, w_router_expert, b_router_expert)
        mix_params = (p["wgk"], p["bgk"], p["glan"], p["gmn"], p["wsp"], p["bsp"], p["wconv"])
        if moe is None:
            mixed = _front(xr, attn_norm[l][None, :], w_in_t, l, mix_params, B, S)
        else:
            xr, mixed = _front(moe["x2"], attn_norm[l][None, :], w_in_t, l, mix_params, B, S, moe)
        x2, h2, route, route_t, counts_rec = _out_router(
            mixed, xr, w_out, l, ffn_norm[l][None, :], p["wr_cat"], p["br"])
        moe = _dispatch_tables(route_t, counts_rec, T)
        xb = _dispatch(moe["fill_start"], moe["n_used"], moe["dest_tiles"], h2, moe["n_rows"])
        yb = _experts(moe["blk_exp"], moe["nxt_exp"], moe["n_used"], xb, w_gate, w_up, w_down, l)
        moe.update(x2=x2, route=route, yb=yb)
    out = _combine_final_norm(moe["dest_tiles"], moe["x2"], moe["route"], moe["yb"],
                              final_norm[None, :])
    return out.reshape(B, S, D)
```

```python
import functools

import jax
import jax.numpy as jnp
from jax import lax
from jax.experimental import pallas as pl
from jax.experimental.pallas import tpu as pltpu

F32 = jnp.float32
BF16 = jnp.bfloat16

D_MODEL = 1024
RMS_EPS = 1e-6
GLA_HEADS = 4
GLA_WIDTH = 512
GLA_DV = 128
GLA_DK = 64
GLA_KDIM = 256
GLA_GATE_RANK = 16
GLA_GATE_NORM = 16.0
GLA_CHUNK = 64
GMLP_HEADS = 4
GMLP_WIDTH = 256
GMLP_DH = 64
GMLP_CHUNK = 128
CONV_WIDTH = 256
CONV_K = 3
N_GROUPS = 4
EXPERTS_PER_GROUP = 8
N_EXPERTS = 32
TOP_K = 2
D_EXPERT = 256

LANES = 128
C_Q, C_K, C_V, C_G = 0, 256, 512, 1024
C_U, C_VG, C_X, C_BG, C_CG, C_GKL = 1536, 1792, 2048, 2304, 2560, 2816
D_PROJ = C_GKL + LANES
D_IN = C_GKL + GLA_GATE_RANK

TM = 256
TS_MIX = TM
MOE_BLK = 256
ROUTER_COLS = LANES
SUBLANES = 8
ROW_TILES = D_MODEL // LANES
assert ROW_TILES == SUBLANES
VMEM_LIMIT = 56 * 1024 * 1024
R_E, R_RANK, R_W = 0, 2, 4


def _dot(a, b):
    return jnp.dot(a, b, preferred_element_type=F32)


def _split_bf16(x):
    hi = x.astype(BF16)
    lo = (x - hi.astype(F32)).astype(BF16)
    return hi, lo


def _rms(x, gain):
    return x * lax.rsqrt(jnp.mean(x * x, axis=-1, keepdims=True) + RMS_EPS) * gain


W_PREP_ROWS = 128
PROJ_CHUNK = 256


def _stage_w_in(wt_ref, wb_ref):
    for c0 in range(0, C_GKL, LANES):
        src = c0 if c0 < C_U else c0 + GLA_GATE_RANK
        wb_ref[:, c0:c0 + LANES] = wt_ref[0, src:src + LANES, :].T.astype(BF16)
    low = jnp.concatenate([wt_ref[0, C_U:C_U + GLA_GATE_RANK, :],
                           jnp.zeros((LANES - GLA_GATE_RANK, D_MODEL), F32)], axis=0)
    wb_ref[:, C_GKL:D_PROJ] = low.T.astype(BF16)


def _row_gather_copy(yb_ref, buf_ref, sem_ref, slot, k, r, d):
    return pltpu.make_async_copy(yb_ref.at[d], buf_ref.at[slot, k, r], sem_ref.at[slot])


def _gather_start(dest_ref, yb_ref, buf_ref, sem_ref, slot, rows=range(TM)):
    for r in rows:
        for k in range(TOP_K):
            _row_gather_copy(yb_ref, buf_ref, sem_ref, slot, k, r,
                             dest_ref[0, 0, k * TM + r]).start(priority=k)


def _gather_wait(yb_ref, buf_ref, sem_ref, slot):
    for k in range(TOP_K):
        pltpu.make_async_copy(yb_ref.at[pl.ds(0, TM)], buf_ref.at[slot, k], sem_ref.at[slot]).wait()


def _combined_residual(dcur_ref, x_ref, route_ref, yb_ref, buf_ref, sem_ref):
    i = pl.program_id(0)
    slot = lax.rem(i, 2)

    @pl.when(i == 0)
    def _():
        _gather_start(dcur_ref, yb_ref, buf_ref, sem_ref, 0)

    _gather_wait(yb_ref, buf_ref, sem_ref, slot)
    w0 = route_ref[:, R_W:R_W + 1]
    w1 = route_ref[:, R_W + 1:R_W + 2]
    y0 = _row_tiles_chunks(buf_ref.at[slot, 0], TM)
    y1 = _row_tiles_chunks(buf_ref.at[slot, 1], TM)
    return jnp.concatenate(
        [x_ref[:, c * LANES:(c + 1) * LANES] + (w0 * y0[c] + w1 * y1[c]) for c in range(ROW_TILES)],
        axis=1)


def _prefetch_groups(n_groups):
    per = -(-TM // n_groups)
    return [range(g * per, min(TM, (g + 1) * per)) for g in range(n_groups)]


def _drain_last_prefetch(yb_ref, buf_ref, sem_ref):
    i = pl.program_id(0)

    @pl.when(i == pl.num_programs(0) - 1)
    def _():
        _gather_wait(yb_ref, buf_ref, sem_ref, 1 - lax.rem(i, 2))


def _combine_specs(n_tiles):
    smem_tile = lambda f: pl.BlockSpec((1, 1, TOP_K * TM), f, memory_space=pltpu.SMEM)
    return [
        smem_tile(lambda i: (i, 0, 0)),
        smem_tile(lambda i: (jnp.minimum(i + 1, n_tiles - 1), 0, 0)),
        pl.BlockSpec((TM, D_MODEL), lambda i: (i, 0)),
        pl.BlockSpec((TM, LANES), lambda i: (i, 0)),
        pl.BlockSpec(memory_space=pl.ANY),
    ]


_COMBINE_SCRATCH = [pltpu.VMEM((2, TOP_K, TM, ROW_TILES, LANES), F32),
                    pltpu.SemaphoreType.DMA((2,))]


def _combine_final_norm_kernel(dcur_ref, dnxt_ref, x_ref, route_ref, yb_ref, gain_ref,
                               o_ref, buf_ref, sem_ref):
    _gather_start(dnxt_ref, yb_ref, buf_ref, sem_ref, 1 - lax.rem(pl.program_id(0), 2))
    x = _combined_residual(dcur_ref, x_ref, route_ref, yb_ref, buf_ref, sem_ref)
    o_ref[...] = _rms(x, gain_ref[...])
    _drain_last_prefetch(yb_ref, buf_ref, sem_ref)


def _combine_final_norm(dest_tiles, x2, route, yb, gain):
    T = x2.shape[0]
    n_tiles = T // TM
    return pl.pallas_call(
        _combine_final_norm_kernel,
        grid=(n_tiles,),
        in_specs=_combine_specs(n_tiles) + [pl.BlockSpec((1, D_MODEL), lambda i: (0, 0))],
        out_specs=pl.BlockSpec((TM, D_MODEL), lambda i: (i, 0)),
        out_shape=jax.ShapeDtypeStruct((T, D_MODEL), F32),
        scratch_shapes=_COMBINE_SCRATCH,
        compiler_params=pltpu.CompilerParams(
            dimension_semantics=("arbitrary",), vmem_limit_bytes=VMEM_LIMIT),
        name="combine_final_norm",
    )(dest_tiles, dest_tiles, x2, route, yb, gain)


def _gelu_tanh(x):
    c = 0.7978845608028654
    return x * (0.5 * (1.0 + jnp.tanh(c * (x + 0.044715 * (x * x * x)))))


def _mixer_kernel(proj_ref, wgk_ref, bgk_ref, glan_ref, gmn_ref, wsp_ref, bsp_ref, wconv_ref,
                  out_ref, st_ref, hc_ref, lcat_ref, wm_ref, *, seq_start, first_step, between):
    TS = TS_MIX
    n_gla = TS // GLA_CHUNK
    n_gm = TS // GMLP_CHUNK

    @pl.when(seq_start)
    def _():
        st_ref[...] = jnp.zeros_like(st_ref)
        hc_ref[...] = jnp.zeros_like(hc_ref)

    @pl.when(first_step)
    def _():
        r = lax.broadcasted_iota(jnp.int32, (TS, TS), 0)
        c = lax.broadcasted_iota(jnp.int32, (TS, TS), 1)
        keep = ((r // GLA_CHUNK) == (c // GLA_CHUNK)) & (c <= r)
        lcat_ref[...] = jnp.where(keep, 1.0, 0.0).astype(BF16)
        t = lax.broadcasted_iota(jnp.int32, (GMLP_CHUNK, GMLP_HEADS * GMLP_CHUNK), 0)
        s = lax.broadcasted_iota(jnp.int32, (GMLP_CHUNK, GMLP_HEADS * GMLP_CHUNK), 1) % GMLP_CHUNK
        wm_ref[...] = jnp.where(s <= t, wsp_ref[...], 0.0).astype(BF16)

    lane256 = lax.broadcasted_iota(jnp.int32, (1, GLA_KDIM), 1)

    q = proj_ref[:, C_Q:C_Q + GLA_KDIM].astype(F32)
    k = proj_ref[:, C_K:C_K + GLA_KDIM].astype(F32)
    v_b = proj_ref[:, C_V:C_V + GLA_WIDTH]
    z = _dot(proj_ref[:, C_GKL:C_GKL + LANES], wgk_ref[...]) + bgk_ref[...]
    gk = (jnp.minimum(z, 0.0) - jnp.log1p(jnp.exp(-jnp.abs(z)))) * (1.0 / GLA_GATE_NORM)
    gk_hi, gk_lo = _split_bf16(gk)
    cs = _dot(lcat_ref[...], jnp.concatenate([gk_hi, gk_lo], axis=1))
    b = cs[:, :GLA_KDIM] + cs[:, GLA_KDIM:]
    b_last = [b[(c + 1) * GLA_CHUNK - 1:(c + 1) * GLA_CHUNK, :] for c in range(n_gla)]
    bl = jnp.concatenate(
        [jnp.broadcast_to(t, (GLA_CHUNK, GLA_KDIM)) for t in b_last], axis=0)
    q_dec = (q * (GLA_DK ** -0.5)) * jnp.exp(b)
    k_inv = (k * jnp.exp(-b)).astype(BF16)
    k_dec = (k * jnp.exp(bl - b)).astype(BF16)
    q_dec_b = q_dec.astype(BF16)

    zero_b = jnp.zeros_like(q_dec_b)
    q_stack = jnp.concatenate(
        [jnp.where((lane256 // GLA_DK) == h, q_dec_b, zero_b) for h in range(GLA_HEADS)], axis=0)
    scores = lax.dot_general(q_stack, k_inv, (((1,), (1,)), ((), ())),
                             preferred_element_type=F32)
    rt = lax.broadcasted_iota(jnp.int32, (TS, TS), 0)
    ct = lax.broadcasted_iota(jnp.int32, (TS, TS), 1)
    causal = ((rt // GLA_CHUNK) == (ct // GLA_CHUNK)) & (ct <= rt)
    o_heads = []
    for h in range(GLA_HEADS):
        p_h = jnp.where(causal, scores[h * TS:(h + 1) * TS, :], 0.0).astype(BF16)
        o_heads.append(_dot(p_h, v_b[:, h * GLA_DV:(h + 1) * GLA_DV]))

    sr = lax.broadcasted_iota(jnp.int32, (GLA_WIDTH, GLA_KDIM), 0) // GLA_DV
    sc = lax.broadcasted_iota(jnp.int32, (GLA_WIDTH, GLA_KDIM), 1) // GLA_DK
    bd_mask = sr == sc
    o_inter = []
    for c in range(n_gla):
        rows = slice(c * GLA_CHUNK, (c + 1) * GLA_CHUNK)
        st = st_ref[...]
        o_inter.append(lax.dot_general(q_dec_b[rows], st.astype(BF16), (((1,), (1,)), ((), ())),
                                       preferred_element_type=F32))
        upd = lax.dot_general(v_b[rows], k_dec[rows], (((0,), (0,)), ((), ())),
                              preferred_element_type=F32)
        decay = jnp.exp(b_last[c])
        st_ref[...] = st * decay + jnp.where(bd_mask, upd, 0.0)
    o_inter = jnp.concatenate(o_inter, axis=0)

    for h in range(GLA_HEADS):
        cols = slice(h * GLA_DV, (h + 1) * GLA_DV)
        o = o_heads[h] + o_inter[:, cols]
        o = o * lax.rsqrt(jnp.mean(o * o, axis=-1, keepdims=True) + RMS_EPS) * glan_ref[...]
        g = proj_ref[:, C_G + h * GLA_DV:C_G + (h + 1) * GLA_DV].astype(F32)
        out_ref[:, cols] = (o * (g * (1.0 / (1.0 + jnp.exp(-g))))).astype(out_ref.dtype)

    between("gla_done")
    u = _gelu_tanh(proj_ref[:, C_U:C_U + GMLP_WIDTH].astype(F32))
    vg = _gelu_tanh(proj_ref[:, C_VG:C_VG + GMLP_WIDTH].astype(F32))
    hr = lax.broadcasted_iota(jnp.int32, (GMLP_WIDTH, GMLP_WIDTH), 0) // GMLP_DH
    hcn = lax.broadcasted_iota(jnp.int32, (GMLP_WIDTH, GMLP_WIDTH), 1) // GMLP_DH
    head_mean = jnp.where(hr == hcn, 1.0 / GMLP_DH, 0.0).astype(BF16)
    sq_hi, sq_lo = _split_bf16(vg * vg)
    ms = _dot(sq_hi, head_mean) + _dot(sq_lo, head_mean)
    v32 = vg * lax.rsqrt(ms + RMS_EPS) * gmn_ref[...]
    for c in range(n_gm):
        rows = slice(c * GMLP_CHUNK, (c + 1) * GMLP_CHUNK)
        vc = v32[rows].astype(BF16)
        zc = jnp.zeros_like(vc)
        rhs = jnp.concatenate(
            [jnp.where((lane256 // GMLP_DH) == h, vc, zc) for h in range(GMLP_HEADS)], axis=0)
        mixed = _dot(wm_ref[...], rhs) + bsp_ref[...]
        out_ref[rows, GLA_WIDTH:GLA_WIDTH + GMLP_WIDTH] = (u[rows] * mixed).astype(out_ref.dtype)

    between("gmlp_done")
    hcv = (proj_ref[:, C_CG:C_CG + CONV_WIDTH].astype(F32)
           * proj_ref[:, C_X:C_X + CONV_WIDTH].astype(F32))
    hc_ref[8:8 + TS, :] = hcv
    y = (wconv_ref[2:3, :] * hcv + wconv_ref[1:2, :] * hc_ref[7:7 + TS, :]
         + wconv_ref[0:1, :] * hc_ref[6:6 + TS, :])
    out_ref[:, GLA_WIDTH + GMLP_WIDTH:] = (
        proj_ref[:, C_BG:C_BG + CONV_WIDTH].astype(F32) * y).astype(out_ref.dtype)
    hc_ref[0:8, :] = hc_ref[TS:TS + 8, :]


_MIXER_SCRATCH = [
    pltpu.VMEM((GLA_WIDTH, GLA_KDIM), F32),
    pltpu.VMEM((TS_MIX + 8, CONV_WIDTH), F32),
    pltpu.VMEM((TS_MIX, TS_MIX), BF16),
    pltpu.VMEM((GMLP_CHUNK, GMLP_HEADS * GMLP_CHUNK), BF16),
]


N_MIX_PARAMS = 7
PROJ_CHUNKS_AT = {"gla_done": 4, "gmlp_done": 4}


def _front_kernel(*refs, tiles_per_seq, combine):
    refs = list(refs)
    if combine:
        dcur_ref, dnxt_ref, x_ref, route_ref, yb_ref = refs[:5]
        del refs[:5]
    else:
        x_ref = refs.pop(0)
    gain_ref, wt_ref = refs[:2]
    mix_refs = refs[2:2 + N_MIX_PARAMS]
    del refs[:2 + N_MIX_PARAMS]
    if combine:
        xo_ref, out_ref, buf_ref, sem_ref = refs[:4]
        del refs[:4]
    else:
        out_ref = refs.pop(0)
    st_ref, hc_ref, lcat_ref, wm_ref, wb_ref, pcur_ref, pnext_ref = refs
    s = pl.program_id(0)

    @pl.when(s == 0)
    def _():
        _stage_w_in(wt_ref, wb_ref)
        pcur_ref[...] = jnp.zeros_like(pcur_ref)

    col_chunks = [(c0, min(c0 + PROJ_CHUNK, D_PROJ)) for c0 in range(0, D_PROJ, PROJ_CHUNK)]
    work = list(zip(_prefetch_groups(len(col_chunks)), col_chunks))

    def project(h, n):
        for _ in range(min(n, len(work))):
            rows, (c0, c1) = work.pop(0)
            if combine:
                _gather_start(dnxt_ref, yb_ref, buf_ref, sem_ref, 1 - lax.rem(s, 2), rows)
            pnext_ref[:, c0:c1] = _dot(h, wb_ref[:, c0:c1]).astype(BF16)

    mixers = functools.partial(
        _mixer_kernel, pcur_ref, *mix_refs, out_ref, st_ref, hc_ref, lcat_ref, wm_ref,
        seq_start=lax.rem(jnp.maximum(s - 1, 0), tiles_per_seq) == 0, first_step=s == 0)
    if combine:
        x = _combined_residual(dcur_ref, x_ref, route_ref, yb_ref, buf_ref, sem_ref)
        xo_ref[...] = x
        h = _rms(x, gain_ref[...]).astype(BF16)
        mixers(between=lambda site: project(h, PROJ_CHUNKS_AT[site]))
        project(h, len(work))
        _drain_last_prefetch(yb_ref, buf_ref, sem_ref)
    else:
        mixers(between=lambda site: None)
        project(_rms(x_ref[...], gain_ref[...]).astype(BF16), len(work))
    pcur_ref[...] = pnext_ref[...]


def _front(x, gain, w_in_t, layer, mix_params, batch, seq, moe=None):
    n_seq = seq // TS_MIX
    n = batch * n_seq
    T = batch * seq
    cur = lambda s: jnp.minimum(s, n - 1)
    full = lambda shape: pl.BlockSpec(shape, lambda s: (0,) * len(shape))
    row = lambda w, f: pl.BlockSpec((TM, w), lambda s: (f(s), 0))
    in_specs, args = [row(D_MODEL, cur)], [x]
    out_specs = [row(D_MODEL, lambda s: jnp.maximum(s - 1, 0))]
    out_shape = [jax.ShapeDtypeStruct((T, D_MODEL), BF16)]
    scratch = list(_MIXER_SCRATCH)
    if moe is not None:
        smem_tile = lambda f: pl.BlockSpec((1, 1, TOP_K * TM), lambda s: (f(s), 0, 0),
                                           memory_space=pltpu.SMEM)
        in_specs = [smem_tile(cur), smem_tile(lambda s: jnp.minimum(s + 1, n - 1))] + in_specs + [
            row(LANES, cur), pl.BlockSpec(memory_space=pl.ANY)]
        args = [moe["dest_tiles"], moe["dest_tiles"]] + args + [moe["route"], moe["yb"]]
        out_specs = [row(D_MODEL, cur)] + out_specs
        out_shape = [jax.ShapeDtypeStruct((T, D_MODEL), F32)] + out_shape
        scratch = _COMBINE_SCRATCH + scratch
    in_specs += [
        full((1, D_MODEL)),
        pl.BlockSpec((1, D_IN, D_MODEL), lambda s: (layer, 0, 0), pipeline_mode=pl.Buffered(1)),
        full((LANES, GLA_KDIM)), full((1, GLA_KDIM)), full((1, GLA_DV)), full((1, GMLP_WIDTH)),
        full((GMLP_CHUNK, GMLP_HEADS * GMLP_CHUNK)), full((GMLP_CHUNK, GMLP_WIDTH)),
        full((8, CONV_WIDTH)),
    ]
    scratch += [pltpu.VMEM((D_MODEL, D_PROJ), BF16), pltpu.VMEM((TS_MIX, D_PROJ), BF16),
                pltpu.VMEM((TS_MIX, D_PROJ), BF16)]
    return pl.pallas_call(
        functools.partial(_front_kernel, tiles_per_seq=n_seq, combine=moe is not None),
        grid=(n + 1,),
        in_specs=in_specs,
        out_specs=out_specs if moe is not None else out_specs[0],
        out_shape=out_shape if moe is not None else out_shape[0],
        scratch_shapes=scratch,
        compiler_params=pltpu.CompilerParams(
            dimension_semantics=("arbitrary",), vmem_limit_bytes=VMEM_LIMIT),
        name="front",
    )(*args, gain, w_in_t, *mix_params)


def _row_tiles_store(tiles_ref, x):
    rows = x.shape[0]
    flat = tiles_ref.reshape(rows * ROW_TILES, LANES)
    for c in range(ROW_TILES):
        flat[pl.ds(c, rows, stride=ROW_TILES), :] = x[:, c * LANES:(c + 1) * LANES]


def _row_tiles_chunks(tiles_ref, rows):
    flat = tiles_ref.reshape(rows * ROW_TILES, LANES)
    return [flat[pl.ds(c, rows, stride=ROW_TILES), :] for c in range(ROW_TILES)]


def _out_router_kernel(mix_ref, x_ref, wo_ref, gain_ref, wrc_ref, br_ref,
                       x2_ref, h2_ref, route_ref, route_t_ref, cnt_ref, tri_ref, wob_ref, lg_ref):
    i = pl.program_id(0)

    @pl.when(i == 0)
    def _():
        cnt_ref[...] = jnp.zeros_like(cnt_ref)
        lg_ref[...] = jnp.zeros_like(lg_ref)
        r = lax.broadcasted_iota(jnp.int32, (TM, TM), 0)
        c = lax.broadcasted_iota(jnp.int32, (TM, TM), 1)
        tri_ref[...] = jnp.where(c < r, 1.0, 0.0).astype(BF16)
        for r0 in range(0, D_MODEL, W_PREP_ROWS):
            wob_ref[r0:r0 + W_PREP_ROWS, :] = wo_ref[0, r0:r0 + W_PREP_ROWS, :].astype(BF16)

    lg = lg_ref[...]
    half = D_MODEL // 2
    mix = mix_ref[...]
    x2_a = x_ref[:, :half] + _dot(mix, wob_ref[:, :half])

    lane = lax.broadcasted_iota(jnp.int32, (TM, LANES), 1).astype(F32)
    neg = -jnp.inf
    is_g = lane < N_GROUPS
    gl = jnp.where(is_g, lg, neg)
    gmax = jnp.max(gl, axis=1, keepdims=True)
    g_top = jnp.min(jnp.where(gl == gmax, lane, float(LANES)), axis=1, keepdims=True)
    g_w = 1.0 / jnp.sum(jnp.where(is_g, jnp.exp(lg - gmax), 0.0), axis=1, keepdims=True)
    first = N_GROUPS + EXPERTS_PER_GROUP * g_top
    el = jnp.where((lane >= first) & (lane < first + EXPERTS_PER_GROUP), lg, neg)
    m1 = jnp.max(el, axis=1, keepdims=True)
    i1 = jnp.min(jnp.where(el == m1, lane, float(LANES)), axis=1, keepdims=True)
    el2 = jnp.where(lane == i1, neg, el)
    m2 = jnp.max(el2, axis=1, keepdims=True)
    i2 = jnp.min(jnp.where(el2 == m2, lane, float(LANES)), axis=1, keepdims=True)
    ratio = jnp.exp(m2 - m1)
    w1 = g_w / (1.0 + ratio)
    w2 = w1 * ratio

    x2_b = x_ref[:, half:] + _dot(mix, wob_ref[:, half:])

    oh1 = jnp.where(lane == i1, 1.0, 0.0)
    oh2 = jnp.where(lane == i2, 1.0, 0.0)
    oh = jnp.where(i > 0, oh1 + oh2, 0.0)
    before = _dot(tri_ref[...], oh.astype(BF16)) + cnt_ref[0:1, :]
    rank1 = jnp.sum(oh1 * before, axis=1, keepdims=True)
    rank2 = jnp.sum(oh2 * before, axis=1, keepdims=True)
    cnt_ref[...] = cnt_ref[...] + jnp.sum(oh, axis=0, keepdims=True)

    rec = jnp.zeros((TM, LANES), F32)
    for col, val in ((R_E, i1 - N_GROUPS), (R_E + 1, i2 - N_GROUPS), (R_RANK, rank1),
                     (R_RANK + 1, rank2), (R_W, w1), (R_W + 1, w2)):
        rec = jnp.where(lane == col, val, rec)
    route_ref[...] = rec
    route_t_ref[0] = rec.T[0:SUBLANES, :]

    x2 = jnp.concatenate([x2_a, x2_b], axis=1)
    x2_ref[...] = x2
    h = _rms(x2, gain_ref[...])
    h_hi, h_lo = _split_bf16(h)
    h2_ref[...] = h_hi
    hh_hl = _dot(h_hi, wrc_ref[...])
    lg_ref[...] = (hh_hl[:, :ROUTER_COLS] + hh_hl[:, ROUTER_COLS:]
                   + _dot(h_lo, wrc_ref[:, :ROUTER_COLS]) + br_ref[...])


def _out_router(mixed, x, w_out, layer, gain, wr_cat, br):
    T = x.shape[0]
    n = T // TM
    row = lambda w: pl.BlockSpec((TM, w), lambda i: (jnp.minimum(i, n - 1), 0))
    lag = lambda i: jnp.maximum(i - 1, 0)
    full = lambda shape: pl.BlockSpec(shape, lambda i: (0, 0))
    wo_spec = pl.BlockSpec((1, D_MODEL, D_MODEL), lambda i: (layer, 0, 0),
                           pipeline_mode=pl.Buffered(1))
    return pl.pallas_call(
        _out_router_kernel,
        grid=(n + 1,),
        in_specs=[row(D_MODEL), row(D_MODEL), wo_spec, full((1, D_MODEL)),
                  full((D_MODEL, 2 * ROUTER_COLS)), full((1, ROUTER_COLS))],
        out_specs=[row(D_MODEL), row(D_MODEL),
                   pl.BlockSpec((TM, LANES), lambda i: (lag(i), 0)),
                   pl.BlockSpec((1, SUBLANES, TM), lambda i: (lag(i), 0, 0)), full((8, LANES))],
        out_shape=[jax.ShapeDtypeStruct((T, D_MODEL), F32),
                   jax.ShapeDtypeStruct((T, D_MODEL), BF16),
                   jax.ShapeDtypeStruct((T, LANES), F32),
                   jax.ShapeDtypeStruct((T // TM, SUBLANES, TM), F32),
                   jax.ShapeDtypeStruct((8, LANES), F32)],
        scratch_shapes=[pltpu.VMEM((TM, TM), BF16), pltpu.VMEM((D_MODEL, D_MODEL), BF16),
                        pltpu.VMEM((TM, ROUTER_COLS), F32)],
        compiler_params=pltpu.CompilerParams(
            dimension_semantics=("arbitrary",), vmem_limit_bytes=VMEM_LIMIT),
        name="out_router",
    )(mixed, x, w_out, gain, wr_cat, br)


def _dispatch_kernel(fill_ref, nu_ref, dest_ref, h_ref, xb_ref, zero_ref, sem_ref, zsem_ref,
                     stage_ref):
    i = pl.program_id(0)
    par = lax.rem(i, 2)
    last = i == pl.num_programs(0) - 1
    n_blocks = xb_ref.shape[0] // MOE_BLK
    spare_fills = [(j >= nu_ref[0], pltpu.make_async_copy(
        zero_ref, xb_ref.at[pl.ds(j * MOE_BLK, MOE_BLK)], zsem_ref.at[1]))
        for j in range(n_blocks - N_EXPERTS, n_blocks)]

    @pl.when(i == 0)
    def _():
        zero_ref[...] = jnp.zeros_like(zero_ref)
        fills = [(fill_ref[e] >= 0, pltpu.make_async_copy(
            zero_ref, xb_ref.at[pl.ds(pl.multiple_of(jnp.maximum(fill_ref[e], 0), MOE_BLK), MOE_BLK)],
            zsem_ref.at[0])) for e in range(N_EXPERTS)]
        for cond, f in fills + spare_fills:
            pl.when(cond)(f.start)
        for cond, f in fills:
            pl.when(cond)(f.wait)

    _row_tiles_store(stage_ref.at[par], h_ref[...].astype(F32))
    for r in range(TM):
        for k in range(TOP_K):
            pltpu.make_async_copy(stage_ref.at[par, r], xb_ref.at[dest_ref[0, 0, k * TM + r]],
                                  sem_ref.at[par]).start(priority=k)

    def wait_tile(p):
        for _ in range(TOP_K):
            pltpu.make_async_copy(stage_ref.at[p], xb_ref.at[pl.ds(0, TM)], sem_ref.at[p]).wait()

    pl.when(i > 0)(lambda: wait_tile(1 - par))
    @pl.when(last)
    def _():
        wait_tile(par)
        for cond, f in spare_fills:
            pl.when(cond)(f.wait)


def _dispatch(fill_start, n_used, dest_tiles, h2, n_rows):
    T = h2.shape[0]
    grid_spec = pltpu.PrefetchScalarGridSpec(
        num_scalar_prefetch=2,
        grid=(T // TM,),
        in_specs=[
            pl.BlockSpec((1, 1, TOP_K * TM), lambda i, fs, nu: (i, 0, 0), memory_space=pltpu.SMEM),
            pl.BlockSpec((TM, D_MODEL), lambda i, fs, nu: (i, 0)),
        ],
        out_specs=pl.BlockSpec(memory_space=pl.ANY),
        scratch_shapes=[pltpu.VMEM((MOE_BLK, ROW_TILES, LANES), F32),
                        pltpu.SemaphoreType.DMA((2,)), pltpu.SemaphoreType.DMA((2,)),
                        pltpu.VMEM((2, TM, ROW_TILES, LANES), F32)],
    )
    return pl.pallas_call(
        _dispatch_kernel,
        grid_spec=grid_spec,
        out_shape=jax.ShapeDtypeStruct((n_rows, ROW_TILES, LANES), F32),
        compiler_params=pltpu.CompilerParams(dimension_semantics=("arbitrary",)),
        name="dispatch",
    )(fill_start, n_used, dest_tiles, h2)


BLOCK_COPY_PARTS = 4
X_SLOTS = 4
Y_SLOTS = 3


class _CopyGroup:
    def __init__(self, copies):
        self.copies = copies

    def start(self):
        for n, c in enumerate(self.copies):
            c.start(priority=n % 2)

    def wait(self):
        for c in self.copies:
            c.wait()


def _expert_kernel(be_ref, nxt_ref, nu_ref, xb_ref, wg_ref, wu_ref, wd_ref, yb_ref,
                   xbuf_ref, ybuf_ref, wgs_ref, wus_ref, wds_ref, wgb_ref, wub_ref, wdb_ref,
                   xsem_ref, ysem_ref, wsem_ref, zbuf_ref, zsem_ref, *, layer, n_blocks):
    n_used = nu_ref[0]

    part = MOE_BLK // BLOCK_COPY_PARTS

    def x_copy(j, slot):
        return _CopyGroup([pltpu.make_async_copy(
            xb_ref.at[pl.ds(j * MOE_BLK + p * part, part)],
            xbuf_ref.at[slot, pl.ds(p * part, part)], xsem_ref.at[slot])
            for p in range(BLOCK_COPY_PARTS)])

    def y_copy(j, slot):
        return _CopyGroup([pltpu.make_async_copy(
            ybuf_ref.at[slot, pl.ds(p * part, part)],
            yb_ref.at[pl.ds(j * MOE_BLK + p * part, part)], ysem_ref.at[slot])
            for p in range(BLOCK_COPY_PARTS)])

    def w_copies(e, ws):
        return [pltpu.make_async_copy(src.at[layer, e], dst.at[ws], wsem_ref.at[ws])
                for src, dst in ((wg_ref, wgs_ref), (wu_ref, wus_ref), (wd_ref, wds_ref))]

    def fill_copy(j):
        return pltpu.make_async_copy(zbuf_ref, yb_ref.at[pl.ds(j * MOE_BLK, MOE_BLK)], zsem_ref)

    for j0 in range(X_SLOTS - 1):
        x_copy(j0, j0).start()
    for c in w_copies(be_ref[0], 0):
        c.start()

    zbuf_ref[...] = jnp.zeros_like(zbuf_ref)

    def fill(j, carry):
        fill_copy(j).start()
        return carry

    lax.fori_loop(n_used, n_blocks, fill, 0)

    def block(j, ws):
        slot = lax.rem(j, Y_SLOTS)
        xslot = lax.rem(j, X_SLOTS)
        first = (j == 0) | (be_ref[j] != be_ref[jnp.maximum(j - 1, 0)])
        ws = jnp.where(first & (j > 0), 1 - ws, ws)

        @pl.when(first)
        def _():
            for c in w_copies(be_ref[j], ws):
                c.wait()
            for r0 in range(0, D_MODEL, W_PREP_ROWS):
                rows = slice(r0, r0 + W_PREP_ROWS)
                wgb_ref[rows, :] = wgs_ref[ws, rows, :].astype(BF16)
                wub_ref[rows, :] = wus_ref[ws, rows, :].astype(BF16)
            for r0 in range(0, D_EXPERT, W_PREP_ROWS):
                rows = slice(r0, r0 + W_PREP_ROWS)
                wdb_ref[rows, :] = wds_ref[ws, rows, :].astype(BF16)

            @pl.when(nxt_ref[j] >= 0)
            def _():
                for c in w_copies(nxt_ref[j], 1 - ws):
                    c.start()

        ahead = j + X_SLOTS - 1

        @pl.when(ahead < n_used)
        def _():
            x_copy(ahead, lax.rem(ahead, X_SLOTS)).start()

        x_copy(j, xslot).wait()
        x = jnp.concatenate(
            [c.astype(BF16) for c in _row_tiles_chunks(xbuf_ref.at[xslot], MOE_BLK)], axis=1)
        g = _dot(x, wgb_ref[...])
        u = _dot(x, wub_ref[...])
        h = (g * (1.0 / (1.0 + jnp.exp(-g)))) * u
        y = _dot(h.astype(BF16), wdb_ref[...])

        @pl.when(j >= Y_SLOTS)
        def _():
            y_copy(j - Y_SLOTS, slot).wait()

        _row_tiles_store(ybuf_ref.at[slot], y)
        y_copy(j, slot).start()
        return ws

    lax.fori_loop(0, n_used, block, jnp.int32(0))

    for back in range(Y_SLOTS, 0, -1):
        y_copy(n_used - back, lax.rem(n_used - back, Y_SLOTS)).wait()

    def fill_wait(j, carry):
        fill_copy(j).wait()
        return carry

    lax.fori_loop(n_used, n_blocks, fill_wait, 0)


def _experts(blk_exp, nxt_exp, n_used, xb, w_gate, w_up, w_down, layer):
    n_blocks = blk_exp.shape[0]
    any_spec = pl.BlockSpec(memory_space=pl.ANY)
    blk = (MOE_BLK, ROW_TILES, LANES)
    grid_spec = pltpu.PrefetchScalarGridSpec(
        num_scalar_prefetch=3,
        grid=(1,),
        in_specs=[any_spec, any_spec, any_spec, any_spec],
        out_specs=any_spec,
        scratch_shapes=[
            pltpu.VMEM((X_SLOTS,) + blk, F32), pltpu.VMEM((Y_SLOTS,) + blk, F32),
            pltpu.VMEM((2, D_MODEL, D_EXPERT), F32), pltpu.VMEM((2, D_MODEL, D_EXPERT), F32),
            pltpu.VMEM((2, D_EXPERT, D_MODEL), F32),
            pltpu.VMEM((D_MODEL, D_EXPERT), BF16), pltpu.VMEM((D_MODEL, D_EXPERT), BF16),
            pltpu.VMEM((D_EXPERT, D_MODEL), BF16),
            pltpu.SemaphoreType.DMA((X_SLOTS,)), pltpu.SemaphoreType.DMA((Y_SLOTS,)),
            pltpu.SemaphoreType.DMA((2,)),
            pltpu.VMEM(blk, F32), pltpu.SemaphoreType.DMA(()),
        ],
    )
    return pl.pallas_call(
        functools.partial(_expert_kernel, layer=layer, n_blocks=n_blocks),
        grid_spec=grid_spec,
        out_shape=jax.ShapeDtypeStruct((n_blocks * MOE_BLK, ROW_TILES, LANES), F32),
        compiler_params=pltpu.CompilerParams(
            dimension_semantics=("arbitrary",), vmem_limit_bytes=VMEM_LIMIT),
        name="experts",
    )(blk_exp, nxt_exp, n_used, xb, w_gate, w_up, w_down)


def _dispatch_tables(route_t, counts_rec, T):
    counts = counts_rec[0, N_GROUPS:N_GROUPS + N_EXPERTS].astype(jnp.int32)
    n_steps = (T * TOP_K) // MOE_BLK + N_EXPERTS
    nblk = (counts + MOE_BLK - 1) // MOE_BLK
    bend = jnp.cumsum(nblk)
    pstart = (bend - nblk) * MOE_BLK
    n_used = bend[-1]
    j = jnp.minimum(jnp.arange(n_steps, dtype=jnp.int32), n_used - 1)
    blk_exp = jnp.minimum(jnp.sum(j[:, None] >= bend[None, :], axis=1), N_EXPERTS - 1)
    n_rows = n_steps * MOE_BLK
    last_blk = jnp.where(counts > 0, (bend - 1) * MOE_BLK, -1)
    ids = jnp.arange(N_EXPERTS, dtype=jnp.int32)
    later = (ids[None, :] > ids[:, None]) & (nblk[None, :] > 0)
    nxt_of = jnp.min(jnp.where(later, ids[None, :], N_EXPERTS), axis=1)
    nxt_tab = jnp.where(nxt_of < N_EXPERTS, nxt_of, -1)
    nxt_exp = jnp.sum(jnp.where(blk_exp[:, None] == ids[None, :], nxt_tab[None, :], 0), axis=1)
    e = route_t[:, R_E:R_E + TOP_K, :].astype(jnp.int32)
    rank = route_t[:, R_RANK:R_RANK + TOP_K, :].astype(jnp.int32)
    seg = jnp.sum(jnp.where(e[..., None] == jnp.arange(N_EXPERTS), pstart, 0), axis=-1)
    dest = jnp.clip(seg + rank, 0, n_steps * MOE_BLK - 1)
    dest_tiles = dest.reshape(T // TM, 1, TOP_K * TM)
    return dict(dest_tiles=dest_tiles, fill_start=last_blk.astype(jnp.int32),
                blk_exp=blk_exp.astype(jnp.int32), nxt_exp=nxt_exp.astype(jnp.int32),
                n_used=n_used.reshape(1).astype(jnp.int32), n_rows=n_rows)


def _prep_layer(l, w_gk_up, b_gk, gla_norm, gmlp_norm, w_spatial, b_spatial, w_conv,
                w_router_group, b_router_group, w_router_expert, b_router_expert):
    wgk = jnp.concatenate(
        [w_gk_up[l], jnp.zeros((LANES - GLA_GATE_RANK, GLA_KDIM), F32)], axis=0).astype(BF16)
    wsp = w_spatial[l].transpose(1, 0, 2).reshape(GMLP_CHUNK, GMLP_HEADS * GMLP_CHUNK)
    bsp = jnp.repeat(b_spatial[l].T, GMLP_DH, axis=1)
    wconv = jnp.concatenate([w_conv[l], jnp.zeros((8 - CONV_K, CONV_WIDTH), F32)], axis=0)
    wr = jnp.concatenate(
        [w_router_group[l], w_router_expert[l],
         jnp.zeros((D_MODEL, ROUTER_COLS - N_GROUPS - N_EXPERTS), F32)], axis=1)
    wr_hi = wr.astype(BF16)
    wr_lo = (wr - wr_hi.astype(F32)).astype(BF16)
    br = jnp.concatenate(
        [b_router_group[l], b_router_expert[l],
         jnp.zeros((ROUTER_COLS - N_GROUPS - N_EXPERTS,), F32)])[None, :]
    return dict(
        wgk=wgk, bgk=b_gk[l][None, :], glan=gla_norm[l][None, :], gmn=gmlp_norm[l][None, :],
        wsp=wsp, bsp=bsp, wconv=wconv, wr_cat=jnp.concatenate([wr_hi, wr_lo], axis=1), br=br)


def kernel(x, attn_norm, w_in, w_gk_up, b_gk, gla_norm, gmlp_norm, w_spatial, b_spatial, w_conv, w_out, ffn_norm, w_router_group, b_router_group, w_router_expert, b_router_expert, w_gate, w_up, w_down, final_norm):
    B, S, D = x.shape
    T = B * S
    depth = w_in.shape[0]
    xr = x.reshape(T, D)
    w_in_t = jnp.swapaxes(w_in, 1, 2)
    moe = None
    for l in range(depth):
        p = _prep_layer(l, w_gk_up, b_gk, gla_norm, gmlp_norm, w_spatial, b_spatial, w_conv,
                        w_router_group, b_router_group, w_router_expert, b_router_expert)
        mix_params = (p["wgk"], p["bgk"], p["glan"], p["gmn"], p["wsp"], p["bsp"], p["wconv"])
        if moe is None:
            mixed = _front(xr, attn_norm[l][None, :], w_in_t, l, mix_params, B, S)
        else:
            xr, mixed = _front(moe["x2"], attn_norm[l][None, :], w_in_t, l, mix_params, B, S, moe)
        x2, h2, route, route_t, counts_rec = _out_router(
            mixed, xr, w_out, l, ffn_norm[l][None, :], p["wr_cat"], p["br"])
        moe = _dispatch_tables(route_t, counts_rec, T)
        xb = _dispatch(moe["fill_start"], moe["n_used"], moe["dest_tiles"], h2, moe["n_rows"])
        yb = _experts(moe["blk_exp"], moe["nxt_exp"], moe["n_used"], xb, w_gate, w_up, w_down, l)
        moe.update(x2=x2, route=route, yb=yb)
    out = _combine_final_norm(moe["dest_tiles"], moe["x2"], moe["route"], moe["yb"],
                              final_norm[None, :])
    return out.reshape(B, S, D)
```

```python
import functools

import jax
import jax.numpy as jnp
from jax import lax
from jax.experimental import pallas as pl
from jax.experimental.pallas import tpu as pltpu

F32 = jnp.float32
BF16 = jnp.bfloat16

D_MODEL = 1024
RMS_EPS = 1e-6
GLA_HEADS = 4
GLA_WIDTH = 512
GLA_DV = 128
GLA_DK = 64
GLA_KDIM = 256
GLA_GATE_RANK = 16
GLA_GATE_NORM = 16.0
GLA_CHUNK = 64
GMLP_HEADS = 4
GMLP_WIDTH = 256
GMLP_DH = 64
GMLP_CHUNK = 128
CONV_WIDTH = 256
CONV_K = 3
N_GROUPS = 4
EXPERTS_PER_GROUP = 8
N_EXPERTS = 32
TOP_K = 2
D_EXPERT = 256

LANES = 128
C_Q, C_K, C_V, C_G = 0, 256, 512, 1024
C_U, C_VG, C_X, C_BG, C_CG, C_GKL = 1536, 1792, 2048, 2304, 2560, 2816
D_PROJ = C_GKL + LANES
D_IN = C_GKL + GLA_GATE_RANK

TM = 256
TS_MIX = TM
MOE_BLK = 256
ROUTER_COLS = LANES
SUBLANES = 8
ROW_TILES = D_MODEL // LANES
assert ROW_TILES == SUBLANES
VMEM_LIMIT = 56 * 1024 * 1024
R_E, R_RANK, R_W = 0, 2, 4


def _dot(a, b):
    return jnp.dot(a, b, preferred_element_type=F32)


def _split_bf16(x):
    hi = x.astype(BF16)
    lo = (x - hi.astype(F32)).astype(BF16)
    return hi, lo


def _rms(x, gain):
    return x * lax.rsqrt(jnp.mean(x * x, axis=-1, keepdims=True) + RMS_EPS) * gain


W_PREP_ROWS = 128
PROJ_CHUNK = 256


def _stage_w_in(wt_ref, wb_ref):
    for c0 in range(0, C_GKL, LANES):
        src = c0 if c0 < C_U else c0 + GLA_GATE_RANK
        wb_ref[:, c0:c0 + LANES] = wt_ref[0, src:src + LANES, :].T.astype(BF16)
    low = jnp.concatenate([wt_ref[0, C_U:C_U + GLA_GATE_RANK, :],
                           jnp.zeros((LANES - GLA_GATE_RANK, D_MODEL), F32)], axis=0)
    wb_ref[:, C_GKL:D_PROJ] = low.T.astype(BF16)


def _row_gather_copy(yb_ref, buf_ref, sem_ref, slot, k, r, d):
    return pltpu.make_async_copy(yb_ref.at[d], buf_ref.at[slot, k, r], sem_ref.at[slot])


def _gather_start(dest_ref, yb_ref, buf_ref, sem_ref, slot, rows=range(TM)):
    for r in rows:
        for k in range(TOP_K):
            _row_gather_copy(yb_ref, buf_ref, sem_ref, slot, k, r,
                             dest_ref[0, 0, k * TM + r]).start(priority=k)


def _gather_wait(yb_ref, buf_ref, sem_ref, slot):
    for k in range(TOP_K):
        pltpu.make_async_copy(yb_ref.at[pl.ds(0, TM)], buf_ref.at[slot, k], sem_ref.at[slot]).wait()


def _combined_residual(dcur_ref, x_ref, route_ref, yb_ref, buf_ref, sem_ref):
    i = pl.program_id(0)
    slot = lax.rem(i, 2)

    @pl.when(i == 0)
    def _():
        _gather_start(dcur_ref, yb_ref, buf_ref, sem_ref, 0)

    _gather_wait(yb_ref, buf_ref, sem_ref, slot)
    w0 = route_ref[:, R_W:R_W + 1]
    w1 = route_ref[:, R_W + 1:R_W + 2]
    y0 = _row_tiles_chunks(buf_ref.at[slot, 0], TM)
    y1 = _row_tiles_chunks(buf_ref.at[slot, 1], TM)
    return jnp.concatenate(
        [x_ref[:, c * LANES:(c + 1) * LANES] + (w0 * y0[c] + w1 * y1[c]) for c in range(ROW_TILES)],
        axis=1)


def _prefetch_groups(n_groups):
    per = -(-TM // n_groups)
    return [range(g * per, min(TM, (g + 1) * per)) for g in range(n_groups)]


def _drain_last_prefetch(yb_ref, buf_ref, sem_ref):
    i = pl.program_id(0)

    @pl.when(i == pl.num_programs(0) - 1)
    def _():
        _gather_wait(yb_ref, buf_ref, sem_ref, 1 - lax.rem(i, 2))


def _combine_specs(n_tiles):
    smem_tile = lambda f: pl.BlockSpec((1, 1, TOP_K * TM), f, memory_space=pltpu.SMEM)
    return [
        smem_tile(lambda i: (i, 0, 0)),
        smem_tile(lambda i: (jnp.minimum(i + 1, n_tiles - 1), 0, 0)),
        pl.BlockSpec((TM, D_MODEL), lambda i: (i, 0)),
        pl.BlockSpec((TM, LANES), lambda i: (i, 0)),
        pl.BlockSpec(memory_space=pl.ANY),
    ]


_COMBINE_SCRATCH = [pltpu.VMEM((2, TOP_K, TM, ROW_TILES, LANES), F32),
                    pltpu.SemaphoreType.DMA((2,))]


def _combine_final_norm_kernel(dcur_ref, dnxt_ref, x_ref, route_ref, yb_ref, gain_ref,
                               o_ref, buf_ref, sem_ref):
    _gather_start(dnxt_ref, yb_ref, buf_ref, sem_ref, 1 - lax.rem(pl.program_id(0), 2))
    x = _combined_residual(dcur_ref, x_ref, route_ref, yb_ref, buf_ref, sem_ref)
    o_ref[...] = _rms(x, gain_ref[...])
    _drain_last_prefetch(yb_ref, buf_ref, sem_ref)


def _combine_final_norm(dest_tiles, x2, route, yb, gain):
    T = x2.shape[0]
    n_tiles = T // TM
    return pl.pallas_call(
        _combine_final_norm_kernel,
        grid=(n_tiles,),
        in_specs=_combine_specs(n_tiles) + [pl.BlockSpec((1, D_MODEL), lambda i: (0, 0))],
        out_specs=pl.BlockSpec((TM, D_MODEL), lambda i: (i, 0)),
        out_shape=jax.ShapeDtypeStruct((T, D_MODEL), F32),
        scratch_shapes=_COMBINE_SCRATCH,
        compiler_params=pltpu.CompilerParams(
            dimension_semantics=("arbitrary",), vmem_limit_bytes=VMEM_LIMIT),
        name="combine_final_norm",
    )(dest_tiles, dest_tiles, x2, route, yb, gain)


def _gelu_tanh(x):
    c = 0.7978845608028654
    return x * (0.5 * (1.0 + jnp.tanh(c * (x + 0.044715 * (x * x * x)))))


def _mixer_kernel(proj_ref, wgk_ref, bgk_ref, glan_ref, gmn_ref, wsp_ref, bsp_ref, wconv_ref,
                  out_ref, st_ref, hc_ref, lcat_ref, wm_ref, *, seq_start, first_step, between):
    TS = TS_MIX
    n_gla = TS // GLA_CHUNK
    n_gm = TS // GMLP_CHUNK

    @pl.when(seq_start)
    def _():
        st_ref[...] = jnp.zeros_like(st_ref)
        hc_ref[...] = jnp.zeros_like(hc_ref)

    @pl.when(first_step)
    def _():
        r = lax.broadcasted_iota(jnp.int32, (TS, TS), 0)
        c = lax.broadcasted_iota(jnp.int32, (TS, TS), 1)
        keep = ((r // GLA_CHUNK) == (c // GLA_CHUNK)) & (c <= r)
        lcat_ref[...] = jnp.where(keep, 1.0, 0.0).astype(BF16)
        t = lax.broadcasted_iota(jnp.int32, (GMLP_CHUNK, GMLP_HEADS * GMLP_CHUNK), 0)
        s = lax.broadcasted_iota(jnp.int32, (GMLP_CHUNK, GMLP_HEADS * GMLP_CHUNK), 1) % GMLP_CHUNK
        wm_ref[...] = jnp.where(s <= t, wsp_ref[...], 0.0).astype(BF16)

    lane256 = lax.broadcasted_iota(jnp.int32, (1, GLA_KDIM), 1)

    q = proj_ref[:, C_Q:C_Q + GLA_KDIM].astype(F32)
    k = proj_ref[:, C_K:C_K + GLA_KDIM].astype(F32)
    v_b = proj_ref[:, C_V:C_V + GLA_WIDTH]
    z = _dot(proj_ref[:, C_GKL:C_GKL + LANES], wgk_ref[...]) + bgk_ref[...]
    gk = (jnp.minimum(z, 0.0) - jnp.log1p(jnp.exp(-jnp.abs(z)))) * (1.0 / GLA_GATE_NORM)
    gk_hi, gk_lo = _split_bf16(gk)
    cs = _dot(lcat_ref[...], jnp.concatenate([gk_hi, gk_lo], axis=1))
    b = cs[:, :GLA_KDIM] + cs[:, GLA_KDIM:]
    b_last = [b[(c + 1) * GLA_CHUNK - 1:(c + 1) * GLA_CHUNK, :] for c in range(n_gla)]
    bl = jnp.concatenate(
        [jnp.broadcast_to(t, (GLA_CHUNK, GLA_KDIM)) for t in b_last], axis=0)
    q_dec = (q * (GLA_DK ** -0.5)) * jnp.exp(b)
    k_inv = (k * jnp.exp(-b)).astype(BF16)
    k_dec = (k * jnp.exp(bl - b)).astype(BF16)
    q_dec_b = q_dec.astype(BF16)

    zero_b = jnp.zeros_like(q_dec_b)
    q_stack = jnp.concatenate(
        [jnp.where((lane256 // GLA_DK) == h, q_dec_b, zero_b) for h in range(GLA_HEADS)], axis=0)
    scores = lax.dot_general(q_stack, k_inv, (((1,), (1,)), ((), ())),
                             preferred_element_type=F32)
    rt = lax.broadcasted_iota(jnp.int32, (TS, TS), 0)
    ct = lax.broadcasted_iota(jnp.int32, (TS, TS), 1)
    causal = ((rt // GLA_CHUNK) == (ct // GLA_CHUNK)) & (ct <= rt)
    o_heads = []
    for h in range(GLA_HEADS):
        p_h = jnp.where(causal, scores[h * TS:(h + 1) * TS, :], 0.0).astype(BF16)
        o_heads.append(_dot(p_h, v_b[:, h * GLA_DV:(h + 1) * GLA_DV]))

    sr = lax.broadcasted_iota(jnp.int32, (GLA_WIDTH, GLA_KDIM), 0) // GLA_DV
    sc = lax.broadcasted_iota(jnp.int32, (GLA_WIDTH, GLA_KDIM), 1) // GLA_DK
    bd_mask = sr == sc
    o_inter = []
    for c in range(n_gla):
        rows = slice(c * GLA_CHUNK, (c + 1) * GLA_CHUNK)
        st = st_ref[...]
        o_inter.append(lax.dot_general(q_dec_b[rows], st.astype(BF16), (((1,), (1,)), ((), ())),
                                       preferred_element_type=F32))
        upd = lax.dot_general(v_b[rows], k_dec[rows], (((0,), (0,)), ((), ())),
                              preferred_element_type=F32)
        decay = jnp.exp(b_last[c])
        st_ref[...] = st * decay + jnp.where(bd_mask, upd, 0.0)
    o_inter = jnp.concatenate(o_inter, axis=0)

    for h in range(GLA_HEADS):
        cols = slice(h * GLA_DV, (h + 1) * GLA_DV)
        o = o_heads[h] + o_inter[:, cols]
        o = o * lax.rsqrt(jnp.mean(o * o, axis=-1, keepdims=True) + RMS_EPS) * glan_ref[...]
        g = proj_ref[:, C_G + h * GLA_DV:C_G + (h + 1) * GLA_DV].astype(F32)
        out_ref[:, cols] = (o * (g * (1.0 / (1.0 + jnp.exp(-g))))).astype(out_ref.dtype)

    between("gla_done")
    u = _gelu_tanh(proj_ref[:, C_U:C_U + GMLP_WIDTH].astype(F32))
    vg = _gelu_tanh(proj_ref[:, C_VG:C_VG + GMLP_WIDTH].astype(F32))
    hr = lax.broadcasted_iota(jnp.int32, (GMLP_WIDTH, GMLP_WIDTH), 0) // GMLP_DH
    hcn = lax.broadcasted_iota(jnp.int32, (GMLP_WIDTH, GMLP_WIDTH), 1) // GMLP_DH
    head_mean = jnp.where(hr == hcn, 1.0 / GMLP_DH, 0.0).astype(BF16)
    sq_hi, sq_lo = _split_bf16(vg * vg)
    ms = _dot(sq_hi, head_mean) + _dot(sq_lo, head_mean)
    v32 = vg * lax.rsqrt(ms + RMS_EPS) * gmn_ref[...]
    for c in range(n_gm):
        rows = slice(c * GMLP_CHUNK, (c + 1) * GMLP_CHUNK)
        vc = v32[rows].astype(BF16)
        zc = jnp.zeros_like(vc)
        rhs = jnp.concatenate(
            [jnp.where((lane256 // GMLP_DH) == h, vc, zc) for h in range(GMLP_HEADS)], axis=0)
        mixed = _dot(wm_ref[...], rhs) + bsp_ref[...]
        out_ref[rows, GLA_WIDTH:GLA_WIDTH + GMLP_WIDTH] = (u[rows] * mixed).astype(out_ref.dtype)

    between("gmlp_done")
    hcv = (proj_ref[:, C_CG:C_CG + CONV_WIDTH].astype(F32)
           * proj_ref[:, C_X:C_X + CONV_WIDTH].astype(F32))
    hc_ref[8:8 + TS, :] = hcv
    y = (wconv_ref[2:3, :] * hcv + wconv_ref[1:2, :] * hc_ref[7:7 + TS, :]
         + wconv_ref[0:1, :] * hc_ref[6:6 + TS, :])
    out_ref[:, GLA_WIDTH + GMLP_WIDTH:] = (
        proj_ref[:, C_BG:C_BG + CONV_WIDTH].astype(F32) * y).astype(out_ref.dtype)
    hc_ref[0:8, :] = hc_ref[TS:TS + 8, :]


_MIXER_SCRATCH = [
    pltpu.VMEM((GLA_WIDTH, GLA_KDIM), F32),
    pltpu.VMEM((TS_MIX + 8, CONV_WIDTH), F32),
    pltpu.VMEM((TS_MIX, TS_MIX), BF16),
    pltpu.VMEM((GMLP_CHUNK, GMLP_HEADS * GMLP_CHUNK), BF16),
]


N_MIX_PARAMS = 7
PROJ_CHUNKS_AT = {"gla_done": 4, "gmlp_done": 4}


def _front_kernel(*refs, tiles_per_seq, combine):
    refs = list(refs)
    if combine:
        dcur_ref, dnxt_ref, x_ref, route_ref, yb_ref = refs[:5]
        del refs[:5]
    else:
        x_ref = refs.pop(0)
    gain_ref, wt_ref = refs[:2]
    mix_refs = refs[2:2 + N_MIX_PARAMS]
    del refs[:2 + N_MIX_PARAMS]
    if combine:
        xo_ref, out_ref, buf_ref, sem_ref = refs[:4]
        del refs[:4]
    else:
        out_ref = refs.pop(0)
    st_ref, hc_ref, lcat_ref, wm_ref, wb_ref, pcur_ref, pnext_ref = refs
    s = pl.program_id(0)

    @pl.when(s == 0)
    def _():
        _stage_w_in(wt_ref, wb_ref)
        pcur_ref[...] = jnp.zeros_like(pcur_ref)

    col_chunks = [(c0, min(c0 + PROJ_CHUNK, D_PROJ)) for c0 in range(0, D_PROJ, PROJ_CHUNK)]
    work = list(zip(_prefetch_groups(len(col_chunks)), col_chunks))

    def project(h, n):
        for _ in range(min(n, len(work))):
            rows, (c0, c1) = work.pop(0)
            if combine:
                _gather_start(dnxt_ref, yb_ref, buf_ref, sem_ref, 1 - lax.rem(s, 2), rows)
            pnext_ref[:, c0:c1] = _dot(h, wb_ref[:, c0:c1]).astype(BF16)

    mixers = functools.partial(
        _mixer_kernel, pcur_ref, *mix_refs, out_ref, st_ref, hc_ref, lcat_ref, wm_ref,
        seq_start=lax.rem(jnp.maximum(s - 1, 0), tiles_per_seq) == 0, first_step=s == 0)
    if combine:
        x = _combined_residual(dcur_ref, x_ref, route_ref, yb_ref, buf_ref, sem_ref)
        xo_ref[...] = x
        h = _rms(x, gain_ref[...]).astype(BF16)
        mixers(between=lambda site: project(h, PROJ_CHUNKS_AT[site]))
        project(h, len(work))
        _drain_last_prefetch(yb_ref, buf_ref, sem_ref)
    else:
        mixers(between=lambda site: None)
        project(_rms(x_ref[...], gain_ref[...]).astype(BF16), len(work))
    pcur_ref[...] = pnext_ref[...]


def _front(x, gain, w_in_t, layer, mix_params, batch, seq, moe=None):
    n_seq = seq // TS_MIX
    n = batch * n_seq
    T = batch * seq
    cur = lambda s: jnp.minimum(s, n - 1)
    full = lambda shape: pl.BlockSpec(shape, lambda s: (0,) * len(shape))
    row = lambda w, f: pl.BlockSpec((TM, w), lambda s: (f(s), 0))
    in_specs, args = [row(D_MODEL, cur)], [x]
    out_specs = [row(D_MODEL, lambda s: jnp.maximum(s - 1, 0))]
    out_shape = [jax.ShapeDtypeStruct((T, D_MODEL), BF16)]
    scratch = list(_MIXER_SCRATCH)
    if moe is not None:
        smem_tile = lambda f: pl.BlockSpec((1, 1, TOP_K * TM), lambda s: (f(s), 0, 0),
                                           memory_space=pltpu.SMEM)
        in_specs = [smem_tile(cur), smem_tile(lambda s: jnp.minimum(s + 1, n - 1))] + in_specs + [
            row(LANES, cur), pl.BlockSpec(memory_space=pl.ANY)]
        args = [moe["dest_tiles"], moe["dest_tiles"]] + args + [moe["route"], moe["yb"]]
        out_specs = [row(D_MODEL, cur)] + out_specs
        out_shape = [jax.ShapeDtypeStruct((T, D_MODEL), F32)] + out_shape
        scratch = _COMBINE_SCRATCH + scratch
    in_specs += [
        full((1, D_MODEL)),
        pl.BlockSpec((1, D_IN, D_MODEL), lambda s: (layer, 0, 0), pipeline_mode=pl.Buffered(1)),
        full((LANES, GLA_KDIM)), full((1, GLA_KDIM)), full((1, GLA_DV)), full((1, GMLP_WIDTH)),
        full((GMLP_CHUNK, GMLP_HEADS * GMLP_CHUNK)), full((GMLP_CHUNK, GMLP_WIDTH)),
        full((8, CONV_WIDTH)),
    ]
    scratch += [pltpu.VMEM((D_MODEL, D_PROJ), BF16), pltpu.VMEM((TS_MIX, D_PROJ), BF16),
                pltpu.VMEM((TS_MIX, D_PROJ), BF16)]
    return pl.pallas_call(
        functools.partial(_front_kernel, tiles_per_seq=n_seq, combine=moe is not None),
        grid=(n + 1,),
        in_specs=in_specs,
        out_specs=out_specs if moe is not None else out_specs[0],
        out_shape=out_shape if moe is not None else out_shape[0],
        scratch_shapes=scratch,
        compiler_params=pltpu.CompilerParams(
            dimension_semantics=("arbitrary",), vmem_limit_bytes=VMEM_LIMIT),
        name="front",
    )(*args, gain, w_in_t, *mix_params)


def _row_tiles_store(tiles_ref, x):
    rows = x.shape[0]
    flat = tiles_ref.reshape(rows * ROW_TILES, LANES)
    for c in range(ROW_TILES):
        flat[pl.ds(c, rows, stride=ROW_TILES), :] = x[:, c * LANES:(c + 1) * LANES]


def _row_tiles_chunks(tiles_ref, rows):
    flat = tiles_ref.reshape(rows * ROW_TILES, LANES)
    return [flat[pl.ds(c, rows, stride=ROW_TILES), :] for c in range(ROW_TILES)]


IN_SLOTS = 3


def _out_router_kernel(mix_hbm, x_hbm, wo_ref, gain_ref, wrc_ref, br_ref,
                       x2_ref, h2_ref, route_ref, route_t_ref, cnt_ref, tri_ref, wob_ref, lg_ref,
                       mixbuf_ref, xbuf_ref, insem_ref):
    i = pl.program_id(0)
    n = pl.num_programs(0) - 1

    def in_copies(t):
        slot = lax.rem(t, IN_SLOTS)
        rows = pl.ds(t * TM, TM)
        return [pltpu.make_async_copy(mix_hbm.at[rows], mixbuf_ref.at[slot], insem_ref.at[slot]),
                pltpu.make_async_copy(x_hbm.at[rows], xbuf_ref.at[slot], insem_ref.at[slot])]

    @pl.when(i == 0)
    def _():
        for t in range(IN_SLOTS - 1):
            for c in in_copies(t):
                c.start()

    @pl.when(i + IN_SLOTS - 1 < n)
    def _():
        for c in in_copies(i + IN_SLOTS - 1):
            c.start()

    @pl.when(i < n)
    def _():
        for c in in_copies(i):
            c.wait()

    tile_slot = lax.rem(jnp.minimum(i, n - 1), IN_SLOTS)
    mix_ref = mixbuf_ref.at[tile_slot]
    x_ref = xbuf_ref.at[tile_slot]

    @pl.when(i == 0)
    def _():
        cnt_ref[...] = jnp.zeros_like(cnt_ref)
        lg_ref[...] = jnp.zeros_like(lg_ref)
        r = lax.broadcasted_iota(jnp.int32, (TM, TM), 0)
        c = lax.broadcasted_iota(jnp.int32, (TM, TM), 1)
        tri_ref[...] = jnp.where(c < r, 1.0, 0.0).astype(BF16)
        for r0 in range(0, D_MODEL, W_PREP_ROWS):
            wob_ref[r0:r0 + W_PREP_ROWS, :] = wo_ref[0, r0:r0 + W_PREP_ROWS, :].astype(BF16)

    lg = lg_ref[...]
    half = D_MODEL // 2
    mix = mix_ref[...]
    x2_a = x_ref[:, :half] + _dot(mix, wob_ref[:, :half])

    lane = lax.broadcasted_iota(jnp.int32, (TM, LANES), 1).astype(F32)
    neg = -jnp.inf
    is_g = lane < N_GROUPS
    gl = jnp.where(is_g, lg, neg)
    gmax = jnp.max(gl, axis=1, keepdims=True)
    g_top = jnp.min(jnp.where(gl == gmax, lane, float(LANES)), axis=1, keepdims=True)
    g_w = 1.0 / jnp.sum(jnp.where(is_g, jnp.exp(lg - gmax), 0.0), axis=1, keepdims=True)
    first = N_GROUPS + EXPERTS_PER_GROUP * g_top
    el = jnp.where((lane >= first) & (lane < first + EXPERTS_PER_GROUP), lg, neg)
    m1 = jnp.max(el, axis=1, keepdims=True)
    i1 = jnp.min(jnp.where(el == m1, lane, float(LANES)), axis=1, keepdims=True)
    el2 = jnp.where(lane == i1, neg, el)
    m2 = jnp.max(el2, axis=1, keepdims=True)
    i2 = jnp.min(jnp.where(el2 == m2, lane, float(LANES)), axis=1, keepdims=True)
    ratio = jnp.exp(m2 - m1)
    w1 = g_w / (1.0 + ratio)
    w2 = w1 * ratio

    x2_b = x_ref[:, half:] + _dot(mix, wob_ref[:, half:])

    oh1 = jnp.where(lane == i1, 1.0, 0.0)
    oh2 = jnp.where(lane == i2, 1.0, 0.0)
    oh = jnp.where(i > 0, oh1 + oh2, 0.0)
    before = _dot(tri_ref[...], oh.astype(BF16)) + cnt_ref[0:1, :]
    rank1 = jnp.sum(oh1 * before, axis=1, keepdims=True)
    rank2 = jnp.sum(oh2 * before, axis=1, keepdims=True)
    cnt_ref[...] = cnt_ref[...] + jnp.sum(oh, axis=0, keepdims=True)

    rec = jnp.zeros((TM, LANES), F32)
    for col, val in ((R_E, i1 - N_GROUPS), (R_E + 1, i2 - N_GROUPS), (R_RANK, rank1),
                     (R_RANK + 1, rank2), (R_W, w1), (R_W + 1, w2)):
        rec = jnp.where(lane == col, val, rec)
    route_ref[...] = rec
    route_t_ref[0] = rec.T[0:SUBLANES, :]

    x2 = jnp.concatenate([x2_a, x2_b], axis=1)
    x2_ref[...] = x2
    h = _rms(x2, gain_ref[...])
    h_hi, h_lo = _split_bf16(h)
    h2_ref[...] = h_hi
    hh_hl = _dot(h_hi, wrc_ref[...])
    lg_ref[...] = (hh_hl[:, :ROUTER_COLS] + hh_hl[:, ROUTER_COLS:]
                   + _dot(h_lo, wrc_ref[:, :ROUTER_COLS]) + br_ref[...])


def _out_router(mixed, x, w_out, layer, gain, wr_cat, br):
    T = x.shape[0]
    n = T // TM
    row = lambda w: pl.BlockSpec((TM, w), lambda i: (jnp.minimum(i, n - 1), 0))
    lag = lambda i: jnp.maximum(i - 1, 0)
    full = lambda shape: pl.BlockSpec(shape, lambda i: (0, 0))
    wo_spec = pl.BlockSpec((1, D_MODEL, D_MODEL), lambda i: (layer, 0, 0),
                           pipeline_mode=pl.Buffered(1))
    return pl.pallas_call(
        _out_router_kernel,
        grid=(n + 1,),
        in_specs=[pl.BlockSpec(memory_space=pl.ANY), pl.BlockSpec(memory_space=pl.ANY), wo_spec,
                  full((1, D_MODEL)), full((D_MODEL, 2 * ROUTER_COLS)), full((1, ROUTER_COLS))],
        out_specs=[row(D_MODEL), row(D_MODEL),
                   pl.BlockSpec((TM, LANES), lambda i: (lag(i), 0)),
                   pl.BlockSpec((1, SUBLANES, TM), lambda i: (lag(i), 0, 0)), full((8, LANES))],
        out_shape=[jax.ShapeDtypeStruct((T, D_MODEL), F32),
                   jax.ShapeDtypeStruct((T, D_MODEL), BF16),
                   jax.ShapeDtypeStruct((T, LANES), F32),
                   jax.ShapeDtypeStruct((T // TM, SUBLANES, TM), F32),
                   jax.ShapeDtypeStruct((8, LANES), F32)],
        scratch_shapes=[pltpu.VMEM((TM, TM), BF16), pltpu.VMEM((D_MODEL, D_MODEL), BF16),
                        pltpu.VMEM((TM, ROUTER_COLS), F32),
                        pltpu.VMEM((IN_SLOTS, TM, D_MODEL), BF16),
                        pltpu.VMEM((IN_SLOTS, TM, D_MODEL), F32),
                        pltpu.SemaphoreType.DMA((IN_SLOTS,))],
        compiler_params=pltpu.CompilerParams(
            dimension_semantics=("arbitrary",), vmem_limit_bytes=VMEM_LIMIT),
        name="out_router",
    )(mixed, x, w_out, gain, wr_cat, br)


def _dispatch_kernel(fill_ref, nu_ref, dest_ref, h_ref, xb_ref, zero_ref, sem_ref, zsem_ref,
                     stage_ref):
    i = pl.program_id(0)
    par = lax.rem(i, 2)
    last = i == pl.num_programs(0) - 1
    n_blocks = xb_ref.shape[0] // MOE_BLK
    spare_fills = [(j >= nu_ref[0], pltpu.make_async_copy(
        zero_ref, xb_ref.at[pl.ds(j * MOE_BLK, MOE_BLK)], zsem_ref.at[1]))
        for j in range(n_blocks - N_EXPERTS, n_blocks)]

    @pl.when(i == 0)
    def _():
        zero_ref[...] = jnp.zeros_like(zero_ref)
        fills = [(fill_ref[e] >= 0, pltpu.make_async_copy(
            zero_ref, xb_ref.at[pl.ds(pl.multiple_of(jnp.maximum(fill_ref[e], 0), MOE_BLK), MOE_BLK)],
            zsem_ref.at[0])) for e in range(N_EXPERTS)]
        for cond, f in fills + spare_fills:
            pl.when(cond)(f.start)
        for cond, f in fills:
            pl.when(cond)(f.wait)

    _row_tiles_store(stage_ref.at[par], h_ref[...].astype(F32))
    for r in range(TM):
        for k in range(TOP_K):
            pltpu.make_async_copy(stage_ref.at[par, r], xb_ref.at[dest_ref[0, 0, k * TM + r]],
                                  sem_ref.at[par]).start(priority=k)

    def wait_tile(p):
        for _ in range(TOP_K):
            pltpu.make_async_copy(stage_ref.at[p], xb_ref.at[pl.ds(0, TM)], sem_ref.at[p]).wait()

    pl.when(i > 0)(lambda: wait_tile(1 - par))
    @pl.when(last)
    def _():
        wait_tile(par)
        for cond, f in spare_fills:
            pl.when(cond)(f.wait)


def _dispatch(fill_start, n_used, dest_tiles, h2, n_rows):
    T = h2.shape[0]
    grid_spec = pltpu.PrefetchScalarGridSpec(
        num_scalar_prefetch=2,
        grid=(T // TM,),
        in_specs=[
            pl.BlockSpec((1, 1, TOP_K * TM), lambda i, fs, nu: (i, 0, 0), memory_space=pltpu.SMEM),
            pl.BlockSpec((TM, D_MODEL), lambda i, fs, nu: (i, 0)),
        ],
        out_specs=pl.BlockSpec(memory_space=pl.ANY),
        scratch_shapes=[pltpu.VMEM((MOE_BLK, ROW_TILES, LANES), F32),
                        pltpu.SemaphoreType.DMA((2,)), pltpu.SemaphoreType.DMA((2,)),
                        pltpu.VMEM((2, TM, ROW_TILES, LANES), F32)],
    )
    return pl.pallas_call(
        _dispatch_kernel,
        grid_spec=grid_spec,
        out_shape=jax.ShapeDtypeStruct((n_rows, ROW_TILES, LANES), F32),
        compiler_params=pltpu.CompilerParams(dimension_semantics=("arbitrary",)),
        name="dispatch",
    )(fill_start, n_used, dest_tiles, h2)


BLOCK_COPY_PARTS = 4
X_SLOTS = 4
Y_SLOTS = 3


class _CopyGroup:
    def __init__(self, copies):
        self.copies = copies

    def start(self):
        for n, c in enumerate(self.copies):
            c.start(priority=n % 2)

    def wait(self):
        for c in self.copies:
            c.wait()


def _expert_kernel(be_ref, nxt_ref, nu_ref, xb_ref, wg_ref, wu_ref, wd_ref, yb_ref,
                   xbuf_ref, ybuf_ref, wgs_ref, wus_ref, wds_ref, wgb_ref, wub_ref, wdb_ref,
                   xsem_ref, ysem_ref, wsem_ref, zbuf_ref, zsem_ref, *, layer, n_blocks):
    n_used = nu_ref[0]

    part = MOE_BLK // BLOCK_COPY_PARTS

    def x_copy(j, slot):
        return _CopyGroup([pltpu.make_async_copy(
            xb_ref.at[pl.ds(j * MOE_BLK + p * part, part)],
            xbuf_ref.at[slot, pl.ds(p * part, part)], xsem_ref.at[slot])
            for p in range(BLOCK_COPY_PARTS)])

    def y_copy(j, slot):
        return _CopyGroup([pltpu.make_async_copy(
            ybuf_ref.at[slot, pl.ds(p * part, part)],
            yb_ref.at[pl.ds(j * MOE_BLK + p * part, part)], ysem_ref.at[slot])
            for p in range(BLOCK_COPY_PARTS)])

    def w_copies(e, ws):
        return [pltpu.make_async_copy(src.at[layer, e], dst.at[ws], wsem_ref.at[ws])
                for src, dst in ((wg_ref, wgs_ref), (wu_ref, wus_ref), (wd_ref, wds_ref))]

    def fill_copy(j):
        return pltpu.make_async_copy(zbuf_ref, yb_ref.at[pl.ds(j * MOE_BLK, MOE_BLK)], zsem_ref)

    for j0 in range(X_SLOTS - 1):
        x_copy(j0, j0).start()
    for c in w_copies(be_ref[0], 0):
        c.start()

    zbuf_ref[...] = jnp.zeros_like(zbuf_ref)

    def fill(j, carry):
        fill_copy(j).start()
        return carry

    lax.fori_loop(n_used, n_blocks, fill, 0)

    def block(j, ws):
        slot = lax.rem(j, Y_SLOTS)
        xslot = lax.rem(j, X_SLOTS)
        first = (j == 0) | (be_ref[j] != be_ref[jnp.maximum(j - 1, 0)])
        ws = jnp.where(first & (j > 0), 1 - ws, ws)

        @pl.when(first)
        def _():
            for c in w_copies(be_ref[j], ws):
                c.wait()
            for r0 in range(0, D_MODEL, W_PREP_ROWS):
                rows = slice(r0, r0 + W_PREP_ROWS)
                wgb_ref[rows, :] = wgs_ref[ws, rows, :].astype(BF16)
                wub_ref[rows, :] = wus_ref[ws, rows, :].astype(BF16)
            for r0 in range(0, D_EXPERT, W_PREP_ROWS):
                rows = slice(r0, r0 + W_PREP_ROWS)
                wdb_ref[rows, :] = wds_ref[ws, rows, :].astype(BF16)

            @pl.when(nxt_ref[j] >= 0)
            def _():
                for c in w_copies(nxt_ref[j], 1 - ws):
                    c.start()

        ahead = j + X_SLOTS - 1

        @pl.when(ahead < n_used)
        def _():
            x_copy(ahead, lax.rem(ahead, X_SLOTS)).start()

        x_copy(j, xslot).wait()
        x = jnp.concatenate(
            [c.astype(BF16) for c in _row_tiles_chunks(xbuf_ref.at[xslot], MOE_BLK)], axis=1)
        g = _dot(x, wgb_ref[...])
        u = _dot(x, wub_ref[...])
        h = (g * (1.0 / (1.0 + jnp.exp(-g)))) * u
        y = _dot(h.astype(BF16), wdb_ref[...])

        @pl.when(j >= Y_SLOTS)
        def _():
            y_copy(j - Y_SLOTS, slot).wait()

        _row_tiles_store(ybuf_ref.at[slot], y)
        y_copy(j, slot).start()
        return ws

    lax.fori_loop(0, n_used, block, jnp.int32(0))

    for back in range(Y_SLOTS, 0, -1):
        y_copy(n_used - back, lax.rem(n_used - back, Y_SLOTS)).wait()

    def fill_wait(j, carry):
        fill_copy(j).wait()
        return carry

    lax.fori_loop(n_used, n_blocks, fill_wait, 0)


def _experts(blk_exp, nxt_exp, n_used, xb, w_gate, w_up, w_down, layer):
    n_blocks = blk_exp.shape[0]
    any_spec = pl.BlockSpec(memory_space=pl.ANY)
    blk = (MOE_BLK, ROW_TILES, LANES)
    grid_spec = pltpu.PrefetchScalarGridSpec(
        num_scalar_prefetch=3,
        grid=(1,),
        in_specs=[any_spec, any_spec, any_spec, any_spec],
        out_specs=any_spec,
        scratch_shapes=[
            pltpu.VMEM((X_SLOTS,) + blk, F32), pltpu.VMEM((Y_SLOTS,) + blk, F32),
            pltpu.VMEM((2, D_MODEL, D_EXPERT), F32), pltpu.VMEM((2, D_MODEL, D_EXPERT), F32),
            pltpu.VMEM((2, D_EXPERT, D_MODEL), F32),
            pltpu.VMEM((D_MODEL, D_EXPERT), BF16), pltpu.VMEM((D_MODEL, D_EXPERT), BF16),
            pltpu.VMEM((D_EXPERT, D_MODEL), BF16),
            pltpu.SemaphoreType.DMA((X_SLOTS,)), pltpu.SemaphoreType.DMA((Y_SLOTS,)),
            pltpu.SemaphoreType.DMA((2,)),
            pltpu.VMEM(blk, F32), pltpu.SemaphoreType.DMA(()),
        ],
    )
    return pl.pallas_call(
        functools.partial(_expert_kernel, layer=layer, n_blocks=n_blocks),
        grid_spec=grid_spec,
        out_shape=jax.ShapeDtypeStruct((n_blocks * MOE_BLK, ROW_TILES, LANES), F32),
        compiler_params=pltpu.CompilerParams(
            dimension_semantics=("arbitrary",), vmem_limit_bytes=VMEM_LIMIT),
        name="experts",
    )(blk_exp, nxt_exp, n_used, xb, w_gate, w_up, w_down)


def _dispatch_tables(route_t, counts_rec, T):
    counts = counts_rec[0, N_GROUPS:N_GROUPS + N_EXPERTS].astype(jnp.int32)
    n_steps = (T * TOP_K) // MOE_BLK + N_EXPERTS
    nblk = (counts + MOE_BLK - 1) // MOE_BLK
    bend = jnp.cumsum(nblk)
    pstart = (bend - nblk) * MOE_BLK
    n_used = bend[-1]
    j = jnp.minimum(jnp.arange(n_steps, dtype=jnp.int32), n_used - 1)
    blk_exp = jnp.minimum(jnp.sum(j[:, None] >= bend[None, :], axis=1), N_EXPERTS - 1)
    n_rows = n_steps * MOE_BLK
    last_blk = jnp.where(counts > 0, (bend - 1) * MOE_BLK, -1)
    ids = jnp.arange(N_EXPERTS, dtype=jnp.int32)
    later = (ids[None, :] > ids[:, None]) & (nblk[None, :] > 0)
    nxt_of = jnp.min(jnp.where(later, ids[None, :], N_EXPERTS), axis=1)
    nxt_tab = jnp.where(nxt_of < N_EXPERTS, nxt_of, -1)
    nxt_exp = jnp.sum(jnp.where(blk_exp[:, None] == ids[None, :], nxt_tab[None, :], 0), axis=1)
    e = route_t[:, R_E:R_E + TOP_K, :].astype(jnp.int32)
    rank = route_t[:, R_RANK:R_RANK + TOP_K, :].astype(jnp.int32)
    seg = jnp.sum(jnp.where(e[..., None] == jnp.arange(N_EXPERTS), pstart, 0), axis=-1)
    dest = jnp.clip(seg + rank, 0, n_steps * MOE_BLK - 1)
    dest_tiles = dest.reshape(T // TM, 1, TOP_K * TM)
    return dict(dest_tiles=dest_tiles, fill_start=last_blk.astype(jnp.int32),
                blk_exp=blk_exp.astype(jnp.int32), nxt_exp=nxt_exp.astype(jnp.int32),
                n_used=n_used.reshape(1).astype(jnp.int32), n_rows=n_rows)


def _prep_layer(l, w_gk_up, b_gk, gla_norm, gmlp_norm, w_spatial, b_spatial, w_conv,
                w_router_group, b_router_group, w_router_expert, b_router_expert):
    wgk = jnp.concatenate(
        [w_gk_up[l], jnp.zeros((LANES - GLA_GATE_RANK, GLA_KDIM), F32)], axis=0).astype(BF16)
    wsp = w_spatial[l].transpose(1, 0, 2).reshape(GMLP_CHUNK, GMLP_HEADS * GMLP_CHUNK)
    bsp = jnp.repeat(b_spatial[l].T, GMLP_DH, axis=1)
    wconv = jnp.concatenate([w_conv[l], jnp.zeros((8 - CONV_K, CONV_WIDTH), F32)], axis=0)
    wr = jnp.concatenate(
        [w_router_group[l], w_router_expert[l],
         jnp.zeros((D_MODEL, ROUTER_COLS - N_GROUPS - N_EXPERTS), F32)], axis=1)
    wr_hi = wr.astype(BF16)
    wr_lo = (wr - wr_hi.astype(F32)).astype(BF16)
    br = jnp.concatenate(
        [b_router_group[l], b_router_expert[l],
         jnp.zeros((ROUTER_COLS - N_GROUPS - N_EXPERTS,), F32)])[None, :]
    return dict(
        wgk=wgk, bgk=b_gk[l][None, :], glan=gla_norm[l][None, :], gmn=gmlp_norm[l][None, :],
        wsp=wsp, bsp=bsp, wconv=wconv, wr_cat=jnp.concatenate([wr_hi, wr_lo], axis=1), br=br)


def kernel(x, attn_norm, w_in, w_gk_up, b_gk, gla_norm, gmlp_norm, w_spatial, b_spatial, w_conv, w_out, ffn_norm, w_router_group, b_router_group, w_router_expert, b_router_expert, w_gate, w_up, w_down, final_norm):
    B, S, D = x.shape
    T = B * S
    depth = w_in.shape[0]
    xr = x.reshape(T, D)
    w_in_t = jnp.swapaxes(w_in, 1, 2)
    moe = None
    for l in range(depth):
        p = _prep_layer(l, w_gk_up, b_gk, gla_norm, gmlp_norm, w_spatial, b_spatial, w_conv,
                        w_router_group, b_router_group, w_router_expert, b_router_expert)
        mix_params = (p["wgk"], p["bgk"], p["glan"], p["gmn"], p["wsp"], p["bsp"], p["wconv"])
        if moe is None:
            mixed = _front(xr, attn_norm[l][None, :], w_in_t, l, mix_params, B, S)
        else:
            xr, mixed = _front(moe["x2"], attn_norm[l][None, :], w_in_t, l, mix_params, B, S, moe)
        x2, h2, route, route_t, counts_rec = _out_router(
            mixed, xr, w_out, l, ffn_norm[l][None, :], p["wr_cat"], p["br"])
        moe = _dispatch_tables(route_t, counts_rec, T)
        xb = _dispatch(moe["fill_start"], moe["n_used"], moe["dest_tiles"], h2, moe["n_rows"])
        yb = _experts(moe["blk_exp"], moe["nxt_exp"], moe["n_used"], xb, w_gate, w_up, w_down, l)
        moe.update(x2=x2, route=route, yb=yb)
    out = _combine_final_norm(moe["dest_tiles"], moe["x2"], moe["route"], moe["yb"],
                              final_norm[None, :])
    return out.reshape(B, S, D)
```

```python
import functools

import jax
import jax.numpy as jnp
from jax import lax
from jax.experimental import pallas as pl
from jax.experimental.pallas import tpu as pltpu

F32 = jnp.float32
BF16 = jnp.bfloat16

D_MODEL = 1024
RMS_EPS = 1e-6
GLA_HEADS = 4
GLA_WIDTH = 512
GLA_DV = 128
GLA_DK = 64
GLA_KDIM = 256
GLA_GATE_RANK = 16
GLA_GATE_NORM = 16.0
GLA_CHUNK = 64
GMLP_HEADS = 4
GMLP_WIDTH = 256
GMLP_DH = 64
GMLP_CHUNK = 128
CONV_WIDTH = 256
CONV_K = 3
N_GROUPS = 4
EXPERTS_PER_GROUP = 8
N_EXPERTS = 32
TOP_K = 2
D_EXPERT = 256

LANES = 128
C_Q, C_K, C_V, C_G = 0, 256, 512, 1024
C_U, C_VG, C_X, C_BG, C_CG, C_GKL = 1536, 1792, 2048, 2304, 2560, 2816
D_PROJ = C_GKL + LANES
D_IN = C_GKL + GLA_GATE_RANK

TM = 256
TS_MIX = TM
MOE_BLK = 256
ROUTER_COLS = LANES
SUBLANES = 8
ROW_TILES = D_MODEL // LANES
assert ROW_TILES == SUBLANES
VMEM_LIMIT = 56 * 1024 * 1024
R_E, R_RANK, R_W = 0, 2, 4


def _dot(a, b):
    return jnp.dot(a, b, preferred_element_type=F32)


def _split_bf16(x):
    hi = x.astype(BF16)
    lo = (x - hi.astype(F32)).astype(BF16)
    return hi, lo


def _rms(x, gain):
    return x * lax.rsqrt(jnp.mean(x * x, axis=-1, keepdims=True) + RMS_EPS) * gain


W_PREP_ROWS = 128
PROJ_CHUNK = 256


def _stage_w_in(wt_ref, wb_ref):
    for c0 in range(0, C_GKL, LANES):
        src = c0 if c0 < C_U else c0 + GLA_GATE_RANK
        wb_ref[:, c0:c0 + LANES] = wt_ref[0, src:src + LANES, :].T.astype(BF16)
    low = jnp.concatenate([wt_ref[0, C_U:C_U + GLA_GATE_RANK, :],
                           jnp.zeros((LANES - GLA_GATE_RANK, D_MODEL), F32)], axis=0)
    wb_ref[:, C_GKL:D_PROJ] = low.T.astype(BF16)


def _row_gather_copy(yb_ref, buf_ref, sem_ref, slot, k, r, d):
    return pltpu.make_async_copy(yb_ref.at[d], buf_ref.at[slot, k, r], sem_ref.at[slot])


def _gather_start(dest_ref, yb_ref, buf_ref, sem_ref, slot, rows=range(TM)):
    for r in rows:
        for k in range(TOP_K):
            _row_gather_copy(yb_ref, buf_ref, sem_ref, slot, k, r,
                             dest_ref[0, 0, k * TM + r]).start(priority=k)


def _gather_wait(yb_ref, buf_ref, sem_ref, slot):
    for k in range(TOP_K):
        pltpu.make_async_copy(yb_ref.at[pl.ds(0, TM)], buf_ref.at[slot, k], sem_ref.at[slot]).wait()


def _combined_residual(dcur_ref, x_ref, route_ref, yb_ref, buf_ref, sem_ref):
    i = pl.program_id(0)
    slot = lax.rem(i, 2)

    @pl.when(i == 0)
    def _():
        _gather_start(dcur_ref, yb_ref, buf_ref, sem_ref, 0)

    _gather_wait(yb_ref, buf_ref, sem_ref, slot)
    w0 = route_ref[:, R_W:R_W + 1]
    w1 = route_ref[:, R_W + 1:R_W + 2]
    y0 = _row_tiles_chunks(buf_ref.at[slot, 0], TM)
    y1 = _row_tiles_chunks(buf_ref.at[slot, 1], TM)
    return jnp.concatenate(
        [x_ref[:, c * LANES:(c + 1) * LANES] + (w0 * y0[c] + w1 * y1[c]) for c in range(ROW_TILES)],
        axis=1)


def _prefetch_groups(n_groups):
    per = -(-TM // n_groups)
    return [range(g * per, min(TM, (g + 1) * per)) for g in range(n_groups)]


def _drain_last_prefetch(yb_ref, buf_ref, sem_ref):
    i = pl.program_id(0)

    @pl.when(i == pl.num_programs(0) - 1)
    def _():
        _gather_wait(yb_ref, buf_ref, sem_ref, 1 - lax.rem(i, 2))


def _combine_specs(n_tiles):
    smem_tile = lambda f: pl.BlockSpec((1, 1, TOP_K * TM), f, memory_space=pltpu.SMEM)
    return [
        smem_tile(lambda i: (i, 0, 0)),
        smem_tile(lambda i: (jnp.minimum(i + 1, n_tiles - 1), 0, 0)),
        pl.BlockSpec((TM, D_MODEL), lambda i: (i, 0)),
        pl.BlockSpec((TM, LANES), lambda i: (i, 0)),
        pl.BlockSpec(memory_space=pl.ANY),
    ]


_COMBINE_SCRATCH = [pltpu.VMEM((2, TOP_K, TM, ROW_TILES, LANES), F32),
                    pltpu.SemaphoreType.DMA((2,))]


def _combine_final_norm_kernel(dcur_ref, dnxt_ref, x_ref, route_ref, yb_ref, gain_ref,
                               o_ref, buf_ref, sem_ref):
    _gather_start(dnxt_ref, yb_ref, buf_ref, sem_ref, 1 - lax.rem(pl.program_id(0), 2))
    x = _combined_residual(dcur_ref, x_ref, route_ref, yb_ref, buf_ref, sem_ref)
    o_ref[...] = _rms(x, gain_ref[...])
    _drain_last_prefetch(yb_ref, buf_ref, sem_ref)


def _combine_final_norm(dest_tiles, x2, route, yb, gain):
    T = x2.shape[0]
    n_tiles = T // TM
    return pl.pallas_call(
        _combine_final_norm_kernel,
        grid=(n_tiles,),
        in_specs=_combine_specs(n_tiles) + [pl.BlockSpec((1, D_MODEL), lambda i: (0, 0))],
        out_specs=pl.BlockSpec((TM, D_MODEL), lambda i: (i, 0)),
        out_shape=jax.ShapeDtypeStruct((T, D_MODEL), F32),
        scratch_shapes=_COMBINE_SCRATCH,
        compiler_params=pltpu.CompilerParams(
            dimension_semantics=("arbitrary",), vmem_limit_bytes=VMEM_LIMIT),
        name="combine_final_norm",
    )(dest_tiles, dest_tiles, x2, route, yb, gain)


def _gelu_tanh(x):
    c = 0.7978845608028654
    return x * (0.5 * (1.0 + jnp.tanh(c * (x + 0.044715 * (x * x * x)))))


def _mixer_kernel(proj_ref, wgk_ref, bgk_ref, glan_ref, gmn_ref, wsp_ref, bsp_ref, wconv_ref,
                  out_ref, st_ref, hc_ref, lcat_ref, wm_ref, *, seq_start, first_step, between):
    TS = TS_MIX
    n_gla = TS // GLA_CHUNK
    n_gm = TS // GMLP_CHUNK

    @pl.when(seq_start)
    def _():
        st_ref[...] = jnp.zeros_like(st_ref)
        hc_ref[...] = jnp.zeros_like(hc_ref)

    @pl.when(first_step)
    def _():
        r = lax.broadcasted_iota(jnp.int32, (TS, TS), 0)
        c = lax.broadcasted_iota(jnp.int32, (TS, TS), 1)
        keep = ((r // GLA_CHUNK) == (c // GLA_CHUNK)) & (c <= r)
        lcat_ref[...] = jnp.where(keep, 1.0, 0.0).astype(BF16)
        t = lax.broadcasted_iota(jnp.int32, (GMLP_CHUNK, GMLP_HEADS * GMLP_CHUNK), 0)
        s = lax.broadcasted_iota(jnp.int32, (GMLP_CHUNK, GMLP_HEADS * GMLP_CHUNK), 1) % GMLP_CHUNK
        wm_ref[...] = jnp.where(s <= t, wsp_ref[...], 0.0).astype(BF16)

    lane256 = lax.broadcasted_iota(jnp.int32, (1, GLA_KDIM), 1)

    q = proj_ref[:, C_Q:C_Q + GLA_KDIM].astype(F32)
    k = proj_ref[:, C_K:C_K + GLA_KDIM].astype(F32)
    v_b = proj_ref[:, C_V:C_V + GLA_WIDTH]
    z = _dot(proj_ref[:, C_GKL:C_GKL + LANES], wgk_ref[...]) + bgk_ref[...]
    gk = (jnp.minimum(z, 0.0) - jnp.log1p(jnp.exp(-jnp.abs(z)))) * (1.0 / GLA_GATE_NORM)
    gk_hi, gk_lo = _split_bf16(gk)
    cs = _dot(lcat_ref[...], jnp.concatenate([gk_hi, gk_lo], axis=1))
    b = cs[:, :GLA_KDIM] + cs[:, GLA_KDIM:]
    b_last = [b[(c + 1) * GLA_CHUNK - 1:(c + 1) * GLA_CHUNK, :] for c in range(n_gla)]
    bl = jnp.concatenate(
        [jnp.broadcast_to(t, (GLA_CHUNK, GLA_KDIM)) for t in b_last], axis=0)
    q_dec = (q * (GLA_DK ** -0.5)) * jnp.exp(b)
    k_inv = (k * jnp.exp(-b)).astype(BF16)
    k_dec = (k * jnp.exp(bl - b)).astype(BF16)
    q_dec_b = q_dec.astype(BF16)

    zero_b = jnp.zeros_like(q_dec_b)
    q_stack = jnp.concatenate(
        [jnp.where((lane256 // GLA_DK) == h, q_dec_b, zero_b) for h in range(GLA_HEADS)], axis=0)
    scores = lax.dot_general(q_stack, k_inv, (((1,), (1,)), ((), ())),
                             preferred_element_type=F32)
    rt = lax.broadcasted_iota(jnp.int32, (TS, TS), 0)
    ct = lax.broadcasted_iota(jnp.int32, (TS, TS), 1)
    causal = ((rt // GLA_CHUNK) == (ct // GLA_CHUNK)) & (ct <= rt)
    o_heads = []
    for h in range(GLA_HEADS):
        p_h = jnp.where(causal, scores[h * TS:(h + 1) * TS, :], 0.0).astype(BF16)
        o_heads.append(_dot(p_h, v_b[:, h * GLA_DV:(h + 1) * GLA_DV]))

    sr = lax.broadcasted_iota(jnp.int32, (GLA_WIDTH, GLA_KDIM), 0) // GLA_DV
    sc = lax.broadcasted_iota(jnp.int32, (GLA_WIDTH, GLA_KDIM), 1) // GLA_DK
    bd_mask = sr == sc
    o_inter = []
    for c in range(n_gla):
        rows = slice(c * GLA_CHUNK, (c + 1) * GLA_CHUNK)
        st = st_ref[...]
        o_inter.append(lax.dot_general(q_dec_b[rows], st.astype(BF16), (((1,), (1,)), ((), ())),
                                       preferred_element_type=F32))
        upd = lax.dot_general(v_b[rows], k_dec[rows], (((0,), (0,)), ((), ())),
                              preferred_element_type=F32)
        decay = jnp.exp(b_last[c])
        st_ref[...] = st * decay + jnp.where(bd_mask, upd, 0.0)
    o_inter = jnp.concatenate(o_inter, axis=0)

    for h in range(GLA_HEADS):
        cols = slice(h * GLA_DV, (h + 1) * GLA_DV)
        o = o_heads[h] + o_inter[:, cols]
        o = o * lax.rsqrt(jnp.mean(o * o, axis=-1, keepdims=True) + RMS_EPS) * glan_ref[...]
        g = proj_ref[:, C_G + h * GLA_DV:C_G + (h + 1) * GLA_DV].astype(F32)
        out_ref[:, cols] = (o * (g * (1.0 / (1.0 + jnp.exp(-g))))).astype(out_ref.dtype)

    between("gla_done")
    u = _gelu_tanh(proj_ref[:, C_U:C_U + GMLP_WIDTH].astype(F32))
    vg = _gelu_tanh(proj_ref[:, C_VG:C_VG + GMLP_WIDTH].astype(F32))
    hr = lax.broadcasted_iota(jnp.int32, (GMLP_WIDTH, GMLP_WIDTH), 0) // GMLP_DH
    hcn = lax.broadcasted_iota(jnp.int32, (GMLP_WIDTH, GMLP_WIDTH), 1) // GMLP_DH
    head_mean = jnp.where(hr == hcn, 1.0 / GMLP_DH, 0.0).astype(BF16)
    sq_hi, sq_lo = _split_bf16(vg * vg)
    ms = _dot(sq_hi, head_mean) + _dot(sq_lo, head_mean)
    v32 = vg * lax.rsqrt(ms + RMS_EPS) * gmn_ref[...]
    for c in range(n_gm):
        rows = slice(c * GMLP_CHUNK, (c + 1) * GMLP_CHUNK)
        vc = v32[rows].astype(BF16)
        zc = jnp.zeros_like(vc)
        rhs = jnp.concatenate(
            [jnp.where((lane256 // GMLP_DH) == h, vc, zc) for h in range(GMLP_HEADS)], axis=0)
        mixed = _dot(wm_ref[...], rhs) + bsp_ref[...]
        out_ref[rows, GLA_WIDTH:GLA_WIDTH + GMLP_WIDTH] = (u[rows] * mixed).astype(out_ref.dtype)

    between("gmlp_done")
    hcv = (proj_ref[:, C_CG:C_CG + CONV_WIDTH].astype(F32)
           * proj_ref[:, C_X:C_X + CONV_WIDTH].astype(F32))
    hc_ref[8:8 + TS, :] = hcv
    y = (wconv_ref[2:3, :] * hcv + wconv_ref[1:2, :] * hc_ref[7:7 + TS, :]
         + wconv_ref[0:1, :] * hc_ref[6:6 + TS, :])
    out_ref[:, GLA_WIDTH + GMLP_WIDTH:] = (
        proj_ref[:, C_BG:C_BG + CONV_WIDTH].astype(F32) * y).astype(out_ref.dtype)
    hc_ref[0:8, :] = hc_ref[TS:TS + 8, :]


_MIXER_SCRATCH = [
    pltpu.VMEM((GLA_WIDTH, GLA_KDIM), F32),
    pltpu.VMEM((TS_MIX + 8, CONV_WIDTH), F32),
    pltpu.VMEM((TS_MIX, TS_MIX), BF16),
    pltpu.VMEM((GMLP_CHUNK, GMLP_HEADS * GMLP_CHUNK), BF16),
]


N_MIX_PARAMS = 7
PROJ_CHUNKS_AT = {"gla_done": 4, "gmlp_done": 4}


def _front_kernel(*refs, tiles_per_seq, combine):
    refs = list(refs)
    if combine:
        dcur_ref, dnxt_ref, x_ref, route_ref, yb_ref = refs[:5]
        del refs[:5]
    else:
        x_ref = refs.pop(0)
    gain_ref, wt_ref = refs[:2]
    mix_refs = refs[2:2 + N_MIX_PARAMS]
    del refs[:2 + N_MIX_PARAMS]
    if combine:
        xo_ref, out_ref, buf_ref, sem_ref = refs[:4]
        del refs[:4]
    else:
        out_ref = refs.pop(0)
    st_ref, hc_ref, lcat_ref, wm_ref, wb_ref, pcur_ref, pnext_ref = refs
    s = pl.program_id(0)

    @pl.when(s == 0)
    def _():
        _stage_w_in(wt_ref, wb_ref)
        pcur_ref[...] = jnp.zeros_like(pcur_ref)

    col_chunks = [(c0, min(c0 + PROJ_CHUNK, D_PROJ)) for c0 in range(0, D_PROJ, PROJ_CHUNK)]
    work = list(zip(_prefetch_groups(len(col_chunks)), col_chunks))

    def project(h, n):
        for _ in range(min(n, len(work))):
            rows, (c0, c1) = work.pop(0)
            if combine:
                _gather_start(dnxt_ref, yb_ref, buf_ref, sem_ref, 1 - lax.rem(s, 2), rows)
            pnext_ref[:, c0:c1] = _dot(h, wb_ref[:, c0:c1]).astype(BF16)

    mixers = functools.partial(
        _mixer_kernel, pcur_ref, *mix_refs, out_ref, st_ref, hc_ref, lcat_ref, wm_ref,
        seq_start=lax.rem(jnp.maximum(s - 1, 0), tiles_per_seq) == 0, first_step=s == 0)
    if combine:
        x = _combined_residual(dcur_ref, x_ref, route_ref, yb_ref, buf_ref, sem_ref)
        xo_ref[...] = x
        h = _rms(x, gain_ref[...]).astype(BF16)
        mixers(between=lambda site: project(h, PROJ_CHUNKS_AT[site]))
        project(h, len(work))
        _drain_last_prefetch(yb_ref, buf_ref, sem_ref)
    else:
        mixers(between=lambda site: None)
        project(_rms(x_ref[...], gain_ref[...]).astype(BF16), len(work))
    pcur_ref[...] = pnext_ref[...]


def _front(x, gain, w_in_t, layer, mix_params, batch, seq, moe=None):
    n_seq = seq // TS_MIX
    n = batch * n_seq
    T = batch * seq
    cur = lambda s: jnp.minimum(s, n - 1)
    full = lambda shape: pl.BlockSpec(shape, lambda s: (0,) * len(shape))
    row = lambda w, f: pl.BlockSpec((TM, w), lambda s: (f(s), 0))
    in_specs, args = [row(D_MODEL, cur)], [x]
    out_specs = [row(D_MODEL, lambda s: jnp.maximum(s - 1, 0))]
    out_shape = [jax.ShapeDtypeStruct((T, D_MODEL), BF16)]
    scratch = list(_MIXER_SCRATCH)
    if moe is not None:
        smem_tile = lambda f: pl.BlockSpec((1, 1, TOP_K * TM), lambda s: (f(s), 0, 0),
                                           memory_space=pltpu.SMEM)
        in_specs = [smem_tile(cur), smem_tile(lambda s: jnp.minimum(s + 1, n - 1))] + in_specs + [
            row(LANES, cur), pl.BlockSpec(memory_space=pl.ANY)]
        args = [moe["dest_tiles"], moe["dest_tiles"]] + args + [moe["route"], moe["yb"]]
        out_specs = [row(D_MODEL, cur)] + out_specs
        out_shape = [jax.ShapeDtypeStruct((T, D_MODEL), F32)] + out_shape
        scratch = _COMBINE_SCRATCH + scratch
    in_specs += [
        full((1, D_MODEL)),
        pl.BlockSpec((1, D_IN, D_MODEL), lambda s: (layer, 0, 0), pipeline_mode=pl.Buffered(1)),
        full((LANES, GLA_KDIM)), full((1, GLA_KDIM)), full((1, GLA_DV)), full((1, GMLP_WIDTH)),
        full((GMLP_CHUNK, GMLP_HEADS * GMLP_CHUNK)), full((GMLP_CHUNK, GMLP_WIDTH)),
        full((8, CONV_WIDTH)),
    ]
    scratch += [pltpu.VMEM((D_MODEL, D_PROJ), BF16), pltpu.VMEM((TS_MIX, D_PROJ), BF16),
                pltpu.VMEM((TS_MIX, D_PROJ), BF16)]
    return pl.pallas_call(
        functools.partial(_front_kernel, tiles_per_seq=n_seq, combine=moe is not None),
        grid=(n + 1,),
        in_specs=in_specs,
        out_specs=out_specs if moe is not None else out_specs[0],
        out_shape=out_shape if moe is not None else out_shape[0],
        scratch_shapes=scratch,
        compiler_params=pltpu.CompilerParams(
            dimension_semantics=("arbitrary",), vmem_limit_bytes=VMEM_LIMIT),
        name="front",
    )(*args, gain, w_in_t, *mix_params)


def _row_tiles_store(tiles_ref, x):
    rows = x.shape[0]
    flat = tiles_ref.reshape(rows * ROW_TILES, LANES)
    for c in range(ROW_TILES):
        flat[pl.ds(c, rows, stride=ROW_TILES), :] = x[:, c * LANES:(c + 1) * LANES]


def _row_tiles_chunks(tiles_ref, rows):
    flat = tiles_ref.reshape(rows * ROW_TILES, LANES)
    return [flat[pl.ds(c, rows, stride=ROW_TILES), :] for c in range(ROW_TILES)]


IN_SLOTS = 3


def _out_router_kernel(mix_hbm, x_hbm, wo_ref, gain_ref, wrc_ref, br_ref,
                       x2_hbm, h2_hbm, route_ref, route_t_ref, cnt_ref, tri_ref, wob_ref, lg_ref,
                       mixbuf_ref, xbuf_ref, insem_ref, x2buf_ref, h2buf_ref, outsem_ref):
    i = pl.program_id(0)
    n = pl.num_programs(0) - 1

    def in_copies(t):
        slot = lax.rem(t, IN_SLOTS)
        rows = pl.ds(t * TM, TM)
        return [pltpu.make_async_copy(mix_hbm.at[rows], mixbuf_ref.at[slot], insem_ref.at[slot]),
                pltpu.make_async_copy(x_hbm.at[rows], xbuf_ref.at[slot], insem_ref.at[slot])]

    @pl.when(i == 0)
    def _():
        for t in range(IN_SLOTS - 1):
            for c in in_copies(t):
                c.start()

    @pl.when(i + IN_SLOTS - 1 < n)
    def _():
        for c in in_copies(i + IN_SLOTS - 1):
            c.start()

    @pl.when(i < n)
    def _():
        for c in in_copies(i):
            c.wait()

    tile_slot = lax.rem(jnp.minimum(i, n - 1), IN_SLOTS)
    mix_ref = mixbuf_ref.at[tile_slot]
    x_ref = xbuf_ref.at[tile_slot]

    @pl.when(i == 0)
    def _():
        cnt_ref[...] = jnp.zeros_like(cnt_ref)
        lg_ref[...] = jnp.zeros_like(lg_ref)
        r = lax.broadcasted_iota(jnp.int32, (TM, TM), 0)
        c = lax.broadcasted_iota(jnp.int32, (TM, TM), 1)
        tri_ref[...] = jnp.where(c < r, 1.0, 0.0).astype(BF16)
        for r0 in range(0, D_MODEL, W_PREP_ROWS):
            wob_ref[r0:r0 + W_PREP_ROWS, :] = wo_ref[0, r0:r0 + W_PREP_ROWS, :].astype(BF16)

    lg = lg_ref[...]
    half = D_MODEL // 2
    mix = mix_ref[...]
    x2_a = x_ref[:, :half] + _dot(mix, wob_ref[:, :half])

    lane = lax.broadcasted_iota(jnp.int32, (TM, LANES), 1).astype(F32)
    neg = -jnp.inf
    is_g = lane < N_GROUPS
    gl = jnp.where(is_g, lg, neg)
    gmax = jnp.max(gl, axis=1, keepdims=True)
    g_top = jnp.min(jnp.where(gl == gmax, lane, float(LANES)), axis=1, keepdims=True)
    g_w = 1.0 / jnp.sum(jnp.where(is_g, jnp.exp(lg - gmax), 0.0), axis=1, keepdims=True)
    first = N_GROUPS + EXPERTS_PER_GROUP * g_top
    el = jnp.where((lane >= first) & (lane < first + EXPERTS_PER_GROUP), lg, neg)
    m1 = jnp.max(el, axis=1, keepdims=True)
    i1 = jnp.min(jnp.where(el == m1, lane, float(LANES)), axis=1, keepdims=True)
    el2 = jnp.where(lane == i1, neg, el)
    m2 = jnp.max(el2, axis=1, keepdims=True)
    i2 = jnp.min(jnp.where(el2 == m2, lane, float(LANES)), axis=1, keepdims=True)
    ratio = jnp.exp(m2 - m1)
    w1 = g_w / (1.0 + ratio)
    w2 = w1 * ratio

    x2_b = x_ref[:, half:] + _dot(mix, wob_ref[:, half:])

    oh1 = jnp.where(lane == i1, 1.0, 0.0)
    oh2 = jnp.where(lane == i2, 1.0, 0.0)
    oh = jnp.where(i > 0, oh1 + oh2, 0.0)
    before = _dot(tri_ref[...], oh.astype(BF16)) + cnt_ref[0:1, :]
    rank1 = jnp.sum(oh1 * before, axis=1, keepdims=True)
    rank2 = jnp.sum(oh2 * before, axis=1, keepdims=True)
    cnt_ref[...] = cnt_ref[...] + jnp.sum(oh, axis=0, keepdims=True)

    rec = jnp.zeros((TM, LANES), F32)
    for col, val in ((R_E, i1 - N_GROUPS), (R_E + 1, i2 - N_GROUPS), (R_RANK, rank1),
                     (R_RANK + 1, rank2), (R_W, w1), (R_W + 1, w2)):
        rec = jnp.where(lane == col, val, rec)
    route_ref[...] = rec
    route_t_ref[0] = rec.T[0:SUBLANES, :]

    x2 = jnp.concatenate([x2_a, x2_b], axis=1)
    h = _rms(x2, gain_ref[...])
    h_hi, h_lo = _split_bf16(h)

    def out_copies(step):
        slot = lax.rem(step, IN_SLOTS)
        rows = pl.ds(step * TM, TM)
        return [pltpu.make_async_copy(x2buf_ref.at[slot], x2_hbm.at[rows], outsem_ref.at[slot]),
                pltpu.make_async_copy(h2buf_ref.at[slot], h2_hbm.at[rows], outsem_ref.at[slot])]

    @pl.when(i >= IN_SLOTS)
    def _():
        for c in out_copies(i - IN_SLOTS):
            c.wait()

    @pl.when(i < n)
    def _():
        slot = lax.rem(i, IN_SLOTS)
        x2buf_ref[slot] = x2
        h2buf_ref[slot] = h_hi
        for c in out_copies(i):
            c.start()

    @pl.when(i == n)
    def _():
        for back in range(IN_SLOTS - 1, 0, -1):
            for c in out_copies(n - back):
                c.wait()
    hh_hl = _dot(h_hi, wrc_ref[...])
    lg_ref[...] = (hh_hl[:, :ROUTER_COLS] + hh_hl[:, ROUTER_COLS:]
                   + _dot(h_lo, wrc_ref[:, :ROUTER_COLS]) + br_ref[...])


def _out_router(mixed, x, w_out, layer, gain, wr_cat, br):
    T = x.shape[0]
    n = T // TM
    row = lambda w: pl.BlockSpec((TM, w), lambda i: (jnp.minimum(i, n - 1), 0))
    lag = lambda i: jnp.maximum(i - 1, 0)
    full = lambda shape: pl.BlockSpec(shape, lambda i: (0, 0))
    wo_spec = pl.BlockSpec((1, D_MODEL, D_MODEL), lambda i: (layer, 0, 0),
                           pipeline_mode=pl.Buffered(1))
    return pl.pallas_call(
        _out_router_kernel,
        grid=(n + 1,),
        in_specs=[pl.BlockSpec(memory_space=pl.ANY), pl.BlockSpec(memory_space=pl.ANY), wo_spec,
                  full((1, D_MODEL)), full((D_MODEL, 2 * ROUTER_COLS)), full((1, ROUTER_COLS))],
        out_specs=[pl.BlockSpec(memory_space=pl.ANY), pl.BlockSpec(memory_space=pl.ANY),
                   pl.BlockSpec((TM, LANES), lambda i: (lag(i), 0)),
                   pl.BlockSpec((1, SUBLANES, TM), lambda i: (lag(i), 0, 0)), full((8, LANES))],
        out_shape=[jax.ShapeDtypeStruct((T, D_MODEL), F32),
                   jax.ShapeDtypeStruct((T, D_MODEL), BF16),
                   jax.ShapeDtypeStruct((T, LANES), F32),
                   jax.ShapeDtypeStruct((T // TM, SUBLANES, TM), F32),
                   jax.ShapeDtypeStruct((8, LANES), F32)],
        scratch_shapes=[pltpu.VMEM((TM, TM), BF16), pltpu.VMEM((D_MODEL, D_MODEL), BF16),
                        pltpu.VMEM((TM, ROUTER_COLS), F32),
                        pltpu.VMEM((IN_SLOTS, TM, D_MODEL), BF16),
                        pltpu.VMEM((IN_SLOTS, TM, D_MODEL), F32),
                        pltpu.SemaphoreType.DMA((IN_SLOTS,)),
                        pltpu.VMEM((IN_SLOTS, TM, D_MODEL), F32),
                        pltpu.VMEM((IN_SLOTS, TM, D_MODEL), BF16),
                        pltpu.SemaphoreType.DMA((IN_SLOTS,))],
        compiler_params=pltpu.CompilerParams(
            dimension_semantics=("arbitrary",), vmem_limit_bytes=VMEM_LIMIT),
        name="out_router",
    )(mixed, x, w_out, gain, wr_cat, br)


def _dispatch_kernel(fill_ref, nu_ref, dest_ref, h_ref, xb_ref, zero_ref, sem_ref, zsem_ref,
                     stage_ref):
    i = pl.program_id(0)
    par = lax.rem(i, 2)
    last = i == pl.num_programs(0) - 1
    n_blocks = xb_ref.shape[0] // MOE_BLK
    spare_fills = [(j >= nu_ref[0], pltpu.make_async_copy(
        zero_ref, xb_ref.at[pl.ds(j * MOE_BLK, MOE_BLK)], zsem_ref.at[1]))
        for j in range(n_blocks - N_EXPERTS, n_blocks)]

    @pl.when(i == 0)
    def _():
        zero_ref[...] = jnp.zeros_like(zero_ref)
        fills = [(fill_ref[e] >= 0, pltpu.make_async_copy(
            zero_ref, xb_ref.at[pl.ds(pl.multiple_of(jnp.maximum(fill_ref[e], 0), MOE_BLK), MOE_BLK)],
            zsem_ref.at[0])) for e in range(N_EXPERTS)]
        for cond, f in fills + spare_fills:
            pl.when(cond)(f.start)
        for cond, f in fills:
            pl.when(cond)(f.wait)

    _row_tiles_store(stage_ref.at[par], h_ref[...].astype(F32))
    for r in range(TM):
        for k in range(TOP_K):
            pltpu.make_async_copy(stage_ref.at[par, r], xb_ref.at[dest_ref[0, 0, k * TM + r]],
                                  sem_ref.at[par]).start(priority=k)

    def wait_tile(p):
        for _ in range(TOP_K):
            pltpu.make_async_copy(stage_ref.at[p], xb_ref.at[pl.ds(0, TM)], sem_ref.at[p]).wait()

    pl.when(i > 0)(lambda: wait_tile(1 - par))
    @pl.when(last)
    def _():
        wait_tile(par)
        for cond, f in spare_fills:
            pl.when(cond)(f.wait)


def _dispatch(fill_start, n_used, dest_tiles, h2, n_rows):
    T = h2.shape[0]
    grid_spec = pltpu.PrefetchScalarGridSpec(
        num_scalar_prefetch=2,
        grid=(T // TM,),
        in_specs=[
            pl.BlockSpec((1, 1, TOP_K * TM), lambda i, fs, nu: (i, 0, 0), memory_space=pltpu.SMEM),
            pl.BlockSpec((TM, D_MODEL), lambda i, fs, nu: (i, 0)),
        ],
        out_specs=pl.BlockSpec(memory_space=pl.ANY),
        scratch_shapes=[pltpu.VMEM((MOE_BLK, ROW_TILES, LANES), F32),
                        pltpu.SemaphoreType.DMA((2,)), pltpu.SemaphoreType.DMA((2,)),
                        pltpu.VMEM((2, TM, ROW_TILES, LANES), F32)],
    )
    return pl.pallas_call(
        _dispatch_kernel,
        grid_spec=grid_spec,
        out_shape=jax.ShapeDtypeStruct((n_rows, ROW_TILES, LANES), F32),
        compiler_params=pltpu.CompilerParams(dimension_semantics=("arbitrary",)),
        name="dispatch",
    )(fill_start, n_used, dest_tiles, h2)


BLOCK_COPY_PARTS = 4
X_SLOTS = 4
Y_SLOTS = 3


class _CopyGroup:
    def __init__(self, copies):
        self.copies = copies

    def start(self):
        for n, c in enumerate(self.copies):
            c.start(priority=n % 2)

    def wait(self):
        for c in self.copies:
            c.wait()


def _expert_kernel(be_ref, nxt_ref, nu_ref, xb_ref, wg_ref, wu_ref, wd_ref, yb_ref,
                   xbuf_ref, ybuf_ref, wgs_ref, wus_ref, wds_ref, wgb_ref, wub_ref, wdb_ref,
                   xsem_ref, ysem_ref, wsem_ref, zbuf_ref, zsem_ref, *, layer, n_blocks):
    n_used = nu_ref[0]

    part = MOE_BLK // BLOCK_COPY_PARTS

    def x_copy(j, slot):
        return _CopyGroup([pltpu.make_async_copy(
            xb_ref.at[pl.ds(j * MOE_BLK + p * part, part)],
            xbuf_ref.at[slot, pl.ds(p * part, part)], xsem_ref.at[slot])
            for p in range(BLOCK_COPY_PARTS)])

    def y_copy(j, slot):
        return _CopyGroup([pltpu.make_async_copy(
            ybuf_ref.at[slot, pl.ds(p * part, part)],
            yb_ref.at[pl.ds(j * MOE_BLK + p * part, part)], ysem_ref.at[slot])
            for p in range(BLOCK_COPY_PARTS)])

    def w_copies(e, ws):
        return [pltpu.make_async_copy(src.at[layer, e], dst.at[ws], wsem_ref.at[ws])
                for src, dst in ((wg_ref, wgs_ref), (wu_ref, wus_ref), (wd_ref, wds_ref))]

    def fill_copy(j):
        return pltpu.make_async_copy(zbuf_ref, yb_ref.at[pl.ds(j * MOE_BLK, MOE_BLK)], zsem_ref)

    for j0 in range(X_SLOTS - 1):
        x_copy(j0, j0).start()
    for c in w_copies(be_ref[0], 0):
        c.start()

    zbuf_ref[...] = jnp.zeros_like(zbuf_ref)

    def fill(j, carry):
        fill_copy(j).start()
        return carry

    lax.fori_loop(n_used, n_blocks, fill, 0)

    def block(j, ws):
        slot = lax.rem(j, Y_SLOTS)
        xslot = lax.rem(j, X_SLOTS)
        first = (j == 0) | (be_ref[j] != be_ref[jnp.maximum(j - 1, 0)])
        ws = jnp.where(first & (j > 0), 1 - ws, ws)

        @pl.when(first)
        def _():
            for c in w_copies(be_ref[j], ws):
                c.wait()
            for r0 in range(0, D_MODEL, W_PREP_ROWS):
                rows = slice(r0, r0 + W_PREP_ROWS)
                wgb_ref[rows, :] = wgs_ref[ws, rows, :].astype(BF16)
                wub_ref[rows, :] = wus_ref[ws, rows, :].astype(BF16)
            for r0 in range(0, D_EXPERT, W_PREP_ROWS):
                rows = slice(r0, r0 + W_PREP_ROWS)
                wdb_ref[rows, :] = wds_ref[ws, rows, :].astype(BF16)

            @pl.when(nxt_ref[j] >= 0)
            def _():
                for c in w_copies(nxt_ref[j], 1 - ws):
                    c.start()

        ahead = j + X_SLOTS - 1

        @pl.when(ahead < n_used)
        def _():
            x_copy(ahead, lax.rem(ahead, X_SLOTS)).start()

        x_copy(j, xslot).wait()
        x = jnp.concatenate(
            [c.astype(BF16) for c in _row_tiles_chunks(xbuf_ref.at[xslot], MOE_BLK)], axis=1)
        g = _dot(x, wgb_ref[...])
        u = _dot(x, wub_ref[...])
        h = (g * (1.0 / (1.0 + jnp.exp(-g)))) * u
        y = _dot(h.astype(BF16), wdb_ref[...])

        @pl.when(j >= Y_SLOTS)
        def _():
            y_copy(j - Y_SLOTS, slot).wait()

        _row_tiles_store(ybuf_ref.at[slot], y)
        y_copy(j, slot).start()
        return ws

    lax.fori_loop(0, n_used, block, jnp.int32(0))

    for back in range(Y_SLOTS, 0, -1):
        y_copy(n_used - back, lax.rem(n_used - back, Y_SLOTS)).wait()

    def fill_wait(j, carry):
        fill_copy(j).wait()
        return carry

    lax.fori_loop(n_used, n_blocks, fill_wait, 0)


def _experts(blk_exp, nxt_exp, n_used, xb, w_gate, w_up, w_down, layer):
    n_blocks = blk_exp.shape[0]
    any_spec = pl.BlockSpec(memory_space=pl.ANY)
    blk = (MOE_BLK, ROW_TILES, LANES)
    grid_spec = pltpu.PrefetchScalarGridSpec(
        num_scalar_prefetch=3,
        grid=(1,),
        in_specs=[any_spec, any_spec, any_spec, any_spec],
        out_specs=any_spec,
        scratch_shapes=[
            pltpu.VMEM((X_SLOTS,) + blk, F32), pltpu.VMEM((Y_SLOTS,) + blk, F32),
            pltpu.VMEM((2, D_MODEL, D_EXPERT), F32), pltpu.VMEM((2, D_MODEL, D_EXPERT), F32),
            pltpu.VMEM((2, D_EXPERT, D_MODEL), F32),
            pltpu.VMEM((D_MODEL, D_EXPERT), BF16), pltpu.VMEM((D_MODEL, D_EXPERT), BF16),
            pltpu.VMEM((D_EXPERT, D_MODEL), BF16),
            pltpu.SemaphoreType.DMA((X_SLOTS,)), pltpu.SemaphoreType.DMA((Y_SLOTS,)),
            pltpu.SemaphoreType.DMA((2,)),
            pltpu.VMEM(blk, F32), pltpu.SemaphoreType.DMA(()),
        ],
    )
    return pl.pallas_call(
        functools.partial(_expert_kernel, layer=layer, n_blocks=n_blocks),
        grid_spec=grid_spec,
        out_shape=jax.ShapeDtypeStruct((n_blocks * MOE_BLK, ROW_TILES, LANES), F32),
        compiler_params=pltpu.CompilerParams(
            dimension_semantics=("arbitrary",), vmem_limit_bytes=VMEM_LIMIT),
        name="experts",
    )(blk_exp, nxt_exp, n_used, xb, w_gate, w_up, w_down)


def _dispatch_tables(route_t, counts_rec, T):
    counts = counts_rec[0, N_GROUPS:N_GROUPS + N_EXPERTS].astype(jnp.int32)
    n_steps = (T * TOP_K) // MOE_BLK + N_EXPERTS
    nblk = (counts + MOE_BLK - 1) // MOE_BLK
    bend = jnp.cumsum(nblk)
    pstart = (bend - nblk) * MOE_BLK
    n_used = bend[-1]
    j = jnp.minimum(jnp.arange(n_steps, dtype=jnp.int32), n_used - 1)
    blk_exp = jnp.minimum(jnp.sum(j[:, None] >= bend[None, :], axis=1), N_EXPERTS - 1)
    n_rows = n_steps * MOE_BLK
    last_blk = jnp.where(counts > 0, (bend - 1) * MOE_BLK, -1)
    ids = jnp.arange(N_EXPERTS, dtype=jnp.int32)
    later = (ids[None, :] > ids[:, None]) & (nblk[None, :] > 0)
    nxt_of = jnp.min(jnp.where(later, ids[None, :], N_EXPERTS), axis=1)
    nxt_tab = jnp.where(nxt_of < N_EXPERTS, nxt_of, -1)
    nxt_exp = jnp.sum(jnp.where(blk_exp[:, None] == ids[None, :], nxt_tab[None, :], 0), axis=1)
    e = route_t[:, R_E:R_E + TOP_K, :].astype(jnp.int32)
    rank = route_t[:, R_RANK:R_RANK + TOP_K, :].astype(jnp.int32)
    seg = jnp.sum(jnp.where(e[..., None] == jnp.arange(N_EXPERTS), pstart, 0), axis=-1)
    dest = jnp.clip(seg + rank, 0, n_steps * MOE_BLK - 1)
    dest_tiles = dest.reshape(T // TM, 1, TOP_K * TM)
    return dict(dest_tiles=dest_tiles, fill_start=last_blk.astype(jnp.int32),
                blk_exp=blk_exp.astype(jnp.int32), nxt_exp=nxt_exp.astype(jnp.int32),
                n_used=n_used.reshape(1).astype(jnp.int32), n_rows=n_rows)


def _prep_layer(l, w_gk_up, b_gk, gla_norm, gmlp_norm, w_spatial, b_spatial, w_conv,
                w_router_group, b_router_group, w_router_expert, b_router_expert):
    wgk = jnp.concatenate(
        [w_gk_up[l], jnp.zeros((LANES - GLA_GATE_RANK, GLA_KDIM), F32)], axis=0).astype(BF16)
    wsp = w_spatial[l].transpose(1, 0, 2).reshape(GMLP_CHUNK, GMLP_HEADS * GMLP_CHUNK)
    bsp = jnp.repeat(b_spatial[l].T, GMLP_DH, axis=1)
    wconv = jnp.concatenate([w_conv[l], jnp.zeros((8 - CONV_K, CONV_WIDTH), F32)], axis=0)
    wr = jnp.concatenate(
        [w_router_group[l], w_router_expert[l],
         jnp.zeros((D_MODEL, ROUTER_COLS - N_GROUPS - N_EXPERTS), F32)], axis=1)
    wr_hi = wr.astype(BF16)
    wr_lo = (wr - wr_hi.astype(F32)).astype(BF16)
    br = jnp.concatenate(
        [b_router_group[l], b_router_expert[l],
         jnp.zeros((ROUTER_COLS - N_GROUPS - N_EXPERTS,), F32)])[None, :]
    return dict(
        wgk=wgk, bgk=b_gk[l][None, :], glan=gla_norm[l][None, :], gmn=gmlp_norm[l][None, :],
        wsp=wsp, bsp=bsp, wconv=wconv, wr_cat=jnp.concatenate([wr_hi, wr_lo], axis=1), br=br)


def kernel(x, attn_norm, w_in, w_gk_up, b_gk, gla_norm, gmlp_norm, w_spatial, b_spatial, w_conv, w_out, ffn_norm, w_router_group, b_router_group, w_router_expert, b_router_expert, w_gate, w_up, w_down, final_norm):
    B, S, D = x.shape
    T = B * S
    depth = w_in.shape[0]
    xr = x.reshape(T, D)
    w_in_t = jnp.swapaxes(w_in, 1, 2)
    moe = None
    for l in range(depth):
        p = _prep_layer(l, w_gk_up, b_gk, gla_norm, gmlp_norm, w_spatial, b_spatial, w_conv,
                        w_router_group, b_router_group, w_router_expert, b_router_expert)
        mix_params = (p["wgk"], p["bgk"], p["glan"], p["gmn"], p["wsp"], p["bsp"], p["wconv"])
        if moe is None:
            mixed = _front(xr, attn_norm[l][None, :], w_in_t, l, mix_params, B, S)
        else:
            xr, mixed = _front(moe["x2"], attn_norm[l][None, :], w_in_t, l, mix_params, B, S, moe)
        x2, h2, route, route_t, counts_rec = _out_router(
            mixed, xr, w_out, l, ffn_norm[l][None, :], p["wr_cat"], p["br"])
        moe = _dispatch_tables(route_t, counts_rec, T)
        xb = _dispatch(moe["fill_start"], moe["n_used"], moe["dest_tiles"], h2, moe["n_rows"])
        yb = _experts(moe["blk_exp"], moe["nxt_exp"], moe["n_used"], xb, w_gate, w_up, w_down, l)
        moe.update(x2=x2, route=route, yb=yb)
    out = _combine_final_norm(moe["dest_tiles"], moe["x2"], moe["route"], moe["yb"],
                              final_norm[None, :])
    return out.reshape(B, S, D)
```

```python
import functools

import jax
import jax.numpy as jnp
from jax import lax
from jax.experimental import pallas as pl
from jax.experimental.pallas import tpu as pltpu

F32 = jnp.float32
BF16 = jnp.bfloat16

D_MODEL = 1024
RMS_EPS = 1e-6
GLA_HEADS = 4
GLA_WIDTH = 512
GLA_DV = 128
GLA_DK = 64
GLA_KDIM = 256
GLA_GATE_RANK = 16
GLA_GATE_NORM = 16.0
GLA_CHUNK = 64
GMLP_HEADS = 4
GMLP_WIDTH = 256
GMLP_DH = 64
GMLP_CHUNK = 128
CONV_WIDTH = 256
CONV_K = 3
N_GROUPS = 4
EXPERTS_PER_GROUP = 8
N_EXPERTS = 32
TOP_K = 2
D_EXPERT = 256

LANES = 128
C_Q, C_K, C_V, C_G = 0, 256, 512, 1024
C_U, C_VG, C_X, C_BG, C_CG, C_GKL = 1536, 1792, 2048, 2304, 2560, 2816
D_PROJ = C_GKL + LANES
D_IN = C_GKL + GLA_GATE_RANK

TM = 256
TS_MIX = TM
MOE_BLK = 256
ROUTER_COLS = LANES
SUBLANES = 8
ROW_TILES = D_MODEL // LANES
assert ROW_TILES == SUBLANES
VMEM_LIMIT = 56 * 1024 * 1024
R_E, R_RANK, R_W = 0, 2, 4


def _dot(a, b):
    return jnp.dot(a, b, preferred_element_type=F32)


def _split_bf16(x):
    hi = x.astype(BF16)
    lo = (x - hi.astype(F32)).astype(BF16)
    return hi, lo


def _rms(x, gain):
    return x * lax.rsqrt(jnp.mean(x * x, axis=-1, keepdims=True) + RMS_EPS) * gain


W_PREP_ROWS = 128
PROJ_CHUNK = 256


def _stage_w_in(wt_ref, wb_ref):
    for c0 in range(0, C_GKL, LANES):
        src = c0 if c0 < C_U else c0 + GLA_GATE_RANK
        wb_ref[:, c0:c0 + LANES] = wt_ref[0, src:src + LANES, :].T.astype(BF16)
    low = jnp.concatenate([wt_ref[0, C_U:C_U + GLA_GATE_RANK, :],
                           jnp.zeros((LANES - GLA_GATE_RANK, D_MODEL), F32)], axis=0)
    wb_ref[:, C_GKL:D_PROJ] = low.T.astype(BF16)


def _row_gather_copy(yb_ref, buf_ref, sem_ref, slot, k, r, d):
    return pltpu.make_async_copy(yb_ref.at[d], buf_ref.at[slot, k, r], sem_ref.at[slot])


def _gather_start(dest_ref, yb_ref, buf_ref, sem_ref, slot, rows=range(TM)):
    for r in rows:
        for k in range(TOP_K):
            _row_gather_copy(yb_ref, buf_ref, sem_ref, slot, k, r,
                             dest_ref[0, 0, k * TM + r]).start(priority=k)


def _gather_wait(yb_ref, buf_ref, sem_ref, slot):
    for k in range(TOP_K):
        pltpu.make_async_copy(yb_ref.at[pl.ds(0, TM)], buf_ref.at[slot, k], sem_ref.at[slot]).wait()


def _combined_residual(dcur_ref, x_ref, route_ref, yb_ref, buf_ref, sem_ref):
    i = pl.program_id(0)
    slot = lax.rem(i, 2)

    @pl.when(i == 0)
    def _():
        _gather_start(dcur_ref, yb_ref, buf_ref, sem_ref, 0)

    _gather_wait(yb_ref, buf_ref, sem_ref, slot)
    w0 = route_ref[:, R_W:R_W + 1]
    w1 = route_ref[:, R_W + 1:R_W + 2]
    y0 = _row_tiles_chunks(buf_ref.at[slot, 0], TM)
    y1 = _row_tiles_chunks(buf_ref.at[slot, 1], TM)
    return jnp.concatenate(
        [x_ref[:, c * LANES:(c + 1) * LANES] + (w0 * y0[c] + w1 * y1[c]) for c in range(ROW_TILES)],
        axis=1)


def _prefetch_groups(n_groups):
    per = -(-TM // n_groups)
    return [range(g * per, min(TM, (g + 1) * per)) for g in range(n_groups)]


def _drain_last_prefetch(yb_ref, buf_ref, sem_ref):
    i = pl.program_id(0)

    @pl.when(i == pl.num_programs(0) - 1)
    def _():
        _gather_wait(yb_ref, buf_ref, sem_ref, 1 - lax.rem(i, 2))


def _combine_specs(n_tiles):
    smem_tile = lambda f: pl.BlockSpec((1, 1, TOP_K * TM), f, memory_space=pltpu.SMEM)
    return [
        smem_tile(lambda i: (i, 0, 0)),
        smem_tile(lambda i: (jnp.minimum(i + 1, n_tiles - 1), 0, 0)),
        pl.BlockSpec((TM, D_MODEL), lambda i: (i, 0)),
        pl.BlockSpec((TM, LANES), lambda i: (i, 0)),
        pl.BlockSpec(memory_space=pl.ANY),
    ]


_COMBINE_SCRATCH = [pltpu.VMEM((2, TOP_K, TM, ROW_TILES, LANES), F32),
                    pltpu.SemaphoreType.DMA((2,))]


def _combine_final_norm_kernel(dcur_ref, dnxt_ref, x_ref, route_ref, yb_ref, gain_ref,
                               o_ref, buf_ref, sem_ref):
    _gather_start(dnxt_ref, yb_ref, buf_ref, sem_ref, 1 - lax.rem(pl.program_id(0), 2))
    x = _combined_residual(dcur_ref, x_ref, route_ref, yb_ref, buf_ref, sem_ref)
    o_ref[...] = _rms(x, gain_ref[...])
    _drain_last_prefetch(yb_ref, buf_ref, sem_ref)


def _combine_final_norm(dest_tiles, x2, route, yb, gain):
    T = x2.shape[0]
    n_tiles = T // TM
    return pl.pallas_call(
        _combine_final_norm_kernel,
        grid=(n_tiles,),
        in_specs=_combine_specs(n_tiles) + [pl.BlockSpec((1, D_MODEL), lambda i: (0, 0))],
        out_specs=pl.BlockSpec((TM, D_MODEL), lambda i: (i, 0)),
        out_shape=jax.ShapeDtypeStruct((T, D_MODEL), F32),
        scratch_shapes=_COMBINE_SCRATCH,
        compiler_params=pltpu.CompilerParams(
            dimension_semantics=("arbitrary",), vmem_limit_bytes=VMEM_LIMIT),
        name="combine_final_norm",
    )(dest_tiles, dest_tiles, x2, route, yb, gain)


def _gelu_tanh(x):
    c = 0.7978845608028654
    return x * (0.5 * (1.0 + jnp.tanh(c * (x + 0.044715 * (x * x * x)))))


def _mixer_kernel(proj_ref, wgk_ref, bgk_ref, glan_ref, gmn_ref, wsp_ref, bsp_ref, wconv_ref,
                  out_ref, st_ref, hc_ref, lcat_ref, wm_ref, *, seq_start, first_step, between):
    TS = TS_MIX
    n_gla = TS // GLA_CHUNK
    n_gm = TS // GMLP_CHUNK

    @pl.when(seq_start)
    def _():
        st_ref[...] = jnp.zeros_like(st_ref)
        hc_ref[...] = jnp.zeros_like(hc_ref)

    @pl.when(first_step)
    def _():
        r = lax.broadcasted_iota(jnp.int32, (TS, TS), 0)
        c = lax.broadcasted_iota(jnp.int32, (TS, TS), 1)
        keep = ((r // GLA_CHUNK) == (c // GLA_CHUNK)) & (c <= r)
        lcat_ref[...] = jnp.where(keep, 1.0, 0.0).astype(BF16)
        t = lax.broadcasted_iota(jnp.int32, (GMLP_CHUNK, GMLP_HEADS * GMLP_CHUNK), 0)
        s = lax.broadcasted_iota(jnp.int32, (GMLP_CHUNK, GMLP_HEADS * GMLP_CHUNK), 1) % GMLP_CHUNK
        wm_ref[...] = jnp.where(s <= t, wsp_ref[...], 0.0).astype(BF16)

    lane256 = lax.broadcasted_iota(jnp.int32, (1, GLA_KDIM), 1)

    q = proj_ref[:, C_Q:C_Q + GLA_KDIM].astype(F32)
    k = proj_ref[:, C_K:C_K + GLA_KDIM].astype(F32)
    v_b = proj_ref[:, C_V:C_V + GLA_WIDTH]
    z = _dot(proj_ref[:, C_GKL:C_GKL + LANES], wgk_ref[...]) + bgk_ref[...]
    gk = (jnp.minimum(z, 0.0) - jnp.log1p(jnp.exp(-jnp.abs(z)))) * (1.0 / GLA_GATE_NORM)
    gk_hi, gk_lo = _split_bf16(gk)
    cs = _dot(lcat_ref[...], jnp.concatenate([gk_hi, gk_lo], axis=1))
    b = cs[:, :GLA_KDIM] + cs[:, GLA_KDIM:]
    b_last = [b[(c + 1) * GLA_CHUNK - 1:(c + 1) * GLA_CHUNK, :] for c in range(n_gla)]
    bl = jnp.concatenate(
        [jnp.broadcast_to(t, (GLA_CHUNK, GLA_KDIM)) for t in b_last], axis=0)
    q_dec = (q * (GLA_DK ** -0.5)) * jnp.exp(b)
    k_inv = (k * jnp.exp(-b)).astype(BF16)
    k_dec = (k * jnp.exp(bl - b)).astype(BF16)
    q_dec_b = q_dec.astype(BF16)

    zero_b = jnp.zeros_like(q_dec_b)
    q_stack = jnp.concatenate(
        [jnp.where((lane256 // GLA_DK) == h, q_dec_b, zero_b) for h in range(GLA_HEADS)], axis=0)
    scores = lax.dot_general(q_stack, k_inv, (((1,), (1,)), ((), ())),
                             preferred_element_type=F32)
    rt = lax.broadcasted_iota(jnp.int32, (TS, TS), 0)
    ct = lax.broadcasted_iota(jnp.int32, (TS, TS), 1)
    causal = ((rt // GLA_CHUNK) == (ct // GLA_CHUNK)) & (ct <= rt)
    o_heads = []
    for h in range(GLA_HEADS):
        p_h = jnp.where(causal, scores[h * TS:(h + 1) * TS, :], 0.0).astype(BF16)
        o_heads.append(_dot(p_h, v_b[:, h * GLA_DV:(h + 1) * GLA_DV]))

    sr = lax.broadcasted_iota(jnp.int32, (GLA_WIDTH, GLA_KDIM), 0) // GLA_DV
    sc = lax.broadcasted_iota(jnp.int32, (GLA_WIDTH, GLA_KDIM), 1) // GLA_DK
    bd_mask = sr == sc
    o_inter = []
    for c in range(n_gla):
        rows = slice(c * GLA_CHUNK, (c + 1) * GLA_CHUNK)
        st = st_ref[...]
        o_inter.append(lax.dot_general(q_dec_b[rows], st.astype(BF16), (((1,), (1,)), ((), ())),
                                       preferred_element_type=F32))
        upd = lax.dot_general(v_b[rows], k_dec[rows], (((0,), (0,)), ((), ())),
                              preferred_element_type=F32)
        decay = jnp.exp(b_last[c])
        st_ref[...] = st * decay + jnp.where(bd_mask, upd, 0.0)
    o_inter = jnp.concatenate(o_inter, axis=0)

    for h in range(GLA_HEADS):
        cols = slice(h * GLA_DV, (h + 1) * GLA_DV)
        o = o_heads[h] + o_inter[:, cols]
        o = o * lax.rsqrt(jnp.mean(o * o, axis=-1, keepdims=True) + RMS_EPS) * glan_ref[...]
        g = proj_ref[:, C_G + h * GLA_DV:C_G + (h + 1) * GLA_DV].astype(F32)
        out_ref[:, cols] = (o * (g * (1.0 / (1.0 + jnp.exp(-g))))).astype(out_ref.dtype)

    between("gla_done")
    u = _gelu_tanh(proj_ref[:, C_U:C_U + GMLP_WIDTH].astype(F32))
    vg = _gelu_tanh(proj_ref[:, C_VG:C_VG + GMLP_WIDTH].astype(F32))
    hr = lax.broadcasted_iota(jnp.int32, (GMLP_WIDTH, GMLP_WIDTH), 0) // GMLP_DH
    hcn = lax.broadcasted_iota(jnp.int32, (GMLP_WIDTH, GMLP_WIDTH), 1) // GMLP_DH
    head_mean = jnp.where(hr == hcn, 1.0 / GMLP_DH, 0.0).astype(BF16)
    sq_hi, sq_lo = _split_bf16(vg * vg)
    ms = _dot(sq_hi, head_mean) + _dot(sq_lo, head_mean)
    v32 = vg * lax.rsqrt(ms + RMS_EPS) * gmn_ref[...]
    for c in range(n_gm):
        rows = slice(c * GMLP_CHUNK, (c + 1) * GMLP_CHUNK)
        vc = v32[rows].astype(BF16)
        zc = jnp.zeros_like(vc)
        rhs = jnp.concatenate(
            [jnp.where((lane256 // GMLP_DH) == h, vc, zc) for h in range(GMLP_HEADS)], axis=0)
        mixed = _dot(wm_ref[...], rhs) + bsp_ref[...]
        out_ref[rows, GLA_WIDTH:GLA_WIDTH + GMLP_WIDTH] = (u[rows] * mixed).astype(out_ref.dtype)

    between("gmlp_done")
    hcv = (proj_ref[:, C_CG:C_CG + CONV_WIDTH].astype(F32)
           * proj_ref[:, C_X:C_X + CONV_WIDTH].astype(F32))
    hc_ref[8:8 + TS, :] = hcv
    y = (wconv_ref[2:3, :] * hcv + wconv_ref[1:2, :] * hc_ref[7:7 + TS, :]
         + wconv_ref[0:1, :] * hc_ref[6:6 + TS, :])
    out_ref[:, GLA_WIDTH + GMLP_WIDTH:] = (
        proj_ref[:, C_BG:C_BG + CONV_WIDTH].astype(F32) * y).astype(out_ref.dtype)
    hc_ref[0:8, :] = hc_ref[TS:TS + 8, :]


_MIXER_SCRATCH = [
    pltpu.VMEM((GLA_WIDTH, GLA_KDIM), F32),
    pltpu.VMEM((TS_MIX + 8, CONV_WIDTH), F32),
    pltpu.VMEM((TS_MIX, TS_MIX), BF16),
    pltpu.VMEM((GMLP_CHUNK, GMLP_HEADS * GMLP_CHUNK), BF16),
]


N_MIX_PARAMS = 7
PROJ_CHUNKS_AT = {"gla_done": 4, "gmlp_done": 4}


def _front_kernel(*refs, tiles_per_seq, combine):
    refs = list(refs)
    if combine:
        dcur_ref, dnxt_ref, x_ref, route_ref, yb_ref = refs[:5]
        del refs[:5]
    else:
        x_ref = refs.pop(0)
    gain_ref, wt_ref = refs[:2]
    mix_refs = refs[2:2 + N_MIX_PARAMS]
    del refs[:2 + N_MIX_PARAMS]
    if combine:
        xo_ref, out_ref, buf_ref, sem_ref = refs[:4]
        del refs[:4]
    else:
        out_ref = refs.pop(0)
    st_ref, hc_ref, lcat_ref, wm_ref, wb_ref, pcur_ref, pnext_ref = refs
    s = pl.program_id(0)

    @pl.when(s == 0)
    def _():
        _stage_w_in(wt_ref, wb_ref)
        pcur_ref[...] = jnp.zeros_like(pcur_ref)

    col_chunks = [(c0, min(c0 + PROJ_CHUNK, D_PROJ)) for c0 in range(0, D_PROJ, PROJ_CHUNK)]
    work = list(zip(_prefetch_groups(len(col_chunks)), col_chunks))

    def project(h, n):
        for _ in range(min(n, len(work))):
            rows, (c0, c1) = work.pop(0)
            if combine:
                _gather_start(dnxt_ref, yb_ref, buf_ref, sem_ref, 1 - lax.rem(s, 2), rows)
            pnext_ref[:, c0:c1] = _dot(h, wb_ref[:, c0:c1]).astype(BF16)

    mixers = functools.partial(
        _mixer_kernel, pcur_ref, *mix_refs, out_ref, st_ref, hc_ref, lcat_ref, wm_ref,
        seq_start=lax.rem(jnp.maximum(s - 1, 0), tiles_per_seq) == 0, first_step=s == 0)
    if combine:
        x = _combined_residual(dcur_ref, x_ref, route_ref, yb_ref, buf_ref, sem_ref)
        xo_ref[...] = x
        h = _rms(x, gain_ref[...]).astype(BF16)
        mixers(between=lambda site: project(h, PROJ_CHUNKS_AT[site]))
        project(h, len(work))
        _drain_last_prefetch(yb_ref, buf_ref, sem_ref)
    else:
        mixers(between=lambda site: None)
        project(_rms(x_ref[...], gain_ref[...]).astype(BF16), len(work))
    pcur_ref[...] = pnext_ref[...]


def _front(x, gain, w_in_t, layer, mix_params, batch, seq, moe=None):
    n_seq = seq // TS_MIX
    n = batch * n_seq
    T = batch * seq
    cur = lambda s: jnp.minimum(s, n - 1)
    full = lambda shape: pl.BlockSpec(shape, lambda s: (0,) * len(shape))
    row = lambda w, f: pl.BlockSpec((TM, w), lambda s: (f(s), 0))
    in_specs, args = [row(D_MODEL, cur)], [x]
    out_specs = [row(D_MODEL, lambda s: jnp.maximum(s - 1, 0))]
    out_shape = [jax.ShapeDtypeStruct((T, D_MODEL), BF16)]
    scratch = list(_MIXER_SCRATCH)
    if moe is not None:
        smem_tile = lambda f: pl.BlockSpec((1, 1, TOP_K * TM), lambda s: (f(s), 0, 0),
                                           memory_space=pltpu.SMEM)
        in_specs = [smem_tile(cur), smem_tile(lambda s: jnp.minimum(s + 1, n - 1))] + in_specs + [
            row(LANES, cur), pl.BlockSpec(memory_space=pl.ANY)]
        args = [moe["dest_tiles"], moe["dest_tiles"]] + args + [moe["route"], moe["yb"]]
        out_specs = [row(D_MODEL, cur)] + out_specs
        out_shape = [jax.ShapeDtypeStruct((T, D_MODEL), F32)] + out_shape
        scratch = _COMBINE_SCRATCH + scratch
    in_specs += [
        full((1, D_MODEL)),
        pl.BlockSpec((1, D_IN, D_MODEL), lambda s: (layer, 0, 0), pipeline_mode=pl.Buffered(1)),
        full((LANES, GLA_KDIM)), full((1, GLA_KDIM)), full((1, GLA_DV)), full((1, GMLP_WIDTH)),
        full((GMLP_CHUNK, GMLP_HEADS * GMLP_CHUNK)), full((GMLP_CHUNK, GMLP_WIDTH)),
        full((8, CONV_WIDTH)),
    ]
    scratch += [pltpu.VMEM((D_MODEL, D_PROJ), BF16), pltpu.VMEM((TS_MIX, D_PROJ), BF16),
                pltpu.VMEM((TS_MIX, D_PROJ), BF16)]
    return pl.pallas_call(
        functools.partial(_front_kernel, tiles_per_seq=n_seq, combine=moe is not None),
        grid=(n + 1,),
        in_specs=in_specs,
        out_specs=out_specs if moe is not None else out_specs[0],
        out_shape=out_shape if moe is not None else out_shape[0],
        scratch_shapes=scratch,
        compiler_params=pltpu.CompilerParams(
            dimension_semantics=("arbitrary",), vmem_limit_bytes=VMEM_LIMIT),
        name="front",
    )(*args, gain, w_in_t, *mix_params)


def _row_tiles_store(tiles_ref, x):
    rows = x.shape[0]
    flat = tiles_ref.reshape(rows * ROW_TILES, LANES)
    for c in range(ROW_TILES):
        flat[pl.ds(c, rows, stride=ROW_TILES), :] = x[:, c * LANES:(c + 1) * LANES]


def _row_tiles_chunks(tiles_ref, rows):
    flat = tiles_ref.reshape(rows * ROW_TILES, LANES)
    return [flat[pl.ds(c, rows, stride=ROW_TILES), :] for c in range(ROW_TILES)]


IN_SLOTS = 4


def _out_router_kernel(mix_hbm, x_hbm, wo_ref, gain_ref, wrc_ref, br_ref,
                       x2_ref, h2_ref, route_ref, route_t_ref, cnt_ref, tri_ref, wob_ref, lg_ref,
                       mixbuf_ref, xbuf_ref, insem_ref):
    i = pl.program_id(0)
    n = pl.num_programs(0) - 1

    def in_copies(t):
        slot = lax.rem(t, IN_SLOTS)
        rows = pl.ds(t * TM, TM)
        return [pltpu.make_async_copy(mix_hbm.at[rows], mixbuf_ref.at[slot], insem_ref.at[slot]),
                pltpu.make_async_copy(x_hbm.at[rows], xbuf_ref.at[slot], insem_ref.at[slot])]

    @pl.when(i == 0)
    def _():
        for t in range(IN_SLOTS - 1):
            for c in in_copies(t):
                c.start()

    @pl.when(i + IN_SLOTS - 1 < n)
    def _():
        for c in in_copies(i + IN_SLOTS - 1):
            c.start()

    @pl.when(i < n)
    def _():
        for c in in_copies(i):
            c.wait()

    tile_slot = lax.rem(jnp.minimum(i, n - 1), IN_SLOTS)
    mix_ref = mixbuf_ref.at[tile_slot]
    x_ref = xbuf_ref.at[tile_slot]

    @pl.when(i == 0)
    def _():
        cnt_ref[...] = jnp.zeros_like(cnt_ref)
        lg_ref[...] = jnp.zeros_like(lg_ref)
        r = lax.broadcasted_iota(jnp.int32, (TM, TM), 0)
        c = lax.broadcasted_iota(jnp.int32, (TM, TM), 1)
        tri_ref[...] = jnp.where(c < r, 1.0, 0.0).astype(BF16)
        for r0 in range(0, D_MODEL, W_PREP_ROWS):
            wob_ref[r0:r0 + W_PREP_ROWS, :] = wo_ref[0, r0:r0 + W_PREP_ROWS, :].astype(BF16)

    lg = lg_ref[...]
    half = D_MODEL // 2
    mix = mix_ref[...]
    x2_a = x_ref[:, :half] + _dot(mix, wob_ref[:, :half])

    lane = lax.broadcasted_iota(jnp.int32, (TM, LANES), 1).astype(F32)
    neg = -jnp.inf
    is_g = lane < N_GROUPS
    gl = jnp.where(is_g, lg, neg)
    gmax = jnp.max(gl, axis=1, keepdims=True)
    g_top = jnp.min(jnp.where(gl == gmax, lane, float(LANES)), axis=1, keepdims=True)
    g_w = 1.0 / jnp.sum(jnp.where(is_g, jnp.exp(lg - gmax), 0.0), axis=1, keepdims=True)
    first = N_GROUPS + EXPERTS_PER_GROUP * g_top
    el = jnp.where((lane >= first) & (lane < first + EXPERTS_PER_GROUP), lg, neg)
    m1 = jnp.max(el, axis=1, keepdims=True)
    i1 = jnp.min(jnp.where(el == m1, lane, float(LANES)), axis=1, keepdims=True)
    el2 = jnp.where(lane == i1, neg, el)
    m2 = jnp.max(el2, axis=1, keepdims=True)
    i2 = jnp.min(jnp.where(el2 == m2, lane, float(LANES)), axis=1, keepdims=True)
    ratio = jnp.exp(m2 - m1)
    w1 = g_w / (1.0 + ratio)
    w2 = w1 * ratio

    x2_b = x_ref[:, half:] + _dot(mix, wob_ref[:, half:])

    oh1 = jnp.where(lane == i1, 1.0, 0.0)
    oh2 = jnp.where(lane == i2, 1.0, 0.0)
    oh = jnp.where(i > 0, oh1 + oh2, 0.0)
    before = _dot(tri_ref[...], oh.astype(BF16)) + cnt_ref[0:1, :]
    rank1 = jnp.sum(oh1 * before, axis=1, keepdims=True)
    rank2 = jnp.sum(oh2 * before, axis=1, keepdims=True)
    cnt_ref[...] = cnt_ref[...] + jnp.sum(oh, axis=0, keepdims=True)

    rec = jnp.zeros((TM, LANES), F32)
    for col, val in ((R_E, i1 - N_GROUPS), (R_E + 1, i2 - N_GROUPS), (R_RANK, rank1),
                     (R_RANK + 1, rank2), (R_W, w1), (R_W + 1, w2)):
        rec = jnp.where(lane == col, val, rec)
    route_ref[...] = rec
    route_t_ref[0] = rec.T[0:SUBLANES, :]

    x2 = jnp.concatenate([x2_a, x2_b], axis=1)
    x2_ref[...] = x2
    h = _rms(x2, gain_ref[...])
    h_hi, h_lo = _split_bf16(h)
    h2_ref[...] = h_hi
    hh_hl = _dot(h_hi, wrc_ref[...])
    lg_ref[...] = (hh_hl[:, :ROUTER_COLS] + hh_hl[:, ROUTER_COLS:]
                   + _dot(h_lo, wrc_ref[:, :ROUTER_COLS]) + br_ref[...])


def _out_router(mixed, x, w_out, layer, gain, wr_cat, br):
    T = x.shape[0]
    n = T // TM
    row = lambda w: pl.BlockSpec((TM, w), lambda i: (jnp.minimum(i, n - 1), 0))
    lag = lambda i: jnp.maximum(i - 1, 0)
    full = lambda shape: pl.BlockSpec(shape, lambda i: (0, 0))
    wo_spec = pl.BlockSpec((1, D_MODEL, D_MODEL), lambda i: (layer, 0, 0),
                           pipeline_mode=pl.Buffered(1))
    return pl.pallas_call(
        _out_router_kernel,
        grid=(n + 1,),
        in_specs=[pl.BlockSpec(memory_space=pl.ANY), pl.BlockSpec(memory_space=pl.ANY), wo_spec,
                  full((1, D_MODEL)), full((D_MODEL, 2 * ROUTER_COLS)), full((1, ROUTER_COLS))],
        out_specs=[row(D_MODEL), row(D_MODEL),
                   pl.BlockSpec((TM, LANES), lambda i: (lag(i), 0)),
                   pl.BlockSpec((1, SUBLANES, TM), lambda i: (lag(i), 0, 0)), full((8, LANES))],
        out_shape=[jax.ShapeDtypeStruct((T, D_MODEL), F32),
                   jax.ShapeDtypeStruct((T, D_MODEL), BF16),
                   jax.ShapeDtypeStruct((T, LANES), F32),
                   jax.ShapeDtypeStruct((T // TM, SUBLANES, TM), F32),
                   jax.ShapeDtypeStruct((8, LANES), F32)],
        scratch_shapes=[pltpu.VMEM((TM, TM), BF16), pltpu.VMEM((D_MODEL, D_MODEL), BF16),
                        pltpu.VMEM((TM, ROUTER_COLS), F32),
                        pltpu.VMEM((IN_SLOTS, TM, D_MODEL), BF16),
                        pltpu.VMEM((IN_SLOTS, TM, D_MODEL), F32),
                        pltpu.SemaphoreType.DMA((IN_SLOTS,))],
        compiler_params=pltpu.CompilerParams(
            dimension_semantics=("arbitrary",), vmem_limit_bytes=VMEM_LIMIT),
        name="out_router",
    )(mixed, x, w_out, gain, wr_cat, br)


def _dispatch_kernel(fill_ref, nu_ref, dest_ref, h_ref, xb_ref, zero_ref, sem_ref, zsem_ref,
                     stage_ref):
    i = pl.program_id(0)
    par = lax.rem(i, 2)
    last = i == pl.num_programs(0) - 1
    n_blocks = xb_ref.shape[0] // MOE_BLK
    spare_fills = [(j >= nu_ref[0], pltpu.make_async_copy(
        zero_ref, xb_ref.at[pl.ds(j * MOE_BLK, MOE_BLK)], zsem_ref.at[1]))
        for j in range(n_blocks - N_EXPERTS, n_blocks)]

    @pl.when(i == 0)
    def _():
        zero_ref[...] = jnp.zeros_like(zero_ref)
        fills = [(fill_ref[e] >= 0, pltpu.make_async_copy(
            zero_ref, xb_ref.at[pl.ds(pl.multiple_of(jnp.maximum(fill_ref[e], 0), MOE_BLK), MOE_BLK)],
            zsem_ref.at[0])) for e in range(N_EXPERTS)]
        for cond, f in fills + spare_fills:
            pl.when(cond)(f.start)
        for cond, f in fills:
            pl.when(cond)(f.wait)

    _row_tiles_store(stage_ref.at[par], h_ref[...].astype(F32))
    for r in range(TM):
        for k in range(TOP_K):
            pltpu.make_async_copy(stage_ref.at[par, r], xb_ref.at[dest_ref[0, 0, k * TM + r]],
                                  sem_ref.at[par]).start(priority=k)

    def wait_tile(p):
        for _ in range(TOP_K):
            pltpu.make_async_copy(stage_ref.at[p], xb_ref.at[pl.ds(0, TM)], sem_ref.at[p]).wait()

    pl.when(i > 0)(lambda: wait_tile(1 - par))
    @pl.when(last)
    def _():
        wait_tile(par)
        for cond, f in spare_fills:
            pl.when(cond)(f.wait)


def _dispatch(fill_start, n_used, dest_tiles, h2, n_rows):
    T = h2.shape[0]
    grid_spec = pltpu.PrefetchScalarGridSpec(
        num_scalar_prefetch=2,
        grid=(T // TM,),
        in_specs=[
            pl.BlockSpec((1, 1, TOP_K * TM), lambda i, fs, nu: (i, 0, 0), memory_space=pltpu.SMEM),
            pl.BlockSpec((TM, D_MODEL), lambda i, fs, nu: (i, 0)),
        ],
        out_specs=pl.BlockSpec(memory_space=pl.ANY),
        scratch_shapes=[pltpu.VMEM((MOE_BLK, ROW_TILES, LANES), F32),
                        pltpu.SemaphoreType.DMA((2,)), pltpu.SemaphoreType.DMA((2,)),
                        pltpu.VMEM((2, TM, ROW_TILES, LANES), F32)],
    )
    return pl.pallas_call(
        _dispatch_kernel,
        grid_spec=grid_spec,
        out_shape=jax.ShapeDtypeStruct((n_rows, ROW_TILES, LANES), F32),
        compiler_params=pltpu.CompilerParams(dimension_semantics=("arbitrary",)),
        name="dispatch",
    )(fill_start, n_used, dest_tiles, h2)


BLOCK_COPY_PARTS = 4
X_SLOTS = 4
Y_SLOTS = 3


class _CopyGroup:
    def __init__(self, copies):
        self.copies = copies

    def start(self):
        for n, c in enumerate(self.copies):
            c.start(priority=n % 2)

    def wait(self):
        for c in self.copies:
            c.wait()


def _expert_kernel(be_ref, nxt_ref, nu_ref, xb_ref, wg_ref, wu_ref, wd_ref, yb_ref,
                   xbuf_ref, ybuf_ref, wgs_ref, wus_ref, wds_ref, wgb_ref, wub_ref, wdb_ref,
                   xsem_ref, ysem_ref, wsem_ref, zbuf_ref, zsem_ref, *, layer, n_blocks):
    n_used = nu_ref[0]

    part = MOE_BLK // BLOCK_COPY_PARTS

    def x_copy(j, slot):
        return _CopyGroup([pltpu.make_async_copy(
            xb_ref.at[pl.ds(j * MOE_BLK + p * part, part)],
            xbuf_ref.at[slot, pl.ds(p * part, part)], xsem_ref.at[slot])
            for p in range(BLOCK_COPY_PARTS)])

    def y_copy(j, slot):
        return _CopyGroup([pltpu.make_async_copy(
            ybuf_ref.at[slot, pl.ds(p * part, part)],
            yb_ref.at[pl.ds(j * MOE_BLK + p * part, part)], ysem_ref.at[slot])
            for p in range(BLOCK_COPY_PARTS)])

    def w_copies(e, ws):
        return [pltpu.make_async_copy(src.at[layer, e], dst.at[ws], wsem_ref.at[ws])
                for src, dst in ((wg_ref, wgs_ref), (wu_ref, wus_ref), (wd_ref, wds_ref))]

    def fill_copy(j):
        return pltpu.make_async_copy(zbuf_ref, yb_ref.at[pl.ds(j * MOE_BLK, MOE_BLK)], zsem_ref)

    for j0 in range(X_SLOTS - 1):
        x_copy(j0, j0).start()
    for c in w_copies(be_ref[0], 0):
        c.start()

    zbuf_ref[...] = jnp.zeros_like(zbuf_ref)

    def fill(j, carry):
        fill_copy(j).start()
        return carry

    lax.fori_loop(n_used, n_blocks, fill, 0)

    def block(j, ws):
        slot = lax.rem(j, Y_SLOTS)
        xslot = lax.rem(j, X_SLOTS)
        first = (j == 0) | (be_ref[j] != be_ref[jnp.maximum(j - 1, 0)])
        ws = jnp.where(first & (j > 0), 1 - ws, ws)

        @pl.when(first)
        def _():
            for c in w_copies(be_ref[j], ws):
                c.wait()
            for r0 in range(0, D_MODEL, W_PREP_ROWS):
                rows = slice(r0, r0 + W_PREP_ROWS)
                wgb_ref[rows, :] = wgs_ref[ws, rows, :].astype(BF16)
                wub_ref[rows, :] = wus_ref[ws, rows, :].astype(BF16)
            for r0 in range(0, D_EXPERT, W_PREP_ROWS):
                rows = slice(r0, r0 + W_PREP_ROWS)
                wdb_ref[rows, :] = wds_ref[ws, rows, :].astype(BF16)

            @pl.when(nxt_ref[j] >= 0)
            def _():
                for c in w_copies(nxt_ref[j], 1 - ws):
                    c.start()

        ahead = j + X_SLOTS - 1

        @pl.when(ahead < n_used)
        def _():
            x_copy(ahead, lax.rem(ahead, X_SLOTS)).start()

        x_copy(j, xslot).wait()
        x = jnp.concatenate(
            [c.astype(BF16) for c in _row_tiles_chunks(xbuf_ref.at[xslot], MOE_BLK)], axis=1)
        g = _dot(x, wgb_ref[...])
        u = _dot(x, wub_ref[...])
        h = (g * (1.0 / (1.0 + jnp.exp(-g)))) * u
        y = _dot(h.astype(BF16), wdb_ref[...])

        @pl.when(j >= Y_SLOTS)
        def _():
            y_copy(j - Y_SLOTS, slot).wait()

        _row_tiles_store(ybuf_ref.at[slot], y)
        y_copy(j, slot).start()
        return ws

    lax.fori_loop(0, n_used, block, jnp.int32(0))

    for back in range(Y_SLOTS, 0, -1):
        y_copy(n_used - back, lax.rem(n_used - back, Y_SLOTS)).wait()

    def fill_wait(j, carry):
        fill_copy(j).wait()
        return carry

    lax.fori_loop(n_used, n_blocks, fill_wait, 0)


def _experts(blk_exp, nxt_exp, n_used, xb, w_gate, w_up, w_down, layer):
    n_blocks = blk_exp.shape[0]
    any_spec = pl.BlockSpec(memory_space=pl.ANY)
    blk = (MOE_BLK, ROW_TILES, LANES)
    grid_spec = pltpu.PrefetchScalarGridSpec(
        num_scalar_prefetch=3,
        grid=(1,),
        in_specs=[any_spec, any_spec, any_spec, any_spec],
        out_specs=any_spec,
        scratch_shapes=[
            pltpu.VMEM((X_SLOTS,) + blk, F32), pltpu.VMEM((Y_SLOTS,) + blk, F32),
            pltpu.VMEM((2, D_MODEL, D_EXPERT), F32), pltpu.VMEM((2, D_MODEL, D_EXPERT), F32),
            pltpu.VMEM((2, D_EXPERT, D_MODEL), F32),
            pltpu.VMEM((D_MODEL, D_EXPERT), BF16), pltpu.VMEM((D_MODEL, D_EXPERT), BF16),
            pltpu.VMEM((D_EXPERT, D_MODEL), BF16),
            pltpu.SemaphoreType.DMA((X_SLOTS,)), pltpu.SemaphoreType.DMA((Y_SLOTS,)),
            pltpu.SemaphoreType.DMA((2,)),
            pltpu.VMEM(blk, F32), pltpu.SemaphoreType.DMA(()),
        ],
    )
    return pl.pallas_call(
        functools.partial(_expert_kernel, layer=layer, n_blocks=n_blocks),
        grid_spec=grid_spec,
        out_shape=jax.ShapeDtypeStruct((n_blocks * MOE_BLK, ROW_TILES, LANES), F32),
        compiler_params=pltpu.CompilerParams(
            dimension_semantics=("arbitrary",), vmem_limit_bytes=VMEM_LIMIT),
        name="experts",
    )(blk_exp, nxt_exp, n_used, xb, w_gate, w_up, w_down)


def _dispatch_tables(route_t, counts_rec, T):
    counts = counts_rec[0, N_GROUPS:N_GROUPS + N_EXPERTS].astype(jnp.int32)
    n_steps = (T * TOP_K) // MOE_BLK + N_EXPERTS
    nblk = (counts + MOE_BLK - 1) // MOE_BLK
    bend = jnp.cumsum(nblk)
    pstart = (bend - nblk) * MOE_BLK
    n_used = bend[-1]
    j = jnp.minimum(jnp.arange(n_steps, dtype=jnp.int32), n_used - 1)
    blk_exp = jnp.minimum(jnp.sum(j[:, None] >= bend[None, :], axis=1), N_EXPERTS - 1)
    n_rows = n_steps * MOE_BLK
    last_blk = jnp.where(counts > 0, (bend - 1) * MOE_BLK, -1)
    ids = jnp.arange(N_EXPERTS, dtype=jnp.int32)
    later = (ids[None, :] > ids[:, None]) & (nblk[None, :] > 0)
    nxt_of = jnp.min(jnp.where(later, ids[None, :], N_EXPERTS), axis=1)
    nxt_tab = jnp.where(nxt_of < N_EXPERTS, nxt_of, -1)
    nxt_exp = jnp.sum(jnp.where(blk_exp[:, None] == ids[None, :], nxt_tab[None, :], 0), axis=1)
    e = route_t[:, R_E:R_E + TOP_K, :].astype(jnp.int32)
    rank = route_t[:, R_RANK:R_RANK + TOP_K, :].astype(jnp.int32)
    seg = jnp.sum(jnp.where(e[..., None] == jnp.arange(N_EXPERTS), pstart, 0), axis=-1)
    dest = jnp.clip(seg + rank, 0, n_steps * MOE_BLK - 1)
    dest_tiles = dest.reshape(T // TM, 1, TOP_K * TM)
    return dict(dest_tiles=dest_tiles, fill_start=last_blk.astype(jnp.int32),
                blk_exp=blk_exp.astype(jnp.int32), nxt_exp=nxt_exp.astype(jnp.int32),
                n_used=n_used.reshape(1).astype(jnp.int32), n_rows=n_rows)


def _prep_layer(l, w_gk_up, b_gk, gla_norm, gmlp_norm, w_spatial, b_spatial, w_conv,
                w_router_group, b_router_group, w_router_expert, b_router_expert):
    wgk = jnp.concatenate(
        [w_gk_up[l], jnp.zeros((LANES - GLA_GATE_RANK, GLA_KDIM), F32)], axis=0).astype(BF16)
    wsp = w_spatial[l].transpose(1, 0, 2).reshape(GMLP_CHUNK, GMLP_HEADS * GMLP_CHUNK)
    bsp = jnp.repeat(b_spatial[l].T, GMLP_DH, axis=1)
    wconv = jnp.concatenate([w_conv[l], jnp.zeros((8 - CONV_K, CONV_WIDTH), F32)], axis=0)
    wr = jnp.concatenate(
        [w_router_group[l], w_router_expert[l],
         jnp.zeros((D_MODEL, ROUTER_COLS - N_GROUPS - N_EXPERTS), F32)], axis=1)
    wr_hi = wr.astype(BF16)
    wr_lo = (wr - wr_hi.astype(F32)).astype(BF16)
    br = jnp.concatenate(
        [b_router_group[l], b_router_expert[l],
         jnp.zeros((ROUTER_COLS - N_GROUPS - N_EXPERTS,), F32)])[None, :]
    return dict(
        wgk=wgk, bgk=b_gk[l][None, :], glan=gla_norm[l][None, :], gmn=gmlp_norm[l][None, :],
        wsp=wsp, bsp=bsp, wconv=wconv, wr_cat=jnp.concatenate([wr_hi, wr_lo], axis=1), br=br)


def kernel(x, attn_norm, w_in, w_gk_up, b_gk, gla_norm, gmlp_norm, w_spatial, b_spatial, w_conv, w_out, ffn_norm, w_router_group, b_router_group, w_router_expert, b_router_expert, w_gate, w_up, w_down, final_norm):
    B, S, D = x.shape
    T = B * S
    depth = w_in.shape[0]
    xr = x.reshape(T, D)
    w_in_t = jnp.swapaxes(w_in, 1, 2)
    moe = None
    for l in range(depth):
        p = _prep_layer(l, w_gk_up, b_gk, gla_norm, gmlp_norm, w_spatial, b_spatial, w_conv,
                        w_router_group, b_router_group, w_router_expert, b_router_expert)
        mix_params = (p["wgk"], p["bgk"], p["glan"], p["gmn"], p["wsp"], p["bsp"], p["wconv"])
        if moe is None:
            mixed = _front(xr, attn_norm[l][None, :], w_in_t, l, mix_params, B, S)
        else:
            xr, mixed = _front(moe["x2"], attn_norm[l][None, :], w_in_t, l, mix_params, B, S, moe)
        x2, h2, route, route_t, counts_rec = _out_router(
            mixed, xr, w_out, l, ffn_norm[l][None, :], p["wr_cat"], p["br"])
        moe = _dispatch_tables(route_t, counts_rec, T)
        xb = _dispatch(moe["fill_start"], moe["n_used"], moe["dest_tiles"], h2, moe["n_rows"])
        yb = _experts(moe["blk_exp"], moe["nxt_exp"], moe["n_used"], xb, w_gate, w_up, w_down, l)
        moe.update(x2=x2, route=route, yb=yb)
    out = _combine_final_norm(moe["dest_tiles"], moe["x2"], moe["route"], moe["yb"],
                              final_norm[None, :])
    return out.reshape(B, S, D)
```

```python
import functools

import jax
import jax.numpy as jnp
from jax import lax
from jax.experimental import pallas as pl
from jax.experimental.pallas import tpu as pltpu

F32 = jnp.float32
BF16 = jnp.bfloat16

D_MODEL = 1024
RMS_EPS = 1e-6
GLA_HEADS = 4
GLA_WIDTH = 512
GLA_DV = 128
GLA_DK = 64
GLA_KDIM = 256
GLA_GATE_RANK = 16
GLA_GATE_NORM = 16.0
GLA_CHUNK = 64
GMLP_HEADS = 4
GMLP_WIDTH = 256
GMLP_DH = 64
GMLP_CHUNK = 128
CONV_WIDTH = 256
CONV_K = 3
N_GROUPS = 4
EXPERTS_PER_GROUP = 8
N_EXPERTS = 32
TOP_K = 2
D_EXPERT = 256

LANES = 128
C_Q, C_K, C_V, C_G = 0, 256, 512, 1024
C_U, C_VG, C_X, C_BG, C_CG, C_GKL = 1536, 1792, 2048, 2304, 2560, 2816
D_PROJ = C_GKL + LANES
D_IN = C_GKL + GLA_GATE_RANK

TM = 256
TS_MIX = TM
MOE_BLK = 256
ROUTER_COLS = LANES
SUBLANES = 8
ROW_TILES = D_MODEL // LANES
assert ROW_TILES == SUBLANES
VMEM_LIMIT = 56 * 1024 * 1024
R_E, R_RANK, R_W = 0, 2, 4


def _dot(a, b):
    return jnp.dot(a, b, preferred_element_type=F32)


def _split_bf16(x):
    hi = x.astype(BF16)
    lo = (x - hi.astype(F32)).astype(BF16)
    return hi, lo


def _rms(x, gain):
    return x * lax.rsqrt(jnp.mean(x * x, axis=-1, keepdims=True) + RMS_EPS) * gain


W_PREP_ROWS = 128
PROJ_CHUNK = 256


def _stage_w_in(wt_ref, wb_ref):
    for c0 in range(0, C_GKL, LANES):
        src = c0 if c0 < C_U else c0 + GLA_GATE_RANK
        wb_ref[:, c0:c0 + LANES] = wt_ref[0, src:src + LANES, :].T.astype(BF16)
    low = jnp.concatenate([wt_ref[0, C_U:C_U + GLA_GATE_RANK, :],
                           jnp.zeros((LANES - GLA_GATE_RANK, D_MODEL), F32)], axis=0)
    wb_ref[:, C_GKL:D_PROJ] = low.T.astype(BF16)


def _row_gather_copy(yb_ref, buf_ref, sem_ref, slot, k, r, d):
    return pltpu.make_async_copy(yb_ref.at[d], buf_ref.at[slot, k, r], sem_ref.at[slot])


def _gather_start(dest_ref, yb_ref, buf_ref, sem_ref, slot, rows=range(TM)):
    for r in rows:
        for k in range(TOP_K):
            _row_gather_copy(yb_ref, buf_ref, sem_ref, slot, k, r,
                             dest_ref[0, 0, k * TM + r]).start(priority=k)


def _gather_wait(yb_ref, buf_ref, sem_ref, slot):
    for k in range(TOP_K):
        pltpu.make_async_copy(yb_ref.at[pl.ds(0, TM)], buf_ref.at[slot, k], sem_ref.at[slot]).wait()


def _combined_residual(dcur_ref, x_ref, route_ref, yb_ref, buf_ref, sem_ref):
    i = pl.program_id(0)
    slot = lax.rem(i, 2)

    @pl.when(i == 0)
    def _():
        _gather_start(dcur_ref, yb_ref, buf_ref, sem_ref, 0)

    _gather_wait(yb_ref, buf_ref, sem_ref, slot)
    w0 = route_ref[:, R_W:R_W + 1]
    w1 = route_ref[:, R_W + 1:R_W + 2]
    y0 = _row_tiles_chunks(buf_ref.at[slot, 0], TM)
    y1 = _row_tiles_chunks(buf_ref.at[slot, 1], TM)
    return jnp.concatenate(
        [x_ref[:, c * LANES:(c + 1) * LANES] + (w0 * y0[c] + w1 * y1[c]) for c in range(ROW_TILES)],
        axis=1)


def _prefetch_groups(n_groups):
    per = -(-TM // n_groups)
    return [range(g * per, min(TM, (g + 1) * per)) for g in range(n_groups)]


def _drain_last_prefetch(yb_ref, buf_ref, sem_ref):
    i = pl.program_id(0)

    @pl.when(i == pl.num_programs(0) - 1)
    def _():
        _gather_wait(yb_ref, buf_ref, sem_ref, 1 - lax.rem(i, 2))


def _combine_specs(n_tiles):
    smem_tile = lambda f: pl.BlockSpec((1, 1, TOP_K * TM), f, memory_space=pltpu.SMEM)
    return [
        smem_tile(lambda i: (i, 0, 0)),
        smem_tile(lambda i: (jnp.minimum(i + 1, n_tiles - 1), 0, 0)),
        pl.BlockSpec((TM, D_MODEL), lambda i: (i, 0)),
        pl.BlockSpec((TM, LANES), lambda i: (i, 0)),
        pl.BlockSpec(memory_space=pl.ANY),
    ]


_COMBINE_SCRATCH = [pltpu.VMEM((2, TOP_K, TM, ROW_TILES, LANES), F32),
                    pltpu.SemaphoreType.DMA((2,))]


def _combine_final_norm_kernel(dcur_ref, dnxt_ref, x_ref, route_ref, yb_ref, gain_ref,
                               o_ref, buf_ref, sem_ref):
    _gather_start(dnxt_ref, yb_ref, buf_ref, sem_ref, 1 - lax.rem(pl.program_id(0), 2))
    x = _combined_residual(dcur_ref, x_ref, route_ref, yb_ref, buf_ref, sem_ref)
    o_ref[...] = _rms(x, gain_ref[...])
    _drain_last_prefetch(yb_ref, buf_ref, sem_ref)


def _combine_final_norm(dest_tiles, x2, route, yb, gain):
    T = x2.shape[0]
    n_tiles = T // TM
    return pl.pallas_call(
        _combine_final_norm_kernel,
        grid=(n_tiles,),
        in_specs=_combine_specs(n_tiles) + [pl.BlockSpec((1, D_MODEL), lambda i: (0, 0))],
        out_specs=pl.BlockSpec((TM, D_MODEL), lambda i: (i, 0)),
        out_shape=jax.ShapeDtypeStruct((T, D_MODEL), F32),
        scratch_shapes=_COMBINE_SCRATCH,
        compiler_params=pltpu.CompilerParams(
            dimension_semantics=("arbitrary",), vmem_limit_bytes=VMEM_LIMIT),
        name="combine_final_norm",
    )(dest_tiles, dest_tiles, x2, route, yb, gain)


def _gelu_tanh(x):
    c = 0.7978845608028654
    return x * (0.5 * (1.0 + jnp.tanh(c * (x + 0.044715 * (x * x * x)))))


def _mixer_kernel(proj_ref, wgk_ref, bgk_ref, glan_ref, gmn_ref, wsp_ref, bsp_ref, wconv_ref,
                  out_ref, st_ref, hc_ref, lcat_ref, wm_ref, *, seq_start, first_step, between):
    TS = TS_MIX
    n_gla = TS // GLA_CHUNK
    n_gm = TS // GMLP_CHUNK

    @pl.when(seq_start)
    def _():
        st_ref[...] = jnp.zeros_like(st_ref)
        hc_ref[...] = jnp.zeros_like(hc_ref)

    @pl.when(first_step)
    def _():
        r = lax.broadcasted_iota(jnp.int32, (TS, TS), 0)
        c = lax.broadcasted_iota(jnp.int32, (TS, TS), 1)
        keep = ((r // GLA_CHUNK) == (c // GLA_CHUNK)) & (c <= r)
        lcat_ref[...] = jnp.where(keep, 1.0, 0.0).astype(BF16)
        t = lax.broadcasted_iota(jnp.int32, (GMLP_CHUNK, GMLP_HEADS * GMLP_CHUNK), 0)
        s = lax.broadcasted_iota(jnp.int32, (GMLP_CHUNK, GMLP_HEADS * GMLP_CHUNK), 1) % GMLP_CHUNK
        wm_ref[...] = jnp.where(s <= t, wsp_ref[...], 0.0).astype(BF16)

    lane256 = lax.broadcasted_iota(jnp.int32, (1, GLA_KDIM), 1)

    q = proj_ref[:, C_Q:C_Q + GLA_KDIM].astype(F32)
    k = proj_ref[:, C_K:C_K + GLA_KDIM].astype(F32)
    v_b = proj_ref[:, C_V:C_V + GLA_WIDTH]
    z = _dot(proj_ref[:, C_GKL:C_GKL + LANES], wgk_ref[...]) + bgk_ref[...]
    gk = (jnp.minimum(z, 0.0) - jnp.log1p(jnp.exp(-jnp.abs(z)))) * (1.0 / GLA_GATE_NORM)
    gk_hi, gk_lo = _split_bf16(gk)
    cs = _dot(lcat_ref[...], jnp.concatenate([gk_hi, gk_lo], axis=1))
    b = cs[:, :GLA_KDIM] + cs[:, GLA_KDIM:]
    b_last = [b[(c + 1) * GLA_CHUNK - 1:(c + 1) * GLA_CHUNK, :] for c in range(n_gla)]
    bl = jnp.concatenate(
        [jnp.broadcast_to(t, (GLA_CHUNK, GLA_KDIM)) for t in b_last], axis=0)
    q_dec = (q * (GLA_DK ** -0.5)) * jnp.exp(b)
    k_inv = (k * jnp.exp(-b)).astype(BF16)
    k_dec = (k * jnp.exp(bl - b)).astype(BF16)
    q_dec_b = q_dec.astype(BF16)

    zero_b = jnp.zeros_like(q_dec_b)
    q_stack = jnp.concatenate(
        [jnp.where((lane256 // GLA_DK) == h, q_dec_b, zero_b) for h in range(GLA_HEADS)], axis=0)
    scores = lax.dot_general(q_stack, k_inv, (((1,), (1,)), ((), ())),
                             preferred_element_type=F32)
    rt = lax.broadcasted_iota(jnp.int32, (TS, TS), 0)
    ct = lax.broadcasted_iota(jnp.int32, (TS, TS), 1)
    causal = ((rt // GLA_CHUNK) == (ct // GLA_CHUNK)) & (ct <= rt)
    o_heads = []
    for h in range(GLA_HEADS):
        p_h = jnp.where(causal, scores[h * TS:(h + 1) * TS, :], 0.0).astype(BF16)
        o_heads.append(_dot(p_h, v_b[:, h * GLA_DV:(h + 1) * GLA_DV]))

    sr = lax.broadcasted_iota(jnp.int32, (GLA_WIDTH, GLA_KDIM), 0) // GLA_DV
    sc = lax.broadcasted_iota(jnp.int32, (GLA_WIDTH, GLA_KDIM), 1) // GLA_DK
    bd_mask = sr == sc
    o_inter = []
    for c in range(n_gla):
        rows = slice(c * GLA_CHUNK, (c + 1) * GLA_CHUNK)
        st = st_ref[...]
        o_inter.append(lax.dot_general(q_dec_b[rows], st.astype(BF16), (((1,), (1,)), ((), ())),
                                       preferred_element_type=F32))
        upd = lax.dot_general(v_b[rows], k_dec[rows], (((0,), (0,)), ((), ())),
                              preferred_element_type=F32)
        decay = jnp.exp(b_last[c])
        st_ref[...] = st * decay + jnp.where(bd_mask, upd, 0.0)
    o_inter = jnp.concatenate(o_inter, axis=0)

    for h in range(GLA_HEADS):
        cols = slice(h * GLA_DV, (h + 1) * GLA_DV)
        o = o_heads[h] + o_inter[:, cols]
        o = o * lax.rsqrt(jnp.mean(o * o, axis=-1, keepdims=True) + RMS_EPS) * glan_ref[...]
        g = proj_ref[:, C_G + h * GLA_DV:C_G + (h + 1) * GLA_DV].astype(F32)
        out_ref[:, cols] = (o * (g * (1.0 / (1.0 + jnp.exp(-g))))).astype(out_ref.dtype)

    between("gla_done")
    u = _gelu_tanh(proj_ref[:, C_U:C_U + GMLP_WIDTH].astype(F32))
    vg = _gelu_tanh(proj_ref[:, C_VG:C_VG + GMLP_WIDTH].astype(F32))
    hr = lax.broadcasted_iota(jnp.int32, (GMLP_WIDTH, GMLP_WIDTH), 0) // GMLP_DH
    hcn = lax.broadcasted_iota(jnp.int32, (GMLP_WIDTH, GMLP_WIDTH), 1) // GMLP_DH
    head_mean = jnp.where(hr == hcn, 1.0 / GMLP_DH, 0.0).astype(BF16)
    sq_hi, sq_lo = _split_bf16(vg * vg)
    ms = _dot(sq_hi, head_mean) + _dot(sq_lo, head_mean)
    v32 = vg * lax.rsqrt(ms + RMS_EPS) * gmn_ref[...]
    for c in range(n_gm):
        rows = slice(c * GMLP_CHUNK, (c + 1) * GMLP_CHUNK)
        vc = v32[rows].astype(BF16)
        zc = jnp.zeros_like(vc)
        rhs = jnp.concatenate(
            [jnp.where((lane256 // GMLP_DH) == h, vc, zc) for h in range(GMLP_HEADS)], axis=0)
        mixed = _dot(wm_ref[...], rhs) + bsp_ref[...]
        out_ref[rows, GLA_WIDTH:GLA_WIDTH + GMLP_WIDTH] = (u[rows] * mixed).astype(out_ref.dtype)

    between("gmlp_done")
    hcv = (proj_ref[:, C_CG:C_CG + CONV_WIDTH].astype(F32)
           * proj_ref[:, C_X:C_X + CONV_WIDTH].astype(F32))
    hc_ref[8:8 + TS, :] = hcv
    y = (wconv_ref[2:3, :] * hcv + wconv_ref[1:2, :] * hc_ref[7:7 + TS, :]
         + wconv_ref[0:1, :] * hc_ref[6:6 + TS, :])
    out_ref[:, GLA_WIDTH + GMLP_WIDTH:] = (
        proj_ref[:, C_BG:C_BG + CONV_WIDTH].astype(F32) * y).astype(out_ref.dtype)
    hc_ref[0:8, :] = hc_ref[TS:TS + 8, :]


_MIXER_SCRATCH = [
    pltpu.VMEM((GLA_WIDTH, GLA_KDIM), F32),
    pltpu.VMEM((TS_MIX + 8, CONV_WIDTH), F32),
    pltpu.VMEM((TS_MIX, TS_MIX), BF16),
    pltpu.VMEM((GMLP_CHUNK, GMLP_HEADS * GMLP_CHUNK), BF16),
]


N_MIX_PARAMS = 7
PROJ_CHUNKS_AT = {"gla_done": 4, "gmlp_done": 4}


def _front_kernel(*refs, tiles_per_seq, combine):
    refs = list(refs)
    if combine:
        dcur_ref, dnxt_ref, x_ref, route_ref, yb_ref = refs[:5]
        del refs[:5]
    else:
        x_ref = refs.pop(0)
    gain_ref, wt_ref = refs[:2]
    mix_refs = refs[2:2 + N_MIX_PARAMS]
    del refs[:2 + N_MIX_PARAMS]
    if combine:
        xo_ref, out_ref, buf_ref, sem_ref = refs[:4]
        del refs[:4]
    else:
        out_ref = refs.pop(0)
    st_ref, hc_ref, lcat_ref, wm_ref, wb_ref, pcur_ref, pnext_ref = refs
    s = pl.program_id(0)

    @pl.when(s == 0)
    def _():
        _stage_w_in(wt_ref, wb_ref)
        pcur_ref[...] = jnp.zeros_like(pcur_ref)

    col_chunks = [(c0, min(c0 + PROJ_CHUNK, D_PROJ)) for c0 in range(0, D_PROJ, PROJ_CHUNK)]
    work = list(zip(_prefetch_groups(len(col_chunks)), col_chunks))

    def project(h, n):
        for _ in range(min(n, len(work))):
            rows, (c0, c1) = work.pop(0)
            if combine:
                _gather_start(dnxt_ref, yb_ref, buf_ref, sem_ref, 1 - lax.rem(s, 2), rows)
            pnext_ref[:, c0:c1] = _dot(h, wb_ref[:, c0:c1]).astype(BF16)

    mixers = functools.partial(
        _mixer_kernel, pcur_ref, *mix_refs, out_ref, st_ref, hc_ref, lcat_ref, wm_ref,
        seq_start=lax.rem(jnp.maximum(s - 1, 0), tiles_per_seq) == 0, first_step=s == 0)
    if combine:
        x = _combined_residual(dcur_ref, x_ref, route_ref, yb_ref, buf_ref, sem_ref)
        xo_ref[...] = x
        h = _rms(x, gain_ref[...]).astype(BF16)
        mixers(between=lambda site: project(h, PROJ_CHUNKS_AT[site]))
        project(h, len(work))
        _drain_last_prefetch(yb_ref, buf_ref, sem_ref)
    else:
        mixers(between=lambda site: None)
        project(_rms(x_ref[...], gain_ref[...]).astype(BF16), len(work))
    pcur_ref[...] = pnext_ref[...]


def _front(x, gain, w_in_t, layer, mix_params, batch, seq, moe=None):
    n_seq = seq // TS_MIX
    n = batch * n_seq
    T = batch * seq
    cur = lambda s: jnp.minimum(s, n - 1)
    full = lambda shape: pl.BlockSpec(shape, lambda s: (0,) * len(shape))
    row = lambda w, f: pl.BlockSpec((TM, w), lambda s: (f(s), 0))
    in_specs, args = [row(D_MODEL, cur)], [x]
    out_specs = [row(D_MODEL, lambda s: jnp.maximum(s - 1, 0))]
    out_shape = [jax.ShapeDtypeStruct((T, D_MODEL), BF16)]
    scratch = list(_MIXER_SCRATCH)
    if moe is not None:
        smem_tile = lambda f: pl.BlockSpec((1, 1, TOP_K * TM), lambda s: (f(s), 0, 0),
                                           memory_space=pltpu.SMEM)
        in_specs = [smem_tile(cur), smem_tile(lambda s: jnp.minimum(s + 1, n - 1))] + in_specs + [
            row(LANES, cur), pl.BlockSpec(memory_space=pl.ANY)]
        args = [moe["dest_tiles"], moe["dest_tiles"]] + args + [moe["route"], moe["yb"]]
        out_specs = [row(D_MODEL, cur)] + out_specs
        out_shape = [jax.ShapeDtypeStruct((T, D_MODEL), F32)] + out_shape
        scratch = _COMBINE_SCRATCH + scratch
    in_specs += [
        full((1, D_MODEL)),
        pl.BlockSpec((1, D_IN, D_MODEL), lambda s: (layer, 0, 0), pipeline_mode=pl.Buffered(1)),
        full((LANES, GLA_KDIM)), full((1, GLA_KDIM)), full((1, GLA_DV)), full((1, GMLP_WIDTH)),
        full((GMLP_CHUNK, GMLP_HEADS * GMLP_CHUNK)), full((GMLP_CHUNK, GMLP_WIDTH)),
        full((8, CONV_WIDTH)),
    ]
    scratch += [pltpu.VMEM((D_MODEL, D_PROJ), BF16), pltpu.VMEM((TS_MIX, D_PROJ), BF16),
                pltpu.VMEM((TS_MIX, D_PROJ), BF16)]
    return pl.pallas_call(
        functools.partial(_front_kernel, tiles_per_seq=n_seq, combine=moe is not None),
        grid=(n + 1,),
        in_specs=in_specs,
        out_specs=out_specs if moe is not None else out_specs[0],
        out_shape=out_shape if moe is not None else out_shape[0],
        scratch_shapes=scratch,
        compiler_params=pltpu.CompilerParams(
            dimension_semantics=("arbitrary",), vmem_limit_bytes=VMEM_LIMIT),
        name="front",
    )(*args, gain, w_in_t, *mix_params)


def _row_tiles_store(tiles_ref, x):
    rows = x.shape[0]
    flat = tiles_ref.reshape(rows * ROW_TILES, LANES)
    for c in range(ROW_TILES):
        flat[pl.ds(c, rows, stride=ROW_TILES), :] = x[:, c * LANES:(c + 1) * LANES]


def _row_tiles_chunks(tiles_ref, rows):
    flat = tiles_ref.reshape(rows * ROW_TILES, LANES)
    return [flat[pl.ds(c, rows, stride=ROW_TILES), :] for c in range(ROW_TILES)]


IN_SLOTS = 3


def _out_router_kernel(mix_hbm, x_hbm, wo_ref, gain_ref, wrc_ref, br_ref,
                       x2_hbm, h2_hbm, route_ref, route_t_ref, cnt_ref, tri_ref, wob_ref, lg_ref,
                       mixbuf_ref, xbuf_ref, insem_ref, x2buf_ref, h2buf_ref, outsem_ref):
    i = pl.program_id(0)
    n = pl.num_programs(0) - 1

    def in_copies(t):
        slot = lax.rem(t, IN_SLOTS)
        rows = pl.ds(t * TM, TM)
        return [pltpu.make_async_copy(mix_hbm.at[rows], mixbuf_ref.at[slot], insem_ref.at[slot]),
                pltpu.make_async_copy(x_hbm.at[rows], xbuf_ref.at[slot], insem_ref.at[slot])]

    @pl.when(i == 0)
    def _():
        for t in range(IN_SLOTS - 1):
            for c in in_copies(t):
                c.start()

    @pl.when(i + IN_SLOTS - 1 < n)
    def _():
        for c in in_copies(i + IN_SLOTS - 1):
            c.start()

    @pl.when(i < n)
    def _():
        for c in in_copies(i):
            c.wait()

    tile_slot = lax.rem(jnp.minimum(i, n - 1), IN_SLOTS)
    mix_ref = mixbuf_ref.at[tile_slot]
    x_ref = xbuf_ref.at[tile_slot]

    @pl.when(i == 0)
    def _():
        cnt_ref[...] = jnp.zeros_like(cnt_ref)
        lg_ref[...] = jnp.zeros_like(lg_ref)
        r = lax.broadcasted_iota(jnp.int32, (TM, TM), 0)
        c = lax.broadcasted_iota(jnp.int32, (TM, TM), 1)
        tri_ref[...] = jnp.where(c < r, 1.0, 0.0).astype(BF16)
        for r0 in range(0, D_MODEL, W_PREP_ROWS):
            wob_ref[r0:r0 + W_PREP_ROWS, :] = wo_ref[0, r0:r0 + W_PREP_ROWS, :].astype(BF16)

    lg = lg_ref[...]
    half = D_MODEL // 2
    mix = mix_ref[...]
    x2_a = x_ref[:, :half] + _dot(mix, wob_ref[:, :half])

    lane = lax.broadcasted_iota(jnp.int32, (TM, LANES), 1).astype(F32)
    neg = -jnp.inf
    is_g = lane < N_GROUPS
    gl = jnp.where(is_g, lg, neg)
    gmax = jnp.max(gl, axis=1, keepdims=True)
    g_top = jnp.min(jnp.where(gl == gmax, lane, float(LANES)), axis=1, keepdims=True)
    g_w = 1.0 / jnp.sum(jnp.where(is_g, jnp.exp(lg - gmax), 0.0), axis=1, keepdims=True)
    first = N_GROUPS + EXPERTS_PER_GROUP * g_top
    el = jnp.where((lane >= first) & (lane < first + EXPERTS_PER_GROUP), lg, neg)
    m1 = jnp.max(el, axis=1, keepdims=True)
    i1 = jnp.min(jnp.where(el == m1, lane, float(LANES)), axis=1, keepdims=True)
    el2 = jnp.where(lane == i1, neg, el)
    m2 = jnp.max(el2, axis=1, keepdims=True)
    i2 = jnp.min(jnp.where(el2 == m2, lane, float(LANES)), axis=1, keepdims=True)
    ratio = jnp.exp(m2 - m1)
    w1 = g_w / (1.0 + ratio)
    w2 = w1 * ratio

    x2_b = x_ref[:, half:] + _dot(mix, wob_ref[:, half:])

    oh1 = jnp.where(lane == i1, 1.0, 0.0)
    oh2 = jnp.where(lane == i2, 1.0, 0.0)
    oh = jnp.where(i > 0, oh1 + oh2, 0.0)
    before = _dot(tri_ref[...], oh.astype(BF16)) + cnt_ref[0:1, :]
    rank1 = jnp.sum(oh1 * before, axis=1, keepdims=True)
    rank2 = jnp.sum(oh2 * before, axis=1, keepdims=True)
    cnt_ref[...] = cnt_ref[...] + jnp.sum(oh, axis=0, keepdims=True)

    rec = jnp.zeros((TM, LANES), F32)
    for col, val in ((R_E, i1 - N_GROUPS), (R_E + 1, i2 - N_GROUPS), (R_RANK, rank1),
                     (R_RANK + 1, rank2), (R_W, w1), (R_W + 1, w2)):
        rec = jnp.where(lane == col, val, rec)
    route_ref[...] = rec
    route_t_ref[0] = rec.T[0:SUBLANES, :]

    x2 = jnp.concatenate([x2_a, x2_b], axis=1)
    h = _rms(x2, gain_ref[...])
    h_hi, h_lo = _split_bf16(h)

    def out_copies(step):
        slot = lax.rem(step, IN_SLOTS)
        rows = pl.ds(step * TM, TM)
        return [pltpu.make_async_copy(x2buf_ref.at[slot], x2_hbm.at[rows], outsem_ref.at[slot]),
                pltpu.make_async_copy(h2buf_ref.at[slot], h2_hbm.at[rows], outsem_ref.at[slot])]

    @pl.when(i >= IN_SLOTS)
    def _():
        for c in out_copies(i - IN_SLOTS):
            c.wait()

    out_slot = lax.rem(i, IN_SLOTS)
    x2buf_ref[out_slot] = x2
    h2buf_ref[out_slot] = h_hi

    @pl.when(i < n)
    def _():
        for c in out_copies(i):
            c.start()

    @pl.when(i == n)
    def _():
        for back in range(IN_SLOTS - 1, 0, -1):
            for c in out_copies(n - back):
                c.wait()
    hh_hl = _dot(h_hi, wrc_ref[...])
    lg_ref[...] = (hh_hl[:, :ROUTER_COLS] + hh_hl[:, ROUTER_COLS:]
                   + _dot(h_lo, wrc_ref[:, :ROUTER_COLS]) + br_ref[...])


def _out_router(mixed, x, w_out, layer, gain, wr_cat, br):
    T = x.shape[0]
    n = T // TM
    row = lambda w: pl.BlockSpec((TM, w), lambda i: (jnp.minimum(i, n - 1), 0))
    lag = lambda i: jnp.maximum(i - 1, 0)
    full = lambda shape: pl.BlockSpec(shape, lambda i: (0, 0))
    wo_spec = pl.BlockSpec((1, D_MODEL, D_MODEL), lambda i: (layer, 0, 0),
                           pipeline_mode=pl.Buffered(1))
    return pl.pallas_call(
        _out_router_kernel,
        grid=(n + 1,),
        in_specs=[pl.BlockSpec(memory_space=pl.ANY), pl.BlockSpec(memory_space=pl.ANY), wo_spec,
                  full((1, D_MODEL)), full((D_MODEL, 2 * ROUTER_COLS)), full((1, ROUTER_COLS))],
        out_specs=[pl.BlockSpec(memory_space=pl.ANY), pl.BlockSpec(memory_space=pl.ANY),
                   pl.BlockSpec((TM, LANES), lambda i: (lag(i), 0)),
                   pl.BlockSpec((1, SUBLANES, TM), lambda i: (lag(i), 0, 0)), full((8, LANES))],
        out_shape=[jax.ShapeDtypeStruct((T, D_MODEL), F32),
                   jax.ShapeDtypeStruct((T, D_MODEL), BF16),
                   jax.ShapeDtypeStruct((T, LANES), F32),
                   jax.ShapeDtypeStruct((T // TM, SUBLANES, TM), F32),
                   jax.ShapeDtypeStruct((8, LANES), F32)],
        scratch_shapes=[pltpu.VMEM((TM, TM), BF16), pltpu.VMEM((D_MODEL, D_MODEL), BF16),
                        pltpu.VMEM((TM, ROUTER_COLS), F32),
                        pltpu.VMEM((IN_SLOTS, TM, D_MODEL), BF16),
                        pltpu.VMEM((IN_SLOTS, TM, D_MODEL), F32),
                        pltpu.SemaphoreType.DMA((IN_SLOTS,)),
                        pltpu.VMEM((IN_SLOTS, TM, D_MODEL), F32),
                        pltpu.VMEM((IN_SLOTS, TM, D_MODEL), BF16),
                        pltpu.SemaphoreType.DMA((IN_SLOTS,))],
        compiler_params=pltpu.CompilerParams(
            dimension_semantics=("arbitrary",), vmem_limit_bytes=VMEM_LIMIT),
        name="out_router",
    )(mixed, x, w_out, gain, wr_cat, br)


def _dispatch_kernel(fill_ref, nu_ref, dest_ref, h_ref, xb_ref, zero_ref, sem_ref, zsem_ref,
                     stage_ref):
    i = pl.program_id(0)
    par = lax.rem(i, 2)
    last = i == pl.num_programs(0) - 1
    n_blocks = xb_ref.shape[0] // MOE_BLK
    spare_fills = [(j >= nu_ref[0], pltpu.make_async_copy(
        zero_ref, xb_ref.at[pl.ds(j * MOE_BLK, MOE_BLK)], zsem_ref.at[1]))
        for j in range(n_blocks - N_EXPERTS, n_blocks)]

    @pl.when(i == 0)
    def _():
        zero_ref[...] = jnp.zeros_like(zero_ref)
        fills = [(fill_ref[e] >= 0, pltpu.make_async_copy(
            zero_ref, xb_ref.at[pl.ds(pl.multiple_of(jnp.maximum(fill_ref[e], 0), MOE_BLK), MOE_BLK)],
            zsem_ref.at[0])) for e in range(N_EXPERTS)]
        for cond, f in fills + spare_fills:
            pl.when(cond)(f.start)
        for cond, f in fills:
            pl.when(cond)(f.wait)

    _row_tiles_store(stage_ref.at[par], h_ref[...].astype(F32))
    for r in range(TM):
        for k in range(TOP_K):
            pltpu.make_async_copy(stage_ref.at[par, r], xb_ref.at[dest_ref[0, 0, k * TM + r]],
                                  sem_ref.at[par]).start(priority=k)

    def wait_tile(p):
        for _ in range(TOP_K):
            pltpu.make_async_copy(stage_ref.at[p], xb_ref.at[pl.ds(0, TM)], sem_ref.at[p]).wait()

    pl.when(i > 0)(lambda: wait_tile(1 - par))
    @pl.when(last)
    def _():
        wait_tile(par)
        for cond, f in spare_fills:
            pl.when(cond)(f.wait)


def _dispatch(fill_start, n_used, dest_tiles, h2, n_rows):
    T = h2.shape[0]
    grid_spec = pltpu.PrefetchScalarGridSpec(
        num_scalar_prefetch=2,
        grid=(T // TM,),
        in_specs=[
            pl.BlockSpec((1, 1, TOP_K * TM), lambda i, fs, nu: (i, 0, 0), memory_space=pltpu.SMEM),
            pl.BlockSpec((TM, D_MODEL), lambda i, fs, nu: (i, 0)),
        ],
        out_specs=pl.BlockSpec(memory_space=pl.ANY),
        scratch_shapes=[pltpu.VMEM((MOE_BLK, ROW_TILES, LANES), F32),
                        pltpu.SemaphoreType.DMA((2,)), pltpu.SemaphoreType.DMA((2,)),
                        pltpu.VMEM((2, TM, ROW_TILES, LANES), F32)],
    )
    return pl.pallas_call(
        _dispatch_kernel,
        grid_spec=grid_spec,
        out_shape=jax.ShapeDtypeStruct((n_rows, ROW_TILES, LANES), F32),
        compiler_params=pltpu.CompilerParams(dimension_semantics=("arbitrary",)),
        name="dispatch",
    )(fill_start, n_used, dest_tiles, h2)


BLOCK_COPY_PARTS = 4
X_SLOTS = 4
Y_SLOTS = 3


class _CopyGroup:
    def __init__(self, copies):
        self.copies = copies

    def start(self):
        for n, c in enumerate(self.copies):
            c.start(priority=n % 2)

    def wait(self):
        for c in self.copies:
            c.wait()


def _expert_kernel(be_ref, nxt_ref, nu_ref, xb_ref, wg_ref, wu_ref, wd_ref, yb_ref,
                   xbuf_ref, ybuf_ref, wgs_ref, wus_ref, wds_ref, wgb_ref, wub_ref, wdb_ref,
                   xsem_ref, ysem_ref, wsem_ref, zbuf_ref, zsem_ref, *, layer, n_blocks):
    n_used = nu_ref[0]

    part = MOE_BLK // BLOCK_COPY_PARTS

    def x_copy(j, slot):
        return _CopyGroup([pltpu.make_async_copy(
            xb_ref.at[pl.ds(j * MOE_BLK + p * part, part)],
            xbuf_ref.at[slot, pl.ds(p * part, part)], xsem_ref.at[slot])
            for p in range(BLOCK_COPY_PARTS)])

    def y_copy(j, slot):
        return _CopyGroup([pltpu.make_async_copy(
            ybuf_ref.at[slot, pl.ds(p * part, part)],
            yb_ref.at[pl.ds(j * MOE_BLK + p * part, part)], ysem_ref.at[slot])
            for p in range(BLOCK_COPY_PARTS)])

    def w_copies(e, ws):
        return [pltpu.make_async_copy(src.at[layer, e], dst.at[ws], wsem_ref.at[ws])
                for src, dst in ((wg_ref, wgs_ref), (wu_ref, wus_ref), (wd_ref, wds_ref))]

    def fill_copy(j):
        return pltpu.make_async_copy(zbuf_ref, yb_ref.at[pl.ds(j * MOE_BLK, MOE_BLK)], zsem_ref)

    for j0 in range(X_SLOTS - 1):
        x_copy(j0, j0).start()
    for c in w_copies(be_ref[0], 0):
        c.start()

    zbuf_ref[...] = jnp.zeros_like(zbuf_ref)

    def fill(j, carry):
        fill_copy(j).start()
        return carry

    lax.fori_loop(n_used, n_blocks, fill, 0)

    def block(j, ws):
        slot = lax.rem(j, Y_SLOTS)
        xslot = lax.rem(j, X_SLOTS)
        first = (j == 0) | (be_ref[j] != be_ref[jnp.maximum(j - 1, 0)])
        ws = jnp.where(first & (j > 0), 1 - ws, ws)

        @pl.when(first)
        def _():
            for c in w_copies(be_ref[j], ws):
                c.wait()
            for r0 in range(0, D_MODEL, W_PREP_ROWS):
                rows = slice(r0, r0 + W_PREP_ROWS)
                wgb_ref[rows, :] = wgs_ref[ws, rows, :].astype(BF16)
                wub_ref[rows, :] = wus_ref[ws, rows, :].astype(BF16)
            for r0 in range(0, D_EXPERT, W_PREP_ROWS):
                rows = slice(r0, r0 + W_PREP_ROWS)
                wdb_ref[rows, :] = wds_ref[ws, rows, :].astype(BF16)

            @pl.when(nxt_ref[j] >= 0)
            def _():
                for c in w_copies(nxt_ref[j], 1 - ws):
                    c.start()

        ahead = j + X_SLOTS - 1

        @pl.when(ahead < n_used)
        def _():
            x_copy(ahead, lax.rem(ahead, X_SLOTS)).start()

        x_copy(j, xslot).wait()
        x = jnp.concatenate(
            [c.astype(BF16) for c in _row_tiles_chunks(xbuf_ref.at[xslot], MOE_BLK)], axis=1)
        g = _dot(x, wgb_ref[...])
        u = _dot(x, wub_ref[...])
        h = (g * (1.0 / (1.0 + jnp.exp(-g)))) * u
        y = _dot(h.astype(BF16), wdb_ref[...])

        @pl.when(j >= Y_SLOTS)
        def _():
            y_copy(j - Y_SLOTS, slot).wait()

        _row_tiles_store(ybuf_ref.at[slot], y)
        y_copy(j, slot).start()
        return ws

    lax.fori_loop(0, n_used, block, jnp.int32(0))

    for back in range(Y_SLOTS, 0, -1):
        y_copy(n_used - back, lax.rem(n_used - back, Y_SLOTS)).wait()

    def fill_wait(j, carry):
        fill_copy(j).wait()
        return carry

    lax.fori_loop(n_used, n_blocks, fill_wait, 0)


def _experts(blk_exp, nxt_exp, n_used, xb, w_gate, w_up, w_down, layer):
    n_blocks = blk_exp.shape[0]
    any_spec = pl.BlockSpec(memory_space=pl.ANY)
    blk = (MOE_BLK, ROW_TILES, LANES)
    grid_spec = pltpu.PrefetchScalarGridSpec(
        num_scalar_prefetch=3,
        grid=(1,),
        in_specs=[any_spec, any_spec, any_spec, any_spec],
        out_specs=any_spec,
        scratch_shapes=[
            pltpu.VMEM((X_SLOTS,) + blk, F32), pltpu.VMEM((Y_SLOTS,) + blk, F32),
            pltpu.VMEM((2, D_MODEL, D_EXPERT), F32), pltpu.VMEM((2, D_MODEL, D_EXPERT), F32),
            pltpu.VMEM((2, D_EXPERT, D_MODEL), F32),
            pltpu.VMEM((D_MODEL, D_EXPERT), BF16), pltpu.VMEM((D_MODEL, D_EXPERT), BF16),
            pltpu.VMEM((D_EXPERT, D_MODEL), BF16),
            pltpu.SemaphoreType.DMA((X_SLOTS,)), pltpu.SemaphoreType.DMA((Y_SLOTS,)),
            pltpu.SemaphoreType.DMA((2,)),
            pltpu.VMEM(blk, F32), pltpu.SemaphoreType.DMA(()),
        ],
    )
    return pl.pallas_call(
        functools.partial(_expert_kernel, layer=layer, n_blocks=n_blocks),
        grid_spec=grid_spec,
        out_shape=jax.ShapeDtypeStruct((n_blocks * MOE_BLK, ROW_TILES, LANES), F32),
        compiler_params=pltpu.CompilerParams(
            dimension_semantics=("arbitrary",), vmem_limit_bytes=VMEM_LIMIT),
        name="experts",
    )(blk_exp, nxt_exp, n_used, xb, w_gate, w_up, w_down)


def _dispatch_tables(route_t, counts_rec, T):
    counts = counts_rec[0, N_GROUPS:N_GROUPS + N_EXPERTS].astype(jnp.int32)
    n_steps = (T * TOP_K) // MOE_BLK + N_EXPERTS
    nblk = (counts + MOE_BLK - 1) // MOE_BLK
    bend = jnp.cumsum(nblk)
    pstart = (bend - nblk) * MOE_BLK
    n_used = bend[-1]
    j = jnp.minimum(jnp.arange(n_steps, dtype=jnp.int32), n_used - 1)
    blk_exp = jnp.minimum(jnp.sum(j[:, None] >= bend[None, :], axis=1), N_EXPERTS - 1)
    n_rows = n_steps * MOE_BLK
    last_blk = jnp.where(counts > 0, (bend - 1) * MOE_BLK, -1)
    ids = jnp.arange(N_EXPERTS, dtype=jnp.int32)
    later = (ids[None, :] > ids[:, None]) & (nblk[None, :] > 0)
    nxt_of = jnp.min(jnp.where(later, ids[None, :], N_EXPERTS), axis=1)
    nxt_tab = jnp.where(nxt_of < N_EXPERTS, nxt_of, -1)
    nxt_exp = jnp.sum(jnp.where(blk_exp[:, None] == ids[None, :], nxt_tab[None, :], 0), axis=1)
    e = route_t[:, R_E:R_E + TOP_K, :].astype(jnp.int32)
    rank = route_t[:, R_RANK:R_RANK + TOP_K, :].astype(jnp.int32)
    seg = jnp.sum(jnp.where(e[..., None] == jnp.arange(N_EXPERTS), pstart, 0), axis=-1)
    dest = jnp.clip(seg + rank, 0, n_steps * MOE_BLK - 1)
    dest_tiles = dest.reshape(T // TM, 1, TOP_K * TM)
    return dict(dest_tiles=dest_tiles, fill_start=last_blk.astype(jnp.int32),
                blk_exp=blk_exp.astype(jnp.int32), nxt_exp=nxt_exp.astype(jnp.int32),
                n_used=n_used.reshape(1).astype(jnp.int32), n_rows=n_rows)


def _prep_layer(l, w_gk_up, b_gk, gla_norm, gmlp_norm, w_spatial, b_spatial, w_conv,
                w_router_group, b_router_group, w_router_expert, b_router_expert):
    wgk = jnp.concatenate(
        [w_gk_up[l], jnp.zeros((LANES - GLA_GATE_RANK, GLA_KDIM), F32)], axis=0).astype(BF16)
    wsp = w_spatial[l].transpose(1, 0, 2).reshape(GMLP_CHUNK, GMLP_HEADS * GMLP_CHUNK)
    bsp = jnp.repeat(b_spatial[l].T, GMLP_DH, axis=1)
    wconv = jnp.concatenate([w_conv[l], jnp.zeros((8 - CONV_K, CONV_WIDTH), F32)], axis=0)
    wr = jnp.concatenate(
        [w_router_group[l], w_router_expert[l],
         jnp.zeros((D_MODEL, ROUTER_COLS - N_GROUPS - N_EXPERTS), F32)], axis=1)
    wr_hi = wr.astype(BF16)
    wr_lo = (wr - wr_hi.astype(F32)).astype(BF16)
    br = jnp.concatenate(
        [b_router_group[l], b_router_expert[l],
         jnp.zeros((ROUTER_COLS - N_GROUPS - N_EXPERTS,), F32)])[None, :]
    return dict(
        wgk=wgk, bgk=b_gk[l][None, :], glan=gla_norm[l][None, :], gmn=gmlp_norm[l][None, :],
        wsp=wsp, bsp=bsp, wconv=wconv, wr_cat=jnp.concatenate([wr_hi, wr_lo], axis=1), br=br)


def kernel(x, attn_norm, w_in, w_gk_up, b_gk, gla_norm, gmlp_norm, w_spatial, b_spatial, w_conv, w_out, ffn_norm, w_router_group, b_router_group, w_router_expert, b_router_expert, w_gate, w_up, w_down, final_norm):
    B, S, D = x.shape
    T = B * S
    depth = w_in.shape[0]
    xr = x.reshape(T, D)
    w_in_t = jnp.swapaxes(w_in, 1, 2)
    moe = None
    for l in range(depth):
        p = _prep_layer(l, w_gk_up, b_gk, gla_norm, gmlp_norm, w_spatial, b_spatial, w_conv,
                        w_router_group, b_router_group, w_router_expert, b_router_expert)
        mix_params = (p["wgk"], p["bgk"], p["glan"], p["gmn"], p["wsp"], p["bsp"], p["wconv"])
        if moe is None:
            mixed = _front(xr, attn_norm[l][None, :], w_in_t, l, mix_params, B, S)
        else:
            xr, mixed = _front(moe["x2"], attn_norm[l][None, :], w_in_t, l, mix_params, B, S, moe)
        x2, h2, route, route_t, counts_rec = _out_router(
            mixed, xr, w_out, l, ffn_norm[l][None, :], p["wr_cat"], p["br"])
        moe = _dispatch_tables(route_t, counts_rec, T)
        xb = _dispatch(moe["fill_start"], moe["n_used"], moe["dest_tiles"], h2, moe["n_rows"])
        yb = _experts(moe["blk_exp"], moe["nxt_exp"], moe["n_used"], xb, w_gate, w_up, w_down, l)
        moe.update(x2=x2, route=route, yb=yb)
    out = _combine_final_norm(moe["dest_tiles"], moe["x2"], moe["route"], moe["yb"],
                              final_norm[None, :])
    return out.reshape(B, S, D)
```

```python
import functools

import jax
import jax.numpy as jnp
from jax import lax
from jax.experimental import pallas as pl
from jax.experimental.pallas import tpu as pltpu

F32 = jnp.float32
BF16 = jnp.bfloat16

D_MODEL = 1024
RMS_EPS = 1e-6
GLA_HEADS = 4
GLA_WIDTH = 512
GLA_DV = 128
GLA_DK = 64
GLA_KDIM = 256
GLA_GATE_RANK = 16
GLA_GATE_NORM = 16.0
GLA_CHUNK = 64
GMLP_HEADS = 4
GMLP_WIDTH = 256
GMLP_DH = 64
GMLP_CHUNK = 128
CONV_WIDTH = 256
CONV_K = 3
N_GROUPS = 4
EXPERTS_PER_GROUP = 8
N_EXPERTS = 32
TOP_K = 2
D_EXPERT = 256

LANES = 128
C_Q, C_K, C_V, C_G = 0, 256, 512, 1024
C_U, C_VG, C_X, C_BG, C_CG, C_GKL = 1536, 1792, 2048, 2304, 2560, 2816
D_PROJ = C_GKL + LANES
D_IN = C_GKL + GLA_GATE_RANK

TM = 256
TS_MIX = TM
MOE_BLK = 256
ROUTER_COLS = LANES
SUBLANES = 8
ROW_TILES = D_MODEL // LANES
assert ROW_TILES == SUBLANES
VMEM_LIMIT = 56 * 1024 * 1024
R_E, R_RANK, R_W = 0, 2, 4


def _dot(a, b):
    return jnp.dot(a, b, preferred_element_type=F32)


def _split_bf16(x):
    hi = x.astype(BF16)
    lo = (x - hi.astype(F32)).astype(BF16)
    return hi, lo


def _rms(x, gain):
    return x * lax.rsqrt(jnp.mean(x * x, axis=-1, keepdims=True) + RMS_EPS) * gain


W_PREP_ROWS = 128
PROJ_CHUNK = 256


def _stage_w_in(wt_ref, wb_ref):
    for c0 in range(0, C_GKL, LANES):
        src = c0 if c0 < C_U else c0 + GLA_GATE_RANK
        wb_ref[:, c0:c0 + LANES] = wt_ref[0, src:src + LANES, :].T.astype(BF16)
    low = jnp.concatenate([wt_ref[0, C_U:C_U + GLA_GATE_RANK, :],
                           jnp.zeros((LANES - GLA_GATE_RANK, D_MODEL), F32)], axis=0)
    wb_ref[:, C_GKL:D_PROJ] = low.T.astype(BF16)


def _row_gather_copy(yb_ref, buf_ref, sem_ref, slot, k, r, d):
    return pltpu.make_async_copy(yb_ref.at[d], buf_ref.at[slot, k, r], sem_ref.at[slot])


def _gather_start(dest_ref, yb_ref, buf_ref, sem_ref, slot, rows=range(TM)):
    for r in rows:
        for k in range(TOP_K):
            _row_gather_copy(yb_ref, buf_ref, sem_ref, slot, k, r,
                             dest_ref[0, 0, k * TM + r]).start(priority=k)


def _gather_wait(yb_ref, buf_ref, sem_ref, slot):
    for k in range(TOP_K):
        pltpu.make_async_copy(yb_ref.at[pl.ds(0, TM)], buf_ref.at[slot, k], sem_ref.at[slot]).wait()


def _combined_residual(dcur_ref, x_ref, route_ref, yb_ref, buf_ref, sem_ref):
    i = pl.program_id(0)
    slot = lax.rem(i, 2)

    @pl.when(i == 0)
    def _():
        _gather_start(dcur_ref, yb_ref, buf_ref, sem_ref, 0)

    _gather_wait(yb_ref, buf_ref, sem_ref, slot)
    w0 = route_ref[:, R_W:R_W + 1]
    w1 = route_ref[:, R_W + 1:R_W + 2]
    y0 = _row_tiles_chunks(buf_ref.at[slot, 0], TM)
    y1 = _row_tiles_chunks(buf_ref.at[slot, 1], TM)
    return jnp.concatenate(
        [x_ref[:, c * LANES:(c + 1) * LANES] + (w0 * y0[c] + w1 * y1[c]) for c in range(ROW_TILES)],
        axis=1)


def _prefetch_groups(n_groups):
    per = -(-TM // n_groups)
    return [range(g * per, min(TM, (g + 1) * per)) for g in range(n_groups)]


def _drain_last_prefetch(yb_ref, buf_ref, sem_ref):
    i = pl.program_id(0)

    @pl.when(i == pl.num_programs(0) - 1)
    def _():
        _gather_wait(yb_ref, buf_ref, sem_ref, 1 - lax.rem(i, 2))


def _combine_specs(n_tiles):
    smem_tile = lambda f: pl.BlockSpec((1, 1, TOP_K * TM), f, memory_space=pltpu.SMEM)
    return [
        smem_tile(lambda i: (i, 0, 0)),
        smem_tile(lambda i: (jnp.minimum(i + 1, n_tiles - 1), 0, 0)),
        pl.BlockSpec((TM, D_MODEL), lambda i: (i, 0)),
        pl.BlockSpec((TM, LANES), lambda i: (i, 0)),
        pl.BlockSpec(memory_space=pl.ANY),
    ]


_COMBINE_SCRATCH = [pltpu.VMEM((2, TOP_K, TM, ROW_TILES, LANES), F32),
                    pltpu.SemaphoreType.DMA((2,))]


def _combine_final_norm_kernel(dcur_ref, dnxt_ref, x_ref, route_ref, yb_ref, gain_ref,
                               o_ref, buf_ref, sem_ref):
    _gather_start(dnxt_ref, yb_ref, buf_ref, sem_ref, 1 - lax.rem(pl.program_id(0), 2))
    x = _combined_residual(dcur_ref, x_ref, route_ref, yb_ref, buf_ref, sem_ref)
    o_ref[...] = _rms(x, gain_ref[...])
    _drain_last_prefetch(yb_ref, buf_ref, sem_ref)


def _combine_final_norm(dest_tiles, x2, route, yb, gain):
    T = x2.shape[0]
    n_tiles = T // TM
    return pl.pallas_call(
        _combine_final_norm_kernel,
        grid=(n_tiles,),
        in_specs=_combine_specs(n_tiles) + [pl.BlockSpec((1, D_MODEL), lambda i: (0, 0))],
        out_specs=pl.BlockSpec((TM, D_MODEL), lambda i: (i, 0)),
        out_shape=jax.ShapeDtypeStruct((T, D_MODEL), F32),
        scratch_shapes=_COMBINE_SCRATCH,
        compiler_params=pltpu.CompilerParams(
            dimension_semantics=("arbitrary",), vmem_limit_bytes=VMEM_LIMIT),
        name="combine_final_norm",
    )(dest_tiles, dest_tiles, x2, route, yb, gain)


def _gelu_tanh(x):
    c = 0.7978845608028654
    return x * (0.5 * (1.0 + jnp.tanh(c * (x + 0.044715 * (x * x * x)))))


def _mixer_kernel(proj_ref, wgk_ref, bgk_ref, glan_ref, gmn_ref, wsp_ref, bsp_ref, wconv_ref,
                  out_ref, st_ref, hc_ref, lcat_ref, wm_ref, *, seq_start, first_step, between):
    TS = TS_MIX
    n_gla = TS // GLA_CHUNK
    n_gm = TS // GMLP_CHUNK

    @pl.when(seq_start)
    def _():
        st_ref[...] = jnp.zeros_like(st_ref)
        hc_ref[...] = jnp.zeros_like(hc_ref)

    @pl.when(first_step)
    def _():
        r = lax.broadcasted_iota(jnp.int32, (TS, TS), 0)
        c = lax.broadcasted_iota(jnp.int32, (TS, TS), 1)
        keep = ((r // GLA_CHUNK) == (c // GLA_CHUNK)) & (c <= r)
        lcat_ref[...] = jnp.where(keep, 1.0, 0.0).astype(BF16)
        t = lax.broadcasted_iota(jnp.int32, (GMLP_CHUNK, GMLP_HEADS * GMLP_CHUNK), 0)
        s = lax.broadcasted_iota(jnp.int32, (GMLP_CHUNK, GMLP_HEADS * GMLP_CHUNK), 1) % GMLP_CHUNK
        wm_ref[...] = jnp.where(s <= t, wsp_ref[...], 0.0).astype(BF16)

    lane256 = lax.broadcasted_iota(jnp.int32, (1, GLA_KDIM), 1)

    q = proj_ref[:, C_Q:C_Q + GLA_KDIM].astype(F32)
    k = proj_ref[:, C_K:C_K + GLA_KDIM].astype(F32)
    v_b = proj_ref[:, C_V:C_V + GLA_WIDTH]
    z = _dot(proj_ref[:, C_GKL:C_GKL + LANES], wgk_ref[...]) + bgk_ref[...]
    gk = (jnp.minimum(z, 0.0) - jnp.log1p(jnp.exp(-jnp.abs(z)))) * (1.0 / GLA_GATE_NORM)
    gk_hi, gk_lo = _split_bf16(gk)
    cs = _dot(lcat_ref[...], jnp.concatenate([gk_hi, gk_lo], axis=1))
    b = cs[:, :GLA_KDIM] + cs[:, GLA_KDIM:]
    b_last = [b[(c + 1) * GLA_CHUNK - 1:(c + 1) * GLA_CHUNK, :] for c in range(n_gla)]
    bl = jnp.concatenate(
        [jnp.broadcast_to(t, (GLA_CHUNK, GLA_KDIM)) for t in b_last], axis=0)
    q_dec = (q * (GLA_DK ** -0.5)) * jnp.exp(b)
    k_inv = (k * jnp.exp(-b)).astype(BF16)
    k_dec = (k * jnp.exp(bl - b)).astype(BF16)
    q_dec_b = q_dec.astype(BF16)

    zero_b = jnp.zeros_like(q_dec_b)
    q_stack = jnp.concatenate(
        [jnp.where((lane256 // GLA_DK) == h, q_dec_b, zero_b) for h in range(GLA_HEADS)], axis=0)
    scores = lax.dot_general(q_stack, k_inv, (((1,), (1,)), ((), ())),
                             preferred_element_type=F32)
    rt = lax.broadcasted_iota(jnp.int32, (TS, TS), 0)
    ct = lax.broadcasted_iota(jnp.int32, (TS, TS), 1)
    causal = ((rt // GLA_CHUNK) == (ct // GLA_CHUNK)) & (ct <= rt)
    o_heads = []
    for h in range(GLA_HEADS):
        p_h = jnp.where(causal, scores[h * TS:(h + 1) * TS, :], 0.0).astype(BF16)
        o_heads.append(_dot(p_h, v_b[:, h * GLA_DV:(h + 1) * GLA_DV]))

    sr = lax.broadcasted_iota(jnp.int32, (GLA_WIDTH, GLA_KDIM), 0) // GLA_DV
    sc = lax.broadcasted_iota(jnp.int32, (GLA_WIDTH, GLA_KDIM), 1) // GLA_DK
    bd_mask = sr == sc
    o_inter = []
    for c in range(n_gla):
        rows = slice(c * GLA_CHUNK, (c + 1) * GLA_CHUNK)
        st = st_ref[...]
        o_inter.append(lax.dot_general(q_dec_b[rows], st.astype(BF16), (((1,), (1,)), ((), ())),
                                       preferred_element_type=F32))
        upd = lax.dot_general(v_b[rows], k_dec[rows], (((0,), (0,)), ((), ())),
                              preferred_element_type=F32)
        decay = jnp.exp(b_last[c])
        st_ref[...] = st * decay + jnp.where(bd_mask, upd, 0.0)
    o_inter = jnp.concatenate(o_inter, axis=0)

    for h in range(GLA_HEADS):
        cols = slice(h * GLA_DV, (h + 1) * GLA_DV)
        o = o_heads[h] + o_inter[:, cols]
        o = o * lax.rsqrt(jnp.mean(o * o, axis=-1, keepdims=True) + RMS_EPS) * glan_ref[...]
        g = proj_ref[:, C_G + h * GLA_DV:C_G + (h + 1) * GLA_DV].astype(F32)
        out_ref[:, cols] = (o * (g * (1.0 / (1.0 + jnp.exp(-g))))).astype(out_ref.dtype)

    between("gla_done")
    u = _gelu_tanh(proj_ref[:, C_U:C_U + GMLP_WIDTH].astype(F32))
    vg = _gelu_tanh(proj_ref[:, C_VG:C_VG + GMLP_WIDTH].astype(F32))
    hr = lax.broadcasted_iota(jnp.int32, (GMLP_WIDTH, GMLP_WIDTH), 0) // GMLP_DH
    hcn = lax.broadcasted_iota(jnp.int32, (GMLP_WIDTH, GMLP_WIDTH), 1) // GMLP_DH
    head_mean = jnp.where(hr == hcn, 1.0 / GMLP_DH, 0.0).astype(BF16)
    sq_hi, sq_lo = _split_bf16(vg * vg)
    ms = _dot(sq_hi, head_mean) + _dot(sq_lo, head_mean)
    v32 = vg * lax.rsqrt(ms + RMS_EPS) * gmn_ref[...]
    for c in range(n_gm):
        rows = slice(c * GMLP_CHUNK, (c + 1) * GMLP_CHUNK)
        vc = v32[rows].astype(BF16)
        zc = jnp.zeros_like(vc)
        rhs = jnp.concatenate(
            [jnp.where((lane256 // GMLP_DH) == h, vc, zc) for h in range(GMLP_HEADS)], axis=0)
        mixed = _dot(wm_ref[...], rhs) + bsp_ref[...]
        out_ref[rows, GLA_WIDTH:GLA_WIDTH + GMLP_WIDTH] = (u[rows] * mixed).astype(out_ref.dtype)

    between("gmlp_done")
    hcv = (proj_ref[:, C_CG:C_CG + CONV_WIDTH].astype(F32)
           * proj_ref[:, C_X:C_X + CONV_WIDTH].astype(F32))
    hc_ref[8:8 + TS, :] = hcv
    y = (wconv_ref[2:3, :] * hcv + wconv_ref[1:2, :] * hc_ref[7:7 + TS, :]
         + wconv_ref[0:1, :] * hc_ref[6:6 + TS, :])
    out_ref[:, GLA_WIDTH + GMLP_WIDTH:] = (
        proj_ref[:, C_BG:C_BG + CONV_WIDTH].astype(F32) * y).astype(out_ref.dtype)
    hc_ref[0:8, :] = hc_ref[TS:TS + 8, :]


_MIXER_SCRATCH = [
    pltpu.VMEM((GLA_WIDTH, GLA_KDIM), F32),
    pltpu.VMEM((TS_MIX + 8, CONV_WIDTH), F32),
    pltpu.VMEM((TS_MIX, TS_MIX), BF16),
    pltpu.VMEM((GMLP_CHUNK, GMLP_HEADS * GMLP_CHUNK), BF16),
]


N_MIX_PARAMS = 7
PROJ_CHUNKS_AT = {"gla_done": 4, "gmlp_done": 4}


def _front_kernel(*refs, tiles_per_seq, combine):
    refs = list(refs)
    if combine:
        dcur_ref, dnxt_ref, x_ref, route_ref, yb_ref = refs[:5]
        del refs[:5]
    else:
        x_ref = refs.pop(0)
    gain_ref, wt_ref = refs[:2]
    mix_refs = refs[2:2 + N_MIX_PARAMS]
    del refs[:2 + N_MIX_PARAMS]
    if combine:
        xo_ref, out_ref, buf_ref, sem_ref = refs[:4]
        del refs[:4]
    else:
        out_ref = refs.pop(0)
    st_ref, hc_ref, lcat_ref, wm_ref, wb_ref, pcur_ref, pnext_ref = refs
    s = pl.program_id(0)

    @pl.when(s == 0)
    def _():
        _stage_w_in(wt_ref, wb_ref)
        pcur_ref[...] = jnp.zeros_like(pcur_ref)

    col_chunks = [(c0, min(c0 + PROJ_CHUNK, D_PROJ)) for c0 in range(0, D_PROJ, PROJ_CHUNK)]
    work = list(zip(_prefetch_groups(len(col_chunks)), col_chunks))

    def project(h, n):
        for _ in range(min(n, len(work))):
            rows, (c0, c1) = work.pop(0)
            if combine:
                _gather_start(dnxt_ref, yb_ref, buf_ref, sem_ref, 1 - lax.rem(s, 2), rows)
            pnext_ref[:, c0:c1] = _dot(h, wb_ref[:, c0:c1]).astype(BF16)

    mixers = functools.partial(
        _mixer_kernel, pcur_ref, *mix_refs, out_ref, st_ref, hc_ref, lcat_ref, wm_ref,
        seq_start=lax.rem(jnp.maximum(s - 1, 0), tiles_per_seq) == 0, first_step=s == 0)
    if combine:
        x = _combined_residual(dcur_ref, x_ref, route_ref, yb_ref, buf_ref, sem_ref)
        xo_ref[...] = x
        h = _rms(x, gain_ref[...]).astype(BF16)
        mixers(between=lambda site: project(h, PROJ_CHUNKS_AT[site]))
        project(h, len(work))
        _drain_last_prefetch(yb_ref, buf_ref, sem_ref)
    else:
        mixers(between=lambda site: None)
        project(_rms(x_ref[...], gain_ref[...]).astype(BF16), len(work))
    pcur_ref[...] = pnext_ref[...]


def _front(x, gain, w_in_t, layer, mix_params, batch, seq, moe=None):
    n_seq = seq // TS_MIX
    n = batch * n_seq
    T = batch * seq
    cur = lambda s: jnp.minimum(s, n - 1)
    full = lambda shape: pl.BlockSpec(shape, lambda s: (0,) * len(shape))
    row = lambda w, f: pl.BlockSpec((TM, w), lambda s: (f(s), 0))
    in_specs, args = [row(D_MODEL, cur)], [x]
    out_specs = [row(D_MODEL, lambda s: jnp.maximum(s - 1, 0))]
    out_shape = [jax.ShapeDtypeStruct((T, D_MODEL), BF16)]
    scratch = list(_MIXER_SCRATCH)
    if moe is not None:
        smem_tile = lambda f: pl.BlockSpec((1, 1, TOP_K * TM), lambda s: (f(s), 0, 0),
                                           memory_space=pltpu.SMEM)
        in_specs = [smem_tile(cur), smem_tile(lambda s: jnp.minimum(s + 1, n - 1))] + in_specs + [
            row(LANES, cur), pl.BlockSpec(memory_space=pl.ANY)]
        args = [moe["dest_tiles"], moe["dest_tiles"]] + args + [moe["route"], moe["yb"]]
        out_specs = [row(D_MODEL, cur)] + out_specs
        out_shape = [jax.ShapeDtypeStruct((T, D_MODEL), F32)] + out_shape
        scratch = _COMBINE_SCRATCH + scratch
    in_specs += [
        full((1, D_MODEL)),
        pl.BlockSpec((1, D_IN, D_MODEL), lambda s: (layer, 0, 0), pipeline_mode=pl.Buffered(1)),
        full((LANES, GLA_KDIM)), full((1, GLA_KDIM)), full((1, GLA_DV)), full((1, GMLP_WIDTH)),
        full((GMLP_CHUNK, GMLP_HEADS * GMLP_CHUNK)), full((GMLP_CHUNK, GMLP_WIDTH)),
        full((8, CONV_WIDTH)),
    ]
    scratch += [pltpu.VMEM((D_MODEL, D_PROJ), BF16), pltpu.VMEM((TS_MIX, D_PROJ), BF16),
                pltpu.VMEM((TS_MIX, D_PROJ), BF16)]
    return pl.pallas_call(
        functools.partial(_front_kernel, tiles_per_seq=n_seq, combine=moe is not None),
        grid=(n + 1,),
        in_specs=in_specs,
        out_specs=out_specs if moe is not None else out_specs[0],
        out_shape=out_shape if moe is not None else out_shape[0],
        scratch_shapes=scratch,
        compiler_params=pltpu.CompilerParams(
            dimension_semantics=("arbitrary",), vmem_limit_bytes=VMEM_LIMIT),
        name="front",
    )(*args, gain, w_in_t, *mix_params)


def _row_tiles_store(tiles_ref, x):
    rows = x.shape[0]
    flat = tiles_ref.reshape(rows * ROW_TILES, LANES)
    for c in range(ROW_TILES):
        flat[pl.ds(c, rows, stride=ROW_TILES), :] = x[:, c * LANES:(c + 1) * LANES]


def _row_tiles_chunks(tiles_ref, rows):
    flat = tiles_ref.reshape(rows * ROW_TILES, LANES)
    return [flat[pl.ds(c, rows, stride=ROW_TILES), :] for c in range(ROW_TILES)]


IN_SLOTS = 3


def _out_router_kernel(mix_hbm, x_hbm, wo_ref, gain_ref, wrc_ref, br_ref,
                       x2_ref, route_ref, route_t_ref, cnt_ref, tri_ref, wob_ref, lg_ref,
                       mixbuf_ref, xbuf_ref, insem_ref):
    i = pl.program_id(0)
    n = pl.num_programs(0) - 1

    def in_copies(t):
        slot = lax.rem(t, IN_SLOTS)
        rows = pl.ds(t * TM, TM)
        return [pltpu.make_async_copy(mix_hbm.at[rows], mixbuf_ref.at[slot], insem_ref.at[slot]),
                pltpu.make_async_copy(x_hbm.at[rows], xbuf_ref.at[slot], insem_ref.at[slot])]

    @pl.when(i == 0)
    def _():
        for t in range(IN_SLOTS - 1):
            for c in in_copies(t):
                c.start()

    @pl.when(i + IN_SLOTS - 1 < n)
    def _():
        for c in in_copies(i + IN_SLOTS - 1):
            c.start()

    @pl.when(i < n)
    def _():
        for c in in_copies(i):
            c.wait()

    tile_slot = lax.rem(jnp.minimum(i, n - 1), IN_SLOTS)
    mix_ref = mixbuf_ref.at[tile_slot]
    x_ref = xbuf_ref.at[tile_slot]

    @pl.when(i == 0)
    def _():
        cnt_ref[...] = jnp.zeros_like(cnt_ref)
        lg_ref[...] = jnp.zeros_like(lg_ref)
        r = lax.broadcasted_iota(jnp.int32, (TM, TM), 0)
        c = lax.broadcasted_iota(jnp.int32, (TM, TM), 1)
        tri_ref[...] = jnp.where(c < r, 1.0, 0.0).astype(BF16)
        for r0 in range(0, D_MODEL, W_PREP_ROWS):
            wob_ref[r0:r0 + W_PREP_ROWS, :] = wo_ref[0, r0:r0 + W_PREP_ROWS, :].astype(BF16)

    lg = lg_ref[...]
    half = D_MODEL // 2
    mix = mix_ref[...]
    x2_a = x_ref[:, :half] + _dot(mix, wob_ref[:, :half])

    lane = lax.broadcasted_iota(jnp.int32, (TM, LANES), 1).astype(F32)
    neg = -jnp.inf
    is_g = lane < N_GROUPS
    gl = jnp.where(is_g, lg, neg)
    gmax = jnp.max(gl, axis=1, keepdims=True)
    g_top = jnp.min(jnp.where(gl == gmax, lane, float(LANES)), axis=1, keepdims=True)
    g_w = 1.0 / jnp.sum(jnp.where(is_g, jnp.exp(lg - gmax), 0.0), axis=1, keepdims=True)
    first = N_GROUPS + EXPERTS_PER_GROUP * g_top
    el = jnp.where((lane >= first) & (lane < first + EXPERTS_PER_GROUP), lg, neg)
    m1 = jnp.max(el, axis=1, keepdims=True)
    i1 = jnp.min(jnp.where(el == m1, lane, float(LANES)), axis=1, keepdims=True)
    el2 = jnp.where(lane == i1, neg, el)
    m2 = jnp.max(el2, axis=1, keepdims=True)
    i2 = jnp.min(jnp.where(el2 == m2, lane, float(LANES)), axis=1, keepdims=True)
    ratio = jnp.exp(m2 - m1)
    w1 = g_w / (1.0 + ratio)
    w2 = w1 * ratio

    x2_b = x_ref[:, half:] + _dot(mix, wob_ref[:, half:])

    oh1 = jnp.where(lane == i1, 1.0, 0.0)
    oh2 = jnp.where(lane == i2, 1.0, 0.0)
    oh = jnp.where(i > 0, oh1 + oh2, 0.0)
    before = _dot(tri_ref[...], oh.astype(BF16)) + cnt_ref[0:1, :]
    rank1 = jnp.sum(oh1 * before, axis=1, keepdims=True)
    rank2 = jnp.sum(oh2 * before, axis=1, keepdims=True)
    cnt_ref[...] = cnt_ref[...] + jnp.sum(oh, axis=0, keepdims=True)

    rec = jnp.zeros((TM, LANES), F32)
    for col, val in ((R_E, i1 - N_GROUPS), (R_E + 1, i2 - N_GROUPS), (R_RANK, rank1),
                     (R_RANK + 1, rank2), (R_W, w1), (R_W + 1, w2)):
        rec = jnp.where(lane == col, val, rec)
    route_ref[...] = rec
    route_t_ref[0] = rec.T[0:SUBLANES, :]

    x2 = jnp.concatenate([x2_a, x2_b], axis=1)
    x2_ref[...] = x2
    h = _rms(x2, gain_ref[...])
    h_hi, h_lo = _split_bf16(h)
    hh_hl = _dot(h_hi, wrc_ref[...])
    lg_ref[...] = (hh_hl[:, :ROUTER_COLS] + hh_hl[:, ROUTER_COLS:]
                   + _dot(h_lo, wrc_ref[:, :ROUTER_COLS]) + br_ref[...])


def _out_router(mixed, x, w_out, layer, gain, wr_cat, br):
    T = x.shape[0]
    n = T // TM
    row = lambda w: pl.BlockSpec((TM, w), lambda i: (jnp.minimum(i, n - 1), 0))
    lag = lambda i: jnp.maximum(i - 1, 0)
    full = lambda shape: pl.BlockSpec(shape, lambda i: (0, 0))
    wo_spec = pl.BlockSpec((1, D_MODEL, D_MODEL), lambda i: (layer, 0, 0),
                           pipeline_mode=pl.Buffered(1))
    return pl.pallas_call(
        _out_router_kernel,
        grid=(n + 1,),
        in_specs=[pl.BlockSpec(memory_space=pl.ANY), pl.BlockSpec(memory_space=pl.ANY), wo_spec,
                  full((1, D_MODEL)), full((D_MODEL, 2 * ROUTER_COLS)), full((1, ROUTER_COLS))],
        out_specs=[row(D_MODEL),
                   pl.BlockSpec((TM, LANES), lambda i: (lag(i), 0)),
                   pl.BlockSpec((1, SUBLANES, TM), lambda i: (lag(i), 0, 0)), full((8, LANES))],
        out_shape=[jax.ShapeDtypeStruct((T, D_MODEL), F32),
                   jax.ShapeDtypeStruct((T, LANES), F32),
                   jax.ShapeDtypeStruct((T // TM, SUBLANES, TM), F32),
                   jax.ShapeDtypeStruct((8, LANES), F32)],
        scratch_shapes=[pltpu.VMEM((TM, TM), BF16), pltpu.VMEM((D_MODEL, D_MODEL), BF16),
                        pltpu.VMEM((TM, ROUTER_COLS), F32),
                        pltpu.VMEM((IN_SLOTS, TM, D_MODEL), BF16),
                        pltpu.VMEM((IN_SLOTS, TM, D_MODEL), F32),
                        pltpu.SemaphoreType.DMA((IN_SLOTS,))],
        compiler_params=pltpu.CompilerParams(
            dimension_semantics=("arbitrary",), vmem_limit_bytes=VMEM_LIMIT),
        name="out_router",
    )(mixed, x, w_out, gain, wr_cat, br)


def _dispatch_kernel(fill_ref, nu_ref, dest_ref, x2_ref, gain_ref, xb_ref, zero_ref, sem_ref,
                     zsem_ref, stage_ref):
    i = pl.program_id(0)
    par = lax.rem(i, 2)
    last = i == pl.num_programs(0) - 1
    n_blocks = xb_ref.shape[0] // MOE_BLK
    spare_fills = [(j >= nu_ref[0], pltpu.make_async_copy(
        zero_ref, xb_ref.at[pl.ds(j * MOE_BLK, MOE_BLK)], zsem_ref.at[1]))
        for j in range(n_blocks - N_EXPERTS, n_blocks)]

    @pl.when(i == 0)
    def _():
        zero_ref[...] = jnp.zeros_like(zero_ref)
        fills = [(fill_ref[e] >= 0, pltpu.make_async_copy(
            zero_ref, xb_ref.at[pl.ds(pl.multiple_of(jnp.maximum(fill_ref[e], 0), MOE_BLK), MOE_BLK)],
            zsem_ref.at[0])) for e in range(N_EXPERTS)]
        for cond, f in fills + spare_fills:
            pl.when(cond)(f.start)
        for cond, f in fills:
            pl.when(cond)(f.wait)

    h = _rms(x2_ref[...], gain_ref[...]).astype(BF16)
    _row_tiles_store(stage_ref.at[par], h.astype(F32))
    for r in range(TM):
        for k in range(TOP_K):
            pltpu.make_async_copy(stage_ref.at[par, r], xb_ref.at[dest_ref[0, 0, k * TM + r]],
                                  sem_ref.at[par]).start(priority=k)

    def wait_tile(p):
        for _ in range(TOP_K):
            pltpu.make_async_copy(stage_ref.at[p], xb_ref.at[pl.ds(0, TM)], sem_ref.at[p]).wait()

    pl.when(i > 0)(lambda: wait_tile(1 - par))
    @pl.when(last)
    def _():
        wait_tile(par)
        for cond, f in spare_fills:
            pl.when(cond)(f.wait)


def _dispatch(fill_start, n_used, dest_tiles, x2, gain, n_rows):
    T = x2.shape[0]
    grid_spec = pltpu.PrefetchScalarGridSpec(
        num_scalar_prefetch=2,
        grid=(T // TM,),
        in_specs=[
            pl.BlockSpec((1, 1, TOP_K * TM), lambda i, fs, nu: (i, 0, 0), memory_space=pltpu.SMEM),
            pl.BlockSpec((TM, D_MODEL), lambda i, fs, nu: (i, 0)),
            pl.BlockSpec((1, D_MODEL), lambda i, fs, nu: (0, 0)),
        ],
        out_specs=pl.BlockSpec(memory_space=pl.ANY),
        scratch_shapes=[pltpu.VMEM((MOE_BLK, ROW_TILES, LANES), F32),
                        pltpu.SemaphoreType.DMA((2,)), pltpu.SemaphoreType.DMA((2,)),
                        pltpu.VMEM((2, TM, ROW_TILES, LANES), F32)],
    )
    return pl.pallas_call(
        _dispatch_kernel,
        grid_spec=grid_spec,
        out_shape=jax.ShapeDtypeStruct((n_rows, ROW_TILES, LANES), F32),
        compiler_params=pltpu.CompilerParams(dimension_semantics=("arbitrary",)),
        name="dispatch",
    )(fill_start, n_used, dest_tiles, x2, gain)


BLOCK_COPY_PARTS = 4
X_SLOTS = 4
Y_SLOTS = 3


class _CopyGroup:
    def __init__(self, copies):
        self.copies = copies

    def start(self):
        for n, c in enumerate(self.copies):
            c.start(priority=n % 2)

    def wait(self):
        for c in self.copies:
            c.wait()


def _expert_kernel(be_ref, nxt_ref, nu_ref, xb_ref, wg_ref, wu_ref, wd_ref, yb_ref,
                   xbuf_ref, ybuf_ref, wgs_ref, wus_ref, wds_ref, wgb_ref, wub_ref, wdb_ref,
                   xsem_ref, ysem_ref, wsem_ref, zbuf_ref, zsem_ref, *, layer, n_blocks):
    n_used = nu_ref[0]

    part = MOE_BLK // BLOCK_COPY_PARTS

    def x_copy(j, slot):
        return _CopyGroup([pltpu.make_async_copy(
            xb_ref.at[pl.ds(j * MOE_BLK + p * part, part)],
            xbuf_ref.at[slot, pl.ds(p * part, part)], xsem_ref.at[slot])
            for p in range(BLOCK_COPY_PARTS)])

    def y_copy(j, slot):
        return _CopyGroup([pltpu.make_async_copy(
            ybuf_ref.at[slot, pl.ds(p * part, part)],
            yb_ref.at[pl.ds(j * MOE_BLK + p * part, part)], ysem_ref.at[slot])
            for p in range(BLOCK_COPY_PARTS)])

    def w_copies(e, ws):
        return [pltpu.make_async_copy(src.at[layer, e], dst.at[ws], wsem_ref.at[ws])
                for src, dst in ((wg_ref, wgs_ref), (wu_ref, wus_ref), (wd_ref, wds_ref))]

    def fill_copy(j):
        return pltpu.make_async_copy(zbuf_ref, yb_ref.at[pl.ds(j * MOE_BLK, MOE_BLK)], zsem_ref)

    for j0 in range(X_SLOTS - 1):
        x_copy(j0, j0).start()
    for c in w_copies(be_ref[0], 0):
        c.start()

    zbuf_ref[...] = jnp.zeros_like(zbuf_ref)

    def fill(j, carry):
        fill_copy(j).start()
        return carry

    lax.fori_loop(n_used, n_blocks, fill, 0)

    def block(j, ws):
        slot = lax.rem(j, Y_SLOTS)
        xslot = lax.rem(j, X_SLOTS)
        first = (j == 0) | (be_ref[j] != be_ref[jnp.maximum(j - 1, 0)])
        ws = jnp.where(first & (j > 0), 1 - ws, ws)

        @pl.when(first)
        def _():
            for c in w_copies(be_ref[j], ws):
                c.wait()
            for r0 in range(0, D_MODEL, W_PREP_ROWS):
                rows = slice(r0, r0 + W_PREP_ROWS)
                wgb_ref[rows, :] = wgs_ref[ws, rows, :].astype(BF16)
                wub_ref[rows, :] = wus_ref[ws, rows, :].astype(BF16)
            for r0 in range(0, D_EXPERT, W_PREP_ROWS):
                rows = slice(r0, r0 + W_PREP_ROWS)
                wdb_ref[rows, :] = wds_ref[ws, rows, :].astype(BF16)

            @pl.when(nxt_ref[j] >= 0)
            def _():
                for c in w_copies(nxt_ref[j], 1 - ws):
                    c.start()

        ahead = j + X_SLOTS - 1

        @pl.when(ahead < n_used)
        def _():
            x_copy(ahead, lax.rem(ahead, X_SLOTS)).start()

        x_copy(j, xslot).wait()
        x = jnp.concatenate(
            [c.astype(BF16) for c in _row_tiles_chunks(xbuf_ref.at[xslot], MOE_BLK)], axis=1)
        g = _dot(x, wgb_ref[...])
        u = _dot(x, wub_ref[...])
        h = (g * (1.0 / (1.0 + jnp.exp(-g)))) * u
        y = _dot(h.astype(BF16), wdb_ref[...])

        @pl.when(j >= Y_SLOTS)
        def _():
            y_copy(j - Y_SLOTS, slot).wait()

        _row_tiles_store(ybuf_ref.at[slot], y)
        y_copy(j, slot).start()
        return ws

    lax.fori_loop(0, n_used, block, jnp.int32(0))

    for back in range(Y_SLOTS, 0, -1):
        y_copy(n_used - back, lax.rem(n_used - back, Y_SLOTS)).wait()

    def fill_wait(j, carry):
        fill_copy(j).wait()
        return carry

    lax.fori_loop(n_used, n_blocks, fill_wait, 0)


def _experts(blk_exp, nxt_exp, n_used, xb, w_gate, w_up, w_down, layer):
    n_blocks = blk_exp.shape[0]
    any_spec = pl.BlockSpec(memory_space=pl.ANY)
    blk = (MOE_BLK, ROW_TILES, LANES)
    grid_spec = pltpu.PrefetchScalarGridSpec(
        num_scalar_prefetch=3,
        grid=(1,),
        in_specs=[any_spec, any_spec, any_spec, any_spec],
        out_specs=any_spec,
        scratch_shapes=[
            pltpu.VMEM((X_SLOTS,) + blk, F32), pltpu.VMEM((Y_SLOTS,) + blk, F32),
            pltpu.VMEM((2, D_MODEL, D_EXPERT), F32), pltpu.VMEM((2, D_MODEL, D_EXPERT), F32),
            pltpu.VMEM((2, D_EXPERT, D_MODEL), F32),
            pltpu.VMEM((D_MODEL, D_EXPERT), BF16), pltpu.VMEM((D_MODEL, D_EXPERT), BF16),
            pltpu.VMEM((D_EXPERT, D_MODEL), BF16),
            pltpu.SemaphoreType.DMA((X_SLOTS,)), pltpu.SemaphoreType.DMA((Y_SLOTS,)),
            pltpu.SemaphoreType.DMA((2,)),
            pltpu.VMEM(blk, F32), pltpu.SemaphoreType.DMA(()),
        ],
    )
    return pl.pallas_call(
        functools.partial(_expert_kernel, layer=layer, n_blocks=n_blocks),
        grid_spec=grid_spec,
        out_shape=jax.ShapeDtypeStruct((n_blocks * MOE_BLK, ROW_TILES, LANES), F32),
        compiler_params=pltpu.CompilerParams(
            dimension_semantics=("arbitrary",), vmem_limit_bytes=VMEM_LIMIT),
        name="experts",
    )(blk_exp, nxt_exp, n_used, xb, w_gate, w_up, w_down)


def _dispatch_tables(route_t, counts_rec, T):
    counts = counts_rec[0, N_GROUPS:N_GROUPS + N_EXPERTS].astype(jnp.int32)
    n_steps = (T * TOP_K) // MOE_BLK + N_EXPERTS
    nblk = (counts + MOE_BLK - 1) // MOE_BLK
    bend = jnp.cumsum(nblk)
    pstart = (bend - nblk) * MOE_BLK
    n_used = bend[-1]
    j = jnp.minimum(jnp.arange(n_steps, dtype=jnp.int32), n_used - 1)
    blk_exp = jnp.minimum(jnp.sum(j[:, None] >= bend[None, :], axis=1), N_EXPERTS - 1)
    n_rows = n_steps * MOE_BLK
    last_blk = jnp.where(counts > 0, (bend - 1) * MOE_BLK, -1)
    ids = jnp.arange(N_EXPERTS, dtype=jnp.int32)
    later = (ids[None, :] > ids[:, None]) & (nblk[None, :] > 0)
    nxt_of = jnp.min(jnp.where(later, ids[None, :], N_EXPERTS), axis=1)
    nxt_tab = jnp.where(nxt_of < N_EXPERTS, nxt_of, -1)
    nxt_exp = jnp.sum(jnp.where(blk_exp[:, None] == ids[None, :], nxt_tab[None, :], 0), axis=1)
    e = route_t[:, R_E:R_E + TOP_K, :].astype(jnp.int32)
    rank = route_t[:, R_RANK:R_RANK + TOP_K, :].astype(jnp.int32)
    seg = jnp.sum(jnp.where(e[..., None] == jnp.arange(N_EXPERTS), pstart, 0), axis=-1)
    dest = jnp.clip(seg + rank, 0, n_steps * MOE_BLK - 1)
    dest_tiles = dest.reshape(T // TM, 1, TOP_K * TM)
    return dict(dest_tiles=dest_tiles, fill_start=last_blk.astype(jnp.int32),
                blk_exp=blk_exp.astype(jnp.int32), nxt_exp=nxt_exp.astype(jnp.int32),
                n_used=n_used.reshape(1).astype(jnp.int32), n_rows=n_rows)


def _prep_layer(l, w_gk_up, b_gk, gla_norm, gmlp_norm, w_spatial, b_spatial, w_conv,
                w_router_group, b_router_group, w_router_expert, b_router_expert):
    wgk = jnp.concatenate(
        [w_gk_up[l], jnp.zeros((LANES - GLA_GATE_RANK, GLA_KDIM), F32)], axis=0).astype(BF16)
    wsp = w_spatial[l].transpose(1, 0, 2).reshape(GMLP_CHUNK, GMLP_HEADS * GMLP_CHUNK)
    bsp = jnp.repeat(b_spatial[l].T, GMLP_DH, axis=1)
    wconv = jnp.concatenate([w_conv[l], jnp.zeros((8 - CONV_K, CONV_WIDTH), F32)], axis=0)
    wr = jnp.concatenate(
        [w_router_group[l], w_router_expert[l],
         jnp.zeros((D_MODEL, ROUTER_COLS - N_GROUPS - N_EXPERTS), F32)], axis=1)
    wr_hi = wr.astype(BF16)
    wr_lo = (wr - wr_hi.astype(F32)).astype(BF16)
    br = jnp.concatenate(
        [b_router_group[l], b_router_expert[l],
         jnp.zeros((ROUTER_COLS - N_GROUPS - N_EXPERTS,), F32)])[None, :]
    return dict(
        wgk=wgk, bgk=b_gk[l][None, :], glan=gla_norm[l][None, :], gmn=gmlp_norm[l][None, :],
        wsp=wsp, bsp=bsp, wconv=wconv, wr_cat=jnp.concatenate([wr_hi, wr_lo], axis=1), br=br)


def kernel(x, attn_norm, w_in, w_gk_up, b_gk, gla_norm, gmlp_norm, w_spatial, b_spatial, w_conv, w_out, ffn_norm, w_router_group, b_router_group, w_router_expert, b_router_expert, w_gate, w_up, w_down, final_norm):
    B, S, D = x.shape
    T = B * S
    depth = w_in.shape[0]
    xr = x.reshape(T, D)
    w_in_t = jnp.swapaxes(w_in, 1, 2)
    moe = None
    for l in range(depth):
        p = _prep_layer(l, w_gk_up, b_gk, gla_norm, gmlp_norm, w_spatial, b_spatial, w_conv,
                        w_router_group, b_router_group, w_router_expert, b_router_expert)
        mix_params = (p["wgk"], p["bgk"], p["glan"], p["gmn"], p["wsp"], p["bsp"], p["wconv"])
        if moe is None:
            mixed = _front(xr, attn_norm[l][None, :], w_in_t, l, mix_params, B, S)
        else:
            xr, mixed = _front(moe["x2"], attn_norm[l][None, :], w_in_t, l, mix_params, B, S, moe)
        x2, route, route_t, counts_rec = _out_router(
            mixed, xr, w_out, l, ffn_norm[l][None, :], p["wr_cat"], p["br"])
        moe = _dispatch_tables(route_t, counts_rec, T)
        xb = _dispatch(moe["fill_start"], moe["n_used"], moe["dest_tiles"], x2,
                       ffn_norm[l][None, :], moe["n_rows"])
        yb = _experts(moe["blk_exp"], moe["nxt_exp"], moe["n_used"], xb, w_gate, w_up, w_down, l)
        moe.update(x2=x2, route=route, yb=yb)
    out = _combine_final_norm(moe["dest_tiles"], moe["x2"], moe["route"], moe["yb"],
                              final_norm[None, :])
    return out.reshape(B, S, D)
```

```python
import functools

import jax
import jax.numpy as jnp
from jax import lax
from jax.experimental import pallas as pl
from jax.experimental.pallas import tpu as pltpu

F32 = jnp.float32
BF16 = jnp.bfloat16

D_MODEL = 1024
RMS_EPS = 1e-6
GLA_HEADS = 4
GLA_WIDTH = 512
GLA_DV = 128
GLA_DK = 64
GLA_KDIM = 256
GLA_GATE_RANK = 16
GLA_GATE_NORM = 16.0
GLA_CHUNK = 64
GMLP_HEADS = 4
GMLP_WIDTH = 256
GMLP_DH = 64
GMLP_CHUNK = 128
CONV_WIDTH = 256
CONV_K = 3
N_GROUPS = 4
EXPERTS_PER_GROUP = 8
N_EXPERTS = 32
TOP_K = 2
D_EXPERT = 256

LANES = 128
C_Q, C_K, C_V, C_G = 0, 256, 512, 1024
C_U, C_VG, C_X, C_BG, C_CG, C_GKL = 1536, 1792, 2048, 2304, 2560, 2816
D_PROJ = C_GKL + LANES
D_IN = C_GKL + GLA_GATE_RANK

TM = 256
TS_MIX = TM
MOE_BLK = 256
ROUTER_COLS = LANES
SUBLANES = 8
ROW_TILES = D_MODEL // LANES
assert ROW_TILES == SUBLANES
VMEM_LIMIT = 56 * 1024 * 1024
R_E, R_RANK, R_W = 0, 2, 4


def _dot(a, b):
    return jnp.dot(a, b, preferred_element_type=F32)


def _split_bf16(x):
    hi = x.astype(BF16)
    lo = (x - hi.astype(F32)).astype(BF16)
    return hi, lo


def _rms(x, gain):
    return x * lax.rsqrt(jnp.mean(x * x, axis=-1, keepdims=True) + RMS_EPS) * gain


W_PREP_ROWS = 128
PROJ_CHUNK = 256


def _stage_w_in(wt_ref, wb_ref):
    for c0 in range(0, C_GKL, LANES):
        src = c0 if c0 < C_U else c0 + GLA_GATE_RANK
        wb_ref[:, c0:c0 + LANES] = wt_ref[0, src:src + LANES, :].T.astype(BF16)
    low = jnp.concatenate([wt_ref[0, C_U:C_U + GLA_GATE_RANK, :],
                           jnp.zeros((LANES - GLA_GATE_RANK, D_MODEL), F32)], axis=0)
    wb_ref[:, C_GKL:D_PROJ] = low.T.astype(BF16)


def _row_gather_copy(yb_ref, buf_ref, sem_ref, slot, k, r, d):
    return pltpu.make_async_copy(yb_ref.at[d], buf_ref.at[slot, k, r], sem_ref.at[slot])


def _gather_start(dest_ref, yb_ref, buf_ref, sem_ref, slot, rows=range(TM)):
    for r in rows:
        for k in range(TOP_K):
            _row_gather_copy(yb_ref, buf_ref, sem_ref, slot, k, r,
                             dest_ref[0, 0, k * TM + r]).start(priority=1)


def _gather_wait(yb_ref, buf_ref, sem_ref, slot):
    for k in range(TOP_K):
        pltpu.make_async_copy(yb_ref.at[pl.ds(0, TM)], buf_ref.at[slot, k], sem_ref.at[slot]).wait()


def _combined_residual(dcur_ref, x_ref, route_ref, yb_ref, buf_ref, sem_ref):
    i = pl.program_id(0)
    slot = lax.rem(i, 2)

    @pl.when(i == 0)
    def _():
        _gather_start(dcur_ref, yb_ref, buf_ref, sem_ref, 0)

    _gather_wait(yb_ref, buf_ref, sem_ref, slot)
    w0 = route_ref[:, R_W:R_W + 1]
    w1 = route_ref[:, R_W + 1:R_W + 2]
    y0 = _row_tiles_chunks(buf_ref.at[slot, 0], TM)
    y1 = _row_tiles_chunks(buf_ref.at[slot, 1], TM)
    return jnp.concatenate(
        [x_ref[:, c * LANES:(c + 1) * LANES] + (w0 * y0[c] + w1 * y1[c]) for c in range(ROW_TILES)],
        axis=1)


def _prefetch_groups(n_groups):
    per = -(-TM // n_groups)
    return [range(g * per, min(TM, (g + 1) * per)) for g in range(n_groups)]


def _drain_last_prefetch(yb_ref, buf_ref, sem_ref):
    i = pl.program_id(0)

    @pl.when(i == pl.num_programs(0) - 1)
    def _():
        _gather_wait(yb_ref, buf_ref, sem_ref, 1 - lax.rem(i, 2))


def _combine_specs(n_tiles):
    smem_tile = lambda f: pl.BlockSpec((1, 1, TOP_K * TM), f, memory_space=pltpu.SMEM)
    return [
        smem_tile(lambda i: (i, 0, 0)),
        smem_tile(lambda i: (jnp.minimum(i + 1, n_tiles - 1), 0, 0)),
        pl.BlockSpec((TM, D_MODEL), lambda i: (i, 0)),
        pl.BlockSpec((TM, LANES), lambda i: (i, 0)),
        pl.BlockSpec(memory_space=pl.ANY),
    ]


_COMBINE_SCRATCH = [pltpu.VMEM((2, TOP_K, TM, ROW_TILES, LANES), F32),
                    pltpu.SemaphoreType.DMA((2,))]


def _combine_final_norm_kernel(dcur_ref, dnxt_ref, x_ref, route_ref, yb_ref, gain_ref,
                               o_ref, buf_ref, sem_ref):
    _gather_start(dnxt_ref, yb_ref, buf_ref, sem_ref, 1 - lax.rem(pl.program_id(0), 2))
    x = _combined_residual(dcur_ref, x_ref, route_ref, yb_ref, buf_ref, sem_ref)
    o_ref[...] = _rms(x, gain_ref[...])
    _drain_last_prefetch(yb_ref, buf_ref, sem_ref)


def _combine_final_norm(dest_tiles, x2, route, yb, gain):
    T = x2.shape[0]
    n_tiles = T // TM
    return pl.pallas_call(
        _combine_final_norm_kernel,
        grid=(n_tiles,),
        in_specs=_combine_specs(n_tiles) + [pl.BlockSpec((1, D_MODEL), lambda i: (0, 0))],
        out_specs=pl.BlockSpec((TM, D_MODEL), lambda i: (i, 0)),
        out_shape=jax.ShapeDtypeStruct((T, D_MODEL), F32),
        scratch_shapes=_COMBINE_SCRATCH,
        compiler_params=pltpu.CompilerParams(
            dimension_semantics=("arbitrary",), vmem_limit_bytes=VMEM_LIMIT),
        name="combine_final_norm",
    )(dest_tiles, dest_tiles, x2, route, yb, gain)


def _gelu_tanh(x):
    c = 0.7978845608028654
    return x * (0.5 * (1.0 + jnp.tanh(c * (x + 0.044715 * (x * x * x)))))


def _mixer_kernel(proj_ref, wgk_ref, bgk_ref, glan_ref, gmn_ref, wsp_ref, bsp_ref, wconv_ref,
                  out_ref, st_ref, hc_ref, lcat_ref, wm_ref, *, seq_start, first_step, between):
    TS = TS_MIX
    n_gla = TS // GLA_CHUNK
    n_gm = TS // GMLP_CHUNK

    @pl.when(seq_start)
    def _():
        st_ref[...] = jnp.zeros_like(st_ref)
        hc_ref[...] = jnp.zeros_like(hc_ref)

    @pl.when(first_step)
    def _():
        r = lax.broadcasted_iota(jnp.int32, (TS, TS), 0)
        c = lax.broadcasted_iota(jnp.int32, (TS, TS), 1)
        keep = ((r // GLA_CHUNK) == (c // GLA_CHUNK)) & (c <= r)
        lcat_ref[...] = jnp.where(keep, 1.0, 0.0).astype(BF16)
        t = lax.broadcasted_iota(jnp.int32, (GMLP_CHUNK, GMLP_HEADS * GMLP_CHUNK), 0)
        s = lax.broadcasted_iota(jnp.int32, (GMLP_CHUNK, GMLP_HEADS * GMLP_CHUNK), 1) % GMLP_CHUNK
        wm_ref[...] = jnp.where(s <= t, wsp_ref[...], 0.0).astype(BF16)

    lane256 = lax.broadcasted_iota(jnp.int32, (1, GLA_KDIM), 1)

    q = proj_ref[:, C_Q:C_Q + GLA_KDIM].astype(F32)
    k = proj_ref[:, C_K:C_K + GLA_KDIM].astype(F32)
    v_b = proj_ref[:, C_V:C_V + GLA_WIDTH]
    z = _dot(proj_ref[:, C_GKL:C_GKL + LANES], wgk_ref[...]) + bgk_ref[...]
    gk = (jnp.minimum(z, 0.0) - jnp.log1p(jnp.exp(-jnp.abs(z)))) * (1.0 / GLA_GATE_NORM)
    gk_hi, gk_lo = _split_bf16(gk)
    cs = _dot(lcat_ref[...], jnp.concatenate([gk_hi, gk_lo], axis=1))
    b = cs[:, :GLA_KDIM] + cs[:, GLA_KDIM:]
    b_last = [b[(c + 1) * GLA_CHUNK - 1:(c + 1) * GLA_CHUNK, :] for c in range(n_gla)]
    bl = jnp.concatenate(
        [jnp.broadcast_to(t, (GLA_CHUNK, GLA_KDIM)) for t in b_last], axis=0)
    q_dec = (q * (GLA_DK ** -0.5)) * jnp.exp(b)
    k_inv = (k * jnp.exp(-b)).astype(BF16)
    k_dec = (k * jnp.exp(bl - b)).astype(BF16)
    q_dec_b = q_dec.astype(BF16)

    zero_b = jnp.zeros_like(q_dec_b)
    q_stack = jnp.concatenate(
        [jnp.where((lane256 // GLA_DK) == h, q_dec_b, zero_b) for h in range(GLA_HEADS)], axis=0)
    scores = lax.dot_general(q_stack, k_inv, (((1,), (1,)), ((), ())),
                             preferred_element_type=F32)
    rt = lax.broadcasted_iota(jnp.int32, (TS, TS), 0)
    ct = lax.broadcasted_iota(jnp.int32, (TS, TS), 1)
    causal = ((rt // GLA_CHUNK) == (ct // GLA_CHUNK)) & (ct <= rt)
    o_heads = []
    for h in range(GLA_HEADS):
        p_h = jnp.where(causal, scores[h * TS:(h + 1) * TS, :], 0.0).astype(BF16)
        o_heads.append(_dot(p_h, v_b[:, h * GLA_DV:(h + 1) * GLA_DV]))

    sr = lax.broadcasted_iota(jnp.int32, (GLA_WIDTH, GLA_KDIM), 0) // GLA_DV
    sc = lax.broadcasted_iota(jnp.int32, (GLA_WIDTH, GLA_KDIM), 1) // GLA_DK
    bd_mask = sr == sc
    o_inter = []
    for c in range(n_gla):
        rows = slice(c * GLA_CHUNK, (c + 1) * GLA_CHUNK)
        st = st_ref[...]
        o_inter.append(lax.dot_general(q_dec_b[rows], st.astype(BF16), (((1,), (1,)), ((), ())),
                                       preferred_element_type=F32))
        upd = lax.dot_general(v_b[rows], k_dec[rows], (((0,), (0,)), ((), ())),
                              preferred_element_type=F32)
        decay = jnp.exp(b_last[c])
        st_ref[...] = st * decay + jnp.where(bd_mask, upd, 0.0)
    o_inter = jnp.concatenate(o_inter, axis=0)

    for h in range(GLA_HEADS):
        cols = slice(h * GLA_DV, (h + 1) * GLA_DV)
        o = o_heads[h] + o_inter[:, cols]
        o = o * lax.rsqrt(jnp.mean(o * o, axis=-1, keepdims=True) + RMS_EPS) * glan_ref[...]
        g = proj_ref[:, C_G + h * GLA_DV:C_G + (h + 1) * GLA_DV].astype(F32)
        out_ref[:, cols] = (o * (g * (1.0 / (1.0 + jnp.exp(-g))))).astype(out_ref.dtype)

    between("gla_done")
    u = _gelu_tanh(proj_ref[:, C_U:C_U + GMLP_WIDTH].astype(F32))
    vg = _gelu_tanh(proj_ref[:, C_VG:C_VG + GMLP_WIDTH].astype(F32))
    hr = lax.broadcasted_iota(jnp.int32, (GMLP_WIDTH, GMLP_WIDTH), 0) // GMLP_DH
    hcn = lax.broadcasted_iota(jnp.int32, (GMLP_WIDTH, GMLP_WIDTH), 1) // GMLP_DH
    head_mean = jnp.where(hr == hcn, 1.0 / GMLP_DH, 0.0).astype(BF16)
    sq_hi, sq_lo = _split_bf16(vg * vg)
    ms = _dot(sq_hi, head_mean) + _dot(sq_lo, head_mean)
    v32 = vg * lax.rsqrt(ms + RMS_EPS) * gmn_ref[...]
    for c in range(n_gm):
        rows = slice(c * GMLP_CHUNK, (c + 1) * GMLP_CHUNK)
        vc = v32[rows].astype(BF16)
        zc = jnp.zeros_like(vc)
        rhs = jnp.concatenate(
            [jnp.where((lane256 // GMLP_DH) == h, vc, zc) for h in range(GMLP_HEADS)], axis=0)
        mixed = _dot(wm_ref[...], rhs) + bsp_ref[...]
        out_ref[rows, GLA_WIDTH:GLA_WIDTH + GMLP_WIDTH] = (u[rows] * mixed).astype(out_ref.dtype)

    between("gmlp_done")
    hcv = (proj_ref[:, C_CG:C_CG + CONV_WIDTH].astype(F32)
           * proj_ref[:, C_X:C_X + CONV_WIDTH].astype(F32))
    hc_ref[8:8 + TS, :] = hcv
    y = (wconv_ref[2:3, :] * hcv + wconv_ref[1:2, :] * hc_ref[7:7 + TS, :]
         + wconv_ref[0:1, :] * hc_ref[6:6 + TS, :])
    out_ref[:, GLA_WIDTH + GMLP_WIDTH:] = (
        proj_ref[:, C_BG:C_BG + CONV_WIDTH].astype(F32) * y).astype(out_ref.dtype)
    hc_ref[0:8, :] = hc_ref[TS:TS + 8, :]


_MIXER_SCRATCH = [
    pltpu.VMEM((GLA_WIDTH, GLA_KDIM), F32),
    pltpu.VMEM((TS_MIX + 8, CONV_WIDTH), F32),
    pltpu.VMEM((TS_MIX, TS_MIX), BF16),
    pltpu.VMEM((GMLP_CHUNK, GMLP_HEADS * GMLP_CHUNK), BF16),
]


N_MIX_PARAMS = 7
PROJ_CHUNKS_AT = {"gla_done": 4, "gmlp_done": 4}


def _front_kernel(*refs, tiles_per_seq, combine):
    refs = list(refs)
    if combine:
        dcur_ref, dnxt_ref, x_ref, route_ref, yb_ref = refs[:5]
        del refs[:5]
    else:
        x_ref = refs.pop(0)
    gain_ref, wt_ref = refs[:2]
    mix_refs = refs[2:2 + N_MIX_PARAMS]
    del refs[:2 + N_MIX_PARAMS]
    if combine:
        xo_ref, out_ref, buf_ref, sem_ref = refs[:4]
        del refs[:4]
    else:
        out_ref = refs.pop(0)
    st_ref, hc_ref, lcat_ref, wm_ref, wb_ref, pcur_ref, pnext_ref = refs
    s = pl.program_id(0)

    @pl.when(s == 0)
    def _():
        _stage_w_in(wt_ref, wb_ref)
        pcur_ref[...] = jnp.zeros_like(pcur_ref)

    col_chunks = [(c0, min(c0 + PROJ_CHUNK, D_PROJ)) for c0 in range(0, D_PROJ, PROJ_CHUNK)]
    work = list(zip(_prefetch_groups(len(col_chunks)), col_chunks))

    def project(h, n):
        for _ in range(min(n, len(work))):
            rows, (c0, c1) = work.pop(0)
            if combine:
                _gather_start(dnxt_ref, yb_ref, buf_ref, sem_ref, 1 - lax.rem(s, 2), rows)
            pnext_ref[:, c0:c1] = _dot(h, wb_ref[:, c0:c1]).astype(BF16)

    mixers = functools.partial(
        _mixer_kernel, pcur_ref, *mix_refs, out_ref, st_ref, hc_ref, lcat_ref, wm_ref,
        seq_start=lax.rem(jnp.maximum(s - 1, 0), tiles_per_seq) == 0, first_step=s == 0)
    if combine:
        x = _combined_residual(dcur_ref, x_ref, route_ref, yb_ref, buf_ref, sem_ref)
        xo_ref[...] = x
        h = _rms(x, gain_ref[...]).astype(BF16)
        mixers(between=lambda site: project(h, PROJ_CHUNKS_AT[site]))
        project(h, len(work))
        _drain_last_prefetch(yb_ref, buf_ref, sem_ref)
    else:
        mixers(between=lambda site: None)
        project(_rms(x_ref[...], gain_ref[...]).astype(BF16), len(work))
    pcur_ref[...] = pnext_ref[...]


def _front(x, gain, w_in_t, layer, mix_params, batch, seq, moe=None):
    n_seq = seq // TS_MIX
    n = batch * n_seq
    T = batch * seq
    cur = lambda s: jnp.minimum(s, n - 1)
    full = lambda shape: pl.BlockSpec(shape, lambda s: (0,) * len(shape))
    row = lambda w, f: pl.BlockSpec((TM, w), lambda s: (f(s), 0))
    in_specs, args = [row(D_MODEL, cur)], [x]
    out_specs = [row(D_MODEL, lambda s: jnp.maximum(s - 1, 0))]
    out_shape = [jax.ShapeDtypeStruct((T, D_MODEL), BF16)]
    scratch = list(_MIXER_SCRATCH)
    if moe is not None:
        smem_tile = lambda f: pl.BlockSpec((1, 1, TOP_K * TM), lambda s: (f(s), 0, 0),
                                           memory_space=pltpu.SMEM)
        in_specs = [smem_tile(cur), smem_tile(lambda s: jnp.minimum(s + 1, n - 1))] + in_specs + [
            row(LANES, cur), pl.BlockSpec(memory_space=pl.ANY)]
        args = [moe["dest_tiles"], moe["dest_tiles"]] + args + [moe["route"], moe["yb"]]
        out_specs = [row(D_MODEL, cur)] + out_specs
        out_shape = [jax.ShapeDtypeStruct((T, D_MODEL), F32)] + out_shape
        scratch = _COMBINE_SCRATCH + scratch
    in_specs += [
        full((1, D_MODEL)),
        pl.BlockSpec((1, D_IN, D_MODEL), lambda s: (layer, 0, 0), pipeline_mode=pl.Buffered(1)),
        full((LANES, GLA_KDIM)), full((1, GLA_KDIM)), full((1, GLA_DV)), full((1, GMLP_WIDTH)),
        full((GMLP_CHUNK, GMLP_HEADS * GMLP_CHUNK)), full((GMLP_CHUNK, GMLP_WIDTH)),
        full((8, CONV_WIDTH)),
    ]
    scratch += [pltpu.VMEM((D_MODEL, D_PROJ), BF16), pltpu.VMEM((TS_MIX, D_PROJ), BF16),
                pltpu.VMEM((TS_MIX, D_PROJ), BF16)]
    return pl.pallas_call(
        functools.partial(_front_kernel, tiles_per_seq=n_seq, combine=moe is not None),
        grid=(n + 1,),
        in_specs=in_specs,
        out_specs=out_specs if moe is not None else out_specs[0],
        out_shape=out_shape if moe is not None else out_shape[0],
        scratch_shapes=scratch,
        compiler_params=pltpu.CompilerParams(
            dimension_semantics=("arbitrary",), vmem_limit_bytes=VMEM_LIMIT),
        name="front",
    )(*args, gain, w_in_t, *mix_params)


def _row_tiles_store(tiles_ref, x):
    rows = x.shape[0]
    flat = tiles_ref.reshape(rows * ROW_TILES, LANES)
    for c in range(ROW_TILES):
        flat[pl.ds(c, rows, stride=ROW_TILES), :] = x[:, c * LANES:(c + 1) * LANES]


def _row_tiles_chunks(tiles_ref, rows):
    flat = tiles_ref.reshape(rows * ROW_TILES, LANES)
    return [flat[pl.ds(c, rows, stride=ROW_TILES), :] for c in range(ROW_TILES)]


IN_SLOTS = 3


def _out_router_kernel(mix_hbm, x_hbm, wo_ref, gain_ref, wrc_ref, br_ref,
                       x2_ref, h2_ref, route_ref, route_t_ref, cnt_ref, tri_ref, wob_ref, lg_ref,
                       mixbuf_ref, xbuf_ref, insem_ref):
    i = pl.program_id(0)
    n = pl.num_programs(0) - 1

    def in_copies(t):
        slot = lax.rem(t, IN_SLOTS)
        rows = pl.ds(t * TM, TM)
        return [pltpu.make_async_copy(mix_hbm.at[rows], mixbuf_ref.at[slot], insem_ref.at[slot]),
                pltpu.make_async_copy(x_hbm.at[rows], xbuf_ref.at[slot], insem_ref.at[slot])]

    @pl.when(i == 0)
    def _():
        for t in range(IN_SLOTS - 1):
            for c in in_copies(t):
                c.start()

    @pl.when(i + IN_SLOTS - 1 < n)
    def _():
        for c in in_copies(i + IN_SLOTS - 1):
            c.start()

    @pl.when(i < n)
    def _():
        for c in in_copies(i):
            c.wait()

    tile_slot = lax.rem(jnp.minimum(i, n - 1), IN_SLOTS)
    mix_ref = mixbuf_ref.at[tile_slot]
    x_ref = xbuf_ref.at[tile_slot]

    @pl.when(i == 0)
    def _():
        cnt_ref[...] = jnp.zeros_like(cnt_ref)
        lg_ref[...] = jnp.zeros_like(lg_ref)
        r = lax.broadcasted_iota(jnp.int32, (TM, TM), 0)
        c = lax.broadcasted_iota(jnp.int32, (TM, TM), 1)
        tri_ref[...] = jnp.where(c < r, 1.0, 0.0).astype(BF16)
        for r0 in range(0, D_MODEL, W_PREP_ROWS):
            wob_ref[r0:r0 + W_PREP_ROWS, :] = wo_ref[0, r0:r0 + W_PREP_ROWS, :].astype(BF16)

    lg = lg_ref[...]
    half = D_MODEL // 2
    mix = mix_ref[...]
    x2_a = x_ref[:, :half] + _dot(mix, wob_ref[:, :half])

    lane = lax.broadcasted_iota(jnp.int32, (TM, LANES), 1).astype(F32)
    neg = -jnp.inf
    is_g = lane < N_GROUPS
    gl = jnp.where(is_g, lg, neg)
    gmax = jnp.max(gl, axis=1, keepdims=True)
    g_top = jnp.min(jnp.where(gl == gmax, lane, float(LANES)), axis=1, keepdims=True)
    g_w = 1.0 / jnp.sum(jnp.where(is_g, jnp.exp(lg - gmax), 0.0), axis=1, keepdims=True)
    first = N_GROUPS + EXPERTS_PER_GROUP * g_top
    el = jnp.where((lane >= first) & (lane < first + EXPERTS_PER_GROUP), lg, neg)
    m1 = jnp.max(el, axis=1, keepdims=True)
    i1 = jnp.min(jnp.where(el == m1, lane, float(LANES)), axis=1, keepdims=True)
    el2 = jnp.where(lane == i1, neg, el)
    m2 = jnp.max(el2, axis=1, keepdims=True)
    i2 = jnp.min(jnp.where(el2 == m2, lane, float(LANES)), axis=1, keepdims=True)
    ratio = jnp.exp(m2 - m1)
    w1 = g_w / (1.0 + ratio)
    w2 = w1 * ratio

    x2_b = x_ref[:, half:] + _dot(mix, wob_ref[:, half:])

    oh1 = jnp.where(lane == i1, 1.0, 0.0)
    oh2 = jnp.where(lane == i2, 1.0, 0.0)
    oh = jnp.where(i > 0, oh1 + oh2, 0.0)
    before = _dot(tri_ref[...], oh.astype(BF16)) + cnt_ref[0:1, :]
    rank1 = jnp.sum(oh1 * before, axis=1, keepdims=True)
    rank2 = jnp.sum(oh2 * before, axis=1, keepdims=True)
    cnt_ref[...] = cnt_ref[...] + jnp.sum(oh, axis=0, keepdims=True)

    rec = jnp.zeros((TM, LANES), F32)
    for col, val in ((R_E, i1 - N_GROUPS), (R_E + 1, i2 - N_GROUPS), (R_RANK, rank1),
                     (R_RANK + 1, rank2), (R_W, w1), (R_W + 1, w2)):
        rec = jnp.where(lane == col, val, rec)
    route_ref[...] = rec
    route_t_ref[0] = rec.T[0:SUBLANES, :]

    x2 = jnp.concatenate([x2_a, x2_b], axis=1)
    x2_ref[...] = x2
    h = _rms(x2, gain_ref[...])
    h_hi, h_lo = _split_bf16(h)
    h2_ref[...] = h_hi
    hh_hl = _dot(h_hi, wrc_ref[...])
    lg_ref[...] = (hh_hl[:, :ROUTER_COLS] + hh_hl[:, ROUTER_COLS:]
                   + _dot(h_lo, wrc_ref[:, :ROUTER_COLS]) + br_ref[...])


def _out_router(mixed, x, w_out, layer, gain, wr_cat, br):
    T = x.shape[0]
    n = T // TM
    row = lambda w: pl.BlockSpec((TM, w), lambda i: (jnp.minimum(i, n - 1), 0))
    lag = lambda i: jnp.maximum(i - 1, 0)
    full = lambda shape: pl.BlockSpec(shape, lambda i: (0, 0))
    wo_spec = pl.BlockSpec((1, D_MODEL, D_MODEL), lambda i: (layer, 0, 0),
                           pipeline_mode=pl.Buffered(1))
    return pl.pallas_call(
        _out_router_kernel,
        grid=(n + 1,),
        in_specs=[pl.BlockSpec(memory_space=pl.ANY), pl.BlockSpec(memory_space=pl.ANY), wo_spec,
                  full((1, D_MODEL)), full((D_MODEL, 2 * ROUTER_COLS)), full((1, ROUTER_COLS))],
        out_specs=[row(D_MODEL), row(D_MODEL),
                   pl.BlockSpec((TM, LANES), lambda i: (lag(i), 0)),
                   pl.BlockSpec((1, SUBLANES, TM), lambda i: (lag(i), 0, 0)), full((8, LANES))],
        out_shape=[jax.ShapeDtypeStruct((T, D_MODEL), F32),
                   jax.ShapeDtypeStruct((T, D_MODEL), BF16),
                   jax.ShapeDtypeStruct((T, LANES), F32),
                   jax.ShapeDtypeStruct((T // TM, SUBLANES, TM), F32),
                   jax.ShapeDtypeStruct((8, LANES), F32)],
        scratch_shapes=[pltpu.VMEM((TM, TM), BF16), pltpu.VMEM((D_MODEL, D_MODEL), BF16),
                        pltpu.VMEM((TM, ROUTER_COLS), F32),
                        pltpu.VMEM((IN_SLOTS, TM, D_MODEL), BF16),
                        pltpu.VMEM((IN_SLOTS, TM, D_MODEL), F32),
                        pltpu.SemaphoreType.DMA((IN_SLOTS,))],
        compiler_params=pltpu.CompilerParams(
            dimension_semantics=("arbitrary",), vmem_limit_bytes=VMEM_LIMIT),
        name="out_router",
    )(mixed, x, w_out, gain, wr_cat, br)


def _dispatch_kernel(fill_ref, nu_ref, dest_ref, h_ref, xb_ref, zero_ref, sem_ref, zsem_ref,
                     stage_ref):
    i = pl.program_id(0)
    par = lax.rem(i, 2)
    last = i == pl.num_programs(0) - 1
    n_blocks = xb_ref.shape[0] // MOE_BLK
    spare_fills = [(j >= nu_ref[0], pltpu.make_async_copy(
        zero_ref, xb_ref.at[pl.ds(j * MOE_BLK, MOE_BLK)], zsem_ref.at[1]))
        for j in range(n_blocks - N_EXPERTS, n_blocks)]

    @pl.when(i == 0)
    def _():
        zero_ref[...] = jnp.zeros_like(zero_ref)
        fills = [(fill_ref[e] >= 0, pltpu.make_async_copy(
            zero_ref, xb_ref.at[pl.ds(pl.multiple_of(jnp.maximum(fill_ref[e], 0), MOE_BLK), MOE_BLK)],
            zsem_ref.at[0])) for e in range(N_EXPERTS)]
        for cond, f in fills + spare_fills:
            pl.when(cond)(f.start)
        for cond, f in fills:
            pl.when(cond)(f.wait)

    _row_tiles_store(stage_ref.at[par], h_ref[...].astype(F32))
    for r in range(TM):
        for k in range(TOP_K):
            pltpu.make_async_copy(stage_ref.at[par, r], xb_ref.at[dest_ref[0, 0, k * TM + r]],
                                  sem_ref.at[par]).start(priority=1)

    def wait_tile(p):
        for _ in range(TOP_K):
            pltpu.make_async_copy(stage_ref.at[p], xb_ref.at[pl.ds(0, TM)], sem_ref.at[p]).wait()

    pl.when(i > 0)(lambda: wait_tile(1 - par))
    @pl.when(last)
    def _():
        wait_tile(par)
        for cond, f in spare_fills:
            pl.when(cond)(f.wait)


def _dispatch(fill_start, n_used, dest_tiles, h2, n_rows):
    T = h2.shape[0]
    grid_spec = pltpu.PrefetchScalarGridSpec(
        num_scalar_prefetch=2,
        grid=(T // TM,),
        in_specs=[
            pl.BlockSpec((1, 1, TOP_K * TM), lambda i, fs, nu: (i, 0, 0), memory_space=pltpu.SMEM),
            pl.BlockSpec((TM, D_MODEL), lambda i, fs, nu: (i, 0)),
        ],
        out_specs=pl.BlockSpec(memory_space=pl.ANY),
        scratch_shapes=[pltpu.VMEM((MOE_BLK, ROW_TILES, LANES), F32),
                        pltpu.SemaphoreType.DMA((2,)), pltpu.SemaphoreType.DMA((2,)),
                        pltpu.VMEM((2, TM, ROW_TILES, LANES), F32)],
    )
    return pl.pallas_call(
        _dispatch_kernel,
        grid_spec=grid_spec,
        out_shape=jax.ShapeDtypeStruct((n_rows, ROW_TILES, LANES), F32),
        compiler_params=pltpu.CompilerParams(dimension_semantics=("arbitrary",)),
        name="dispatch",
    )(fill_start, n_used, dest_tiles, h2)


BLOCK_COPY_PARTS = 4
X_SLOTS = 4
Y_SLOTS = 3


class _CopyGroup:
    def __init__(self, copies):
        self.copies = copies

    def start(self):
        for n, c in enumerate(self.copies):
            c.start(priority=n % 2)

    def wait(self):
        for c in self.copies:
            c.wait()


def _expert_kernel(be_ref, nxt_ref, nu_ref, xb_ref, wg_ref, wu_ref, wd_ref, yb_ref,
                   xbuf_ref, ybuf_ref, wgs_ref, wus_ref, wds_ref, wgb_ref, wub_ref, wdb_ref,
                   xsem_ref, ysem_ref, wsem_ref, zbuf_ref, zsem_ref, *, layer, n_blocks):
    n_used = nu_ref[0]

    part = MOE_BLK // BLOCK_COPY_PARTS

    def x_copy(j, slot):
        return _CopyGroup([pltpu.make_async_copy(
            xb_ref.at[pl.ds(j * MOE_BLK + p * part, part)],
            xbuf_ref.at[slot, pl.ds(p * part, part)], xsem_ref.at[slot])
            for p in range(BLOCK_COPY_PARTS)])

    def y_copy(j, slot):
        return _CopyGroup([pltpu.make_async_copy(
            ybuf_ref.at[slot, pl.ds(p * part, part)],
            yb_ref.at[pl.ds(j * MOE_BLK + p * part, part)], ysem_ref.at[slot])
            for p in range(BLOCK_COPY_PARTS)])

    def w_copies(e, ws):
        return [pltpu.make_async_copy(src.at[layer, e], dst.at[ws], wsem_ref.at[ws])
                for src, dst in ((wg_ref, wgs_ref), (wu_ref, wus_ref), (wd_ref, wds_ref))]

    def fill_copy(j):
        return pltpu.make_async_copy(zbuf_ref, yb_ref.at[pl.ds(j * MOE_BLK, MOE_BLK)], zsem_ref)

    for j0 in range(X_SLOTS - 1):
        x_copy(j0, j0).start()
    for c in w_copies(be_ref[0], 0):
        c.start()

    zbuf_ref[...] = jnp.zeros_like(zbuf_ref)

    def fill(j, carry):
        fill_copy(j).start()
        return carry

    lax.fori_loop(n_used, n_blocks, fill, 0)

    def block(j, ws):
        slot = lax.rem(j, Y_SLOTS)
        xslot = lax.rem(j, X_SLOTS)
        first = (j == 0) | (be_ref[j] != be_ref[jnp.maximum(j - 1, 0)])
        ws = jnp.where(first & (j > 0), 1 - ws, ws)

        @pl.when(first)
        def _():
            for c in w_copies(be_ref[j], ws):
                c.wait()
            for r0 in range(0, D_MODEL, W_PREP_ROWS):
                rows = slice(r0, r0 + W_PREP_ROWS)
                wgb_ref[rows, :] = wgs_ref[ws, rows, :].astype(BF16)
                wub_ref[rows, :] = wus_ref[ws, rows, :].astype(BF16)
            for r0 in range(0, D_EXPERT, W_PREP_ROWS):
                rows = slice(r0, r0 + W_PREP_ROWS)
                wdb_ref[rows, :] = wds_ref[ws, rows, :].astype(BF16)

            @pl.when(nxt_ref[j] >= 0)
            def _():
                for c in w_copies(nxt_ref[j], 1 - ws):
                    c.start()

        ahead = j + X_SLOTS - 1

        @pl.when(ahead < n_used)
        def _():
            x_copy(ahead, lax.rem(ahead, X_SLOTS)).start()

        x_copy(j, xslot).wait()
        x = jnp.concatenate(
            [c.astype(BF16) for c in _row_tiles_chunks(xbuf_ref.at[xslot], MOE_BLK)], axis=1)
        g = _dot(x, wgb_ref[...])
        u = _dot(x, wub_ref[...])
        h = (g * (1.0 / (1.0 + jnp.exp(-g)))) * u
        y = _dot(h.astype(BF16), wdb_ref[...])

        @pl.when(j >= Y_SLOTS)
        def _():
            y_copy(j - Y_SLOTS, slot).wait()

        _row_tiles_store(ybuf_ref.at[slot], y)
        y_copy(j, slot).start()
        return ws

    lax.fori_loop(0, n_used, block, jnp.int32(0))

    for back in range(Y_SLOTS, 0, -1):
        y_copy(n_used - back, lax.rem(n_used - back, Y_SLOTS)).wait()

    def fill_wait(j, carry):
        fill_copy(j).wait()
        return carry

    lax.fori_loop(n_used, n_blocks, fill_wait, 0)


def _experts(blk_exp, nxt_exp, n_used, xb, w_gate, w_up, w_down, layer):
    n_blocks = blk_exp.shape[0]
    any_spec = pl.BlockSpec(memory_space=pl.ANY)
    blk = (MOE_BLK, ROW_TILES, LANES)
    grid_spec = pltpu.PrefetchScalarGridSpec(
        num_scalar_prefetch=3,
        grid=(1,),
        in_specs=[any_spec, any_spec, any_spec, any_spec],
        out_specs=any_spec,
        scratch_shapes=[
            pltpu.VMEM((X_SLOTS,) + blk, F32), pltpu.VMEM((Y_SLOTS,) + blk, F32),
            pltpu.VMEM((2, D_MODEL, D_EXPERT), F32), pltpu.VMEM((2, D_MODEL, D_EXPERT), F32),
            pltpu.VMEM((2, D_EXPERT, D_MODEL), F32),
            pltpu.VMEM((D_MODEL, D_EXPERT), BF16), pltpu.VMEM((D_MODEL, D_EXPERT), BF16),
            pltpu.VMEM((D_EXPERT, D_MODEL), BF16),
            pltpu.SemaphoreType.DMA((X_SLOTS,)), pltpu.SemaphoreType.DMA((Y_SLOTS,)),
            pltpu.SemaphoreType.DMA((2,)),
            pltpu.VMEM(blk, F32), pltpu.SemaphoreType.DMA(()),
        ],
    )
    return pl.pallas_call(
        functools.partial(_expert_kernel, layer=layer, n_blocks=n_blocks),
        grid_spec=grid_spec,
        out_shape=jax.ShapeDtypeStruct((n_blocks * MOE_BLK, ROW_TILES, LANES), F32),
        compiler_params=pltpu.CompilerParams(
            dimension_semantics=("arbitrary",), vmem_limit_bytes=VMEM_LIMIT),
        name="experts",
    )(blk_exp, nxt_exp, n_used, xb, w_gate, w_up, w_down)


def _dispatch_tables(route_t, counts_rec, T):
    counts = counts_rec[0, N_GROUPS:N_GROUPS + N_EXPERTS].astype(jnp.int32)
    n_steps = (T * TOP_K) // MOE_BLK + N_EXPERTS
    nblk = (counts + MOE_BLK - 1) // MOE_BLK
    bend = jnp.cumsum(nblk)
    pstart = (bend - nblk) * MOE_BLK
    n_used = bend[-1]
    j = jnp.minimum(jnp.arange(n_steps, dtype=jnp.int32), n_used - 1)
    blk_exp = jnp.minimum(jnp.sum(j[:, None] >= bend[None, :], axis=1), N_EXPERTS - 1)
    n_rows = n_steps * MOE_BLK
    last_blk = jnp.where(counts > 0, (bend - 1) * MOE_BLK, -1)
    ids = jnp.arange(N_EXPERTS, dtype=jnp.int32)
    later = (ids[None, :] > ids[:, None]) & (nblk[None, :] > 0)
    nxt_of = jnp.min(jnp.where(later, ids[None, :], N_EXPERTS), axis=1)
    nxt_tab = jnp.where(nxt_of < N_EXPERTS, nxt_of, -1)
    nxt_exp = jnp.sum(jnp.where(blk_exp[:, None] == ids[None, :], nxt_tab[None, :], 0), axis=1)
    e = route_t[:, R_E:R_E + TOP_K, :].astype(jnp.int32)
    rank = route_t[:, R_RANK:R_RANK + TOP_K, :].astype(jnp.int32)
    seg = jnp.sum(jnp.where(e[..., None] == jnp.arange(N_EXPERTS), pstart, 0), axis=-1)
    dest = jnp.clip(seg + rank, 0, n_steps * MOE_BLK - 1)
    dest_tiles = dest.reshape(T // TM, 1, TOP_K * TM)
    return dict(dest_tiles=dest_tiles, fill_start=last_blk.astype(jnp.int32),
                blk_exp=blk_exp.astype(jnp.int32), nxt_exp=nxt_exp.astype(jnp.int32),
                n_used=n_used.reshape(1).astype(jnp.int32), n_rows=n_rows)


def _prep_layer(l, w_gk_up, b_gk, gla_norm, gmlp_norm, w_spatial, b_spatial, w_conv,
                w_router_group, b_router_group, w_router_expert, b_router_expert):
    wgk = jnp.concatenate(
        [w_gk_up[l], jnp.zeros((LANES - GLA_GATE_RANK, GLA_KDIM), F32)], axis=0).astype(BF16)
    wsp = w_spatial[l].transpose(1, 0, 2).reshape(GMLP_CHUNK, GMLP_HEADS * GMLP_CHUNK)
    bsp = jnp.repeat(b_spatial[l].T, GMLP_DH, axis=1)
    wconv = jnp.concatenate([w_conv[l], jnp.zeros((8 - CONV_K, CONV_WIDTH), F32)], axis=0)
    wr = jnp.concatenate(
        [w_router_group[l], w_router_expert[l],
         jnp.zeros((D_MODEL, ROUTER_COLS - N_GROUPS - N_EXPERTS), F32)], axis=1)
    wr_hi = wr.astype(BF16)
    wr_lo = (wr - wr_hi.astype(F32)).astype(BF16)
    br = jnp.concatenate(
        [b_router_group[l], b_router_expert[l],
         jnp.zeros((ROUTER_COLS - N_GROUPS - N_EXPERTS,), F32)])[None, :]
    return dict(
        wgk=wgk, bgk=b_gk[l][None, :], glan=gla_norm[l][None, :], gmn=gmlp_norm[l][None, :],
        wsp=wsp, bsp=bsp, wconv=wconv, wr_cat=jnp.concatenate([wr_hi, wr_lo], axis=1), br=br)


def kernel(x, attn_norm, w_in, w_gk_up, b_gk, gla_norm, gmlp_norm, w_spatial, b_spatial, w_conv, w_out, ffn_norm, w_router_group, b_router_group, w_router_expert, b_router_expert, w_gate, w_up, w_down, final_norm):
    B, S, D = x.shape
    T = B * S
    depth = w_in.shape[0]
    xr = x.reshape(T, D)
    w_in_t = jnp.swapaxes(w_in, 1, 2)
    moe = None
    for l in range(depth):
        p = _prep_layer(l, w_gk_up, b_gk, gla_norm, gmlp_norm, w_spatial, b_spatial, w_conv,
                        w_router_group, b_router_group, w_router_expert, b_router_expert)
        mix_params = (p["wgk"], p["bgk"], p["glan"], p["gmn"], p["wsp"], p["bsp"], p["wconv"])
        if moe is None:
            mixed = _front(xr, attn_norm[l][None, :], w_in_t, l, mix_params, B, S)
        else:
            xr, mixed = _front(moe["x2"], attn_norm[l][None, :], w_in_t, l, mix_params, B, S, moe)
        x2, h2, route, route_t, counts_rec = _out_router(
            mixed, xr, w_out, l, ffn_norm[l][None, :], p["wr_cat"], p["br"])
        moe = _dispatch_tables(route_t, counts_rec, T)
        xb = _dispatch(moe["fill_start"], moe["n_used"], moe["dest_tiles"], h2, moe["n_rows"])
        yb = _experts(moe["blk_exp"], moe["nxt_exp"], moe["n_used"], xb, w_gate, w_up, w_down, l)
        moe.update(x2=x2, route=route, yb=yb)
    out = _combine_final_norm(moe["dest_tiles"], moe["x2"], moe["route"], moe["yb"],
                              final_norm[None, :])
    return out.reshape(B, S, D)
```

```python
import functools

import jax
import jax.numpy as jnp
from jax import lax
from jax.experimental import pallas as pl
from jax.experimental.pallas import tpu as pltpu

F32 = jnp.float32
BF16 = jnp.bfloat16

D_MODEL = 1024
RMS_EPS = 1e-6
GLA_HEADS = 4
GLA_WIDTH = 512
GLA_DV = 128
GLA_DK = 64
GLA_KDIM = 256
GLA_GATE_RANK = 16
GLA_GATE_NORM = 16.0
GLA_CHUNK = 64
GMLP_HEADS = 4
GMLP_WIDTH = 256
GMLP_DH = 64
GMLP_CHUNK = 128
CONV_WIDTH = 256
CONV_K = 3
N_GROUPS = 4
EXPERTS_PER_GROUP = 8
N_EXPERTS = 32
TOP_K = 2
D_EXPERT = 256

LANES = 128
C_Q, C_K, C_V, C_G = 0, 256, 512, 1024
C_U, C_VG, C_X, C_BG, C_CG, C_GKL = 1536, 1792, 2048, 2304, 2560, 2816
D_PROJ = C_GKL + LANES
D_IN = C_GKL + GLA_GATE_RANK

TM = 256
TS_MIX = TM
MOE_BLK = 256
ROUTER_COLS = LANES
SUBLANES = 8
ROW_TILES = D_MODEL // LANES
assert ROW_TILES == SUBLANES
VMEM_LIMIT = 56 * 1024 * 1024
R_E, R_RANK, R_W = 0, 2, 4


def _dot(a, b):
    return jnp.dot(a, b, preferred_element_type=F32)


def _split_bf16(x):
    hi = x.astype(BF16)
    lo = (x - hi.astype(F32)).astype(BF16)
    return hi, lo


def _rms(x, gain):
    return x * lax.rsqrt(jnp.mean(x * x, axis=-1, keepdims=True) + RMS_EPS) * gain


W_PREP_ROWS = 128
PROJ_CHUNK = 256


def _stage_w_in(wt_ref, wb_ref):
    for c0 in range(0, C_GKL, LANES):
        src = c0 if c0 < C_U else c0 + GLA_GATE_RANK
        wb_ref[:, c0:c0 + LANES] = wt_ref[0, src:src + LANES, :].T.astype(BF16)
    low = jnp.concatenate([wt_ref[0, C_U:C_U + GLA_GATE_RANK, :],
                           jnp.zeros((LANES - GLA_GATE_RANK, D_MODEL), F32)], axis=0)
    wb_ref[:, C_GKL:D_PROJ] = low.T.astype(BF16)


def _row_gather_copy(yb_ref, buf_ref, sem_ref, slot, k, r, d):
    return pltpu.make_async_copy(yb_ref.at[d], buf_ref.at[slot, k, r], sem_ref.at[slot])


def _gather_start(dest_ref, yb_ref, buf_ref, sem_ref, slot, rows=range(TM)):
    for r in rows:
        for k in range(TOP_K):
            _row_gather_copy(yb_ref, buf_ref, sem_ref, slot, k, r,
                             dest_ref[0, 0, k * TM + r]).start(priority=k)


def _gather_wait(yb_ref, buf_ref, sem_ref, slot):
    for k in range(TOP_K):
        pltpu.make_async_copy(yb_ref.at[pl.ds(0, TM)], buf_ref.at[slot, k], sem_ref.at[slot]).wait()


def _combined_residual(dcur_ref, x_ref, route_ref, yb_ref, buf_ref, sem_ref):
    i = pl.program_id(0)
    slot = lax.rem(i, 2)

    @pl.when(i == 0)
    def _():
        _gather_start(dcur_ref, yb_ref, buf_ref, sem_ref, 0)

    _gather_wait(yb_ref, buf_ref, sem_ref, slot)
    w0 = route_ref[:, R_W:R_W + 1]
    w1 = route_ref[:, R_W + 1:R_W + 2]
    y0 = _row_tiles_chunks(buf_ref.at[slot, 0], TM)
    y1 = _row_tiles_chunks(buf_ref.at[slot, 1], TM)
    return jnp.concatenate(
        [x_ref[:, c * LANES:(c + 1) * LANES] + (w0 * y0[c] + w1 * y1[c]) for c in range(ROW_TILES)],
        axis=1)


def _prefetch_groups(n_groups):
    per = -(-TM // n_groups)
    return [range(g * per, min(TM, (g + 1) * per)) for g in range(n_groups)]


def _drain_last_prefetch(yb_ref, buf_ref, sem_ref):
    i = pl.program_id(0)

    @pl.when(i == pl.num_programs(0) - 1)
    def _():
        _gather_wait(yb_ref, buf_ref, sem_ref, 1 - lax.rem(i, 2))


def _combine_specs(n_tiles):
    smem_tile = lambda f: pl.BlockSpec((1, 1, TOP_K * TM), f, memory_space=pltpu.SMEM)
    return [
        smem_tile(lambda i: (i, 0, 0)),
        smem_tile(lambda i: (jnp.minimum(i + 1, n_tiles - 1), 0, 0)),
        pl.BlockSpec((TM, D_MODEL), lambda i: (i, 0)),
        pl.BlockSpec((TM, LANES), lambda i: (i, 0)),
        pl.BlockSpec(memory_space=pl.ANY),
    ]


_COMBINE_SCRATCH = [pltpu.VMEM((2, TOP_K, TM, ROW_TILES, LANES), F32),
                    pltpu.SemaphoreType.DMA((2,))]


def _combine_final_norm_kernel(dcur_ref, dnxt_ref, x_ref, route_ref, yb_ref, gain_ref,
                               o_ref, buf_ref, sem_ref):
    _gather_start(dnxt_ref, yb_ref, buf_ref, sem_ref, 1 - lax.rem(pl.program_id(0), 2))
    x = _combined_residual(dcur_ref, x_ref, route_ref, yb_ref, buf_ref, sem_ref)
    o_ref[...] = _rms(x, gain_ref[...])
    _drain_last_prefetch(yb_ref, buf_ref, sem_ref)


def _combine_final_norm(dest_tiles, x2, route, yb, gain):
    T = x2.shape[0]
    n_tiles = T // TM
    return pl.pallas_call(
        _combine_final_norm_kernel,
        grid=(n_tiles,),
        in_specs=_combine_specs(n_tiles) + [pl.BlockSpec((1, D_MODEL), lambda i: (0, 0))],
        out_specs=pl.BlockSpec((TM, D_MODEL), lambda i: (i, 0)),
        out_shape=jax.ShapeDtypeStruct((T, D_MODEL), F32),
        scratch_shapes=_COMBINE_SCRATCH,
        compiler_params=pltpu.CompilerParams(
            dimension_semantics=("arbitrary",), vmem_limit_bytes=VMEM_LIMIT),
        name="combine_final_norm",
    )(dest_tiles, dest_tiles, x2, route, yb, gain)


def _gelu_tanh(x):
    c = 0.7978845608028654
    return x * (0.5 * (1.0 + jnp.tanh(c * (x + 0.044715 * (x * x * x)))))


def _mixer_kernel(proj_ref, wgk_ref, bgk_ref, glan_ref, gmn_ref, wsp_ref, bsp_ref, wconv_ref,
                  out_ref, st_ref, hc_ref, lcat_ref, wm_ref, *, seq_start, first_step, between):
    TS = TS_MIX
    n_gla = TS // GLA_CHUNK
    n_gm = TS // GMLP_CHUNK

    @pl.when(seq_start)
    def _():
        st_ref[...] = jnp.zeros_like(st_ref)
        hc_ref[...] = jnp.zeros_like(hc_ref)

    @pl.when(first_step)
    def _():
        r = lax.broadcasted_iota(jnp.int32, (TS, TS), 0)
        c = lax.broadcasted_iota(jnp.int32, (TS, TS), 1)
        keep = ((r // GLA_CHUNK) == (c // GLA_CHUNK)) & (c <= r)
        lcat_ref[...] = jnp.where(keep, 1.0, 0.0).astype(BF16)
        t = lax.broadcasted_iota(jnp.int32, (GMLP_CHUNK, GMLP_HEADS * GMLP_CHUNK), 0)
        s = lax.broadcasted_iota(jnp.int32, (GMLP_CHUNK, GMLP_HEADS * GMLP_CHUNK), 1) % GMLP_CHUNK
        wm_ref[...] = jnp.where(s <= t, wsp_ref[...], 0.0).astype(BF16)

    lane256 = lax.broadcasted_iota(jnp.int32, (1, GLA_KDIM), 1)

    q = proj_ref[:, C_Q:C_Q + GLA_KDIM].astype(F32)
    k = proj_ref[:, C_K:C_K + GLA_KDIM].astype(F32)
    v_b = proj_ref[:, C_V:C_V + GLA_WIDTH]
    z = _dot(proj_ref[:, C_GKL:C_GKL + LANES], wgk_ref[...]) + bgk_ref[...]
    gk = (jnp.minimum(z, 0.0) - jnp.log1p(jnp.exp(-jnp.abs(z)))) * (1.0 / GLA_GATE_NORM)
    gk_hi, gk_lo = _split_bf16(gk)
    cs = _dot(lcat_ref[...], jnp.concatenate([gk_hi, gk_lo], axis=1))
    b = cs[:, :GLA_KDIM] + cs[:, GLA_KDIM:]
    b_last = [b[(c + 1) * GLA_CHUNK - 1:(c + 1) * GLA_CHUNK, :] for c in range(n_gla)]
    bl = jnp.concatenate(
        [jnp.broadcast_to(t, (GLA_CHUNK, GLA_KDIM)) for t in b_last], axis=0)
    q_dec = (q * (GLA_DK ** -0.5)) * jnp.exp(b)
    k_inv = (k * jnp.exp(-b)).astype(BF16)
    k_dec = (k * jnp.exp(bl - b)).astype(BF16)
    q_dec_b = q_dec.astype(BF16)

    zero_b = jnp.zeros_like(q_dec_b)
    q_stack = jnp.concatenate(
        [jnp.where((lane256 // GLA_DK) == h, q_dec_b, zero_b) for h in range(GLA_HEADS)], axis=0)
    scores = lax.dot_general(q_stack, k_inv, (((1,), (1,)), ((), ())),
                             preferred_element_type=F32)
    rt = lax.broadcasted_iota(jnp.int32, (TS, TS), 0)
    ct = lax.broadcasted_iota(jnp.int32, (TS, TS), 1)
    causal = ((rt // GLA_CHUNK) == (ct // GLA_CHUNK)) & (ct <= rt)
    o_heads = []
    for h in range(GLA_HEADS):
        p_h = jnp.where(causal, scores[h * TS:(h + 1) * TS, :], 0.0).astype(BF16)
        o_heads.append(_dot(p_h, v_b[:, h * GLA_DV:(h + 1) * GLA_DV]))

    sr = lax.broadcasted_iota(jnp.int32, (GLA_WIDTH, GLA_KDIM), 0) // GLA_DV
    sc = lax.broadcasted_iota(jnp.int32, (GLA_WIDTH, GLA_KDIM), 1) // GLA_DK
    bd_mask = sr == sc
    o_inter = []
    for c in range(n_gla):
        rows = slice(c * GLA_CHUNK, (c + 1) * GLA_CHUNK)
        st = st_ref[...]
        o_inter.append(lax.dot_general(q_dec_b[rows], st.astype(BF16), (((1,), (1,)), ((), ())),
                                       preferred_element_type=F32))
        upd = lax.dot_general(v_b[rows], k_dec[rows], (((0,), (0,)), ((), ())),
                              preferred_element_type=F32)
        decay = jnp.exp(b_last[c])
        st_ref[...] = st * decay + jnp.where(bd_mask, upd, 0.0)
    o_inter = jnp.concatenate(o_inter, axis=0)

    for h in range(GLA_HEADS):
        cols = slice(h * GLA_DV, (h + 1) * GLA_DV)
        o = o_heads[h] + o_inter[:, cols]
        o = o * lax.rsqrt(jnp.mean(o * o, axis=-1, keepdims=True) + RMS_EPS) * glan_ref[...]
        g = proj_ref[:, C_G + h * GLA_DV:C_G + (h + 1) * GLA_DV].astype(F32)
        out_ref[:, cols] = (o * (g * (1.0 / (1.0 + jnp.exp(-g))))).astype(out_ref.dtype)

    between("gla_done")
    u = _gelu_tanh(proj_ref[:, C_U:C_U + GMLP_WIDTH].astype(F32))
    vg = _gelu_tanh(proj_ref[:, C_VG:C_VG + GMLP_WIDTH].astype(F32))
    hr = lax.broadcasted_iota(jnp.int32, (GMLP_WIDTH, GMLP_WIDTH), 0) // GMLP_DH
    hcn = lax.broadcasted_iota(jnp.int32, (GMLP_WIDTH, GMLP_WIDTH), 1) // GMLP_DH
    head_mean = jnp.where(hr == hcn, 1.0 / GMLP_DH, 0.0).astype(BF16)
    sq_hi, sq_lo = _split_bf16(vg * vg)
    ms = _dot(sq_hi, head_mean) + _dot(sq_lo, head_mean)
    v32 = vg * lax.rsqrt(ms + RMS_EPS) * gmn_ref[...]
    for c in range(n_gm):
        rows = slice(c * GMLP_CHUNK, (c + 1) * GMLP_CHUNK)
        vc = v32[rows].astype(BF16)
        zc = jnp.zeros_like(vc)
        rhs = jnp.concatenate(
            [jnp.where((lane256 // GMLP_DH) == h, vc, zc) for h in range(GMLP_HEADS)], axis=0)
        mixed = _dot(wm_ref[...], rhs) + bsp_ref[...]
        out_ref[rows, GLA_WIDTH:GLA_WIDTH + GMLP_WIDTH] = (u[rows] * mixed).astype(out_ref.dtype)

    between("gmlp_done")
    hcv = (proj_ref[:, C_CG:C_CG + CONV_WIDTH].astype(F32)
           * proj_ref[:, C_X:C_X + CONV_WIDTH].astype(F32))
    hc_ref[8:8 + TS, :] = hcv
    y = (wconv_ref[2:3, :] * hcv + wconv_ref[1:2, :] * hc_ref[7:7 + TS, :]
         + wconv_ref[0:1, :] * hc_ref[6:6 + TS, :])
    out_ref[:, GLA_WIDTH + GMLP_WIDTH:] = (
        proj_ref[:, C_BG:C_BG + CONV_WIDTH].astype(F32) * y).astype(out_ref.dtype)
    hc_ref[0:8, :] = hc_ref[TS:TS + 8, :]


_MIXER_SCRATCH = [
    pltpu.VMEM((GLA_WIDTH, GLA_KDIM), F32),
    pltpu.VMEM((TS_MIX + 8, CONV_WIDTH), F32),
    pltpu.VMEM((TS_MIX, TS_MIX), BF16),
    pltpu.VMEM((GMLP_CHUNK, GMLP_HEADS * GMLP_CHUNK), BF16),
]


N_MIX_PARAMS = 7
PROJ_CHUNKS_AT = {"gla_done": 4, "gmlp_done": 4}


def _front_kernel(*refs, tiles_per_seq, combine):
    refs = list(refs)
    if combine:
        dcur_ref, dnxt_ref, x_ref, route_ref, yb_ref = refs[:5]
        del refs[:5]
    else:
        x_ref = refs.pop(0)
    gain_ref, wt_ref = refs[:2]
    mix_refs = refs[2:2 + N_MIX_PARAMS]
    del refs[:2 + N_MIX_PARAMS]
    if combine:
        xo_ref, out_ref, buf_ref, sem_ref = refs[:4]
        del refs[:4]
    else:
        out_ref = refs.pop(0)
    st_ref, hc_ref, lcat_ref, wm_ref, wb_ref, pcur_ref, pnext_ref = refs
    s = pl.program_id(0)

    @pl.when(s == 0)
    def _():
        _stage_w_in(wt_ref, wb_ref)
        pcur_ref[...] = jnp.zeros_like(pcur_ref)

    col_chunks = [(c0, min(c0 + PROJ_CHUNK, D_PROJ)) for c0 in range(0, D_PROJ, PROJ_CHUNK)]
    work = list(zip(_prefetch_groups(len(col_chunks)), col_chunks))

    def project(h, n):
        for _ in range(min(n, len(work))):
            rows, (c0, c1) = work.pop(0)
            if combine:
                _gather_start(dnxt_ref, yb_ref, buf_ref, sem_ref, 1 - lax.rem(s, 2), rows)
            pnext_ref[:, c0:c1] = _dot(h, wb_ref[:, c0:c1]).astype(BF16)

    mixers = functools.partial(
        _mixer_kernel, pcur_ref, *mix_refs, out_ref, st_ref, hc_ref, lcat_ref, wm_ref,
        seq_start=lax.rem(jnp.maximum(s - 1, 0), tiles_per_seq) == 0, first_step=s == 0)
    if combine:
        x = _combined_residual(dcur_ref, x_ref, route_ref, yb_ref, buf_ref, sem_ref)
        xo_ref[...] = x
        h = _rms(x, gain_ref[...]).astype(BF16)
        mixers(between=lambda site: project(h, PROJ_CHUNKS_AT[site]))
        project(h, len(work))
        _drain_last_prefetch(yb_ref, buf_ref, sem_ref)
    else:
        mixers(between=lambda site: None)
        project(_rms(x_ref[...], gain_ref[...]).astype(BF16), len(work))
    pcur_ref[...] = pnext_ref[...]


def _front(x, gain, w_in_t, layer, mix_params, batch, seq, moe=None):
    n_seq = seq // TS_MIX
    n = batch * n_seq
    T = batch * seq
    cur = lambda s: jnp.minimum(s, n - 1)
    full = lambda shape: pl.BlockSpec(shape, lambda s: (0,) * len(shape))
    row = lambda w, f: pl.BlockSpec((TM, w), lambda s: (f(s), 0))
    in_specs, args = [row(D_MODEL, cur)], [x]
    out_specs = [row(D_MODEL, lambda s: jnp.maximum(s - 1, 0))]
    out_shape = [jax.ShapeDtypeStruct((T, D_MODEL), BF16)]
    scratch = list(_MIXER_SCRATCH)
    if moe is not None:
        smem_tile = lambda f: pl.BlockSpec((1, 1, TOP_K * TM), lambda s: (f(s), 0, 0),
                                           memory_space=pltpu.SMEM)
        in_specs = [smem_tile(cur), smem_tile(lambda s: jnp.minimum(s + 1, n - 1))] + in_specs + [
            row(LANES, cur), pl.BlockSpec(memory_space=pl.ANY)]
        args = [moe["dest_tiles"], moe["dest_tiles"]] + args + [moe["route"], moe["yb"]]
        out_specs = [row(D_MODEL, cur)] + out_specs
        out_shape = [jax.ShapeDtypeStruct((T, D_MODEL), F32)] + out_shape
        scratch = _COMBINE_SCRATCH + scratch
    in_specs += [
        full((1, D_MODEL)),
        pl.BlockSpec((1, D_IN, D_MODEL), lambda s: (layer, 0, 0), pipeline_mode=pl.Buffered(1)),
        full((LANES, GLA_KDIM)), full((1, GLA_KDIM)), full((1, GLA_DV)), full((1, GMLP_WIDTH)),
        full((GMLP_CHUNK, GMLP_HEADS * GMLP_CHUNK)), full((GMLP_CHUNK, GMLP_WIDTH)),
        full((8, CONV_WIDTH)),
    ]
    scratch += [pltpu.VMEM((D_MODEL, D_PROJ), BF16), pltpu.VMEM((TS_MIX, D_PROJ), BF16),
                pltpu.VMEM((TS_MIX, D_PROJ), BF16)]
    return pl.pallas_call(
        functools.partial(_front_kernel, tiles_per_seq=n_seq, combine=moe is not None),
        grid=(n + 1,),
        in_specs=in_specs,
        out_specs=out_specs if moe is not None else out_specs[0],
        out_shape=out_shape if moe is not None else out_shape[0],
        scratch_shapes=scratch,
        compiler_params=pltpu.CompilerParams(
            dimension_semantics=("arbitrary",), vmem_limit_bytes=VMEM_LIMIT),
        name="front",
    )(*args, gain, w_in_t, *mix_params)


def _row_tiles_store(tiles_ref, x):
    rows = x.shape[0]
    flat = tiles_ref.reshape(rows * ROW_TILES, LANES)
    for c in range(ROW_TILES):
        flat[pl.ds(c, rows, stride=ROW_TILES), :] = x[:, c * LANES:(c + 1) * LANES]


def _row_tiles_chunks(tiles_ref, rows):
    flat = tiles_ref.reshape(rows * ROW_TILES, LANES)
    return [flat[pl.ds(c, rows, stride=ROW_TILES), :] for c in range(ROW_TILES)]


IN_SLOTS = 3


def _out_router_kernel(mix_hbm, x_hbm, wo_ref, gain_ref, wrc_ref, br_ref,
                       x2_ref, h2_ref, route_ref, route_t_ref, cnt_ref, tri_ref, wob_ref, lg_ref,
                       mixbuf_ref, xbuf_ref, insem_ref):
    i = pl.program_id(0)
    n = pl.num_programs(0) - 1

    def in_copies(t):
        slot = lax.rem(t, IN_SLOTS)
        rows = pl.ds(t * TM, TM)
        return [pltpu.make_async_copy(mix_hbm.at[rows], mixbuf_ref.at[slot], insem_ref.at[slot]),
                pltpu.make_async_copy(x_hbm.at[rows], xbuf_ref.at[slot], insem_ref.at[slot])]

    @pl.when(i == 0)
    def _():
        for t in range(IN_SLOTS - 1):
            for c in in_copies(t):
                c.start()

    @pl.when(i + IN_SLOTS - 1 < n)
    def _():
        for c in in_copies(i + IN_SLOTS - 1):
            c.start()

    @pl.when(i < n)
    def _():
        for c in in_copies(i):
            c.wait()

    tile_slot = lax.rem(jnp.minimum(i, n - 1), IN_SLOTS)
    mix_ref = mixbuf_ref.at[tile_slot]
    x_ref = xbuf_ref.at[tile_slot]

    @pl.when(i == 0)
    def _():
        cnt_ref[...] = jnp.zeros_like(cnt_ref)
        lg_ref[...] = jnp.zeros_like(lg_ref)
        r = lax.broadcasted_iota(jnp.int32, (TM, TM), 0)
        c = lax.broadcasted_iota(jnp.int32, (TM, TM), 1)
        tri_ref[...] = jnp.where(c < r, 1.0, 0.0).astype(BF16)
        for r0 in range(0, D_MODEL, W_PREP_ROWS):
            wob_ref[r0:r0 + W_PREP_ROWS, :] = wo_ref[0, r0:r0 + W_PREP_ROWS, :].astype(BF16)

    lg = lg_ref[...]
    half = D_MODEL // 2
    mix = mix_ref[...]
    x2_a = x_ref[:, :half] + _dot(mix, wob_ref[:, :half])

    lane = lax.broadcasted_iota(jnp.int32, (TM, LANES), 1).astype(F32)
    neg = -jnp.inf
    is_g = lane < N_GROUPS
    gl = jnp.where(is_g, lg, neg)
    gmax = jnp.max(gl, axis=1, keepdims=True)
    g_top = jnp.min(jnp.where(gl == gmax, lane, float(LANES)), axis=1, keepdims=True)
    g_w = 1.0 / jnp.sum(jnp.where(is_g, jnp.exp(lg - gmax), 0.0), axis=1, keepdims=True)
    first = N_GROUPS + EXPERTS_PER_GROUP * g_top
    el = jnp.where((lane >= first) & (lane < first + EXPERTS_PER_GROUP), lg, neg)
    m1 = jnp.max(el, axis=1, keepdims=True)
    i1 = jnp.min(jnp.where(el == m1, lane, float(LANES)), axis=1, keepdims=True)
    el2 = jnp.where(lane == i1, neg, el)
    m2 = jnp.max(el2, axis=1, keepdims=True)
    i2 = jnp.min(jnp.where(el2 == m2, lane, float(LANES)), axis=1, keepdims=True)
    ratio = jnp.exp(m2 - m1)
    w1 = g_w / (1.0 + ratio)
    w2 = w1 * ratio

    x2_b = x_ref[:, half:] + _dot(mix, wob_ref[:, half:])

    oh1 = jnp.where(lane == i1, 1.0, 0.0)
    oh2 = jnp.where(lane == i2, 1.0, 0.0)
    oh = jnp.where(i > 0, oh1 + oh2, 0.0)
    before = _dot(tri_ref[...], oh.astype(BF16)) + cnt_ref[0:1, :]
    rank1 = jnp.sum(oh1 * before, axis=1, keepdims=True)
    rank2 = jnp.sum(oh2 * before, axis=1, keepdims=True)
    cnt_ref[...] = cnt_ref[...] + jnp.sum(oh, axis=0, keepdims=True)

    rec = jnp.zeros((TM, LANES), F32)
    for col, val in ((R_E, i1 - N_GROUPS), (R_E + 1, i2 - N_GROUPS), (R_RANK, rank1),
                     (R_RANK + 1, rank2), (R_W, w1), (R_W + 1, w2)):
        rec = jnp.where(lane == col, val, rec)
    route_ref[...] = rec
    route_t_ref[0] = rec.T[0:SUBLANES, :]

    x2 = jnp.concatenate([x2_a, x2_b], axis=1)
    x2_ref[...] = x2
    h = _rms(x2, gain_ref[...])
    h_hi, h_lo = _split_bf16(h)
    h2_ref[...] = h_hi
    hh_hl = _dot(h_hi, wrc_ref[...])
    lg_ref[...] = (hh_hl[:, :ROUTER_COLS] + hh_hl[:, ROUTER_COLS:]
                   + _dot(h_lo, wrc_ref[:, :ROUTER_COLS]) + br_ref[...])


def _out_router(mixed, x, w_out, layer, gain, wr_cat, br):
    T = x.shape[0]
    n = T // TM
    row = lambda w: pl.BlockSpec((TM, w), lambda i: (jnp.minimum(i, n - 1), 0))
    lag = lambda i: jnp.maximum(i - 1, 0)
    full = lambda shape: pl.BlockSpec(shape, lambda i: (0, 0))
    wo_spec = pl.BlockSpec((1, D_MODEL, D_MODEL), lambda i: (layer, 0, 0),
                           pipeline_mode=pl.Buffered(1))
    return pl.pallas_call(
        _out_router_kernel,
        grid=(n + 1,),
        in_specs=[pl.BlockSpec(memory_space=pl.ANY), pl.BlockSpec(memory_space=pl.ANY), wo_spec,
                  full((1, D_MODEL)), full((D_MODEL, 2 * ROUTER_COLS)), full((1, ROUTER_COLS))],
        out_specs=[row(D_MODEL), row(D_MODEL),
                   pl.BlockSpec((TM, LANES), lambda i: (lag(i), 0)),
                   pl.BlockSpec((1, SUBLANES, TM), lambda i: (lag(i), 0, 0)), full((8, LANES))],
        out_shape=[jax.ShapeDtypeStruct((T, D_MODEL), F32),
                   jax.ShapeDtypeStruct((T, D_MODEL), BF16),
                   jax.ShapeDtypeStruct((T, LANES), F32),
                   jax.ShapeDtypeStruct((T // TM, SUBLANES, TM), F32),
                   jax.ShapeDtypeStruct((8, LANES), F32)],
        scratch_shapes=[pltpu.VMEM((TM, TM), BF16), pltpu.VMEM((D_MODEL, D_MODEL), BF16),
                        pltpu.VMEM((TM, ROUTER_COLS), F32),
                        pltpu.VMEM((IN_SLOTS, TM, D_MODEL), BF16),
                        pltpu.VMEM((IN_SLOTS, TM, D_MODEL), F32),
                        pltpu.SemaphoreType.DMA((IN_SLOTS,))],
        compiler_params=pltpu.CompilerParams(
            dimension_semantics=("arbitrary",), vmem_limit_bytes=VMEM_LIMIT),
        name="out_router",
    )(mixed, x, w_out, gain, wr_cat, br)


def _dispatch_kernel(fill_ref, nu_ref, dest_ref, h_ref, xb_ref, zero_ref, sem_ref, zsem_ref,
                     stage_ref):
    i = pl.program_id(0)
    par = lax.rem(i, 2)
    last = i == pl.num_programs(0) - 1
    n_blocks = xb_ref.shape[0] // MOE_BLK
    spare_fills = [(j >= nu_ref[0], pltpu.make_async_copy(
        zero_ref, xb_ref.at[pl.ds(j * MOE_BLK, MOE_BLK)], zsem_ref.at[1]))
        for j in range(n_blocks - N_EXPERTS, n_blocks)]

    @pl.when(i == 0)
    def _():
        zero_ref[...] = jnp.zeros_like(zero_ref)
        fills = [(fill_ref[e] >= 0, pltpu.make_async_copy(
            zero_ref, xb_ref.at[pl.ds(pl.multiple_of(jnp.maximum(fill_ref[e], 0), MOE_BLK), MOE_BLK)],
            zsem_ref.at[0])) for e in range(N_EXPERTS)]
        for cond, f in fills + spare_fills:
            pl.when(cond)(f.start)
        for cond, f in fills:
            pl.when(cond)(f.wait)

    _row_tiles_store(stage_ref.at[par], h_ref[...].astype(F32))
    for r in range(TM):
        for k in range(TOP_K):
            pltpu.make_async_copy(stage_ref.at[par, r], xb_ref.at[dest_ref[0, 0, k * TM + r]],
                                  sem_ref.at[par]).start(priority=k)

    def wait_tile(p):
        for _ in range(TOP_K):
            pltpu.make_async_copy(stage_ref.at[p], xb_ref.at[pl.ds(0, TM)], sem_ref.at[p]).wait()

    pl.when(i > 0)(lambda: wait_tile(1 - par))
    @pl.when(last)
    def _():
        wait_tile(par)
        for cond, f in spare_fills:
            pl.when(cond)(f.wait)


def _dispatch(fill_start, n_used, dest_tiles, h2, n_rows):
    T = h2.shape[0]
    grid_spec = pltpu.PrefetchScalarGridSpec(
        num_scalar_prefetch=2,
        grid=(T // TM,),
        in_specs=[
            pl.BlockSpec((1, 1, TOP_K * TM), lambda i, fs, nu: (i, 0, 0), memory_space=pltpu.SMEM),
            pl.BlockSpec((TM, D_MODEL), lambda i, fs, nu: (i, 0)),
        ],
        out_specs=pl.BlockSpec(memory_space=pl.ANY),
        scratch_shapes=[pltpu.VMEM((MOE_BLK, ROW_TILES, LANES), F32),
                        pltpu.SemaphoreType.DMA((2,)), pltpu.SemaphoreType.DMA((2,)),
                        pltpu.VMEM((2, TM, ROW_TILES, LANES), F32)],
    )
    return pl.pallas_call(
        _dispatch_kernel,
        grid_spec=grid_spec,
        out_shape=jax.ShapeDtypeStruct((n_rows, ROW_TILES, LANES), F32),
        compiler_params=pltpu.CompilerParams(dimension_semantics=("arbitrary",)),
        name="dispatch",
    )(fill_start, n_used, dest_tiles, h2)


BLOCK_COPY_PARTS = 4
X_SLOTS = 4
Y_SLOTS = 3


class _CopyGroup:
    def __init__(self, copies):
        self.copies = copies

    def start(self):
        for n, c in enumerate(self.copies):
            c.start(priority=n % 2)

    def wait(self):
        for c in self.copies:
            c.wait()


def _expert_kernel(be_ref, nxt_ref, nu_ref, half_ref, xb_ref, wg_ref, wu_ref, wd_ref, yb_ref,
                   xbuf_ref, ybuf_ref, wgs_ref, wus_ref, wds_ref, wgb_ref, wub_ref, wdb_ref,
                   xsem_ref, ysem_ref, wsem_ref, zbuf_ref, zsem_ref, *, layer, n_blocks):
    n_used = nu_ref[0]

    part = MOE_BLK // BLOCK_COPY_PARTS

    def x_copy(j, slot):
        return _CopyGroup([pltpu.make_async_copy(
            xb_ref.at[pl.ds(j * MOE_BLK + p * part, part)],
            xbuf_ref.at[slot, pl.ds(p * part, part)], xsem_ref.at[slot])
            for p in range(BLOCK_COPY_PARTS)])

    def y_copy(j, slot):
        return _CopyGroup([pltpu.make_async_copy(
            ybuf_ref.at[slot, pl.ds(p * part, part)],
            yb_ref.at[pl.ds(j * MOE_BLK + p * part, part)], ysem_ref.at[slot])
            for p in range(BLOCK_COPY_PARTS)])

    def w_copies(e, ws):
        return [pltpu.make_async_copy(src.at[layer, e], dst.at[ws], wsem_ref.at[ws])
                for src, dst in ((wg_ref, wgs_ref), (wu_ref, wus_ref), (wd_ref, wds_ref))]

    def fill_copy(j):
        return pltpu.make_async_copy(zbuf_ref, yb_ref.at[pl.ds(j * MOE_BLK, MOE_BLK)], zsem_ref)

    for j0 in range(X_SLOTS - 1):
        x_copy(j0, j0).start()
    for c in w_copies(be_ref[0], 0):
        c.start()

    zbuf_ref[...] = jnp.zeros_like(zbuf_ref)

    def fill(j, carry):
        fill_copy(j).start()
        return carry

    lax.fori_loop(n_used, n_blocks, fill, 0)

    def block(j, ws):
        slot = lax.rem(j, Y_SLOTS)
        xslot = lax.rem(j, X_SLOTS)
        first = (j == 0) | (be_ref[j] != be_ref[jnp.maximum(j - 1, 0)])
        ws = jnp.where(first & (j > 0), 1 - ws, ws)

        @pl.when(first)
        def _():
            for c in w_copies(be_ref[j], ws):
                c.wait()
            for r0 in range(0, D_MODEL, W_PREP_ROWS):
                rows = slice(r0, r0 + W_PREP_ROWS)
                wgb_ref[rows, :] = wgs_ref[ws, rows, :].astype(BF16)
                wub_ref[rows, :] = wus_ref[ws, rows, :].astype(BF16)
            for r0 in range(0, D_EXPERT, W_PREP_ROWS):
                rows = slice(r0, r0 + W_PREP_ROWS)
                wdb_ref[rows, :] = wds_ref[ws, rows, :].astype(BF16)

            @pl.when(nxt_ref[j] >= 0)
            def _():
                for c in w_copies(nxt_ref[j], 1 - ws):
                    c.start()

        ahead = j + X_SLOTS - 1

        @pl.when(ahead < n_used)
        def _():
            x_copy(ahead, lax.rem(ahead, X_SLOTS)).start()

        x_copy(j, xslot).wait()

        @pl.when(j >= Y_SLOTS)
        def _():
            y_copy(j - Y_SLOTS, slot).wait()

        def mlp(rows):
            x = jnp.concatenate(
                [c.astype(BF16) for c in _row_tiles_chunks(xbuf_ref.at[xslot, pl.ds(0, rows)], rows)],
                axis=1)
            g = _dot(x, wgb_ref[...])
            u = _dot(x, wub_ref[...])
            h = (g * (1.0 / (1.0 + jnp.exp(-g)))) * u
            _row_tiles_store(ybuf_ref.at[slot, pl.ds(0, rows)], _dot(h.astype(BF16), wdb_ref[...]))

        half_full = half_ref[j] == 1

        @pl.when(half_full)
        def _():
            mlp(MOE_BLK // 2)
            ybuf_ref[slot, pl.ds(MOE_BLK // 2, MOE_BLK // 2)] = jnp.zeros(
                (MOE_BLK // 2, ROW_TILES, LANES), F32)

        @pl.when(jnp.logical_not(half_full))
        def _():
            mlp(MOE_BLK)

        y_copy(j, slot).start()
        return ws

    lax.fori_loop(0, n_used, block, jnp.int32(0))

    for back in range(Y_SLOTS, 0, -1):
        y_copy(n_used - back, lax.rem(n_used - back, Y_SLOTS)).wait()

    def fill_wait(j, carry):
        fill_copy(j).wait()
        return carry

    lax.fori_loop(n_used, n_blocks, fill_wait, 0)


def _experts(blk_exp, nxt_exp, n_used, blk_half, xb, w_gate, w_up, w_down, layer):
    n_blocks = blk_exp.shape[0]
    any_spec = pl.BlockSpec(memory_space=pl.ANY)
    blk = (MOE_BLK, ROW_TILES, LANES)
    grid_spec = pltpu.PrefetchScalarGridSpec(
        num_scalar_prefetch=4,
        grid=(1,),
        in_specs=[any_spec, any_spec, any_spec, any_spec],
        out_specs=any_spec,
        scratch_shapes=[
            pltpu.VMEM((X_SLOTS,) + blk, F32), pltpu.VMEM((Y_SLOTS,) + blk, F32),
            pltpu.VMEM((2, D_MODEL, D_EXPERT), F32), pltpu.VMEM((2, D_MODEL, D_EXPERT), F32),
            pltpu.VMEM((2, D_EXPERT, D_MODEL), F32),
            pltpu.VMEM((D_MODEL, D_EXPERT), BF16), pltpu.VMEM((D_MODEL, D_EXPERT), BF16),
            pltpu.VMEM((D_EXPERT, D_MODEL), BF16),
            pltpu.SemaphoreType.DMA((X_SLOTS,)), pltpu.SemaphoreType.DMA((Y_SLOTS,)),
            pltpu.SemaphoreType.DMA((2,)),
            pltpu.VMEM(blk, F32), pltpu.SemaphoreType.DMA(()),
        ],
    )
    return pl.pallas_call(
        functools.partial(_expert_kernel, layer=layer, n_blocks=n_blocks),
        grid_spec=grid_spec,
        out_shape=jax.ShapeDtypeStruct((n_blocks * MOE_BLK, ROW_TILES, LANES), F32),
        compiler_params=pltpu.CompilerParams(
            dimension_semantics=("arbitrary",), vmem_limit_bytes=VMEM_LIMIT),
        name="experts",
    )(blk_exp, nxt_exp, n_used, blk_half, xb, w_gate, w_up, w_down)


def _dispatch_tables(route_t, counts_rec, T):
    counts = counts_rec[0, N_GROUPS:N_GROUPS + N_EXPERTS].astype(jnp.int32)
    n_steps = (T * TOP_K) // MOE_BLK + N_EXPERTS
    nblk = (counts + MOE_BLK - 1) // MOE_BLK
    bend = jnp.cumsum(nblk)
    pstart = (bend - nblk) * MOE_BLK
    n_used = bend[-1]
    j = jnp.minimum(jnp.arange(n_steps, dtype=jnp.int32), n_used - 1)
    blk_exp = jnp.minimum(jnp.sum(j[:, None] >= bend[None, :], axis=1), N_EXPERTS - 1)
    n_rows = n_steps * MOE_BLK
    last_blk = jnp.where(counts > 0, (bend - 1) * MOE_BLK, -1)
    ids = jnp.arange(N_EXPERTS, dtype=jnp.int32)
    later = (ids[None, :] > ids[:, None]) & (nblk[None, :] > 0)
    nxt_of = jnp.min(jnp.where(later, ids[None, :], N_EXPERTS), axis=1)
    nxt_tab = jnp.where(nxt_of < N_EXPERTS, nxt_of, -1)
    own = blk_exp[:, None] == ids[None, :]
    nxt_exp = jnp.sum(jnp.where(own, nxt_tab[None, :], 0), axis=1)
    seg_end = jnp.sum(jnp.where(own, (pstart + counts)[None, :], 0), axis=1)
    blk_half = (seg_end - j * MOE_BLK <= MOE_BLK // 2).astype(jnp.int32)
    e = route_t[:, R_E:R_E + TOP_K, :].astype(jnp.int32)
    rank = route_t[:, R_RANK:R_RANK + TOP_K, :].astype(jnp.int32)
    seg = jnp.sum(jnp.where(e[..., None] == jnp.arange(N_EXPERTS), pstart, 0), axis=-1)
    dest = jnp.clip(seg + rank, 0, n_steps * MOE_BLK - 1)
    dest_tiles = dest.reshape(T // TM, 1, TOP_K * TM)
    return dict(dest_tiles=dest_tiles, fill_start=last_blk.astype(jnp.int32),
                blk_exp=blk_exp.astype(jnp.int32), nxt_exp=nxt_exp.astype(jnp.int32),
                blk_half=blk_half,
                n_used=n_used.reshape(1).astype(jnp.int32), n_rows=n_rows)


def _prep_layer(l, w_gk_up, b_gk, gla_norm, gmlp_norm, w_spatial, b_spatial, w_conv,
                w_router_group, b_router_group, w_router_expert, b_router_expert):
    wgk = jnp.concatenate(
        [w_gk_up[l], jnp.zeros((LANES - GLA_GATE_RANK, GLA_KDIM), F32)], axis=0).astype(BF16)
    wsp = w_spatial[l].transpose(1, 0, 2).reshape(GMLP_CHUNK, GMLP_HEADS * GMLP_CHUNK)
    bsp = jnp.repeat(b_spatial[l].T, GMLP_DH, axis=1)
    wconv = jnp.concatenate([w_conv[l], jnp.zeros((8 - CONV_K, CONV_WIDTH), F32)], axis=0)
    wr = jnp.concatenate(
        [w_router_group[l], w_router_expert[l],
         jnp.zeros((D_MODEL, ROUTER_COLS - N_GROUPS - N_EXPERTS), F32)], axis=1)
    wr_hi = wr.astype(BF16)
    wr_lo = (wr - wr_hi.astype(F32)).astype(BF16)
    br = jnp.concatenate(
        [b_router_group[l], b_router_expert[l],
         jnp.zeros((ROUTER_COLS - N_GROUPS - N_EXPERTS,), F32)])[None, :]
    return dict(
        wgk=wgk, bgk=b_gk[l][None, :], glan=gla_norm[l][None, :], gmn=gmlp_norm[l][None, :],
        wsp=wsp, bsp=bsp, wconv=wconv, wr_cat=jnp.concatenate([wr_hi, wr_lo], axis=1), br=br)


def kernel(x, attn_norm, w_in, w_gk_up, b_gk, gla_norm, gmlp_norm, w_spatial, b_spatial, w_conv, w_out, ffn_norm, w_router_group, b_router_group, w_router_expert, b_router_expert, w_gate, w_up, w_down, final_norm):
    B, S, D = x.shape
    T = B * S
    depth = w_in.shape[0]
    xr = x.reshape(T, D)
    w_in_t = jnp.swapaxes(w_in, 1, 2)
    moe = None
    for l in range(depth):
        p = _prep_layer(l, w_gk_up, b_gk, gla_norm, gmlp_norm, w_spatial, b_spatial, w_conv,
                        w_router_group, b_router_group, w_router_expert, b_router_expert)
        mix_params = (p["wgk"], p["bgk"], p["glan"], p["gmn"], p["wsp"], p["bsp"], p["wconv"])
        if moe is None:
            mixed = _front(xr, attn_norm[l][None, :], w_in_t, l, mix_params, B, S)
        else:
            xr, mixed = _front(moe["x2"], attn_norm[l][None, :], w_in_t, l, mix_params, B, S, moe)
        x2, h2, route, route_t, counts_rec = _out_router(
            mixed, xr, w_out, l, ffn_norm[l][None, :], p["wr_cat"], p["br"])
        moe = _dispatch_tables(route_t, counts_rec, T)
        xb = _dispatch(moe["fill_start"], moe["n_used"], moe["dest_tiles"], h2, moe["n_rows"])
        yb = _experts(moe["blk_exp"], moe["nxt_exp"], moe["n_used"], moe["blk_half"], xb,
                      w_gate, w_up, w_down, l)
        moe.update(x2=x2, route=route, yb=yb)
    out = _combine_final_norm(moe["dest_tiles"], moe["x2"], moe["route"], moe["yb"],
                              final_norm[None, :])
    return out.reshape(B, S, D)
```

```python
import functools

import jax
import jax.numpy as jnp
from jax import lax
from jax.experimental import pallas as pl
from jax.experimental.pallas import tpu as pltpu

F32 = jnp.float32
BF16 = jnp.bfloat16

D_MODEL = 1024
RMS_EPS = 1e-6
GLA_HEADS = 4
GLA_WIDTH = 512
GLA_DV = 128
GLA_DK = 64
GLA_KDIM = 256
GLA_GATE_RANK = 16
GLA_GATE_NORM = 16.0
GLA_CHUNK = 64
GMLP_HEADS = 4
GMLP_WIDTH = 256
GMLP_DH = 64
GMLP_CHUNK = 128
CONV_WIDTH = 256
CONV_K = 3
N_GROUPS = 4
EXPERTS_PER_GROUP = 8
N_EXPERTS = 32
TOP_K = 2
D_EXPERT = 256

LANES = 128
C_Q, C_K, C_V, C_G = 0, 256, 512, 1024
C_U, C_VG, C_X, C_BG, C_CG, C_GKL = 1536, 1792, 2048, 2304, 2560, 2816
D_PROJ = C_GKL + LANES
D_IN = C_GKL + GLA_GATE_RANK

TM = 256
TS_MIX = TM
MOE_BLK = 256
ROUTER_COLS = LANES
SUBLANES = 8
ROW_TILES = D_MODEL // LANES
assert ROW_TILES == SUBLANES
VMEM_LIMIT = 56 * 1024 * 1024
R_E, R_RANK, R_W = 0, 2, 4


def _dot(a, b):
    return jnp.dot(a, b, preferred_element_type=F32)


def _split_bf16(x):
    hi = x.astype(BF16)
    lo = (x - hi.astype(F32)).astype(BF16)
    return hi, lo


def _rms(x, gain):
    return x * lax.rsqrt(jnp.mean(x * x, axis=-1, keepdims=True) + RMS_EPS) * gain


W_PREP_ROWS = 128
PROJ_CHUNK = 256


def _stage_w_in(wt_ref, wb_ref):
    for c0 in range(0, C_GKL, LANES):
        src = c0 if c0 < C_U else c0 + GLA_GATE_RANK
        wb_ref[:, c0:c0 + LANES] = wt_ref[0, src:src + LANES, :].T.astype(BF16)
    low = jnp.concatenate([wt_ref[0, C_U:C_U + GLA_GATE_RANK, :],
                           jnp.zeros((LANES - GLA_GATE_RANK, D_MODEL), F32)], axis=0)
    wb_ref[:, C_GKL:D_PROJ] = low.T.astype(BF16)


def _row_gather_copy(yb_ref, buf_ref, sem_ref, slot, k, r, d):
    return pltpu.make_async_copy(yb_ref.at[d], buf_ref.at[slot, k, r], sem_ref.at[slot])


def _gather_start(dest_ref, yb_ref, buf_ref, sem_ref, slot, rows=range(TM)):
    for r in rows:
        for k in range(TOP_K):
            _row_gather_copy(yb_ref, buf_ref, sem_ref, slot, k, r,
                             dest_ref[0, 0, k * TM + r]).start(priority=k)


def _gather_wait(yb_ref, buf_ref, sem_ref, slot):
    for k in range(TOP_K):
        pltpu.make_async_copy(yb_ref.at[pl.ds(0, TM)], buf_ref.at[slot, k], sem_ref.at[slot]).wait()


def _combined_residual(dcur_ref, x_ref, route_ref, yb_ref, buf_ref, sem_ref):
    i = pl.program_id(0)
    slot = lax.rem(i, 2)

    @pl.when(i == 0)
    def _():
        _gather_start(dcur_ref, yb_ref, buf_ref, sem_ref, 0)

    _gather_wait(yb_ref, buf_ref, sem_ref, slot)
    w0 = route_ref[:, R_W:R_W + 1]
    w1 = route_ref[:, R_W + 1:R_W + 2]
    y0 = _row_tiles_chunks(buf_ref.at[slot, 0], TM)
    y1 = _row_tiles_chunks(buf_ref.at[slot, 1], TM)
    return jnp.concatenate(
        [x_ref[:, c * LANES:(c + 1) * LANES] + (w0 * y0[c] + w1 * y1[c]) for c in range(ROW_TILES)],
        axis=1)


def _prefetch_groups(n_groups):
    per = -(-TM // n_groups)
    return [range(g * per, min(TM, (g + 1) * per)) for g in range(n_groups)]


def _drain_last_prefetch(yb_ref, buf_ref, sem_ref):
    i = pl.program_id(0)

    @pl.when(i == pl.num_programs(0) - 1)
    def _():
        _gather_wait(yb_ref, buf_ref, sem_ref, 1 - lax.rem(i, 2))


def _combine_specs(n_tiles):
    smem_tile = lambda f: pl.BlockSpec((1, 1, TOP_K * TM), f, memory_space=pltpu.SMEM)
    return [
        smem_tile(lambda i: (i, 0, 0)),
        smem_tile(lambda i: (jnp.minimum(i + 1, n_tiles - 1), 0, 0)),
        pl.BlockSpec((TM, D_MODEL), lambda i: (i, 0)),
        pl.BlockSpec((TM, LANES), lambda i: (i, 0)),
        pl.BlockSpec(memory_space=pl.ANY),
    ]


_COMBINE_SCRATCH = [pltpu.VMEM((2, TOP_K, TM, ROW_TILES, LANES), F32),
                    pltpu.SemaphoreType.DMA((2,))]


def _combine_final_norm_kernel(dcur_ref, dnxt_ref, x_ref, route_ref, yb_ref, gain_ref,
                               o_ref, buf_ref, sem_ref):
    _gather_start(dnxt_ref, yb_ref, buf_ref, sem_ref, 1 - lax.rem(pl.program_id(0), 2))
    x = _combined_residual(dcur_ref, x_ref, route_ref, yb_ref, buf_ref, sem_ref)
    o_ref[...] = _rms(x, gain_ref[...])
    _drain_last_prefetch(yb_ref, buf_ref, sem_ref)


def _combine_final_norm(dest_tiles, x2, route, yb, gain):
    T = x2.shape[0]
    n_tiles = T // TM
    return pl.pallas_call(
        _combine_final_norm_kernel,
        grid=(n_tiles,),
        in_specs=_combine_specs(n_tiles) + [pl.BlockSpec((1, D_MODEL), lambda i: (0, 0))],
        out_specs=pl.BlockSpec((TM, D_MODEL), lambda i: (i, 0)),
        out_shape=jax.ShapeDtypeStruct((T, D_MODEL), F32),
        scratch_shapes=_COMBINE_SCRATCH,
        compiler_params=pltpu.CompilerParams(
            dimension_semantics=("arbitrary",), vmem_limit_bytes=VMEM_LIMIT),
        name="combine_final_norm",
    )(dest_tiles, dest_tiles, x2, route, yb, gain)


def _gelu_tanh(x):
    c = 0.7978845608028654
    return x * (0.5 * (1.0 + jnp.tanh(c * (x + 0.044715 * (x * x * x)))))


def _mixer_kernel(proj_ref, wgk_ref, bgk_ref, glan_ref, gmn_ref, wsp_ref, bsp_ref, wconv_ref,
                  out_ref, st_ref, hc_ref, lcat_ref, wm_ref, *, seq_start, first_step, between):
    TS = TS_MIX
    n_gla = TS // GLA_CHUNK
    n_gm = TS // GMLP_CHUNK

    @pl.when(seq_start)
    def _():
        st_ref[...] = jnp.zeros_like(st_ref)
        hc_ref[...] = jnp.zeros_like(hc_ref)

    @pl.when(first_step)
    def _():
        r = lax.broadcasted_iota(jnp.int32, (TS, TS), 0)
        c = lax.broadcasted_iota(jnp.int32, (TS, TS), 1)
        keep = ((r // GLA_CHUNK) == (c // GLA_CHUNK)) & (c <= r)
        lcat_ref[...] = jnp.where(keep, 1.0, 0.0).astype(BF16)
        t = lax.broadcasted_iota(jnp.int32, (GMLP_CHUNK, GMLP_HEADS * GMLP_CHUNK), 0)
        s = lax.broadcasted_iota(jnp.int32, (GMLP_CHUNK, GMLP_HEADS * GMLP_CHUNK), 1) % GMLP_CHUNK
        wm_ref[...] = jnp.where(s <= t, wsp_ref[...], 0.0).astype(BF16)

    lane256 = lax.broadcasted_iota(jnp.int32, (1, GLA_KDIM), 1)

    q = proj_ref[:, C_Q:C_Q + GLA_KDIM].astype(F32)
    k = proj_ref[:, C_K:C_K + GLA_KDIM].astype(F32)
    v_b = proj_ref[:, C_V:C_V + GLA_WIDTH]
    z = _dot(proj_ref[:, C_GKL:C_GKL + LANES], wgk_ref[...]) + bgk_ref[...]
    gk = (jnp.minimum(z, 0.0) - jnp.log1p(jnp.exp(-jnp.abs(z)))) * (1.0 / GLA_GATE_NORM)
    gk_hi, gk_lo = _split_bf16(gk)
    cs = _dot(lcat_ref[...], jnp.concatenate([gk_hi, gk_lo], axis=1))
    b = cs[:, :GLA_KDIM] + cs[:, GLA_KDIM:]
    b_last = [b[(c + 1) * GLA_CHUNK - 1:(c + 1) * GLA_CHUNK, :] for c in range(n_gla)]
    bl = jnp.concatenate(
        [jnp.broadcast_to(t, (GLA_CHUNK, GLA_KDIM)) for t in b_last], axis=0)
    q_dec = (q * (GLA_DK ** -0.5)) * jnp.exp(b)
    k_inv = (k * jnp.exp(-b)).astype(BF16)
    k_dec = (k * jnp.exp(bl - b)).astype(BF16)
    q_dec_b = q_dec.astype(BF16)

    zero_b = jnp.zeros_like(q_dec_b)
    q_stack = jnp.concatenate(
        [jnp.where((lane256 // GLA_DK) == h, q_dec_b, zero_b) for h in range(GLA_HEADS)], axis=0)
    scores = lax.dot_general(q_stack, k_inv, (((1,), (1,)), ((), ())),
                             preferred_element_type=F32)
    rt = lax.broadcasted_iota(jnp.int32, (TS, TS), 0)
    ct = lax.broadcasted_iota(jnp.int32, (TS, TS), 1)
    causal = ((rt // GLA_CHUNK) == (ct // GLA_CHUNK)) & (ct <= rt)
    o_heads = []
    for h in range(GLA_HEADS):
        p_h = jnp.where(causal, scores[h * TS:(h + 1) * TS, :], 0.0).astype(BF16)
        o_heads.append(_dot(p_h, v_b[:, h * GLA_DV:(h + 1) * GLA_DV]))

    sr = lax.broadcasted_iota(jnp.int32, (GLA_WIDTH, GLA_KDIM), 0) // GLA_DV
    sc = lax.broadcasted_iota(jnp.int32, (GLA_WIDTH, GLA_KDIM), 1) // GLA_DK
    bd_mask = sr == sc
    o_inter = []
    for c in range(n_gla):
        rows = slice(c * GLA_CHUNK, (c + 1) * GLA_CHUNK)
        st = st_ref[...]
        o_inter.append(lax.dot_general(q_dec_b[rows], st.astype(BF16), (((1,), (1,)), ((), ())),
                                       preferred_element_type=F32))
        upd = lax.dot_general(v_b[rows], k_dec[rows], (((0,), (0,)), ((), ())),
                              preferred_element_type=F32)
        decay = jnp.exp(b_last[c])
        st_ref[...] = st * decay + jnp.where(bd_mask, upd, 0.0)
    o_inter = jnp.concatenate(o_inter, axis=0)

    for h in range(GLA_HEADS):
        cols = slice(h * GLA_DV, (h + 1) * GLA_DV)
        o = o_heads[h] + o_inter[:, cols]
        o = o * lax.rsqrt(jnp.mean(o * o, axis=-1, keepdims=True) + RMS_EPS) * glan_ref[...]
        g = proj_ref[:, C_G + h * GLA_DV:C_G + (h + 1) * GLA_DV].astype(F32)
        out_ref[:, cols] = (o * (g * (1.0 / (1.0 + jnp.exp(-g))))).astype(out_ref.dtype)

    between("gla_done")
    u = _gelu_tanh(proj_ref[:, C_U:C_U + GMLP_WIDTH].astype(F32))
    vg = _gelu_tanh(proj_ref[:, C_VG:C_VG + GMLP_WIDTH].astype(F32))
    hr = lax.broadcasted_iota(jnp.int32, (GMLP_WIDTH, GMLP_WIDTH), 0) // GMLP_DH
    hcn = lax.broadcasted_iota(jnp.int32, (GMLP_WIDTH, GMLP_WIDTH), 1) // GMLP_DH
    head_mean = jnp.where(hr == hcn, 1.0 / GMLP_DH, 0.0).astype(BF16)
    sq_hi, sq_lo = _split_bf16(vg * vg)
    ms = _dot(sq_hi, head_mean) + _dot(sq_lo, head_mean)
    v32 = vg * lax.rsqrt(ms + RMS_EPS) * gmn_ref[...]
    for c in range(n_gm):
        rows = slice(c * GMLP_CHUNK, (c + 1) * GMLP_CHUNK)
        vc = v32[rows].astype(BF16)
        zc = jnp.zeros_like(vc)
        rhs = jnp.concatenate(
            [jnp.where((lane256 // GMLP_DH) == h, vc, zc) for h in range(GMLP_HEADS)], axis=0)
        mixed = _dot(wm_ref[...], rhs) + bsp_ref[...]
        out_ref[rows, GLA_WIDTH:GLA_WIDTH + GMLP_WIDTH] = (u[rows] * mixed).astype(out_ref.dtype)

    between("gmlp_done")
    hcv = (proj_ref[:, C_CG:C_CG + CONV_WIDTH].astype(F32)
           * proj_ref[:, C_X:C_X + CONV_WIDTH].astype(F32))
    hc_ref[8:8 + TS, :] = hcv
    y = (wconv_ref[2:3, :] * hcv + wconv_ref[1:2, :] * hc_ref[7:7 + TS, :]
         + wconv_ref[0:1, :] * hc_ref[6:6 + TS, :])
    out_ref[:, GLA_WIDTH + GMLP_WIDTH:] = (
        proj_ref[:, C_BG:C_BG + CONV_WIDTH].astype(F32) * y).astype(out_ref.dtype)
    hc_ref[0:8, :] = hc_ref[TS:TS + 8, :]


_MIXER_SCRATCH = [
    pltpu.VMEM((GLA_WIDTH, GLA_KDIM), F32),
    pltpu.VMEM((TS_MIX + 8, CONV_WIDTH), F32),
    pltpu.VMEM((TS_MIX, TS_MIX), BF16),
    pltpu.VMEM((GMLP_CHUNK, GMLP_HEADS * GMLP_CHUNK), BF16),
]


N_MIX_PARAMS = 7
PROJ_CHUNKS_AT = {"gla_done": 4, "gmlp_done": 4}


def _front_kernel(*refs, tiles_per_seq, combine):
    refs = list(refs)
    if combine:
        dcur_ref, dnxt_ref, x_ref, route_ref, yb_ref = refs[:5]
        del refs[:5]
    else:
        x_ref = refs.pop(0)
    gain_ref, wt_ref = refs[:2]
    mix_refs = refs[2:2 + N_MIX_PARAMS]
    del refs[:2 + N_MIX_PARAMS]
    if combine:
        xo_ref, out_ref, buf_ref, sem_ref = refs[:4]
        del refs[:4]
    else:
        out_ref = refs.pop(0)
    st_ref, hc_ref, lcat_ref, wm_ref, wb_ref, pcur_ref, pnext_ref = refs
    s = pl.program_id(0)

    @pl.when(s == 0)
    def _():
        _stage_w_in(wt_ref, wb_ref)
        pcur_ref[...] = jnp.zeros_like(pcur_ref)

    col_chunks = [(c0, min(c0 + PROJ_CHUNK, D_PROJ)) for c0 in range(0, D_PROJ, PROJ_CHUNK)]
    work = list(zip(_prefetch_groups(len(col_chunks)), col_chunks))

    def project(h, n):
        for _ in range(min(n, len(work))):
            rows, (c0, c1) = work.pop(0)
            if combine:
                _gather_start(dnxt_ref, yb_ref, buf_ref, sem_ref, 1 - lax.rem(s, 2), rows)
            pnext_ref[:, c0:c1] = _dot(h, wb_ref[:, c0:c1]).astype(BF16)

    mixers = functools.partial(
        _mixer_kernel, pcur_ref, *mix_refs, out_ref, st_ref, hc_ref, lcat_ref, wm_ref,
        seq_start=lax.rem(jnp.maximum(s - 1, 0), tiles_per_seq) == 0, first_step=s == 0)
    if combine:
        x = _combined_residual(dcur_ref, x_ref, route_ref, yb_ref, buf_ref, sem_ref)
        xo_ref[...] = x
        h = _rms(x, gain_ref[...]).astype(BF16)
        mixers(between=lambda site: project(h, PROJ_CHUNKS_AT[site]))
        project(h, len(work))
        _drain_last_prefetch(yb_ref, buf_ref, sem_ref)
    else:
        mixers(between=lambda site: None)
        project(_rms(x_ref[...], gain_ref[...]).astype(BF16), len(work))
    pcur_ref[...] = pnext_ref[...]


def _front(x, gain, w_in_t, layer, mix_params, batch, seq, moe=None):
    n_seq = seq // TS_MIX
    n = batch * n_seq
    T = batch * seq
    cur = lambda s: jnp.minimum(s, n - 1)
    full = lambda shape: pl.BlockSpec(shape, lambda s: (0,) * len(shape))
    row = lambda w, f: pl.BlockSpec((TM, w), lambda s: (f(s), 0))
    in_specs, args = [row(D_MODEL, cur)], [x]
    out_specs = [row(D_MODEL, lambda s: jnp.maximum(s - 1, 0))]
    out_shape = [jax.ShapeDtypeStruct((T, D_MODEL), BF16)]
    scratch = list(_MIXER_SCRATCH)
    if moe is not None:
        smem_tile = lambda f: pl.BlockSpec((1, 1, TOP_K * TM), lambda s: (f(s), 0, 0),
                                           memory_space=pltpu.SMEM)
        in_specs = [smem_tile(cur), smem_tile(lambda s: jnp.minimum(s + 1, n - 1))] + in_specs + [
            row(LANES, cur), pl.BlockSpec(memory_space=pl.ANY)]
        args = [moe["dest_tiles"], moe["dest_tiles"]] + args + [moe["route"], moe["yb"]]
        out_specs = [row(D_MODEL, cur)] + out_specs
        out_shape = [jax.ShapeDtypeStruct((T, D_MODEL), F32)] + out_shape
        scratch = _COMBINE_SCRATCH + scratch
    in_specs += [
        full((1, D_MODEL)),
        pl.BlockSpec((1, D_IN, D_MODEL), lambda s: (layer, 0, 0), pipeline_mode=pl.Buffered(1)),
        full((LANES, GLA_KDIM)), full((1, GLA_KDIM)), full((1, GLA_DV)), full((1, GMLP_WIDTH)),
        full((GMLP_CHUNK, GMLP_HEADS * GMLP_CHUNK)), full((GMLP_CHUNK, GMLP_WIDTH)),
        full((8, CONV_WIDTH)),
    ]
    scratch += [pltpu.VMEM((D_MODEL, D_PROJ), BF16), pltpu.VMEM((TS_MIX, D_PROJ), BF16),
                pltpu.VMEM((TS_MIX, D_PROJ), BF16)]
    return pl.pallas_call(
        functools.partial(_front_kernel, tiles_per_seq=n_seq, combine=moe is not None),
        grid=(n + 1,),
        in_specs=in_specs,
        out_specs=out_specs if moe is not None else out_specs[0],
        out_shape=out_shape if moe is not None else out_shape[0],
        scratch_shapes=scratch,
        compiler_params=pltpu.CompilerParams(
            dimension_semantics=("arbitrary",), vmem_limit_bytes=VMEM_LIMIT),
        name="front",
    )(*args, gain, w_in_t, *mix_params)


def _row_tiles_store(tiles_ref, x):
    rows = x.shape[0]
    flat = tiles_ref.reshape(rows * ROW_TILES, LANES)
    for c in range(ROW_TILES):
        flat[pl.ds(c, rows, stride=ROW_TILES), :] = x[:, c * LANES:(c + 1) * LANES]


def _row_tiles_chunks(tiles_ref, rows):
    flat = tiles_ref.reshape(rows * ROW_TILES, LANES)
    return [flat[pl.ds(c, rows, stride=ROW_TILES), :] for c in range(ROW_TILES)]


IN_SLOTS = 3


def _out_router_kernel(mix_hbm, x_hbm, wo_ref, gain_ref, wrc_ref, br_ref,
                       x2_ref, h2_ref, route_ref, route_t_ref, cnt_ref, tri_ref, wob_ref, lg_ref,
                       mixbuf_ref, xbuf_ref, insem_ref):
    i = pl.program_id(0)
    n = pl.num_programs(0) - 1

    def in_copies(t):
        slot = lax.rem(t, IN_SLOTS)
        rows = pl.ds(t * TM, TM)
        return [pltpu.make_async_copy(mix_hbm.at[rows], mixbuf_ref.at[slot], insem_ref.at[slot]),
                pltpu.make_async_copy(x_hbm.at[rows], xbuf_ref.at[slot], insem_ref.at[slot])]

    @pl.when(i == 0)
    def _():
        for t in range(IN_SLOTS - 1):
            for c in in_copies(t):
                c.start()

    @pl.when(i + IN_SLOTS - 1 < n)
    def _():
        for c in in_copies(i + IN_SLOTS - 1):
            c.start()

    @pl.when(i < n)
    def _():
        for c in in_copies(i):
            c.wait()

    tile_slot = lax.rem(jnp.minimum(i, n - 1), IN_SLOTS)
    mix_ref = mixbuf_ref.at[tile_slot]
    x_ref = xbuf_ref.at[tile_slot]

    @pl.when(i == 0)
    def _():
        cnt_ref[...] = jnp.zeros_like(cnt_ref)
        lg_ref[...] = jnp.zeros_like(lg_ref)
        r = lax.broadcasted_iota(jnp.int32, (TM, TM), 0)
        c = lax.broadcasted_iota(jnp.int32, (TM, TM), 1)
        tri_ref[...] = jnp.where(c < r, 1.0, 0.0).astype(BF16)
        for r0 in range(0, D_MODEL, W_PREP_ROWS):
            wob_ref[r0:r0 + W_PREP_ROWS, :] = wo_ref[0, r0:r0 + W_PREP_ROWS, :].astype(BF16)

    lg = lg_ref[...]
    half = D_MODEL // 2
    mix = mix_ref[...]
    x2_a = x_ref[:, :half] + _dot(mix, wob_ref[:, :half])

    lane = lax.broadcasted_iota(jnp.int32, (TM, LANES), 1).astype(F32)
    neg = -jnp.inf
    is_g = lane < N_GROUPS
    gl = jnp.where(is_g, lg, neg)
    gmax = jnp.max(gl, axis=1, keepdims=True)
    g_top = jnp.min(jnp.where(gl == gmax, lane, float(LANES)), axis=1, keepdims=True)
    g_w = 1.0 / jnp.sum(jnp.where(is_g, jnp.exp(lg - gmax), 0.0), axis=1, keepdims=True)
    first = N_GROUPS + EXPERTS_PER_GROUP * g_top
    el = jnp.where((lane >= first) & (lane < first + EXPERTS_PER_GROUP), lg, neg)
    m1 = jnp.max(el, axis=1, keepdims=True)
    i1 = jnp.min(jnp.where(el == m1, lane, float(LANES)), axis=1, keepdims=True)
    el2 = jnp.where(lane == i1, neg, el)
    m2 = jnp.max(el2, axis=1, keepdims=True)
    i2 = jnp.min(jnp.where(el2 == m2, lane, float(LANES)), axis=1, keepdims=True)
    ratio = jnp.exp(m2 - m1)
    w1 = g_w / (1.0 + ratio)
    w2 = w1 * ratio

    x2_b = x_ref[:, half:] + _dot(mix, wob_ref[:, half:])

    oh1 = jnp.where(lane == i1, 1.0, 0.0)
    oh2 = jnp.where(lane == i2, 1.0, 0.0)
    oh = jnp.where(i > 0, oh1 + oh2, 0.0)
    before = _dot(tri_ref[...], oh.astype(BF16)) + cnt_ref[0:1, :]
    rank1 = jnp.sum(oh1 * before, axis=1, keepdims=True)
    rank2 = jnp.sum(oh2 * before, axis=1, keepdims=True)
    cnt_ref[...] = cnt_ref[...] + jnp.sum(oh, axis=0, keepdims=True)

    rec = jnp.zeros((TM, LANES), F32)
    for col, val in ((R_E, i1 - N_GROUPS), (R_E + 1, i2 - N_GROUPS), (R_RANK, rank1),
                     (R_RANK + 1, rank2), (R_W, w1), (R_W + 1, w2)):
        rec = jnp.where(lane == col, val, rec)
    route_ref[...] = rec
    route_t_ref[0] = rec.T[0:SUBLANES, :]

    x2 = jnp.concatenate([x2_a, x2_b], axis=1)
    x2_ref[...] = x2
    h = _rms(x2, gain_ref[...])
    h_hi, h_lo = _split_bf16(h)
    h2_ref[...] = h_hi
    hh_hl = _dot(h_hi, wrc_ref[...])
    lg_ref[...] = (hh_hl[:, :ROUTER_COLS] + hh_hl[:, ROUTER_COLS:]
                   + _dot(h_lo, wrc_ref[:, :ROUTER_COLS]) + br_ref[...])


def _out_router(mixed, x, w_out, layer, gain, wr_cat, br):
    T = x.shape[0]
    n = T // TM
    row = lambda w: pl.BlockSpec((TM, w), lambda i: (jnp.minimum(i, n - 1), 0))
    lag = lambda i: jnp.maximum(i - 1, 0)
    full = lambda shape: pl.BlockSpec(shape, lambda i: (0, 0))
    wo_spec = pl.BlockSpec((1, D_MODEL, D_MODEL), lambda i: (layer, 0, 0),
                           pipeline_mode=pl.Buffered(1))
    return pl.pallas_call(
        _out_router_kernel,
        grid=(n + 1,),
        in_specs=[pl.BlockSpec(memory_space=pl.ANY), pl.BlockSpec(memory_space=pl.ANY), wo_spec,
                  full((1, D_MODEL)), full((D_MODEL, 2 * ROUTER_COLS)), full((1, ROUTER_COLS))],
        out_specs=[row(D_MODEL), row(D_MODEL),
                   pl.BlockSpec((TM, LANES), lambda i: (lag(i), 0)),
                   pl.BlockSpec((1, SUBLANES, TM), lambda i: (lag(i), 0, 0)), full((8, LANES))],
        out_shape=[jax.ShapeDtypeStruct((T, D_MODEL), F32),
                   jax.ShapeDtypeStruct((T, D_MODEL), BF16),
                   jax.ShapeDtypeStruct((T, LANES), F32),
                   jax.ShapeDtypeStruct((T // TM, SUBLANES, TM), F32),
                   jax.ShapeDtypeStruct((8, LANES), F32)],
        scratch_shapes=[pltpu.VMEM((TM, TM), BF16), pltpu.VMEM((D_MODEL, D_MODEL), BF16),
                        pltpu.VMEM((TM, ROUTER_COLS), F32),
                        pltpu.VMEM((IN_SLOTS, TM, D_MODEL), BF16),
                        pltpu.VMEM((IN_SLOTS, TM, D_MODEL), F32),
                        pltpu.SemaphoreType.DMA((IN_SLOTS,))],
        compiler_params=pltpu.CompilerParams(
            dimension_semantics=("arbitrary",), vmem_limit_bytes=VMEM_LIMIT),
        name="out_router",
    )(mixed, x, w_out, gain, wr_cat, br)


def _dispatch_kernel(fill_ref, nu_ref, dest_ref, h_ref, xb_ref, zero_ref, sem_ref, zsem_ref,
                     stage_ref):
    i = pl.program_id(0)
    par = lax.rem(i, 2)
    last = i == pl.num_programs(0) - 1
    n_blocks = xb_ref.shape[0] // MOE_BLK
    spare_fills = [(j >= nu_ref[0], pltpu.make_async_copy(
        zero_ref, xb_ref.at[pl.ds(j * MOE_BLK, MOE_BLK)], zsem_ref.at[1]))
        for j in range(n_blocks - N_EXPERTS, n_blocks)]

    @pl.when(i == 0)
    def _():
        zero_ref[...] = jnp.zeros_like(zero_ref)
        fills = [(fill_ref[e] >= 0, pltpu.make_async_copy(
            zero_ref, xb_ref.at[pl.ds(pl.multiple_of(jnp.maximum(fill_ref[e], 0), MOE_BLK), MOE_BLK)],
            zsem_ref.at[0])) for e in range(N_EXPERTS)]
        for cond, f in fills + spare_fills:
            pl.when(cond)(f.start)
        for cond, f in fills:
            pl.when(cond)(f.wait)

    _row_tiles_store(stage_ref.at[par], h_ref[...].astype(F32))
    for r in range(TM):
        for k in range(TOP_K):
            pltpu.make_async_copy(stage_ref.at[par, r], xb_ref.at[dest_ref[0, 0, k * TM + r]],
                                  sem_ref.at[par]).start(priority=k)

    def wait_tile(p):
        for _ in range(TOP_K):
            pltpu.make_async_copy(stage_ref.at[p], xb_ref.at[pl.ds(0, TM)], sem_ref.at[p]).wait()

    pl.when(i > 0)(lambda: wait_tile(1 - par))
    @pl.when(last)
    def _():
        wait_tile(par)
        for cond, f in spare_fills:
            pl.when(cond)(f.wait)


def _dispatch(fill_start, n_used, dest_tiles, h2, n_rows):
    T = h2.shape[0]
    grid_spec = pltpu.PrefetchScalarGridSpec(
        num_scalar_prefetch=2,
        grid=(T // TM,),
        in_specs=[
            pl.BlockSpec((1, 1, TOP_K * TM), lambda i, fs, nu: (i, 0, 0), memory_space=pltpu.SMEM),
            pl.BlockSpec((TM, D_MODEL), lambda i, fs, nu: (i, 0)),
        ],
        out_specs=pl.BlockSpec(memory_space=pl.ANY),
        scratch_shapes=[pltpu.VMEM((MOE_BLK, ROW_TILES, LANES), F32),
                        pltpu.SemaphoreType.DMA((2,)), pltpu.SemaphoreType.DMA((2,)),
                        pltpu.VMEM((2, TM, ROW_TILES, LANES), F32)],
    )
    return pl.pallas_call(
        _dispatch_kernel,
        grid_spec=grid_spec,
        out_shape=jax.ShapeDtypeStruct((n_rows, ROW_TILES, LANES), F32),
        compiler_params=pltpu.CompilerParams(dimension_semantics=("arbitrary",)),
        name="dispatch",
    )(fill_start, n_used, dest_tiles, h2)


BLOCK_COPY_PARTS = 4
BLOCK_QUARTERS = 4
X_SLOTS = 4
Y_SLOTS = 3


class _CopyGroup:
    def __init__(self, copies):
        self.copies = copies

    def start(self):
        for n, c in enumerate(self.copies):
            c.start(priority=n % 2)

    def wait(self):
        for c in self.copies:
            c.wait()


def _expert_kernel(be_ref, nxt_ref, nu_ref, half_ref, xb_ref, wg_ref, wu_ref, wd_ref, yb_ref,
                   xbuf_ref, ybuf_ref, wgs_ref, wus_ref, wds_ref, wgb_ref, wub_ref, wdb_ref,
                   xsem_ref, ysem_ref, wsem_ref, zbuf_ref, zsem_ref, *, layer, n_blocks):
    n_used = nu_ref[0]

    part = MOE_BLK // BLOCK_COPY_PARTS

    def x_copy(j, slot):
        return _CopyGroup([pltpu.make_async_copy(
            xb_ref.at[pl.ds(j * MOE_BLK + p * part, part)],
            xbuf_ref.at[slot, pl.ds(p * part, part)], xsem_ref.at[slot])
            for p in range(BLOCK_COPY_PARTS)])

    def y_copy(j, slot):
        return _CopyGroup([pltpu.make_async_copy(
            ybuf_ref.at[slot, pl.ds(p * part, part)],
            yb_ref.at[pl.ds(j * MOE_BLK + p * part, part)], ysem_ref.at[slot])
            for p in range(BLOCK_COPY_PARTS)])

    def w_copies(e, ws):
        return [pltpu.make_async_copy(src.at[layer, e], dst.at[ws], wsem_ref.at[ws])
                for src, dst in ((wg_ref, wgs_ref), (wu_ref, wus_ref), (wd_ref, wds_ref))]

    def fill_copy(j):
        return pltpu.make_async_copy(zbuf_ref, yb_ref.at[pl.ds(j * MOE_BLK, MOE_BLK)], zsem_ref)

    for j0 in range(X_SLOTS - 1):
        x_copy(j0, j0).start()
    for c in w_copies(be_ref[0], 0):
        c.start()

    zbuf_ref[...] = jnp.zeros_like(zbuf_ref)

    def fill(j, carry):
        fill_copy(j).start()
        return carry

    lax.fori_loop(n_used, n_blocks, fill, 0)

    def block(j, ws):
        slot = lax.rem(j, Y_SLOTS)
        xslot = lax.rem(j, X_SLOTS)
        first = (j == 0) | (be_ref[j] != be_ref[jnp.maximum(j - 1, 0)])
        ws = jnp.where(first & (j > 0), 1 - ws, ws)

        @pl.when(first)
        def _():
            for c in w_copies(be_ref[j], ws):
                c.wait()
            for r0 in range(0, D_MODEL, W_PREP_ROWS):
                rows = slice(r0, r0 + W_PREP_ROWS)
                wgb_ref[rows, :] = wgs_ref[ws, rows, :].astype(BF16)
                wub_ref[rows, :] = wus_ref[ws, rows, :].astype(BF16)
            for r0 in range(0, D_EXPERT, W_PREP_ROWS):
                rows = slice(r0, r0 + W_PREP_ROWS)
                wdb_ref[rows, :] = wds_ref[ws, rows, :].astype(BF16)

            @pl.when(nxt_ref[j] >= 0)
            def _():
                for c in w_copies(nxt_ref[j], 1 - ws):
                    c.start()

        ahead = j + X_SLOTS - 1

        @pl.when(ahead < n_used)
        def _():
            x_copy(ahead, lax.rem(ahead, X_SLOTS)).start()

        x_copy(j, xslot).wait()

        @pl.when(j >= Y_SLOTS)
        def _():
            y_copy(j - Y_SLOTS, slot).wait()

        def mlp(rows):
            x = jnp.concatenate(
                [c.astype(BF16) for c in _row_tiles_chunks(xbuf_ref.at[xslot, pl.ds(0, rows)], rows)],
                axis=1)
            g = _dot(x, wgb_ref[...])
            u = _dot(x, wub_ref[...])
            h = (g * (1.0 / (1.0 + jnp.exp(-g)))) * u
            _row_tiles_store(ybuf_ref.at[slot, pl.ds(0, rows)], _dot(h.astype(BF16), wdb_ref[...]))

        quarter = MOE_BLK // BLOCK_QUARTERS
        for q in range(1, BLOCK_QUARTERS + 1):
            @pl.when(half_ref[j] == q)
            def _(q=q):
                mlp(q * quarter)
                if q < BLOCK_QUARTERS:
                    ybuf_ref[slot, pl.ds(q * quarter, MOE_BLK - q * quarter)] = jnp.zeros(
                        (MOE_BLK - q * quarter, ROW_TILES, LANES), F32)

        y_copy(j, slot).start()
        return ws

    lax.fori_loop(0, n_used, block, jnp.int32(0))

    for back in range(Y_SLOTS, 0, -1):
        y_copy(n_used - back, lax.rem(n_used - back, Y_SLOTS)).wait()

    def fill_wait(j, carry):
        fill_copy(j).wait()
        return carry

    lax.fori_loop(n_used, n_blocks, fill_wait, 0)


def _experts(blk_exp, nxt_exp, n_used, blk_half, xb, w_gate, w_up, w_down, layer):
    n_blocks = blk_exp.shape[0]
    any_spec = pl.BlockSpec(memory_space=pl.ANY)
    blk = (MOE_BLK, ROW_TILES, LANES)
    grid_spec = pltpu.PrefetchScalarGridSpec(
        num_scalar_prefetch=4,
        grid=(1,),
        in_specs=[any_spec, any_spec, any_spec, any_spec],
        out_specs=any_spec,
        scratch_shapes=[
            pltpu.VMEM((X_SLOTS,) + blk, F32), pltpu.VMEM((Y_SLOTS,) + blk, F32),
            pltpu.VMEM((2, D_MODEL, D_EXPERT), F32), pltpu.VMEM((2, D_MODEL, D_EXPERT), F32),
            pltpu.VMEM((2, D_EXPERT, D_MODEL), F32),
            pltpu.VMEM((D_MODEL, D_EXPERT), BF16), pltpu.VMEM((D_MODEL, D_EXPERT), BF16),
            pltpu.VMEM((D_EXPERT, D_MODEL), BF16),
            pltpu.SemaphoreType.DMA((X_SLOTS,)), pltpu.SemaphoreType.DMA((Y_SLOTS,)),
            pltpu.SemaphoreType.DMA((2,)),
            pltpu.VMEM(blk, F32), pltpu.SemaphoreType.DMA(()),
        ],
    )
    return pl.pallas_call(
        functools.partial(_expert_kernel, layer=layer, n_blocks=n_blocks),
        grid_spec=grid_spec,
        out_shape=jax.ShapeDtypeStruct((n_blocks * MOE_BLK, ROW_TILES, LANES), F32),
        compiler_params=pltpu.CompilerParams(
            dimension_semantics=("arbitrary",), vmem_limit_bytes=VMEM_LIMIT),
        name="experts",
    )(blk_exp, nxt_exp, n_used, blk_half, xb, w_gate, w_up, w_down)


def _dispatch_tables(route_t, counts_rec, T):
    counts = counts_rec[0, N_GROUPS:N_GROUPS + N_EXPERTS].astype(jnp.int32)
    n_steps = (T * TOP_K) // MOE_BLK + N_EXPERTS
    nblk = (counts + MOE_BLK - 1) // MOE_BLK
    bend = jnp.cumsum(nblk)
    pstart = (bend - nblk) * MOE_BLK
    n_used = bend[-1]
    j = jnp.minimum(jnp.arange(n_steps, dtype=jnp.int32), n_used - 1)
    blk_exp = jnp.minimum(jnp.sum(j[:, None] >= bend[None, :], axis=1), N_EXPERTS - 1)
    n_rows = n_steps * MOE_BLK
    last_blk = jnp.where(counts > 0, (bend - 1) * MOE_BLK, -1)
    ids = jnp.arange(N_EXPERTS, dtype=jnp.int32)
    later = (ids[None, :] > ids[:, None]) & (nblk[None, :] > 0)
    nxt_of = jnp.min(jnp.where(later, ids[None, :], N_EXPERTS), axis=1)
    nxt_tab = jnp.where(nxt_of < N_EXPERTS, nxt_of, -1)
    own = blk_exp[:, None] == ids[None, :]
    nxt_exp = jnp.sum(jnp.where(own, nxt_tab[None, :], 0), axis=1)
    seg_end = jnp.sum(jnp.where(own, (pstart + counts)[None, :], 0), axis=1)
    quarter = MOE_BLK // BLOCK_QUARTERS
    blk_half = jnp.clip((seg_end - j * MOE_BLK + quarter - 1) // quarter, 1, BLOCK_QUARTERS)
    blk_half = blk_half.astype(jnp.int32)
    e = route_t[:, R_E:R_E + TOP_K, :].astype(jnp.int32)
    rank = route_t[:, R_RANK:R_RANK + TOP_K, :].astype(jnp.int32)
    seg = jnp.sum(jnp.where(e[..., None] == jnp.arange(N_EXPERTS), pstart, 0), axis=-1)
    dest = jnp.clip(seg + rank, 0, n_steps * MOE_BLK - 1)
    dest_tiles = dest.reshape(T // TM, 1, TOP_K * TM)
    return dict(dest_tiles=dest_tiles, fill_start=last_blk.astype(jnp.int32),
                blk_exp=blk_exp.astype(jnp.int32), nxt_exp=nxt_exp.astype(jnp.int32),
                blk_half=blk_half,
                n_used=n_used.reshape(1).astype(jnp.int32), n_rows=n_rows)


def _prep_layer(l, w_gk_up, b_gk, gla_norm, gmlp_norm, w_spatial, b_spatial, w_conv,
                w_router_group, b_router_group, w_router_expert, b_router_expert):
    wgk = jnp.concatenate(
        [w_gk_up[l], jnp.zeros((LANES - GLA_GATE_RANK, GLA_KDIM), F32)], axis=0).astype(BF16)
    wsp = w_spatial[l].transpose(1, 0, 2).reshape(GMLP_CHUNK, GMLP_HEADS * GMLP_CHUNK)
    bsp = jnp.repeat(b_spatial[l].T, GMLP_DH, axis=1)
    wconv = jnp.concatenate([w_conv[l], jnp.zeros((8 - CONV_K, CONV_WIDTH), F32)], axis=0)
    wr = jnp.concatenate(
        [w_router_group[l], w_router_expert[l],
         jnp.zeros((D_MODEL, ROUTER_COLS - N_GROUPS - N_EXPERTS), F32)], axis=1)
    wr_hi = wr.astype(BF16)
    wr_lo = (wr - wr_hi.astype(F32)).astype(BF16)
    br = jnp.concatenate(
        [b_router_group[l], b_router_expert[l],
         jnp.zeros((ROUTER_COLS - N_GROUPS - N_EXPERTS,), F32)])[None, :]
    return dict(
        wgk=wgk, bgk=b_gk[l][None, :], glan=gla_norm[l][None, :], gmn=gmlp_norm[l][None, :],
        wsp=wsp, bsp=bsp, wconv=wconv, wr_cat=jnp.concatenate([wr_hi, wr_lo], axis=1), br=br)


def kernel(x, attn_norm, w_in, w_gk_up, b_gk, gla_norm, gmlp_norm, w_spatial, b_spatial, w_conv, w_out, ffn_norm, w_router_group, b_router_group, w_router_expert, b_router_expert, w_gate, w_up, w_down, final_norm):
    B, S, D = x.shape
    T = B * S
    depth = w_in.shape[0]
    xr = x.reshape(T, D)
    w_in_t = jnp.swapaxes(w_in, 1, 2)
    moe = None
    for l in range(depth):
        p = _prep_layer(l, w_gk_up, b_gk, gla_norm, gmlp_norm, w_spatial, b_spatial, w_conv,
                        w_router_group, b_router_group, w_router_expert, b_router_expert)
        mix_params = (p["wgk"], p["bgk"], p["glan"], p["gmn"], p["wsp"], p["bsp"], p["wconv"])
        if moe is None:
            mixed = _front(xr, attn_norm[l][None, :], w_in_t, l, mix_params, B, S)
        else:
            xr, mixed = _front(moe["x2"], attn_norm[l][None, :], w_in_t, l, mix_params, B, S, moe)
        x2, h2, route, route_t, counts_rec = _out_router(
            mixed, xr, w_out, l, ffn_norm[l][None, :], p["wr_cat"], p["br"])
        moe = _dispatch_tables(route_t, counts_rec, T)
        xb = _dispatch(moe["fill_start"], moe["n_used"], moe["dest_tiles"], h2, moe["n_rows"])
        yb = _experts(moe["blk_exp"], moe["nxt_exp"], moe["n_used"], moe["blk_half"], xb,
                      w_gate, w_up, w_down, l)
        moe.update(x2=x2, route=route, yb=yb)
    out = _combine_final_norm(moe["dest_tiles"], moe["x2"], moe["route"], moe["yb"],
                              final_norm[None, :])
    return out.reshape(B, S, D)
```

```python
import functools

import jax
import jax.numpy as jnp
from jax import lax
from jax.experimental import pallas as pl
from jax.experimental.pallas import tpu as pltpu

F32 = jnp.float32
BF16 = jnp.bfloat16

D_MODEL = 1024
RMS_EPS = 1e-6
GLA_HEADS = 4
GLA_WIDTH = 512
GLA_DV = 128
GLA_DK = 64
GLA_KDIM = 256
GLA_GATE_RANK = 16
GLA_GATE_NORM = 16.0
GLA_CHUNK = 64
GMLP_HEADS = 4
GMLP_WIDTH = 256
GMLP_DH = 64
GMLP_CHUNK = 128
CONV_WIDTH = 256
CONV_K = 3
N_GROUPS = 4
EXPERTS_PER_GROUP = 8
N_EXPERTS = 32
TOP_K = 2
D_EXPERT = 256

LANES = 128
C_Q, C_K, C_V, C_G = 0, 256, 512, 1024
C_U, C_VG, C_X, C_BG, C_CG, C_GKL = 1536, 1792, 2048, 2304, 2560, 2816
D_PROJ = C_GKL + LANES
D_IN = C_GKL + GLA_GATE_RANK

TM = 256
TS_MIX = TM
MOE_BLK = 256
ROUTER_COLS = LANES
SUBLANES = 8
ROW_TILES = D_MODEL // LANES
assert ROW_TILES == SUBLANES
VMEM_LIMIT = 56 * 1024 * 1024
R_E, R_RANK, R_W = 0, 2, 4


def _dot(a, b):
    return jnp.dot(a, b, preferred_element_type=F32)


def _split_bf16(x):
    hi = x.astype(BF16)
    lo = (x - hi.astype(F32)).astype(BF16)
    return hi, lo


def _rms(x, gain):
    return x * lax.rsqrt(jnp.mean(x * x, axis=-1, keepdims=True) + RMS_EPS) * gain


W_PREP_ROWS = 128
PROJ_CHUNK = 256


def _stage_w_in(wt_ref, wb_ref):
    for c0 in range(0, C_GKL, LANES):
        src = c0 if c0 < C_U else c0 + GLA_GATE_RANK
        wb_ref[:, c0:c0 + LANES] = wt_ref[0, src:src + LANES, :].T.astype(BF16)
    low = jnp.concatenate([wt_ref[0, C_U:C_U + GLA_GATE_RANK, :],
                           jnp.zeros((LANES - GLA_GATE_RANK, D_MODEL), F32)], axis=0)
    wb_ref[:, C_GKL:D_PROJ] = low.T.astype(BF16)


def _row_gather_copy(yb_ref, buf_ref, sem_ref, slot, k, r, d):
    return pltpu.make_async_copy(yb_ref.at[d], buf_ref.at[slot, k, r], sem_ref.at[slot])


def _gather_start(dest_ref, yb_ref, buf_ref, sem_ref, slot, rows=range(TM)):
    for r in rows:
        for k in range(TOP_K):
            _row_gather_copy(yb_ref, buf_ref, sem_ref, slot, k, r,
                             dest_ref[0, 0, k * TM + r]).start(priority=k)


def _gather_wait(yb_ref, buf_ref, sem_ref, slot):
    for k in range(TOP_K):
        pltpu.make_async_copy(yb_ref.at[pl.ds(0, TM)], buf_ref.at[slot, k], sem_ref.at[slot]).wait()


def _combined_residual(dcur_ref, x_ref, route_ref, yb_ref, buf_ref, sem_ref):
    i = pl.program_id(0)
    slot = lax.rem(i, 2)

    @pl.when(i == 0)
    def _():
        _gather_start(dcur_ref, yb_ref, buf_ref, sem_ref, 0)

    _gather_wait(yb_ref, buf_ref, sem_ref, slot)
    w0 = route_ref[:, R_W:R_W + 1]
    w1 = route_ref[:, R_W + 1:R_W + 2]
    y0 = _row_tiles_chunks(buf_ref.at[slot, 0], TM)
    y1 = _row_tiles_chunks(buf_ref.at[slot, 1], TM)
    return jnp.concatenate(
        [x_ref[:, c * LANES:(c + 1) * LANES] + (w0 * y0[c] + w1 * y1[c]) for c in range(ROW_TILES)],
        axis=1)


def _prefetch_groups(n_groups):
    per = -(-TM // n_groups)
    return [range(g * per, min(TM, (g + 1) * per)) for g in range(n_groups)]


def _drain_last_prefetch(yb_ref, buf_ref, sem_ref):
    i = pl.program_id(0)

    @pl.when(i == pl.num_programs(0) - 1)
    def _():
        _gather_wait(yb_ref, buf_ref, sem_ref, 1 - lax.rem(i, 2))


def _combine_specs(n_tiles):
    smem_tile = lambda f: pl.BlockSpec((1, 1, TOP_K * TM), f, memory_space=pltpu.SMEM)
    return [
        smem_tile(lambda i: (i, 0, 0)),
        smem_tile(lambda i: (jnp.minimum(i + 1, n_tiles - 1), 0, 0)),
        pl.BlockSpec((TM, D_MODEL), lambda i: (i, 0)),
        pl.BlockSpec((TM, LANES), lambda i: (i, 0)),
        pl.BlockSpec(memory_space=pl.ANY),
    ]


_COMBINE_SCRATCH = [pltpu.VMEM((2, TOP_K, TM, ROW_TILES, LANES), F32),
                    pltpu.SemaphoreType.DMA((2,))]


def _combine_final_norm_kernel(dcur_ref, dnxt_ref, x_ref, route_ref, yb_ref, gain_ref,
                               o_ref, buf_ref, sem_ref):
    _gather_start(dnxt_ref, yb_ref, buf_ref, sem_ref, 1 - lax.rem(pl.program_id(0), 2))
    x = _combined_residual(dcur_ref, x_ref, route_ref, yb_ref, buf_ref, sem_ref)
    o_ref[...] = _rms(x, gain_ref[...])
    _drain_last_prefetch(yb_ref, buf_ref, sem_ref)


def _combine_final_norm(dest_tiles, x2, route, yb, gain):
    T = x2.shape[0]
    n_tiles = T // TM
    return pl.pallas_call(
        _combine_final_norm_kernel,
        grid=(n_tiles,),
        in_specs=_combine_specs(n_tiles) + [pl.BlockSpec((1, D_MODEL), lambda i: (0, 0))],
        out_specs=pl.BlockSpec((TM, D_MODEL), lambda i: (i, 0)),
        out_shape=jax.ShapeDtypeStruct((T, D_MODEL), F32),
        scratch_shapes=_COMBINE_SCRATCH,
        compiler_params=pltpu.CompilerParams(
            dimension_semantics=("arbitrary",), vmem_limit_bytes=VMEM_LIMIT),
        name="combine_final_norm",
    )(dest_tiles, dest_tiles, x2, route, yb, gain)


def _gelu_tanh(x):
    c = 0.7978845608028654
    return x * (0.5 * (1.0 + jnp.tanh(c * (x + 0.044715 * (x * x * x)))))


def _mixer_kernel(proj_ref, wgk_ref, bgk_ref, glan_ref, gmn_ref, wsp_ref, bsp_ref, wconv_ref,
                  out_ref, st_ref, hc_ref, lcat_ref, wm_ref, *, seq_start, first_step, between):
    TS = TS_MIX
    n_gla = TS // GLA_CHUNK
    n_gm = TS // GMLP_CHUNK

    @pl.when(seq_start)
    def _():
        st_ref[...] = jnp.zeros_like(st_ref)
        hc_ref[...] = jnp.zeros_like(hc_ref)

    @pl.when(first_step)
    def _():
        r = lax.broadcasted_iota(jnp.int32, (TS, TS), 0)
        c = lax.broadcasted_iota(jnp.int32, (TS, TS), 1)
        keep = ((r // GLA_CHUNK) == (c // GLA_CHUNK)) & (c <= r)
        lcat_ref[...] = jnp.where(keep, 1.0, 0.0).astype(BF16)
        t = lax.broadcasted_iota(jnp.int32, (GMLP_CHUNK, GMLP_HEADS * GMLP_CHUNK), 0)
        s = lax.broadcasted_iota(jnp.int32, (GMLP_CHUNK, GMLP_HEADS * GMLP_CHUNK), 1) % GMLP_CHUNK
        wm_ref[...] = jnp.where(s <= t, wsp_ref[...], 0.0).astype(BF16)

    lane256 = lax.broadcasted_iota(jnp.int32, (1, GLA_KDIM), 1)

    q = proj_ref[:, C_Q:C_Q + GLA_KDIM].astype(F32)
    k = proj_ref[:, C_K:C_K + GLA_KDIM].astype(F32)
    v_b = proj_ref[:, C_V:C_V + GLA_WIDTH]
    z = _dot(proj_ref[:, C_GKL:C_GKL + LANES], wgk_ref[...]) + bgk_ref[...]
    gk = (jnp.minimum(z, 0.0) - jnp.log1p(jnp.exp(-jnp.abs(z)))) * (1.0 / GLA_GATE_NORM)
    gk_hi, gk_lo = _split_bf16(gk)
    cs = _dot(lcat_ref[...], jnp.concatenate([gk_hi, gk_lo], axis=1))
    b = cs[:, :GLA_KDIM] + cs[:, GLA_KDIM:]
    b_last = [b[(c + 1) * GLA_CHUNK - 1:(c + 1) * GLA_CHUNK, :] for c in range(n_gla)]
    bl = jnp.concatenate(
        [jnp.broadcast_to(t, (GLA_CHUNK, GLA_KDIM)) for t in b_last], axis=0)
    q_dec = (q * (GLA_DK ** -0.5)) * jnp.exp(b)
    k_inv = (k * jnp.exp(-b)).astype(BF16)
    k_dec = (k * jnp.exp(bl - b)).astype(BF16)
    q_dec_b = q_dec.astype(BF16)

    zero_b = jnp.zeros_like(q_dec_b)
    q_stack = jnp.concatenate(
        [jnp.where((lane256 // GLA_DK) == h, q_dec_b, zero_b) for h in range(GLA_HEADS)], axis=0)
    scores = lax.dot_general(q_stack, k_inv, (((1,), (1,)), ((), ())),
                             preferred_element_type=F32)
    rt = lax.broadcasted_iota(jnp.int32, (TS, TS), 0)
    ct = lax.broadcasted_iota(jnp.int32, (TS, TS), 1)
    causal = ((rt // GLA_CHUNK) == (ct // GLA_CHUNK)) & (ct <= rt)
    o_heads = []
    for h in range(GLA_HEADS):
        p_h = jnp.where(causal, scores[h * TS:(h + 1) * TS, :], 0.0).astype(BF16)
        o_heads.append(_dot(p_h, v_b[:, h * GLA_DV:(h + 1) * GLA_DV]))
        between("copy")

    sr = lax.broadcasted_iota(jnp.int32, (GLA_WIDTH, GLA_KDIM), 0) // GLA_DV
    sc = lax.broadcasted_iota(jnp.int32, (GLA_WIDTH, GLA_KDIM), 1) // GLA_DK
    bd_mask = sr == sc
    o_inter = []
    for c in range(n_gla):
        rows = slice(c * GLA_CHUNK, (c + 1) * GLA_CHUNK)
        st = st_ref[...]
        o_inter.append(lax.dot_general(q_dec_b[rows], st.astype(BF16), (((1,), (1,)), ((), ())),
                                       preferred_element_type=F32))
        upd = lax.dot_general(v_b[rows], k_dec[rows], (((0,), (0,)), ((), ())),
                              preferred_element_type=F32)
        decay = jnp.exp(b_last[c])
        st_ref[...] = st * decay + jnp.where(bd_mask, upd, 0.0)
        between("copy")
    o_inter = jnp.concatenate(o_inter, axis=0)

    for h in range(GLA_HEADS):
        cols = slice(h * GLA_DV, (h + 1) * GLA_DV)
        o = o_heads[h] + o_inter[:, cols]
        o = o * lax.rsqrt(jnp.mean(o * o, axis=-1, keepdims=True) + RMS_EPS) * glan_ref[...]
        g = proj_ref[:, C_G + h * GLA_DV:C_G + (h + 1) * GLA_DV].astype(F32)
        out_ref[:, cols] = (o * (g * (1.0 / (1.0 + jnp.exp(-g))))).astype(out_ref.dtype)
        between("copy")

    between("gla_done")
    u = _gelu_tanh(proj_ref[:, C_U:C_U + GMLP_WIDTH].astype(F32))
    vg = _gelu_tanh(proj_ref[:, C_VG:C_VG + GMLP_WIDTH].astype(F32))
    hr = lax.broadcasted_iota(jnp.int32, (GMLP_WIDTH, GMLP_WIDTH), 0) // GMLP_DH
    hcn = lax.broadcasted_iota(jnp.int32, (GMLP_WIDTH, GMLP_WIDTH), 1) // GMLP_DH
    head_mean = jnp.where(hr == hcn, 1.0 / GMLP_DH, 0.0).astype(BF16)
    sq_hi, sq_lo = _split_bf16(vg * vg)
    ms = _dot(sq_hi, head_mean) + _dot(sq_lo, head_mean)
    v32 = vg * lax.rsqrt(ms + RMS_EPS) * gmn_ref[...]
    for c in range(n_gm):
        rows = slice(c * GMLP_CHUNK, (c + 1) * GMLP_CHUNK)
        vc = v32[rows].astype(BF16)
        zc = jnp.zeros_like(vc)
        rhs = jnp.concatenate(
            [jnp.where((lane256 // GMLP_DH) == h, vc, zc) for h in range(GMLP_HEADS)], axis=0)
        mixed = _dot(wm_ref[...], rhs) + bsp_ref[...]
        out_ref[rows, GLA_WIDTH:GLA_WIDTH + GMLP_WIDTH] = (u[rows] * mixed).astype(out_ref.dtype)

    between("gmlp_done")
    hcv = (proj_ref[:, C_CG:C_CG + CONV_WIDTH].astype(F32)
           * proj_ref[:, C_X:C_X + CONV_WIDTH].astype(F32))
    hc_ref[8:8 + TS, :] = hcv
    y = (wconv_ref[2:3, :] * hcv + wconv_ref[1:2, :] * hc_ref[7:7 + TS, :]
         + wconv_ref[0:1, :] * hc_ref[6:6 + TS, :])
    out_ref[:, GLA_WIDTH + GMLP_WIDTH:] = (
        proj_ref[:, C_BG:C_BG + CONV_WIDTH].astype(F32) * y).astype(out_ref.dtype)
    hc_ref[0:8, :] = hc_ref[TS:TS + 8, :]


_MIXER_SCRATCH = [
    pltpu.VMEM((GLA_WIDTH, GLA_KDIM), F32),
    pltpu.VMEM((TS_MIX + 8, CONV_WIDTH), F32),
    pltpu.VMEM((TS_MIX, TS_MIX), BF16),
    pltpu.VMEM((GMLP_CHUNK, GMLP_HEADS * GMLP_CHUNK), BF16),
]


N_MIX_PARAMS = 7
PROJ_CHUNKS_AT = {"gla_done": 4, "gmlp_done": 4}
N_COPY_SITES = 14


def _front_kernel(*refs, tiles_per_seq, combine):
    refs = list(refs)
    if combine:
        dcur_ref, dnxt_ref, x_ref, route_ref, yb_ref = refs[:5]
        del refs[:5]
    else:
        x_ref = refs.pop(0)
    gain_ref, wt_ref = refs[:2]
    mix_refs = refs[2:2 + N_MIX_PARAMS]
    del refs[:2 + N_MIX_PARAMS]
    if combine:
        xo_ref, out_ref, buf_ref, sem_ref = refs[:4]
        del refs[:4]
    else:
        out_ref = refs.pop(0)
    st_ref, hc_ref, lcat_ref, wm_ref, wb_ref, pcur_ref, pnext_ref = refs
    s = pl.program_id(0)

    @pl.when(s == 0)
    def _():
        _stage_w_in(wt_ref, wb_ref)
        pcur_ref[...] = jnp.zeros_like(pcur_ref)

    col_chunks = [(c0, min(c0 + PROJ_CHUNK, D_PROJ)) for c0 in range(0, D_PROJ, PROJ_CHUNK)]
    work = list(col_chunks)
    copy_groups = _prefetch_groups(N_COPY_SITES)

    def project(h, n):
        for _ in range(min(n, len(work))):
            c0, c1 = work.pop(0)
            pnext_ref[:, c0:c1] = _dot(h, wb_ref[:, c0:c1]).astype(BF16)

    def issue(n=1):
        for _ in range(min(n, len(copy_groups))):
            _gather_start(dnxt_ref, yb_ref, buf_ref, sem_ref, 1 - lax.rem(s, 2), copy_groups.pop(0))

    mixers = functools.partial(
        _mixer_kernel, pcur_ref, *mix_refs, out_ref, st_ref, hc_ref, lcat_ref, wm_ref,
        seq_start=lax.rem(jnp.maximum(s - 1, 0), tiles_per_seq) == 0, first_step=s == 0)
    if combine:
        x = _combined_residual(dcur_ref, x_ref, route_ref, yb_ref, buf_ref, sem_ref)
        xo_ref[...] = x
        h = _rms(x, gain_ref[...]).astype(BF16)

        def at_site(site):
            issue()
            project(h, PROJ_CHUNKS_AT.get(site, 0))

        mixers(between=at_site)
        issue(len(copy_groups))
        project(h, len(work))
        _drain_last_prefetch(yb_ref, buf_ref, sem_ref)
    else:
        mixers(between=lambda site: None)
        project(_rms(x_ref[...], gain_ref[...]).astype(BF16), len(work))
    pcur_ref[...] = pnext_ref[...]


def _front(x, gain, w_in_t, layer, mix_params, batch, seq, moe=None):
    n_seq = seq // TS_MIX
    n = batch * n_seq
    T = batch * seq
    cur = lambda s: jnp.minimum(s, n - 1)
    full = lambda shape: pl.BlockSpec(shape, lambda s: (0,) * len(shape))
    row = lambda w, f: pl.BlockSpec((TM, w), lambda s: (f(s), 0))
    in_specs, args = [row(D_MODEL, cur)], [x]
    out_specs = [row(D_MODEL, lambda s: jnp.maximum(s - 1, 0))]
    out_shape = [jax.ShapeDtypeStruct((T, D_MODEL), BF16)]
    scratch = list(_MIXER_SCRATCH)
    if moe is not None:
        smem_tile = lambda f: pl.BlockSpec((1, 1, TOP_K * TM), lambda s: (f(s), 0, 0),
                                           memory_space=pltpu.SMEM)
        in_specs = [smem_tile(cur), smem_tile(lambda s: jnp.minimum(s + 1, n - 1))] + in_specs + [
            row(LANES, cur), pl.BlockSpec(memory_space=pl.ANY)]
        args = [moe["dest_tiles"], moe["dest_tiles"]] + args + [moe["route"], moe["yb"]]
        out_specs = [row(D_MODEL, cur)] + out_specs
        out_shape = [jax.ShapeDtypeStruct((T, D_MODEL), F32)] + out_shape
        scratch = _COMBINE_SCRATCH + scratch
    in_specs += [
        full((1, D_MODEL)),
        pl.BlockSpec((1, D_IN, D_MODEL), lambda s: (layer, 0, 0), pipeline_mode=pl.Buffered(1)),
        full((LANES, GLA_KDIM)), full((1, GLA_KDIM)), full((1, GLA_DV)), full((1, GMLP_WIDTH)),
        full((GMLP_CHUNK, GMLP_HEADS * GMLP_CHUNK)), full((GMLP_CHUNK, GMLP_WIDTH)),
        full((8, CONV_WIDTH)),
    ]
    scratch += [pltpu.VMEM((D_MODEL, D_PROJ), BF16), pltpu.VMEM((TS_MIX, D_PROJ), BF16),
                pltpu.VMEM((TS_MIX, D_PROJ), BF16)]
    return pl.pallas_call(
        functools.partial(_front_kernel, tiles_per_seq=n_seq, combine=moe is not None),
        grid=(n + 1,),
        in_specs=in_specs,
        out_specs=out_specs if moe is not None else out_specs[0],
        out_shape=out_shape if moe is not None else out_shape[0],
        scratch_shapes=scratch,
        compiler_params=pltpu.CompilerParams(
            dimension_semantics=("arbitrary",), vmem_limit_bytes=VMEM_LIMIT),
        name="front",
    )(*args, gain, w_in_t, *mix_params)


def _row_tiles_store(tiles_ref, x):
    rows = x.shape[0]
    flat = tiles_ref.reshape(rows * ROW_TILES, LANES)
    for c in range(ROW_TILES):
        flat[pl.ds(c, rows, stride=ROW_TILES), :] = x[:, c * LANES:(c + 1) * LANES]


def _row_tiles_chunks(tiles_ref, rows):
    flat = tiles_ref.reshape(rows * ROW_TILES, LANES)
    return [flat[pl.ds(c, rows, stride=ROW_TILES), :] for c in range(ROW_TILES)]


IN_SLOTS = 3


def _out_router_kernel(mix_hbm, x_hbm, wo_ref, gain_ref, wrc_ref, br_ref,
                       x2_ref, h2_ref, route_ref, route_t_ref, cnt_ref, tri_ref, wob_ref, lg_ref,
                       mixbuf_ref, xbuf_ref, insem_ref):
    i = pl.program_id(0)
    n = pl.num_programs(0) - 1

    def in_copies(t):
        slot = lax.rem(t, IN_SLOTS)
        rows = pl.ds(t * TM, TM)
        return [pltpu.make_async_copy(mix_hbm.at[rows], mixbuf_ref.at[slot], insem_ref.at[slot]),
                pltpu.make_async_copy(x_hbm.at[rows], xbuf_ref.at[slot], insem_ref.at[slot])]

    @pl.when(i == 0)
    def _():
        for t in range(IN_SLOTS - 1):
            for c in in_copies(t):
                c.start()

    @pl.when(i + IN_SLOTS - 1 < n)
    def _():
        for c in in_copies(i + IN_SLOTS - 1):
            c.start()

    @pl.when(i < n)
    def _():
        for c in in_copies(i):
            c.wait()

    tile_slot = lax.rem(jnp.minimum(i, n - 1), IN_SLOTS)
    mix_ref = mixbuf_ref.at[tile_slot]
    x_ref = xbuf_ref.at[tile_slot]

    @pl.when(i == 0)
    def _():
        cnt_ref[...] = jnp.zeros_like(cnt_ref)
        lg_ref[...] = jnp.zeros_like(lg_ref)
        r = lax.broadcasted_iota(jnp.int32, (TM, TM), 0)
        c = lax.broadcasted_iota(jnp.int32, (TM, TM), 1)
        tri_ref[...] = jnp.where(c < r, 1.0, 0.0).astype(BF16)
        for r0 in range(0, D_MODEL, W_PREP_ROWS):
            wob_ref[r0:r0 + W_PREP_ROWS, :] = wo_ref[0, r0:r0 + W_PREP_ROWS, :].astype(BF16)

    lg = lg_ref[...]
    half = D_MODEL // 2
    mix = mix_ref[...]
    x2_a = x_ref[:, :half] + _dot(mix, wob_ref[:, :half])

    lane = lax.broadcasted_iota(jnp.int32, (TM, LANES), 1).astype(F32)
    neg = -jnp.inf
    is_g = lane < N_GROUPS
    gl = jnp.where(is_g, lg, neg)
    gmax = jnp.max(gl, axis=1, keepdims=True)
    g_top = jnp.min(jnp.where(gl == gmax, lane, float(LANES)), axis=1, keepdims=True)
    g_w = 1.0 / jnp.sum(jnp.where(is_g, jnp.exp(lg - gmax), 0.0), axis=1, keepdims=True)
    first = N_GROUPS + EXPERTS_PER_GROUP * g_top
    el = jnp.where((lane >= first) & (lane < first + EXPERTS_PER_GROUP), lg, neg)
    m1 = jnp.max(el, axis=1, keepdims=True)
    i1 = jnp.min(jnp.where(el == m1, lane, float(LANES)), axis=1, keepdims=True)
    el2 = jnp.where(lane == i1, neg, el)
    m2 = jnp.max(el2, axis=1, keepdims=True)
    i2 = jnp.min(jnp.where(el2 == m2, lane, float(LANES)), axis=1, keepdims=True)
    ratio = jnp.exp(m2 - m1)
    w1 = g_w / (1.0 + ratio)
    w2 = w1 * ratio

    x2_b = x_ref[:, half:] + _dot(mix, wob_ref[:, half:])

    oh1 = jnp.where(lane == i1, 1.0, 0.0)
    oh2 = jnp.where(lane == i2, 1.0, 0.0)
    oh = jnp.where(i > 0, oh1 + oh2, 0.0)
    before = _dot(tri_ref[...], oh.astype(BF16)) + cnt_ref[0:1, :]
    rank1 = jnp.sum(oh1 * before, axis=1, keepdims=True)
    rank2 = jnp.sum(oh2 * before, axis=1, keepdims=True)
    cnt_ref[...] = cnt_ref[...] + jnp.sum(oh, axis=0, keepdims=True)

    rec = jnp.zeros((TM, LANES), F32)
    for col, val in ((R_E, i1 - N_GROUPS), (R_E + 1, i2 - N_GROUPS), (R_RANK, rank1),
                     (R_RANK + 1, rank2), (R_W, w1), (R_W + 1, w2)):
        rec = jnp.where(lane == col, val, rec)
    route_ref[...] = rec
    route_t_ref[0] = rec.T[0:SUBLANES, :]

    x2 = jnp.concatenate([x2_a, x2_b], axis=1)
    x2_ref[...] = x2
    h = _rms(x2, gain_ref[...])
    h_hi, h_lo = _split_bf16(h)
    h2_ref[...] = h_hi
    hh_hl = _dot(h_hi, wrc_ref[...])
    lg_ref[...] = (hh_hl[:, :ROUTER_COLS] + hh_hl[:, ROUTER_COLS:]
                   + _dot(h_lo, wrc_ref[:, :ROUTER_COLS]) + br_ref[...])


def _out_router(mixed, x, w_out, layer, gain, wr_cat, br):
    T = x.shape[0]
    n = T // TM
    row = lambda w: pl.BlockSpec((TM, w), lambda i: (jnp.minimum(i, n - 1), 0))
    lag = lambda i: jnp.maximum(i - 1, 0)
    full = lambda shape: pl.BlockSpec(shape, lambda i: (0, 0))
    wo_spec = pl.BlockSpec((1, D_MODEL, D_MODEL), lambda i: (layer, 0, 0),
                           pipeline_mode=pl.Buffered(1))
    return pl.pallas_call(
        _out_router_kernel,
        grid=(n + 1,),
        in_specs=[pl.BlockSpec(memory_space=pl.ANY), pl.BlockSpec(memory_space=pl.ANY), wo_spec,
                  full((1, D_MODEL)), full((D_MODEL, 2 * ROUTER_COLS)), full((1, ROUTER_COLS))],
        out_specs=[row(D_MODEL), row(D_MODEL),
                   pl.BlockSpec((TM, LANES), lambda i: (lag(i), 0)),
                   pl.BlockSpec((1, SUBLANES, TM), lambda i: (lag(i), 0, 0)), full((8, LANES))],
        out_shape=[jax.ShapeDtypeStruct((T, D_MODEL), F32),
                   jax.ShapeDtypeStruct((T, D_MODEL), BF16),
                   jax.ShapeDtypeStruct((T, LANES), F32),
                   jax.ShapeDtypeStruct((T // TM, SUBLANES, TM), F32),
                   jax.ShapeDtypeStruct((8, LANES), F32)],
        scratch_shapes=[pltpu.VMEM((TM, TM), BF16), pltpu.VMEM((D_MODEL, D_MODEL), BF16),
                        pltpu.VMEM((TM, ROUTER_COLS), F32),
                        pltpu.VMEM((IN_SLOTS, TM, D_MODEL), BF16),
                        pltpu.VMEM((IN_SLOTS, TM, D_MODEL), F32),
                        pltpu.SemaphoreType.DMA((IN_SLOTS,))],
        compiler_params=pltpu.CompilerParams(
            dimension_semantics=("arbitrary",), vmem_limit_bytes=VMEM_LIMIT),
        name="out_router",
    )(mixed, x, w_out, gain, wr_cat, br)


def _dispatch_kernel(fill_ref, nu_ref, dest_ref, h_ref, xb_ref, zero_ref, sem_ref, zsem_ref,
                     stage_ref):
    i = pl.program_id(0)
    par = lax.rem(i, 2)
    last = i == pl.num_programs(0) - 1
    n_blocks = xb_ref.shape[0] // MOE_BLK
    spare_fills = [(j >= nu_ref[0], pltpu.make_async_copy(
        zero_ref, xb_ref.at[pl.ds(j * MOE_BLK, MOE_BLK)], zsem_ref.at[1]))
        for j in range(n_blocks - N_EXPERTS, n_blocks)]

    @pl.when(i == 0)
    def _():
        zero_ref[...] = jnp.zeros_like(zero_ref)
        fills = [(fill_ref[e] >= 0, pltpu.make_async_copy(
            zero_ref, xb_ref.at[pl.ds(pl.multiple_of(jnp.maximum(fill_ref[e], 0), MOE_BLK), MOE_BLK)],
            zsem_ref.at[0])) for e in range(N_EXPERTS)]
        for cond, f in fills + spare_fills:
            pl.when(cond)(f.start)
        for cond, f in fills:
            pl.when(cond)(f.wait)

    _row_tiles_store(stage_ref.at[par], h_ref[...].astype(F32))
    for r in range(TM):
        for k in range(TOP_K):
            pltpu.make_async_copy(stage_ref.at[par, r], xb_ref.at[dest_ref[0, 0, k * TM + r]],
                                  sem_ref.at[par]).start(priority=k)

    def wait_tile(p):
        for _ in range(TOP_K):
            pltpu.make_async_copy(stage_ref.at[p], xb_ref.at[pl.ds(0, TM)], sem_ref.at[p]).wait()

    pl.when(i > 0)(lambda: wait_tile(1 - par))
    @pl.when(last)
    def _():
        wait_tile(par)
        for cond, f in spare_fills:
            pl.when(cond)(f.wait)


def _dispatch(fill_start, n_used, dest_tiles, h2, n_rows):
    T = h2.shape[0]
    grid_spec = pltpu.PrefetchScalarGridSpec(
        num_scalar_prefetch=2,
        grid=(T // TM,),
        in_specs=[
            pl.BlockSpec((1, 1, TOP_K * TM), lambda i, fs, nu: (i, 0, 0), memory_space=pltpu.SMEM),
            pl.BlockSpec((TM, D_MODEL), lambda i, fs, nu: (i, 0)),
        ],
        out_specs=pl.BlockSpec(memory_space=pl.ANY),
        scratch_shapes=[pltpu.VMEM((MOE_BLK, ROW_TILES, LANES), F32),
                        pltpu.SemaphoreType.DMA((2,)), pltpu.SemaphoreType.DMA((2,)),
                        pltpu.VMEM((2, TM, ROW_TILES, LANES), F32)],
    )
    return pl.pallas_call(
        _dispatch_kernel,
        grid_spec=grid_spec,
        out_shape=jax.ShapeDtypeStruct((n_rows, ROW_TILES, LANES), F32),
        compiler_params=pltpu.CompilerParams(dimension_semantics=("arbitrary",)),
        name="dispatch",
    )(fill_start, n_used, dest_tiles, h2)


BLOCK_COPY_PARTS = 4
X_SLOTS = 4
Y_SLOTS = 3


class _CopyGroup:
    def __init__(self, copies):
        self.copies = copies

    def start(self):
        for n, c in enumerate(self.copies):
            c.start(priority=n % 2)

    def wait(self):
        for c in self.copies:
            c.wait()


def _expert_kernel(be_ref, nxt_ref, nu_ref, half_ref, xb_ref, wg_ref, wu_ref, wd_ref, yb_ref,
                   xbuf_ref, ybuf_ref, wgs_ref, wus_ref, wds_ref, wgb_ref, wub_ref, wdb_ref,
                   xsem_ref, ysem_ref, wsem_ref, zbuf_ref, zsem_ref, *, layer, n_blocks):
    n_used = nu_ref[0]

    part = MOE_BLK // BLOCK_COPY_PARTS

    def x_copy(j, slot):
        return _CopyGroup([pltpu.make_async_copy(
            xb_ref.at[pl.ds(j * MOE_BLK + p * part, part)],
            xbuf_ref.at[slot, pl.ds(p * part, part)], xsem_ref.at[slot])
            for p in range(BLOCK_COPY_PARTS)])

    def y_copy(j, slot):
        return _CopyGroup([pltpu.make_async_copy(
            ybuf_ref.at[slot, pl.ds(p * part, part)],
            yb_ref.at[pl.ds(j * MOE_BLK + p * part, part)], ysem_ref.at[slot])
            for p in range(BLOCK_COPY_PARTS)])

    def w_copies(e, ws):
        return [pltpu.make_async_copy(src.at[layer, e], dst.at[ws], wsem_ref.at[ws])
                for src, dst in ((wg_ref, wgs_ref), (wu_ref, wus_ref), (wd_ref, wds_ref))]

    def fill_copy(j):
        return pltpu.make_async_copy(zbuf_ref, yb_ref.at[pl.ds(j * MOE_BLK, MOE_BLK)], zsem_ref)

    for j0 in range(X_SLOTS - 1):
        x_copy(j0, j0).start()
    for c in w_copies(be_ref[0], 0):
        c.start()

    zbuf_ref[...] = jnp.zeros_like(zbuf_ref)

    def fill(j, carry):
        fill_copy(j).start()
        return carry

    lax.fori_loop(n_used, n_blocks, fill, 0)

    def block(j, ws):
        slot = lax.rem(j, Y_SLOTS)
        xslot = lax.rem(j, X_SLOTS)
        first = (j == 0) | (be_ref[j] != be_ref[jnp.maximum(j - 1, 0)])
        ws = jnp.where(first & (j > 0), 1 - ws, ws)

        @pl.when(first)
        def _():
            for c in w_copies(be_ref[j], ws):
                c.wait()
            for r0 in range(0, D_MODEL, W_PREP_ROWS):
                rows = slice(r0, r0 + W_PREP_ROWS)
                wgb_ref[rows, :] = wgs_ref[ws, rows, :].astype(BF16)
                wub_ref[rows, :] = wus_ref[ws, rows, :].astype(BF16)
            for r0 in range(0, D_EXPERT, W_PREP_ROWS):
                rows = slice(r0, r0 + W_PREP_ROWS)
                wdb_ref[rows, :] = wds_ref[ws, rows, :].astype(BF16)

            @pl.when(nxt_ref[j] >= 0)
            def _():
                for c in w_copies(nxt_ref[j], 1 - ws):
                    c.start()

        ahead = j + X_SLOTS - 1

        @pl.when(ahead < n_used)
        def _():
            x_copy(ahead, lax.rem(ahead, X_SLOTS)).start()

        x_copy(j, xslot).wait()

        @pl.when(j >= Y_SLOTS)
        def _():
            y_copy(j - Y_SLOTS, slot).wait()

        def mlp(rows):
            x = jnp.concatenate(
                [c.astype(BF16) for c in _row_tiles_chunks(xbuf_ref.at[xslot, pl.ds(0, rows)], rows)],
                axis=1)
            g = _dot(x, wgb_ref[...])
            u = _dot(x, wub_ref[...])
            h = (g * (1.0 / (1.0 + jnp.exp(-g)))) * u
            _row_tiles_store(ybuf_ref.at[slot, pl.ds(0, rows)], _dot(h.astype(BF16), wdb_ref[...]))

        half_full = half_ref[j] == 1

        @pl.when(half_full)
        def _():
            mlp(MOE_BLK // 2)
            ybuf_ref[slot, pl.ds(MOE_BLK // 2, MOE_BLK // 2)] = jnp.zeros(
                (MOE_BLK // 2, ROW_TILES, LANES), F32)

        @pl.when(jnp.logical_not(half_full))
        def _():
            mlp(MOE_BLK)

        y_copy(j, slot).start()
        return ws

    lax.fori_loop(0, n_used, block, jnp.int32(0))

    for back in range(Y_SLOTS, 0, -1):
        y_copy(n_used - back, lax.rem(n_used - back, Y_SLOTS)).wait()

    def fill_wait(j, carry):
        fill_copy(j).wait()
        return carry

    lax.fori_loop(n_used, n_blocks, fill_wait, 0)


def _experts(blk_exp, nxt_exp, n_used, blk_half, xb, w_gate, w_up, w_down, layer):
    n_blocks = blk_exp.shape[0]
    any_spec = pl.BlockSpec(memory_space=pl.ANY)
    blk = (MOE_BLK, ROW_TILES, LANES)
    grid_spec = pltpu.PrefetchScalarGridSpec(
        num_scalar_prefetch=4,
        grid=(1,),
        in_specs=[any_spec, any_spec, any_spec, any_spec],
        out_specs=any_spec,
        scratch_shapes=[
            pltpu.VMEM((X_SLOTS,) + blk, F32), pltpu.VMEM((Y_SLOTS,) + blk, F32),
            pltpu.VMEM((2, D_MODEL, D_EXPERT), F32), pltpu.VMEM((2, D_MODEL, D_EXPERT), F32),
            pltpu.VMEM((2, D_EXPERT, D_MODEL), F32),
            pltpu.VMEM((D_MODEL, D_EXPERT), BF16), pltpu.VMEM((D_MODEL, D_EXPERT), BF16),
            pltpu.VMEM((D_EXPERT, D_MODEL), BF16),
            pltpu.SemaphoreType.DMA((X_SLOTS,)), pltpu.SemaphoreType.DMA((Y_SLOTS,)),
            pltpu.SemaphoreType.DMA((2,)),
            pltpu.VMEM(blk, F32), pltpu.SemaphoreType.DMA(()),
        ],
    )
    return pl.pallas_call(
        functools.partial(_expert_kernel, layer=layer, n_blocks=n_blocks),
        grid_spec=grid_spec,
        out_shape=jax.ShapeDtypeStruct((n_blocks * MOE_BLK, ROW_TILES, LANES), F32),
        compiler_params=pltpu.CompilerParams(
            dimension_semantics=("arbitrary",), vmem_limit_bytes=VMEM_LIMIT),
        name="experts",
    )(blk_exp, nxt_exp, n_used, blk_half, xb, w_gate, w_up, w_down)


def _dispatch_tables(route_t, counts_rec, T):
    counts = counts_rec[0, N_GROUPS:N_GROUPS + N_EXPERTS].astype(jnp.int32)
    n_steps = (T * TOP_K) // MOE_BLK + N_EXPERTS
    nblk = (counts + MOE_BLK - 1) // MOE_BLK
    bend = jnp.cumsum(nblk)
    pstart = (bend - nblk) * MOE_BLK
    n_used = bend[-1]
    j = jnp.minimum(jnp.arange(n_steps, dtype=jnp.int32), n_used - 1)
    blk_exp = jnp.minimum(jnp.sum(j[:, None] >= bend[None, :], axis=1), N_EXPERTS - 1)
    n_rows = n_steps * MOE_BLK
    last_blk = jnp.where(counts > 0, (bend - 1) * MOE_BLK, -1)
    ids = jnp.arange(N_EXPERTS, dtype=jnp.int32)
    later = (ids[None, :] > ids[:, None]) & (nblk[None, :] > 0)
    nxt_of = jnp.min(jnp.where(later, ids[None, :], N_EXPERTS), axis=1)
    nxt_tab = jnp.where(nxt_of < N_EXPERTS, nxt_of, -1)
    own = blk_exp[:, None] == ids[None, :]
    nxt_exp = jnp.sum(jnp.where(own, nxt_tab[None, :], 0), axis=1)
    seg_end = jnp.sum(jnp.where(own, (pstart + counts)[None, :], 0), axis=1)
    blk_half = (seg_end - j * MOE_BLK <= MOE_BLK // 2).astype(jnp.int32)
    e = route_t[:, R_E:R_E + TOP_K, :].astype(jnp.int32)
    rank = route_t[:, R_RANK:R_RANK + TOP_K, :].astype(jnp.int32)
    seg = jnp.sum(jnp.where(e[..., None] == jnp.arange(N_EXPERTS), pstart, 0), axis=-1)
    dest = jnp.clip(seg + rank, 0, n_steps * MOE_BLK - 1)
    dest_tiles = dest.reshape(T // TM, 1, TOP_K * TM)
    return dict(dest_tiles=dest_tiles, fill_start=last_blk.astype(jnp.int32),
                blk_exp=blk_exp.astype(jnp.int32), nxt_exp=nxt_exp.astype(jnp.int32),
                blk_half=blk_half,
                n_used=n_used.reshape(1).astype(jnp.int32), n_rows=n_rows)


def _prep_layer(l, w_gk_up, b_gk, gla_norm, gmlp_norm, w_spatial, b_spatial, w_conv,
                w_router_group, b_router_group, w_router_expert, b_router_expert):
    wgk = jnp.concatenate(
        [w_gk_up[l], jnp.zeros((LANES - GLA_GATE_RANK, GLA_KDIM), F32)], axis=0).astype(BF16)
    wsp = w_spatial[l].transpose(1, 0, 2).reshape(GMLP_CHUNK, GMLP_HEADS * GMLP_CHUNK)
    bsp = jnp.repeat(b_spatial[l].T, GMLP_DH, axis=1)
    wconv = jnp.concatenate([w_conv[l], jnp.zeros((8 - CONV_K, CONV_WIDTH), F32)], axis=0)
    wr = jnp.concatenate(
        [w_router_group[l], w_router_expert[l],
         jnp.zeros((D_MODEL, ROUTER_COLS - N_GROUPS - N_EXPERTS), F32)], axis=1)
    wr_hi = wr.astype(BF16)
    wr_lo = (wr - wr_hi.astype(F32)).astype(BF16)
    br = jnp.concatenate(
        [b_router_group[l], b_router_expert[l],
         jnp.zeros((ROUTER_COLS - N_GROUPS - N_EXPERTS,), F32)])[None, :]
    return dict(
        wgk=wgk, bgk=b_gk[l][None, :], glan=gla_norm[l][None, :], gmn=gmlp_norm[l][None, :],
        wsp=wsp, bsp=bsp, wconv=wconv, wr_cat=jnp.concatenate([wr_hi, wr_lo], axis=1), br=br)


def kernel(x, attn_norm, w_in, w_gk_up, b_gk, gla_norm, gmlp_norm, w_spatial, b_spatial, w_conv, w_out, ffn_norm, w_router_group, b_router_group, w_router_expert, b_router_expert, w_gate, w_up, w_down, final_norm):
    B, S, D = x.shape
    T = B * S
    depth = w_in.shape[0]
    xr = x.reshape(T, D)
    w_in_t = jnp.swapaxes(w_in, 1, 2)
    moe = None
    for l in range(depth):
        p = _prep_layer(l, w_gk_up, b_gk, gla_norm, gmlp_norm, w_spatial, b_spatial, w_conv,
                        w_router_group, b_router_group, w_router_expert, b_router_expert)
        mix_params = (p["wgk"], p["bgk"], p["glan"], p["gmn"], p["wsp"], p["bsp"], p["wconv"])
        if moe is None:
            mixed = _front(xr, attn_norm[l][None, :], w_in_t, l, mix_params, B, S)
        else:
            xr, mixed = _front(moe["x2"], attn_norm[l][None, :], w_in_t, l, mix_params, B, S, moe)
        x2, h2, route, route_t, counts_rec = _out_router(
            mixed, xr, w_out, l, ffn_norm[l][None, :], p["wr_cat"], p["br"])
        moe = _dispatch_tables(route_t, counts_rec, T)
        xb = _dispatch(moe["fill_start"], moe["n_used"], moe["dest_tiles"], h2, moe["n_rows"])
        yb = _experts(moe["blk_exp"], moe["nxt_exp"], moe["n_used"], moe["blk_half"], xb,
                      w_gate, w_up, w_down, l)
        moe.update(x2=x2, route=route, yb=yb)
    out = _combine_final_norm(moe["dest_tiles"], moe["x2"], moe["route"], moe["yb"],
                              final_norm[None, :])
    return out.reshape(B, S, D)
```

```python
import functools

import jax
import jax.numpy as jnp
from jax import lax
from jax.experimental import pallas as pl
from jax.experimental.pallas import tpu as pltpu

F32 = jnp.float32
BF16 = jnp.bfloat16

D_MODEL = 1024
RMS_EPS = 1e-6
GLA_HEADS = 4
GLA_WIDTH = 512
GLA_DV = 128
GLA_DK = 64
GLA_KDIM = 256
GLA_GATE_RANK = 16
GLA_GATE_NORM = 16.0
GLA_CHUNK = 64
GMLP_HEADS = 4
GMLP_WIDTH = 256
GMLP_DH = 64
GMLP_CHUNK = 128
CONV_WIDTH = 256
CONV_K = 3
N_GROUPS = 4
EXPERTS_PER_GROUP = 8
N_EXPERTS = 32
TOP_K = 2
D_EXPERT = 256

LANES = 128
C_Q, C_K, C_V, C_G = 0, 256, 512, 1024
C_U, C_VG, C_X, C_BG, C_CG, C_GKL = 1536, 1792, 2048, 2304, 2560, 2816
D_PROJ = C_GKL + LANES
D_IN = C_GKL + GLA_GATE_RANK

TM = 256
TS_MIX = TM
MOE_BLK = 256
ROUTER_COLS = LANES
SUBLANES = 8
ROW_TILES = D_MODEL // LANES
assert ROW_TILES == SUBLANES
VMEM_LIMIT = 56 * 1024 * 1024
R_E, R_RANK, R_W = 0, 2, 4


def _dot(a, b):
    return jnp.dot(a, b, preferred_element_type=F32)


def _split_bf16(x):
    hi = x.astype(BF16)
    lo = (x - hi.astype(F32)).astype(BF16)
    return hi, lo


def _rms(x, gain):
    return x * lax.rsqrt(jnp.mean(x * x, axis=-1, keepdims=True) + RMS_EPS) * gain


W_PREP_ROWS = 128
PROJ_CHUNK = 256


def _stage_w_in(wt_ref, wb_ref):
    for c0 in range(0, C_GKL, LANES):
        src = c0 if c0 < C_U else c0 + GLA_GATE_RANK
        wb_ref[:, c0:c0 + LANES] = wt_ref[0, src:src + LANES, :].T.astype(BF16)
    low = jnp.concatenate([wt_ref[0, C_U:C_U + GLA_GATE_RANK, :],
                           jnp.zeros((LANES - GLA_GATE_RANK, D_MODEL), F32)], axis=0)
    wb_ref[:, C_GKL:D_PROJ] = low.T.astype(BF16)


def _row_gather_copy(yb_ref, buf_ref, sem_ref, slot, k, r, d):
    return pltpu.make_async_copy(yb_ref.at[d], buf_ref.at[slot, k, r], sem_ref.at[slot])


def _gather_start(dest_ref, yb_ref, buf_ref, sem_ref, slot, rows=range(TM)):
    for r in rows:
        for k in range(TOP_K):
            _row_gather_copy(yb_ref, buf_ref, sem_ref, slot, k, r,
                             dest_ref[0, 0, k * TM + r]).start(priority=k)


def _gather_wait(yb_ref, buf_ref, sem_ref, slot):
    for k in range(TOP_K):
        pltpu.make_async_copy(yb_ref.at[pl.ds(0, TM)], buf_ref.at[slot, k], sem_ref.at[slot]).wait()


def _combined_residual(dcur_ref, x_ref, route_ref, yb_ref, buf_ref, sem_ref):
    i = pl.program_id(0)
    slot = lax.rem(i, 2)

    @pl.when(i == 0)
    def _():
        _gather_start(dcur_ref, yb_ref, buf_ref, sem_ref, 0)

    _gather_wait(yb_ref, buf_ref, sem_ref, slot)
    w0 = route_ref[:, R_W:R_W + 1]
    w1 = route_ref[:, R_W + 1:R_W + 2]
    y0 = _row_tiles_chunks(buf_ref.at[slot, 0], TM)
    y1 = _row_tiles_chunks(buf_ref.at[slot, 1], TM)
    return jnp.concatenate(
        [x_ref[:, c * LANES:(c + 1) * LANES] + (w0 * y0[c] + w1 * y1[c]) for c in range(ROW_TILES)],
        axis=1)


def _prefetch_groups(n_groups):
    per = -(-TM // n_groups)
    return [range(g * per, min(TM, (g + 1) * per)) for g in range(n_groups)]


def _drain_last_prefetch(yb_ref, buf_ref, sem_ref):
    i = pl.program_id(0)

    @pl.when(i == pl.num_programs(0) - 1)
    def _():
        _gather_wait(yb_ref, buf_ref, sem_ref, 1 - lax.rem(i, 2))


def _combine_specs(n_tiles):
    smem_tile = lambda f: pl.BlockSpec((1, 1, TOP_K * TM), f, memory_space=pltpu.SMEM)
    return [
        smem_tile(lambda i: (i, 0, 0)),
        smem_tile(lambda i: (jnp.minimum(i + 1, n_tiles - 1), 0, 0)),
        pl.BlockSpec((TM, D_MODEL), lambda i: (i, 0)),
        pl.BlockSpec((TM, LANES), lambda i: (i, 0)),
        pl.BlockSpec(memory_space=pl.ANY),
    ]


_COMBINE_SCRATCH = [pltpu.VMEM((2, TOP_K, TM, ROW_TILES, LANES), F32),
                    pltpu.SemaphoreType.DMA((2,))]


def _combine_final_norm_kernel(dcur_ref, dnxt_ref, x_ref, route_ref, yb_ref, gain_ref,
                               o_ref, buf_ref, sem_ref):
    _gather_start(dnxt_ref, yb_ref, buf_ref, sem_ref, 1 - lax.rem(pl.program_id(0), 2))
    x = _combined_residual(dcur_ref, x_ref, route_ref, yb_ref, buf_ref, sem_ref)
    o_ref[...] = _rms(x, gain_ref[...])
    _drain_last_prefetch(yb_ref, buf_ref, sem_ref)


def _combine_final_norm(dest_tiles, x2, route, yb, gain):
    T = x2.shape[0]
    n_tiles = T // TM
    return pl.pallas_call(
        _combine_final_norm_kernel,
        grid=(n_tiles,),
        in_specs=_combine_specs(n_tiles) + [pl.BlockSpec((1, D_MODEL), lambda i: (0, 0))],
        out_specs=pl.BlockSpec((TM, D_MODEL), lambda i: (i, 0)),
        out_shape=jax.ShapeDtypeStruct((T, D_MODEL), F32),
        scratch_shapes=_COMBINE_SCRATCH,
        compiler_params=pltpu.CompilerParams(
            dimension_semantics=("arbitrary",), vmem_limit_bytes=VMEM_LIMIT),
        name="combine_final_norm",
    )(dest_tiles, dest_tiles, x2, route, yb, gain)


def _gelu_tanh(x):
    c = 0.7978845608028654
    return x * (0.5 * (1.0 + jnp.tanh(c * (x + 0.044715 * (x * x * x)))))


def _mixer_kernel(proj_ref, wgk_ref, bgk_ref, glan_ref, gmn_ref, wsp_ref, bsp_ref, wconv_ref,
                  out_ref, st_ref, hc_ref, lcat_ref, wm_ref, *, seq_start, first_step, between):
    TS = TS_MIX
    n_gla = TS // GLA_CHUNK
    n_gm = TS // GMLP_CHUNK

    @pl.when(seq_start)
    def _():
        st_ref[...] = jnp.zeros_like(st_ref)
        hc_ref[...] = jnp.zeros_like(hc_ref)

    @pl.when(first_step)
    def _():
        r = lax.broadcasted_iota(jnp.int32, (TS, TS), 0)
        c = lax.broadcasted_iota(jnp.int32, (TS, TS), 1)
        keep = ((r // GLA_CHUNK) == (c // GLA_CHUNK)) & (c <= r)
        lcat_ref[...] = jnp.where(keep, 1.0, 0.0).astype(BF16)
        t = lax.broadcasted_iota(jnp.int32, (GMLP_CHUNK, GMLP_HEADS * GMLP_CHUNK), 0)
        s = lax.broadcasted_iota(jnp.int32, (GMLP_CHUNK, GMLP_HEADS * GMLP_CHUNK), 1) % GMLP_CHUNK
        wm_ref[...] = jnp.where(s <= t, wsp_ref[...], 0.0).astype(BF16)

    lane256 = lax.broadcasted_iota(jnp.int32, (1, GLA_KDIM), 1)

    q = proj_ref[:, C_Q:C_Q + GLA_KDIM].astype(F32)
    k = proj_ref[:, C_K:C_K + GLA_KDIM].astype(F32)
    v_b = proj_ref[:, C_V:C_V + GLA_WIDTH]
    z = _dot(proj_ref[:, C_GKL:C_GKL + LANES], wgk_ref[...]) + bgk_ref[...]
    gk = (jnp.minimum(z, 0.0) - jnp.log1p(jnp.exp(-jnp.abs(z)))) * (1.0 / GLA_GATE_NORM)
    gk_hi, gk_lo = _split_bf16(gk)
    cs = _dot(lcat_ref[...], jnp.concatenate([gk_hi, gk_lo], axis=1))
    b = cs[:, :GLA_KDIM] + cs[:, GLA_KDIM:]
    b_last = [b[(c + 1) * GLA_CHUNK - 1:(c + 1) * GLA_CHUNK, :] for c in range(n_gla)]
    bl = jnp.concatenate(
        [jnp.broadcast_to(t, (GLA_CHUNK, GLA_KDIM)) for t in b_last], axis=0)
    q_dec = (q * (GLA_DK ** -0.5)) * jnp.exp(b)
    k_inv = (k * jnp.exp(-b)).astype(BF16)
    k_dec = (k * jnp.exp(bl - b)).astype(BF16)
    q_dec_b = q_dec.astype(BF16)

    zero_b = jnp.zeros_like(q_dec_b)
    q_stack = jnp.concatenate(
        [jnp.where((lane256 // GLA_DK) == h, q_dec_b, zero_b) for h in range(GLA_HEADS)], axis=0)
    scores = lax.dot_general(q_stack, k_inv, (((1,), (1,)), ((), ())),
                             preferred_element_type=F32)
    rt = lax.broadcasted_iota(jnp.int32, (TS, TS), 0)
    ct = lax.broadcasted_iota(jnp.int32, (TS, TS), 1)
    causal = ((rt // GLA_CHUNK) == (ct // GLA_CHUNK)) & (ct <= rt)
    o_heads = []
    for h in range(GLA_HEADS):
        p_h = jnp.where(causal, scores[h * TS:(h + 1) * TS, :], 0.0).astype(BF16)
        o_heads.append(_dot(p_h, v_b[:, h * GLA_DV:(h + 1) * GLA_DV]))

    sr = lax.broadcasted_iota(jnp.int32, (GLA_WIDTH, GLA_KDIM), 0) // GLA_DV
    sc = lax.broadcasted_iota(jnp.int32, (GLA_WIDTH, GLA_KDIM), 1) // GLA_DK
    bd_mask = sr == sc
    o_inter = []
    for c in range(n_gla):
        rows = slice(c * GLA_CHUNK, (c + 1) * GLA_CHUNK)
        st = st_ref[...]
        o_inter.append(lax.dot_general(q_dec_b[rows], st.astype(BF16), (((1,), (1,)), ((), ())),
                                       preferred_element_type=F32))
        upd = lax.dot_general(v_b[rows], k_dec[rows], (((0,), (0,)), ((), ())),
                              preferred_element_type=F32)
        decay = jnp.exp(b_last[c])
        st_ref[...] = st * decay + jnp.where(bd_mask, upd, 0.0)
    o_inter = jnp.concatenate(o_inter, axis=0)

    for h in range(GLA_HEADS):
        cols = slice(h * GLA_DV, (h + 1) * GLA_DV)
        o = o_heads[h] + o_inter[:, cols]
        o = o * lax.rsqrt(jnp.mean(o * o, axis=-1, keepdims=True) + RMS_EPS) * glan_ref[...]
        g = proj_ref[:, C_G + h * GLA_DV:C_G + (h + 1) * GLA_DV].astype(F32)
        out_ref[:, cols] = (o * (g * (1.0 / (1.0 + jnp.exp(-g))))).astype(out_ref.dtype)

    between("gla_done")
    u = _gelu_tanh(proj_ref[:, C_U:C_U + GMLP_WIDTH].astype(F32))
    vg = _gelu_tanh(proj_ref[:, C_VG:C_VG + GMLP_WIDTH].astype(F32))
    hr = lax.broadcasted_iota(jnp.int32, (GMLP_WIDTH, GMLP_WIDTH), 0) // GMLP_DH
    hcn = lax.broadcasted_iota(jnp.int32, (GMLP_WIDTH, GMLP_WIDTH), 1) // GMLP_DH
    head_mean = jnp.where(hr == hcn, 1.0 / GMLP_DH, 0.0).astype(BF16)
    sq_hi, sq_lo = _split_bf16(vg * vg)
    ms = _dot(sq_hi, head_mean) + _dot(sq_lo, head_mean)
    v32 = vg * lax.rsqrt(ms + RMS_EPS) * gmn_ref[...]
    for c in range(n_gm):
        rows = slice(c * GMLP_CHUNK, (c + 1) * GMLP_CHUNK)
        vc = v32[rows].astype(BF16)
        zc = jnp.zeros_like(vc)
        rhs = jnp.concatenate(
            [jnp.where((lane256 // GMLP_DH) == h, vc, zc) for h in range(GMLP_HEADS)], axis=0)
        mixed = _dot(wm_ref[...], rhs) + bsp_ref[...]
        out_ref[rows, GLA_WIDTH:GLA_WIDTH + GMLP_WIDTH] = (u[rows] * mixed).astype(out_ref.dtype)

    between("gmlp_done")
    hcv = (proj_ref[:, C_CG:C_CG + CONV_WIDTH].astype(F32)
           * proj_ref[:, C_X:C_X + CONV_WIDTH].astype(F32))
    hc_ref[8:8 + TS, :] = hcv
    y = (wconv_ref[2:3, :] * hcv + wconv_ref[1:2, :] * hc_ref[7:7 + TS, :]
         + wconv_ref[0:1, :] * hc_ref[6:6 + TS, :])
    out_ref[:, GLA_WIDTH + GMLP_WIDTH:] = (
        proj_ref[:, C_BG:C_BG + CONV_WIDTH].astype(F32) * y).astype(out_ref.dtype)
    hc_ref[0:8, :] = hc_ref[TS:TS + 8, :]


_MIXER_SCRATCH = [
    pltpu.VMEM((GLA_WIDTH, GLA_KDIM), F32),
    pltpu.VMEM((TS_MIX + 8, CONV_WIDTH), F32),
    pltpu.VMEM((TS_MIX, TS_MIX), BF16),
    pltpu.VMEM((GMLP_CHUNK, GMLP_HEADS * GMLP_CHUNK), BF16),
]


N_MIX_PARAMS = 7
PROJ_CHUNKS_AT = {"gla_done": 4, "gmlp_done": 4}


def _front_kernel(*refs, tiles_per_seq, combine):
    refs = list(refs)
    if combine:
        dcur_ref, dnxt_ref, x_ref, route_ref, yb_ref = refs[:5]
        del refs[:5]
    else:
        x_ref = refs.pop(0)
    gain_ref, wt_ref = refs[:2]
    mix_refs = refs[2:2 + N_MIX_PARAMS]
    del refs[:2 + N_MIX_PARAMS]
    if combine:
        xo_ref, out_ref, buf_ref, sem_ref = refs[:4]
        del refs[:4]
    else:
        out_ref = refs.pop(0)
    st_ref, hc_ref, lcat_ref, wm_ref, wb_ref, pcur_ref, pnext_ref = refs
    s = pl.program_id(0)

    @pl.when(s == 0)
    def _():
        _stage_w_in(wt_ref, wb_ref)
        pcur_ref[...] = jnp.zeros_like(pcur_ref)

    col_chunks = [(c0, min(c0 + PROJ_CHUNK, D_PROJ)) for c0 in range(0, D_PROJ, PROJ_CHUNK)]
    work = list(zip(_prefetch_groups(len(col_chunks)), col_chunks))

    def project(h, n):
        for _ in range(min(n, len(work))):
            rows, (c0, c1) = work.pop(0)
            if combine:
                _gather_start(dnxt_ref, yb_ref, buf_ref, sem_ref, 1 - lax.rem(s, 2), rows)
            pnext_ref[:, c0:c1] = _dot(h, wb_ref[:, c0:c1]).astype(BF16)

    mixers = functools.partial(
        _mixer_kernel, pcur_ref, *mix_refs, out_ref, st_ref, hc_ref, lcat_ref, wm_ref,
        seq_start=lax.rem(jnp.maximum(s - 1, 0), tiles_per_seq) == 0, first_step=s == 0)
    if combine:
        x = _combined_residual(dcur_ref, x_ref, route_ref, yb_ref, buf_ref, sem_ref)
        xo_ref[...] = x
        h = _rms(x, gain_ref[...]).astype(BF16)
        mixers(between=lambda site: project(h, PROJ_CHUNKS_AT[site]))
        project(h, len(work))
        _drain_last_prefetch(yb_ref, buf_ref, sem_ref)
    else:
        mixers(between=lambda site: None)
        project(_rms(x_ref[...], gain_ref[...]).astype(BF16), len(work))
    pcur_ref[...] = pnext_ref[...]


def _front(x, gain, w_in_t, layer, mix_params, batch, seq, moe=None):
    n_seq = seq // TS_MIX
    n = batch * n_seq
    T = batch * seq
    cur = lambda s: jnp.minimum(s, n - 1)
    full = lambda shape: pl.BlockSpec(shape, lambda s: (0,) * len(shape))
    row = lambda w, f: pl.BlockSpec((TM, w), lambda s: (f(s), 0))
    in_specs, args = [row(D_MODEL, cur)], [x]
    out_specs = [row(D_MODEL, lambda s: jnp.maximum(s - 1, 0))]
    out_shape = [jax.ShapeDtypeStruct((T, D_MODEL), BF16)]
    scratch = list(_MIXER_SCRATCH)
    if moe is not None:
        smem_tile = lambda f: pl.BlockSpec((1, 1, TOP_K * TM), lambda s: (f(s), 0, 0),
                                           memory_space=pltpu.SMEM)
        in_specs = [smem_tile(cur), smem_tile(lambda s: jnp.minimum(s + 1, n - 1))] + in_specs + [
            row(LANES, cur), pl.BlockSpec(memory_space=pl.ANY)]
        args = [moe["dest_tiles"], moe["dest_tiles"]] + args + [moe["route"], moe["yb"]]
        out_specs = [row(D_MODEL, cur)] + out_specs
        out_shape = [jax.ShapeDtypeStruct((T, D_MODEL), F32)] + out_shape
        scratch = _COMBINE_SCRATCH + scratch
    in_specs += [
        full((1, D_MODEL)),
        pl.BlockSpec((1, D_IN, D_MODEL), lambda s: (layer, 0, 0), pipeline_mode=pl.Buffered(1)),
        full((LANES, GLA_KDIM)), full((1, GLA_KDIM)), full((1, GLA_DV)), full((1, GMLP_WIDTH)),
        full((GMLP_CHUNK, GMLP_HEADS * GMLP_CHUNK)), full((GMLP_CHUNK, GMLP_WIDTH)),
        full((8, CONV_WIDTH)),
    ]
    scratch += [pltpu.VMEM((D_MODEL, D_PROJ), BF16), pltpu.VMEM((TS_MIX, D_PROJ), BF16),
                pltpu.VMEM((TS_MIX, D_PROJ), BF16)]
    return pl.pallas_call(
        functools.partial(_front_kernel, tiles_per_seq=n_seq, combine=moe is not None),
        grid=(n + 1,),
        in_specs=in_specs,
        out_specs=out_specs if moe is not None else out_specs[0],
        out_shape=out_shape if moe is not None else out_shape[0],
        scratch_shapes=scratch,
        compiler_params=pltpu.CompilerParams(
            dimension_semantics=("arbitrary",), vmem_limit_bytes=VMEM_LIMIT),
        name="front",
    )(*args, gain, w_in_t, *mix_params)


def _row_tiles_store(tiles_ref, x):
    rows = x.shape[0]
    flat = tiles_ref.reshape(rows * ROW_TILES, LANES)
    for c in range(ROW_TILES):
        flat[pl.ds(c, rows, stride=ROW_TILES), :] = x[:, c * LANES:(c + 1) * LANES]


def _row_tiles_chunks(tiles_ref, rows):
    flat = tiles_ref.reshape(rows * ROW_TILES, LANES)
    return [flat[pl.ds(c, rows, stride=ROW_TILES), :] for c in range(ROW_TILES)]


IN_SLOTS = 3


def _out_router_kernel(mix_hbm, x_hbm, wo_ref, gain_ref, wrc_ref, br_ref,
                       x2_ref, h2_ref, route_ref, route_t_ref, cnt_ref, tri_ref, wob_ref, lg_ref,
                       mixbuf_ref, xbuf_ref, insem_ref):
    i = pl.program_id(0)
    n = pl.num_programs(0) - 1

    def in_copies(t):
        slot = lax.rem(t, IN_SLOTS)
        rows = pl.ds(t * TM, TM)
        return [pltpu.make_async_copy(mix_hbm.at[rows], mixbuf_ref.at[slot], insem_ref.at[slot]),
                pltpu.make_async_copy(x_hbm.at[rows], xbuf_ref.at[slot], insem_ref.at[slot])]

    @pl.when(i == 0)
    def _():
        for t in range(IN_SLOTS - 1):
            for c in in_copies(t):
                c.start()

    @pl.when(i + IN_SLOTS - 1 < n)
    def _():
        for c in in_copies(i + IN_SLOTS - 1):
            c.start()

    @pl.when(i < n)
    def _():
        for c in in_copies(i):
            c.wait()

    tile_slot = lax.rem(jnp.minimum(i, n - 1), IN_SLOTS)
    mix_ref = mixbuf_ref.at[tile_slot]
    x_ref = xbuf_ref.at[tile_slot]

    @pl.when(i == 0)
    def _():
        cnt_ref[...] = jnp.zeros_like(cnt_ref)
        lg_ref[...] = jnp.zeros_like(lg_ref)
        r = lax.broadcasted_iota(jnp.int32, (TM, TM), 0)
        c = lax.broadcasted_iota(jnp.int32, (TM, TM), 1)
        tri_ref[...] = jnp.where(c < r, 1.0, 0.0).astype(BF16)
        for r0 in range(0, D_MODEL, W_PREP_ROWS):
            wob_ref[r0:r0 + W_PREP_ROWS, :] = wo_ref[0, r0:r0 + W_PREP_ROWS, :].astype(BF16)

    lg = lg_ref[...]
    half = D_MODEL // 2
    mix = mix_ref[...]
    x2_a = x_ref[:, :half] + _dot(mix, wob_ref[:, :half])

    lane = lax.broadcasted_iota(jnp.int32, (TM, LANES), 1).astype(F32)
    neg = -jnp.inf
    is_g = lane < N_GROUPS
    gl = jnp.where(is_g, lg, neg)
    gmax = jnp.max(gl, axis=1, keepdims=True)
    g_top = jnp.min(jnp.where(gl == gmax, lane, float(LANES)), axis=1, keepdims=True)
    g_w = 1.0 / jnp.sum(jnp.where(is_g, jnp.exp(lg - gmax), 0.0), axis=1, keepdims=True)
    first = N_GROUPS + EXPERTS_PER_GROUP * g_top
    el = jnp.where((lane >= first) & (lane < first + EXPERTS_PER_GROUP), lg, neg)
    m1 = jnp.max(el, axis=1, keepdims=True)
    i1 = jnp.min(jnp.where(el == m1, lane, float(LANES)), axis=1, keepdims=True)
    el2 = jnp.where(lane == i1, neg, el)
    m2 = jnp.max(el2, axis=1, keepdims=True)
    i2 = jnp.min(jnp.where(el2 == m2, lane, float(LANES)), axis=1, keepdims=True)
    ratio = jnp.exp(m2 - m1)
    w1 = g_w / (1.0 + ratio)
    w2 = w1 * ratio

    x2_b = x_ref[:, half:] + _dot(mix, wob_ref[:, half:])

    oh1 = jnp.where(lane == i1, 1.0, 0.0)
    oh2 = jnp.where(lane == i2, 1.0, 0.0)
    oh = jnp.where(i > 0, oh1 + oh2, 0.0)
    before = _dot(tri_ref[...], oh.astype(BF16)) + cnt_ref[0:1, :]
    rank1 = jnp.sum(oh1 * before, axis=1, keepdims=True)
    rank2 = jnp.sum(oh2 * before, axis=1, keepdims=True)
    cnt_ref[...] = cnt_ref[...] + jnp.sum(oh, axis=0, keepdims=True)

    rec = jnp.zeros((TM, LANES), F32)
    for col, val in ((R_E, i1 - N_GROUPS), (R_E + 1, i2 - N_GROUPS), (R_RANK, rank1),
                     (R_RANK + 1, rank2), (R_W, w1), (R_W + 1, w2)):
        rec = jnp.where(lane == col, val, rec)
    route_ref[...] = rec
    route_t_ref[0] = rec.T[0:SUBLANES, :]

    x2 = jnp.concatenate([x2_a, x2_b], axis=1)
    x2_ref[...] = x2
    h = _rms(x2, gain_ref[...])
    h_hi, h_lo = _split_bf16(h)
    h2_ref[...] = h_hi
    hh_hl = _dot(h_hi, wrc_ref[...])
    lg_ref[...] = (hh_hl[:, :ROUTER_COLS] + hh_hl[:, ROUTER_COLS:]
                   + _dot(h_lo, wrc_ref[:, :ROUTER_COLS]) + br_ref[...])


def _out_router(mixed, x, w_out, layer, gain, wr_cat, br):
    T = x.shape[0]
    n = T // TM
    row = lambda w: pl.BlockSpec((TM, w), lambda i: (jnp.minimum(i, n - 1), 0))
    lag = lambda i: jnp.maximum(i - 1, 0)
    full = lambda shape: pl.BlockSpec(shape, lambda i: (0, 0))
    wo_spec = pl.BlockSpec((1, D_MODEL, D_MODEL), lambda i: (layer, 0, 0),
                           pipeline_mode=pl.Buffered(1))
    return pl.pallas_call(
        _out_router_kernel,
        grid=(n + 1,),
        in_specs=[pl.BlockSpec(memory_space=pl.ANY), pl.BlockSpec(memory_space=pl.ANY), wo_spec,
                  full((1, D_MODEL)), full((D_MODEL, 2 * ROUTER_COLS)), full((1, ROUTER_COLS))],
        out_specs=[row(D_MODEL), row(D_MODEL),
                   pl.BlockSpec((TM, LANES), lambda i: (lag(i), 0)),
                   pl.BlockSpec((1, SUBLANES, TM), lambda i: (lag(i), 0, 0)), full((8, LANES))],
        out_shape=[jax.ShapeDtypeStruct((T, D_MODEL), F32),
                   jax.ShapeDtypeStruct((T, D_MODEL), BF16),
                   jax.ShapeDtypeStruct((T, LANES), F32),
                   jax.ShapeDtypeStruct((T // TM, SUBLANES, TM), F32),
                   jax.ShapeDtypeStruct((8, LANES), F32)],
        scratch_shapes=[pltpu.VMEM((TM, TM), BF16), pltpu.VMEM((D_MODEL, D_MODEL), BF16),
                        pltpu.VMEM((TM, ROUTER_COLS), F32),
                        pltpu.VMEM((IN_SLOTS, TM, D_MODEL), BF16),
                        pltpu.VMEM((IN_SLOTS, TM, D_MODEL), F32),
                        pltpu.SemaphoreType.DMA((IN_SLOTS,))],
        compiler_params=pltpu.CompilerParams(
            dimension_semantics=("arbitrary",), vmem_limit_bytes=VMEM_LIMIT),
        name="out_router",
    )(mixed, x, w_out, gain, wr_cat, br)


def _dispatch_kernel(fill_ref, nu_ref, dest_ref, h_ref, xb_ref, zero_ref, sem_ref, zsem_ref,
                     stage_ref):
    i = pl.program_id(0)
    par = lax.rem(i, 2)
    last = i == pl.num_programs(0) - 1
    n_blocks = xb_ref.shape[0] // MOE_BLK
    spare_fills = [(j >= nu_ref[0], pltpu.make_async_copy(
        zero_ref, xb_ref.at[pl.ds(j * MOE_BLK, MOE_BLK)], zsem_ref.at[1]))
        for j in range(n_blocks - N_EXPERTS, n_blocks)]

    @pl.when(i == 0)
    def _():
        zero_ref[...] = jnp.zeros_like(zero_ref)
        fills = [(fill_ref[e] >= 0, pltpu.make_async_copy(
            zero_ref, xb_ref.at[pl.ds(pl.multiple_of(jnp.maximum(fill_ref[e], 0), MOE_BLK), MOE_BLK)],
            zsem_ref.at[0])) for e in range(N_EXPERTS)]
        for cond, f in fills + spare_fills:
            pl.when(cond)(f.start)
        for cond, f in fills:
            pl.when(cond)(f.wait)

    _row_tiles_store(stage_ref.at[par], h_ref[...].astype(F32))
    for r in range(TM):
        for k in range(TOP_K):
            pltpu.make_async_copy(stage_ref.at[par, r], xb_ref.at[dest_ref[0, 0, k * TM + r]],
                                  sem_ref.at[par]).start(priority=k)

    def wait_tile(p):
        for _ in range(TOP_K):
            pltpu.make_async_copy(stage_ref.at[p], xb_ref.at[pl.ds(0, TM)], sem_ref.at[p]).wait()

    pl.when(i > 0)(lambda: wait_tile(1 - par))
    @pl.when(last)
    def _():
        wait_tile(par)
        for cond, f in spare_fills:
            pl.when(cond)(f.wait)


def _dispatch(fill_start, n_used, dest_tiles, h2, n_rows):
    T = h2.shape[0]
    grid_spec = pltpu.PrefetchScalarGridSpec(
        num_scalar_prefetch=2,
        grid=(T // TM,),
        in_specs=[
            pl.BlockSpec((1, 1, TOP_K * TM), lambda i, fs, nu: (i, 0, 0), memory_space=pltpu.SMEM),
            pl.BlockSpec((TM, D_MODEL), lambda i, fs, nu: (i, 0)),
        ],
        out_specs=pl.BlockSpec(memory_space=pl.ANY),
        scratch_shapes=[pltpu.VMEM((MOE_BLK, ROW_TILES, LANES), F32),
                        pltpu.SemaphoreType.DMA((2,)), pltpu.SemaphoreType.DMA((2,)),
                        pltpu.VMEM((2, TM, ROW_TILES, LANES), F32)],
    )
    return pl.pallas_call(
        _dispatch_kernel,
        grid_spec=grid_spec,
        out_shape=jax.ShapeDtypeStruct((n_rows, ROW_TILES, LANES), F32),
        compiler_params=pltpu.CompilerParams(dimension_semantics=("arbitrary",)),
        name="dispatch",
    )(fill_start, n_used, dest_tiles, h2)


BLOCK_COPY_PARTS = 4
X_SLOTS = 4
Y_SLOTS = 4


class _CopyGroup:
    def __init__(self, copies):
        self.copies = copies

    def start(self):
        for n, c in enumerate(self.copies):
            c.start(priority=n % 2)

    def wait(self):
        for c in self.copies:
            c.wait()


def _expert_kernel(be_ref, nxt_ref, nu_ref, half_ref, xb_ref, wg_ref, wu_ref, wd_ref, yb_ref,
                   xbuf_ref, ybuf_ref, wgs_ref, wus_ref, wds_ref, wgb_ref, wub_ref, wdb_ref,
                   xsem_ref, ysem_ref, wsem_ref, zbuf_ref, zsem_ref, *, layer, n_blocks):
    n_used = nu_ref[0]

    part = MOE_BLK // BLOCK_COPY_PARTS

    def x_copy(j, slot):
        return _CopyGroup([pltpu.make_async_copy(
            xb_ref.at[pl.ds(j * MOE_BLK + p * part, part)],
            xbuf_ref.at[slot, pl.ds(p * part, part)], xsem_ref.at[slot])
            for p in range(BLOCK_COPY_PARTS)])

    def y_copy(j, slot):
        return _CopyGroup([pltpu.make_async_copy(
            ybuf_ref.at[slot, pl.ds(p * part, part)],
            yb_ref.at[pl.ds(j * MOE_BLK + p * part, part)], ysem_ref.at[slot])
            for p in range(BLOCK_COPY_PARTS)])

    def w_copies(e, ws):
        return [pltpu.make_async_copy(src.at[layer, e], dst.at[ws], wsem_ref.at[ws])
                for src, dst in ((wg_ref, wgs_ref), (wu_ref, wus_ref), (wd_ref, wds_ref))]

    def fill_copy(j):
        return pltpu.make_async_copy(zbuf_ref, yb_ref.at[pl.ds(j * MOE_BLK, MOE_BLK)], zsem_ref)

    for j0 in range(X_SLOTS - 1):
        x_copy(j0, j0).start()
    for c in w_copies(be_ref[0], 0):
        c.start()

    zbuf_ref[...] = jnp.zeros_like(zbuf_ref)

    def fill(j, carry):
        fill_copy(j).start()
        return carry

    lax.fori_loop(n_used, n_blocks, fill, 0)

    def block(j, ws):
        slot = lax.rem(j, Y_SLOTS)
        xslot = lax.rem(j, X_SLOTS)
        first = (j == 0) | (be_ref[j] != be_ref[jnp.maximum(j - 1, 0)])
        ws = jnp.where(first & (j > 0), 1 - ws, ws)

        @pl.when(first)
        def _():
            for c in w_copies(be_ref[j], ws):
                c.wait()
            for r0 in range(0, D_MODEL, W_PREP_ROWS):
                rows = slice(r0, r0 + W_PREP_ROWS)
                wgb_ref[rows, :] = wgs_ref[ws, rows, :].astype(BF16)
                wub_ref[rows, :] = wus_ref[ws, rows, :].astype(BF16)
            for r0 in range(0, D_EXPERT, W_PREP_ROWS):
                rows = slice(r0, r0 + W_PREP_ROWS)
                wdb_ref[rows, :] = wds_ref[ws, rows, :].astype(BF16)

            @pl.when(nxt_ref[j] >= 0)
            def _():
                for c in w_copies(nxt_ref[j], 1 - ws):
                    c.start()

        ahead = j + X_SLOTS - 1

        @pl.when(ahead < n_used)
        def _():
            x_copy(ahead, lax.rem(ahead, X_SLOTS)).start()

        x_copy(j, xslot).wait()

        @pl.when(j >= Y_SLOTS)
        def _():
            y_copy(j - Y_SLOTS, slot).wait()

        def mlp(rows):
            x = jnp.concatenate(
                [c.astype(BF16) for c in _row_tiles_chunks(xbuf_ref.at[xslot, pl.ds(0, rows)], rows)],
                axis=1)
            g = _dot(x, wgb_ref[...])
            u = _dot(x, wub_ref[...])
            h = (g * (1.0 / (1.0 + jnp.exp(-g)))) * u
            _row_tiles_store(ybuf_ref.at[slot, pl.ds(0, rows)], _dot(h.astype(BF16), wdb_ref[...]))

        half_full = half_ref[j] == 1

        @pl.when(half_full)
        def _():
            mlp(MOE_BLK // 2)
            ybuf_ref[slot, pl.ds(MOE_BLK // 2, MOE_BLK // 2)] = jnp.zeros(
                (MOE_BLK // 2, ROW_TILES, LANES), F32)

        @pl.when(jnp.logical_not(half_full))
        def _():
            mlp(MOE_BLK)

        y_copy(j, slot).start()
        return ws

    lax.fori_loop(0, n_used, block, jnp.int32(0))

    for back in range(Y_SLOTS, 0, -1):
        y_copy(n_used - back, lax.rem(n_used - back, Y_SLOTS)).wait()

    def fill_wait(j, carry):
        fill_copy(j).wait()
        return carry

    lax.fori_loop(n_used, n_blocks, fill_wait, 0)


def _experts(blk_exp, nxt_exp, n_used, blk_half, xb, w_gate, w_up, w_down, layer):
    n_blocks = blk_exp.shape[0]
    any_spec = pl.BlockSpec(memory_space=pl.ANY)
    blk = (MOE_BLK, ROW_TILES, LANES)
    grid_spec = pltpu.PrefetchScalarGridSpec(
        num_scalar_prefetch=4,
        grid=(1,),
        in_specs=[any_spec, any_spec, any_spec, any_spec],
        out_specs=any_spec,
        scratch_shapes=[
            pltpu.VMEM((X_SLOTS,) + blk, F32), pltpu.VMEM((Y_SLOTS,) + blk, F32),
            pltpu.VMEM((2, D_MODEL, D_EXPERT), F32), pltpu.VMEM((2, D_MODEL, D_EXPERT), F32),
            pltpu.VMEM((2, D_EXPERT, D_MODEL), F32),
            pltpu.VMEM((D_MODEL, D_EXPERT), BF16), pltpu.VMEM((D_MODEL, D_EXPERT), BF16),
            pltpu.VMEM((D_EXPERT, D_MODEL), BF16),
            pltpu.SemaphoreType.DMA((X_SLOTS,)), pltpu.SemaphoreType.DMA((Y_SLOTS,)),
            pltpu.SemaphoreType.DMA((2,)),
            pltpu.VMEM(blk, F32), pltpu.SemaphoreType.DMA(()),
        ],
    )
    return pl.pallas_call(
        functools.partial(_expert_kernel, layer=layer, n_blocks=n_blocks),
        grid_spec=grid_spec,
        out_shape=jax.ShapeDtypeStruct((n_blocks * MOE_BLK, ROW_TILES, LANES), F32),
        compiler_params=pltpu.CompilerParams(
            dimension_semantics=("arbitrary",), vmem_limit_bytes=VMEM_LIMIT),
        name="experts",
    )(blk_exp, nxt_exp, n_used, blk_half, xb, w_gate, w_up, w_down)


def _dispatch_tables(route_t, counts_rec, T):
    counts = counts_rec[0, N_GROUPS:N_GROUPS + N_EXPERTS].astype(jnp.int32)
    n_steps = (T * TOP_K) // MOE_BLK + N_EXPERTS
    nblk = (counts + MOE_BLK - 1) // MOE_BLK
    bend = jnp.cumsum(nblk)
    pstart = (bend - nblk) * MOE_BLK
    n_used = bend[-1]
    j = jnp.minimum(jnp.arange(n_steps, dtype=jnp.int32), n_used - 1)
    blk_exp = jnp.minimum(jnp.sum(j[:, None] >= bend[None, :], axis=1), N_EXPERTS - 1)
    n_rows = n_steps * MOE_BLK
    last_blk = jnp.where(counts > 0, (bend - 1) * MOE_BLK, -1)
    ids = jnp.arange(N_EXPERTS, dtype=jnp.int32)
    later = (ids[None, :] > ids[:, None]) & (nblk[None, :] > 0)
    nxt_of = jnp.min(jnp.where(later, ids[None, :], N_EXPERTS), axis=1)
    nxt_tab = jnp.where(nxt_of < N_EXPERTS, nxt_of, -1)
    own = blk_exp[:, None] == ids[None, :]
    nxt_exp = jnp.sum(jnp.where(own, nxt_tab[None, :], 0), axis=1)
    seg_end = jnp.sum(jnp.where(own, (pstart + counts)[None, :], 0), axis=1)
    blk_half = (seg_end - j * MOE_BLK <= MOE_BLK // 2).astype(jnp.int32)
    e = route_t[:, R_E:R_E + TOP_K, :].astype(jnp.int32)
    rank = route_t[:, R_RANK:R_RANK + TOP_K, :].astype(jnp.int32)
    seg = jnp.sum(jnp.where(e[..., None] == jnp.arange(N_EXPERTS), pstart, 0), axis=-1)
    dest = jnp.clip(seg + rank, 0, n_steps * MOE_BLK - 1)
    dest_tiles = dest.reshape(T // TM, 1, TOP_K * TM)
    return dict(dest_tiles=dest_tiles, fill_start=last_blk.astype(jnp.int32),
                blk_exp=blk_exp.astype(jnp.int32), nxt_exp=nxt_exp.astype(jnp.int32),
                blk_half=blk_half,
                n_used=n_used.reshape(1).astype(jnp.int32), n_rows=n_rows)


def _prep_layer(l, w_gk_up, b_gk, gla_norm, gmlp_norm, w_spatial, b_spatial, w_conv,
                w_router_group, b_router_group, w_router_expert, b_router_expert):
    wgk = jnp.concatenate(
        [w_gk_up[l], jnp.zeros((LANES - GLA_GATE_RANK, GLA_KDIM), F32)], axis=0).astype(BF16)
    wsp = w_spatial[l].transpose(1, 0, 2).reshape(GMLP_CHUNK, GMLP_HEADS * GMLP_CHUNK)
    bsp = jnp.repeat(b_spatial[l].T, GMLP_DH, axis=1)
    wconv = jnp.concatenate([w_conv[l], jnp.zeros((8 - CONV_K, CONV_WIDTH), F32)], axis=0)
    wr = jnp.concatenate(
        [w_router_group[l], w_router_expert[l],
         jnp.zeros((D_MODEL, ROUTER_COLS - N_GROUPS - N_EXPERTS), F32)], axis=1)
    wr_hi = wr.astype(BF16)
    wr_lo = (wr - wr_hi.astype(F32)).astype(BF16)
    br = jnp.concatenate(
        [b_router_group[l], b_router_expert[l],
         jnp.zeros((ROUTER_COLS - N_GROUPS - N_EXPERTS,), F32)])[None, :]
    return dict(
        wgk=wgk, bgk=b_gk[l][None, :], glan=gla_norm[l][None, :], gmn=gmlp_norm[l][None, :],
        wsp=wsp, bsp=bsp, wconv=wconv, wr_cat=jnp.concatenate([wr_hi, wr_lo], axis=1), br=br)


def kernel(x, attn_norm, w_in, w_gk_up, b_gk, gla_norm, gmlp_norm, w_spatial, b_spatial, w_conv, w_out, ffn_norm, w_router_group, b_router_group, w_router_expert, b_router_expert, w_gate, w_up, w_down, final_norm):
    B, S, D = x.shape
    T = B * S
    depth = w_in.shape[0]
    xr = x.reshape(T, D)
    w_in_t = jnp.swapaxes(w_in, 1, 2)
    moe = None
    for l in range(depth):
        p = _prep_layer(l, w_gk_up, b_gk, gla_norm, gmlp_norm, w_spatial, b_spatial, w_conv,
                        w_router_group, b_router_group, w_router_expert, b_router_expert)
        mix_params = (p["wgk"], p["bgk"], p["glan"], p["gmn"], p["wsp"], p["bsp"], p["wconv"])
        if moe is None:
            mixed = _front(xr, attn_norm[l][None, :], w_in_t, l, mix_params, B, S)
        else:
            xr, mixed = _front(moe["x2"], attn_norm[l][None, :], w_in_t, l, mix_params, B, S, moe)
        x2, h2, route, route_t, counts_rec = _out_router(
            mixed, xr, w_out, l, ffn_norm[l][None, :], p["wr_cat"], p["br"])
        moe = _dispatch_tables(route_t, counts_rec, T)
        xb = _dispatch(moe["fill_start"], moe["n_used"], moe["dest_tiles"], h2, moe["n_rows"])
        yb = _experts(moe["blk_exp"], moe["nxt_exp"], moe["n_used"], moe["blk_half"], xb,
                      w_gate, w_up, w_down, l)
        moe.update(x2=x2, route=route, yb=yb)
    out = _combine_final_norm(moe["dest_tiles"], moe["x2"], moe["route"], moe["yb"],
                              final_norm[None, :])
    return out.reshape(B, S, D)
```

```python
import functools

import jax
import jax.numpy as jnp
from jax import lax
from jax.experimental import pallas as pl
from jax.experimental.pallas import tpu as pltpu

F32 = jnp.float32
BF16 = jnp.bfloat16

D_MODEL = 1024
RMS_EPS = 1e-6
GLA_HEADS = 4
GLA_WIDTH = 512
GLA_DV = 128
GLA_DK = 64
GLA_KDIM = 256
GLA_GATE_RANK = 16
GLA_GATE_NORM = 16.0
GLA_CHUNK = 64
GMLP_HEADS = 4
GMLP_WIDTH = 256
GMLP_DH = 64
GMLP_CHUNK = 128
CONV_WIDTH = 256
CONV_K = 3
N_GROUPS = 4
EXPERTS_PER_GROUP = 8
N_EXPERTS = 32
TOP_K = 2
D_EXPERT = 256

LANES = 128
C_Q, C_K, C_V, C_G = 0, 256, 512, 1024
C_U, C_VG, C_X, C_BG, C_CG, C_GKL = 1536, 1792, 2048, 2304, 2560, 2816
D_PROJ = C_GKL + LANES
D_IN = C_GKL + GLA_GATE_RANK

TM = 256
TS_MIX = TM
MOE_BLK = 256
ROUTER_COLS = LANES
SUBLANES = 8
ROW_TILES = D_MODEL // LANES
assert ROW_TILES == SUBLANES
VMEM_LIMIT = 56 * 1024 * 1024
R_E, R_RANK, R_W = 0, 2, 4


def _dot(a, b):
    return jnp.dot(a, b, preferred_element_type=F32)


def _split_bf16(x):
    hi = x.astype(BF16)
    lo = (x - hi.astype(F32)).astype(BF16)
    return hi, lo


def _rms(x, gain):
    return x * lax.rsqrt(jnp.mean(x * x, axis=-1, keepdims=True) + RMS_EPS) * gain


W_PREP_ROWS = 128
PROJ_CHUNK = 256


def _stage_w_in(wt_ref, wb_ref):
    for c0 in range(0, C_GKL, LANES):
        src = c0 if c0 < C_U else c0 + GLA_GATE_RANK
        wb_ref[:, c0:c0 + LANES] = wt_ref[0, src:src + LANES, :].T.astype(BF16)
    low = jnp.concatenate([wt_ref[0, C_U:C_U + GLA_GATE_RANK, :],
                           jnp.zeros((LANES - GLA_GATE_RANK, D_MODEL), F32)], axis=0)
    wb_ref[:, C_GKL:D_PROJ] = low.T.astype(BF16)


def _row_gather_copy(yb_ref, buf_ref, sem_ref, slot, k, r, d):
    return pltpu.make_async_copy(yb_ref.at[d], buf_ref.at[slot, k, r], sem_ref.at[slot])


def _gather_start(dest_ref, yb_ref, buf_ref, sem_ref, slot, rows=range(TM)):
    for r in rows:
        for k in range(TOP_K):
            _row_gather_copy(yb_ref, buf_ref, sem_ref, slot, k, r,
                             dest_ref[0, 0, k * TM + r]).start(priority=k)


def _gather_wait(yb_ref, buf_ref, sem_ref, slot):
    for k in range(TOP_K):
        pltpu.make_async_copy(yb_ref.at[pl.ds(0, TM)], buf_ref.at[slot, k], sem_ref.at[slot]).wait()


def _combined_residual(dcur_ref, x_ref, route_ref, yb_ref, buf_ref, sem_ref):
    i = pl.program_id(0)
    slot = lax.rem(i, 2)

    @pl.when(i == 0)
    def _():
        _gather_start(dcur_ref, yb_ref, buf_ref, sem_ref, 0)

    _gather_wait(yb_ref, buf_ref, sem_ref, slot)
    w0 = route_ref[:, R_W:R_W + 1]
    w1 = route_ref[:, R_W + 1:R_W + 2]
    y0 = _row_tiles_chunks(buf_ref.at[slot, 0], TM)
    y1 = _row_tiles_chunks(buf_ref.at[slot, 1], TM)
    return jnp.concatenate(
        [x_ref[:, c * LANES:(c + 1) * LANES] + (w0 * y0[c] + w1 * y1[c]) for c in range(ROW_TILES)],
        axis=1)


def _prefetch_groups(n_groups):
    per = -(-TM // n_groups)
    return [range(g * per, min(TM, (g + 1) * per)) for g in range(n_groups)]


def _drain_last_prefetch(yb_ref, buf_ref, sem_ref):
    i = pl.program_id(0)

    @pl.when(i == pl.num_programs(0) - 1)
    def _():
        _gather_wait(yb_ref, buf_ref, sem_ref, 1 - lax.rem(i, 2))


def _combine_specs(n_tiles):
    smem_tile = lambda f: pl.BlockSpec((1, 1, TOP_K * TM), f, memory_space=pltpu.SMEM)
    return [
        smem_tile(lambda i: (i, 0, 0)),
        smem_tile(lambda i: (jnp.minimum(i + 1, n_tiles - 1), 0, 0)),
        pl.BlockSpec((TM, D_MODEL), lambda i: (i, 0)),
        pl.BlockSpec((TM, LANES), lambda i: (i, 0)),
        pl.BlockSpec(memory_space=pl.ANY),
    ]


_COMBINE_SCRATCH = [pltpu.VMEM((2, TOP_K, TM, ROW_TILES, LANES), F32),
                    pltpu.SemaphoreType.DMA((2,))]


def _combine_final_norm_kernel(dcur_ref, dnxt_ref, x_ref, route_ref, yb_ref, gain_ref,
                               o_ref, buf_ref, sem_ref):
    _gather_start(dnxt_ref, yb_ref, buf_ref, sem_ref, 1 - lax.rem(pl.program_id(0), 2))
    x = _combined_residual(dcur_ref, x_ref, route_ref, yb_ref, buf_ref, sem_ref)
    o_ref[...] = _rms(x, gain_ref[...])
    _drain_last_prefetch(yb_ref, buf_ref, sem_ref)


def _combine_final_norm(dest_tiles, x2, route, yb, gain):
    T = x2.shape[0]
    n_tiles = T // TM
    return pl.pallas_call(
        _combine_final_norm_kernel,
        grid=(n_tiles,),
        in_specs=_combine_specs(n_tiles) + [pl.BlockSpec((1, D_MODEL), lambda i: (0, 0))],
        out_specs=pl.BlockSpec((TM, D_MODEL), lambda i: (i, 0)),
        out_shape=jax.ShapeDtypeStruct((T, D_MODEL), F32),
        scratch_shapes=_COMBINE_SCRATCH,
        compiler_params=pltpu.CompilerParams(
            dimension_semantics=("arbitrary",), vmem_limit_bytes=VMEM_LIMIT),
        name="combine_final_norm",
    )(dest_tiles, dest_tiles, x2, route, yb, gain)


def _gelu_tanh(x):
    c = 0.7978845608028654
    return x * (0.5 * (1.0 + jnp.tanh(c * (x + 0.044715 * (x * x * x)))))


def _mixer_kernel(proj_ref, wgk_ref, bgk_ref, glan_ref, gmn_ref, wsp_ref, bsp_ref, wconv_ref,
                  out_ref, st_ref, hc_ref, lcat_ref, wm_ref, *, seq_start, first_step, between):
    TS = TS_MIX
    n_gla = TS // GLA_CHUNK
    n_gm = TS // GMLP_CHUNK

    @pl.when(seq_start)
    def _():
        st_ref[...] = jnp.zeros_like(st_ref)
        hc_ref[...] = jnp.zeros_like(hc_ref)

    @pl.when(first_step)
    def _():
        r = lax.broadcasted_iota(jnp.int32, (TS, TS), 0)
        c = lax.broadcasted_iota(jnp.int32, (TS, TS), 1)
        keep = ((r // GLA_CHUNK) == (c // GLA_CHUNK)) & (c <= r)
        lcat_ref[...] = jnp.where(keep, 1.0, 0.0).astype(BF16)
        t = lax.broadcasted_iota(jnp.int32, (GMLP_CHUNK, GMLP_HEADS * GMLP_CHUNK), 0)
        s = lax.broadcasted_iota(jnp.int32, (GMLP_CHUNK, GMLP_HEADS * GMLP_CHUNK), 1) % GMLP_CHUNK
        wm_ref[...] = jnp.where(s <= t, wsp_ref[...], 0.0).astype(BF16)

    lane256 = lax.broadcasted_iota(jnp.int32, (1, GLA_KDIM), 1)

    q = proj_ref[:, C_Q:C_Q + GLA_KDIM].astype(F32)
    k = proj_ref[:, C_K:C_K + GLA_KDIM].astype(F32)
    v_b = proj_ref[:, C_V:C_V + GLA_WIDTH]
    z = _dot(proj_ref[:, C_GKL:C_GKL + LANES], wgk_ref[...]) + bgk_ref[...]
    gk = (jnp.minimum(z, 0.0) - jnp.log1p(jnp.exp(-jnp.abs(z)))) * (1.0 / GLA_GATE_NORM)
    gk_hi, gk_lo = _split_bf16(gk)
    cs = _dot(lcat_ref[...], jnp.concatenate([gk_hi, gk_lo], axis=1))
    b = cs[:, :GLA_KDIM] + cs[:, GLA_KDIM:]
    b_last = [b[(c + 1) * GLA_CHUNK - 1:(c + 1) * GLA_CHUNK, :] for c in range(n_gla)]
    bl = jnp.concatenate(
        [jnp.broadcast_to(t, (GLA_CHUNK, GLA_KDIM)) for t in b_last], axis=0)
    q_dec = (q * (GLA_DK ** -0.5)) * jnp.exp(b)
    k_inv = (k * jnp.exp(-b)).astype(BF16)
    k_dec = (k * jnp.exp(bl - b)).astype(BF16)
    q_dec_b = q_dec.astype(BF16)

    zero_b = jnp.zeros_like(q_dec_b)
    q_stack = jnp.concatenate(
        [jnp.where((lane256 // GLA_DK) == h, q_dec_b, zero_b) for h in range(GLA_HEADS)], axis=0)
    scores = lax.dot_general(q_stack, k_inv, (((1,), (1,)), ((), ())),
                             preferred_element_type=F32)
    rt = lax.broadcasted_iota(jnp.int32, (TS, TS), 0)
    ct = lax.broadcasted_iota(jnp.int32, (TS, TS), 1)
    causal = ((rt // GLA_CHUNK) == (ct // GLA_CHUNK)) & (ct <= rt)
    o_heads = []
    for h in range(GLA_HEADS):
        p_h = jnp.where(causal, scores[h * TS:(h + 1) * TS, :], 0.0).astype(BF16)
        o_heads.append(_dot(p_h, v_b[:, h * GLA_DV:(h + 1) * GLA_DV]))

    sr = lax.broadcasted_iota(jnp.int32, (GLA_WIDTH, GLA_KDIM), 0) // GLA_DV
    sc = lax.broadcasted_iota(jnp.int32, (GLA_WIDTH, GLA_KDIM), 1) // GLA_DK
    bd_mask = sr == sc
    o_inter = []
    for c in range(n_gla):
        rows = slice(c * GLA_CHUNK, (c + 1) * GLA_CHUNK)
        st = st_ref[...]
        o_inter.append(lax.dot_general(q_dec_b[rows], st.astype(BF16), (((1,), (1,)), ((), ())),
                                       preferred_element_type=F32))
        upd = lax.dot_general(v_b[rows], k_dec[rows], (((0,), (0,)), ((), ())),
                              preferred_element_type=F32)
        decay = jnp.exp(b_last[c])
        st_ref[...] = st * decay + jnp.where(bd_mask, upd, 0.0)
    o_inter = jnp.concatenate(o_inter, axis=0)

    for h in range(GLA_HEADS):
        cols = slice(h * GLA_DV, (h + 1) * GLA_DV)
        o = o_heads[h] + o_inter[:, cols]
        o = o * lax.rsqrt(jnp.mean(o * o, axis=-1, keepdims=True) + RMS_EPS) * glan_ref[...]
        g = proj_ref[:, C_G + h * GLA_DV:C_G + (h + 1) * GLA_DV].astype(F32)
        out_ref[:, cols] = (o * (g * (1.0 / (1.0 + jnp.exp(-g))))).astype(out_ref.dtype)

    between("gla_done")
    u = _gelu_tanh(proj_ref[:, C_U:C_U + GMLP_WIDTH].astype(F32))
    vg = _gelu_tanh(proj_ref[:, C_VG:C_VG + GMLP_WIDTH].astype(F32))
    hr = lax.broadcasted_iota(jnp.int32, (GMLP_WIDTH, GMLP_WIDTH), 0) // GMLP_DH
    hcn = lax.broadcasted_iota(jnp.int32, (GMLP_WIDTH, GMLP_WIDTH), 1) // GMLP_DH
    head_mean = jnp.where(hr == hcn, 1.0 / GMLP_DH, 0.0).astype(BF16)
    sq_hi, sq_lo = _split_bf16(vg * vg)
    ms = _dot(sq_hi, head_mean) + _dot(sq_lo, head_mean)
    v32 = vg * lax.rsqrt(ms + RMS_EPS) * gmn_ref[...]
    for c in range(n_gm):
        rows = slice(c * GMLP_CHUNK, (c + 1) * GMLP_CHUNK)
        vc = v32[rows].astype(BF16)
        zc = jnp.zeros_like(vc)
        rhs = jnp.concatenate(
            [jnp.where((lane256 // GMLP_DH) == h, vc, zc) for h in range(GMLP_HEADS)], axis=0)
        mixed = _dot(wm_ref[...], rhs) + bsp_ref[...]
        out_ref[rows, GLA_WIDTH:GLA_WIDTH + GMLP_WIDTH] = (u[rows] * mixed).astype(out_ref.dtype)

    between("gmlp_done")
    hcv = (proj_ref[:, C_CG:C_CG + CONV_WIDTH].astype(F32)
           * proj_ref[:, C_X:C_X + CONV_WIDTH].astype(F32))
    hc_ref[8:8 + TS, :] = hcv
    y = (wconv_ref[2:3, :] * hcv + wconv_ref[1:2, :] * hc_ref[7:7 + TS, :]
         + wconv_ref[0:1, :] * hc_ref[6:6 + TS, :])
    out_ref[:, GLA_WIDTH + GMLP_WIDTH:] = (
        proj_ref[:, C_BG:C_BG + CONV_WIDTH].astype(F32) * y).astype(out_ref.dtype)
    hc_ref[0:8, :] = hc_ref[TS:TS + 8, :]


_MIXER_SCRATCH = [
    pltpu.VMEM((GLA_WIDTH, GLA_KDIM), F32),
    pltpu.VMEM((TS_MIX + 8, CONV_WIDTH), F32),
    pltpu.VMEM((TS_MIX, TS_MIX), BF16),
    pltpu.VMEM((GMLP_CHUNK, GMLP_HEADS * GMLP_CHUNK), BF16),
]


N_MIX_PARAMS = 7
PROJ_CHUNKS_AT = {"gla_done": 4, "gmlp_done": 4}


def _front_kernel(*refs, tiles_per_seq, combine):
    refs = list(refs)
    if combine:
        dcur_ref, dnxt_ref, x_ref, route_ref, yb_ref = refs[:5]
        del refs[:5]
    else:
        x_ref = refs.pop(0)
    gain_ref, wt_ref = refs[:2]
    mix_refs = refs[2:2 + N_MIX_PARAMS]
    del refs[:2 + N_MIX_PARAMS]
    if combine:
        xo_ref, out_ref, buf_ref, sem_ref = refs[:4]
        del refs[:4]
    else:
        out_ref = refs.pop(0)
    st_ref, hc_ref, lcat_ref, wm_ref, wb_ref, pcur_ref, pnext_ref = refs
    s = pl.program_id(0)

    @pl.when(s == 0)
    def _():
        _stage_w_in(wt_ref, wb_ref)
        pcur_ref[...] = jnp.zeros_like(pcur_ref)

    col_chunks = [(c0, min(c0 + PROJ_CHUNK, D_PROJ)) for c0 in range(0, D_PROJ, PROJ_CHUNK)]
    work = list(zip(_prefetch_groups(len(col_chunks)), col_chunks))

    def project(h, n):
        for _ in range(min(n, len(work))):
            rows, (c0, c1) = work.pop(0)
            if combine:
                _gather_start(dnxt_ref, yb_ref, buf_ref, sem_ref, 1 - lax.rem(s, 2), rows)
            pnext_ref[:, c0:c1] = _dot(h, wb_ref[:, c0:c1]).astype(BF16)

    mixers = functools.partial(
        _mixer_kernel, pcur_ref, *mix_refs, out_ref, st_ref, hc_ref, lcat_ref, wm_ref,
        seq_start=lax.rem(jnp.maximum(s - 1, 0), tiles_per_seq) == 0, first_step=s == 0)
    if combine:
        x = _combined_residual(dcur_ref, x_ref, route_ref, yb_ref, buf_ref, sem_ref)
        xo_ref[...] = x
        h = _rms(x, gain_ref[...]).astype(BF16)
        mixers(between=lambda site: project(h, PROJ_CHUNKS_AT[site]))
        project(h, len(work))
        _drain_last_prefetch(yb_ref, buf_ref, sem_ref)
    else:
        mixers(between=lambda site: None)
        project(_rms(x_ref[...], gain_ref[...]).astype(BF16), len(work))
    pcur_ref[...] = pnext_ref[...]


def _front(x, gain, w_in_t, layer, mix_params, batch, seq, moe=None):
    n_seq = seq // TS_MIX
    n = batch * n_seq
    T = batch * seq
    cur = lambda s: jnp.minimum(s, n - 1)
    full = lambda shape: pl.BlockSpec(shape, lambda s: (0,) * len(shape))
    row = lambda w, f: pl.BlockSpec((TM, w), lambda s: (f(s), 0))
    in_specs, args = [row(D_MODEL, cur)], [x]
    out_specs = [row(D_MODEL, lambda s: jnp.maximum(s - 1, 0))]
    out_shape = [jax.ShapeDtypeStruct((T, D_MODEL), BF16)]
    scratch = list(_MIXER_SCRATCH)
    if moe is not None:
        smem_tile = lambda f: pl.BlockSpec((1, 1, TOP_K * TM), lambda s: (f(s), 0, 0),
                                           memory_space=pltpu.SMEM)
        in_specs = [smem_tile(cur), smem_tile(lambda s: jnp.minimum(s + 1, n - 1))] + in_specs + [
            row(LANES, cur), pl.BlockSpec(memory_space=pl.ANY)]
        args = [moe["dest_tiles"], moe["dest_tiles"]] + args + [moe["route"], moe["yb"]]
        out_specs = [row(D_MODEL, cur)] + out_specs
        out_shape = [jax.ShapeDtypeStruct((T, D_MODEL), F32)] + out_shape
        scratch = _COMBINE_SCRATCH + scratch
    in_specs += [
        full((1, D_MODEL)),
        pl.BlockSpec((1, D_IN, D_MODEL), lambda s: (layer, 0, 0), pipeline_mode=pl.Buffered(1)),
        full((LANES, GLA_KDIM)), full((1, GLA_KDIM)), full((1, GLA_DV)), full((1, GMLP_WIDTH)),
        full((GMLP_CHUNK, GMLP_HEADS * GMLP_CHUNK)), full((GMLP_CHUNK, GMLP_WIDTH)),
        full((8, CONV_WIDTH)),
    ]
    scratch += [pltpu.VMEM((D_MODEL, D_PROJ), BF16), pltpu.VMEM((TS_MIX, D_PROJ), BF16),
                pltpu.VMEM((TS_MIX, D_PROJ), BF16)]
    return pl.pallas_call(
        functools.partial(_front_kernel, tiles_per_seq=n_seq, combine=moe is not None),
        grid=(n + 1,),
        in_specs=in_specs,
        out_specs=out_specs if moe is not None else out_specs[0],
        out_shape=out_shape if moe is not None else out_shape[0],
        scratch_shapes=scratch,
        compiler_params=pltpu.CompilerParams(
            dimension_semantics=("arbitrary",), vmem_limit_bytes=VMEM_LIMIT),
        name="front",
    )(*args, gain, w_in_t, *mix_params)


def _row_tiles_store(tiles_ref, x):
    rows = x.shape[0]
    flat = tiles_ref.reshape(rows * ROW_TILES, LANES)
    for c in range(ROW_TILES):
        flat[pl.ds(c, rows, stride=ROW_TILES), :] = x[:, c * LANES:(c + 1) * LANES]


def _row_tiles_chunks(tiles_ref, rows):
    flat = tiles_ref.reshape(rows * ROW_TILES, LANES)
    return [flat[pl.ds(c, rows, stride=ROW_TILES), :] for c in range(ROW_TILES)]


IN_SLOTS = 3


def _out_router_kernel(mix_hbm, x_hbm, wo_ref, gain_ref, wrc_ref, br_ref,
                       x2_ref, h2_ref, route_ref, route_t_ref, cnt_ref, tri_ref, wob_ref, lg_ref,
                       mixbuf_ref, xbuf_ref, insem_ref):
    i = pl.program_id(0)
    n = pl.num_programs(0) - 1

    def in_copies(t):
        slot = lax.rem(t, IN_SLOTS)
        rows = pl.ds(t * TM, TM)
        return [pltpu.make_async_copy(mix_hbm.at[rows], mixbuf_ref.at[slot], insem_ref.at[slot]),
                pltpu.make_async_copy(x_hbm.at[rows], xbuf_ref.at[slot], insem_ref.at[slot])]

    @pl.when(i == 0)
    def _():
        for t in range(IN_SLOTS - 1):
            for c in in_copies(t):
                c.start()

    @pl.when(i + IN_SLOTS - 1 < n)
    def _():
        for c in in_copies(i + IN_SLOTS - 1):
            c.start()

    @pl.when(i < n)
    def _():
        for c in in_copies(i):
            c.wait()

    tile_slot = lax.rem(jnp.minimum(i, n - 1), IN_SLOTS)
    mix_ref = mixbuf_ref.at[tile_slot]
    x_ref = xbuf_ref.at[tile_slot]

    @pl.when(i == 0)
    def _():
        cnt_ref[...] = jnp.zeros_like(cnt_ref)
        lg_ref[...] = jnp.zeros_like(lg_ref)
        r = lax.broadcasted_iota(jnp.int32, (TM, TM), 0)
        c = lax.broadcasted_iota(jnp.int32, (TM, TM), 1)
        tri_ref[...] = jnp.where(c < r, 1.0, 0.0).astype(BF16)
        for r0 in range(0, D_MODEL, W_PREP_ROWS):
            wob_ref[r0:r0 + W_PREP_ROWS, :] = wo_ref[0, r0:r0 + W_PREP_ROWS, :].astype(BF16)

    lg = lg_ref[...]
    half = D_MODEL // 2
    mix = mix_ref[...]
    x2_a = x_ref[:, :half] + _dot(mix, wob_ref[:, :half])

    lane = lax.broadcasted_iota(jnp.int32, (TM, LANES), 1).astype(F32)
    neg = -jnp.inf
    is_g = lane < N_GROUPS
    gl = jnp.where(is_g, lg, neg)
    gmax = jnp.max(gl, axis=1, keepdims=True)
    g_top = jnp.min(jnp.where(gl == gmax, lane, float(LANES)), axis=1, keepdims=True)
    g_w = 1.0 / jnp.sum(jnp.where(is_g, jnp.exp(lg - gmax), 0.0), axis=1, keepdims=True)
    first = N_GROUPS + EXPERTS_PER_GROUP * g_top
    el = jnp.where((lane >= first) & (lane < first + EXPERTS_PER_GROUP), lg, neg)
    m1 = jnp.max(el, axis=1, keepdims=True)
    i1 = jnp.min(jnp.where(el == m1, lane, float(LANES)), axis=1, keepdims=True)
    el2 = jnp.where(lane == i1, neg, el)
    m2 = jnp.max(el2, axis=1, keepdims=True)
    i2 = jnp.min(jnp.where(el2 == m2, lane, float(LANES)), axis=1, keepdims=True)
    ratio = jnp.exp(m2 - m1)
    w1 = g_w / (1.0 + ratio)
    w2 = w1 * ratio

    x2_b = x_ref[:, half:] + _dot(mix, wob_ref[:, half:])

    oh1 = jnp.where(lane == i1, 1.0, 0.0)
    oh2 = jnp.where(lane == i2, 1.0, 0.0)
    oh = jnp.where(i > 0, oh1 + oh2, 0.0)
    before = _dot(tri_ref[...], oh.astype(BF16)) + cnt_ref[0:1, :]
    rank1 = jnp.sum(oh1 * before, axis=1, keepdims=True)
    rank2 = jnp.sum(oh2 * before, axis=1, keepdims=True)
    cnt_ref[...] = cnt_ref[...] + jnp.sum(oh, axis=0, keepdims=True)

    rec = jnp.zeros((TM, LANES), F32)
    for col, val in ((R_E, i1 - N_GROUPS), (R_E + 1, i2 - N_GROUPS), (R_RANK, rank1),
                     (R_RANK + 1, rank2), (R_W, w1), (R_W + 1, w2)):
        rec = jnp.where(lane == col, val, rec)
    route_ref[...] = rec
    route_t_ref[0] = rec.T[0:SUBLANES, :]

    x2 = jnp.concatenate([x2_a, x2_b], axis=1)
    x2_ref[...] = x2
    h = _rms(x2, gain_ref[...])
    h_hi, h_lo = _split_bf16(h)
    h2_ref[...] = h_hi
    hh_hl = _dot(h_hi, wrc_ref[...])
    lg_ref[...] = (hh_hl[:, :ROUTER_COLS] + hh_hl[:, ROUTER_COLS:]
                   + _dot(h_lo, wrc_ref[:, :ROUTER_COLS]) + br_ref[...])


def _out_router(mixed, x, w_out, layer, gain, wr_cat, br):
    T = x.shape[0]
    n = T // TM
    row = lambda w: pl.BlockSpec((TM, w), lambda i: (jnp.minimum(i, n - 1), 0))
    lag = lambda i: jnp.maximum(i - 1, 0)
    full = lambda shape: pl.BlockSpec(shape, lambda i: (0, 0))
    wo_spec = pl.BlockSpec((1, D_MODEL, D_MODEL), lambda i: (layer, 0, 0),
                           pipeline_mode=pl.Buffered(1))
    return pl.pallas_call(
        _out_router_kernel,
        grid=(n + 1,),
        in_specs=[pl.BlockSpec(memory_space=pl.ANY), pl.BlockSpec(memory_space=pl.ANY), wo_spec,
                  full((1, D_MODEL)), full((D_MODEL, 2 * ROUTER_COLS)), full((1, ROUTER_COLS))],
        out_specs=[row(D_MODEL), row(D_MODEL),
                   pl.BlockSpec((TM, LANES), lambda i: (lag(i), 0)),
                   pl.BlockSpec((1, SUBLANES, TM), lambda i: (lag(i), 0, 0)), full((8, LANES))],
        out_shape=[jax.ShapeDtypeStruct((T, D_MODEL), F32),
                   jax.ShapeDtypeStruct((T, D_MODEL), BF16),
                   jax.ShapeDtypeStruct((T, LANES), F32),
                   jax.ShapeDtypeStruct((T // TM, SUBLANES, TM), F32),
                   jax.ShapeDtypeStruct((8, LANES), F32)],
        scratch_shapes=[pltpu.VMEM((TM, TM), BF16), pltpu.VMEM((D_MODEL, D_MODEL), BF16),
                        pltpu.VMEM((TM, ROUTER_COLS), F32),
                        pltpu.VMEM((IN_SLOTS, TM, D_MODEL), BF16),
                        pltpu.VMEM((IN_SLOTS, TM, D_MODEL), F32),
                        pltpu.SemaphoreType.DMA((IN_SLOTS,))],
        compiler_params=pltpu.CompilerParams(
            dimension_semantics=("arbitrary",), vmem_limit_bytes=VMEM_LIMIT),
        name="out_router",
    )(mixed, x, w_out, gain, wr_cat, br)


def _dispatch_kernel(fill_ref, nu_ref, dest_ref, h_ref, xb_ref, zero_ref, sem_ref, zsem_ref,
                     stage_ref):
    i = pl.program_id(0)
    par = lax.rem(i, 2)
    last = i == pl.num_programs(0) - 1
    n_blocks = xb_ref.shape[0] // MOE_BLK
    spare_fills = [(j >= nu_ref[0], pltpu.make_async_copy(
        zero_ref, xb_ref.at[pl.ds(j * MOE_BLK, MOE_BLK)], zsem_ref.at[1]))
        for j in range(n_blocks - N_EXPERTS, n_blocks)]

    @pl.when(i == 0)
    def _():
        zero_ref[...] = jnp.zeros_like(zero_ref)
        fills = [(fill_ref[e] >= 0, pltpu.make_async_copy(
            zero_ref, xb_ref.at[pl.ds(pl.multiple_of(jnp.maximum(fill_ref[e], 0), MOE_BLK), MOE_BLK)],
            zsem_ref.at[0])) for e in range(N_EXPERTS)]
        for cond, f in fills + spare_fills:
            pl.when(cond)(f.start)
        for cond, f in fills:
            pl.when(cond)(f.wait)

    _row_tiles_store(stage_ref.at[par], h_ref[...].astype(F32))
    for r in range(TM):
        for k in range(TOP_K):
            pltpu.make_async_copy(stage_ref.at[par, r], xb_ref.at[dest_ref[0, 0, k * TM + r]],
                                  sem_ref.at[par]).start(priority=k)

    def wait_tile(p):
        for _ in range(TOP_K):
            pltpu.make_async_copy(stage_ref.at[p], xb_ref.at[pl.ds(0, TM)], sem_ref.at[p]).wait()

    pl.when(i > 0)(lambda: wait_tile(1 - par))
    @pl.when(last)
    def _():
        wait_tile(par)
        for cond, f in spare_fills:
            pl.when(cond)(f.wait)


def _dispatch(fill_start, n_used, dest_tiles, h2, n_rows):
    T = h2.shape[0]
    grid_spec = pltpu.PrefetchScalarGridSpec(
        num_scalar_prefetch=2,
        grid=(T // TM,),
        in_specs=[
            pl.BlockSpec((1, 1, TOP_K * TM), lambda i, fs, nu: (i, 0, 0), memory_space=pltpu.SMEM),
            pl.BlockSpec((TM, D_MODEL), lambda i, fs, nu: (i, 0)),
        ],
        out_specs=pl.BlockSpec(memory_space=pl.ANY),
        scratch_shapes=[pltpu.VMEM((MOE_BLK, ROW_TILES, LANES), F32),
                        pltpu.SemaphoreType.DMA((2,)), pltpu.SemaphoreType.DMA((2,)),
                        pltpu.VMEM((2, TM, ROW_TILES, LANES), F32)],
    )
    return pl.pallas_call(
        _dispatch_kernel,
        grid_spec=grid_spec,
        out_shape=jax.ShapeDtypeStruct((n_rows, ROW_TILES, LANES), F32),
        compiler_params=pltpu.CompilerParams(dimension_semantics=("arbitrary",)),
        name="dispatch",
    )(fill_start, n_used, dest_tiles, h2)


BLOCK_COPY_PARTS = 4
X_SLOTS = 4
Y_SLOTS = 3


class _CopyGroup:
    def __init__(self, copies):
        self.copies = copies

    def start(self):
        for n, c in enumerate(self.copies):
            c.start(priority=n % 2)

    def wait(self):
        for c in self.copies:
            c.wait()


def _expert_kernel(be_ref, nxt_ref, nu_ref, half_ref, xb_ref, wg_ref, wu_ref, wd_ref, yb_ref,
                   xbuf_ref, ybuf_ref, wgs_ref, wus_ref, wds_ref, wgb_ref, wub_ref, wdb_ref,
                   xsem_ref, ysem_ref, wsem_ref, zbuf_ref, zsem_ref, *, layer, n_blocks):
    n_used = nu_ref[0]

    part = MOE_BLK // BLOCK_COPY_PARTS

    def x_copy(j, slot):
        return _CopyGroup([pltpu.make_async_copy(
            xb_ref.at[pl.ds(j * MOE_BLK + p * part, part)],
            xbuf_ref.at[slot, pl.ds(p * part, part)], xsem_ref.at[slot])
            for p in range(BLOCK_COPY_PARTS)])

    def y_copy(j, slot):
        return _CopyGroup([pltpu.make_async_copy(
            ybuf_ref.at[slot, pl.ds(p * part, part)],
            yb_ref.at[pl.ds(j * MOE_BLK + p * part, part)], ysem_ref.at[slot])
            for p in range(BLOCK_COPY_PARTS)])

    def w_copies(e, ws):
        return [pltpu.make_async_copy(src.at[layer, e], dst.at[ws], wsem_ref.at[ws])
                for src, dst in ((wg_ref, wgs_ref), (wu_ref, wus_ref), (wd_ref, wds_ref))]

    def fill_copy(j):
        return pltpu.make_async_copy(zbuf_ref, yb_ref.at[pl.ds(j * MOE_BLK, MOE_BLK)], zsem_ref)

    for j0 in range(X_SLOTS - 1):
        x_copy(j0, j0).start()
    for c in w_copies(be_ref[0], 0):
        c.start()

    zbuf_ref[...] = jnp.zeros_like(zbuf_ref)

    def fill(j, carry):
        fill_copy(j).start()
        return carry

    lax.fori_loop(n_used, n_blocks, fill, 0)

    def block(j, ws):
        slot = lax.rem(j, Y_SLOTS)
        xslot = lax.rem(j, X_SLOTS)
        first = (j == 0) | (be_ref[j] != be_ref[jnp.maximum(j - 1, 0)])
        ws = jnp.where(first & (j > 0), 1 - ws, ws)

        @pl.when(first)
        def _():
            for c in w_copies(be_ref[j], ws):
                c.wait()
            for r0 in range(0, D_MODEL, W_PREP_ROWS):
                rows = slice(r0, r0 + W_PREP_ROWS)
                wgb_ref[rows, :] = wgs_ref[ws, rows, :].astype(BF16)
                wub_ref[rows, :] = wus_ref[ws, rows, :].astype(BF16)
            for r0 in range(0, D_EXPERT, W_PREP_ROWS):
                rows = slice(r0, r0 + W_PREP_ROWS)
                wdb_ref[rows, :] = wds_ref[ws, rows, :].astype(BF16)

            @pl.when(nxt_ref[j] >= 0)
            def _():
                for c in w_copies(nxt_ref[j], 1 - ws):
                    c.start()

        ahead = j + X_SLOTS - 1

        @pl.when(ahead < n_used)
        def _():
            x_copy(ahead, lax.rem(ahead, X_SLOTS)).start()

        x_copy(j, xslot).wait()

        @pl.when(j >= Y_SLOTS)
        def _():
            y_copy(j - Y_SLOTS, slot).wait()

        def mlp(rows):
            x = jnp.concatenate(
                [c.astype(BF16) for c in _row_tiles_chunks(xbuf_ref.at[xslot, pl.ds(0, rows)], rows)],
                axis=1)
            g = _dot(x, wgb_ref[...])
            u = _dot(x, wub_ref[...])
            h = (g * (1.0 / (1.0 + jnp.exp(-g)))) * u
            _row_tiles_store(ybuf_ref.at[slot, pl.ds(0, rows)], _dot(h.astype(BF16), wdb_ref[...]))

        half_full = half_ref[j] == 1

        @pl.when(half_full)
        def _():
            mlp(MOE_BLK // 2)
            ybuf_ref[slot, pl.ds(MOE_BLK // 2, MOE_BLK // 2)] = jnp.zeros(
                (MOE_BLK // 2, ROW_TILES, LANES), F32)

        @pl.when(jnp.logical_not(half_full))
        def _():
            mlp(MOE_BLK)

        y_copy(j, slot).start()
        return ws

    lax.fori_loop(0, n_used, block, jnp.int32(0))

    for back in range(Y_SLOTS, 0, -1):
        y_copy(n_used - back, lax.rem(n_used - back, Y_SLOTS)).wait()

    def fill_wait(j, carry):
        fill_copy(j).wait()
        return carry

    lax.fori_loop(n_used, n_blocks, fill_wait, 0)


def _experts(blk_exp, nxt_exp, n_used, blk_half, xb, w_gate, w_up, w_down, layer):
    n_blocks = blk_exp.shape[0]
    any_spec = pl.BlockSpec(memory_space=pl.ANY)
    blk = (MOE_BLK, ROW_TILES, LANES)
    grid_spec = pltpu.PrefetchScalarGridSpec(
        num_scalar_prefetch=4,
        grid=(1,),
        in_specs=[any_spec, any_spec, any_spec, any_spec],
        out_specs=any_spec,
        scratch_shapes=[
            pltpu.VMEM((X_SLOTS,) + blk, F32), pltpu.VMEM((Y_SLOTS,) + blk, F32),
            pltpu.VMEM((2, D_MODEL, D_EXPERT), F32), pltpu.VMEM((2, D_MODEL, D_EXPERT), F32),
            pltpu.VMEM((2, D_EXPERT, D_MODEL), F32),
            pltpu.VMEM((D_MODEL, D_EXPERT), BF16), pltpu.VMEM((D_MODEL, D_EXPERT), BF16),
            pltpu.VMEM((D_EXPERT, D_MODEL), BF16),
            pltpu.SemaphoreType.DMA((X_SLOTS,)), pltpu.SemaphoreType.DMA((Y_SLOTS,)),
            pltpu.SemaphoreType.DMA((2,)),
            pltpu.VMEM(blk, F32), pltpu.SemaphoreType.DMA(()),
        ],
    )
    return pl.pallas_call(
        functools.partial(_expert_kernel, layer=layer, n_blocks=n_blocks),
        grid_spec=grid_spec,
        out_shape=jax.ShapeDtypeStruct((n_blocks * MOE_BLK, ROW_TILES, LANES), F32),
        compiler_params=pltpu.CompilerParams(
            dimension_semantics=("arbitrary",), vmem_limit_bytes=VMEM_LIMIT),
        name="experts",
    )(blk_exp, nxt_exp, n_used, blk_half, xb, w_gate, w_up, w_down)


def _dispatch_tables(route_t, counts_rec, T):
    counts = counts_rec[0, N_GROUPS:N_GROUPS + N_EXPERTS].astype(jnp.int32)
    n_steps = (T * TOP_K) // MOE_BLK + N_EXPERTS
    nblk = (counts + MOE_BLK - 1) // MOE_BLK
    bend = jnp.cumsum(nblk)
    pstart = (bend - nblk) * MOE_BLK
    n_used = bend[-1]
    j = jnp.minimum(jnp.arange(n_steps, dtype=jnp.int32), n_used - 1)
    blk_exp = jnp.minimum(jnp.sum(j[:, None] >= bend[None, :], axis=1), N_EXPERTS - 1)
    n_rows = n_steps * MOE_BLK
    last_blk = jnp.where(counts > 0, (bend - 1) * MOE_BLK, -1)
    ids = jnp.arange(N_EXPERTS, dtype=jnp.int32)
    later = (ids[None, :] > ids[:, None]) & (nblk[None, :] > 0)
    nxt_of = jnp.min(jnp.where(later, ids[None, :], N_EXPERTS), axis=1)
    nxt_tab = jnp.where(nxt_of < N_EXPERTS, nxt_of, -1)
    own = blk_exp[:, None] == ids[None, :]
    nxt_exp = jnp.sum(jnp.where(own, nxt_tab[None, :], 0), axis=1)
    seg_end = jnp.sum(jnp.where(own, (pstart + counts)[None, :], 0), axis=1)
    blk_half = (seg_end - j * MOE_BLK <= MOE_BLK // 2).astype(jnp.int32)
    e = route_t[:, R_E:R_E + TOP_K, :].astype(jnp.int32)
    rank = route_t[:, R_RANK:R_RANK + TOP_K, :].astype(jnp.int32)
    seg = jnp.sum(jnp.where(e[..., None] == jnp.arange(N_EXPERTS), pstart, 0), axis=-1)
    dest = jnp.clip(seg + rank, 0, n_steps * MOE_BLK - 1)
    dest_tiles = dest.reshape(T // TM, 1, TOP_K * TM)
    return dict(dest_tiles=dest_tiles, fill_start=last_blk.astype(jnp.int32),
                blk_exp=blk_exp.astype(jnp.int32), nxt_exp=nxt_exp.astype(jnp.int32),
                blk_half=blk_half,
                n_used=n_used.reshape(1).astype(jnp.int32), n_rows=n_rows)


def _prep_layer(l, w_gk_up, b_gk, gla_norm, gmlp_norm, w_spatial, b_spatial, w_conv,
                w_router_group, b_router_group, w_router_expert, b_router_expert):
    wgk = jnp.concatenate(
        [w_gk_up[l], jnp.zeros((LANES - GLA_GATE_RANK, GLA_KDIM), F32)], axis=0).astype(BF16)
    wsp = w_spatial[l].transpose(1, 0, 2).reshape(GMLP_CHUNK, GMLP_HEADS * GMLP_CHUNK)
    bsp = jnp.repeat(b_spatial[l].T, GMLP_DH, axis=1)
    wconv = jnp.concatenate([w_conv[l], jnp.zeros((8 - CONV_K, CONV_WIDTH), F32)], axis=0)
    wr = jnp.concatenate(
        [w_router_group[l], w_router_expert[l],
         jnp.zeros((D_MODEL, ROUTER_COLS - N_GROUPS - N_EXPERTS), F32)], axis=1)
    wr_hi = wr.astype(BF16)
    wr_lo = (wr - wr_hi.astype(F32)).astype(BF16)
    br = jnp.concatenate(
        [b_router_group[l], b_router_expert[l],
         jnp.zeros((ROUTER_COLS - N_GROUPS - N_EXPERTS,), F32)])[None, :]
    return dict(
        wgk=wgk, bgk=b_gk[l][None, :], glan=gla_norm[l][None, :], gmn=gmlp_norm[l][None, :],
        wsp=wsp, bsp=bsp, wconv=wconv, wr_cat=jnp.concatenate([wr_hi, wr_lo], axis=1), br=br)


def kernel(x, attn_norm, w_in, w_gk_up, b_gk, gla_norm, gmlp_norm, w_spatial, b_spatial, w_conv, w_out, ffn_norm, w_router_group, b_router_group, w_router_expert, b_router_expert, w_gate, w_up, w_down, final_norm):
    B, S, D = x.shape
    T = B * S
    depth = w_in.shape[0]
    xr = x.reshape(T, D)
    w_in_t = jnp.swapaxes(w_in, 1, 2)
    moe = None
    for l in range(depth):
        p = _prep_layer(l, w_gk_up, b_gk, gla_norm, gmlp_norm, w_spatial, b_spatial, w_conv,
                        w_router_group, b_router_group, w_router_expert, b_router_expert)
        mix_params = (p["wgk"], p["bgk"], p["glan"], p["gmn"], p["wsp"], p["bsp"], p["wconv"])
        if moe is None:
            mixed = _front(xr, attn_norm[l][None, :], w_in_t, l, mix_params, B, S)
        else:
            xr, mixed = _front(moe["x2"], attn_norm[l][None, :], w_in_t, l, mix_params, B, S, moe)
        x2, h2, route, route_t, counts_rec = _out_router(
            mixed, xr, w_out, l, ffn_norm[l][None, :], p["wr_cat"], p["br"])
        moe = _dispatch_tables(route_t, counts_rec, T)
        xb = _dispatch(moe["fill_start"], moe["n_used"], moe["dest_tiles"], h2, moe["n_rows"])
        yb = _experts(moe["blk_exp"], moe["nxt_exp"], moe["n_used"], moe["blk_half"], xb,
                      w_gate, w_up, w_down, l)
        moe.update(x2=x2, route=route, yb=yb)
    out = _combine_final_norm(moe["dest_tiles"], moe["x2"], moe["route"], moe["yb"],
                              final_norm[None, :])
    return out.reshape(B, S, D)
```
